```python
import jax, jax.numpy as jnp
from jax import lax
import numpy as np


D_MODEL = 1024
BATCH = 8
SEQ = 2048
DEPTH = 1

GLA_HEADS = 4
GLA_DK = 64
GLA_DV = 128
GLA_WIDTH = GLA_HEADS * GLA_DV
GLA_GATE_RANK = 16
GLA_GATE_NORM = 16.0
GDN_HEADS = 4
GDN_DK = 128
GDN_DV = 128
GDN_WIDTH = GDN_HEADS * GDN_DV
CONV_W = 4
CHUNK = 64
N_GROUPS = 8
EXPERTS_PER_GROUP = 8
N_EXPERTS = N_GROUPS * EXPERTS_PER_GROUP
TOP_K = 2
D_EXPERT = 512
DISPATCH_BLOCK = 128
LN_EPS = 1e-5
RMS_EPS = 1e-6
ALPHA_DN = (2.0 * DEPTH) ** 0.25
BETA_DN = (8.0 * DEPTH) ** -0.25

_SEG_SIZES = (GLA_HEADS * GLA_DK, GLA_HEADS * GLA_DK, GLA_WIDTH, GLA_WIDTH, GLA_GATE_RANK,
              GDN_HEADS * GDN_DK, GDN_HEADS * GDN_DK, GDN_WIDTH, GDN_WIDTH, GDN_HEADS, GDN_HEADS)
IN_PROJ_DIM = sum(_SEG_SIZES)
CONV_CH = 2 * GDN_HEADS * GDN_DK + GDN_WIDTH

kernel_name = 'hybrid_gla_gdn_hmoe_deepnorm'


def _layer_norm(x, g, b):
    xf = x.astype(jnp.float32)
    mu = jnp.mean(xf, axis=-1, keepdims=True)
    var = jnp.mean(jnp.square(xf - mu), axis=-1, keepdims=True)
    return ((xf - mu) * lax.rsqrt(var + LN_EPS) * g.astype(jnp.float32) + b.astype(jnp.float32)).astype(x.dtype)


def _rms_norm(x, g):
    return x * lax.rsqrt(jnp.mean(jnp.square(x), axis=-1, keepdims=True) + RMS_EPS) * g.astype(jnp.float32)


def _l2_norm(x):
    return x * lax.rsqrt(jnp.sum(jnp.square(x), axis=-1, keepdims=True) + RMS_EPS)


def _to_chunks(t, heads):
    b, s, w = t.shape
    return t.reshape(b, s // CHUNK, CHUNK, heads, w // heads).transpose(1, 0, 3, 2, 4)


def _to_chunks_scalar(t):
    b, s, h = t.shape
    return t.reshape(b, s // CHUNK, CHUNK, h).transpose(1, 0, 3, 2)


def _from_chunks(t):
    nc, b, h, c, d = t.shape
    return t.transpose(1, 0, 3, 2, 4).reshape(b, nc * c, h, d)


def _causal_conv(x, w):
    c = x.shape[-1]
    return lax.conv_general_dilated(x, w[:, None, :], window_strides=(1,), padding=[(CONV_W - 1, 0)],
                                    dimension_numbers=('NWC', 'WIO', 'NWC'), feature_group_count=c)


def _gla_chunked(q, k, v, g):
    nc, bsz, nh, c, dk = q.shape
    dv = v.shape[-1]
    causal = jnp.tril(jnp.ones((c, c), dtype=bool))

    def step(state, inp):
        qc, kc, vc, gc = inp
        b = jnp.cumsum(gc, axis=-2)
        rel = jnp.exp(jnp.where(causal[:, :, None], b[..., :, None, :] - b[..., None, :, :], -jnp.inf))
        attn = jnp.einsum('bhid,bhjd,bhijd->bhij', qc, kc, rel)
        o = jnp.einsum('bhid,bhde->bhie', qc * jnp.exp(b), state) + jnp.einsum('bhij,bhje->bhie', attn, vc)
        b_last = b[..., -1:, :]
        state = jnp.exp(b_last[..., 0, :])[..., None] * state + jnp.einsum('bhjd,bhje->bhde', kc * jnp.exp(b_last - b), vc)
        return state, o

    s0 = jnp.zeros((bsz, nh, dk, dv), q.dtype)
    _, o = lax.scan(step, s0, (q, k, v, g))
    return o


def _gdn_chunked(q, k, v, beta, g):
    c = q.shape[-2]
    dv = v.shape[-1]
    causal = jnp.tril(jnp.ones((c, c), dtype=bool))
    strict = jnp.tril(jnp.ones((c, c), dtype=bool), -1)
    eye = jnp.eye(c, dtype=q.dtype)
    gam = jnp.cumsum(g, axis=-1)
    decay = jnp.exp(jnp.where(causal, gam[..., :, None] - gam[..., None, :], -jnp.inf))
    a_mat = jnp.where(strict, beta[..., :, None] * jnp.einsum('nbhid,nbhjd->nbhij', k, k) * decay, 0.0)
    rhs = jnp.concatenate([beta[..., None] * v, beta[..., None] * k * jnp.exp(gam)[..., None]], axis=-1)
    uw = lax.linalg.triangular_solve(eye + a_mat, rhs, left_side=True, lower=True, unit_diagonal=True)
    u, w = uw[..., :dv], uw[..., dv:]
    attn = jnp.einsum('nbhid,nbhjd->nbhij', q, k) * decay
    q_dec = q * jnp.exp(gam)[..., None]
    gam_last = gam[..., -1:]
    k_dec = k * jnp.exp(gam_last - gam)[..., None]
    chunk_decay = jnp.exp(gam_last[..., 0])

    def step(state, inp):
        qd, at, uc, wc, kd, dl = inp
        v_new = uc - jnp.einsum('bhid,bhde->bhie', wc, state)
        o = jnp.einsum('bhid,bhde->bhie', qd, state) + jnp.einsum('bhij,bhje->bhie', at, v_new)
        state = dl[..., None, None] * state + jnp.einsum('bhjd,bhje->bhde', kd, v_new)
        return state, o

    nc, bsz, nh, _, dk = q.shape
    s0 = jnp.zeros((bsz, nh, dk, dv), q.dtype)
    _, o = lax.scan(step, s0, (q_dec, attn, u, w, k_dec, chunk_decay))
    return o


def _mixer(h, w_in, w_gk_up, b_gk, conv_w, a_log, dt_bias, gla_norm_g, gdn_norm_g, w_out):
    f32 = jnp.float32
    bsz, seq, _ = h.shape
    proj = (h @ w_in).astype(f32)
    qa, ka, va, ra, lra, qb, kb, vb, zb, bb, ab = jnp.split(proj, np.cumsum(_SEG_SIZES)[:-1].tolist(), axis=-1)
    gk = jax.nn.log_sigmoid(lra @ w_gk_up.astype(f32) + b_gk.astype(f32)) / GLA_GATE_NORM
    o_a = _gla_chunked(_to_chunks(qa * GLA_DK ** -0.5, GLA_HEADS), _to_chunks(ka, GLA_HEADS),
                       _to_chunks(va, GLA_HEADS), _to_chunks(gk, GLA_HEADS))
    o_a = _rms_norm(_from_chunks(o_a), gla_norm_g) * jax.nn.silu(ra.reshape(bsz, seq, GLA_HEADS, GLA_DV))
    qkv = jax.nn.silu(_causal_conv(jnp.concatenate([qb, kb, vb], axis=-1), conv_w.astype(f32)))
    qb, kb, vb = jnp.split(qkv, [GDN_HEADS * GDN_DK, 2 * GDN_HEADS * GDN_DK], axis=-1)
    qb = _l2_norm(_to_chunks(qb, GDN_HEADS)) * GDN_DK ** -0.5
    kb = _l2_norm(_to_chunks(kb, GDN_HEADS))
    beta = _to_chunks_scalar(jax.nn.sigmoid(bb))
    g_b = _to_chunks_scalar(-jnp.exp(a_log.astype(f32)) * jax.nn.softplus(ab + dt_bias.astype(f32)))
    o_b = _gdn_chunked(qb, kb, _to_chunks(vb, GDN_HEADS), beta, g_b)
    o_b = _rms_norm(_from_chunks(o_b), gdn_norm_g) * jax.nn.silu(zb.reshape(bsz, seq, GDN_HEADS, GDN_DV))
    mixed = jnp.concatenate([o_a.reshape(bsz, seq, GLA_WIDTH), o_b.reshape(bsz, seq, GDN_WIDTH)], axis=-1)
    return (mixed @ w_out.astype(f32)).astype(h.dtype)


def _moe(h, w_router_group, b_router_group, w_router_expert, b_router_expert, w_gate, w_up, w_down):
    f32 = jnp.float32
    bsz, seq, d = h.shape
    n = bsz * seq
    xt = h.reshape(n, d)
    xf = xt.astype(f32)
    group_prob = jax.nn.softmax(xf @ w_router_group.astype(f32) + b_router_group.astype(f32), axis=-1)
    p_group, g_sel = lax.top_k(group_prob, 1)
    g_sel = g_sel[:, 0]
    e_logits = (xf @ w_router_expert.astype(f32) + b_router_expert.astype(f32)).reshape(n, N_GROUPS, EXPERTS_PER_GROUP)
    e_logits = e_logits[jnp.arange(n), g_sel]
    top_logit, top_e = lax.top_k(e_logits, TOP_K)
    gate = p_group * jax.nn.softmax(top_logit, axis=-1)
    expert_id = (g_sel[:, None] * EXPERTS_PER_GROUP + top_e).reshape(-1)
    n_assign = n * TOP_K
    tok = jnp.arange(n_assign, dtype=jnp.int32) // TOP_K
    order = jnp.argsort(expert_id)
    e_sorted = expert_id[order]
    counts = jnp.bincount(expert_id, length=N_EXPERTS)
    starts = jnp.cumsum(counts) - counts
    padded = (counts + DISPATCH_BLOCK - 1) // DISPATCH_BLOCK * DISPATCH_BLOCK
    pends = jnp.cumsum(padded)
    pstarts = pends - padded
    dest_sorted = pstarts[e_sorted] + jnp.arange(n_assign, dtype=jnp.int32) - starts[e_sorted]
    dest = jnp.zeros((n_assign,), jnp.int32).at[order].set(dest_sorted.astype(jnp.int32))
    n_blocks = -(-n_assign // DISPATCH_BLOCK) + N_EXPERTS
    buf_tok = jnp.full((n_blocks * DISPATCH_BLOCK,), n, dtype=jnp.int32).at[dest].set(tok)
    x_pad = jnp.concatenate([xt, jnp.zeros((1, d), xt.dtype)], axis=0)
    xbuf = x_pad[buf_tok].reshape(n_blocks, DISPATCH_BLOCK, d)
    blk_expert = jnp.minimum(jnp.searchsorted(pends, jnp.arange(n_blocks) * DISPATCH_BLOCK, side='right'), N_EXPERTS - 1)

    def expert_block(blk):
        xb, e = blk
        hid = jax.nn.silu(xb @ w_gate[e]) * (xb @ w_up[e])
        return hid @ w_down[e]

    ybuf = lax.map(expert_block, (xbuf, blk_expert)).reshape(n_blocks * DISPATCH_BLOCK, d)
    y_assign = ybuf[dest].reshape(n, TOP_K, d)
    y = jnp.einsum('nkd,nk->nd', y_assign, gate.astype(y_assign.dtype))
    return y.reshape(bsz, seq, d).astype(h.dtype)


def setup_inputs(seed: int = 0) -> dict:
    key = jax.random.key(seed)
    ks = jax.random.split(key, 24)
    f32 = jnp.float32
    L = DEPTH

    def nrm(k, shape, scale):
        return jax.random.normal(k, shape, f32) * scale

    x = nrm(ks[0], (BATCH, SEQ, D_MODEL), 1.0)
    w_in = nrm(ks[1], (L, D_MODEL, IN_PROJ_DIM), D_MODEL ** -0.5)
    w_gk_up = nrm(ks[2], (L, GLA_GATE_RANK, GLA_HEADS * GLA_DK), GLA_GATE_RANK ** -0.5)
    b_gk = nrm(ks[3], (L, GLA_HEADS * GLA_DK), 0.1)
    conv_w = nrm(ks[4], (L, CONV_W, CONV_CH), CONV_W ** -0.5)
    a_log = jnp.log(jax.random.uniform(ks[5], (L, GDN_HEADS), f32, minval=1.0, maxval=16.0))
    dt = jnp.exp(jax.random.uniform(ks[6], (L, GDN_HEADS), f32, minval=float(np.log(1e-3)), maxval=float(np.log(1e-1))))
    dt_bias = dt + jnp.log(-jnp.expm1(-dt))
    gla_norm_g = 1.0 + nrm(ks[7], (L, GLA_DV), 0.02)
    gdn_norm_g = 1.0 + nrm(ks[8], (L, GDN_DV), 0.02)
    w_out = nrm(ks[9], (L, GLA_WIDTH + GDN_WIDTH, D_MODEL), (GLA_WIDTH + GDN_WIDTH) ** -0.5 * BETA_DN)
    ln1_g = 1.0 + nrm(ks[10], (L, D_MODEL), 0.02)
    ln1_b = nrm(ks[11], (L, D_MODEL), 0.02)
    w_router_group = nrm(ks[12], (L, D_MODEL, N_GROUPS), D_MODEL ** -0.5)
    b_router_group = nrm(ks[13], (L, N_GROUPS), 0.01)
    w_router_expert = nrm(ks[14], (L, D_MODEL, N_EXPERTS), D_MODEL ** -0.5)
    b_router_expert = nrm(ks[15], (L, N_EXPERTS), 0.01)
    w_gate = nrm(ks[16], (L, N_EXPERTS, D_MODEL, D_EXPERT), D_MODEL ** -0.5)
    w_up = nrm(ks[17], (L, N_EXPERTS, D_MODEL, D_EXPERT), D_MODEL ** -0.5)
    w_down = nrm(ks[18], (L, N_EXPERTS, D_EXPERT, D_MODEL), D_EXPERT ** -0.5 * BETA_DN)
    ln2_g = 1.0 + nrm(ks[19], (L, D_MODEL), 0.02)
    ln2_b = nrm(ks[20], (L, D_MODEL), 0.02)
    return {'x': x, 'w_in': w_in, 'w_gk_up': w_gk_up, 'b_gk': b_gk, 'conv_w': conv_w,
            'a_log': a_log, 'dt_bias': dt_bias, 'gla_norm_g': gla_norm_g, 'gdn_norm_g': gdn_norm_g,
            'w_out': w_out, 'ln1_g': ln1_g, 'ln1_b': ln1_b,
            'w_router_group': w_router_group, 'b_router_group': b_router_group,
            'w_router_expert': w_router_expert, 'b_router_expert': b_router_expert,
            'w_gate': w_gate, 'w_up': w_up, 'w_down': w_down, 'ln2_g': ln2_g, 'ln2_b': ln2_b}


def reference(x, w_in, w_gk_up, b_gk, conv_w, a_log, dt_bias, gla_norm_g, gdn_norm_g, w_out,
              ln1_g, ln1_b, w_router_group, b_router_group, w_router_expert, b_router_expert,
              w_gate, w_up, w_down, ln2_g, ln2_b):
    for l in range(DEPTH):
        y = _mixer(x, w_in[l], w_gk_up[l], b_gk[l], conv_w[l], a_log[l], dt_bias[l],
                   gla_norm_g[l], gdn_norm_g[l], w_out[l])
        x = _layer_norm(ALPHA_DN * x + y, ln1_g[l], ln1_b[l])
        y = _moe(x, w_router_group[l], b_router_group[l], w_router_expert[l], b_router_expert[l],
                 w_gate[l], w_up[l], w_down[l])
        x = _layer_norm(ALPHA_DN * x + y, ln2_g[l], ln2_b[l])
    return x
```

```python
import functools

import jax
import jax.numpy as jnp
import numpy as np
from jax import lax
from jax.experimental import pallas as pl
from jax.experimental.pallas import tpu as pltpu

F32 = jnp.float32
BF16 = jnp.bfloat16
HIGHEST = lax.Precision.HIGHEST

D_MODEL = 1024
DEPTH = 1
GLA_HEADS = 4
GLA_DK = 64
GLA_DV = 128
GLA_WIDTH = GLA_HEADS * GLA_DV
GLA_GATE_RANK = 16
GLA_GATE_NORM = 16.0
GDN_HEADS = 4
GDN_DK = 128
GDN_DV = 128
GDN_WIDTH = GDN_HEADS * GDN_DV
CONV_W = 4
CHUNK = 64
N_GROUPS = 8
EXPERTS_PER_GROUP = 8
N_EXPERTS = N_GROUPS * EXPERTS_PER_GROUP
TOP_K = 2
D_EXPERT = 512
LN_EPS = 1e-5
RMS_EPS = 1e-6
ALPHA_DN = (2.0 * DEPTH) ** 0.25

MAIN_W = 2 * GLA_HEADS * GLA_DK + 2 * GLA_WIDTH + 4 * GDN_WIDTH
AUX_W = 128
AUX_LRA = 0
AUX_BB = GLA_GATE_RANK
AUX_AB = GLA_GATE_RANK + GDN_HEADS
SUB = 16
VMEM_LIMIT = 56 * 1024 * 1024


def _cparams(*sem):
    return pltpu.CompilerParams(dimension_semantics=sem, vmem_limit_bytes=VMEM_LIMIT)


def _sigmoid(x):
    return 1.0 / (1.0 + jnp.exp(-x))


def _silu(x):
    return x * _sigmoid(x)


def _log_sigmoid(x):
    return jnp.minimum(x, 0.0) - jnp.log(1.0 + jnp.exp(-jnp.abs(x)))


def _softplus(x):
    return jnp.maximum(x, 0.0) + jnp.log(1.0 + jnp.exp(-jnp.abs(x)))


def _in_proj_kernel(x_ref, wm_ref, wa_ref, main_ref, aux_ref, *, col_block):
    xb = x_ref[...].astype(BF16)
    for j in range(MAIN_W // col_block):
        sl = slice(j * col_block, (j + 1) * col_block)
        main_ref[:, sl] = jnp.dot(xb, wm_ref[:, sl], preferred_element_type=F32).astype(BF16)
    aux_ref[...] = jnp.dot(xb, wa_ref[...], preferred_element_type=F32)


def _in_proj(x2d, w_main, w_aux, *, tm=512, col_block=512):
    n = x2d.shape[0]
    return pl.pallas_call(
        functools.partial(_in_proj_kernel, col_block=col_block),
        grid=(n // tm,),
        in_specs=[pl.BlockSpec((tm, D_MODEL), lambda i: (i, 0)),
                  pl.BlockSpec((D_MODEL, MAIN_W), lambda i: (0, 0)),
                  pl.BlockSpec((D_MODEL, AUX_W), lambda i: (0, 0))],
        out_specs=[pl.BlockSpec((tm, MAIN_W), lambda i: (i, 0)),
                   pl.BlockSpec((tm, AUX_W), lambda i: (i, 0))],
        out_shape=[jax.ShapeDtypeStruct((n, MAIN_W), BF16),
                   jax.ShapeDtypeStruct((n, AUX_W), F32)],
        compiler_params=_cparams("parallel"),
        name="in_proj",
    )(x2d, w_main, w_aux)


def _split_w_in(w_in):
    sizes = (GLA_HEADS * GLA_DK, GLA_HEADS * GLA_DK, GLA_WIDTH, GLA_WIDTH, GLA_GATE_RANK,
             GDN_HEADS * GDN_DK, GDN_HEADS * GDN_DK, GDN_WIDTH, GDN_WIDTH, GDN_HEADS, GDN_HEADS)
    offs = np.cumsum((0,) + sizes)
    seg = [w_in[:, offs[i]:offs[i + 1]] for i in range(len(sizes))]
    qa, ka, va, ra, lra, qb, kb, vb, zb, bb, ab = seg
    w_main = jnp.concatenate([qa, ka, va, ra, qb, kb, vb, zb], axis=1).astype(BF16)
    pad = jnp.zeros((w_in.shape[0], AUX_W - GLA_GATE_RANK - 2 * GDN_HEADS), w_in.dtype)
    w_aux = jnp.concatenate([lra, bb, ab, pad], axis=1).astype(BF16)
    return w_main, w_aux


_NT = (((1,), (1,)), ((), ()))
_TN = (((0,), (0,)), ((), ()))


def _gla_kernel(q_ref, k_ref, v_ref, r_ref, aux_ref, wgk_ref, bgk_ref, g_ref, o_ref, st_ref, *, ts):
    c = CHUNK
    nsub = c // SUB

    @pl.when(pl.program_id(1) == 0)
    def _():
        st_ref[...] = jnp.zeros_like(st_ref)

    pre = jnp.dot(aux_ref[...], wgk_ref[...], precision=HIGHEST, preferred_element_type=F32) + bgk_ref[...]
    gk_all = _log_sigmoid(pre) * (1.0 / GLA_GATE_NORM)

    ri = lax.broadcasted_iota(jnp.int32, (c, c), 0)
    ci = lax.broadcasted_iota(jnp.int32, (c, c), 1)
    tri = (ci <= ri).astype(F32)
    row128 = lax.broadcasted_iota(jnp.int32, (c, 128), 0)
    lane128 = lax.broadcasted_iota(jnp.int32, (1, 128), 1)
    lane_masks = [(lane128 // GLA_DK) == hh for hh in range(2)]
    sub_valid = [row128 < SUB * (i + 1) for i in range(nsub)]
    ar = lax.broadcasted_iota(jnp.int32, (c, nsub * c), 0)
    ac = lax.broadcasted_iota(jnp.int32, (c, nsub * c), 1)
    amask = ((ac // c) == (ar // SUB)) & ((ac % c) <= ar)
    scale = GLA_DK ** -0.5
    gnorm = g_ref[...]

    for ch in range(ts // c):
        rows = slice(ch * c, (ch + 1) * c)
        b_all = jnp.dot(tri, gk_all[rows], precision=HIGHEST, preferred_element_type=F32)
        for p in range(2):
            lanes = slice(128 * p, 128 * (p + 1))
            bp = b_all[:, lanes]
            qp = q_ref[rows, lanes].astype(F32) * scale
            kp = k_ref[rows, lanes].astype(F32)
            c_row = jnp.concatenate(
                [jnp.broadcast_to(bp[SUB * i:SUB * i + 1, :], (SUB, 128)) for i in range(nsub)], axis=0)
            qt = (qp * jnp.exp(bp - c_row)).astype(BF16)
            qd = qp * jnp.exp(bp)
            b_last = bp[c - 1:c, :]
            kd = kp * jnp.exp(b_last - bp)
            kts = []
            for i in range(nsub):
                e = jnp.exp(jnp.where(sub_valid[i], bp[SUB * i:SUB * i + 1, :] - bp, 0.0))
                kts.append(jnp.where(sub_valid[i], kp * e, 0.0))
            kstack = jnp.concatenate(kts, axis=0)
            st = st_ref[p]
            st_b = st.astype(BF16)
            upd = st * jnp.exp(b_last)
            for hh in range(2):
                h = 2 * p + hh
                lm = lane_masks[hh]
                vh = v_ref[rows, 128 * h:128 * (h + 1)]
                ks = jnp.where(lm, kstack, 0.0).astype(BF16)
                r_mat = lax.dot_general(qt, ks, _NT, preferred_element_type=F32)
                a_wide = jnp.where(amask, r_mat, 0.0).astype(BF16)
                v4 = jnp.concatenate([vh] * nsub, axis=0)
                o = jnp.dot(a_wide, v4, preferred_element_type=F32)
                o = o + lax.dot_general(jnp.where(lm, qd, 0.0).astype(BF16), st_b, _NT,
                                        preferred_element_type=F32)
                upd = upd + lax.dot_general(vh, jnp.where(lm, kd, 0.0).astype(BF16), _TN,
                                            preferred_element_type=F32)
                o = o * lax.rsqrt(jnp.mean(o * o, axis=-1, keepdims=True) + RMS_EPS) * gnorm
                gate = _silu(r_ref[rows, 128 * h:128 * (h + 1)].astype(F32))
                o_ref[rows, 128 * h:128 * (h + 1)] = (o * gate).astype(BF16)
            st_ref[p] = upd


def _gla(main, aux, wgk_pad, b_gk, gla_norm_g, *, bsz, seq, ts=256):
    nt = seq // ts
    row = lambda b, t: b * nt + t
    return pl.pallas_call(
        functools.partial(_gla_kernel, ts=ts),
        grid=(bsz, nt),
        in_specs=[pl.BlockSpec((ts, 256), lambda b, t: (row(b, t), 0)),
                  pl.BlockSpec((ts, 256), lambda b, t: (row(b, t), 1)),
                  pl.BlockSpec((ts, 512), lambda b, t: (row(b, t), 1)),
                  pl.BlockSpec((ts, 512), lambda b, t: (row(b, t), 2)),
                  pl.BlockSpec((ts, AUX_W), lambda b, t: (row(b, t), 0)),
                  pl.BlockSpec((AUX_W, 256), lambda b, t: (0, 0)),
                  pl.BlockSpec((1, 256), lambda b, t: (0, 0)),
                  pl.BlockSpec((1, 128), lambda b, t: (0, 0))],
        out_specs=pl.BlockSpec((ts, GLA_WIDTH), lambda b, t: (row(b, t), 0)),
        out_shape=jax.ShapeDtypeStruct((bsz * seq, GLA_WIDTH), BF16),
        scratch_shapes=[pltpu.VMEM((2, 128, 128), F32)],
        compiler_params=_cparams("parallel", "arbitrary"),
        name="gla",
    )(main, main, main, main, aux, wgk_pad, b_gk, gla_norm_g)


CONV_PAD = 8


def _mm(a, b):
    return jnp.dot(a.astype(BF16), b.astype(BF16), preferred_element_type=F32)


def _gdn_kernel(q_ref, k_ref, v_ref, z_ref, aux_ref, cw_ref, alog_ref, dtb_ref, g_ref, o_ref,
                st_ref, xe_ref, *, ts):
    c = CHUNK
    nch = ts // c
    hw = GDN_HEADS * GDN_DK

    @pl.when(pl.program_id(1) == 0)
    def _():
        st_ref[...] = jnp.zeros_like(st_ref)
        xe_ref[0:CONV_PAD, :] = jnp.zeros((CONV_PAD, 3 * hw), F32)

    xe_ref[CONV_PAD:, 0:hw] = q_ref[...].astype(F32)
    xe_ref[CONV_PAD:, hw:2 * hw] = k_ref[...].astype(F32)
    xe_ref[CONV_PAD:, 2 * hw:3 * hw] = v_ref[...].astype(F32)

    aux = aux_ref[...]
    beta_full = _sigmoid(aux)
    g_full = -jnp.exp(alog_ref[...]) * _softplus(aux + dtb_ref[...])
    ri = lax.broadcasted_iota(jnp.int32, (ts, ts), 0)
    ci = lax.broadcasted_iota(jnp.int32, (ts, ts), 1)
    same64 = (ri // c) == (ci // c)
    same32 = (ri // 32) == (ci // 32)
    same16 = (ri // 16) == (ci // 16)
    causal = same64 & (ci <= ri)
    strict = same64 & (ci < ri)
    lvl0 = same16 & (ci < ri)
    lvl1 = same32 & jnp.logical_not(same16) & (ci < ri)
    lvl2 = same64 & jnp.logical_not(same32) & (ci < ri)
    gam_full = jnp.dot(causal.astype(F32), g_full, precision=HIGHEST, preferred_element_type=F32)
    gam_t = gam_full.T
    gnorm = g_ref[...]

    def conv(col):
        acc = jnp.zeros((ts, 128), F32)
        for i in range(CONV_W):
            lo = CONV_PAD - (CONV_W - 1) + i
            acc = acc + xe_ref[lo:lo + ts, col:col + 128] * cw_ref[i:i + 1, col:col + 128]
        return _silu(acc)

    for h in range(GDN_HEADS):
        qh = conv(128 * h)
        kh = conv(hw + 128 * h)
        vh = conv(2 * hw + 128 * h)
        qn = qh * lax.rsqrt(jnp.sum(qh * qh, axis=-1, keepdims=True) + RMS_EPS) * (GDN_DK ** -0.5)
        kn = kh * lax.rsqrt(jnp.sum(kh * kh, axis=-1, keepdims=True) + RMS_EPS)
        beta = beta_full[:, AUX_BB + h:AUX_BB + h + 1]
        gam = gam_full[:, AUX_AB + h:AUX_AB + h + 1]
        gam_r = gam_t[AUX_AB + h:AUX_AB + h + 1, :]
        egam = jnp.exp(gam)
        kb = kn.astype(BF16)
        kk = lax.dot_general(kb, kb, _NT, preferred_element_type=F32)
        qk = lax.dot_general(qn.astype(BF16), kb, _NT, preferred_element_type=F32)
        dec = jnp.exp(jnp.where(causal, gam - gam_r, 0.0))
        a = beta * kk * dec
        attn = jnp.where(causal, qk * dec, 0.0).astype(BF16)

        x1 = jnp.where(lvl0, -a, 0.0)
        x2 = _mm(x1, x1)
        p = x1 + x2 + _mm(x1, x2)
        x4 = _mm(x2, x2)
        p = p + x4 + _mm(p, x4)
        x8 = _mm(x4, x4)
        p = p + x8 + _mm(p, x8)
        for lvl in (lvl1, lvl2):
            low = jnp.where(lvl, a, 0.0)
            zz = low + _mm(p, low)
            p = p - zz - _mm(zz, p)

        rhs = jnp.concatenate([beta * vh, beta * egam * kn], axis=1)
        uw = rhs + _mm(p, rhs)
        u = uw[:, :GDN_DV]
        w = uw[:, GDN_DV:].astype(BF16)
        qd = (qn * egam).astype(BF16)
        gl_rows = jnp.concatenate(
            [jnp.broadcast_to(gam[c * (j + 1) - 1:c * (j + 1), :], (c, 1)) for j in range(nch)], axis=0)
        kd = (kn * jnp.exp(gl_rows - gam)).astype(BF16)
        s = st_ref[h]
        vnews = []
        for j in range(nch):
            rows = slice(c * j, c * (j + 1))
            sb = s.astype(BF16)
            vnew = u[rows] - jnp.dot(w[rows], sb, preferred_element_type=F32)
            vn16 = vnew.astype(BF16)
            vnews.append(vn16)
            vfull = jnp.concatenate(vnews + [jnp.zeros((c, GDN_DV), BF16)] * (nch - 1 - j), axis=0)
            o = jnp.dot(qd[rows], sb, preferred_element_type=F32)
            o = o + jnp.dot(attn[rows], vfull, preferred_element_type=F32)
            s = jnp.exp(gam[c * (j + 1) - 1:c * (j + 1), :]) * s + lax.dot_general(
                kd[rows], vn16, _TN, preferred_element_type=F32)
            o = o * lax.rsqrt(jnp.mean(o * o, axis=-1, keepdims=True) + RMS_EPS) * gnorm
            gate = _silu(z_ref[rows, 128 * h:128 * (h + 1)].astype(F32))
            o_ref[rows, 128 * h:128 * (h + 1)] = (o * gate).astype(BF16)
        st_ref[h] = s

    xe_ref[0:CONV_PAD, :] = xe_ref[ts:ts + CONV_PAD, :]


def _gdn(main, aux, conv_w, alog_pad, dtb_pad, gdn_norm_g, *, bsz, seq, ts=256):
    nt = seq // ts
    row = lambda b, t: b * nt + t
    hw = GDN_HEADS * GDN_DK
    return pl.pallas_call(
        functools.partial(_gdn_kernel, ts=ts),
        grid=(bsz, nt),
        in_specs=[pl.BlockSpec((ts, hw), lambda b, t: (row(b, t), 3)),
                  pl.BlockSpec((ts, hw), lambda b, t: (row(b, t), 4)),
                  pl.BlockSpec((ts, hw), lambda b, t: (row(b, t), 5)),
                  pl.BlockSpec((ts, hw), lambda b, t: (row(b, t), 6)),
                  pl.BlockSpec((ts, AUX_W), lambda b, t: (row(b, t), 0)),
                  pl.BlockSpec((CONV_W, 3 * hw), lambda b, t: (0, 0)),
                  pl.BlockSpec((1, AUX_W), lambda b, t: (0, 0)),
                  pl.BlockSpec((1, AUX_W), lambda b, t: (0, 0)),
                  pl.BlockSpec((1, GDN_DV), lambda b, t: (0, 0))],
        out_specs=pl.BlockSpec((ts, GDN_WIDTH), lambda b, t: (row(b, t), 0)),
        out_shape=jax.ShapeDtypeStruct((bsz * seq, GDN_WIDTH), BF16),
        scratch_shapes=[pltpu.VMEM((GDN_HEADS, GDN_DK, GDN_DV), F32),
                        pltpu.VMEM((ts + CONV_PAD, 3 * hw), F32)],
        compiler_params=_cparams("parallel", "arbitrary"),
        name="gdn",
    )(main, main, main, main, aux, conv_w, alog_pad, dtb_pad, gdn_norm_g)


def _pad_aux_row(v, offset):
    return jnp.zeros((1, AUX_W), F32).at[0, offset:offset + v.shape[0]].set(v.astype(F32))


ROUTE_W = 128
R_E1, R_E2, R_RANK1, R_RANK2, R_G1, R_G2 = range(6)
ROUTER_GROUP_COL = N_EXPERTS


def _layer_norm(h, g, b):
    mu = jnp.mean(h, axis=-1, keepdims=True)
    hc = h - mu
    var = jnp.mean(hc * hc, axis=-1, keepdims=True)
    return hc * lax.rsqrt(var + LN_EPS) * g + b


def _post_mix_kernel(oa_ref, ob_ref, x_ref, wo_ref, g_ref, b_ref, wr_ref, br_ref,
                     x1_ref, route_ref, cnt_ref, carry_ref, *, tm):
    @pl.when(pl.program_id(0) == 0)
    def _():
        carry_ref[...] = jnp.zeros_like(carry_ref)

    y = jnp.dot(oa_ref[...], wo_ref[0:GLA_WIDTH, :], preferred_element_type=F32)
    y = y + jnp.dot(ob_ref[...], wo_ref[GLA_WIDTH:, :], preferred_element_type=F32)
    x1 = _layer_norm(ALPHA_DN * x_ref[...] + y, g_ref[...], b_ref[...])
    x1_ref[...] = x1

    logits = jnp.dot(x1, wr_ref[...], precision=HIGHEST, preferred_element_type=F32) + br_ref[...]
    lane = lax.broadcasted_iota(jnp.int32, (tm, ROUTE_W), 1)
    big = jnp.int32(1 << 20)
    neg = jnp.float32(-jnp.inf)

    def first_argmax(vals):
        m = jnp.max(vals, axis=-1, keepdims=True)
        idx = jnp.min(jnp.where(vals == m, lane, big), axis=-1, keepdims=True)
        return m, idx

    is_group = (lane >= ROUTER_GROUP_COL) & (lane < ROUTER_GROUP_COL + N_GROUPS)
    gl = jnp.where(is_group, logits, neg)
    gmax, gidx = first_argmax(gl)
    p_group = 1.0 / jnp.sum(jnp.exp(gl - gmax), axis=-1, keepdims=True)
    g_sel = gidx - ROUTER_GROUP_COL
    el = jnp.where((lane // EXPERTS_PER_GROUP) == g_sel, logits, neg)
    t1, e1 = first_argmax(el)
    t2, e2 = first_argmax(jnp.where(lane == e1, neg, el))
    ex = jnp.exp(t2 - t1)
    w1 = 1.0 / (1.0 + ex)
    gate1 = p_group * w1
    gate2 = p_group * (ex * w1)

    oh1 = lane == e1
    oh2 = lane == e2
    ri = lax.broadcasted_iota(jnp.int32, (tm, tm), 0)
    ci = lax.broadcasted_iota(jnp.int32, (tm, tm), 1)
    lstrict = (ci < ri).astype(BF16)
    oh1f = oh1.astype(F32)
    oh2f = oh2.astype(F32)
    c1 = jnp.dot(lstrict, oh1f.astype(BF16), preferred_element_type=F32)
    c2 = jnp.dot(lstrict, oh2f.astype(BF16), preferred_element_type=F32)
    carry = carry_ref[...]
    tot1 = jnp.sum(oh1f, axis=0, keepdims=True)
    tot2 = jnp.sum(oh2f, axis=0, keepdims=True)
    rank1 = jnp.sum(jnp.where(oh1, c1 + carry, 0.0), axis=-1, keepdims=True)
    rank2 = jnp.sum(jnp.where(oh2, c2 + (carry + tot1), 0.0), axis=-1, keepdims=True)
    new_carry = carry + tot1 + tot2
    carry_ref[...] = new_carry
    cnt_ref[...] = new_carry

    cols = (e1.astype(F32), e2.astype(F32), rank1, rank2, gate1, gate2)
    route = jnp.zeros((tm, ROUTE_W), F32)
    for j, col in enumerate(cols):
        route = jnp.where(lane == j, col, route)
    route_ref[...] = route


def _post_mix(o_a, o_b, x2d, w_out_b, ln_g, ln_b, w_router, b_router, *, tm=512):
    n = x2d.shape[0]
    return pl.pallas_call(
        functools.partial(_post_mix_kernel, tm=tm),
        grid=(n // tm,),
        in_specs=[pl.BlockSpec((tm, GLA_WIDTH), lambda i: (i, 0)),
                  pl.BlockSpec((tm, GDN_WIDTH), lambda i: (i, 0)),
                  pl.BlockSpec((tm, D_MODEL), lambda i: (i, 0)),
                  pl.BlockSpec((GLA_WIDTH + GDN_WIDTH, D_MODEL), lambda i: (0, 0)),
                  pl.BlockSpec((1, D_MODEL), lambda i: (0, 0)),
                  pl.BlockSpec((1, D_MODEL), lambda i: (0, 0)),
                  pl.BlockSpec((D_MODEL, ROUTE_W), lambda i: (0, 0)),
                  pl.BlockSpec((1, ROUTE_W), lambda i: (0, 0))],
        out_specs=[pl.BlockSpec((tm, D_MODEL), lambda i: (i, 0)),
                   pl.BlockSpec((tm, ROUTE_W), lambda i: (i, 0)),
                   pl.BlockSpec((1, ROUTE_W), lambda i: (0, 0))],
        out_shape=[jax.ShapeDtypeStruct((n, D_MODEL), F32),
                   jax.ShapeDtypeStruct((n, ROUTE_W), F32),
                   jax.ShapeDtypeStruct((1, ROUTE_W), F32)],
        scratch_shapes=[pltpu.VMEM((1, ROUTE_W), F32)],
        compiler_params=_cparams("arbitrary"),
        name="post_mix",
    )(o_a, o_b, x2d, w_out_b, ln_g, ln_b, w_router, b_router)


EXPERT_BLOCK = 256


def _expert_kernel(blk_e_ref, first_ref, nvalid_ref, tok_ref, x_hbm, wg_ref, wu_ref, wd_ref, y_ref,
                   xbuf, wgb, wub, wdb, sem, *, tb):
    i = pl.program_id(0)
    nvalid = nvalid_ref[0]
    slot = i % 2

    def issue(block, slot_):
        def body(r, carry):
            t = tok_ref[block * tb + r]
            pltpu.make_async_copy(x_hbm.at[pl.ds(t, 1), :], xbuf.at[slot_, pl.ds(r, 1), :], sem.at[slot_]).start()
            return carry
        lax.fori_loop(0, tb, body, 0)

    @pl.when(i == 0)
    def _():
        issue(0, 0)

    @pl.when(i + 1 < nvalid)
    def _():
        issue(i + 1, 1 - slot)

    @pl.when(i < nvalid)
    def _():
        pltpu.make_async_copy(x_hbm.at[pl.ds(0, tb), :], xbuf.at[slot], sem.at[slot]).wait()

        @pl.when(first_ref[i] == 1)
        def _():
            wgb[...] = wg_ref[...].astype(BF16)
            wub[...] = wu_ref[...].astype(BF16)
            wdb[...] = wd_ref[...].astype(BF16)

        xb = xbuf[slot].astype(BF16)
        hg = jnp.dot(xb, wgb[...], preferred_element_type=F32)
        hu = jnp.dot(xb, wub[...], preferred_element_type=F32)
        hid = (_silu(hg) * hu).astype(BF16)
        y_ref[...] = jnp.dot(hid, wdb[...], preferred_element_type=F32)

    @pl.when(i >= nvalid)
    def _():
        y_ref[...] = jnp.zeros_like(y_ref)


def _experts(blk_e, first, nvalid, buf_tok, x1, w_gate, w_up, w_down, *, tb=EXPERT_BLOCK):
    nb = blk_e.shape[0]
    w_map = lambda i, be, fr, nv, tk: (be[i], 0, 0)
    grid_spec = pltpu.PrefetchScalarGridSpec(
        num_scalar_prefetch=4,
        grid=(nb,),
        in_specs=[pl.BlockSpec(memory_space=pl.ANY),
                  pl.BlockSpec((None, D_MODEL, D_EXPERT), w_map),
                  pl.BlockSpec((None, D_MODEL, D_EXPERT), w_map),
                  pl.BlockSpec((None, D_EXPERT, D_MODEL), w_map)],
        out_specs=pl.BlockSpec((tb, D_MODEL), lambda i, be, fr, nv, tk: (i, 0)),
        scratch_shapes=[pltpu.VMEM((2, tb, D_MODEL), F32),
                        pltpu.VMEM((D_MODEL, D_EXPERT), BF16),
                        pltpu.VMEM((D_MODEL, D_EXPERT), BF16),
                        pltpu.VMEM((D_EXPERT, D_MODEL), BF16),
                        pltpu.SemaphoreType.DMA((2,))],
    )
    return pl.pallas_call(
        functools.partial(_expert_kernel, tb=tb),
        grid_spec=grid_spec,
        out_shape=jax.ShapeDtypeStruct((nb * tb, D_MODEL), F32),
        compiler_params=_cparams("arbitrary"),
        name="experts",
    )(blk_e, first, nvalid, buf_tok, x1, w_gate, w_up, w_down)


def _combine_kernel(dest_ref, y_hbm, x1_ref, route_ref, g_ref, b_ref, o_ref, ybuf, sem, *, tt):
    i = pl.program_id(0)
    nsteps = pl.num_programs(0)
    slot = i % 2

    def issue(tile, slot_):
        def body(r, carry):
            a = (tile * tt + r) * TOP_K
            for k in range(TOP_K):
                pltpu.make_async_copy(y_hbm.at[pl.ds(dest_ref[a + k], 1), :],
                                      ybuf.at[slot_, k, pl.ds(r, 1), :], sem.at[slot_]).start()
            return carry
        lax.fori_loop(0, tt, body, 0)

    @pl.when(i == 0)
    def _():
        issue(0, 0)

    @pl.when(i + 1 < nsteps)
    def _():
        issue(i + 1, 1 - slot)

    for k in range(TOP_K):
        pltpu.make_async_copy(y_hbm.at[pl.ds(0, tt), :], ybuf.at[slot, k], sem.at[slot]).wait()
    route = route_ref[...]
    y = route[:, R_G1:R_G1 + 1] * ybuf[slot, 0] + route[:, R_G2:R_G2 + 1] * ybuf[slot, 1]
    o_ref[...] = _layer_norm(ALPHA_DN * x1_ref[...] + y, g_ref[...], b_ref[...])


def _combine(dest_flat, ybuf, x1, route, ln_g, ln_b, *, tt=256):
    n = x1.shape[0]
    grid_spec = pltpu.PrefetchScalarGridSpec(
        num_scalar_prefetch=1,
        grid=(n // tt,),
        in_specs=[pl.BlockSpec(memory_space=pl.ANY),
                  pl.BlockSpec((tt, D_MODEL), lambda i, d: (i, 0)),
                  pl.BlockSpec((tt, ROUTE_W), lambda i, d: (i, 0)),
                  pl.BlockSpec((1, D_MODEL), lambda i, d: (0, 0)),
                  pl.BlockSpec((1, D_MODEL), lambda i, d: (0, 0))],
        out_specs=pl.BlockSpec((tt, D_MODEL), lambda i, d: (i, 0)),
        scratch_shapes=[pltpu.VMEM((2, TOP_K, tt, D_MODEL), F32),
                        pltpu.SemaphoreType.DMA((2,))],
    )
    return pl.pallas_call(
        functools.partial(_combine_kernel, tt=tt),
        grid_spec=grid_spec,
        out_shape=jax.ShapeDtypeStruct((n, D_MODEL), F32),
        compiler_params=_cparams("arbitrary"),
        name="combine",
    )(dest_flat, ybuf, x1, route, ln_g, ln_b)


def _dispatch_plan(route, counts_row, n, tb):
    counts = counts_row[0, :N_EXPERTS].astype(jnp.int32)
    padded = (counts + tb - 1) // tb * tb
    pends = jnp.cumsum(padded)
    pstarts = pends - padded
    e = route[:, R_E1:R_E2 + 1].astype(jnp.int32)
    rank = route[:, R_RANK1:R_RANK2 + 1].astype(jnp.int32)
    dest = (pstarts[e] + rank).reshape(-1)
    nb = (n * TOP_K) // tb + N_EXPERTS
    tok = jnp.arange(n * TOP_K, dtype=jnp.int32) // TOP_K
    buf_tok = jnp.zeros((nb * tb,), jnp.int32).at[dest].set(tok)
    nvalid = (pends[-1] // tb).astype(jnp.int32)
    blk = jnp.arange(nb, dtype=jnp.int32)
    blk_e = jnp.minimum(jnp.searchsorted(pends, blk * tb, side='right'), N_EXPERTS - 1).astype(jnp.int32)
    blk_e = jnp.where(blk < nvalid, blk_e, blk_e[jnp.maximum(nvalid - 1, 0)])
    first = jnp.concatenate([jnp.ones((1,), jnp.int32), (blk_e[1:] != blk_e[:-1]).astype(jnp.int32)])
    return blk_e, first, nvalid.reshape(1), buf_tok, dest


def kernel(x, w_in, w_gk_up, b_gk, conv_w, a_log, dt_bias, gla_norm_g, gdn_norm_g, w_out, ln1_g, ln1_b, w_router_group, b_router_group, w_router_expert, b_router_expert, w_gate, w_up, w_down, ln2_g, ln2_b):
    bsz, seq, d = x.shape
    n = bsz * seq
    x2d = x.reshape(n, d)
    w_main, w_aux = _split_w_in(w_in[0])
    main, aux = _in_proj(x2d, w_main, w_aux)
    wgk = jnp.zeros((AUX_W, 256), F32).at[:GLA_GATE_RANK].set(w_gk_up[0])
    o_a = _gla(main, aux, wgk, b_gk[0][None], gla_norm_g[0][None], bsz=bsz, seq=seq)
    o_b = _gdn(main, aux, conv_w[0], _pad_aux_row(a_log[0], AUX_AB), _pad_aux_row(dt_bias[0], AUX_AB),
               gdn_norm_g[0][None], bsz=bsz, seq=seq)
    rpad = jnp.zeros((d, ROUTE_W - N_EXPERTS - N_GROUPS), F32)
    w_router = jnp.concatenate([w_router_expert[0], w_router_group[0], rpad], axis=1)
    b_router = jnp.concatenate([b_router_expert[0], b_router_group[0], rpad[0]])[None]
    x1, route, counts = _post_mix(o_a, o_b, x2d, w_out[0].astype(BF16), ln1_g[0][None], ln1_b[0][None],
                                  w_router, b_router)
    blk_e, first, nvalid, buf_tok, dest = _dispatch_plan(route, counts, n, EXPERT_BLOCK)
    ybuf = _experts(blk_e, first, nvalid, buf_tok, x1, w_gate[0], w_up[0], w_down[0])
    out = _combine(dest, ybuf, x1, route, ln2_g[0][None], ln2_b[0][None])
    return out.reshape(bsz, seq, d)
```

```python
import functools

import jax
import jax.numpy as jnp
import numpy as np
from jax import lax
from jax.experimental import pallas as pl
from jax.experimental.pallas import tpu as pltpu

F32 = jnp.float32
BF16 = jnp.bfloat16
HIGHEST = lax.Precision.HIGHEST

D_MODEL = 1024
DEPTH = 1
GLA_HEADS = 4
GLA_DK = 64
GLA_DV = 128
GLA_WIDTH = GLA_HEADS * GLA_DV
GLA_GATE_RANK = 16
GLA_GATE_NORM = 16.0
GDN_HEADS = 4
GDN_DK = 128
GDN_DV = 128
GDN_WIDTH = GDN_HEADS * GDN_DV
CONV_W = 4
CHUNK = 64
N_GROUPS = 8
EXPERTS_PER_GROUP = 8
N_EXPERTS = N_GROUPS * EXPERTS_PER_GROUP
TOP_K = 2
D_EXPERT = 512
LN_EPS = 1e-5
RMS_EPS = 1e-6
ALPHA_DN = (2.0 * DEPTH) ** 0.25

MAIN_W = 2 * GLA_HEADS * GLA_DK + 2 * GLA_WIDTH + 4 * GDN_WIDTH
AUX_W = 128
AUX_LRA = 0
AUX_BB = GLA_GATE_RANK
AUX_AB = GLA_GATE_RANK + GDN_HEADS
SUB = 16
VMEM_LIMIT = 56 * 1024 * 1024


def _cparams(*sem):
    return pltpu.CompilerParams(dimension_semantics=sem, vmem_limit_bytes=VMEM_LIMIT)


def _sigmoid(x):
    return 1.0 / (1.0 + jnp.exp(-x))


def _silu(x):
    return x * _sigmoid(x)


def _log_sigmoid(x):
    return jnp.minimum(x, 0.0) - jnp.log(1.0 + jnp.exp(-jnp.abs(x)))


def _softplus(x):
    return jnp.maximum(x, 0.0) + jnp.log(1.0 + jnp.exp(-jnp.abs(x)))


def _in_proj_kernel(x_ref, wm_ref, wa_ref, main_ref, aux_ref, *, col_block):
    xb = x_ref[...].astype(BF16)
    for j in range(MAIN_W // col_block):
        sl = slice(j * col_block, (j + 1) * col_block)
        main_ref[:, sl] = jnp.dot(xb, wm_ref[:, sl], preferred_element_type=F32).astype(BF16)
    aux_ref[...] = jnp.dot(xb, wa_ref[...], preferred_element_type=F32)


def _in_proj(x2d, w_main, w_aux, *, tm=512, col_block=512):
    n = x2d.shape[0]
    return pl.pallas_call(
        functools.partial(_in_proj_kernel, col_block=col_block),
        grid=(n // tm,),
        in_specs=[pl.BlockSpec((tm, D_MODEL), lambda i: (i, 0)),
                  pl.BlockSpec((D_MODEL, MAIN_W), lambda i: (0, 0)),
                  pl.BlockSpec((D_MODEL, AUX_W), lambda i: (0, 0))],
        out_specs=[pl.BlockSpec((tm, MAIN_W), lambda i: (i, 0)),
                   pl.BlockSpec((tm, AUX_W), lambda i: (i, 0))],
        out_shape=[jax.ShapeDtypeStruct((n, MAIN_W), BF16),
                   jax.ShapeDtypeStruct((n, AUX_W), F32)],
        compiler_params=_cparams("parallel"),
        name="in_proj",
    )(x2d, w_main, w_aux)


def _split_w_in(w_in):
    sizes = (GLA_HEADS * GLA_DK, GLA_HEADS * GLA_DK, GLA_WIDTH, GLA_WIDTH, GLA_GATE_RANK,
             GDN_HEADS * GDN_DK, GDN_HEADS * GDN_DK, GDN_WIDTH, GDN_WIDTH, GDN_HEADS, GDN_HEADS)
    offs = np.cumsum((0,) + sizes)
    seg = [w_in[:, offs[i]:offs[i + 1]] for i in range(len(sizes))]
    qa, ka, va, ra, lra, qb, kb, vb, zb, bb, ab = seg
    w_main = jnp.concatenate([qa, ka, va, ra, qb, kb, vb, zb], axis=1).astype(BF16)
    pad = jnp.zeros((w_in.shape[0], AUX_W - GLA_GATE_RANK - 2 * GDN_HEADS), w_in.dtype)
    w_aux = jnp.concatenate([lra, bb, ab, pad], axis=1).astype(BF16)
    return w_main, w_aux


_NT = (((1,), (1,)), ((), ()))
_TN = (((0,), (0,)), ((), ()))


def _gla_kernel(q_ref, k_ref, v_ref, r_ref, aux_ref, wgk_ref, bgk_ref, g_ref, o_ref, st_ref, *, ts):
    c = CHUNK
    nsub = c // SUB

    @pl.when(pl.program_id(1) == 0)
    def _():
        st_ref[...] = jnp.zeros_like(st_ref)

    pre = jnp.dot(aux_ref[...], wgk_ref[...], precision=HIGHEST, preferred_element_type=F32) + bgk_ref[...]
    gk_all = _log_sigmoid(pre) * (1.0 / GLA_GATE_NORM)

    ri = lax.broadcasted_iota(jnp.int32, (c, c), 0)
    ci = lax.broadcasted_iota(jnp.int32, (c, c), 1)
    tri = (ci <= ri).astype(F32)
    row128 = lax.broadcasted_iota(jnp.int32, (c, 128), 0)
    lane128 = lax.broadcasted_iota(jnp.int32, (1, 128), 1)
    lane_masks = [(lane128 // GLA_DK) == hh for hh in range(2)]
    sub_valid = [row128 < SUB * (i + 1) for i in range(nsub)]
    ar = lax.broadcasted_iota(jnp.int32, (c, nsub * c), 0)
    ac = lax.broadcasted_iota(jnp.int32, (c, nsub * c), 1)
    amask = ((ac // c) == (ar // SUB)) & ((ac % c) <= ar)
    scale = GLA_DK ** -0.5
    gnorm = g_ref[...]

    for ch in range(ts // c):
        rows = slice(ch * c, (ch + 1) * c)
        b_all = jnp.dot(tri, gk_all[rows], precision=HIGHEST, preferred_element_type=F32)
        for p in range(2):
            lanes = slice(128 * p, 128 * (p + 1))
            bp = b_all[:, lanes]
            qp = q_ref[rows, lanes].astype(F32) * scale
            kp = k_ref[rows, lanes].astype(F32)
            c_row = jnp.concatenate(
                [jnp.broadcast_to(bp[SUB * i:SUB * i + 1, :], (SUB, 128)) for i in range(nsub)], axis=0)
            qt = (qp * jnp.exp(bp - c_row)).astype(BF16)
            qd = qp * jnp.exp(bp)
            b_last = bp[c - 1:c, :]
            kd = kp * jnp.exp(b_last - bp)
            kts = []
            for i in range(nsub):
                e = jnp.exp(jnp.where(sub_valid[i], bp[SUB * i:SUB * i + 1, :] - bp, 0.0))
                kts.append(jnp.where(sub_valid[i], kp * e, 0.0))
            kstack = jnp.concatenate(kts, axis=0)
            st = st_ref[p]
            st_b = st.astype(BF16)
            upd = st * jnp.exp(b_last)
            for hh in range(2):
                h = 2 * p + hh
                lm = lane_masks[hh]
                vh = v_ref[rows, 128 * h:128 * (h + 1)]
                ks = jnp.where(lm, kstack, 0.0).astype(BF16)
                r_mat = lax.dot_general(qt, ks, _NT, preferred_element_type=F32)
                a_wide = jnp.where(amask, r_mat, 0.0).astype(BF16)
                v4 = jnp.concatenate([vh] * nsub, axis=0)
                o = jnp.dot(a_wide, v4, preferred_element_type=F32)
                o = o + lax.dot_general(jnp.where(lm, qd, 0.0).astype(BF16), st_b, _NT,
                                        preferred_element_type=F32)
                upd = upd + lax.dot_general(vh, jnp.where(lm, kd, 0.0).astype(BF16), _TN,
                                            preferred_element_type=F32)
                o = o * lax.rsqrt(jnp.mean(o * o, axis=-1, keepdims=True) + RMS_EPS) * gnorm
                gate = _silu(r_ref[rows, 128 * h:128 * (h + 1)].astype(F32))
                o_ref[rows, 128 * h:128 * (h + 1)] = (o * gate).astype(BF16)
            st_ref[p] = upd


def _gla(main, aux, wgk_pad, b_gk, gla_norm_g, *, bsz, seq, ts=256):
    nt = seq // ts
    row = lambda b, t: b * nt + t
    return pl.pallas_call(
        functools.partial(_gla_kernel, ts=ts),
        grid=(bsz, nt),
        in_specs=[pl.BlockSpec((ts, 256), lambda b, t: (row(b, t), 0)),
                  pl.BlockSpec((ts, 256), lambda b, t: (row(b, t), 1)),
                  pl.BlockSpec((ts, 512), lambda b, t: (row(b, t), 1)),
                  pl.BlockSpec((ts, 512), lambda b, t: (row(b, t), 2)),
                  pl.BlockSpec((ts, AUX_W), lambda b, t: (row(b, t), 0)),
                  pl.BlockSpec((AUX_W, 256), lambda b, t: (0, 0)),
                  pl.BlockSpec((1, 256), lambda b, t: (0, 0)),
                  pl.BlockSpec((1, 128), lambda b, t: (0, 0))],
        out_specs=pl.BlockSpec((ts, GLA_WIDTH), lambda b, t: (row(b, t), 0)),
        out_shape=jax.ShapeDtypeStruct((bsz * seq, GLA_WIDTH), BF16),
        scratch_shapes=[pltpu.VMEM((2, 128, 128), F32)],
        compiler_params=_cparams("parallel", "arbitrary"),
        name="gla",
    )(main, main, main, main, aux, wgk_pad, b_gk, gla_norm_g)


CONV_PAD = 8


def _mm(a, b):
    return jnp.dot(a.astype(BF16), b.astype(BF16), preferred_element_type=F32)


def _gdn_kernel(q_ref, k_ref, v_ref, z_ref, aux_ref, cw_ref, alog_ref, dtb_ref, g_ref, o_ref,
                st_ref, xe_ref, *, ts):
    c = CHUNK
    nch = ts // c
    hw = GDN_HEADS * GDN_DK

    @pl.when(pl.program_id(1) == 0)
    def _():
        st_ref[...] = jnp.zeros_like(st_ref)
        xe_ref[0:CONV_PAD, :] = jnp.zeros((CONV_PAD, 3 * hw), F32)

    xe_ref[CONV_PAD:, 0:hw] = q_ref[...].astype(F32)
    xe_ref[CONV_PAD:, hw:2 * hw] = k_ref[...].astype(F32)
    xe_ref[CONV_PAD:, 2 * hw:3 * hw] = v_ref[...].astype(F32)

    aux = aux_ref[...]
    beta_full = _sigmoid(aux)
    g_full = -jnp.exp(alog_ref[...]) * _softplus(aux + dtb_ref[...])
    ri = lax.broadcasted_iota(jnp.int32, (ts, ts), 0)
    ci = lax.broadcasted_iota(jnp.int32, (ts, ts), 1)
    same64 = (ri // c) == (ci // c)
    same32 = (ri // 32) == (ci // 32)
    same16 = (ri // 16) == (ci // 16)
    causal = same64 & (ci <= ri)
    strict = same64 & (ci < ri)
    lvl0 = same16 & (ci < ri)
    lvl1 = same32 & jnp.logical_not(same16) & (ci < ri)
    lvl2 = same64 & jnp.logical_not(same32) & (ci < ri)
    gam_full = jnp.dot(causal.astype(F32), g_full, precision=HIGHEST, preferred_element_type=F32)
    gam_t = gam_full.T
    gnorm = g_ref[...]

    def conv(col):
        acc = jnp.zeros((ts, 128), F32)
        for i in range(CONV_W):
            lo = CONV_PAD - (CONV_W - 1) + i
            acc = acc + xe_ref[lo:lo + ts, col:col + 128] * cw_ref[i:i + 1, col:col + 128]
        return _silu(acc)

    for h in range(GDN_HEADS):
        qh = conv(128 * h)
        kh = conv(hw + 128 * h)
        vh = conv(2 * hw + 128 * h)
        qn = qh * lax.rsqrt(jnp.sum(qh * qh, axis=-1, keepdims=True) + RMS_EPS) * (GDN_DK ** -0.5)
        kn = kh * lax.rsqrt(jnp.sum(kh * kh, axis=-1, keepdims=True) + RMS_EPS)
        beta = beta_full[:, AUX_BB + h:AUX_BB + h + 1]
        gam = gam_full[:, AUX_AB + h:AUX_AB + h + 1]
        gam_r = gam_t[AUX_AB + h:AUX_AB + h + 1, :]
        egam = jnp.exp(gam)
        kb = kn.astype(BF16)
        kk = lax.dot_general(kb, kb, _NT, preferred_element_type=F32)
        qk = lax.dot_general(qn.astype(BF16), kb, _NT, preferred_element_type=F32)
        dec = jnp.exp(jnp.where(causal, gam - gam_r, 0.0))
        a = beta * kk * dec
        attn = jnp.where(causal, qk * dec, 0.0).astype(BF16)

        x1 = jnp.where(lvl0, -a, 0.0)
        x2 = _mm(x1, x1)
        p = x1 + x2 + _mm(x1, x2)
        x4 = _mm(x2, x2)
        p = p + x4 + _mm(p, x4)
        x8 = _mm(x4, x4)
        p = p + x8 + _mm(p, x8)
        for lvl in (lvl1, lvl2):
            low = jnp.where(lvl, a, 0.0)
            zz = low + _mm(p, low)
            p = p - zz - _mm(zz, p)

        rhs = jnp.concatenate([beta * vh, beta * egam * kn], axis=1)
        uw = rhs + _mm(p, rhs)
        u = uw[:, :GDN_DV]
        w = uw[:, GDN_DV:].astype(BF16)
        qd = (qn * egam).astype(BF16)
        gl_rows = jnp.concatenate(
            [jnp.broadcast_to(gam[c * (j + 1) - 1:c * (j + 1), :], (c, 1)) for j in range(nch)], axis=0)
        kd = (kn * jnp.exp(gl_rows - gam)).astype(BF16)
        s = st_ref[h]
        vnews = []
        for j in range(nch):
            rows = slice(c * j, c * (j + 1))
            sb = s.astype(BF16)
            vnew = u[rows] - jnp.dot(w[rows], sb, preferred_element_type=F32)
            vn16 = vnew.astype(BF16)
            vnews.append(vn16)
            vfull = jnp.concatenate(vnews + [jnp.zeros((c, GDN_DV), BF16)] * (nch - 1 - j), axis=0)
            o = jnp.dot(qd[rows], sb, preferred_element_type=F32)
            o = o + jnp.dot(attn[rows], vfull, preferred_element_type=F32)
            s = jnp.exp(gam[c * (j + 1) - 1:c * (j + 1), :]) * s + lax.dot_general(
                kd[rows], vn16, _TN, preferred_element_type=F32)
            o = o * lax.rsqrt(jnp.mean(o * o, axis=-1, keepdims=True) + RMS_EPS) * gnorm
            gate = _silu(z_ref[rows, 128 * h:128 * (h + 1)].astype(F32))
            o_ref[rows, 128 * h:128 * (h + 1)] = (o * gate).astype(BF16)
        st_ref[h] = s

    xe_ref[0:CONV_PAD, :] = xe_ref[ts:ts + CONV_PAD, :]


def _gdn(main, aux, conv_w, alog_pad, dtb_pad, gdn_norm_g, *, bsz, seq, ts=256):
    nt = seq // ts
    row = lambda b, t: b * nt + t
    hw = GDN_HEADS * GDN_DK
    return pl.pallas_call(
        functools.partial(_gdn_kernel, ts=ts),
        grid=(bsz, nt),
        in_specs=[pl.BlockSpec((ts, hw), lambda b, t: (row(b, t), 3)),
                  pl.BlockSpec((ts, hw), lambda b, t: (row(b, t), 4)),
                  pl.BlockSpec((ts, hw), lambda b, t: (row(b, t), 5)),
                  pl.BlockSpec((ts, hw), lambda b, t: (row(b, t), 6)),
                  pl.BlockSpec((ts, AUX_W), lambda b, t: (row(b, t), 0)),
                  pl.BlockSpec((CONV_W, 3 * hw), lambda b, t: (0, 0)),
                  pl.BlockSpec((1, AUX_W), lambda b, t: (0, 0)),
                  pl.BlockSpec((1, AUX_W), lambda b, t: (0, 0)),
                  pl.BlockSpec((1, GDN_DV), lambda b, t: (0, 0))],
        out_specs=pl.BlockSpec((ts, GDN_WIDTH), lambda b, t: (row(b, t), 0)),
        out_shape=jax.ShapeDtypeStruct((bsz * seq, GDN_WIDTH), BF16),
        scratch_shapes=[pltpu.VMEM((GDN_HEADS, GDN_DK, GDN_DV), F32),
                        pltpu.VMEM((ts + CONV_PAD, 3 * hw), F32)],
        compiler_params=_cparams("parallel", "arbitrary"),
        name="gdn",
    )(main, main, main, main, aux, conv_w, alog_pad, dtb_pad, gdn_norm_g)


def _pad_aux_row(v, offset):
    return jnp.zeros((1, AUX_W), F32).at[0, offset:offset + v.shape[0]].set(v.astype(F32))


ROUTE_W = 128
R_E1, R_E2, R_RANK1, R_RANK2, R_G1, R_G2 = range(6)
ROUTER_GROUP_COL = N_EXPERTS


def _layer_norm(h, g, b):
    mu = jnp.mean(h, axis=-1, keepdims=True)
    hc = h - mu
    var = jnp.mean(hc * hc, axis=-1, keepdims=True)
    return hc * lax.rsqrt(var + LN_EPS) * g + b


def _pack_bf16_pairs(x):
    w = x.shape[1] // 2
    lo = lax.bitcast_convert_type(x[:, :w].astype(BF16).astype(F32), jnp.uint32)
    hi = lax.bitcast_convert_type(x[:, w:].astype(BF16).astype(F32), jnp.uint32)
    return (lo >> 16) | (hi & jnp.uint32(0xFFFF0000))


def _unpack_bf16_pairs(p):
    lo = lax.bitcast_convert_type(p << 16, F32)
    hi = lax.bitcast_convert_type(p & jnp.uint32(0xFFFF0000), F32)
    return lo, hi


def _post_mix_kernel(oa_ref, ob_ref, x_ref, wo_ref, g_ref, b_ref, wr_ref, br_ref,
                     x1_ref, x1p_ref, route_ref, cnt_ref, carry_ref, *, tm):
    @pl.when(pl.program_id(0) == 0)
    def _():
        carry_ref[...] = jnp.zeros_like(carry_ref)

    y = jnp.dot(oa_ref[...], wo_ref[0:GLA_WIDTH, :], preferred_element_type=F32)
    y = y + jnp.dot(ob_ref[...], wo_ref[GLA_WIDTH:, :], preferred_element_type=F32)
    x1 = _layer_norm(ALPHA_DN * x_ref[...] + y, g_ref[...], b_ref[...])
    x1_ref[...] = x1
    x1p_ref[...] = _pack_bf16_pairs(x1)

    logits = jnp.dot(x1, wr_ref[...], precision=HIGHEST, preferred_element_type=F32) + br_ref[...]
    lane = lax.broadcasted_iota(jnp.int32, (tm, ROUTE_W), 1)
    big = jnp.int32(1 << 20)
    neg = jnp.float32(-jnp.inf)

    def first_argmax(vals):
        m = jnp.max(vals, axis=-1, keepdims=True)
        idx = jnp.min(jnp.where(vals == m, lane, big), axis=-1, keepdims=True)
        return m, idx

    is_group = (lane >= ROUTER_GROUP_COL) & (lane < ROUTER_GROUP_COL + N_GROUPS)
    gl = jnp.where(is_group, logits, neg)
    gmax, gidx = first_argmax(gl)
    p_group = 1.0 / jnp.sum(jnp.exp(gl - gmax), axis=-1, keepdims=True)
    g_sel = gidx - ROUTER_GROUP_COL
    el = jnp.where((lane // EXPERTS_PER_GROUP) == g_sel, logits, neg)
    t1, e1 = first_argmax(el)
    t2, e2 = first_argmax(jnp.where(lane == e1, neg, el))
    ex = jnp.exp(t2 - t1)
    w1 = 1.0 / (1.0 + ex)
    gate1 = p_group * w1
    gate2 = p_group * (ex * w1)

    oh1 = lane == e1
    oh2 = lane == e2
    ri = lax.broadcasted_iota(jnp.int32, (tm, tm), 0)
    ci = lax.broadcasted_iota(jnp.int32, (tm, tm), 1)
    lstrict = (ci < ri).astype(BF16)
    oh1f = oh1.astype(F32)
    oh2f = oh2.astype(F32)
    c1 = jnp.dot(lstrict, oh1f.astype(BF16), preferred_element_type=F32)
    c2 = jnp.dot(lstrict, oh2f.astype(BF16), preferred_element_type=F32)
    carry = carry_ref[...]
    tot1 = jnp.sum(oh1f, axis=0, keepdims=True)
    tot2 = jnp.sum(oh2f, axis=0, keepdims=True)
    rank1 = jnp.sum(jnp.where(oh1, c1 + carry, 0.0), axis=-1, keepdims=True)
    rank2 = jnp.sum(jnp.where(oh2, c2 + (carry + tot1), 0.0), axis=-1, keepdims=True)
    new_carry = carry + tot1 + tot2
    carry_ref[...] = new_carry
    cnt_ref[...] = new_carry

    cols = (e1.astype(F32), e2.astype(F32), rank1, rank2, gate1, gate2)
    route = jnp.zeros((tm, ROUTE_W), F32)
    for j, col in enumerate(cols):
        route = jnp.where(lane == j, col, route)
    route_ref[...] = route


def _post_mix(o_a, o_b, x2d, w_out_b, ln_g, ln_b, w_router, b_router, *, tm=512):
    n = x2d.shape[0]
    return pl.pallas_call(
        functools.partial(_post_mix_kernel, tm=tm),
        grid=(n // tm,),
        in_specs=[pl.BlockSpec((tm, GLA_WIDTH), lambda i: (i, 0)),
                  pl.BlockSpec((tm, GDN_WIDTH), lambda i: (i, 0)),
                  pl.BlockSpec((tm, D_MODEL), lambda i: (i, 0)),
                  pl.BlockSpec((GLA_WIDTH + GDN_WIDTH, D_MODEL), lambda i: (0, 0)),
                  pl.BlockSpec((1, D_MODEL), lambda i: (0, 0)),
                  pl.BlockSpec((1, D_MODEL), lambda i: (0, 0)),
                  pl.BlockSpec((D_MODEL, ROUTE_W), lambda i: (0, 0)),
                  pl.BlockSpec((1, ROUTE_W), lambda i: (0, 0))],
        out_specs=[pl.BlockSpec((tm, D_MODEL), lambda i: (i, 0)),
                   pl.BlockSpec((tm, D_MODEL // 2), lambda i: (i, 0)),
                   pl.BlockSpec((tm, ROUTE_W), lambda i: (i, 0)),
                   pl.BlockSpec((1, ROUTE_W), lambda i: (0, 0))],
        out_shape=[jax.ShapeDtypeStruct((n, D_MODEL), F32),
                   jax.ShapeDtypeStruct((n, D_MODEL // 2), jnp.uint32),
                   jax.ShapeDtypeStruct((n, ROUTE_W), F32),
                   jax.ShapeDtypeStruct((1, ROUTE_W), F32)],
        scratch_shapes=[pltpu.VMEM((1, ROUTE_W), F32)],
        compiler_params=_cparams("arbitrary"),
        name="post_mix",
    )(o_a, o_b, x2d, w_out_b, ln_g, ln_b, w_router, b_router)


EXPERT_BLOCK = 256


HALF = D_MODEL // 2
ROW_UNROLL = 8


def _plan_kernel(route_ref, cnt_ref, dest_ref, *, tm, tb):
    counts = cnt_ref[...].astype(jnp.int32)
    padded = ((counts + (tb - 1)) & jnp.int32(-tb)).astype(F32)
    ri = lax.broadcasted_iota(jnp.int32, (ROUTE_W, ROUTE_W), 0)
    ci = lax.broadcasted_iota(jnp.int32, (ROUTE_W, ROUTE_W), 1)
    pstart = jnp.dot(jnp.broadcast_to(padded, (8, ROUTE_W)), (ri < ci).astype(F32),
                     precision=HIGHEST, preferred_element_type=F32)[0:1, :]
    route = route_ref[...]
    lane = lax.broadcasted_iota(jnp.int32, (tm, ROUTE_W), 1).astype(F32)
    out = jnp.zeros((tm, ROUTE_W), F32)
    for k, (ce, cr) in enumerate(((R_E1, R_RANK1), (R_E2, R_RANK2))):
        off = jnp.sum(jnp.where(lane == route[:, ce:ce + 1], pstart, 0.0), axis=-1, keepdims=True)
        out = jnp.where(lane == float(k), off + route[:, cr:cr + 1], out)
    dest_ref[...] = out.T[0:8, :].astype(jnp.int32)


def _plan(route, counts, *, tb, tm=512):
    n = route.shape[0]
    return pl.pallas_call(
        functools.partial(_plan_kernel, tm=tm, tb=tb),
        grid=(n // tm,),
        in_specs=[pl.BlockSpec((tm, ROUTE_W), lambda i: (i, 0)),
                  pl.BlockSpec((1, ROUTE_W), lambda i: (0, 0))],
        out_specs=pl.BlockSpec((8, tm), lambda i: (0, i)),
        out_shape=jax.ShapeDtypeStruct((8, n), jnp.int32),
        compiler_params=_cparams("parallel"),
        name="plan",
    )(route, counts)


def _dispatch_kernel(d1_ref, d2_ref, xp_ref, xs_init_ref, xs_ref, sem, *, tm):
    del xs_init_ref
    base = pl.program_id(0) * tm

    def body(j, carry):
        for u in range(ROW_UNROLL):
            r = j * ROW_UNROLL + u
            for d_ref in (d1_ref, d2_ref):
                pltpu.make_async_copy(xp_ref.at[pl.ds(r, 1), :], xs_ref.at[pl.ds(d_ref[base + r], 1), :], sem).start()
        return carry
    lax.fori_loop(0, tm // ROW_UNROLL, body, 0)
    for _ in range(TOP_K):
        pltpu.make_async_copy(xp_ref, xs_ref.at[pl.ds(0, tm), :], sem).wait()


def _dispatch(dest1, dest2, x1p, n_rows, *, tm=512):
    n = x1p.shape[0]
    grid_spec = pltpu.PrefetchScalarGridSpec(
        num_scalar_prefetch=2,
        grid=(n // tm,),
        in_specs=[pl.BlockSpec((tm, HALF), lambda i, a, b: (i, 0)),
                  pl.BlockSpec(memory_space=pl.ANY)],
        out_specs=pl.BlockSpec(memory_space=pl.ANY),
        scratch_shapes=[pltpu.SemaphoreType.DMA(())],
    )
    return pl.pallas_call(
        functools.partial(_dispatch_kernel, tm=tm),
        grid_spec=grid_spec,
        out_shape=jax.ShapeDtypeStruct((n_rows, HALF), jnp.uint32),
        input_output_aliases={3: 0},
        compiler_params=_cparams("arbitrary"),
        name="dispatch",
    )(dest1, dest2, x1p, jnp.zeros((n_rows, HALF), jnp.uint32))


def _expert_kernel(blk_e_ref, first_ref, nvalid_ref, xs_ref, wg_ref, wu_ref, wd_ref, y_ref, wgb, wub, wdb):
    i = pl.program_id(0)

    @pl.when(i < nvalid_ref[0])
    def _():
        @pl.when(first_ref[i] == 1)
        def _():
            wgb[...] = wg_ref[...].astype(BF16)
            wub[...] = wu_ref[...].astype(BF16)
            wdb[...] = wd_ref[...].astype(BF16)

        lo, hi = _unpack_bf16_pairs(xs_ref[...])
        lo = lo.astype(BF16)
        hi = hi.astype(BF16)
        hg = jnp.dot(lo, wgb[0:HALF, :], preferred_element_type=F32)
        hg = hg + jnp.dot(hi, wgb[HALF:, :], preferred_element_type=F32)
        hu = jnp.dot(lo, wub[0:HALF, :], preferred_element_type=F32)
        hu = hu + jnp.dot(hi, wub[HALF:, :], preferred_element_type=F32)
        hid = (_silu(hg) * hu).astype(BF16)
        y_ref[...] = _pack_bf16_pairs(jnp.dot(hid, wdb[...], preferred_element_type=F32))

    @pl.when(i >= nvalid_ref[0])
    def _():
        y_ref[...] = jnp.zeros_like(y_ref)


def _experts(blk_e, first, nvalid, xs, w_gate, w_up, w_down, *, tb=EXPERT_BLOCK):
    nb = blk_e.shape[0]
    w_map = lambda i, be, fr, nv: (be[i], 0, 0)
    grid_spec = pltpu.PrefetchScalarGridSpec(
        num_scalar_prefetch=3,
        grid=(nb,),
        in_specs=[pl.BlockSpec((tb, HALF), lambda i, be, fr, nv: (jnp.minimum(i, nv[0] - 1), 0)),
                  pl.BlockSpec((None, D_MODEL, D_EXPERT), w_map),
                  pl.BlockSpec((None, D_MODEL, D_EXPERT), w_map),
                  pl.BlockSpec((None, D_EXPERT, D_MODEL), w_map)],
        out_specs=pl.BlockSpec((tb, HALF), lambda i, be, fr, nv: (i, 0)),
        scratch_shapes=[pltpu.VMEM((D_MODEL, D_EXPERT), BF16),
                        pltpu.VMEM((D_MODEL, D_EXPERT), BF16),
                        pltpu.VMEM((D_EXPERT, D_MODEL), BF16)],
    )
    return pl.pallas_call(
        _expert_kernel,
        grid_spec=grid_spec,
        out_shape=jax.ShapeDtypeStruct((nb * tb, HALF), jnp.uint32),
        compiler_params=_cparams("arbitrary"),
        name="experts",
    )(blk_e, first, nvalid, xs, w_gate, w_up, w_down)


def _combine_kernel(d1_ref, d2_ref, y_hbm, x1_ref, route_ref, g_ref, b_ref, o_ref, ybuf, sem, *, tt):
    i = pl.program_id(0)
    nsteps = pl.num_programs(0)
    slot = i % 2

    def issue(tile, slot_):
        def body(j, carry):
            for u in range(ROW_UNROLL):
                r = j * ROW_UNROLL + u
                for k, d_ref in enumerate((d1_ref, d2_ref)):
                    pltpu.make_async_copy(y_hbm.at[pl.ds(d_ref[tile * tt + r], 1), :],
                                          ybuf.at[slot_, k, pl.ds(r, 1), :], sem.at[slot_]).start()
            return carry
        lax.fori_loop(0, tt // ROW_UNROLL, body, 0)

    @pl.when(i == 0)
    def _():
        issue(0, 0)

    @pl.when(i + 1 < nsteps)
    def _():
        issue(i + 1, 1 - slot)

    for k in range(TOP_K):
        pltpu.make_async_copy(y_hbm.at[pl.ds(0, tt), :], ybuf.at[slot, k], sem.at[slot]).wait()
    route = route_ref[...]
    g1 = route[:, R_G1:R_G1 + 1]
    g2 = route[:, R_G2:R_G2 + 1]
    lo1, hi1 = _unpack_bf16_pairs(ybuf[slot, 0])
    lo2, hi2 = _unpack_bf16_pairs(ybuf[slot, 1])
    h_lo = ALPHA_DN * x1_ref[:, 0:HALF] + (g1 * lo1 + g2 * lo2)
    h_hi = ALPHA_DN * x1_ref[:, HALF:] + (g1 * hi1 + g2 * hi2)
    mu = (jnp.sum(h_lo, axis=-1, keepdims=True) + jnp.sum(h_hi, axis=-1, keepdims=True)) * (1.0 / D_MODEL)
    c_lo = h_lo - mu
    c_hi = h_hi - mu
    var = (jnp.sum(c_lo * c_lo, axis=-1, keepdims=True) + jnp.sum(c_hi * c_hi, axis=-1, keepdims=True)) * (1.0 / D_MODEL)
    inv = lax.rsqrt(var + LN_EPS)
    o_ref[:, 0:HALF] = c_lo * inv * g_ref[:, 0:HALF] + b_ref[:, 0:HALF]
    o_ref[:, HALF:] = c_hi * inv * g_ref[:, HALF:] + b_ref[:, HALF:]


def _combine(dest1, dest2, ybuf, x1, route, ln_g, ln_b, *, tt=256):
    n = x1.shape[0]
    grid_spec = pltpu.PrefetchScalarGridSpec(
        num_scalar_prefetch=2,
        grid=(n // tt,),
        in_specs=[pl.BlockSpec(memory_space=pl.ANY),
                  pl.BlockSpec((tt, D_MODEL), lambda i, a, b: (i, 0)),
                  pl.BlockSpec((tt, ROUTE_W), lambda i, a, b: (i, 0)),
                  pl.BlockSpec((1, D_MODEL), lambda i, a, b: (0, 0)),
                  pl.BlockSpec((1, D_MODEL), lambda i, a, b: (0, 0))],
        out_specs=pl.BlockSpec((tt, D_MODEL), lambda i, a, b: (i, 0)),
        scratch_shapes=[pltpu.VMEM((2, TOP_K, tt, HALF), jnp.uint32),
                        pltpu.SemaphoreType.DMA((2,))],
    )
    return pl.pallas_call(
        functools.partial(_combine_kernel, tt=tt),
        grid_spec=grid_spec,
        out_shape=jax.ShapeDtypeStruct((n, D_MODEL), F32),
        compiler_params=_cparams("arbitrary"),
        name="combine",
    )(dest1, dest2, ybuf, x1, route, ln_g, ln_b)


def _block_plan(counts_row, n, tb):
    counts = counts_row[0, :N_EXPERTS].astype(jnp.int32)
    pends = jnp.cumsum((counts + tb - 1) // tb * tb)
    nb = (n * TOP_K) // tb + N_EXPERTS
    nvalid = (pends[-1] // tb).astype(jnp.int32)
    blk = jnp.arange(nb, dtype=jnp.int32)
    blk_e = jnp.minimum(jnp.searchsorted(pends, blk * tb, side='right'), N_EXPERTS - 1).astype(jnp.int32)
    blk_e = jnp.where(blk < nvalid, blk_e, blk_e[jnp.maximum(nvalid - 1, 0)])
    first = jnp.concatenate([jnp.ones((1,), jnp.int32), (blk_e[1:] != blk_e[:-1]).astype(jnp.int32)])
    return blk_e, first, nvalid.reshape(1)


def kernel(x, w_in, w_gk_up, b_gk, conv_w, a_log, dt_bias, gla_norm_g, gdn_norm_g, w_out, ln1_g, ln1_b, w_router_group, b_router_group, w_router_expert, b_router_expert, w_gate, w_up, w_down, ln2_g, ln2_b):
    bsz, seq, d = x.shape
    n = bsz * seq
    x2d = x.reshape(n, d)
    w_main, w_aux = _split_w_in(w_in[0])
    main, aux = _in_proj(x2d, w_main, w_aux)
    wgk = jnp.zeros((AUX_W, 256), F32).at[:GLA_GATE_RANK].set(w_gk_up[0])
    o_a = _gla(main, aux, wgk, b_gk[0][None], gla_norm_g[0][None], bsz=bsz, seq=seq)
    o_b = _gdn(main, aux, conv_w[0], _pad_aux_row(a_log[0], AUX_AB), _pad_aux_row(dt_bias[0], AUX_AB),
               gdn_norm_g[0][None], bsz=bsz, seq=seq)
    rpad = jnp.zeros((d, ROUTE_W - N_EXPERTS - N_GROUPS), F32)
    w_router = jnp.concatenate([w_router_expert[0], w_router_group[0], rpad], axis=1)
    b_router = jnp.concatenate([b_router_expert[0], b_router_group[0], rpad[0]])[None]
    x1, x1p, route, counts = _post_mix(o_a, o_b, x2d, w_out[0].astype(BF16), ln1_g[0][None], ln1_b[0][None],
                                       w_router, b_router)
    dest = _plan(route, counts, tb=EXPERT_BLOCK)
    blk_e, first, nvalid = _block_plan(counts, n, EXPERT_BLOCK)
    xs = _dispatch(dest[0], dest[1], x1p, blk_e.shape[0] * EXPERT_BLOCK)
    ybuf = _experts(blk_e, first, nvalid, xs, w_gate[0], w_up[0], w_down[0])
    out = _combine(dest[0], dest[1], ybuf, x1, route, ln2_g[0][None], ln2_b[0][None])
    return out.reshape(bsz, seq, d)
```

```python
import functools

import jax
import jax.numpy as jnp
import numpy as np
from jax import lax
from jax.experimental import pallas as pl
from jax.experimental.pallas import tpu as pltpu

F32 = jnp.float32
BF16 = jnp.bfloat16
HIGHEST = lax.Precision.HIGHEST

D_MODEL = 1024
DEPTH = 1
GLA_HEADS = 4
GLA_DK = 64
GLA_DV = 128
GLA_WIDTH = GLA_HEADS * GLA_DV
GLA_GATE_RANK = 16
GLA_GATE_NORM = 16.0
GDN_HEADS = 4
GDN_DK = 128
GDN_DV = 128
GDN_WIDTH = GDN_HEADS * GDN_DV
CONV_W = 4
CHUNK = 64
N_GROUPS = 8
EXPERTS_PER_GROUP = 8
N_EXPERTS = N_GROUPS * EXPERTS_PER_GROUP
TOP_K = 2
D_EXPERT = 512
LN_EPS = 1e-5
RMS_EPS = 1e-6
ALPHA_DN = (2.0 * DEPTH) ** 0.25

MAIN_W = 2 * GLA_HEADS * GLA_DK + 2 * GLA_WIDTH + 4 * GDN_WIDTH
AUX_W = 128
AUX_LRA = 0
AUX_BB = GLA_GATE_RANK
AUX_AB = GLA_GATE_RANK + GDN_HEADS
SUB = 16
VMEM_LIMIT = 56 * 1024 * 1024


def _cparams(*sem):
    return pltpu.CompilerParams(dimension_semantics=sem, vmem_limit_bytes=VMEM_LIMIT)


def _sigmoid(x):
    return 1.0 / (1.0 + jnp.exp(-x))


def _silu(x):
    return x * _sigmoid(x)


def _log_sigmoid(x):
    return jnp.minimum(x, 0.0) - jnp.log(1.0 + jnp.exp(-jnp.abs(x)))


def _softplus(x):
    return jnp.maximum(x, 0.0) + jnp.log(1.0 + jnp.exp(-jnp.abs(x)))


def _in_proj_kernel(x_ref, wm_ref, wa_ref, main_ref, aux_ref, *, col_block):
    xb = x_ref[...].astype(BF16)
    for j in range(MAIN_W // col_block):
        sl = slice(j * col_block, (j + 1) * col_block)
        main_ref[:, sl] = jnp.dot(xb, wm_ref[:, sl], preferred_element_type=F32).astype(BF16)
    aux_ref[...] = jnp.dot(xb, wa_ref[...], preferred_element_type=F32)


def _in_proj(x2d, w_main, w_aux, *, tm=512, col_block=512):
    n = x2d.shape[0]
    return pl.pallas_call(
        functools.partial(_in_proj_kernel, col_block=col_block),
        grid=(n // tm,),
        in_specs=[pl.BlockSpec((tm, D_MODEL), lambda i: (i, 0)),
                  pl.BlockSpec((D_MODEL, MAIN_W), lambda i: (0, 0)),
                  pl.BlockSpec((D_MODEL, AUX_W), lambda i: (0, 0))],
        out_specs=[pl.BlockSpec((tm, MAIN_W), lambda i: (i, 0)),
                   pl.BlockSpec((tm, AUX_W), lambda i: (i, 0))],
        out_shape=[jax.ShapeDtypeStruct((n, MAIN_W), BF16),
                   jax.ShapeDtypeStruct((n, AUX_W), F32)],
        compiler_params=_cparams("parallel"),
        name="in_proj",
    )(x2d, w_main, w_aux)


def _split_w_in(w_in):
    sizes = (GLA_HEADS * GLA_DK, GLA_HEADS * GLA_DK, GLA_WIDTH, GLA_WIDTH, GLA_GATE_RANK,
             GDN_HEADS * GDN_DK, GDN_HEADS * GDN_DK, GDN_WIDTH, GDN_WIDTH, GDN_HEADS, GDN_HEADS)
    offs = np.cumsum((0,) + sizes)
    seg = [w_in[:, offs[i]:offs[i + 1]] for i in range(len(sizes))]
    qa, ka, va, ra, lra, qb, kb, vb, zb, bb, ab = seg
    w_main = jnp.concatenate([qa, ka, va, ra, qb, kb, vb, zb], axis=1).astype(BF16)
    pad = jnp.zeros((w_in.shape[0], AUX_W - GLA_GATE_RANK - 2 * GDN_HEADS), w_in.dtype)
    w_aux = jnp.concatenate([lra, bb, ab, pad], axis=1).astype(BF16)
    return w_main, w_aux


_NT = (((1,), (1,)), ((), ()))
_TN = (((0,), (0,)), ((), ()))


def _gla_kernel(q_ref, k_ref, v_ref, r_ref, aux_ref, wgk_ref, bgk_ref, g_ref, o_ref, st_ref, *, ts):
    c = CHUNK
    nsub = c // SUB

    @pl.when(pl.program_id(1) == 0)
    def _():
        st_ref[...] = jnp.zeros_like(st_ref)

    pre = jnp.dot(aux_ref[...], wgk_ref[...], precision=HIGHEST, preferred_element_type=F32) + bgk_ref[...]
    gk_all = _log_sigmoid(pre) * (1.0 / GLA_GATE_NORM)

    ri = lax.broadcasted_iota(jnp.int32, (c, c), 0)
    ci = lax.broadcasted_iota(jnp.int32, (c, c), 1)
    tri = (ci <= ri).astype(F32)
    row128 = lax.broadcasted_iota(jnp.int32, (c, 128), 0)
    lane128 = lax.broadcasted_iota(jnp.int32, (1, 128), 1)
    lane_masks = [(lane128 // GLA_DK) == hh for hh in range(2)]
    sub_valid = [row128 < SUB * (i + 1) for i in range(nsub)]
    ar = lax.broadcasted_iota(jnp.int32, (c, nsub * c), 0)
    ac = lax.broadcasted_iota(jnp.int32, (c, nsub * c), 1)
    amask = ((ac // c) == (ar // SUB)) & ((ac % c) <= ar)
    scale = GLA_DK ** -0.5
    gnorm = g_ref[...]

    nch = ts // c
    rows = [slice(ch * c, (ch + 1) * c) for ch in range(nch)]
    b_all = [jnp.dot(tri, gk_all[r], precision=HIGHEST, preferred_element_type=F32) for r in rows]

    cp = [(ch, p) for ch in range(nch) for p in range(2)]
    qt, qd, kd, kstack, dlast = {}, {}, {}, {}, {}
    for ch, p in cp:
        lanes = slice(128 * p, 128 * (p + 1))
        bp = b_all[ch][:, lanes]
        qp = q_ref[rows[ch], lanes].astype(F32) * scale
        kp = k_ref[rows[ch], lanes].astype(F32)
        c_row = jnp.concatenate(
            [jnp.broadcast_to(bp[SUB * i:SUB * i + 1, :], (SUB, 128)) for i in range(nsub)], axis=0)
        qt[ch, p] = (qp * jnp.exp(bp - c_row)).astype(BF16)
        qd[ch, p] = qp * jnp.exp(bp)
        b_last = bp[c - 1:c, :]
        dlast[ch, p] = jnp.exp(b_last)
        kd[ch, p] = kp * jnp.exp(b_last - bp)
        kts = []
        for i in range(nsub):
            e = jnp.exp(jnp.where(sub_valid[i], bp[SUB * i:SUB * i + 1, :] - bp, 0.0))
            kts.append(jnp.where(sub_valid[i], kp * e, 0.0))
        kstack[ch, p] = jnp.concatenate(kts, axis=0)

    cph = [(ch, p, hh) for ch, p in cp for hh in range(2)]
    vh = {(ch, p, hh): v_ref[rows[ch], 128 * (2 * p + hh):128 * (2 * p + hh + 1)] for ch, p, hh in cph}
    r_mat = {(ch, p, hh): lax.dot_general(qt[ch, p], jnp.where(lane_masks[hh], kstack[ch, p], 0.0).astype(BF16),
                                          _NT, preferred_element_type=F32) for ch, p, hh in cph}
    o_intra = {k: jnp.dot(jnp.where(amask, r_mat[k], 0.0).astype(BF16), jnp.concatenate([vh[k]] * nsub, axis=0),
                          preferred_element_type=F32) for k in cph}
    upd = {(ch, p, hh): lax.dot_general(vh[ch, p, hh], jnp.where(lane_masks[hh], kd[ch, p], 0.0).astype(BF16), _TN,
                                        preferred_element_type=F32) for ch, p, hh in cph}

    st = [st_ref[p] for p in range(2)]
    for ch in range(nch):
        st_b = [s.astype(BF16) for s in st]
        o_inter = {(p, hh): lax.dot_general(jnp.where(lane_masks[hh], qd[ch, p], 0.0).astype(BF16), st_b[p], _NT,
                                            preferred_element_type=F32) for p in range(2) for hh in range(2)}
        st = [st[p] * dlast[ch, p] + upd[ch, p, 0] + upd[ch, p, 1] for p in range(2)]
        for p in range(2):
            for hh in range(2):
                h = 2 * p + hh
                o = o_intra[ch, p, hh] + o_inter[p, hh]
                o = o * lax.rsqrt(jnp.mean(o * o, axis=-1, keepdims=True) + RMS_EPS) * gnorm
                gate = _silu(r_ref[rows[ch], 128 * h:128 * (h + 1)].astype(F32))
                o_ref[rows[ch], 128 * h:128 * (h + 1)] = (o * gate).astype(BF16)
    for p in range(2):
        st_ref[p] = st[p]


def _gla(main, aux, wgk_pad, b_gk, gla_norm_g, *, bsz, seq, ts=256):
    nt = seq // ts
    row = lambda b, t: b * nt + t
    return pl.pallas_call(
        functools.partial(_gla_kernel, ts=ts),
        grid=(bsz, nt),
        in_specs=[pl.BlockSpec((ts, 256), lambda b, t: (row(b, t), 0)),
                  pl.BlockSpec((ts, 256), lambda b, t: (row(b, t), 1)),
                  pl.BlockSpec((ts, 512), lambda b, t: (row(b, t), 1)),
                  pl.BlockSpec((ts, 512), lambda b, t: (row(b, t), 2)),
                  pl.BlockSpec((ts, AUX_W), lambda b, t: (row(b, t), 0)),
                  pl.BlockSpec((AUX_W, 256), lambda b, t: (0, 0)),
                  pl.BlockSpec((1, 256), lambda b, t: (0, 0)),
                  pl.BlockSpec((1, 128), lambda b, t: (0, 0))],
        out_specs=pl.BlockSpec((ts, GLA_WIDTH), lambda b, t: (row(b, t), 0)),
        out_shape=jax.ShapeDtypeStruct((bsz * seq, GLA_WIDTH), BF16),
        scratch_shapes=[pltpu.VMEM((2, 128, 128), F32)],
        compiler_params=_cparams("parallel", "arbitrary"),
        name="gla",
    )(main, main, main, main, aux, wgk_pad, b_gk, gla_norm_g)


CONV_PAD = 8


def _mm(a, b):
    return jnp.dot(a.astype(BF16), b.astype(BF16), preferred_element_type=F32)


def _gdn_kernel(q_ref, k_ref, v_ref, z_ref, aux_ref, cw_ref, alog_ref, dtb_ref, g_ref, o_ref,
                st_ref, xe_ref, *, ts):
    c = CHUNK
    nch = ts // c
    hw = GDN_HEADS * GDN_DK

    @pl.when(pl.program_id(1) == 0)
    def _():
        st_ref[...] = jnp.zeros_like(st_ref)
        xe_ref[0:CONV_PAD, :] = jnp.zeros((CONV_PAD, 3 * hw), F32)

    xe_ref[CONV_PAD:, 0:hw] = q_ref[...].astype(F32)
    xe_ref[CONV_PAD:, hw:2 * hw] = k_ref[...].astype(F32)
    xe_ref[CONV_PAD:, 2 * hw:3 * hw] = v_ref[...].astype(F32)

    aux = aux_ref[...]
    beta_full = _sigmoid(aux)
    g_full = -jnp.exp(alog_ref[...]) * _softplus(aux + dtb_ref[...])
    ri = lax.broadcasted_iota(jnp.int32, (ts, ts), 0)
    ci = lax.broadcasted_iota(jnp.int32, (ts, ts), 1)
    cum = (((ri // c) == (ci // c)) & (ci <= ri)).astype(F32)
    gam_full = jnp.dot(cum, g_full, precision=HIGHEST, preferred_element_type=F32)
    gam_t = gam_full.T
    gnorm = g_ref[...]

    row = lax.broadcasted_iota(jnp.int32, (c, 2 * c), 0)
    lane = lax.broadcasted_iota(jnp.int32, (c, 2 * c), 1)
    left = lane < c
    col = lane % c
    causal = col <= row
    strict = col < row
    same32 = (row // 32) == (col // 32)
    same16 = (row // 16) == (col // 16)
    lvl0 = same16 & strict
    lvl1 = same32 & jnp.logical_not(same16) & strict
    lvl2 = jnp.logical_not(same32) & strict

    def pack_rows(x2):
        return jnp.where(left, x2[0:c], x2[c:2 * c])

    def pack_cols(x1):
        return jnp.where(left, x1[0:c], x1[c:2 * c])

    def halves(x):
        return jnp.where(left, x, 0.0).astype(BF16), jnp.where(left, 0.0, x).astype(BF16)

    def mmp(x, y):
        return jnp.dot(x.astype(BF16), jnp.concatenate(halves(y), axis=0), preferred_element_type=F32)

    def conv(col):
        acc = jnp.zeros((ts, 128), F32)
        for i in range(CONV_W):
            lo = CONV_PAD - (CONV_W - 1) + i
            acc = acc + xe_ref[lo:lo + ts, col:col + 128] * cw_ref[i:i + 1, col:col + 128]
        return _silu(acc)

    heads = range(GDN_HEADS)
    npair = nch // 2
    items = [(h, pr) for h in heads for pr in range(npair)]
    prow = [slice(2 * c * pr, 2 * c * (pr + 1)) for pr in range(npair)]

    qn, kn, beta, gam, gam_r, kb, rhs, rhs_b, qd, kd = ([None] * GDN_HEADS for _ in range(10))
    for h in heads:
        qh = conv(128 * h)
        kh = conv(hw + 128 * h)
        vh = conv(2 * hw + 128 * h)
        qn[h] = qh * lax.rsqrt(jnp.sum(qh * qh, axis=-1, keepdims=True) + RMS_EPS) * (GDN_DK ** -0.5)
        kn[h] = kh * lax.rsqrt(jnp.sum(kh * kh, axis=-1, keepdims=True) + RMS_EPS)
        beta[h] = beta_full[:, AUX_BB + h:AUX_BB + h + 1]
        gam[h] = gam_full[:, AUX_AB + h:AUX_AB + h + 1]
        gam_r[h] = gam_t[AUX_AB + h:AUX_AB + h + 1, :]
        egam = jnp.exp(gam[h])
        kb[h] = kn[h].astype(BF16)
        rhs[h] = jnp.concatenate([beta[h] * vh, beta[h] * egam * kn[h]], axis=1)
        rhs_b[h] = rhs[h].astype(BF16)
        qd[h] = (qn[h] * egam).astype(BF16)
        gl_rows = jnp.concatenate(
            [jnp.broadcast_to(gam[h][c * (j + 1) - 1:c * (j + 1), :], (c, 1)) for j in range(nch)], axis=0)
        kd[h] = (kn[h] * jnp.exp(gl_rows - gam[h])).astype(BF16)

    kk = [pack_rows(lax.dot_general(kb[h][prow[pr]], kb[h][prow[pr]], _NT, preferred_element_type=F32))
          for h, pr in items]
    qk = [pack_rows(lax.dot_general(qn[h][prow[pr]].astype(BF16), kb[h][prow[pr]], _NT,
                                    preferred_element_type=F32)) for h, pr in items]
    dec = [jnp.exp(jnp.where(causal, pack_cols(gam[h][prow[pr]]) - gam_r[h][:, prow[pr]], 0.0)) for h, pr in items]
    a = [pack_cols(beta[h][prow[pr]]) * kk[i] * dec[i] for i, (h, pr) in enumerate(items)]
    attn = [halves(jnp.where(causal, qk[i] * dec[i], 0.0)) for i in range(len(items))]

    x1 = [jnp.where(lvl0, -ai, 0.0) for ai in a]
    x2 = [mmp(x, x) for x in x1]
    y = [mmp(x, xx) for x, xx in zip(x1, x2)]
    p = [x + xx + yy for x, xx, yy in zip(x1, x2, y)]
    xk = x2
    for _ in range(2):
        xk = [mmp(x, x) for x in xk]
        y = [mmp(pp, x) for pp, x in zip(p, xk)]
        p = [pp + x + yy for pp, x, yy in zip(p, xk, y)]
    for lvl in (lvl1, lvl2):
        low = [jnp.where(lvl, ai, 0.0) for ai in a]
        y = [mmp(pp, lo) for pp, lo in zip(p, low)]
        zz = [lo + yy for lo, yy in zip(low, y)]
        y = [mmp(z, pp) for z, pp in zip(zz, p)]
        p = [pp - z - yy for pp, z, yy in zip(p, zz, y)]

    us = [[None] * nch for _ in heads]
    ws = [[None] * nch for _ in heads]
    for i, (h, pr) in enumerate(items):
        for half, p_half in enumerate(halves(p[i])):
            j = 2 * pr + half
            uw = rhs[h][c * j:c * (j + 1)] + jnp.dot(p_half, rhs_b[h][prow[pr]], preferred_element_type=F32)
            us[h][j] = uw[:, :GDN_DV]
            ws[h][j] = uw[:, GDN_DV:].astype(BF16)

    s = [st_ref[h] for h in heads]
    vprev = [None] * GDN_HEADS
    for j in range(nch):
        rows = slice(c * j, c * (j + 1))
        sb = [s[h].astype(BF16) for h in heads]
        vn16 = [(us[h][j] - jnp.dot(ws[h][j], sb[h], preferred_element_type=F32)).astype(BF16) for h in heads]
        o_inter = [jnp.dot(qd[h][rows], sb[h], preferred_element_type=F32) for h in heads]
        for h in heads:
            vpair = jnp.concatenate([vn16[h], jnp.zeros_like(vn16[h])] if j % 2 == 0 else [vprev[h], vn16[h]], axis=0)
            o = o_inter[h] + jnp.dot(attn[h * npair + j // 2][j % 2], vpair, preferred_element_type=F32)
            s[h] = jnp.exp(gam[h][c * (j + 1) - 1:c * (j + 1), :]) * s[h] + lax.dot_general(
                kd[h][rows], vn16[h], _TN, preferred_element_type=F32)
            o = o * lax.rsqrt(jnp.mean(o * o, axis=-1, keepdims=True) + RMS_EPS) * gnorm
            gate = _silu(z_ref[rows, 128 * h:128 * (h + 1)].astype(F32))
            o_ref[rows, 128 * h:128 * (h + 1)] = (o * gate).astype(BF16)
        vprev = vn16
    for h in heads:
        st_ref[h] = s[h]

    xe_ref[0:CONV_PAD, :] = xe_ref[ts:ts + CONV_PAD, :]


def _gdn(main, aux, conv_w, alog_pad, dtb_pad, gdn_norm_g, *, bsz, seq, ts=256):
    nt = seq // ts
    row = lambda b, t: b * nt + t
    hw = GDN_HEADS * GDN_DK
    return pl.pallas_call(
        functools.partial(_gdn_kernel, ts=ts),
        grid=(bsz, nt),
        in_specs=[pl.BlockSpec((ts, hw), lambda b, t: (row(b, t), 3)),
                  pl.BlockSpec((ts, hw), lambda b, t: (row(b, t), 4)),
                  pl.BlockSpec((ts, hw), lambda b, t: (row(b, t), 5)),
                  pl.BlockSpec((ts, hw), lambda b, t: (row(b, t), 6)),
                  pl.BlockSpec((ts, AUX_W), lambda b, t: (row(b, t), 0)),
                  pl.BlockSpec((CONV_W, 3 * hw), lambda b, t: (0, 0)),
                  pl.BlockSpec((1, AUX_W), lambda b, t: (0, 0)),
                  pl.BlockSpec((1, AUX_W), lambda b, t: (0, 0)),
                  pl.BlockSpec((1, GDN_DV), lambda b, t: (0, 0))],
        out_specs=pl.BlockSpec((ts, GDN_WIDTH), lambda b, t: (row(b, t), 0)),
        out_shape=jax.ShapeDtypeStruct((bsz * seq, GDN_WIDTH), BF16),
        scratch_shapes=[pltpu.VMEM((GDN_HEADS, GDN_DK, GDN_DV), F32),
                        pltpu.VMEM((ts + CONV_PAD, 3 * hw), F32)],
        compiler_params=_cparams("parallel", "arbitrary"),
        name="gdn",
    )(main, main, main, main, aux, conv_w, alog_pad, dtb_pad, gdn_norm_g)


def _pad_aux_row(v, offset):
    return jnp.zeros((1, AUX_W), F32).at[0, offset:offset + v.shape[0]].set(v.astype(F32))


ROUTE_W = 128
R_E1, R_E2, R_RANK1, R_RANK2, R_G1, R_G2 = range(6)
ROUTER_GROUP_COL = N_EXPERTS


def _layer_norm(h, g, b):
    mu = jnp.mean(h, axis=-1, keepdims=True)
    hc = h - mu
    var = jnp.mean(hc * hc, axis=-1, keepdims=True)
    return hc * lax.rsqrt(var + LN_EPS) * g + b


def _pack_bf16_pairs(x):
    w = x.shape[1] // 2
    lo = lax.bitcast_convert_type(x[:, :w].astype(BF16).astype(F32), jnp.uint32)
    hi = lax.bitcast_convert_type(x[:, w:].astype(BF16).astype(F32), jnp.uint32)
    return (lo >> 16) | (hi & jnp.uint32(0xFFFF0000))


def _unpack_bf16_pairs(p):
    lo = lax.bitcast_convert_type(p << 16, F32)
    hi = lax.bitcast_convert_type(p & jnp.uint32(0xFFFF0000), F32)
    return lo, hi


def _post_mix_kernel(oa_ref, ob_ref, x_ref, wo_ref, g_ref, b_ref, wr_ref, br_ref,
                     x1_ref, x1p_ref, route_ref, cnt_ref, carry_ref, *, tm):
    @pl.when(pl.program_id(0) == 0)
    def _():
        carry_ref[...] = jnp.zeros_like(carry_ref)

    y = jnp.dot(oa_ref[...], wo_ref[0:GLA_WIDTH, :], preferred_element_type=F32)
    y = y + jnp.dot(ob_ref[...], wo_ref[GLA_WIDTH:, :], preferred_element_type=F32)
    x1 = _layer_norm(ALPHA_DN * x_ref[...] + y, g_ref[...], b_ref[...])
    x1_ref[...] = x1
    x1p_ref[...] = _pack_bf16_pairs(x1)

    logits = jnp.dot(x1, wr_ref[...], precision=HIGHEST, preferred_element_type=F32) + br_ref[...]
    lane = lax.broadcasted_iota(jnp.int32, (tm, ROUTE_W), 1)
    big = jnp.int32(1 << 20)
    neg = jnp.float32(-jnp.inf)

    def first_argmax(vals):
        m = jnp.max(vals, axis=-1, keepdims=True)
        idx = jnp.min(jnp.where(vals == m, lane, big), axis=-1, keepdims=True)
        return m, idx

    is_group = (lane >= ROUTER_GROUP_COL) & (lane < ROUTER_GROUP_COL + N_GROUPS)
    gl = jnp.where(is_group, logits, neg)
    gmax, gidx = first_argmax(gl)
    p_group = 1.0 / jnp.sum(jnp.exp(gl - gmax), axis=-1, keepdims=True)
    g_sel = gidx - ROUTER_GROUP_COL
    el = jnp.where((lane // EXPERTS_PER_GROUP) == g_sel, logits, neg)
    t1, e1 = first_argmax(el)
    t2, e2 = first_argmax(jnp.where(lane == e1, neg, el))
    ex = jnp.exp(t2 - t1)
    w1 = 1.0 / (1.0 + ex)
    gate1 = p_group * w1
    gate2 = p_group * (ex * w1)

    oh1 = lane == e1
    oh2 = lane == e2
    ri = lax.broadcasted_iota(jnp.int32, (tm, tm), 0)
    ci = lax.broadcasted_iota(jnp.int32, (tm, tm), 1)
    lstrict = (ci < ri).astype(BF16)
    oh1f = oh1.astype(F32)
    oh2f = oh2.astype(F32)
    c1 = jnp.dot(lstrict, oh1f.astype(BF16), preferred_element_type=F32)
    c2 = jnp.dot(lstrict, oh2f.astype(BF16), preferred_element_type=F32)
    carry = carry_ref[...]
    tot1 = jnp.sum(oh1f, axis=0, keepdims=True)
    tot2 = jnp.sum(oh2f, axis=0, keepdims=True)
    rank1 = jnp.sum(jnp.where(oh1, c1 + carry, 0.0), axis=-1, keepdims=True)
    rank2 = jnp.sum(jnp.where(oh2, c2 + (carry + tot1), 0.0), axis=-1, keepdims=True)
    new_carry = carry + tot1 + tot2
    carry_ref[...] = new_carry
    cnt_ref[...] = new_carry

    cols = (e1.astype(F32), e2.astype(F32), rank1, rank2, gate1, gate2)
    route = jnp.zeros((tm, ROUTE_W), F32)
    for j, col in enumerate(cols):
        route = jnp.where(lane == j, col, route)
    route_ref[...] = route


def _post_mix(o_a, o_b, x2d, w_out_b, ln_g, ln_b, w_router, b_router, *, tm=512):
    n = x2d.shape[0]
    return pl.pallas_call(
        functools.partial(_post_mix_kernel, tm=tm),
        grid=(n // tm,),
        in_specs=[pl.BlockSpec((tm, GLA_WIDTH), lambda i: (i, 0)),
                  pl.BlockSpec((tm, GDN_WIDTH), lambda i: (i, 0)),
                  pl.BlockSpec((tm, D_MODEL), lambda i: (i, 0)),
                  pl.BlockSpec((GLA_WIDTH + GDN_WIDTH, D_MODEL), lambda i: (0, 0)),
                  pl.BlockSpec((1, D_MODEL), lambda i: (0, 0)),
                  pl.BlockSpec((1, D_MODEL), lambda i: (0, 0)),
                  pl.BlockSpec((D_MODEL, ROUTE_W), lambda i: (0, 0)),
                  pl.BlockSpec((1, ROUTE_W), lambda i: (0, 0))],
        out_specs=[pl.BlockSpec((tm, D_MODEL), lambda i: (i, 0)),
                   pl.BlockSpec((tm, D_MODEL // 2), lambda i: (i, 0)),
                   pl.BlockSpec((tm, ROUTE_W), lambda i: (i, 0)),
                   pl.BlockSpec((1, ROUTE_W), lambda i: (0, 0))],
        out_shape=[jax.ShapeDtypeStruct((n, D_MODEL), F32),
                   jax.ShapeDtypeStruct((n, D_MODEL // 2), jnp.uint32),
                   jax.ShapeDtypeStruct((n, ROUTE_W), F32),
                   jax.ShapeDtypeStruct((1, ROUTE_W), F32)],
        scratch_shapes=[pltpu.VMEM((1, ROUTE_W), F32)],
        compiler_params=_cparams("arbitrary"),
        name="post_mix",
    )(o_a, o_b, x2d, w_out_b, ln_g, ln_b, w_router, b_router)


EXPERT_BLOCK = 256


HALF = D_MODEL // 2
ROW_UNROLL = 8


def _plan_kernel(route_ref, cnt_ref, dest_ref, *, tm, tb):
    counts = cnt_ref[...].astype(jnp.int32)
    padded = ((counts + (tb - 1)) & jnp.int32(-tb)).astype(F32)
    ri = lax.broadcasted_iota(jnp.int32, (ROUTE_W, ROUTE_W), 0)
    ci = lax.broadcasted_iota(jnp.int32, (ROUTE_W, ROUTE_W), 1)
    pstart = jnp.dot(jnp.broadcast_to(padded, (8, ROUTE_W)), (ri < ci).astype(F32),
                     precision=HIGHEST, preferred_element_type=F32)[0:1, :]
    route = route_ref[...]
    lane = lax.broadcasted_iota(jnp.int32, (tm, ROUTE_W), 1).astype(F32)
    out = jnp.zeros((tm, ROUTE_W), F32)
    for k, (ce, cr) in enumerate(((R_E1, R_RANK1), (R_E2, R_RANK2))):
        off = jnp.sum(jnp.where(lane == route[:, ce:ce + 1], pstart, 0.0), axis=-1, keepdims=True)
        out = jnp.where(lane == float(k), off + route[:, cr:cr + 1], out)
    dest_ref[...] = out.T[0:8, :].astype(jnp.int32)


def _plan(route, counts, *, tb, tm=512):
    n = route.shape[0]
    return pl.pallas_call(
        functools.partial(_plan_kernel, tm=tm, tb=tb),
        grid=(n // tm,),
        in_specs=[pl.BlockSpec((tm, ROUTE_W), lambda i: (i, 0)),
                  pl.BlockSpec((1, ROUTE_W), lambda i: (0, 0))],
        out_specs=pl.BlockSpec((8, tm), lambda i: (0, i)),
        out_shape=jax.ShapeDtypeStruct((8, n), jnp.int32),
        compiler_params=_cparams("parallel"),
        name="plan",
    )(route, counts)


def _dispatch_kernel(d1_ref, d2_ref, xp_ref, xs_init_ref, xs_ref, sem, *, tm):
    del xs_init_ref
    base = pl.program_id(0) * tm

    def body(j, carry):
        for u in range(ROW_UNROLL):
            r = j * ROW_UNROLL + u
            for d_ref in (d1_ref, d2_ref):
                pltpu.make_async_copy(xp_ref.at[pl.ds(r, 1), :], xs_ref.at[pl.ds(d_ref[base + r], 1), :], sem).start()
        return carry
    lax.fori_loop(0, tm // ROW_UNROLL, body, 0)
    for _ in range(TOP_K):
        pltpu.make_async_copy(xp_ref, xs_ref.at[pl.ds(0, tm), :], sem).wait()


def _dispatch(dest1, dest2, x1p, n_rows, *, tm=512):
    n = x1p.shape[0]
    grid_spec = pltpu.PrefetchScalarGridSpec(
        num_scalar_prefetch=2,
        grid=(n // tm,),
        in_specs=[pl.BlockSpec((tm, HALF), lambda i, a, b: (i, 0)),
                  pl.BlockSpec(memory_space=pl.ANY)],
        out_specs=pl.BlockSpec(memory_space=pl.ANY),
        scratch_shapes=[pltpu.SemaphoreType.DMA(())],
    )
    return pl.pallas_call(
        functools.partial(_dispatch_kernel, tm=tm),
        grid_spec=grid_spec,
        out_shape=jax.ShapeDtypeStruct((n_rows, HALF), jnp.uint32),
        input_output_aliases={3: 0},
        compiler_params=_cparams("arbitrary"),
        name="dispatch",
    )(dest1, dest2, x1p, jnp.zeros((n_rows, HALF), jnp.uint32))


def _expert_kernel(blk_e_ref, first_ref, nvalid_ref, xs_ref, wg_ref, wu_ref, wd_ref, y_ref, wgb, wub, wdb):
    i = pl.program_id(0)

    @pl.when(i < nvalid_ref[0])
    def _():
        @pl.when(first_ref[i] == 1)
        def _():
            wgb[...] = wg_ref[...].astype(BF16)
            wub[...] = wu_ref[...].astype(BF16)
            wdb[...] = wd_ref[...].astype(BF16)

        lo, hi = _unpack_bf16_pairs(xs_ref[...])
        lo = lo.astype(BF16)
        hi = hi.astype(BF16)
        hg = jnp.dot(lo, wgb[0:HALF, :], preferred_element_type=F32)
        hg = hg + jnp.dot(hi, wgb[HALF:, :], preferred_element_type=F32)
        hu = jnp.dot(lo, wub[0:HALF, :], preferred_element_type=F32)
        hu = hu + jnp.dot(hi, wub[HALF:, :], preferred_element_type=F32)
        hid = (_silu(hg) * hu).astype(BF16)
        y_ref[...] = _pack_bf16_pairs(jnp.dot(hid, wdb[...], preferred_element_type=F32))

    @pl.when(i >= nvalid_ref[0])
    def _():
        y_ref[...] = jnp.zeros_like(y_ref)


def _experts(blk_e, first, nvalid, xs, w_gate, w_up, w_down, *, tb=EXPERT_BLOCK):
    nb = blk_e.shape[0]
    w_map = lambda i, be, fr, nv: (be[i], 0, 0)
    grid_spec = pltpu.PrefetchScalarGridSpec(
        num_scalar_prefetch=3,
        grid=(nb,),
        in_specs=[pl.BlockSpec((tb, HALF), lambda i, be, fr, nv: (jnp.minimum(i, nv[0] - 1), 0)),
                  pl.BlockSpec((None, D_MODEL, D_EXPERT), w_map),
                  pl.BlockSpec((None, D_MODEL, D_EXPERT), w_map),
                  pl.BlockSpec((None, D_EXPERT, D_MODEL), w_map)],
        out_specs=pl.BlockSpec((tb, HALF), lambda i, be, fr, nv: (i, 0)),
        scratch_shapes=[pltpu.VMEM((D_MODEL, D_EXPERT), BF16),
                        pltpu.VMEM((D_MODEL, D_EXPERT), BF16),
                        pltpu.VMEM((D_EXPERT, D_MODEL), BF16)],
    )
    return pl.pallas_call(
        _expert_kernel,
        grid_spec=grid_spec,
        out_shape=jax.ShapeDtypeStruct((nb * tb, HALF), jnp.uint32),
        compiler_params=_cparams("arbitrary"),
        name="experts",
    )(blk_e, first, nvalid, xs, w_gate, w_up, w_down)


def _combine_kernel(d1_ref, d2_ref, y_hbm, x1_ref, route_ref, g_ref, b_ref, o_ref, ybuf, sem, *, tt):
    i = pl.program_id(0)
    nsteps = pl.num_programs(0)
    slot = i % 2

    def issue(tile, slot_):
        def body(j, carry):
            for u in range(ROW_UNROLL):
                r = j * ROW_UNROLL + u
                for k, d_ref in enumerate((d1_ref, d2_ref)):
                    pltpu.make_async_copy(y_hbm.at[pl.ds(d_ref[tile * tt + r], 1), :],
                                          ybuf.at[slot_, k, pl.ds(r, 1), :], sem.at[slot_]).start()
            return carry
        lax.fori_loop(0, tt // ROW_UNROLL, body, 0)

    @pl.when(i == 0)
    def _():
        issue(0, 0)

    @pl.when(i + 1 < nsteps)
    def _():
        issue(i + 1, 1 - slot)

    for k in range(TOP_K):
        pltpu.make_async_copy(y_hbm.at[pl.ds(0, tt), :], ybuf.at[slot, k], sem.at[slot]).wait()
    route = route_ref[...]
    g1 = route[:, R_G1:R_G1 + 1]
    g2 = route[:, R_G2:R_G2 + 1]
    lo1, hi1 = _unpack_bf16_pairs(ybuf[slot, 0])
    lo2, hi2 = _unpack_bf16_pairs(ybuf[slot, 1])
    h_lo = ALPHA_DN * x1_ref[:, 0:HALF] + (g1 * lo1 + g2 * lo2)
    h_hi = ALPHA_DN * x1_ref[:, HALF:] + (g1 * hi1 + g2 * hi2)
    mu = (jnp.sum(h_lo, axis=-1, keepdims=True) + jnp.sum(h_hi, axis=-1, keepdims=True)) * (1.0 / D_MODEL)
    c_lo = h_lo - mu
    c_hi = h_hi - mu
    var = (jnp.sum(c_lo * c_lo, axis=-1, keepdims=True) + jnp.sum(c_hi * c_hi, axis=-1, keepdims=True)) * (1.0 / D_MODEL)
    inv = lax.rsqrt(var + LN_EPS)
    o_ref[:, 0:HALF] = c_lo * inv * g_ref[:, 0:HALF] + b_ref[:, 0:HALF]
    o_ref[:, HALF:] = c_hi * inv * g_ref[:, HALF:] + b_ref[:, HALF:]


def _combine(dest1, dest2, ybuf, x1, route, ln_g, ln_b, *, tt=256):
    n = x1.shape[0]
    grid_spec = pltpu.PrefetchScalarGridSpec(
        num_scalar_prefetch=2,
        grid=(n // tt,),
        in_specs=[pl.BlockSpec(memory_space=pl.ANY),
                  pl.BlockSpec((tt, D_MODEL), lambda i, a, b: (i, 0)),
                  pl.BlockSpec((tt, ROUTE_W), lambda i, a, b: (i, 0)),
                  pl.BlockSpec((1, D_MODEL), lambda i, a, b: (0, 0)),
                  pl.BlockSpec((1, D_MODEL), lambda i, a, b: (0, 0))],
        out_specs=pl.BlockSpec((tt, D_MODEL), lambda i, a, b: (i, 0)),
        scratch_shapes=[pltpu.VMEM((2, TOP_K, tt, HALF), jnp.uint32),
                        pltpu.SemaphoreType.DMA((2,))],
    )
    return pl.pallas_call(
        functools.partial(_combine_kernel, tt=tt),
        grid_spec=grid_spec,
        out_shape=jax.ShapeDtypeStruct((n, D_MODEL), F32),
        compiler_params=_cparams("arbitrary"),
        name="combine",
    )(dest1, dest2, ybuf, x1, route, ln_g, ln_b)


def _block_plan(counts_row, n, tb):
    counts = counts_row[0, :N_EXPERTS].astype(jnp.int32)
    pends = jnp.cumsum((counts + tb - 1) // tb * tb)
    nb = (n * TOP_K) // tb + N_EXPERTS
    nvalid = (pends[-1] // tb).astype(jnp.int32)
    blk = jnp.arange(nb, dtype=jnp.int32)
    blk_e = jnp.minimum(jnp.searchsorted(pends, blk * tb, side='right'), N_EXPERTS - 1).astype(jnp.int32)
    blk_e = jnp.where(blk < nvalid, blk_e, blk_e[jnp.maximum(nvalid - 1, 0)])
    first = jnp.concatenate([jnp.ones((1,), jnp.int32), (blk_e[1:] != blk_e[:-1]).astype(jnp.int32)])
    return blk_e, first, nvalid.reshape(1)


def kernel(x, w_in, w_gk_up, b_gk, conv_w, a_log, dt_bias, gla_norm_g, gdn_norm_g, w_out, ln1_g, ln1_b, w_router_group, b_router_group, w_router_expert, b_router_expert, w_gate, w_up, w_down, ln2_g, ln2_b):
    bsz, seq, d = x.shape
    n = bsz * seq
    x2d = x.reshape(n, d)
    w_main, w_aux = _split_w_in(w_in[0])
    main, aux = _in_proj(x2d, w_main, w_aux)
    wgk = jnp.zeros((AUX_W, 256), F32).at[:GLA_GATE_RANK].set(w_gk_up[0])
    o_a = _gla(main, aux, wgk, b_gk[0][None], gla_norm_g[0][None], bsz=bsz, seq=seq)
    o_b = _gdn(main, aux, conv_w[0], _pad_aux_row(a_log[0], AUX_AB), _pad_aux_row(dt_bias[0], AUX_AB),
               gdn_norm_g[0][None], bsz=bsz, seq=seq)
    rpad = jnp.zeros((d, ROUTE_W - N_EXPERTS - N_GROUPS), F32)
    w_router = jnp.concatenate([w_router_expert[0], w_router_group[0], rpad], axis=1)
    b_router = jnp.concatenate([b_router_expert[0], b_router_group[0], rpad[0]])[None]
    x1, x1p, route, counts = _post_mix(o_a, o_b, x2d, w_out[0].astype(BF16), ln1_g[0][None], ln1_b[0][None],
                                       w_router, b_router)
    dest = _plan(route, counts, tb=EXPERT_BLOCK)
    blk_e, first, nvalid = _block_plan(counts, n, EXPERT_BLOCK)
    xs = _dispatch(dest[0], dest[1], x1p, blk_e.shape[0] * EXPERT_BLOCK)
    ybuf = _experts(blk_e, first, nvalid, xs, w_gate[0], w_up[0], w_down[0])
    out = _combine(dest[0], dest[1], ybuf, x1, route, ln2_g[0][None], ln2_b[0][None])
    return out.reshape(bsz, seq, d)
```

```python
import functools

import jax
import jax.numpy as jnp
import numpy as np
from jax import lax
from jax.experimental import pallas as pl
from jax.experimental.pallas import tpu as pltpu

F32 = jnp.float32
BF16 = jnp.bfloat16
HIGHEST = lax.Precision.HIGHEST

D_MODEL = 1024
DEPTH = 1
GLA_HEADS = 4
GLA_DK = 64
GLA_DV = 128
GLA_WIDTH = GLA_HEADS * GLA_DV
GLA_GATE_RANK = 16
GLA_GATE_NORM = 16.0
GDN_HEADS = 4
GDN_DK = 128
GDN_DV = 128
GDN_WIDTH = GDN_HEADS * GDN_DV
CONV_W = 4
CHUNK = 64
N_GROUPS = 8
EXPERTS_PER_GROUP = 8
N_EXPERTS = N_GROUPS * EXPERTS_PER_GROUP
TOP_K = 2
D_EXPERT = 512
LN_EPS = 1e-5
RMS_EPS = 1e-6
ALPHA_DN = (2.0 * DEPTH) ** 0.25

MAIN_W = 2 * GLA_HEADS * GLA_DK + 2 * GLA_WIDTH + 4 * GDN_WIDTH
AUX_W = 128
AUX_LRA = 0
AUX_BB = GLA_GATE_RANK
AUX_AB = GLA_GATE_RANK + GDN_HEADS
SUB = 16
VMEM_LIMIT = 56 * 1024 * 1024


def _cparams(*sem):
    return pltpu.CompilerParams(dimension_semantics=sem, vmem_limit_bytes=VMEM_LIMIT)


def _sigmoid(x):
    return 1.0 / (1.0 + jnp.exp(-x))


def _silu(x):
    return x * _sigmoid(x)


def _log_sigmoid(x):
    return jnp.minimum(x, 0.0) - jnp.log(1.0 + jnp.exp(-jnp.abs(x)))


def _softplus(x):
    return jnp.maximum(x, 0.0) + jnp.log(1.0 + jnp.exp(-jnp.abs(x)))


def _in_proj_kernel(x_ref, wm_ref, wa_ref, main_ref, aux_ref, *, col_block):
    xb = x_ref[...].astype(BF16)
    for j in range(MAIN_W // col_block):
        sl = slice(j * col_block, (j + 1) * col_block)
        main_ref[:, sl] = jnp.dot(xb, wm_ref[:, sl], preferred_element_type=F32).astype(BF16)
    aux_ref[...] = jnp.dot(xb, wa_ref[...], preferred_element_type=F32)


def _in_proj(x2d, w_main, w_aux, *, tm=512, col_block=512):
    n = x2d.shape[0]
    return pl.pallas_call(
        functools.partial(_in_proj_kernel, col_block=col_block),
        grid=(n // tm,),
        in_specs=[pl.BlockSpec((tm, D_MODEL), lambda i: (i, 0)),
                  pl.BlockSpec((D_MODEL, MAIN_W), lambda i: (0, 0)),
                  pl.BlockSpec((D_MODEL, AUX_W), lambda i: (0, 0))],
        out_specs=[pl.BlockSpec((tm, MAIN_W), lambda i: (i, 0)),
                   pl.BlockSpec((tm, AUX_W), lambda i: (i, 0))],
        out_shape=[jax.ShapeDtypeStruct((n, MAIN_W), BF16),
                   jax.ShapeDtypeStruct((n, AUX_W), F32)],
        compiler_params=_cparams("parallel"),
        name="in_proj",
    )(x2d, w_main, w_aux)


def _split_w_in(w_in):
    sizes = (GLA_HEADS * GLA_DK, GLA_HEADS * GLA_DK, GLA_WIDTH, GLA_WIDTH, GLA_GATE_RANK,
             GDN_HEADS * GDN_DK, GDN_HEADS * GDN_DK, GDN_WIDTH, GDN_WIDTH, GDN_HEADS, GDN_HEADS)
    offs = np.cumsum((0,) + sizes)
    seg = [w_in[:, offs[i]:offs[i + 1]] for i in range(len(sizes))]
    qa, ka, va, ra, lra, qb, kb, vb, zb, bb, ab = seg
    w_main = jnp.concatenate([qa, ka, va, ra, qb, kb, vb, zb], axis=1).astype(BF16)
    pad = jnp.zeros((w_in.shape[0], AUX_W - GLA_GATE_RANK - 2 * GDN_HEADS), w_in.dtype)
    w_aux = jnp.concatenate([lra, bb, ab, pad], axis=1).astype(BF16)
    return w_main, w_aux


_NT = (((1,), (1,)), ((), ()))
_TN = (((0,), (0,)), ((), ()))


def _gla_kernel(q_ref, k_ref, v_ref, r_ref, aux_ref, wgk_ref, bgk_ref, g_ref, o_ref, st_ref, *, ts):
    c = CHUNK
    nsub = c // SUB

    @pl.when(pl.program_id(1) == 0)
    def _():
        st_ref[...] = jnp.zeros_like(st_ref)

    pre = jnp.dot(aux_ref[...], wgk_ref[...], precision=HIGHEST, preferred_element_type=F32) + bgk_ref[...]
    gk_all = _log_sigmoid(pre) * (1.0 / GLA_GATE_NORM)

    ri = lax.broadcasted_iota(jnp.int32, (c, c), 0)
    ci = lax.broadcasted_iota(jnp.int32, (c, c), 1)
    tri = (ci <= ri).astype(F32)
    row128 = lax.broadcasted_iota(jnp.int32, (c, 128), 0)
    lane128 = lax.broadcasted_iota(jnp.int32, (1, 128), 1)
    lane_masks = [(lane128 // GLA_DK) == hh for hh in range(2)]
    sub_valid = [row128 < SUB * (i + 1) for i in range(nsub)]
    ar = lax.broadcasted_iota(jnp.int32, (c, nsub * c), 0)
    ac = lax.broadcasted_iota(jnp.int32, (c, nsub * c), 1)
    amask = ((ac // c) == (ar // SUB)) & ((ac % c) <= ar)
    scale = GLA_DK ** -0.5
    gnorm = g_ref[...]

    nch = ts // c
    rows = [slice(ch * c, (ch + 1) * c) for ch in range(nch)]
    b_all = [jnp.dot(tri, gk_all[r], precision=HIGHEST, preferred_element_type=F32) for r in rows]

    cp = [(ch, p) for ch in range(nch) for p in range(2)]
    qt, qd, kd, kstack, dlast = {}, {}, {}, {}, {}
    for ch, p in cp:
        lanes = slice(128 * p, 128 * (p + 1))
        bp = b_all[ch][:, lanes]
        qp = q_ref[rows[ch], lanes].astype(F32) * scale
        kp = k_ref[rows[ch], lanes].astype(F32)
        c_row = jnp.concatenate(
            [jnp.broadcast_to(bp[SUB * i:SUB * i + 1, :], (SUB, 128)) for i in range(nsub)], axis=0)
        qt[ch, p] = (qp * jnp.exp(bp - c_row)).astype(BF16)
        qd[ch, p] = qp * jnp.exp(bp)
        b_last = bp[c - 1:c, :]
        dlast[ch, p] = jnp.exp(b_last)
        kd[ch, p] = kp * jnp.exp(b_last - bp)
        kts = []
        for i in range(nsub):
            e = jnp.exp(jnp.where(sub_valid[i], bp[SUB * i:SUB * i + 1, :] - bp, 0.0))
            kts.append(jnp.where(sub_valid[i], kp * e, 0.0))
        kstack[ch, p] = jnp.concatenate(kts, axis=0)

    cph = [(ch, p, hh) for ch, p in cp for hh in range(2)]
    vh = {(ch, p, hh): v_ref[rows[ch], 128 * (2 * p + hh):128 * (2 * p + hh + 1)] for ch, p, hh in cph}
    r_mat = {(ch, p, hh): lax.dot_general(qt[ch, p], jnp.where(lane_masks[hh], kstack[ch, p], 0.0).astype(BF16),
                                          _NT, preferred_element_type=F32) for ch, p, hh in cph}
    o_intra = {k: jnp.dot(jnp.where(amask, r_mat[k], 0.0).astype(BF16), jnp.concatenate([vh[k]] * nsub, axis=0),
                          preferred_element_type=F32) for k in cph}
    upd = {(ch, p, hh): lax.dot_general(vh[ch, p, hh], jnp.where(lane_masks[hh], kd[ch, p], 0.0).astype(BF16), _TN,
                                        preferred_element_type=F32) for ch, p, hh in cph}

    st = [st_ref[p] for p in range(2)]
    for ch in range(nch):
        st_b = [s.astype(BF16) for s in st]
        o_inter = {(p, hh): lax.dot_general(jnp.where(lane_masks[hh], qd[ch, p], 0.0).astype(BF16), st_b[p], _NT,
                                            preferred_element_type=F32) for p in range(2) for hh in range(2)}
        st = [st[p] * dlast[ch, p] + upd[ch, p, 0] + upd[ch, p, 1] for p in range(2)]
        for p in range(2):
            for hh in range(2):
                h = 2 * p + hh
                o = o_intra[ch, p, hh] + o_inter[p, hh]
                o = o * lax.rsqrt(jnp.mean(o * o, axis=-1, keepdims=True) + RMS_EPS) * gnorm
                gate = _silu(r_ref[rows[ch], 128 * h:128 * (h + 1)].astype(F32))
                o_ref[rows[ch], 128 * h:128 * (h + 1)] = (o * gate).astype(BF16)
    for p in range(2):
        st_ref[p] = st[p]


def _gla(main, aux, wgk_pad, b_gk, gla_norm_g, *, bsz, seq, ts=256):
    nt = seq // ts
    row = lambda b, t: b * nt + t
    return pl.pallas_call(
        functools.partial(_gla_kernel, ts=ts),
        grid=(bsz, nt),
        in_specs=[pl.BlockSpec((ts, 256), lambda b, t: (row(b, t), 0)),
                  pl.BlockSpec((ts, 256), lambda b, t: (row(b, t), 1)),
                  pl.BlockSpec((ts, 512), lambda b, t: (row(b, t), 1)),
                  pl.BlockSpec((ts, 512), lambda b, t: (row(b, t), 2)),
                  pl.BlockSpec((ts, AUX_W), lambda b, t: (row(b, t), 0)),
                  pl.BlockSpec((AUX_W, 256), lambda b, t: (0, 0)),
                  pl.BlockSpec((1, 256), lambda b, t: (0, 0)),
                  pl.BlockSpec((1, 128), lambda b, t: (0, 0))],
        out_specs=pl.BlockSpec((ts, GLA_WIDTH), lambda b, t: (row(b, t), 0)),
        out_shape=jax.ShapeDtypeStruct((bsz * seq, GLA_WIDTH), BF16),
        scratch_shapes=[pltpu.VMEM((2, 128, 128), F32)],
        compiler_params=_cparams("parallel", "arbitrary"),
        name="gla",
    )(main, main, main, main, aux, wgk_pad, b_gk, gla_norm_g)


CONV_PAD = 8


def _mm(a, b):
    return jnp.dot(a.astype(BF16), b.astype(BF16), preferred_element_type=F32)


def _gdn_kernel(q_ref, k_ref, v_ref, z_ref, aux_ref, cw_ref, alog_ref, dtb_ref, g_ref, o_ref,
                st_ref, xe_ref, *, ts):
    c = CHUNK
    nch = ts // c
    hw = GDN_HEADS * GDN_DK

    @pl.when(pl.program_id(1) == 0)
    def _():
        st_ref[...] = jnp.zeros_like(st_ref)
        xe_ref[0:CONV_PAD, :] = jnp.zeros((CONV_PAD, 3 * hw), F32)

    xe_ref[CONV_PAD:, 0:hw] = q_ref[...].astype(F32)
    xe_ref[CONV_PAD:, hw:2 * hw] = k_ref[...].astype(F32)
    xe_ref[CONV_PAD:, 2 * hw:3 * hw] = v_ref[...].astype(F32)

    aux = aux_ref[...]
    beta_full = _sigmoid(aux)
    g_full = -jnp.exp(alog_ref[...]) * _softplus(aux + dtb_ref[...])
    ri = lax.broadcasted_iota(jnp.int32, (ts, ts), 0)
    ci = lax.broadcasted_iota(jnp.int32, (ts, ts), 1)
    cum = (((ri // c) == (ci // c)) & (ci <= ri)).astype(F32)
    gam_full = jnp.dot(cum, g_full, precision=HIGHEST, preferred_element_type=F32)
    gam_t = gam_full.T
    gnorm = g_ref[...]

    row = lax.broadcasted_iota(jnp.int32, (c, 2 * c), 0)
    lane = lax.broadcasted_iota(jnp.int32, (c, 2 * c), 1)
    left = lane < c
    col = lane % c
    causal = col <= row
    strict = col < row
    same32 = (row // 32) == (col // 32)
    same16 = (row // 16) == (col // 16)
    lvl0 = same16 & strict
    lvl1 = same32 & jnp.logical_not(same16) & strict
    lvl2 = jnp.logical_not(same32) & strict

    def pack_rows(x2):
        return jnp.where(left, x2[0:c], x2[c:2 * c])

    def pack_cols(x1):
        return jnp.where(left, x1[0:c], x1[c:2 * c])

    def halves(x):
        return jnp.where(left, x, 0.0).astype(BF16), jnp.where(left, 0.0, x).astype(BF16)

    def mmp(x, y):
        return jnp.dot(x.astype(BF16), jnp.concatenate(halves(y), axis=0), preferred_element_type=F32)

    def conv(col):
        acc = jnp.zeros((ts, 128), F32)
        for i in range(CONV_W):
            lo = CONV_PAD - (CONV_W - 1) + i
            acc = acc + xe_ref[lo:lo + ts, col:col + 128] * cw_ref[i:i + 1, col:col + 128]
        return _silu(acc)

    heads = range(GDN_HEADS)
    npair = nch // 2
    items = [(h, pr) for h in heads for pr in range(npair)]
    prow = [slice(2 * c * pr, 2 * c * (pr + 1)) for pr in range(npair)]

    qn, kn, beta, gam, gam_r, kb, rhs, rhs_b, qd, kd = ([None] * GDN_HEADS for _ in range(10))
    for h in heads:
        qh = conv(128 * h)
        kh = conv(hw + 128 * h)
        vh = conv(2 * hw + 128 * h)
        qn[h] = qh * lax.rsqrt(jnp.sum(qh * qh, axis=-1, keepdims=True) + RMS_EPS) * (GDN_DK ** -0.5)
        kn[h] = kh * lax.rsqrt(jnp.sum(kh * kh, axis=-1, keepdims=True) + RMS_EPS)
        beta[h] = beta_full[:, AUX_BB + h:AUX_BB + h + 1]
        gam[h] = gam_full[:, AUX_AB + h:AUX_AB + h + 1]
        gam_r[h] = gam_t[AUX_AB + h:AUX_AB + h + 1, :]
        egam = jnp.exp(gam[h])
        kb[h] = kn[h].astype(BF16)
        rhs[h] = jnp.concatenate([beta[h] * vh, beta[h] * egam * kn[h]], axis=1)
        rhs_b[h] = rhs[h].astype(BF16)
        qd[h] = (qn[h] * egam).astype(BF16)
        gl_rows = jnp.concatenate(
            [jnp.broadcast_to(gam[h][c * (j + 1) - 1:c * (j + 1), :], (c, 1)) for j in range(nch)], axis=0)
        kd[h] = (kn[h] * jnp.exp(gl_rows - gam[h])).astype(BF16)

    kk = [pack_rows(lax.dot_general(kb[h][prow[pr]], kb[h][prow[pr]], _NT, preferred_element_type=F32))
          for h, pr in items]
    qk = [pack_rows(lax.dot_general(qn[h][prow[pr]].astype(BF16), kb[h][prow[pr]], _NT,
                                    preferred_element_type=F32)) for h, pr in items]
    dec = [jnp.exp(jnp.where(causal, pack_cols(gam[h][prow[pr]]) - gam_r[h][:, prow[pr]], 0.0)) for h, pr in items]
    a = [pack_cols(beta[h][prow[pr]]) * kk[i] * dec[i] for i, (h, pr) in enumerate(items)]
    attn = [halves(jnp.where(causal, qk[i] * dec[i], 0.0)) for i in range(len(items))]

    x1 = [jnp.where(lvl0, -ai, 0.0) for ai in a]
    x2 = [mmp(x, x) for x in x1]
    y = [mmp(x, xx) for x, xx in zip(x1, x2)]
    p = [x + xx + yy for x, xx, yy in zip(x1, x2, y)]
    xk = x2
    for _ in range(2):
        xk = [mmp(x, x) for x in xk]
        y = [mmp(pp, x) for pp, x in zip(p, xk)]
        p = [pp + x + yy for pp, x, yy in zip(p, xk, y)]
    for lvl in (lvl1, lvl2):
        low = [jnp.where(lvl, ai, 0.0) for ai in a]
        y = [mmp(pp, lo) for pp, lo in zip(p, low)]
        zz = [lo + yy for lo, yy in zip(low, y)]
        y = [mmp(z, pp) for z, pp in zip(zz, p)]
        p = [pp - z - yy for pp, z, yy in zip(p, zz, y)]

    us = [[None] * nch for _ in heads]
    ws = [[None] * nch for _ in heads]
    for i, (h, pr) in enumerate(items):
        for half, p_half in enumerate(halves(p[i])):
            j = 2 * pr + half
            uw = rhs[h][c * j:c * (j + 1)] + jnp.dot(p_half, rhs_b[h][prow[pr]], preferred_element_type=F32)
            us[h][j] = uw[:, :GDN_DV]
            ws[h][j] = uw[:, GDN_DV:].astype(BF16)

    s = [st_ref[h] for h in heads]
    vprev = [None] * GDN_HEADS
    for j in range(nch):
        rows = slice(c * j, c * (j + 1))
        sb = [s[h].astype(BF16) for h in heads]
        vn16 = [(us[h][j] - jnp.dot(ws[h][j], sb[h], preferred_element_type=F32)).astype(BF16) for h in heads]
        o_inter = [jnp.dot(qd[h][rows], sb[h], preferred_element_type=F32) for h in heads]
        for h in heads:
            vpair = jnp.concatenate([vn16[h], jnp.zeros_like(vn16[h])] if j % 2 == 0 else [vprev[h], vn16[h]], axis=0)
            o = o_inter[h] + jnp.dot(attn[h * npair + j // 2][j % 2], vpair, preferred_element_type=F32)
            s[h] = jnp.exp(gam[h][c * (j + 1) - 1:c * (j + 1), :]) * s[h] + lax.dot_general(
                kd[h][rows], vn16[h], _TN, preferred_element_type=F32)
            o = o * lax.rsqrt(jnp.mean(o * o, axis=-1, keepdims=True) + RMS_EPS) * gnorm
            gate = _silu(z_ref[rows, 128 * h:128 * (h + 1)].astype(F32))
            o_ref[rows, 128 * h:128 * (h + 1)] = (o * gate).astype(BF16)
        vprev = vn16
    for h in heads:
        st_ref[h] = s[h]

    xe_ref[0:CONV_PAD, :] = xe_ref[ts:ts + CONV_PAD, :]


def _gdn(main, aux, conv_w, alog_pad, dtb_pad, gdn_norm_g, *, bsz, seq, ts=256):
    nt = seq // ts
    row = lambda b, t: b * nt + t
    hw = GDN_HEADS * GDN_DK
    return pl.pallas_call(
        functools.partial(_gdn_kernel, ts=ts),
        grid=(bsz, nt),
        in_specs=[pl.BlockSpec((ts, hw), lambda b, t: (row(b, t), 3)),
                  pl.BlockSpec((ts, hw), lambda b, t: (row(b, t), 4)),
                  pl.BlockSpec((ts, hw), lambda b, t: (row(b, t), 5)),
                  pl.BlockSpec((ts, hw), lambda b, t: (row(b, t), 6)),
                  pl.BlockSpec((ts, AUX_W), lambda b, t: (row(b, t), 0)),
                  pl.BlockSpec((CONV_W, 3 * hw), lambda b, t: (0, 0)),
                  pl.BlockSpec((1, AUX_W), lambda b, t: (0, 0)),
                  pl.BlockSpec((1, AUX_W), lambda b, t: (0, 0)),
                  pl.BlockSpec((1, GDN_DV), lambda b, t: (0, 0))],
        out_specs=pl.BlockSpec((ts, GDN_WIDTH), lambda b, t: (row(b, t), 0)),
        out_shape=jax.ShapeDtypeStruct((bsz * seq, GDN_WIDTH), BF16),
        scratch_shapes=[pltpu.VMEM((GDN_HEADS, GDN_DK, GDN_DV), F32),
                        pltpu.VMEM((ts + CONV_PAD, 3 * hw), F32)],
        compiler_params=_cparams("parallel", "arbitrary"),
        name="gdn",
    )(main, main, main, main, aux, conv_w, alog_pad, dtb_pad, gdn_norm_g)


def _pad_aux_row(v, offset):
    return jnp.zeros((1, AUX_W), F32).at[0, offset:offset + v.shape[0]].set(v.astype(F32))


ROUTE_W = 128
R_E1, R_E2, R_RANK1, R_RANK2, R_G1, R_G2 = range(6)
ROUTER_GROUP_COL = N_EXPERTS


def _layer_norm(h, g, b):
    mu = jnp.mean(h, axis=-1, keepdims=True)
    hc = h - mu
    var = jnp.mean(hc * hc, axis=-1, keepdims=True)
    return hc * lax.rsqrt(var + LN_EPS) * g + b


def _pack_bf16_pairs(x):
    w = x.shape[1] // 2
    lo = lax.bitcast_convert_type(x[:, :w].astype(BF16).astype(F32), jnp.uint32)
    hi = lax.bitcast_convert_type(x[:, w:].astype(BF16).astype(F32), jnp.uint32)
    return (lo >> 16) | (hi & jnp.uint32(0xFFFF0000))


def _unpack_bf16_pairs(p):
    lo = lax.bitcast_convert_type(p << 16, F32)
    hi = lax.bitcast_convert_type(p & jnp.uint32(0xFFFF0000), F32)
    return lo, hi


def _post_mix_kernel(oa_ref, ob_ref, x_ref, wo_ref, g_ref, b_ref, wr_ref, br_ref,
                     x1_ref, x1p_ref, route_ref, cnt_ref, carry_ref, *, tm):
    @pl.when(pl.program_id(0) == 0)
    def _():
        carry_ref[...] = jnp.zeros_like(carry_ref)

    y = jnp.dot(oa_ref[...], wo_ref[0:GLA_WIDTH, :], preferred_element_type=F32)
    y = y + jnp.dot(ob_ref[...], wo_ref[GLA_WIDTH:, :], preferred_element_type=F32)
    x1 = _layer_norm(ALPHA_DN * x_ref[...] + y, g_ref[...], b_ref[...])
    x1_ref[...] = x1
    x1p_ref[...] = _pack_bf16_pairs(x1)

    xh = x1.astype(BF16)
    xl = (x1 - xh.astype(F32)).astype(BF16)
    hw = jnp.dot(xh, wr_ref[...], preferred_element_type=F32)
    logits = (hw[:, :ROUTE_W] + hw[:, ROUTE_W:]
              + jnp.dot(xl, wr_ref[:, :ROUTE_W], preferred_element_type=F32) + br_ref[...])
    lane = lax.broadcasted_iota(jnp.int32, (tm, ROUTE_W), 1)
    big = jnp.int32(1 << 20)
    neg = jnp.float32(-jnp.inf)

    def first_argmax(vals):
        m = jnp.max(vals, axis=-1, keepdims=True)
        idx = jnp.min(jnp.where(vals == m, lane, big), axis=-1, keepdims=True)
        return m, idx

    is_group = (lane >= ROUTER_GROUP_COL) & (lane < ROUTER_GROUP_COL + N_GROUPS)
    gl = jnp.where(is_group, logits, neg)
    gmax, gidx = first_argmax(gl)
    p_group = 1.0 / jnp.sum(jnp.exp(gl - gmax), axis=-1, keepdims=True)
    g_sel = gidx - ROUTER_GROUP_COL
    el = jnp.where((lane // EXPERTS_PER_GROUP) == g_sel, logits, neg)
    t1, e1 = first_argmax(el)
    t2, e2 = first_argmax(jnp.where(lane == e1, neg, el))
    ex = jnp.exp(t2 - t1)
    w1 = 1.0 / (1.0 + ex)
    gate1 = p_group * w1
    gate2 = p_group * (ex * w1)

    oh1 = lane == e1
    oh2 = lane == e2
    ri = lax.broadcasted_iota(jnp.int32, (tm, tm), 0)
    ci = lax.broadcasted_iota(jnp.int32, (tm, tm), 1)
    lstrict = (ci < ri).astype(BF16)
    oh1f = oh1.astype(F32)
    oh2f = oh2.astype(F32)
    c1 = jnp.dot(lstrict, oh1f.astype(BF16), preferred_element_type=F32)
    c2 = jnp.dot(lstrict, oh2f.astype(BF16), preferred_element_type=F32)
    carry = carry_ref[...]
    tot1 = jnp.sum(oh1f, axis=0, keepdims=True)
    tot2 = jnp.sum(oh2f, axis=0, keepdims=True)
    rank1 = jnp.sum(jnp.where(oh1, c1 + carry, 0.0), axis=-1, keepdims=True)
    rank2 = jnp.sum(jnp.where(oh2, c2 + (carry + tot1), 0.0), axis=-1, keepdims=True)
    new_carry = carry + tot1 + tot2
    carry_ref[...] = new_carry
    cnt_ref[...] = new_carry

    cols = (e1.astype(F32), e2.astype(F32), rank1, rank2, gate1, gate2)
    route = jnp.zeros((tm, ROUTE_W), F32)
    for j, col in enumerate(cols):
        route = jnp.where(lane == j, col, route)
    route_ref[...] = route


def _post_mix(o_a, o_b, x2d, w_out_b, ln_g, ln_b, w_router, b_router, *, tm=512):
    n = x2d.shape[0]
    return pl.pallas_call(
        functools.partial(_post_mix_kernel, tm=tm),
        grid=(n // tm,),
        in_specs=[pl.BlockSpec((tm, GLA_WIDTH), lambda i: (i, 0)),
                  pl.BlockSpec((tm, GDN_WIDTH), lambda i: (i, 0)),
                  pl.BlockSpec((tm, D_MODEL), lambda i: (i, 0)),
                  pl.BlockSpec((GLA_WIDTH + GDN_WIDTH, D_MODEL), lambda i: (0, 0)),
                  pl.BlockSpec((1, D_MODEL), lambda i: (0, 0)),
                  pl.BlockSpec((1, D_MODEL), lambda i: (0, 0)),
                  pl.BlockSpec((D_MODEL, 2 * ROUTE_W), lambda i: (0, 0)),
                  pl.BlockSpec((1, ROUTE_W), lambda i: (0, 0))],
        out_specs=[pl.BlockSpec((tm, D_MODEL), lambda i: (i, 0)),
                   pl.BlockSpec((tm, D_MODEL // 2), lambda i: (i, 0)),
                   pl.BlockSpec((tm, ROUTE_W), lambda i: (i, 0)),
                   pl.BlockSpec((1, ROUTE_W), lambda i: (0, 0))],
        out_shape=[jax.ShapeDtypeStruct((n, D_MODEL), F32),
                   jax.ShapeDtypeStruct((n, D_MODEL // 2), jnp.uint32),
                   jax.ShapeDtypeStruct((n, ROUTE_W), F32),
                   jax.ShapeDtypeStruct((1, ROUTE_W), F32)],
        scratch_shapes=[pltpu.VMEM((1, ROUTE_W), F32)],
        compiler_params=_cparams("arbitrary"),
        name="post_mix",
    )(o_a, o_b, x2d, w_out_b, ln_g, ln_b, w_router, b_router)


EXPERT_BLOCK = 256


HALF = D_MODEL // 2
ROW_UNROLL = 8


def _plan_kernel(route_ref, cnt_ref, dest_ref, *, tm, tb):
    counts = cnt_ref[...].astype(jnp.int32)
    padded = ((counts + (tb - 1)) & jnp.int32(-tb)).astype(F32)
    ri = lax.broadcasted_iota(jnp.int32, (ROUTE_W, ROUTE_W), 0)
    ci = lax.broadcasted_iota(jnp.int32, (ROUTE_W, ROUTE_W), 1)
    pstart = jnp.dot(jnp.broadcast_to(padded, (8, ROUTE_W)), (ri < ci).astype(F32),
                     precision=HIGHEST, preferred_element_type=F32)[0:1, :]
    route = route_ref[...]
    lane = lax.broadcasted_iota(jnp.int32, (tm, ROUTE_W), 1).astype(F32)
    out = jnp.zeros((tm, ROUTE_W), F32)
    for k, (ce, cr) in enumerate(((R_E1, R_RANK1), (R_E2, R_RANK2))):
        off = jnp.sum(jnp.where(lane == route[:, ce:ce + 1], pstart, 0.0), axis=-1, keepdims=True)
        out = jnp.where(lane == float(k), off + route[:, cr:cr + 1], out)
    dest_ref[...] = out.T[0:8, :].astype(jnp.int32)


def _plan(route, counts, *, tb, tm=512):
    n = route.shape[0]
    return pl.pallas_call(
        functools.partial(_plan_kernel, tm=tm, tb=tb),
        grid=(n // tm,),
        in_specs=[pl.BlockSpec((tm, ROUTE_W), lambda i: (i, 0)),
                  pl.BlockSpec((1, ROUTE_W), lambda i: (0, 0))],
        out_specs=pl.BlockSpec((8, tm), lambda i: (0, i)),
        out_shape=jax.ShapeDtypeStruct((8, n), jnp.int32),
        compiler_params=_cparams("parallel"),
        name="plan",
    )(route, counts)


def _dispatch_kernel(d1_ref, d2_ref, xp_ref, xs_init_ref, xs_ref, sem, *, tm):
    del xs_init_ref
    base = pl.program_id(0) * tm

    def body(j, carry):
        for u in range(ROW_UNROLL):
            r = j * ROW_UNROLL + u
            for d_ref in (d1_ref, d2_ref):
                pltpu.make_async_copy(xp_ref.at[pl.ds(r, 1), :], xs_ref.at[pl.ds(d_ref[base + r], 1), :], sem).start()
        return carry
    lax.fori_loop(0, tm // ROW_UNROLL, body, 0)
    for _ in range(TOP_K):
        pltpu.make_async_copy(xp_ref, xs_ref.at[pl.ds(0, tm), :], sem).wait()


def _dispatch(dest1, dest2, x1p, n_rows, *, tm=512):
    n = x1p.shape[0]
    grid_spec = pltpu.PrefetchScalarGridSpec(
        num_scalar_prefetch=2,
        grid=(n // tm,),
        in_specs=[pl.BlockSpec((tm, HALF), lambda i, a, b: (i, 0)),
                  pl.BlockSpec(memory_space=pl.ANY)],
        out_specs=pl.BlockSpec(memory_space=pl.ANY),
        scratch_shapes=[pltpu.SemaphoreType.DMA(())],
    )
    return pl.pallas_call(
        functools.partial(_dispatch_kernel, tm=tm),
        grid_spec=grid_spec,
        out_shape=jax.ShapeDtypeStruct((n_rows, HALF), jnp.uint32),
        input_output_aliases={3: 0},
        compiler_params=_cparams("arbitrary"),
        name="dispatch",
    )(dest1, dest2, x1p, jnp.zeros((n_rows, HALF), jnp.uint32))


def _expert_kernel(blk_e_ref, first_ref, next_e_ref, nvalid_ref, xs_ref, wg_hbm, wu_hbm, wd_hbm, y_ref,
                   wgs, wus, wds, wgb, wub, wdb, sem):
    i = pl.program_id(0)

    def weight_copies(e):
        return [pltpu.make_async_copy(src.at[e], dst, sem.at[j])
                for j, (src, dst) in enumerate(((wg_hbm, wgs), (wu_hbm, wus), (wd_hbm, wds)))]

    @pl.when((i == 0) & (nvalid_ref[0] > 0))
    def _():
        for cp in weight_copies(blk_e_ref[0]):
            cp.start()

    @pl.when(i < nvalid_ref[0])
    def _():
        @pl.when(first_ref[i] == 1)
        def _():
            for cp in weight_copies(blk_e_ref[i]):
                cp.wait()
            wgb[...] = wgs[...].astype(BF16)
            wub[...] = wus[...].astype(BF16)
            wdb[...] = wds[...].astype(BF16)

            @pl.when(next_e_ref[i] >= 0)
            def _():
                for cp in weight_copies(next_e_ref[i]):
                    cp.start()

        lo, hi = _unpack_bf16_pairs(xs_ref[...])
        lo = lo.astype(BF16)
        hi = hi.astype(BF16)
        hg = jnp.dot(lo, wgb[0:HALF, :], preferred_element_type=F32)
        hg = hg + jnp.dot(hi, wgb[HALF:, :], preferred_element_type=F32)
        hu = jnp.dot(lo, wub[0:HALF, :], preferred_element_type=F32)
        hu = hu + jnp.dot(hi, wub[HALF:, :], preferred_element_type=F32)
        hid = (_silu(hg) * hu).astype(BF16)
        y_ref[...] = _pack_bf16_pairs(jnp.dot(hid, wdb[...], preferred_element_type=F32))

    @pl.when(i >= nvalid_ref[0])
    def _():
        y_ref[...] = jnp.zeros_like(y_ref)


def _experts(blk_e, first, next_e, nvalid, xs, w_gate, w_up, w_down, *, tb=EXPERT_BLOCK):
    nb = blk_e.shape[0]
    grid_spec = pltpu.PrefetchScalarGridSpec(
        num_scalar_prefetch=4,
        grid=(nb,),
        in_specs=[pl.BlockSpec((tb, HALF), lambda i, be, fr, ne, nv: (jnp.clip(i, 0, jnp.maximum(nv[0] - 1, 0)), 0)),
                  pl.BlockSpec(memory_space=pl.ANY),
                  pl.BlockSpec(memory_space=pl.ANY),
                  pl.BlockSpec(memory_space=pl.ANY)],
        out_specs=pl.BlockSpec((tb, HALF), lambda i, be, fr, ne, nv: (i, 0)),
        scratch_shapes=[pltpu.VMEM((D_MODEL, D_EXPERT), F32),
                        pltpu.VMEM((D_MODEL, D_EXPERT), F32),
                        pltpu.VMEM((D_EXPERT, D_MODEL), F32),
                        pltpu.VMEM((D_MODEL, D_EXPERT), BF16),
                        pltpu.VMEM((D_MODEL, D_EXPERT), BF16),
                        pltpu.VMEM((D_EXPERT, D_MODEL), BF16),
                        pltpu.SemaphoreType.DMA((3,))],
    )
    return pl.pallas_call(
        _expert_kernel,
        grid_spec=grid_spec,
        out_shape=jax.ShapeDtypeStruct((nb * tb, HALF), jnp.uint32),
        compiler_params=_cparams("arbitrary"),
        name="experts",
    )(blk_e, first, next_e, nvalid, xs, w_gate, w_up, w_down)


def _combine_kernel(d1_ref, d2_ref, y_hbm, x1_ref, route_ref, g_ref, b_ref, o_ref, ybuf, sem, *, tt):
    i = pl.program_id(0)
    nsteps = pl.num_programs(0)
    slot = i % 2

    def issue(tile, slot_):
        def body(j, carry):
            for u in range(ROW_UNROLL):
                r = j * ROW_UNROLL + u
                for k, d_ref in enumerate((d1_ref, d2_ref)):
                    pltpu.make_async_copy(y_hbm.at[pl.ds(d_ref[tile * tt + r], 1), :],
                                          ybuf.at[slot_, k, pl.ds(r, 1), :], sem.at[slot_]).start()
            return carry
        lax.fori_loop(0, tt // ROW_UNROLL, body, 0)

    @pl.when(i == 0)
    def _():
        issue(0, 0)

    @pl.when(i + 1 < nsteps)
    def _():
        issue(i + 1, 1 - slot)

    for k in range(TOP_K):
        pltpu.make_async_copy(y_hbm.at[pl.ds(0, tt), :], ybuf.at[slot, k], sem.at[slot]).wait()
    route = route_ref[...]
    g1 = route[:, R_G1:R_G1 + 1]
    g2 = route[:, R_G2:R_G2 + 1]
    lo1, hi1 = _unpack_bf16_pairs(ybuf[slot, 0])
    lo2, hi2 = _unpack_bf16_pairs(ybuf[slot, 1])
    h_lo = ALPHA_DN * x1_ref[:, 0:HALF] + (g1 * lo1 + g2 * lo2)
    h_hi = ALPHA_DN * x1_ref[:, HALF:] + (g1 * hi1 + g2 * hi2)
    mu = (jnp.sum(h_lo, axis=-1, keepdims=True) + jnp.sum(h_hi, axis=-1, keepdims=True)) * (1.0 / D_MODEL)
    c_lo = h_lo - mu
    c_hi = h_hi - mu
    var = (jnp.sum(c_lo * c_lo, axis=-1, keepdims=True) + jnp.sum(c_hi * c_hi, axis=-1, keepdims=True)) * (1.0 / D_MODEL)
    inv = lax.rsqrt(var + LN_EPS)
    o_ref[:, 0:HALF] = c_lo * inv * g_ref[:, 0:HALF] + b_ref[:, 0:HALF]
    o_ref[:, HALF:] = c_hi * inv * g_ref[:, HALF:] + b_ref[:, HALF:]


def _combine(dest1, dest2, ybuf, x1, route, ln_g, ln_b, *, tt=256):
    n = x1.shape[0]
    grid_spec = pltpu.PrefetchScalarGridSpec(
        num_scalar_prefetch=2,
        grid=(n // tt,),
        in_specs=[pl.BlockSpec(memory_space=pl.ANY),
                  pl.BlockSpec((tt, D_MODEL), lambda i, a, b: (i, 0)),
                  pl.BlockSpec((tt, ROUTE_W), lambda i, a, b: (i, 0)),
                  pl.BlockSpec((1, D_MODEL), lambda i, a, b: (0, 0)),
                  pl.BlockSpec((1, D_MODEL), lambda i, a, b: (0, 0))],
        out_specs=pl.BlockSpec((tt, D_MODEL), lambda i, a, b: (i, 0)),
        scratch_shapes=[pltpu.VMEM((2, TOP_K, tt, HALF), jnp.uint32),
                        pltpu.SemaphoreType.DMA((2,))],
    )
    return pl.pallas_call(
        functools.partial(_combine_kernel, tt=tt),
        grid_spec=grid_spec,
        out_shape=jax.ShapeDtypeStruct((n, D_MODEL), F32),
        compiler_params=_cparams("arbitrary"),
        name="combine",
    )(dest1, dest2, ybuf, x1, route, ln_g, ln_b)


def _block_plan(counts_row, n, tb):
    counts = counts_row[0, :N_EXPERTS].astype(jnp.int32)
    pends = jnp.cumsum((counts + tb - 1) // tb * tb)
    nb = (n * TOP_K) // tb + N_EXPERTS
    nvalid = (pends[-1] // tb).astype(jnp.int32)
    blk = jnp.arange(nb, dtype=jnp.int32)
    blk_e = jnp.minimum(jnp.searchsorted(pends, blk * tb, side='right'), N_EXPERTS - 1).astype(jnp.int32)
    blk_e = jnp.where(blk < nvalid, blk_e, blk_e[jnp.maximum(nvalid - 1, 0)])
    first = jnp.concatenate([jnp.ones((1,), jnp.int32), (blk_e[1:] != blk_e[:-1]).astype(jnp.int32)])
    ids = jnp.arange(N_EXPERTS, dtype=jnp.int32)
    later = lax.cummin(jnp.where(counts > 0, ids, N_EXPERTS), axis=0, reverse=True)
    nxt = jnp.concatenate([later[1:], jnp.full((1,), N_EXPERTS, jnp.int32)])
    nxt = jnp.where(nxt < N_EXPERTS, nxt, -1)
    return blk_e, first, nxt[blk_e], nvalid.reshape(1)


def kernel(x, w_in, w_gk_up, b_gk, conv_w, a_log, dt_bias, gla_norm_g, gdn_norm_g, w_out, ln1_g, ln1_b, w_router_group, b_router_group, w_router_expert, b_router_expert, w_gate, w_up, w_down, ln2_g, ln2_b):
    bsz, seq, d = x.shape
    n = bsz * seq
    x2d = x.reshape(n, d)
    w_main, w_aux = _split_w_in(w_in[0])
    main, aux = _in_proj(x2d, w_main, w_aux)
    wgk = jnp.zeros((AUX_W, 256), F32).at[:GLA_GATE_RANK].set(w_gk_up[0])
    o_a = _gla(main, aux, wgk, b_gk[0][None], gla_norm_g[0][None], bsz=bsz, seq=seq)
    o_b = _gdn(main, aux, conv_w[0], _pad_aux_row(a_log[0], AUX_AB), _pad_aux_row(dt_bias[0], AUX_AB),
               gdn_norm_g[0][None], bsz=bsz, seq=seq)
    rpad = jnp.zeros((d, ROUTE_W - N_EXPERTS - N_GROUPS), F32)
    w_router = jnp.concatenate([w_router_expert[0], w_router_group[0], rpad], axis=1)
    w_router_hi = w_router.astype(BF16)
    w_router = jnp.concatenate([w_router_hi, (w_router - w_router_hi.astype(F32)).astype(BF16)], axis=1)
    b_router = jnp.concatenate([b_router_expert[0], b_router_group[0], rpad[0]])[None]
    x1, x1p, route, counts = _post_mix(o_a, o_b, x2d, w_out[0].astype(BF16), ln1_g[0][None], ln1_b[0][None],
                                       w_router, b_router)
    dest = _plan(route, counts, tb=EXPERT_BLOCK)
    blk_e, first, next_e, nvalid = _block_plan(counts, n, EXPERT_BLOCK)
    xs = _dispatch(dest[0], dest[1], x1p, blk_e.shape[0] * EXPERT_BLOCK)
    ybuf = _experts(blk_e, first, next_e, nvalid, xs, w_gate[0], w_up[0], w_down[0])
    out = _combine(dest[0], dest[1], ybuf, x1, route, ln2_g[0][None], ln2_b[0][None])
    return out.reshape(bsz, seq, d)
```

```python
import functools

import jax
import jax.numpy as jnp
import numpy as np
from jax import lax
from jax.experimental import pallas as pl
from jax.experimental.pallas import tpu as pltpu

F32 = jnp.float32
BF16 = jnp.bfloat16
HIGHEST = lax.Precision.HIGHEST

D_MODEL = 1024
DEPTH = 1
GLA_HEADS = 4
GLA_DK = 64
GLA_DV = 128
GLA_WIDTH = GLA_HEADS * GLA_DV
GLA_GATE_RANK = 16
GLA_GATE_NORM = 16.0
GDN_HEADS = 4
GDN_DK = 128
GDN_DV = 128
GDN_WIDTH = GDN_HEADS * GDN_DV
CONV_W = 4
CHUNK = 64
N_GROUPS = 8
EXPERTS_PER_GROUP = 8
N_EXPERTS = N_GROUPS * EXPERTS_PER_GROUP
TOP_K = 2
D_EXPERT = 512
LN_EPS = 1e-5
RMS_EPS = 1e-6
ALPHA_DN = (2.0 * DEPTH) ** 0.25

MAIN_W = 2 * GLA_HEADS * GLA_DK + 2 * GLA_WIDTH + 4 * GDN_WIDTH
AUX_W = 128
AUX_LRA = 0
AUX_BB = GLA_GATE_RANK
AUX_AB = GLA_GATE_RANK + GDN_HEADS
SUB = 16
VMEM_LIMIT = 56 * 1024 * 1024


def _cparams(*sem):
    return pltpu.CompilerParams(dimension_semantics=sem, vmem_limit_bytes=VMEM_LIMIT)


def _sigmoid(x):
    return 1.0 / (1.0 + jnp.exp(-x))


def _silu(x):
    return x * _sigmoid(x)


def _log_sigmoid(x):
    return jnp.minimum(x, 0.0) - jnp.log(1.0 + jnp.exp(-jnp.abs(x)))


def _softplus(x):
    return jnp.maximum(x, 0.0) + jnp.log(1.0 + jnp.exp(-jnp.abs(x)))


def _in_proj_kernel(x_ref, wm_ref, wa_ref, main_ref, aux_ref, *, col_block):
    xb = x_ref[...].astype(BF16)
    for j in range(MAIN_W // col_block):
        sl = slice(j * col_block, (j + 1) * col_block)
        main_ref[:, sl] = jnp.dot(xb, wm_ref[:, sl], preferred_element_type=F32).astype(BF16)
    aux_ref[...] = jnp.dot(xb, wa_ref[...], preferred_element_type=F32)


def _in_proj(x2d, w_main, w_aux, *, tm=512, col_block=512):
    n = x2d.shape[0]
    return pl.pallas_call(
        functools.partial(_in_proj_kernel, col_block=col_block),
        grid=(n // tm,),
        in_specs=[pl.BlockSpec((tm, D_MODEL), lambda i: (i, 0)),
                  pl.BlockSpec((D_MODEL, MAIN_W), lambda i: (0, 0)),
                  pl.BlockSpec((D_MODEL, AUX_W), lambda i: (0, 0))],
        out_specs=[pl.BlockSpec((tm, MAIN_W), lambda i: (i, 0)),
                   pl.BlockSpec((tm, AUX_W), lambda i: (i, 0))],
        out_shape=[jax.ShapeDtypeStruct((n, MAIN_W), BF16),
                   jax.ShapeDtypeStruct((n, AUX_W), F32)],
        compiler_params=_cparams("parallel"),
        name="in_proj",
    )(x2d, w_main, w_aux)


def _split_w_in(w_in):
    sizes = (GLA_HEADS * GLA_DK, GLA_HEADS * GLA_DK, GLA_WIDTH, GLA_WIDTH, GLA_GATE_RANK,
             GDN_HEADS * GDN_DK, GDN_HEADS * GDN_DK, GDN_WIDTH, GDN_WIDTH, GDN_HEADS, GDN_HEADS)
    offs = np.cumsum((0,) + sizes)
    seg = [w_in[:, offs[i]:offs[i + 1]] for i in range(len(sizes))]
    qa, ka, va, ra, lra, qb, kb, vb, zb, bb, ab = seg
    w_main = jnp.concatenate([qa, ka, va, ra, qb, kb, vb, zb], axis=1).astype(BF16)
    pad = jnp.zeros((w_in.shape[0], AUX_W - GLA_GATE_RANK - 2 * GDN_HEADS), w_in.dtype)
    w_aux = jnp.concatenate([lra, bb, ab, pad], axis=1).astype(BF16)
    return w_main, w_aux


_NT = (((1,), (1,)), ((), ()))
_TN = (((0,), (0,)), ((), ()))


def _gla_kernel(q_ref, k_ref, v_ref, r_ref, aux_ref, wgk_ref, bgk_ref, g_ref, o_ref, st_ref, *, ts):
    c = CHUNK
    nsub = c // SUB

    @pl.when(pl.program_id(1) == 0)
    def _():
        st_ref[...] = jnp.zeros_like(st_ref)

    pre = jnp.dot(aux_ref[...], wgk_ref[...], precision=HIGHEST, preferred_element_type=F32) + bgk_ref[...]
    gk_all = _log_sigmoid(pre) * (1.0 / GLA_GATE_NORM)

    ri = lax.broadcasted_iota(jnp.int32, (c, c), 0)
    ci = lax.broadcasted_iota(jnp.int32, (c, c), 1)
    tri = (ci <= ri).astype(F32)
    row128 = lax.broadcasted_iota(jnp.int32, (c, 128), 0)
    lane128 = lax.broadcasted_iota(jnp.int32, (1, 128), 1)
    lane_masks = [(lane128 // GLA_DK) == hh for hh in range(2)]
    sub_valid = [row128 < SUB * (i + 1) for i in range(nsub)]
    ar = lax.broadcasted_iota(jnp.int32, (c, nsub * c), 0)
    ac = lax.broadcasted_iota(jnp.int32, (c, nsub * c), 1)
    amask = ((ac // c) == (ar // SUB)) & ((ac % c) <= ar)
    scale = GLA_DK ** -0.5
    gnorm = g_ref[...]

    nch = ts // c
    rows = [slice(ch * c, (ch + 1) * c) for ch in range(nch)]
    b_all = [jnp.dot(tri, gk_all[r], precision=HIGHEST, preferred_element_type=F32) for r in rows]

    cp = [(ch, p) for ch in range(nch) for p in range(2)]
    qt, qd, kd, kstack, dlast = {}, {}, {}, {}, {}
    for ch, p in cp:
        lanes = slice(128 * p, 128 * (p + 1))
        bp = b_all[ch][:, lanes]
        qp = q_ref[rows[ch], lanes].astype(F32) * scale
        kp = k_ref[rows[ch], lanes].astype(F32)
        c_row = jnp.concatenate(
            [jnp.broadcast_to(bp[SUB * i:SUB * i + 1, :], (SUB, 128)) for i in range(nsub)], axis=0)
        qt[ch, p] = (qp * jnp.exp(bp - c_row)).astype(BF16)
        qd[ch, p] = qp * jnp.exp(bp)
        b_last = bp[c - 1:c, :]
        dlast[ch, p] = jnp.exp(b_last)
        kd[ch, p] = kp * jnp.exp(b_last - bp)
        kts = []
        for i in range(nsub):
            e = jnp.exp(jnp.where(sub_valid[i], bp[SUB * i:SUB * i + 1, :] - bp, 0.0))
            kts.append(jnp.where(sub_valid[i], kp * e, 0.0))
        kstack[ch, p] = jnp.concatenate(kts, axis=0)

    cph = [(ch, p, hh) for ch, p in cp for hh in range(2)]
    vh = {(ch, p, hh): v_ref[rows[ch], 128 * (2 * p + hh):128 * (2 * p + hh + 1)] for ch, p, hh in cph}
    r_mat = {(ch, p, hh): lax.dot_general(qt[ch, p], jnp.where(lane_masks[hh], kstack[ch, p], 0.0).astype(BF16),
                                          _NT, preferred_element_type=F32) for ch, p, hh in cph}
    o_intra = {k: jnp.dot(jnp.where(amask, r_mat[k], 0.0).astype(BF16), jnp.concatenate([vh[k]] * nsub, axis=0),
                          preferred_element_type=F32) for k in cph}
    upd = {(ch, p, hh): lax.dot_general(vh[ch, p, hh], jnp.where(lane_masks[hh], kd[ch, p], 0.0).astype(BF16), _TN,
                                        preferred_element_type=F32) for ch, p, hh in cph}

    st = [st_ref[p] for p in range(2)]
    for ch in range(nch):
        st_b = [s.astype(BF16) for s in st]
        o_inter = {(p, hh): lax.dot_general(jnp.where(lane_masks[hh], qd[ch, p], 0.0).astype(BF16), st_b[p], _NT,
                                            preferred_element_type=F32) for p in range(2) for hh in range(2)}
        st = [st[p] * dlast[ch, p] + upd[ch, p, 0] + upd[ch, p, 1] for p in range(2)]
        for p in range(2):
            for hh in range(2):
                h = 2 * p + hh
                o = o_intra[ch, p, hh] + o_inter[p, hh]
                o = o * lax.rsqrt(jnp.mean(o * o, axis=-1, keepdims=True) + RMS_EPS) * gnorm
                gate = _silu(r_ref[rows[ch], 128 * h:128 * (h + 1)].astype(F32))
                o_ref[rows[ch], 128 * h:128 * (h + 1)] = (o * gate).astype(BF16)
    for p in range(2):
        st_ref[p] = st[p]


def _gla(main, aux, wgk_pad, b_gk, gla_norm_g, *, bsz, seq, ts=256):
    nt = seq // ts
    row = lambda b, t: b * nt + t
    return pl.pallas_call(
        functools.partial(_gla_kernel, ts=ts),
        grid=(bsz, nt),
        in_specs=[pl.BlockSpec((ts, 256), lambda b, t: (row(b, t), 0)),
                  pl.BlockSpec((ts, 256), lambda b, t: (row(b, t), 1)),
                  pl.BlockSpec((ts, 512), lambda b, t: (row(b, t), 1)),
                  pl.BlockSpec((ts, 512), lambda b, t: (row(b, t), 2)),
                  pl.BlockSpec((ts, AUX_W), lambda b, t: (row(b, t), 0)),
                  pl.BlockSpec((AUX_W, 256), lambda b, t: (0, 0)),
                  pl.BlockSpec((1, 256), lambda b, t: (0, 0)),
                  pl.BlockSpec((1, 128), lambda b, t: (0, 0))],
        out_specs=pl.BlockSpec((ts, GLA_WIDTH), lambda b, t: (row(b, t), 0)),
        out_shape=jax.ShapeDtypeStruct((bsz * seq, GLA_WIDTH), BF16),
        scratch_shapes=[pltpu.VMEM((2, 128, 128), F32)],
        compiler_params=_cparams("parallel", "arbitrary"),
        name="gla",
    )(main, main, main, main, aux, wgk_pad, b_gk, gla_norm_g)


CONV_PAD = 8


def _mm(a, b):
    return jnp.dot(a.astype(BF16), b.astype(BF16), preferred_element_type=F32)


def _gdn_kernel(q_ref, k_ref, v_ref, z_ref, aux_ref, cw_ref, alog_ref, dtb_ref, g_ref, o_ref,
                st_ref, xe_ref, *, ts):
    c = CHUNK
    nch = ts // c
    hw = GDN_HEADS * GDN_DK

    @pl.when(pl.program_id(1) == 0)
    def _():
        st_ref[...] = jnp.zeros_like(st_ref)
        xe_ref[0:CONV_PAD, :] = jnp.zeros((CONV_PAD, 3 * hw), F32)

    xe_ref[CONV_PAD:, 0:hw] = q_ref[...].astype(F32)
    xe_ref[CONV_PAD:, hw:2 * hw] = k_ref[...].astype(F32)
    xe_ref[CONV_PAD:, 2 * hw:3 * hw] = v_ref[...].astype(F32)

    aux = aux_ref[...]
    beta_full = _sigmoid(aux)
    g_full = -jnp.exp(alog_ref[...]) * _softplus(aux + dtb_ref[...])
    ri = lax.broadcasted_iota(jnp.int32, (ts, ts), 0)
    ci = lax.broadcasted_iota(jnp.int32, (ts, ts), 1)
    cum = (((ri // c) == (ci // c)) & (ci <= ri)).astype(F32)
    gam_full = jnp.dot(cum, g_full, precision=HIGHEST, preferred_element_type=F32)
    gam_t = gam_full.T
    gnorm = g_ref[...]

    row = lax.broadcasted_iota(jnp.int32, (c, 2 * c), 0)
    lane = lax.broadcasted_iota(jnp.int32, (c, 2 * c), 1)
    left = lane < c
    col = lane % c
    causal = col <= row
    strict = col < row
    same32 = (row // 32) == (col // 32)
    same16 = (row // 16) == (col // 16)
    lvl0 = same16 & strict
    lvl1 = same32 & jnp.logical_not(same16) & strict
    lvl2 = jnp.logical_not(same32) & strict

    def pack_rows(x2):
        return jnp.where(left, x2[0:c], x2[c:2 * c])

    def pack_cols(x1):
        return jnp.where(left, x1[0:c], x1[c:2 * c])

    def halves(x):
        return jnp.where(left, x, 0.0).astype(BF16), jnp.where(left, 0.0, x).astype(BF16)

    def mmp(x, y):
        return jnp.dot(x.astype(BF16), jnp.concatenate(halves(y), axis=0), preferred_element_type=F32)

    def conv(col):
        acc = jnp.zeros((ts, 128), F32)
        for i in range(CONV_W):
            lo = CONV_PAD - (CONV_W - 1) + i
            acc = acc + xe_ref[lo:lo + ts, col:col + 128] * cw_ref[i:i + 1, col:col + 128]
        return _silu(acc)

    heads = range(GDN_HEADS)
    npair = nch // 2
    items = [(h, pr) for h in heads for pr in range(npair)]
    prow = [slice(2 * c * pr, 2 * c * (pr + 1)) for pr in range(npair)]

    qn, kn, beta, gam, gam_r, kb, rhs, rhs_b, qd, kd = ([None] * GDN_HEADS for _ in range(10))
    for h in heads:
        qh = conv(128 * h)
        kh = conv(hw + 128 * h)
        vh = conv(2 * hw + 128 * h)
        qn[h] = qh * lax.rsqrt(jnp.sum(qh * qh, axis=-1, keepdims=True) + RMS_EPS) * (GDN_DK ** -0.5)
        kn[h] = kh * lax.rsqrt(jnp.sum(kh * kh, axis=-1, keepdims=True) + RMS_EPS)
        beta[h] = beta_full[:, AUX_BB + h:AUX_BB + h + 1]
        gam[h] = gam_full[:, AUX_AB + h:AUX_AB + h + 1]
        gam_r[h] = gam_t[AUX_AB + h:AUX_AB + h + 1, :]
        egam = jnp.exp(gam[h])
        kb[h] = kn[h].astype(BF16)
        rhs[h] = jnp.concatenate([beta[h] * vh, beta[h] * egam * kn[h]], axis=1)
        rhs_b[h] = rhs[h].astype(BF16)
        qd[h] = (qn[h] * egam).astype(BF16)
        gl_rows = jnp.concatenate(
            [jnp.broadcast_to(gam[h][c * (j + 1) - 1:c * (j + 1), :], (c, 1)) for j in range(nch)], axis=0)
        kd[h] = (kn[h] * jnp.exp(gl_rows - gam[h])).astype(BF16)

    kk = [pack_rows(lax.dot_general(kb[h][prow[pr]], kb[h][prow[pr]], _NT, preferred_element_type=F32))
          for h, pr in items]
    qk = [pack_rows(lax.dot_general(qn[h][prow[pr]].astype(BF16), kb[h][prow[pr]], _NT,
                                    preferred_element_type=F32)) for h, pr in items]
    dec = [jnp.exp(jnp.where(causal, pack_cols(gam[h][prow[pr]]) - gam_r[h][:, prow[pr]], 0.0)) for h, pr in items]
    a = [pack_cols(beta[h][prow[pr]]) * kk[i] * dec[i] for i, (h, pr) in enumerate(items)]
    attn = [halves(jnp.where(causal, qk[i] * dec[i], 0.0)) for i in range(len(items))]

    x1 = [jnp.where(lvl0, -ai, 0.0) for ai in a]
    x2 = [mmp(x, x) for x in x1]
    y = [mmp(x, xx) for x, xx in zip(x1, x2)]
    p = [x + xx + yy for x, xx, yy in zip(x1, x2, y)]
    xk = x2
    for _ in range(2):
        xk = [mmp(x, x) for x in xk]
        y = [mmp(pp, x) for pp, x in zip(p, xk)]
        p = [pp + x + yy for pp, x, yy in zip(p, xk, y)]
    for lvl in (lvl1, lvl2):
        low = [jnp.where(lvl, ai, 0.0) for ai in a]
        y = [mmp(pp, lo) for pp, lo in zip(p, low)]
        zz = [lo + yy for lo, yy in zip(low, y)]
        y = [mmp(z, pp) for z, pp in zip(zz, p)]
        p = [pp - z - yy for pp, z, yy in zip(p, zz, y)]

    us = [[None] * nch for _ in heads]
    ws = [[None] * nch for _ in heads]
    for i, (h, pr) in enumerate(items):
        for half, p_half in enumerate(halves(p[i])):
            j = 2 * pr + half
            uw = rhs[h][c * j:c * (j + 1)] + jnp.dot(p_half, rhs_b[h][prow[pr]], preferred_element_type=F32)
            us[h][j] = uw[:, :GDN_DV]
            ws[h][j] = uw[:, GDN_DV:].astype(BF16)

    s = [st_ref[h] for h in heads]
    vprev = [None] * GDN_HEADS
    for j in range(nch):
        rows = slice(c * j, c * (j + 1))
        sb = [s[h].astype(BF16) for h in heads]
        vn16 = [(us[h][j] - jnp.dot(ws[h][j], sb[h], preferred_element_type=F32)).astype(BF16) for h in heads]
        o_inter = [jnp.dot(qd[h][rows], sb[h], preferred_element_type=F32) for h in heads]
        for h in heads:
            vpair = jnp.concatenate([vn16[h], jnp.zeros_like(vn16[h])] if j % 2 == 0 else [vprev[h], vn16[h]], axis=0)
            o = o_inter[h] + jnp.dot(attn[h * npair + j // 2][j % 2], vpair, preferred_element_type=F32)
            s[h] = jnp.exp(gam[h][c * (j + 1) - 1:c * (j + 1), :]) * s[h] + lax.dot_general(
                kd[h][rows], vn16[h], _TN, preferred_element_type=F32)
            o = o * lax.rsqrt(jnp.mean(o * o, axis=-1, keepdims=True) + RMS_EPS) * gnorm
            gate = _silu(z_ref[rows, 128 * h:128 * (h + 1)].astype(F32))
            o_ref[rows, 128 * h:128 * (h + 1)] = (o * gate).astype(BF16)
        vprev = vn16
    for h in heads:
        st_ref[h] = s[h]

    xe_ref[0:CONV_PAD, :] = xe_ref[ts:ts + CONV_PAD, :]


def _gdn(main, aux, conv_w, alog_pad, dtb_pad, gdn_norm_g, *, bsz, seq, ts=256):
    nt = seq // ts
    row = lambda b, t: b * nt + t
    hw = GDN_HEADS * GDN_DK
    return pl.pallas_call(
        functools.partial(_gdn_kernel, ts=ts),
        grid=(bsz, nt),
        in_specs=[pl.BlockSpec((ts, hw), lambda b, t: (row(b, t), 3)),
                  pl.BlockSpec((ts, hw), lambda b, t: (row(b, t), 4)),
                  pl.BlockSpec((ts, hw), lambda b, t: (row(b, t), 5)),
                  pl.BlockSpec((ts, hw), lambda b, t: (row(b, t), 6)),
                  pl.BlockSpec((ts, AUX_W), lambda b, t: (row(b, t), 0)),
                  pl.BlockSpec((CONV_W, 3 * hw), lambda b, t: (0, 0)),
                  pl.BlockSpec((1, AUX_W), lambda b, t: (0, 0)),
                  pl.BlockSpec((1, AUX_W), lambda b, t: (0, 0)),
                  pl.BlockSpec((1, GDN_DV), lambda b, t: (0, 0))],
        out_specs=pl.BlockSpec((ts, GDN_WIDTH), lambda b, t: (row(b, t), 0)),
        out_shape=jax.ShapeDtypeStruct((bsz * seq, GDN_WIDTH), BF16),
        scratch_shapes=[pltpu.VMEM((GDN_HEADS, GDN_DK, GDN_DV), F32),
                        pltpu.VMEM((ts + CONV_PAD, 3 * hw), F32)],
        compiler_params=_cparams("parallel", "arbitrary"),
        name="gdn",
    )(main, main, main, main, aux, conv_w, alog_pad, dtb_pad, gdn_norm_g)


def _pad_aux_row(v, offset):
    return jnp.zeros((1, AUX_W), F32).at[0, offset:offset + v.shape[0]].set(v.astype(F32))


ROUTE_W = 128
R_E1, R_E2, R_RANK1, R_RANK2, R_G1, R_G2 = range(6)
ROUTER_GROUP_COL = N_EXPERTS


def _layer_norm(h, g, b):
    mu = jnp.mean(h, axis=-1, keepdims=True)
    hc = h - mu
    var = jnp.mean(hc * hc, axis=-1, keepdims=True)
    return hc * lax.rsqrt(var + LN_EPS) * g + b


def _pack_bf16_pairs(x):
    w = x.shape[1] // 2
    lo = lax.bitcast_convert_type(x[:, :w].astype(BF16).astype(F32), jnp.uint32)
    hi = lax.bitcast_convert_type(x[:, w:].astype(BF16).astype(F32), jnp.uint32)
    return (lo >> 16) | (hi & jnp.uint32(0xFFFF0000))


def _unpack_bf16_pairs(p):
    lo = lax.bitcast_convert_type(p << 16, F32)
    hi = lax.bitcast_convert_type(p & jnp.uint32(0xFFFF0000), F32)
    return lo, hi


def _post_mix_kernel(oa_ref, ob_ref, x_ref, wo_ref, g_ref, b_ref, wr_ref, br_ref,
                     x1p_ref, route_ref, cnt_ref, carry_ref, *, tm):
    @pl.when(pl.program_id(0) == 0)
    def _():
        carry_ref[...] = jnp.zeros_like(carry_ref)

    y = jnp.dot(oa_ref[...], wo_ref[0:GLA_WIDTH, :], preferred_element_type=F32)
    y = y + jnp.dot(ob_ref[...], wo_ref[GLA_WIDTH:, :], preferred_element_type=F32)
    x1 = _layer_norm(ALPHA_DN * x_ref[...] + y, g_ref[...], b_ref[...])
    x1p_ref[...] = _pack_bf16_pairs(x1)

    xh = x1.astype(BF16)
    xl = (x1 - xh.astype(F32)).astype(BF16)
    hw = jnp.dot(xh, wr_ref[...], preferred_element_type=F32)
    logits = (hw[:, :ROUTE_W] + hw[:, ROUTE_W:]
              + jnp.dot(xl, wr_ref[:, :ROUTE_W], preferred_element_type=F32) + br_ref[...])
    lane = lax.broadcasted_iota(jnp.int32, (tm, ROUTE_W), 1)
    big = jnp.int32(1 << 20)
    neg = jnp.float32(-jnp.inf)

    def first_argmax(vals):
        m = jnp.max(vals, axis=-1, keepdims=True)
        idx = jnp.min(jnp.where(vals == m, lane, big), axis=-1, keepdims=True)
        return m, idx

    is_group = (lane >= ROUTER_GROUP_COL) & (lane < ROUTER_GROUP_COL + N_GROUPS)
    gl = jnp.where(is_group, logits, neg)
    gmax, gidx = first_argmax(gl)
    p_group = 1.0 / jnp.sum(jnp.exp(gl - gmax), axis=-1, keepdims=True)
    g_sel = gidx - ROUTER_GROUP_COL
    el = jnp.where((lane // EXPERTS_PER_GROUP) == g_sel, logits, neg)
    t1, e1 = first_argmax(el)
    t2, e2 = first_argmax(jnp.where(lane == e1, neg, el))
    ex = jnp.exp(t2 - t1)
    w1 = 1.0 / (1.0 + ex)
    gate1 = p_group * w1
    gate2 = p_group * (ex * w1)

    oh1 = lane == e1
    oh2 = lane == e2
    ri = lax.broadcasted_iota(jnp.int32, (tm, tm), 0)
    ci = lax.broadcasted_iota(jnp.int32, (tm, tm), 1)
    lstrict = (ci < ri).astype(BF16)
    oh1f = oh1.astype(F32)
    oh2f = oh2.astype(F32)
    c1 = jnp.dot(lstrict, oh1f.astype(BF16), preferred_element_type=F32)
    c2 = jnp.dot(lstrict, oh2f.astype(BF16), preferred_element_type=F32)
    carry = carry_ref[...]
    tot1 = jnp.sum(oh1f, axis=0, keepdims=True)
    tot2 = jnp.sum(oh2f, axis=0, keepdims=True)
    rank1 = jnp.sum(jnp.where(oh1, c1 + carry, 0.0), axis=-1, keepdims=True)
    rank2 = jnp.sum(jnp.where(oh2, c2 + (carry + tot1), 0.0), axis=-1, keepdims=True)
    new_carry = carry + tot1 + tot2
    carry_ref[...] = new_carry
    cnt_ref[...] = new_carry

    cols = (e1.astype(F32), e2.astype(F32), rank1, rank2, gate1, gate2)
    route = jnp.zeros((tm, ROUTE_W), F32)
    for j, col in enumerate(cols):
        route = jnp.where(lane == j, col, route)
    route_ref[...] = route


def _post_mix(o_a, o_b, x2d, w_out_b, ln_g, ln_b, w_router, b_router, *, tm=512):
    n = x2d.shape[0]
    return pl.pallas_call(
        functools.partial(_post_mix_kernel, tm=tm),
        grid=(n // tm,),
        in_specs=[pl.BlockSpec((tm, GLA_WIDTH), lambda i: (i, 0)),
                  pl.BlockSpec((tm, GDN_WIDTH), lambda i: (i, 0)),
                  pl.BlockSpec((tm, D_MODEL), lambda i: (i, 0)),
                  pl.BlockSpec((GLA_WIDTH + GDN_WIDTH, D_MODEL), lambda i: (0, 0)),
                  pl.BlockSpec((1, D_MODEL), lambda i: (0, 0)),
                  pl.BlockSpec((1, D_MODEL), lambda i: (0, 0)),
                  pl.BlockSpec((D_MODEL, 2 * ROUTE_W), lambda i: (0, 0)),
                  pl.BlockSpec((1, ROUTE_W), lambda i: (0, 0))],
        out_specs=[pl.BlockSpec((tm, D_MODEL // 2), lambda i: (i, 0)),
                   pl.BlockSpec((tm, ROUTE_W), lambda i: (i, 0)),
                   pl.BlockSpec((1, ROUTE_W), lambda i: (0, 0))],
        out_shape=[jax.ShapeDtypeStruct((n, D_MODEL // 2), jnp.uint32),
                   jax.ShapeDtypeStruct((n, ROUTE_W), F32),
                   jax.ShapeDtypeStruct((1, ROUTE_W), F32)],
        scratch_shapes=[pltpu.VMEM((1, ROUTE_W), F32)],
        compiler_params=_cparams("arbitrary"),
        name="post_mix",
    )(o_a, o_b, x2d, w_out_b, ln_g, ln_b, w_router, b_router)


EXPERT_BLOCK = 256


HALF = D_MODEL // 2
ROW_UNROLL = 8


def _plan_kernel(route_ref, cnt_ref, dest_ref, *, tm, tb):
    counts = cnt_ref[...].astype(jnp.int32)
    padded = ((counts + (tb - 1)) & jnp.int32(-tb)).astype(F32)
    ri = lax.broadcasted_iota(jnp.int32, (ROUTE_W, ROUTE_W), 0)
    ci = lax.broadcasted_iota(jnp.int32, (ROUTE_W, ROUTE_W), 1)
    pstart = jnp.dot(jnp.broadcast_to(padded, (8, ROUTE_W)), (ri < ci).astype(F32),
                     precision=HIGHEST, preferred_element_type=F32)[0:1, :]
    route = route_ref[...]
    lane = lax.broadcasted_iota(jnp.int32, (tm, ROUTE_W), 1).astype(F32)
    out = jnp.zeros((tm, ROUTE_W), F32)
    for k, (ce, cr) in enumerate(((R_E1, R_RANK1), (R_E2, R_RANK2))):
        off = jnp.sum(jnp.where(lane == route[:, ce:ce + 1], pstart, 0.0), axis=-1, keepdims=True)
        out = jnp.where(lane == float(k), off + route[:, cr:cr + 1], out)
    dest_ref[...] = out.T[0:8, :].astype(jnp.int32)


def _plan(route, counts, *, tb, tm=2048):
    n = route.shape[0]
    tm = min(tm, n)
    return pl.pallas_call(
        functools.partial(_plan_kernel, tm=tm, tb=tb),
        grid=(n // tm,),
        in_specs=[pl.BlockSpec((tm, ROUTE_W), lambda i: (i, 0)),
                  pl.BlockSpec((1, ROUTE_W), lambda i: (0, 0))],
        out_specs=pl.BlockSpec((8, tm), lambda i: (0, i)),
        out_shape=jax.ShapeDtypeStruct((8, n), jnp.int32),
        compiler_params=_cparams("parallel"),
        name="plan",
    )(route, counts)


def _dispatch_kernel(d1_ref, d2_ref, xp_ref, xs_init_ref, xs_ref, sem, *, tm):
    del xs_init_ref
    base = pl.program_id(0) * tm

    def body(j, carry):
        for u in range(ROW_UNROLL):
            r = j * ROW_UNROLL + u
            for k, d_ref in enumerate((d1_ref, d2_ref)):
                pltpu.make_async_copy(xp_ref.at[pl.ds(r, 1), :], xs_ref.at[pl.ds(d_ref[base + r], 1), :],
                                      sem).start(priority=k)
        return carry
    lax.fori_loop(0, tm // ROW_UNROLL, body, 0)
    for _ in range(TOP_K):
        pltpu.make_async_copy(xp_ref, xs_ref.at[pl.ds(0, tm), :], sem).wait()


def _dispatch(dest1, dest2, x1p, n_rows, *, tm=512):
    n = x1p.shape[0]
    grid_spec = pltpu.PrefetchScalarGridSpec(
        num_scalar_prefetch=2,
        grid=(n // tm,),
        in_specs=[pl.BlockSpec((tm, HALF), lambda i, a, b: (i, 0)),
                  pl.BlockSpec(memory_space=pl.ANY)],
        out_specs=pl.BlockSpec(memory_space=pl.ANY),
        scratch_shapes=[pltpu.SemaphoreType.DMA(())],
    )
    return pl.pallas_call(
        functools.partial(_dispatch_kernel, tm=tm),
        grid_spec=grid_spec,
        out_shape=jax.ShapeDtypeStruct((n_rows, HALF), jnp.uint32),
        input_output_aliases={3: 0},
        compiler_params=_cparams("arbitrary"),
        name="dispatch",
    )(dest1, dest2, x1p, jnp.zeros((n_rows, HALF), jnp.uint32))


def _expert_kernel(blk_e_ref, first_ref, next_e_ref, nvalid_ref, xs_ref, wg_hbm, wu_hbm, wd_hbm, y_ref,
                   wgs, wus, wds, wgb, wub, wdb, sem):
    i = pl.program_id(0)

    def weight_copies(e):
        return [pltpu.make_async_copy(src.at[e], dst, sem.at[j])
                for j, (src, dst) in enumerate(((wg_hbm, wgs), (wu_hbm, wus), (wd_hbm, wds)))]

    @pl.when((i == 0) & (nvalid_ref[0] > 0))
    def _():
        for cp in weight_copies(blk_e_ref[0]):
            cp.start()

    @pl.when(i < nvalid_ref[0])
    def _():
        @pl.when(first_ref[i] == 1)
        def _():
            for cp in weight_copies(blk_e_ref[i]):
                cp.wait()
            wgb[...] = wgs[...].astype(BF16)
            wub[...] = wus[...].astype(BF16)
            wdb[...] = wds[...].astype(BF16)

            @pl.when(next_e_ref[i] >= 0)
            def _():
                for cp in weight_copies(next_e_ref[i]):
                    cp.start()

        lo, hi = _unpack_bf16_pairs(xs_ref[...])
        lo = lo.astype(BF16)
        hi = hi.astype(BF16)
        hg = jnp.dot(lo, wgb[0:HALF, :], preferred_element_type=F32)
        hg = hg + jnp.dot(hi, wgb[HALF:, :], preferred_element_type=F32)
        hu = jnp.dot(lo, wub[0:HALF, :], preferred_element_type=F32)
        hu = hu + jnp.dot(hi, wub[HALF:, :], preferred_element_type=F32)
        hid = (_silu(hg) * hu).astype(BF16)
        y_ref[...] = _pack_bf16_pairs(jnp.dot(hid, wdb[...], preferred_element_type=F32))

    @pl.when(i >= nvalid_ref[0])
    def _():
        y_ref[...] = jnp.zeros_like(y_ref)


def _experts(blk_e, first, next_e, nvalid, xs, w_gate, w_up, w_down, *, tb=EXPERT_BLOCK):
    nb = blk_e.shape[0]
    grid_spec = pltpu.PrefetchScalarGridSpec(
        num_scalar_prefetch=4,
        grid=(nb,),
        in_specs=[pl.BlockSpec((tb, HALF), lambda i, be, fr, ne, nv: (jnp.clip(i, 0, jnp.maximum(nv[0] - 1, 0)), 0)),
                  pl.BlockSpec(memory_space=pl.ANY),
                  pl.BlockSpec(memory_space=pl.ANY),
                  pl.BlockSpec(memory_space=pl.ANY)],
        out_specs=pl.BlockSpec((tb, HALF), lambda i, be, fr, ne, nv: (i, 0)),
        scratch_shapes=[pltpu.VMEM((D_MODEL, D_EXPERT), F32),
                        pltpu.VMEM((D_MODEL, D_EXPERT), F32),
                        pltpu.VMEM((D_EXPERT, D_MODEL), F32),
                        pltpu.VMEM((D_MODEL, D_EXPERT), BF16),
                        pltpu.VMEM((D_MODEL, D_EXPERT), BF16),
                        pltpu.VMEM((D_EXPERT, D_MODEL), BF16),
                        pltpu.SemaphoreType.DMA((3,))],
    )
    return pl.pallas_call(
        _expert_kernel,
        grid_spec=grid_spec,
        out_shape=jax.ShapeDtypeStruct((nb * tb, HALF), jnp.uint32),
        compiler_params=_cparams("arbitrary"),
        name="experts",
    )(blk_e, first, next_e, nvalid, xs, w_gate, w_up, w_down)


def _combine_kernel(d1_ref, d2_ref, y_hbm, x1_ref, route_ref, g_ref, b_ref, o_ref, ybuf, sem, *, tt):
    i = pl.program_id(0)
    nsteps = pl.num_programs(0)
    slot = i % 2

    def issue(tile, slot_):
        def body(j, carry):
            for u in range(ROW_UNROLL):
                r = j * ROW_UNROLL + u
                for k, d_ref in enumerate((d1_ref, d2_ref)):
                    pltpu.make_async_copy(y_hbm.at[pl.ds(d_ref[tile * tt + r], 1), :],
                                          ybuf.at[slot_, k, pl.ds(r, 1), :], sem.at[slot_]).start(priority=k)
            return carry
        lax.fori_loop(0, tt // ROW_UNROLL, body, 0)

    @pl.when(i == 0)
    def _():
        issue(0, 0)

    @pl.when(i + 1 < nsteps)
    def _():
        issue(i + 1, 1 - slot)

    for k in range(TOP_K):
        pltpu.make_async_copy(y_hbm.at[pl.ds(0, tt), :], ybuf.at[slot, k], sem.at[slot]).wait()
    route = route_ref[...]
    g1 = route[:, R_G1:R_G1 + 1]
    g2 = route[:, R_G2:R_G2 + 1]
    lo1, hi1 = _unpack_bf16_pairs(ybuf[slot, 0])
    lo2, hi2 = _unpack_bf16_pairs(ybuf[slot, 1])
    x_lo, x_hi = _unpack_bf16_pairs(x1_ref[...])
    h_lo = ALPHA_DN * x_lo + (g1 * lo1 + g2 * lo2)
    h_hi = ALPHA_DN * x_hi + (g1 * hi1 + g2 * hi2)
    mu = (jnp.sum(h_lo, axis=-1, keepdims=True) + jnp.sum(h_hi, axis=-1, keepdims=True)) * (1.0 / D_MODEL)
    c_lo = h_lo - mu
    c_hi = h_hi - mu
    var = (jnp.sum(c_lo * c_lo, axis=-1, keepdims=True) + jnp.sum(c_hi * c_hi, axis=-1, keepdims=True)) * (1.0 / D_MODEL)
    inv = lax.rsqrt(var + LN_EPS)
    o_ref[:, 0:HALF] = c_lo * inv * g_ref[:, 0:HALF] + b_ref[:, 0:HALF]
    o_ref[:, HALF:] = c_hi * inv * g_ref[:, HALF:] + b_ref[:, HALF:]


def _combine(dest1, dest2, ybuf, x1p, route, ln_g, ln_b, *, tt=256):
    n = x1p.shape[0]
    grid_spec = pltpu.PrefetchScalarGridSpec(
        num_scalar_prefetch=2,
        grid=(n // tt,),
        in_specs=[pl.BlockSpec(memory_space=pl.ANY),
                  pl.BlockSpec((tt, HALF), lambda i, a, b: (i, 0)),
                  pl.BlockSpec((tt, ROUTE_W), lambda i, a, b: (i, 0)),
                  pl.BlockSpec((1, D_MODEL), lambda i, a, b: (0, 0)),
                  pl.BlockSpec((1, D_MODEL), lambda i, a, b: (0, 0))],
        out_specs=pl.BlockSpec((tt, D_MODEL), lambda i, a, b: (i, 0)),
        scratch_shapes=[pltpu.VMEM((2, TOP_K, tt, HALF), jnp.uint32),
                        pltpu.SemaphoreType.DMA((2,))],
    )
    return pl.pallas_call(
        functools.partial(_combine_kernel, tt=tt),
        grid_spec=grid_spec,
        out_shape=jax.ShapeDtypeStruct((n, D_MODEL), F32),
        compiler_params=_cparams("arbitrary"),
        name="combine",
    )(dest1, dest2, ybuf, x1p, route, ln_g, ln_b)


def _block_plan(counts_row, n, tb):
    counts = counts_row[0, :N_EXPERTS].astype(jnp.int32)
    pends = jnp.cumsum((counts + tb - 1) // tb * tb)
    nb = (n * TOP_K) // tb + N_EXPERTS
    nvalid = (pends[-1] // tb).astype(jnp.int32)
    blk = jnp.arange(nb, dtype=jnp.int32)
    blk_e = jnp.minimum(jnp.searchsorted(pends, blk * tb, side='right'), N_EXPERTS - 1).astype(jnp.int32)
    blk_e = jnp.where(blk < nvalid, blk_e, blk_e[jnp.maximum(nvalid - 1, 0)])
    first = jnp.concatenate([jnp.ones((1,), jnp.int32), (blk_e[1:] != blk_e[:-1]).astype(jnp.int32)])
    ids = jnp.arange(N_EXPERTS, dtype=jnp.int32)
    later = lax.cummin(jnp.where(counts > 0, ids, N_EXPERTS), axis=0, reverse=True)
    nxt = jnp.concatenate([later[1:], jnp.full((1,), N_EXPERTS, jnp.int32)])
    nxt = jnp.where(nxt < N_EXPERTS, nxt, -1)
    return blk_e, first, nxt[blk_e], nvalid.reshape(1)


def kernel(x, w_in, w_gk_up, b_gk, conv_w, a_log, dt_bias, gla_norm_g, gdn_norm_g, w_out, ln1_g, ln1_b, w_router_group, b_router_group, w_router_expert, b_router_expert, w_gate, w_up, w_down, ln2_g, ln2_b):
    bsz, seq, d = x.shape
    n = bsz * seq
    x2d = x.reshape(n, d)
    w_main, w_aux = _split_w_in(w_in[0])
    main, aux = _in_proj(x2d, w_main, w_aux)
    wgk = jnp.zeros((AUX_W, 256), F32).at[:GLA_GATE_RANK].set(w_gk_up[0])
    o_a = _gla(main, aux, wgk, b_gk[0][None], gla_norm_g[0][None], bsz=bsz, seq=seq)
    o_b = _gdn(main, aux, conv_w[0], _pad_aux_row(a_log[0], AUX_AB), _pad_aux_row(dt_bias[0], AUX_AB),
               gdn_norm_g[0][None], bsz=bsz, seq=seq)
    rpad = jnp.zeros((d, ROUTE_W - N_EXPERTS - N_GROUPS), F32)
    w_router = jnp.concatenate([w_router_expert[0], w_router_group[0], rpad], axis=1)
    w_router_hi = w_router.astype(BF16)
    w_router = jnp.concatenate([w_router_hi, (w_router - w_router_hi.astype(F32)).astype(BF16)], axis=1)
    b_router = jnp.concatenate([b_router_expert[0], b_router_group[0], rpad[0]])[None]
    x1p, route, counts = _post_mix(o_a, o_b, x2d, w_out[0].astype(BF16), ln1_g[0][None], ln1_b[0][None],
                                       w_router, b_router)
    dest = _plan(route, counts, tb=EXPERT_BLOCK)
    blk_e, first, next_e, nvalid = _block_plan(counts, n, EXPERT_BLOCK)
    xs = _dispatch(dest[0], dest[1], x1p, blk_e.shape[0] * EXPERT_BLOCK)
    ybuf = _experts(blk_e, first, next_e, nvalid, xs, w_gate[0], w_up[0], w_down[0])
    out = _combine(dest[0], dest[1], ybuf, x1p, route, ln2_g[0][None], ln2_b[0][None])
    return out.reshape(bsz, seq, d)
```

```python
import functools

import jax
import jax.numpy as jnp
import numpy as np
from jax import lax
from jax.experimental import pallas as pl
from jax.experimental.pallas import tpu as pltpu

F32 = jnp.float32
BF16 = jnp.bfloat16
HIGHEST = lax.Precision.HIGHEST

D_MODEL = 1024
DEPTH = 1
GLA_HEADS = 4
GLA_DK = 64
GLA_DV = 128
GLA_WIDTH = GLA_HEADS * GLA_DV
GLA_GATE_RANK = 16
GLA_GATE_NORM = 16.0
GDN_HEADS = 4
GDN_DK = 128
GDN_DV = 128
GDN_WIDTH = GDN_HEADS * GDN_DV
CONV_W = 4
CHUNK = 64
N_GROUPS = 8
EXPERTS_PER_GROUP = 8
N_EXPERTS = N_GROUPS * EXPERTS_PER_GROUP
TOP_K = 2
D_EXPERT = 512
LN_EPS = 1e-5
RMS_EPS = 1e-6
ALPHA_DN = (2.0 * DEPTH) ** 0.25

MAIN_W = 2 * GLA_HEADS * GLA_DK + 2 * GLA_WIDTH + 4 * GDN_WIDTH
AUX_W = 128
AUX_LRA = 0
AUX_BB = GLA_GATE_RANK
AUX_AB = GLA_GATE_RANK + GDN_HEADS
SUB = 16
VMEM_LIMIT = 56 * 1024 * 1024


def _cparams(*sem):
    return pltpu.CompilerParams(dimension_semantics=sem, vmem_limit_bytes=VMEM_LIMIT)


def _sigmoid(x):
    return 1.0 / (1.0 + jnp.exp(-x))


def _silu(x):
    return x * _sigmoid(x)


def _log_sigmoid(x):
    return jnp.minimum(x, 0.0) - jnp.log(1.0 + jnp.exp(-jnp.abs(x)))


def _softplus(x):
    return jnp.maximum(x, 0.0) + jnp.log(1.0 + jnp.exp(-jnp.abs(x)))


def _in_proj_kernel(x_ref, wm_ref, wa_ref, main_ref, aux_ref, *, col_block):
    xb = x_ref[...].astype(BF16)
    for j in range(MAIN_W // col_block):
        sl = slice(j * col_block, (j + 1) * col_block)
        main_ref[:, sl] = jnp.dot(xb, wm_ref[:, sl], preferred_element_type=F32).astype(BF16)
    aux_ref[...] = jnp.dot(xb, wa_ref[...], preferred_element_type=F32)


def _in_proj(x2d, w_main, w_aux, *, tm=512, col_block=512):
    n = x2d.shape[0]
    return pl.pallas_call(
        functools.partial(_in_proj_kernel, col_block=col_block),
        grid=(n // tm,),
        in_specs=[pl.BlockSpec((tm, D_MODEL), lambda i: (i, 0)),
                  pl.BlockSpec((D_MODEL, MAIN_W), lambda i: (0, 0)),
                  pl.BlockSpec((D_MODEL, AUX_W), lambda i: (0, 0))],
        out_specs=[pl.BlockSpec((tm, MAIN_W), lambda i: (i, 0)),
                   pl.BlockSpec((tm, AUX_W), lambda i: (i, 0))],
        out_shape=[jax.ShapeDtypeStruct((n, MAIN_W), BF16),
                   jax.ShapeDtypeStruct((n, AUX_W), F32)],
        compiler_params=_cparams("parallel"),
        name="in_proj",
    )(x2d, w_main, w_aux)


def _split_w_in(w_in):
    sizes = (GLA_HEADS * GLA_DK, GLA_HEADS * GLA_DK, GLA_WIDTH, GLA_WIDTH, GLA_GATE_RANK,
             GDN_HEADS * GDN_DK, GDN_HEADS * GDN_DK, GDN_WIDTH, GDN_WIDTH, GDN_HEADS, GDN_HEADS)
    offs = np.cumsum((0,) + sizes)
    seg = [w_in[:, offs[i]:offs[i + 1]] for i in range(len(sizes))]
    qa, ka, va, ra, lra, qb, kb, vb, zb, bb, ab = seg
    w_main = jnp.concatenate([qa, ka, va, ra, qb, kb, vb, zb], axis=1).astype(BF16)
    pad = jnp.zeros((w_in.shape[0], AUX_W - GLA_GATE_RANK - 2 * GDN_HEADS), w_in.dtype)
    w_aux = jnp.concatenate([lra, bb, ab, pad], axis=1).astype(BF16)
    return w_main, w_aux


_NT = (((1,), (1,)), ((), ()))
_TN = (((0,), (0,)), ((), ()))


def _gla_kernel(q_ref, k_ref, v_ref, r_ref, aux_ref, wgk_ref, bgk_ref, g_ref, o_ref, st_ref, *, ts):
    c = CHUNK
    nsub = c // SUB

    @pl.when(pl.program_id(1) == 0)
    def _():
        st_ref[...] = jnp.zeros_like(st_ref)

    pre = jnp.dot(aux_ref[...], wgk_ref[...], precision=HIGHEST, preferred_element_type=F32) + bgk_ref[...]
    gk_all = _log_sigmoid(pre) * (1.0 / GLA_GATE_NORM)

    ri = lax.broadcasted_iota(jnp.int32, (c, c), 0)
    ci = lax.broadcasted_iota(jnp.int32, (c, c), 1)
    tri = (ci <= ri).astype(F32)
    row128 = lax.broadcasted_iota(jnp.int32, (c, 128), 0)
    lane128 = lax.broadcasted_iota(jnp.int32, (1, 128), 1)
    lane_masks = [(lane128 // GLA_DK) == hh for hh in range(2)]
    sub_valid = [row128 < SUB * (i + 1) for i in range(nsub)]
    ar = lax.broadcasted_iota(jnp.int32, (c, nsub * c), 0)
    ac = lax.broadcasted_iota(jnp.int32, (c, nsub * c), 1)
    amask = ((ac // c) == (ar // SUB)) & ((ac % c) <= ar)
    scale = GLA_DK ** -0.5
    gnorm = g_ref[...]

    nch = ts // c
    rows = [slice(ch * c, (ch + 1) * c) for ch in range(nch)]
    b_all = [jnp.dot(tri, gk_all[r], precision=HIGHEST, preferred_element_type=F32) for r in rows]

    cp = [(ch, p) for ch in range(nch) for p in range(2)]
    qt, qd, kd, kstack, dlast = {}, {}, {}, {}, {}
    for ch, p in cp:
        lanes = slice(128 * p, 128 * (p + 1))
        bp = b_all[ch][:, lanes]
        qp = q_ref[rows[ch], lanes].astype(F32) * scale
        kp = k_ref[rows[ch], lanes].astype(F32)
        c_row = jnp.concatenate(
            [jnp.broadcast_to(bp[SUB * i:SUB * i + 1, :], (SUB, 128)) for i in range(nsub)], axis=0)
        qt[ch, p] = (qp * jnp.exp(bp - c_row)).astype(BF16)
        qd[ch, p] = qp * jnp.exp(bp)
        b_last = bp[c - 1:c, :]
        dlast[ch, p] = jnp.exp(b_last)
        kd[ch, p] = kp * jnp.exp(b_last - bp)
        kts = []
        for i in range(nsub):
            e = jnp.exp(jnp.where(sub_valid[i], bp[SUB * i:SUB * i + 1, :] - bp, 0.0))
            kts.append(jnp.where(sub_valid[i], kp * e, 0.0))
        kstack[ch, p] = jnp.concatenate(kts, axis=0)

    cph = [(ch, p, hh) for ch, p in cp for hh in range(2)]
    vh = {(ch, p, hh): v_ref[rows[ch], 128 * (2 * p + hh):128 * (2 * p + hh + 1)] for ch, p, hh in cph}
    r_mat = {(ch, p, hh): lax.dot_general(qt[ch, p], jnp.where(lane_masks[hh], kstack[ch, p], 0.0).astype(BF16),
                                          _NT, preferred_element_type=F32) for ch, p, hh in cph}
    o_intra = {k: jnp.dot(jnp.where(amask, r_mat[k], 0.0).astype(BF16), jnp.concatenate([vh[k]] * nsub, axis=0),
                          preferred_element_type=F32) for k in cph}
    upd = {(ch, p, hh): lax.dot_general(vh[ch, p, hh], jnp.where(lane_masks[hh], kd[ch, p], 0.0).astype(BF16), _TN,
                                        preferred_element_type=F32) for ch, p, hh in cph}

    st = [st_ref[p] for p in range(2)]
    for ch in range(nch):
        st_b = [s.astype(BF16) for s in st]
        o_inter = {(p, hh): lax.dot_general(jnp.where(lane_masks[hh], qd[ch, p], 0.0).astype(BF16), st_b[p], _NT,
                                            preferred_element_type=F32) for p in range(2) for hh in range(2)}
        st = [st[p] * dlast[ch, p] + upd[ch, p, 0] + upd[ch, p, 1] for p in range(2)]
        for p in range(2):
            for hh in range(2):
                h = 2 * p + hh
                o = o_intra[ch, p, hh] + o_inter[p, hh]
                o = o * lax.rsqrt(jnp.mean(o * o, axis=-1, keepdims=True) + RMS_EPS) * gnorm
                gate = _silu(r_ref[rows[ch], 128 * h:128 * (h + 1)].astype(F32))
                o_ref[rows[ch], 128 * h:128 * (h + 1)] = (o * gate).astype(BF16)
    for p in range(2):
        st_ref[p] = st[p]


def _gla(main, aux, wgk_pad, b_gk, gla_norm_g, *, bsz, seq, ts=256):
    nt = seq // ts
    row = lambda b, t: b * nt + t
    return pl.pallas_call(
        functools.partial(_gla_kernel, ts=ts),
        grid=(bsz, nt),
        in_specs=[pl.BlockSpec((ts, 256), lambda b, t: (row(b, t), 0)),
                  pl.BlockSpec((ts, 256), lambda b, t: (row(b, t), 1)),
                  pl.BlockSpec((ts, 512), lambda b, t: (row(b, t), 1)),
                  pl.BlockSpec((ts, 512), lambda b, t: (row(b, t), 2)),
                  pl.BlockSpec((ts, AUX_W), lambda b, t: (row(b, t), 0)),
                  pl.BlockSpec((AUX_W, 256), lambda b, t: (0, 0)),
                  pl.BlockSpec((1, 256), lambda b, t: (0, 0)),
                  pl.BlockSpec((1, 128), lambda b, t: (0, 0))],
        out_specs=pl.BlockSpec((ts, GLA_WIDTH), lambda b, t: (row(b, t), 0)),
        out_shape=jax.ShapeDtypeStruct((bsz * seq, GLA_WIDTH), BF16),
        scratch_shapes=[pltpu.VMEM((2, 128, 128), F32)],
        compiler_params=_cparams("parallel", "arbitrary"),
        name="gla",
    )(main, main, main, main, aux, wgk_pad, b_gk, gla_norm_g)


CONV_PAD = 8


def _mm(a, b):
    return jnp.dot(a.astype(BF16), b.astype(BF16), preferred_element_type=F32)


def _gdn_kernel(q_ref, k_ref, v_ref, z_ref, aux_ref, cw_ref, alog_ref, dtb_ref, g_ref, o_ref,
                st_ref, xe_ref, *, ts):
    c = CHUNK
    nch = ts // c
    hw = GDN_HEADS * GDN_DK

    @pl.when(pl.program_id(1) == 0)
    def _():
        st_ref[...] = jnp.zeros_like(st_ref)
        xe_ref[0:CONV_PAD, :] = jnp.zeros((CONV_PAD, 3 * hw), F32)

    xe_ref[CONV_PAD:, 0:hw] = q_ref[...].astype(F32)
    xe_ref[CONV_PAD:, hw:2 * hw] = k_ref[...].astype(F32)
    xe_ref[CONV_PAD:, 2 * hw:3 * hw] = v_ref[...].astype(F32)

    aux = aux_ref[...]
    beta_full = _sigmoid(aux)
    g_full = -jnp.exp(alog_ref[...]) * _softplus(aux + dtb_ref[...])
    ri = lax.broadcasted_iota(jnp.int32, (ts, ts), 0)
    ci = lax.broadcasted_iota(jnp.int32, (ts, ts), 1)
    cum = (((ri // c) == (ci // c)) & (ci <= ri)).astype(F32)
    gam_full = jnp.dot(cum, g_full, precision=HIGHEST, preferred_element_type=F32)
    gam_t = gam_full.T
    gnorm = g_ref[...]

    row = lax.broadcasted_iota(jnp.int32, (c, 2 * c), 0)
    lane = lax.broadcasted_iota(jnp.int32, (c, 2 * c), 1)
    left = lane < c
    col = lane % c
    causal = col <= row
    strict = col < row
    same32 = (row // 32) == (col // 32)
    same16 = (row // 16) == (col // 16)
    lvl0 = same16 & strict
    lvl1 = same32 & jnp.logical_not(same16) & strict
    lvl2 = jnp.logical_not(same32) & strict

    def pack_rows(x2):
        return jnp.where(left, x2[0:c], x2[c:2 * c])

    def pack_cols(x1):
        return jnp.where(left, x1[0:c], x1[c:2 * c])

    def halves(x):
        return jnp.where(left, x, 0.0).astype(BF16), jnp.where(left, 0.0, x).astype(BF16)

    def mmp(x, y):
        return jnp.dot(x.astype(BF16), jnp.concatenate(halves(y), axis=0), preferred_element_type=F32)

    def conv(col):
        acc = jnp.zeros((ts, 128), F32)
        for i in range(CONV_W):
            lo = CONV_PAD - (CONV_W - 1) + i
            acc = acc + xe_ref[lo:lo + ts, col:col + 128] * cw_ref[i:i + 1, col:col + 128]
        return _silu(acc)

    heads = range(GDN_HEADS)
    npair = nch // 2
    items = [(h, pr) for h in heads for pr in range(npair)]
    prow = [slice(2 * c * pr, 2 * c * (pr + 1)) for pr in range(npair)]

    qn, kn, beta, gam, gam_r, kb, rhs, rhs_b, qd, kd = ([None] * GDN_HEADS for _ in range(10))
    for h in heads:
        qh = conv(128 * h)
        kh = conv(hw + 128 * h)
        vh = conv(2 * hw + 128 * h)
        qn[h] = qh * lax.rsqrt(jnp.sum(qh * qh, axis=-1, keepdims=True) + RMS_EPS) * (GDN_DK ** -0.5)
        kn[h] = kh * lax.rsqrt(jnp.sum(kh * kh, axis=-1, keepdims=True) + RMS_EPS)
        beta[h] = beta_full[:, AUX_BB + h:AUX_BB + h + 1]
        gam[h] = gam_full[:, AUX_AB + h:AUX_AB + h + 1]
        gam_r[h] = gam_t[AUX_AB + h:AUX_AB + h + 1, :]
        egam = jnp.exp(gam[h])
        kb[h] = kn[h].astype(BF16)
        rhs[h] = jnp.concatenate([beta[h] * vh, beta[h] * egam * kn[h]], axis=1)
        rhs_b[h] = rhs[h].astype(BF16)
        qd[h] = (qn[h] * egam).astype(BF16)
        gl_rows = jnp.concatenate(
            [jnp.broadcast_to(gam[h][c * (j + 1) - 1:c * (j + 1), :], (c, 1)) for j in range(nch)], axis=0)
        kd[h] = (kn[h] * jnp.exp(gl_rows - gam[h])).astype(BF16)

    kk = [pack_rows(lax.dot_general(kb[h][prow[pr]], kb[h][prow[pr]], _NT, preferred_element_type=F32))
          for h, pr in items]
    qk = [pack_rows(lax.dot_general(qn[h][prow[pr]].astype(BF16), kb[h][prow[pr]], _NT,
                                    preferred_element_type=F32)) for h, pr in items]
    dec = [jnp.exp(jnp.where(causal, pack_cols(gam[h][prow[pr]]) - gam_r[h][:, prow[pr]], 0.0)) for h, pr in items]
    a = [pack_cols(beta[h][prow[pr]]) * kk[i] * dec[i] for i, (h, pr) in enumerate(items)]
    attn = [halves(jnp.where(causal, qk[i] * dec[i], 0.0)) for i in range(len(items))]

    x1 = [jnp.where(lvl0, -ai, 0.0) for ai in a]
    x2 = [mmp(x, x) for x in x1]
    y = [mmp(x, xx) for x, xx in zip(x1, x2)]
    p = [x + xx + yy for x, xx, yy in zip(x1, x2, y)]
    xk = x2
    for _ in range(2):
        xk = [mmp(x, x) for x in xk]
        y = [mmp(pp, x) for pp, x in zip(p, xk)]
        p = [pp + x + yy for pp, x, yy in zip(p, xk, y)]
    for lvl in (lvl1, lvl2):
        low = [jnp.where(lvl, ai, 0.0) for ai in a]
        y = [mmp(pp, lo) for pp, lo in zip(p, low)]
        zz = [lo + yy for lo, yy in zip(low, y)]
        y = [mmp(z, pp) for z, pp in zip(zz, p)]
        p = [pp - z - yy for pp, z, yy in zip(p, zz, y)]

    us = [[None] * nch for _ in heads]
    ws = [[None] * nch for _ in heads]
    for i, (h, pr) in enumerate(items):
        for half, p_half in enumerate(halves(p[i])):
            j = 2 * pr + half
            uw = rhs[h][c * j:c * (j + 1)] + jnp.dot(p_half, rhs_b[h][prow[pr]], preferred_element_type=F32)
            us[h][j] = uw[:, :GDN_DV]
            ws[h][j] = uw[:, GDN_DV:].astype(BF16)

    s = [st_ref[h] for h in heads]
    vprev = [None] * GDN_HEADS
    for j in range(nch):
        rows = slice(c * j, c * (j + 1))
        sb = [s[h].astype(BF16) for h in heads]
        vn16 = [(us[h][j] - jnp.dot(ws[h][j], sb[h], preferred_element_type=F32)).astype(BF16) for h in heads]
        o_inter = [jnp.dot(qd[h][rows], sb[h], preferred_element_type=F32) for h in heads]
        for h in heads:
            vpair = jnp.concatenate([vn16[h], jnp.zeros_like(vn16[h])] if j % 2 == 0 else [vprev[h], vn16[h]], axis=0)
            o = o_inter[h] + jnp.dot(attn[h * npair + j // 2][j % 2], vpair, preferred_element_type=F32)
            s[h] = jnp.exp(gam[h][c * (j + 1) - 1:c * (j + 1), :]) * s[h] + lax.dot_general(
                kd[h][rows], vn16[h], _TN, preferred_element_type=F32)
            o = o * lax.rsqrt(jnp.mean(o * o, axis=-1, keepdims=True) + RMS_EPS) * gnorm
            gate = _silu(z_ref[rows, 128 * h:128 * (h + 1)].astype(F32))
            o_ref[rows, 128 * h:128 * (h + 1)] = (o * gate).astype(BF16)
        vprev = vn16
    for h in heads:
        st_ref[h] = s[h]

    xe_ref[0:CONV_PAD, :] = xe_ref[ts:ts + CONV_PAD, :]


def _gdn(main, aux, conv_w, alog_pad, dtb_pad, gdn_norm_g, *, bsz, seq, ts=256):
    nt = seq // ts
    row = lambda b, t: b * nt + t
    hw = GDN_HEADS * GDN_DK
    return pl.pallas_call(
        functools.partial(_gdn_kernel, ts=ts),
        grid=(bsz, nt),
        in_specs=[pl.BlockSpec((ts, hw), lambda b, t: (row(b, t), 3)),
                  pl.BlockSpec((ts, hw), lambda b, t: (row(b, t), 4)),
                  pl.BlockSpec((ts, hw), lambda b, t: (row(b, t), 5)),
                  pl.BlockSpec((ts, hw), lambda b, t: (row(b, t), 6)),
                  pl.BlockSpec((ts, AUX_W), lambda b, t: (row(b, t), 0)),
                  pl.BlockSpec((CONV_W, 3 * hw), lambda b, t: (0, 0)),
                  pl.BlockSpec((1, AUX_W), lambda b, t: (0, 0)),
                  pl.BlockSpec((1, AUX_W), lambda b, t: (0, 0)),
                  pl.BlockSpec((1, GDN_DV), lambda b, t: (0, 0))],
        out_specs=pl.BlockSpec((ts, GDN_WIDTH), lambda b, t: (row(b, t), 0)),
        out_shape=jax.ShapeDtypeStruct((bsz * seq, GDN_WIDTH), BF16),
        scratch_shapes=[pltpu.VMEM((GDN_HEADS, GDN_DK, GDN_DV), F32),
                        pltpu.VMEM((ts + CONV_PAD, 3 * hw), F32)],
        compiler_params=_cparams("parallel", "arbitrary"),
        name="gdn",
    )(main, main, main, main, aux, conv_w, alog_pad, dtb_pad, gdn_norm_g)


def _pad_aux_row(v, offset):
    return jnp.zeros((1, AUX_W), F32).at[0, offset:offset + v.shape[0]].set(v.astype(F32))


ROUTE_W = 128
R_E1, R_E2, R_RANK1, R_RANK2, R_G1, R_G2 = range(6)
ROUTER_GROUP_COL = N_EXPERTS


def _layer_norm(h, g, b):
    mu = jnp.mean(h, axis=-1, keepdims=True)
    hc = h - mu
    var = jnp.mean(hc * hc, axis=-1, keepdims=True)
    return hc * lax.rsqrt(var + LN_EPS) * g + b


def _pack_bf16_pairs(x):
    w = x.shape[1] // 2
    lo = lax.bitcast_convert_type(x[:, :w].astype(BF16).astype(F32), jnp.uint32)
    hi = lax.bitcast_convert_type(x[:, w:].astype(BF16).astype(F32), jnp.uint32)
    return (lo >> 16) | (hi & jnp.uint32(0xFFFF0000))


def _unpack_bf16_pairs(p):
    lo = lax.bitcast_convert_type(p << 16, F32)
    hi = lax.bitcast_convert_type(p & jnp.uint32(0xFFFF0000), F32)
    return lo, hi


HALF = D_MODEL // 2
TOK_ROWS = HALF // 128


def _store_token_rows(ref, packed):
    m = packed.shape[0]
    for c in range(TOK_ROWS):
        ref[pl.ds(c, m, stride=TOK_ROWS), :] = packed[:, 128 * c:128 * (c + 1)]


def _load_token_rows(ref, m):
    return jnp.concatenate([ref[pl.ds(c, m, stride=TOK_ROWS), :] for c in range(TOK_ROWS)], axis=1)


def _post_mix_kernel(oa_ref, ob_ref, x_ref, wo_ref, g_ref, b_ref, wr_ref, br_ref,
                     x1p_ref, route_ref, cnt_ref, carry_ref, *, tm):
    @pl.when(pl.program_id(0) == 0)
    def _():
        carry_ref[...] = jnp.zeros_like(carry_ref)

    y = jnp.dot(oa_ref[...], wo_ref[0:GLA_WIDTH, :], preferred_element_type=F32)
    y = y + jnp.dot(ob_ref[...], wo_ref[GLA_WIDTH:, :], preferred_element_type=F32)
    x1 = _layer_norm(ALPHA_DN * x_ref[...] + y, g_ref[...], b_ref[...])
    _store_token_rows(x1p_ref, _pack_bf16_pairs(x1))

    xh = x1.astype(BF16)
    xl = (x1 - xh.astype(F32)).astype(BF16)
    hw = jnp.dot(xh, wr_ref[...], preferred_element_type=F32)
    logits = (hw[:, :ROUTE_W] + hw[:, ROUTE_W:]
              + jnp.dot(xl, wr_ref[:, :ROUTE_W], preferred_element_type=F32) + br_ref[...])
    lane = lax.broadcasted_iota(jnp.int32, (tm, ROUTE_W), 1)
    big = jnp.int32(1 << 20)
    neg = jnp.float32(-jnp.inf)

    def first_argmax(vals):
        m = jnp.max(vals, axis=-1, keepdims=True)
        idx = jnp.min(jnp.where(vals == m, lane, big), axis=-1, keepdims=True)
        return m, idx

    is_group = (lane >= ROUTER_GROUP_COL) & (lane < ROUTER_GROUP_COL + N_GROUPS)
    gl = jnp.where(is_group, logits, neg)
    gmax, gidx = first_argmax(gl)
    p_group = 1.0 / jnp.sum(jnp.exp(gl - gmax), axis=-1, keepdims=True)
    g_sel = gidx - ROUTER_GROUP_COL
    el = jnp.where((lane // EXPERTS_PER_GROUP) == g_sel, logits, neg)
    t1, e1 = first_argmax(el)
    t2, e2 = first_argmax(jnp.where(lane == e1, neg, el))
    ex = jnp.exp(t2 - t1)
    w1 = 1.0 / (1.0 + ex)
    gate1 = p_group * w1
    gate2 = p_group * (ex * w1)

    oh1 = lane == e1
    oh2 = lane == e2
    ri = lax.broadcasted_iota(jnp.int32, (tm, tm), 0)
    ci = lax.broadcasted_iota(jnp.int32, (tm, tm), 1)
    lstrict = (ci < ri).astype(BF16)
    oh1f = oh1.astype(F32)
    oh2f = oh2.astype(F32)
    c1 = jnp.dot(lstrict, oh1f.astype(BF16), preferred_element_type=F32)
    c2 = jnp.dot(lstrict, oh2f.astype(BF16), preferred_element_type=F32)
    carry = carry_ref[...]
    tot1 = jnp.sum(oh1f, axis=0, keepdims=True)
    tot2 = jnp.sum(oh2f, axis=0, keepdims=True)
    rank1 = jnp.sum(jnp.where(oh1, c1 + carry, 0.0), axis=-1, keepdims=True)
    rank2 = jnp.sum(jnp.where(oh2, c2 + (carry + tot1), 0.0), axis=-1, keepdims=True)
    new_carry = carry + tot1 + tot2
    carry_ref[...] = new_carry
    cnt_ref[...] = new_carry

    cols = (e1.astype(F32), e2.astype(F32), rank1, rank2, gate1, gate2)
    route = jnp.zeros((tm, ROUTE_W), F32)
    for j, col in enumerate(cols):
        route = jnp.where(lane == j, col, route)
    route_ref[...] = route


def _post_mix(o_a, o_b, x2d, w_out_b, ln_g, ln_b, w_router, b_router, *, tm=512):
    n = x2d.shape[0]
    return pl.pallas_call(
        functools.partial(_post_mix_kernel, tm=tm),
        grid=(n // tm,),
        in_specs=[pl.BlockSpec((tm, GLA_WIDTH), lambda i: (i, 0)),
                  pl.BlockSpec((tm, GDN_WIDTH), lambda i: (i, 0)),
                  pl.BlockSpec((tm, D_MODEL), lambda i: (i, 0)),
                  pl.BlockSpec((GLA_WIDTH + GDN_WIDTH, D_MODEL), lambda i: (0, 0)),
                  pl.BlockSpec((1, D_MODEL), lambda i: (0, 0)),
                  pl.BlockSpec((1, D_MODEL), lambda i: (0, 0)),
                  pl.BlockSpec((D_MODEL, 2 * ROUTE_W), lambda i: (0, 0)),
                  pl.BlockSpec((1, ROUTE_W), lambda i: (0, 0))],
        out_specs=[pl.BlockSpec((tm * TOK_ROWS, 128), lambda i: (i, 0)),
                   pl.BlockSpec((tm, ROUTE_W), lambda i: (i, 0)),
                   pl.BlockSpec((1, ROUTE_W), lambda i: (0, 0))],
        out_shape=[jax.ShapeDtypeStruct((n * TOK_ROWS, 128), jnp.uint32),
                   jax.ShapeDtypeStruct((n, ROUTE_W), F32),
                   jax.ShapeDtypeStruct((1, ROUTE_W), F32)],
        scratch_shapes=[pltpu.VMEM((1, ROUTE_W), F32)],
        compiler_params=_cparams("arbitrary"),
        name="post_mix",
    )(o_a, o_b, x2d, w_out_b, ln_g, ln_b, w_router, b_router)


EXPERT_BLOCK = 256


ROW_UNROLL = 8


def _plan_kernel(route_ref, cnt_ref, dest_ref, *, tm, tb):
    counts = cnt_ref[...].astype(jnp.int32)
    padded = ((counts + (tb - 1)) & jnp.int32(-tb)).astype(F32)
    ri = lax.broadcasted_iota(jnp.int32, (ROUTE_W, ROUTE_W), 0)
    ci = lax.broadcasted_iota(jnp.int32, (ROUTE_W, ROUTE_W), 1)
    pstart = jnp.dot(jnp.broadcast_to(padded, (8, ROUTE_W)), (ri < ci).astype(F32),
                     precision=HIGHEST, preferred_element_type=F32)[0:1, :]
    route = route_ref[...]
    lane = lax.broadcasted_iota(jnp.int32, (tm, ROUTE_W), 1).astype(F32)
    out = jnp.zeros((tm, ROUTE_W), F32)
    for k, (ce, cr) in enumerate(((R_E1, R_RANK1), (R_E2, R_RANK2))):
        off = jnp.sum(jnp.where(lane == route[:, ce:ce + 1], pstart, 0.0), axis=-1, keepdims=True)
        out = jnp.where(lane == float(k), off + route[:, cr:cr + 1], out)
    dest_ref[...] = out.T[0:8, :].astype(jnp.int32) * TOK_ROWS


def _plan(route, counts, *, tb, tm=2048):
    n = route.shape[0]
    tm = min(tm, n)
    return pl.pallas_call(
        functools.partial(_plan_kernel, tm=tm, tb=tb),
        grid=(n // tm,),
        in_specs=[pl.BlockSpec((tm, ROUTE_W), lambda i: (i, 0)),
                  pl.BlockSpec((1, ROUTE_W), lambda i: (0, 0))],
        out_specs=pl.BlockSpec((8, tm), lambda i: (0, i)),
        out_shape=jax.ShapeDtypeStruct((8, n), jnp.int32),
        compiler_params=_cparams("parallel"),
        name="plan",
    )(route, counts)


def _dispatch_kernel(d1_ref, d2_ref, xp_ref, xs_init_ref, xs_ref, sem, *, tm):
    del xs_init_ref
    base = pl.program_id(0) * tm

    def body(j, carry):
        for u in range(ROW_UNROLL):
            r = j * ROW_UNROLL + u
            for k, d_ref in enumerate((d1_ref, d2_ref)):
                dst = pl.multiple_of(d_ref[base + r], TOK_ROWS)
                pltpu.make_async_copy(xp_ref.at[pl.ds(r * TOK_ROWS, TOK_ROWS), :],
                                      xs_ref.at[pl.ds(dst, TOK_ROWS), :], sem).start(priority=k)
        return carry
    lax.fori_loop(0, tm // ROW_UNROLL, body, 0)
    for _ in range(TOP_K):
        pltpu.make_async_copy(xp_ref, xs_ref.at[pl.ds(0, tm * TOK_ROWS), :], sem).wait()


def _dispatch(dest1, dest2, x1p, n_rows, *, tm=512):
    n = x1p.shape[0] // TOK_ROWS
    grid_spec = pltpu.PrefetchScalarGridSpec(
        num_scalar_prefetch=2,
        grid=(n // tm,),
        in_specs=[pl.BlockSpec((tm * TOK_ROWS, 128), lambda i, a, b: (i, 0)),
                  pl.BlockSpec(memory_space=pl.ANY)],
        out_specs=pl.BlockSpec(memory_space=pl.ANY),
        scratch_shapes=[pltpu.SemaphoreType.DMA(())],
    )
    return pl.pallas_call(
        functools.partial(_dispatch_kernel, tm=tm),
        grid_spec=grid_spec,
        out_shape=jax.ShapeDtypeStruct((n_rows * TOK_ROWS, 128), jnp.uint32),
        input_output_aliases={3: 0},
        compiler_params=_cparams("arbitrary"),
        name="dispatch",
    )(dest1, dest2, x1p, jnp.zeros((n_rows * TOK_ROWS, 128), jnp.uint32))


def _expert_kernel(blk_e_ref, first_ref, next_e_ref, nvalid_ref, xs_ref, wg_hbm, wu_hbm, wd_hbm, y_ref,
                   wgs, wus, wds, wgb, wub, wdb, sem, *, tb):
    i = pl.program_id(0)

    def weight_copies(e):
        return [pltpu.make_async_copy(src.at[e], dst, sem.at[j])
                for j, (src, dst) in enumerate(((wg_hbm, wgs), (wu_hbm, wus), (wd_hbm, wds)))]

    @pl.when((i == 0) & (nvalid_ref[0] > 0))
    def _():
        for cp in weight_copies(blk_e_ref[0]):
            cp.start()

    @pl.when(i < nvalid_ref[0])
    def _():
        @pl.when(first_ref[i] == 1)
        def _():
            for cp in weight_copies(blk_e_ref[i]):
                cp.wait()
            wgb[...] = wgs[...].astype(BF16)
            wub[...] = wus[...].astype(BF16)
            wdb[...] = wds[...].astype(BF16)

            @pl.when(next_e_ref[i] >= 0)
            def _():
                for cp in weight_copies(next_e_ref[i]):
                    cp.start()

        lo, hi = _unpack_bf16_pairs(_load_token_rows(xs_ref, tb))
        lo = lo.astype(BF16)
        hi = hi.astype(BF16)
        hg = jnp.dot(lo, wgb[0:HALF, :], preferred_element_type=F32)
        hg = hg + jnp.dot(hi, wgb[HALF:, :], preferred_element_type=F32)
        hu = jnp.dot(lo, wub[0:HALF, :], preferred_element_type=F32)
        hu = hu + jnp.dot(hi, wub[HALF:, :], preferred_element_type=F32)
        hid = (_silu(hg) * hu).astype(BF16)
        _store_token_rows(y_ref, _pack_bf16_pairs(jnp.dot(hid, wdb[...], preferred_element_type=F32)))

    @pl.when(i >= nvalid_ref[0])
    def _():
        y_ref[...] = jnp.zeros_like(y_ref)


def _experts(blk_e, first, next_e, nvalid, xs, w_gate, w_up, w_down, *, tb=EXPERT_BLOCK):
    nb = blk_e.shape[0]
    grid_spec = pltpu.PrefetchScalarGridSpec(
        num_scalar_prefetch=4,
        grid=(nb,),
        in_specs=[pl.BlockSpec((tb * TOK_ROWS, 128),
                               lambda i, be, fr, ne, nv: (jnp.clip(i, 0, jnp.maximum(nv[0] - 1, 0)), 0)),
                  pl.BlockSpec(memory_space=pl.ANY),
                  pl.BlockSpec(memory_space=pl.ANY),
                  pl.BlockSpec(memory_space=pl.ANY)],
        out_specs=pl.BlockSpec((tb * TOK_ROWS, 128), lambda i, be, fr, ne, nv: (i, 0)),
        scratch_shapes=[pltpu.VMEM((D_MODEL, D_EXPERT), F32),
                        pltpu.VMEM((D_MODEL, D_EXPERT), F32),
                        pltpu.VMEM((D_EXPERT, D_MODEL), F32),
                        pltpu.VMEM((D_MODEL, D_EXPERT), BF16),
                        pltpu.VMEM((D_MODEL, D_EXPERT), BF16),
                        pltpu.VMEM((D_EXPERT, D_MODEL), BF16),
                        pltpu.SemaphoreType.DMA((3,))],
    )
    return pl.pallas_call(
        functools.partial(_expert_kernel, tb=tb),
        grid_spec=grid_spec,
        out_shape=jax.ShapeDtypeStruct((nb * tb * TOK_ROWS, 128), jnp.uint32),
        compiler_params=_cparams("arbitrary"),
        name="experts",
    )(blk_e, first, next_e, nvalid, xs, w_gate, w_up, w_down)


def _combine_kernel(d1_ref, d2_ref, y_hbm, x1_ref, route_ref, g_ref, b_ref, o_ref, ybuf, sem, *, tt):
    i = pl.program_id(0)
    nsteps = pl.num_programs(0)
    slot = i % 2

    def issue(tile, slot_):
        def body(j, carry):
            for u in range(ROW_UNROLL):
                r = j * ROW_UNROLL + u
                for k, d_ref in enumerate((d1_ref, d2_ref)):
                    src = pl.multiple_of(d_ref[tile * tt + r], TOK_ROWS)
                    pltpu.make_async_copy(y_hbm.at[pl.ds(src, TOK_ROWS), :],
                                          ybuf.at[slot_, k, pl.ds(r * TOK_ROWS, TOK_ROWS), :],
                                          sem.at[slot_]).start(priority=k)
            return carry
        lax.fori_loop(0, tt // ROW_UNROLL, body, 0)

    @pl.when(i == 0)
    def _():
        issue(0, 0)

    @pl.when(i + 1 < nsteps)
    def _():
        issue(i + 1, 1 - slot)

    for k in range(TOP_K):
        pltpu.make_async_copy(y_hbm.at[pl.ds(0, tt * TOK_ROWS), :], ybuf.at[slot, k], sem.at[slot]).wait()
    route = route_ref[...]
    g1 = route[:, R_G1:R_G1 + 1]
    g2 = route[:, R_G2:R_G2 + 1]
    lo1, hi1 = _unpack_bf16_pairs(_load_token_rows(ybuf.at[slot, 0], tt))
    lo2, hi2 = _unpack_bf16_pairs(_load_token_rows(ybuf.at[slot, 1], tt))
    x_lo, x_hi = _unpack_bf16_pairs(_load_token_rows(x1_ref, tt))
    h_lo = ALPHA_DN * x_lo + (g1 * lo1 + g2 * lo2)
    h_hi = ALPHA_DN * x_hi + (g1 * hi1 + g2 * hi2)
    mu = (jnp.sum(h_lo, axis=-1, keepdims=True) + jnp.sum(h_hi, axis=-1, keepdims=True)) * (1.0 / D_MODEL)
    c_lo = h_lo - mu
    c_hi = h_hi - mu
    var = (jnp.sum(c_lo * c_lo, axis=-1, keepdims=True) + jnp.sum(c_hi * c_hi, axis=-1, keepdims=True)) * (1.0 / D_MODEL)
    inv = lax.rsqrt(var + LN_EPS)
    o_ref[:, 0:HALF] = c_lo * inv * g_ref[:, 0:HALF] + b_ref[:, 0:HALF]
    o_ref[:, HALF:] = c_hi * inv * g_ref[:, HALF:] + b_ref[:, HALF:]


def _combine(dest1, dest2, ybuf, x1p, route, ln_g, ln_b, *, tt=256):
    n = x1p.shape[0] // TOK_ROWS
    grid_spec = pltpu.PrefetchScalarGridSpec(
        num_scalar_prefetch=2,
        grid=(n // tt,),
        in_specs=[pl.BlockSpec(memory_space=pl.ANY),
                  pl.BlockSpec((tt * TOK_ROWS, 128), lambda i, a, b: (i, 0)),
                  pl.BlockSpec((tt, ROUTE_W), lambda i, a, b: (i, 0)),
                  pl.BlockSpec((1, D_MODEL), lambda i, a, b: (0, 0)),
                  pl.BlockSpec((1, D_MODEL), lambda i, a, b: (0, 0))],
        out_specs=pl.BlockSpec((tt, D_MODEL), lambda i, a, b: (i, 0)),
        scratch_shapes=[pltpu.VMEM((2, TOP_K, tt * TOK_ROWS, 128), jnp.uint32),
                        pltpu.SemaphoreType.DMA((2,))],
    )
    return pl.pallas_call(
        functools.partial(_combine_kernel, tt=tt),
        grid_spec=grid_spec,
        out_shape=jax.ShapeDtypeStruct((n, D_MODEL), F32),
        compiler_params=_cparams("arbitrary"),
        name="combine",
    )(dest1, dest2, ybuf, x1p, route, ln_g, ln_b)


def _block_plan(counts_row, n, tb):
    counts = counts_row[0, :N_EXPERTS].astype(jnp.int32)
    pends = jnp.cumsum((counts + tb - 1) // tb * tb)
    nb = (n * TOP_K) // tb + N_EXPERTS
    nvalid = (pends[-1] // tb).astype(jnp.int32)
    blk = jnp.arange(nb, dtype=jnp.int32)
    blk_e = jnp.minimum(jnp.searchsorted(pends, blk * tb, side='right'), N_EXPERTS - 1).astype(jnp.int32)
    blk_e = jnp.where(blk < nvalid, blk_e, blk_e[jnp.maximum(nvalid - 1, 0)])
    first = jnp.concatenate([jnp.ones((1,), jnp.int32), (blk_e[1:] != blk_e[:-1]).astype(jnp.int32)])
    ids = jnp.arange(N_EXPERTS, dtype=jnp.int32)
    later = lax.cummin(jnp.where(counts > 0, ids, N_EXPERTS), axis=0, reverse=True)
    nxt = jnp.concatenate([later[1:], jnp.full((1,), N_EXPERTS, jnp.int32)])
    nxt = jnp.where(nxt < N_EXPERTS, nxt, -1)
    return blk_e, first, nxt[blk_e], nvalid.reshape(1)


def kernel(x, w_in, w_gk_up, b_gk, conv_w, a_log, dt_bias, gla_norm_g, gdn_norm_g, w_out, ln1_g, ln1_b, w_router_group, b_router_group, w_router_expert, b_router_expert, w_gate, w_up, w_down, ln2_g, ln2_b):
    bsz, seq, d = x.shape
    n = bsz * seq
    x2d = x.reshape(n, d)
    w_main, w_aux = _split_w_in(w_in[0])
    main, aux = _in_proj(x2d, w_main, w_aux)
    wgk = jnp.zeros((AUX_W, 256), F32).at[:GLA_GATE_RANK].set(w_gk_up[0])
    o_a = _gla(main, aux, wgk, b_gk[0][None], gla_norm_g[0][None], bsz=bsz, seq=seq)
    o_b = _gdn(main, aux, conv_w[0], _pad_aux_row(a_log[0], AUX_AB), _pad_aux_row(dt_bias[0], AUX_AB),
               gdn_norm_g[0][None], bsz=bsz, seq=seq)
    rpad = jnp.zeros((d, ROUTE_W - N_EXPERTS - N_GROUPS), F32)
    w_router = jnp.concatenate([w_router_expert[0], w_router_group[0], rpad], axis=1)
    w_router_hi = w_router.astype(BF16)
    w_router = jnp.concatenate([w_router_hi, (w_router - w_router_hi.astype(F32)).astype(BF16)], axis=1)
    b_router = jnp.concatenate([b_router_expert[0], b_router_group[0], rpad[0]])[None]
    x1p, route, counts = _post_mix(o_a, o_b, x2d, w_out[0].astype(BF16), ln1_g[0][None], ln1_b[0][None],
                                       w_router, b_router)
    dest = _plan(route, counts, tb=EXPERT_BLOCK)
    blk_e, first, next_e, nvalid = _block_plan(counts, n, EXPERT_BLOCK)
    xs = _dispatch(dest[0], dest[1], x1p, blk_e.shape[0] * EXPERT_BLOCK)
    ybuf = _experts(blk_e, first, next_e, nvalid, xs, w_gate[0], w_up[0], w_down[0])
    out = _combine(dest[0], dest[1], ybuf, x1p, route, ln2_g[0][None], ln2_b[0][None])
    return out.reshape(bsz, seq, d)
```

```python
import functools

import jax
import jax.numpy as jnp
import numpy as np
from jax import lax
from jax.experimental import pallas as pl
from jax.experimental.pallas import tpu as pltpu

F32 = jnp.float32
BF16 = jnp.bfloat16
HIGHEST = lax.Precision.HIGHEST

D_MODEL = 1024
DEPTH = 1
GLA_HEADS = 4
GLA_DK = 64
GLA_DV = 128
GLA_WIDTH = GLA_HEADS * GLA_DV
GLA_GATE_RANK = 16
GLA_GATE_NORM = 16.0
GDN_HEADS = 4
GDN_DK = 128
GDN_DV = 128
GDN_WIDTH = GDN_HEADS * GDN_DV
CONV_W = 4
CHUNK = 64
N_GROUPS = 8
EXPERTS_PER_GROUP = 8
N_EXPERTS = N_GROUPS * EXPERTS_PER_GROUP
TOP_K = 2
D_EXPERT = 512
LN_EPS = 1e-5
RMS_EPS = 1e-6
ALPHA_DN = (2.0 * DEPTH) ** 0.25

MAIN_W = 2 * GLA_HEADS * GLA_DK + 2 * GLA_WIDTH + 4 * GDN_WIDTH
AUX_W = 128
AUX_LRA = 0
AUX_BB = GLA_GATE_RANK
AUX_AB = GLA_GATE_RANK + GDN_HEADS
SUB = 16
VMEM_LIMIT = 56 * 1024 * 1024


def _cparams(*sem):
    return pltpu.CompilerParams(dimension_semantics=sem, vmem_limit_bytes=VMEM_LIMIT)


def _sigmoid(x):
    return 1.0 / (1.0 + jnp.exp(-x))


def _silu(x):
    return x * _sigmoid(x)


def _log_sigmoid(x):
    return jnp.minimum(x, 0.0) - jnp.log(1.0 + jnp.exp(-jnp.abs(x)))


def _softplus(x):
    return jnp.maximum(x, 0.0) + jnp.log(1.0 + jnp.exp(-jnp.abs(x)))


CONV_PAD = 8
GDN_HW = GDN_HEADS * GDN_DK
GDN_QKV_BLOCK0 = (2 * GLA_HEADS * GLA_DK + 2 * GLA_WIDTH) // GDN_HW


def _in_proj_kernel(x_ref, wm_ref, wa_ref, cw_ref, main_ref, aux_ref, stage_ref, *, tm, tiles_per_seq):
    xb = x_ref[...].astype(BF16)

    @pl.when((pl.program_id(0) % tiles_per_seq) == 0)
    def _():
        stage_ref[:, 0:CONV_PAD, :] = jnp.zeros((3 * GDN_HEADS, CONV_PAD, GDN_DK), F32)

    for j in range(MAIN_W // GDN_HW):
        sl = slice(j * GDN_HW, (j + 1) * GDN_HW)
        r = jnp.dot(xb, wm_ref[:, sl], preferred_element_type=F32)
        b = j - GDN_QKV_BLOCK0
        if not 0 <= b < 3:
            main_ref[:, sl] = r.astype(BF16)
            continue
        for h in range(GDN_HEADS):
            st = stage_ref.at[b * GDN_HEADS + h]
            cols = slice(b * GDN_HW + h * GDN_DK, b * GDN_HW + (h + 1) * GDN_DK)
            st[CONV_PAD:, :] = r[:, h * GDN_DK:(h + 1) * GDN_DK]
            acc = jnp.zeros((tm, GDN_DK), F32)
            for i in range(CONV_W):
                lo = CONV_PAD - (CONV_W - 1) + i
                acc = acc + st[lo:lo + tm, :] * cw_ref[i:i + 1, cols]
            st[0:CONV_PAD, :] = st[tm:tm + CONV_PAD, :]
            yh = _silu(acc)
            if b < 2:
                yh = yh * lax.rsqrt(jnp.sum(yh * yh, axis=-1, keepdims=True) + RMS_EPS)
            if b == 0:
                yh = yh * (GDN_DK ** -0.5)
            main_ref[:, j * GDN_HW + h * GDN_DK:j * GDN_HW + (h + 1) * GDN_DK] = yh.astype(BF16)
    aux_ref[...] = jnp.dot(xb, wa_ref[...], preferred_element_type=F32)


def _in_proj(x2d, w_main, w_aux, conv_w, *, seq, tm=512):
    n = x2d.shape[0]
    return pl.pallas_call(
        functools.partial(_in_proj_kernel, tm=tm, tiles_per_seq=seq // tm),
        grid=(n // tm,),
        in_specs=[pl.BlockSpec((tm, D_MODEL), lambda i: (i, 0)),
                  pl.BlockSpec((D_MODEL, MAIN_W), lambda i: (0, 0)),
                  pl.BlockSpec((D_MODEL, AUX_W), lambda i: (0, 0)),
                  pl.BlockSpec((CONV_W, 3 * GDN_HW), lambda i: (0, 0))],
        out_specs=[pl.BlockSpec((tm, MAIN_W), lambda i: (i, 0)),
                   pl.BlockSpec((tm, AUX_W), lambda i: (i, 0))],
        out_shape=[jax.ShapeDtypeStruct((n, MAIN_W), BF16),
                   jax.ShapeDtypeStruct((n, AUX_W), F32)],
        scratch_shapes=[pltpu.VMEM((3 * GDN_HEADS, tm + CONV_PAD, GDN_DK), F32)],
        compiler_params=_cparams("arbitrary"),
        name="in_proj",
    )(x2d, w_main, w_aux, conv_w)


def _split_w_in(w_in):
    sizes = (GLA_HEADS * GLA_DK, GLA_HEADS * GLA_DK, GLA_WIDTH, GLA_WIDTH, GLA_GATE_RANK,
             GDN_HEADS * GDN_DK, GDN_HEADS * GDN_DK, GDN_WIDTH, GDN_WIDTH, GDN_HEADS, GDN_HEADS)
    offs = np.cumsum((0,) + sizes)
    seg = [w_in[:, offs[i]:offs[i + 1]] for i in range(len(sizes))]
    qa, ka, va, ra, lra, qb, kb, vb, zb, bb, ab = seg
    w_main = jnp.concatenate([qa, ka, va, ra, qb, kb, vb, zb], axis=1).astype(BF16)
    pad = jnp.zeros((w_in.shape[0], AUX_W - GLA_GATE_RANK - 2 * GDN_HEADS), w_in.dtype)
    w_aux = jnp.concatenate([lra, bb, ab, pad], axis=1).astype(BF16)
    return w_main, w_aux


_NT = (((1,), (1,)), ((), ()))
_TN = (((0,), (0,)), ((), ()))


def _gla_kernel(q_ref, k_ref, v_ref, r_ref, aux_ref, wgk_ref, bgk_ref, g_ref, o_ref, st_ref, *, ts):
    c = CHUNK
    nsub = c // SUB

    @pl.when(pl.program_id(1) == 0)
    def _():
        st_ref[...] = jnp.zeros_like(st_ref)

    pre = jnp.dot(aux_ref[...], wgk_ref[...], precision=HIGHEST, preferred_element_type=F32) + bgk_ref[...]
    gk_all = _log_sigmoid(pre) * (1.0 / GLA_GATE_NORM)

    ri = lax.broadcasted_iota(jnp.int32, (c, c), 0)
    ci = lax.broadcasted_iota(jnp.int32, (c, c), 1)
    tri = (ci <= ri).astype(F32)
    row128 = lax.broadcasted_iota(jnp.int32, (c, 128), 0)
    lane128 = lax.broadcasted_iota(jnp.int32, (1, 128), 1)
    lane_masks = [(lane128 // GLA_DK) == hh for hh in range(2)]
    sub_valid = [row128 < SUB * (i + 1) for i in range(nsub)]
    ar = lax.broadcasted_iota(jnp.int32, (c, nsub * c), 0)
    ac = lax.broadcasted_iota(jnp.int32, (c, nsub * c), 1)
    amask = ((ac // c) == (ar // SUB)) & ((ac % c) <= ar)
    scale = GLA_DK ** -0.5
    gnorm = g_ref[...]

    nch = ts // c
    rows = [slice(ch * c, (ch + 1) * c) for ch in range(nch)]
    b_all = [jnp.dot(tri, gk_all[r], precision=HIGHEST, preferred_element_type=F32) for r in rows]

    cp = [(ch, p) for ch in range(nch) for p in range(2)]
    qt, qd, kd, kstack, dlast = {}, {}, {}, {}, {}
    for ch, p in cp:
        lanes = slice(128 * p, 128 * (p + 1))
        bp = b_all[ch][:, lanes]
        qp = q_ref[rows[ch], lanes].astype(F32) * scale
        kp = k_ref[rows[ch], lanes].astype(F32)
        c_row = jnp.concatenate(
            [jnp.broadcast_to(bp[SUB * i:SUB * i + 1, :], (SUB, 128)) for i in range(nsub)], axis=0)
        qt[ch, p] = (qp * jnp.exp(bp - c_row)).astype(BF16)
        qd[ch, p] = qp * jnp.exp(bp)
        b_last = bp[c - 1:c, :]
        dlast[ch, p] = jnp.exp(b_last)
        kd[ch, p] = kp * jnp.exp(b_last - bp)
        kts = []
        for i in range(nsub):
            e = jnp.exp(jnp.where(sub_valid[i], bp[SUB * i:SUB * i + 1, :] - bp, 0.0))
            kts.append(jnp.where(sub_valid[i], kp * e, 0.0))
        kstack[ch, p] = jnp.concatenate(kts, axis=0)

    cph = [(ch, p, hh) for ch, p in cp for hh in range(2)]
    vh = {(ch, p, hh): v_ref[rows[ch], 128 * (2 * p + hh):128 * (2 * p + hh + 1)] for ch, p, hh in cph}
    r_mat = {(ch, p, hh): lax.dot_general(qt[ch, p], jnp.where(lane_masks[hh], kstack[ch, p], 0.0).astype(BF16),
                                          _NT, preferred_element_type=F32) for ch, p, hh in cph}
    o_intra = {k: jnp.dot(jnp.where(amask, r_mat[k], 0.0).astype(BF16), jnp.concatenate([vh[k]] * nsub, axis=0),
                          preferred_element_type=F32) for k in cph}
    upd = {(ch, p, hh): lax.dot_general(vh[ch, p, hh], jnp.where(lane_masks[hh], kd[ch, p], 0.0).astype(BF16), _TN,
                                        preferred_element_type=F32) for ch, p, hh in cph}

    st = [st_ref[p] for p in range(2)]
    for ch in range(nch):
        st_b = [s.astype(BF16) for s in st]
        o_inter = {(p, hh): lax.dot_general(jnp.where(lane_masks[hh], qd[ch, p], 0.0).astype(BF16), st_b[p], _NT,
                                            preferred_element_type=F32) for p in range(2) for hh in range(2)}
        st = [st[p] * dlast[ch, p] + upd[ch, p, 0] + upd[ch, p, 1] for p in range(2)]
        for p in range(2):
            for hh in range(2):
                h = 2 * p + hh
                o = o_intra[ch, p, hh] + o_inter[p, hh]
                o = o * lax.rsqrt(jnp.mean(o * o, axis=-1, keepdims=True) + RMS_EPS) * gnorm
                gate = _silu(r_ref[rows[ch], 128 * h:128 * (h + 1)].astype(F32))
                o_ref[rows[ch], 128 * h:128 * (h + 1)] = (o * gate).astype(BF16)
    for p in range(2):
        st_ref[p] = st[p]


def _gla(main, aux, wgk_pad, b_gk, gla_norm_g, *, bsz, seq, ts=256):
    nt = seq // ts
    row = lambda b, t: b * nt + t
    return pl.pallas_call(
        functools.partial(_gla_kernel, ts=ts),
        grid=(bsz, nt),
        in_specs=[pl.BlockSpec((ts, 256), lambda b, t: (row(b, t), 0)),
                  pl.BlockSpec((ts, 256), lambda b, t: (row(b, t), 1)),
                  pl.BlockSpec((ts, 512), lambda b, t: (row(b, t), 1)),
                  pl.BlockSpec((ts, 512), lambda b, t: (row(b, t), 2)),
                  pl.BlockSpec((ts, AUX_W), lambda b, t: (row(b, t), 0)),
                  pl.BlockSpec((AUX_W, 256), lambda b, t: (0, 0)),
                  pl.BlockSpec((1, 256), lambda b, t: (0, 0)),
                  pl.BlockSpec((1, 128), lambda b, t: (0, 0))],
        out_specs=pl.BlockSpec((ts, GLA_WIDTH), lambda b, t: (row(b, t), 0)),
        out_shape=jax.ShapeDtypeStruct((bsz * seq, GLA_WIDTH), BF16),
        scratch_shapes=[pltpu.VMEM((2, 128, 128), F32)],
        compiler_params=_cparams("parallel", "arbitrary"),
        name="gla",
    )(main, main, main, main, aux, wgk_pad, b_gk, gla_norm_g)


def _gdn_kernel(q_ref, k_ref, v_ref, z_ref, aux_ref, alog_ref, dtb_ref, g_ref, o_ref, st_ref, *, ts):
    c = CHUNK
    nch = ts // c

    @pl.when(pl.program_id(1) == 0)
    def _():
        st_ref[...] = jnp.zeros_like(st_ref)

    aux = aux_ref[...]
    beta_full = _sigmoid(aux)
    g_full = -jnp.exp(alog_ref[...]) * _softplus(aux + dtb_ref[...])
    ri = lax.broadcasted_iota(jnp.int32, (ts, ts), 0)
    ci = lax.broadcasted_iota(jnp.int32, (ts, ts), 1)
    cum = (((ri // c) == (ci // c)) & (ci <= ri)).astype(F32)
    gam_full = jnp.dot(cum, g_full, precision=HIGHEST, preferred_element_type=F32)
    gam_t = gam_full.T
    gnorm = g_ref[...]

    row = lax.broadcasted_iota(jnp.int32, (c, 2 * c), 0)
    lane = lax.broadcasted_iota(jnp.int32, (c, 2 * c), 1)
    left = lane < c
    col = lane % c
    causal = col <= row
    strict = col < row
    same32 = (row // 32) == (col // 32)
    same16 = (row // 16) == (col // 16)
    lvl0 = same16 & strict
    lvl1 = same32 & jnp.logical_not(same16) & strict
    lvl2 = jnp.logical_not(same32) & strict

    def pack_rows(x2):
        return jnp.where(left, x2[0:c], x2[c:2 * c])

    def pack_cols(x1):
        return jnp.where(left, x1[0:c], x1[c:2 * c])

    def halves(x):
        return jnp.where(left, x, 0.0).astype(BF16), jnp.where(left, 0.0, x).astype(BF16)

    def mmp(x, y):
        return jnp.dot(x.astype(BF16), jnp.concatenate(halves(y), axis=0), preferred_element_type=F32)

    heads = range(GDN_HEADS)
    npair = nch // 2
    items = [(h, pr) for h in heads for pr in range(npair)]
    prow = [slice(2 * c * pr, 2 * c * (pr + 1)) for pr in range(npair)]

    qn, kn, beta, gam, gam_r, kb, rhs, rhs_b, qd, kd = ([None] * GDN_HEADS for _ in range(10))
    for h in heads:
        hl = slice(128 * h, 128 * (h + 1))
        kb[h] = k_ref[:, hl]
        qn[h] = q_ref[:, hl].astype(F32)
        kn[h] = kb[h].astype(F32)
        vh = v_ref[:, hl].astype(F32)
        beta[h] = beta_full[:, AUX_BB + h:AUX_BB + h + 1]
        gam[h] = gam_full[:, AUX_AB + h:AUX_AB + h + 1]
        gam_r[h] = gam_t[AUX_AB + h:AUX_AB + h + 1, :]
        egam = jnp.exp(gam[h])
        rhs[h] = jnp.concatenate([beta[h] * vh, beta[h] * egam * kn[h]], axis=1)
        rhs_b[h] = rhs[h].astype(BF16)
        qd[h] = (qn[h] * egam).astype(BF16)
        gl_rows = jnp.concatenate(
            [jnp.broadcast_to(gam[h][c * (j + 1) - 1:c * (j + 1), :], (c, 1)) for j in range(nch)], axis=0)
        kd[h] = (kn[h] * jnp.exp(gl_rows - gam[h])).astype(BF16)

    kk = [pack_rows(lax.dot_general(kb[h][prow[pr]], kb[h][prow[pr]], _NT, preferred_element_type=F32))
          for h, pr in items]
    qk = [pack_rows(lax.dot_general(qn[h][prow[pr]].astype(BF16), kb[h][prow[pr]], _NT,
                                    preferred_element_type=F32)) for h, pr in items]
    dec = [jnp.exp(jnp.where(causal, pack_cols(gam[h][prow[pr]]) - gam_r[h][:, prow[pr]], 0.0)) for h, pr in items]
    a = [pack_cols(beta[h][prow[pr]]) * kk[i] * dec[i] for i, (h, pr) in enumerate(items)]
    attn = [halves(jnp.where(causal, qk[i] * dec[i], 0.0)) for i in range(len(items))]

    x1 = [jnp.where(lvl0, -ai, 0.0) for ai in a]
    x2 = [mmp(x, x) for x in x1]
    y = [mmp(x, xx) for x, xx in zip(x1, x2)]
    p = [x + xx + yy for x, xx, yy in zip(x1, x2, y)]
    xk = x2
    for _ in range(2):
        xk = [mmp(x, x) for x in xk]
        y = [mmp(pp, x) for pp, x in zip(p, xk)]
        p = [pp + x + yy for pp, x, yy in zip(p, xk, y)]
    for lvl in (lvl1, lvl2):
        low = [jnp.where(lvl, ai, 0.0) for ai in a]
        y = [mmp(pp, lo) for pp, lo in zip(p, low)]
        zz = [lo + yy for lo, yy in zip(low, y)]
        y = [mmp(z, pp) for z, pp in zip(zz, p)]
        p = [pp - z - yy for pp, z, yy in zip(p, zz, y)]

    us = [[None] * nch for _ in heads]
    ws = [[None] * nch for _ in heads]
    for i, (h, pr) in enumerate(items):
        for half, p_half in enumerate(halves(p[i])):
            j = 2 * pr + half
            uw = rhs[h][c * j:c * (j + 1)] + jnp.dot(p_half, rhs_b[h][prow[pr]], preferred_element_type=F32)
            us[h][j] = uw[:, :GDN_DV]
            ws[h][j] = uw[:, GDN_DV:].astype(BF16)

    s = [st_ref[h] for h in heads]
    vprev = [None] * GDN_HEADS
    for j in range(nch):
        rows = slice(c * j, c * (j + 1))
        sb = [s[h].astype(BF16) for h in heads]
        vn16 = [(us[h][j] - jnp.dot(ws[h][j], sb[h], preferred_element_type=F32)).astype(BF16) for h in heads]
        o_inter = [jnp.dot(qd[h][rows], sb[h], preferred_element_type=F32) for h in heads]
        for h in heads:
            vpair = jnp.concatenate([vn16[h], jnp.zeros_like(vn16[h])] if j % 2 == 0 else [vprev[h], vn16[h]], axis=0)
            o = o_inter[h] + jnp.dot(attn[h * npair + j // 2][j % 2], vpair, preferred_element_type=F32)
            s[h] = jnp.exp(gam[h][c * (j + 1) - 1:c * (j + 1), :]) * s[h] + lax.dot_general(
                kd[h][rows], vn16[h], _TN, preferred_element_type=F32)
            o = o * lax.rsqrt(jnp.mean(o * o, axis=-1, keepdims=True) + RMS_EPS) * gnorm
            gate = _silu(z_ref[rows, 128 * h:128 * (h + 1)].astype(F32))
            o_ref[rows, 128 * h:128 * (h + 1)] = (o * gate).astype(BF16)
        vprev = vn16
    for h in heads:
        st_ref[h] = s[h]


def _gdn(main, aux, alog_pad, dtb_pad, gdn_norm_g, *, bsz, seq, ts=256):
    nt = seq // ts
    row = lambda b, t: b * nt + t
    hw = GDN_HEADS * GDN_DK
    return pl.pallas_call(
        functools.partial(_gdn_kernel, ts=ts),
        grid=(bsz, nt),
        in_specs=[pl.BlockSpec((ts, hw), lambda b, t: (row(b, t), 3)),
                  pl.BlockSpec((ts, hw), lambda b, t: (row(b, t), 4)),
                  pl.BlockSpec((ts, hw), lambda b, t: (row(b, t), 5)),
                  pl.BlockSpec((ts, hw), lambda b, t: (row(b, t), 6)),
                  pl.BlockSpec((ts, AUX_W), lambda b, t: (row(b, t), 0)),
                  pl.BlockSpec((1, AUX_W), lambda b, t: (0, 0)),
                  pl.BlockSpec((1, AUX_W), lambda b, t: (0, 0)),
                  pl.BlockSpec((1, GDN_DV), lambda b, t: (0, 0))],
        out_specs=pl.BlockSpec((ts, GDN_WIDTH), lambda b, t: (row(b, t), 0)),
        out_shape=jax.ShapeDtypeStruct((bsz * seq, GDN_WIDTH), BF16),
        scratch_shapes=[pltpu.VMEM((GDN_HEADS, GDN_DK, GDN_DV), F32)],
        compiler_params=_cparams("parallel", "arbitrary"),
        name="gdn",
    )(main, main, main, main, aux, alog_pad, dtb_pad, gdn_norm_g)


def _pad_aux_row(v, offset):
    return jnp.zeros((1, AUX_W), F32).at[0, offset:offset + v.shape[0]].set(v.astype(F32))


ROUTE_W = 128
R_E1, R_E2, R_RANK1, R_RANK2, R_G1, R_G2 = range(6)
ROUTER_GROUP_COL = N_EXPERTS


def _layer_norm(h, g, b):
    mu = jnp.mean(h, axis=-1, keepdims=True)
    hc = h - mu
    var = jnp.mean(hc * hc, axis=-1, keepdims=True)
    return hc * lax.rsqrt(var + LN_EPS) * g + b


def _pack_bf16_pairs(x):
    w = x.shape[1] // 2
    lo = lax.bitcast_convert_type(x[:, :w].astype(BF16).astype(F32), jnp.uint32)
    hi = lax.bitcast_convert_type(x[:, w:].astype(BF16).astype(F32), jnp.uint32)
    return (lo >> 16) | (hi & jnp.uint32(0xFFFF0000))


def _unpack_bf16_pairs(p):
    lo = lax.bitcast_convert_type(p << 16, F32)
    hi = lax.bitcast_convert_type(p & jnp.uint32(0xFFFF0000), F32)
    return lo, hi


HALF = D_MODEL // 2
TOK_ROWS = HALF // 128


def _store_token_rows(ref, packed):
    m = packed.shape[0]
    for c in range(TOK_ROWS):
        ref[pl.ds(c, m, stride=TOK_ROWS), :] = packed[:, 128 * c:128 * (c + 1)]


def _load_token_rows(ref, m):
    return jnp.concatenate([ref[pl.ds(c, m, stride=TOK_ROWS), :] for c in range(TOK_ROWS)], axis=1)


def _post_mix_kernel(oa_ref, ob_ref, x_ref, wo_ref, g_ref, b_ref, wr_ref, br_ref,
                     x1p_ref, route_ref, cnt_ref, carry_ref, *, tm):
    @pl.when(pl.program_id(0) == 0)
    def _():
        carry_ref[...] = jnp.zeros_like(carry_ref)

    y = jnp.dot(oa_ref[...], wo_ref[0:GLA_WIDTH, :], preferred_element_type=F32)
    y = y + jnp.dot(ob_ref[...], wo_ref[GLA_WIDTH:, :], preferred_element_type=F32)
    x1 = _layer_norm(ALPHA_DN * x_ref[...] + y, g_ref[...], b_ref[...])
    _store_token_rows(x1p_ref, _pack_bf16_pairs(x1))

    xh = x1.astype(BF16)
    xl = (x1 - xh.astype(F32)).astype(BF16)
    hw = jnp.dot(xh, wr_ref[...], preferred_element_type=F32)
    logits = (hw[:, :ROUTE_W] + hw[:, ROUTE_W:]
              + jnp.dot(xl, wr_ref[:, :ROUTE_W], preferred_element_type=F32) + br_ref[...])
    lane = lax.broadcasted_iota(jnp.int32, (tm, ROUTE_W), 1)
    big = jnp.int32(1 << 20)
    neg = jnp.float32(-jnp.inf)

    def first_argmax(vals):
        m = jnp.max(vals, axis=-1, keepdims=True)
        idx = jnp.min(jnp.where(vals == m, lane, big), axis=-1, keepdims=True)
        return m, idx

    is_group = (lane >= ROUTER_GROUP_COL) & (lane < ROUTER_GROUP_COL + N_GROUPS)
    gl = jnp.where(is_group, logits, neg)
    gmax, gidx = first_argmax(gl)
    p_group = 1.0 / jnp.sum(jnp.exp(gl - gmax), axis=-1, keepdims=True)
    g_sel = gidx - ROUTER_GROUP_COL
    el = jnp.where((lane // EXPERTS_PER_GROUP) == g_sel, logits, neg)
    t1, e1 = first_argmax(el)
    t2, e2 = first_argmax(jnp.where(lane == e1, neg, el))
    ex = jnp.exp(t2 - t1)
    w1 = 1.0 / (1.0 + ex)
    gate1 = p_group * w1
    gate2 = p_group * (ex * w1)

    oh1 = lane == e1
    oh2 = lane == e2
    ri = lax.broadcasted_iota(jnp.int32, (tm, tm), 0)
    ci = lax.broadcasted_iota(jnp.int32, (tm, tm), 1)
    lstrict = (ci < ri).astype(BF16)
    oh1f = oh1.astype(F32)
    oh2f = oh2.astype(F32)
    c1 = jnp.dot(lstrict, oh1f.astype(BF16), preferred_element_type=F32)
    c2 = jnp.dot(lstrict, oh2f.astype(BF16), preferred_element_type=F32)
    carry = carry_ref[...]
    tot1 = jnp.sum(oh1f, axis=0, keepdims=True)
    tot2 = jnp.sum(oh2f, axis=0, keepdims=True)
    rank1 = jnp.sum(jnp.where(oh1, c1 + carry, 0.0), axis=-1, keepdims=True)
    rank2 = jnp.sum(jnp.where(oh2, c2 + (carry + tot1), 0.0), axis=-1, keepdims=True)
    new_carry = carry + tot1 + tot2
    carry_ref[...] = new_carry
    cnt_ref[...] = new_carry

    cols = (e1.astype(F32), e2.astype(F32), rank1, rank2, gate1, gate2)
    route = jnp.zeros((tm, ROUTE_W), F32)
    for j, col in enumerate(cols):
        route = jnp.where(lane == j, col, route)
    route_ref[...] = route


def _post_mix(o_a, o_b, x2d, w_out_b, ln_g, ln_b, w_router, b_router, *, tm=512):
    n = x2d.shape[0]
    return pl.pallas_call(
        functools.partial(_post_mix_kernel, tm=tm),
        grid=(n // tm,),
        in_specs=[pl.BlockSpec((tm, GLA_WIDTH), lambda i: (i, 0)),
                  pl.BlockSpec((tm, GDN_WIDTH), lambda i: (i, 0)),
                  pl.BlockSpec((tm, D_MODEL), lambda i: (i, 0)),
                  pl.BlockSpec((GLA_WIDTH + GDN_WIDTH, D_MODEL), lambda i: (0, 0)),
                  pl.BlockSpec((1, D_MODEL), lambda i: (0, 0)),
                  pl.BlockSpec((1, D_MODEL), lambda i: (0, 0)),
                  pl.BlockSpec((D_MODEL, 2 * ROUTE_W), lambda i: (0, 0)),
                  pl.BlockSpec((1, ROUTE_W), lambda i: (0, 0))],
        out_specs=[pl.BlockSpec((tm * TOK_ROWS, 128), lambda i: (i, 0)),
                   pl.BlockSpec((tm, ROUTE_W), lambda i: (i, 0)),
                   pl.BlockSpec((1, ROUTE_W), lambda i: (0, 0))],
        out_shape=[jax.ShapeDtypeStruct((n * TOK_ROWS, 128), jnp.uint32),
                   jax.ShapeDtypeStruct((n, ROUTE_W), F32),
                   jax.ShapeDtypeStruct((1, ROUTE_W), F32)],
        scratch_shapes=[pltpu.VMEM((1, ROUTE_W), F32)],
        compiler_params=_cparams("arbitrary"),
        name="post_mix",
    )(o_a, o_b, x2d, w_out_b, ln_g, ln_b, w_router, b_router)


EXPERT_BLOCK = 256


ROW_UNROLL = 8


def _plan_kernel(route_ref, cnt_ref, dest_ref, *, tm, tb):
    counts = cnt_ref[...].astype(jnp.int32)
    padded = ((counts + (tb - 1)) & jnp.int32(-tb)).astype(F32)
    ri = lax.broadcasted_iota(jnp.int32, (ROUTE_W, ROUTE_W), 0)
    ci = lax.broadcasted_iota(jnp.int32, (ROUTE_W, ROUTE_W), 1)
    pstart = jnp.dot(jnp.broadcast_to(padded, (8, ROUTE_W)), (ri < ci).astype(F32),
                     precision=HIGHEST, preferred_element_type=F32)[0:1, :]
    route = route_ref[...]
    lane = lax.broadcasted_iota(jnp.int32, (tm, ROUTE_W), 1).astype(F32)
    out = jnp.zeros((tm, ROUTE_W), F32)
    for k, (ce, cr) in enumerate(((R_E1, R_RANK1), (R_E2, R_RANK2))):
        off = jnp.sum(jnp.where(lane == route[:, ce:ce + 1], pstart, 0.0), axis=-1, keepdims=True)
        out = jnp.where(lane == float(k), off + route[:, cr:cr + 1], out)
    dest_ref[...] = out.T[0:8, :].astype(jnp.int32) * TOK_ROWS


def _plan(route, counts, *, tb, tm=2048):
    n = route.shape[0]
    tm = min(tm, n)
    return pl.pallas_call(
        functools.partial(_plan_kernel, tm=tm, tb=tb),
        grid=(n // tm,),
        in_specs=[pl.BlockSpec((tm, ROUTE_W), lambda i: (i, 0)),
                  pl.BlockSpec((1, ROUTE_W), lambda i: (0, 0))],
        out_specs=pl.BlockSpec((8, tm), lambda i: (0, i)),
        out_shape=jax.ShapeDtypeStruct((8, n), jnp.int32),
        compiler_params=_cparams("parallel"),
        name="plan",
    )(route, counts)


def _dispatch_kernel(d1_ref, d2_ref, xp_ref, xs_init_ref, xs_ref, sem, *, tm):
    del xs_init_ref
    base = pl.program_id(0) * tm

    def body(j, carry):
        for u in range(ROW_UNROLL):
            r = j * ROW_UNROLL + u
            for k, d_ref in enumerate((d1_ref, d2_ref)):
                dst = pl.multiple_of(d_ref[base + r], TOK_ROWS)
                pltpu.make_async_copy(xp_ref.at[pl.ds(r * TOK_ROWS, TOK_ROWS), :],
                                      xs_ref.at[pl.ds(dst, TOK_ROWS), :], sem).start(priority=k)
        return carry
    lax.fori_loop(0, tm // ROW_UNROLL, body, 0)
    for _ in range(TOP_K):
        pltpu.make_async_copy(xp_ref, xs_ref.at[pl.ds(0, tm * TOK_ROWS), :], sem).wait()


def _dispatch(dest1, dest2, x1p, n_rows, *, tm=512):
    n = x1p.shape[0] // TOK_ROWS
    grid_spec = pltpu.PrefetchScalarGridSpec(
        num_scalar_prefetch=2,
        grid=(n // tm,),
        in_specs=[pl.BlockSpec((tm * TOK_ROWS, 128), lambda i, a, b: (i, 0)),
                  pl.BlockSpec(memory_space=pl.ANY)],
        out_specs=pl.BlockSpec(memory_space=pl.ANY),
        scratch_shapes=[pltpu.SemaphoreType.DMA(())],
    )
    return pl.pallas_call(
        functools.partial(_dispatch_kernel, tm=tm),
        grid_spec=grid_spec,
        out_shape=jax.ShapeDtypeStruct((n_rows * TOK_ROWS, 128), jnp.uint32),
        input_output_aliases={3: 0},
        compiler_params=_cparams("arbitrary"),
        name="dispatch",
    )(dest1, dest2, x1p, jnp.zeros((n_rows * TOK_ROWS, 128), jnp.uint32))


def _expert_kernel(blk_e_ref, first_ref, next_e_ref, nvalid_ref, xs_ref, wg_hbm, wu_hbm, wd_hbm, y_ref,
                   wgs, wus, wds, wgb, wub, wdb, sem, *, tb):
    i = pl.program_id(0)

    def weight_copies(e):
        return [pltpu.make_async_copy(src.at[e], dst, sem.at[j])
                for j, (src, dst) in enumerate(((wg_hbm, wgs), (wu_hbm, wus), (wd_hbm, wds)))]

    @pl.when((i == 0) & (nvalid_ref[0] > 0))
    def _():
        for cp in weight_copies(blk_e_ref[0]):
            cp.start()

    @pl.when(i < nvalid_ref[0])
    def _():
        @pl.when(first_ref[i] == 1)
        def _():
            for cp in weight_copies(blk_e_ref[i]):
                cp.wait()
            wgb[...] = wgs[...].astype(BF16)
            wub[...] = wus[...].astype(BF16)
            wdb[...] = wds[...].astype(BF16)

            @pl.when(next_e_ref[i] >= 0)
            def _():
                for cp in weight_copies(next_e_ref[i]):
                    cp.start()

        lo, hi = _unpack_bf16_pairs(_load_token_rows(xs_ref, tb))
        lo = lo.astype(BF16)
        hi = hi.astype(BF16)
        hg = jnp.dot(lo, wgb[0:HALF, :], preferred_element_type=F32)
        hg = hg + jnp.dot(hi, wgb[HALF:, :], preferred_element_type=F32)
        hu = jnp.dot(lo, wub[0:HALF, :], preferred_element_type=F32)
        hu = hu + jnp.dot(hi, wub[HALF:, :], preferred_element_type=F32)
        hid = (_silu(hg) * hu).astype(BF16)
        _store_token_rows(y_ref, _pack_bf16_pairs(jnp.dot(hid, wdb[...], preferred_element_type=F32)))

    @pl.when(i >= nvalid_ref[0])
    def _():
        y_ref[...] = jnp.zeros_like(y_ref)


def _experts(blk_e, first, next_e, nvalid, xs, w_gate, w_up, w_down, *, tb=EXPERT_BLOCK):
    nb = blk_e.shape[0]
    grid_spec = pltpu.PrefetchScalarGridSpec(
        num_scalar_prefetch=4,
        grid=(nb,),
        in_specs=[pl.BlockSpec((tb * TOK_ROWS, 128),
                               lambda i, be, fr, ne, nv: (jnp.clip(i, 0, jnp.maximum(nv[0] - 1, 0)), 0)),
                  pl.BlockSpec(memory_space=pl.ANY),
                  pl.BlockSpec(memory_space=pl.ANY),
                  pl.BlockSpec(memory_space=pl.ANY)],
        out_specs=pl.BlockSpec((tb * TOK_ROWS, 128), lambda i, be, fr, ne, nv: (i, 0)),
        scratch_shapes=[pltpu.VMEM((D_MODEL, D_EXPERT), F32),
                        pltpu.VMEM((D_MODEL, D_EXPERT), F32),
                        pltpu.VMEM((D_EXPERT, D_MODEL), F32),
                        pltpu.VMEM((D_MODEL, D_EXPERT), BF16),
                        pltpu.VMEM((D_MODEL, D_EXPERT), BF16),
                        pltpu.VMEM((D_EXPERT, D_MODEL), BF16),
                        pltpu.SemaphoreType.DMA((3,))],
    )
    return pl.pallas_call(
        functools.partial(_expert_kernel, tb=tb),
        grid_spec=grid_spec,
        out_shape=jax.ShapeDtypeStruct((nb * tb * TOK_ROWS, 128), jnp.uint32),
        compiler_params=_cparams("arbitrary"),
        name="experts",
    )(blk_e, first, next_e, nvalid, xs, w_gate, w_up, w_down)


def _combine_kernel(d1_ref, d2_ref, y_hbm, x1_ref, route_ref, g_ref, b_ref, o_ref, ybuf, sem, *, tt):
    i = pl.program_id(0)
    nsteps = pl.num_programs(0)
    slot = i % 2

    def issue(tile, slot_):
        def body(j, carry):
            for u in range(ROW_UNROLL):
                r = j * ROW_UNROLL + u
                for k, d_ref in enumerate((d1_ref, d2_ref)):
                    src = pl.multiple_of(d_ref[tile * tt + r], TOK_ROWS)
                    pltpu.make_async_copy(y_hbm.at[pl.ds(src, TOK_ROWS), :],
                                          ybuf.at[slot_, k, pl.ds(r * TOK_ROWS, TOK_ROWS), :],
                                          sem.at[slot_]).start(priority=k)
            return carry
        lax.fori_loop(0, tt // ROW_UNROLL, body, 0)

    @pl.when(i == 0)
    def _():
        issue(0, 0)

    @pl.when(i + 1 < nsteps)
    def _():
        issue(i + 1, 1 - slot)

    for k in range(TOP_K):
        pltpu.make_async_copy(y_hbm.at[pl.ds(0, tt * TOK_ROWS), :], ybuf.at[slot, k], sem.at[slot]).wait()
    route = route_ref[...]
    g1 = route[:, R_G1:R_G1 + 1]
    g2 = route[:, R_G2:R_G2 + 1]
    lo1, hi1 = _unpack_bf16_pairs(_load_token_rows(ybuf.at[slot, 0], tt))
    lo2, hi2 = _unpack_bf16_pairs(_load_token_rows(ybuf.at[slot, 1], tt))
    x_lo, x_hi = _unpack_bf16_pairs(_load_token_rows(x1_ref, tt))
    h_lo = ALPHA_DN * x_lo + (g1 * lo1 + g2 * lo2)
    h_hi = ALPHA_DN * x_hi + (g1 * hi1 + g2 * hi2)
    mu = (jnp.sum(h_lo, axis=-1, keepdims=True) + jnp.sum(h_hi, axis=-1, keepdims=True)) * (1.0 / D_MODEL)
    c_lo = h_lo - mu
    c_hi = h_hi - mu
    var = (jnp.sum(c_lo * c_lo, axis=-1, keepdims=True) + jnp.sum(c_hi * c_hi, axis=-1, keepdims=True)) * (1.0 / D_MODEL)
    inv = lax.rsqrt(var + LN_EPS)
    o_ref[:, 0:HALF] = c_lo * inv * g_ref[:, 0:HALF] + b_ref[:, 0:HALF]
    o_ref[:, HALF:] = c_hi * inv * g_ref[:, HALF:] + b_ref[:, HALF:]


def _combine(dest1, dest2, ybuf, x1p, route, ln_g, ln_b, *, tt=256):
    n = x1p.shape[0] // TOK_ROWS
    grid_spec = pltpu.PrefetchScalarGridSpec(
        num_scalar_prefetch=2,
        grid=(n // tt,),
        in_specs=[pl.BlockSpec(memory_space=pl.ANY),
                  pl.BlockSpec((tt * TOK_ROWS, 128), lambda i, a, b: (i, 0)),
                  pl.BlockSpec((tt, ROUTE_W), lambda i, a, b: (i, 0)),
                  pl.BlockSpec((1, D_MODEL), lambda i, a, b: (0, 0)),
                  pl.BlockSpec((1, D_MODEL), lambda i, a, b: (0, 0))],
        out_specs=pl.BlockSpec((tt, D_MODEL), lambda i, a, b: (i, 0)),
        scratch_shapes=[pltpu.VMEM((2, TOP_K, tt * TOK_ROWS, 128), jnp.uint32),
                        pltpu.SemaphoreType.DMA((2,))],
    )
    return pl.pallas_call(
        functools.partial(_combine_kernel, tt=tt),
        grid_spec=grid_spec,
        out_shape=jax.ShapeDtypeStruct((n, D_MODEL), F32),
        compiler_params=_cparams("arbitrary"),
        name="combine",
    )(dest1, dest2, ybuf, x1p, route, ln_g, ln_b)


def _block_plan(counts_row, n, tb):
    counts = counts_row[0, :N_EXPERTS].astype(jnp.int32)
    pends = jnp.cumsum((counts + tb - 1) // tb * tb)
    nb = (n * TOP_K) // tb + N_EXPERTS
    nvalid = (pends[-1] // tb).astype(jnp.int32)
    blk = jnp.arange(nb, dtype=jnp.int32)
    blk_e = jnp.sum((pends[None, :] <= (blk * tb)[:, None]).astype(jnp.int32), axis=1)
    blk_e = jnp.minimum(blk_e, N_EXPERTS - 1)
    blk_e = jnp.where(blk < nvalid, blk_e, blk_e[jnp.maximum(nvalid - 1, 0)])
    first = jnp.concatenate([jnp.ones((1,), jnp.int32), (blk_e[1:] != blk_e[:-1]).astype(jnp.int32)])
    ids = jnp.arange(N_EXPERTS, dtype=jnp.int32)
    later = lax.cummin(jnp.where(counts > 0, ids, N_EXPERTS), axis=0, reverse=True)
    nxt = jnp.concatenate([later[1:], jnp.full((1,), N_EXPERTS, jnp.int32)])
    nxt = jnp.where(nxt < N_EXPERTS, nxt, -1)
    return blk_e, first, nxt[blk_e], nvalid.reshape(1)


def kernel(x, w_in, w_gk_up, b_gk, conv_w, a_log, dt_bias, gla_norm_g, gdn_norm_g, w_out, ln1_g, ln1_b, w_router_group, b_router_group, w_router_expert, b_router_expert, w_gate, w_up, w_down, ln2_g, ln2_b):
    bsz, seq, d = x.shape
    n = bsz * seq
    x2d = x.reshape(n, d)
    w_main, w_aux = _split_w_in(w_in[0])
    main, aux = _in_proj(x2d, w_main, w_aux, conv_w[0].astype(F32), seq=seq)
    wgk = jnp.zeros((AUX_W, 256), F32).at[:GLA_GATE_RANK].set(w_gk_up[0])
    o_a = _gla(main, aux, wgk, b_gk[0][None], gla_norm_g[0][None], bsz=bsz, seq=seq)
    o_b = _gdn(main, aux, _pad_aux_row(a_log[0], AUX_AB), _pad_aux_row(dt_bias[0], AUX_AB),
               gdn_norm_g[0][None], bsz=bsz, seq=seq)
    rpad = jnp.zeros((d, ROUTE_W - N_EXPERTS - N_GROUPS), F32)
    w_router = jnp.concatenate([w_router_expert[0], w_router_group[0], rpad], axis=1)
    w_router_hi = w_router.astype(BF16)
    w_router = jnp.concatenate([w_router_hi, (w_router - w_router_hi.astype(F32)).astype(BF16)], axis=1)
    b_router = jnp.concatenate([b_router_expert[0], b_router_group[0], rpad[0]])[None]
    x1p, route, counts = _post_mix(o_a, o_b, x2d, w_out[0].astype(BF16), ln1_g[0][None], ln1_b[0][None],
                                       w_router, b_router)
    dest = _plan(route, counts, tb=EXPERT_BLOCK)
    blk_e, first, next_e, nvalid = _block_plan(counts, n, EXPERT_BLOCK)
    xs = _dispatch(dest[0], dest[1], x1p, blk_e.shape[0] * EXPERT_BLOCK)
    ybuf = _experts(blk_e, first, next_e, nvalid, xs, w_gate[0], w_up[0], w_down[0])
    out = _combine(dest[0], dest[1], ybuf, x1p, route, ln2_g[0][None], ln2_b[0][None])
    return out.reshape(bsz, seq, d)
```

```python
import functools

import jax
import jax.numpy as jnp
import numpy as np
from jax import lax
from jax.experimental import pallas as pl
from jax.experimental.pallas import tpu as pltpu

F32 = jnp.float32
BF16 = jnp.bfloat16
HIGHEST = lax.Precision.HIGHEST

D_MODEL = 1024
DEPTH = 1
GLA_HEADS = 4
GLA_DK = 64
GLA_DV = 128
GLA_WIDTH = GLA_HEADS * GLA_DV
GLA_GATE_RANK = 16
GLA_GATE_NORM = 16.0
GDN_HEADS = 4
GDN_DK = 128
GDN_DV = 128
GDN_WIDTH = GDN_HEADS * GDN_DV
CONV_W = 4
CHUNK = 64
N_GROUPS = 8
EXPERTS_PER_GROUP = 8
N_EXPERTS = N_GROUPS * EXPERTS_PER_GROUP
TOP_K = 2
D_EXPERT = 512
LN_EPS = 1e-5
RMS_EPS = 1e-6
ALPHA_DN = (2.0 * DEPTH) ** 0.25

MAIN_W = 2 * GLA_HEADS * GLA_DK + 2 * GLA_WIDTH + 4 * GDN_WIDTH
AUX_W = 128
AUX_LRA = 0
AUX_BB = GLA_GATE_RANK
AUX_AB = GLA_GATE_RANK + GDN_HEADS
SUB = 16
VMEM_LIMIT = 56 * 1024 * 1024


def _cparams(*sem):
    return pltpu.CompilerParams(dimension_semantics=sem, vmem_limit_bytes=VMEM_LIMIT)


def _sigmoid(x):
    return 1.0 / (1.0 + jnp.exp(-x))


def _silu(x):
    return x * _sigmoid(x)


def _log_sigmoid(x):
    return jnp.minimum(x, 0.0) - jnp.log(1.0 + jnp.exp(-jnp.abs(x)))


def _softplus(x):
    return jnp.maximum(x, 0.0) + jnp.log(1.0 + jnp.exp(-jnp.abs(x)))


CONV_PAD = 8
GDN_HW = GDN_HEADS * GDN_DK
GDN_QKV_BLOCK0 = (2 * GLA_HEADS * GLA_DK + 2 * GLA_WIDTH) // GDN_HW


def _in_proj_kernel(x_ref, wm_ref, wa_ref, cw_ref, main_ref, aux_ref, stage_ref, *, tm, tiles_per_seq):
    xb = x_ref[...].astype(BF16)

    @pl.when((pl.program_id(0) % tiles_per_seq) == 0)
    def _():
        stage_ref[:, 0:CONV_PAD, :] = jnp.zeros((3 * GDN_HEADS, CONV_PAD, GDN_DK), F32)

    for j in range(MAIN_W // GDN_HW):
        sl = slice(j * GDN_HW, (j + 1) * GDN_HW)
        r = jnp.dot(xb, wm_ref[:, sl], preferred_element_type=F32)
        b = j - GDN_QKV_BLOCK0
        if not 0 <= b < 3:
            main_ref[:, sl] = r.astype(BF16)
            continue
        for h in range(GDN_HEADS):
            st = stage_ref.at[b * GDN_HEADS + h]
            cols = slice(b * GDN_HW + h * GDN_DK, b * GDN_HW + (h + 1) * GDN_DK)
            st[CONV_PAD:, :] = r[:, h * GDN_DK:(h + 1) * GDN_DK]
            acc = jnp.zeros((tm, GDN_DK), F32)
            for i in range(CONV_W):
                lo = CONV_PAD - (CONV_W - 1) + i
                acc = acc + st[lo:lo + tm, :] * cw_ref[i:i + 1, cols]
            st[0:CONV_PAD, :] = st[tm:tm + CONV_PAD, :]
            yh = _silu(acc)
            if b < 2:
                yh = yh * lax.rsqrt(jnp.sum(yh * yh, axis=-1, keepdims=True) + RMS_EPS)
            if b == 0:
                yh = yh * (GDN_DK ** -0.5)
            main_ref[:, j * GDN_HW + h * GDN_DK:j * GDN_HW + (h + 1) * GDN_DK] = yh.astype(BF16)
    aux_ref[...] = jnp.dot(xb, wa_ref[...], preferred_element_type=F32)


def _in_proj(x2d, w_main, w_aux, conv_w, *, seq, tm=512):
    n = x2d.shape[0]
    return pl.pallas_call(
        functools.partial(_in_proj_kernel, tm=tm, tiles_per_seq=seq // tm),
        grid=(n // tm,),
        in_specs=[pl.BlockSpec((tm, D_MODEL), lambda i: (i, 0)),
                  pl.BlockSpec((D_MODEL, MAIN_W), lambda i: (0, 0)),
                  pl.BlockSpec((D_MODEL, AUX_W), lambda i: (0, 0)),
                  pl.BlockSpec((CONV_W, 3 * GDN_HW), lambda i: (0, 0))],
        out_specs=[pl.BlockSpec((tm, MAIN_W), lambda i: (i, 0)),
                   pl.BlockSpec((tm, AUX_W), lambda i: (i, 0))],
        out_shape=[jax.ShapeDtypeStruct((n, MAIN_W), BF16),
                   jax.ShapeDtypeStruct((n, AUX_W), F32)],
        scratch_shapes=[pltpu.VMEM((3 * GDN_HEADS, tm + CONV_PAD, GDN_DK), F32)],
        compiler_params=_cparams("arbitrary"),
        name="in_proj",
    )(x2d, w_main, w_aux, conv_w)


def _split_w_in(w_in):
    sizes = (GLA_HEADS * GLA_DK, GLA_HEADS * GLA_DK, GLA_WIDTH, GLA_WIDTH, GLA_GATE_RANK,
             GDN_HEADS * GDN_DK, GDN_HEADS * GDN_DK, GDN_WIDTH, GDN_WIDTH, GDN_HEADS, GDN_HEADS)
    offs = np.cumsum((0,) + sizes)
    seg = [w_in[:, offs[i]:offs[i + 1]] for i in range(len(sizes))]
    qa, ka, va, ra, lra, qb, kb, vb, zb, bb, ab = seg
    w_main = jnp.concatenate([qa, ka, va, ra, qb, kb, vb, zb], axis=1).astype(BF16)
    pad = jnp.zeros((w_in.shape[0], AUX_W - GLA_GATE_RANK - 2 * GDN_HEADS), w_in.dtype)
    w_aux = jnp.concatenate([lra, bb, ab, pad], axis=1).astype(BF16)
    return w_main, w_aux


_NT = (((1,), (1,)), ((), ()))
_TN = (((0,), (0,)), ((), ()))


def _gla_kernel(q_ref, k_ref, v_ref, r_ref, aux_ref, wgk_ref, bgk_ref, g_ref, o_ref, st_ref, *, ts):
    c = CHUNK
    nsub = c // SUB

    @pl.when(pl.program_id(1) == 0)
    def _():
        st_ref[...] = jnp.zeros_like(st_ref)

    pre = jnp.dot(aux_ref[...], wgk_ref[...], precision=HIGHEST, preferred_element_type=F32) + bgk_ref[...]
    gk_all = _log_sigmoid(pre) * (1.0 / GLA_GATE_NORM)

    ri = lax.broadcasted_iota(jnp.int32, (c, c), 0)
    ci = lax.broadcasted_iota(jnp.int32, (c, c), 1)
    tri = (ci <= ri).astype(F32)
    row128 = lax.broadcasted_iota(jnp.int32, (c, 128), 0)
    lane128 = lax.broadcasted_iota(jnp.int32, (1, 128), 1)
    lane_masks = [(lane128 // GLA_DK) == hh for hh in range(2)]
    sub_valid = [row128 < SUB * (i + 1) for i in range(nsub)]
    ar = lax.broadcasted_iota(jnp.int32, (c, nsub * c), 0)
    ac = lax.broadcasted_iota(jnp.int32, (c, nsub * c), 1)
    amask = ((ac // c) == (ar // SUB)) & ((ac % c) <= ar)
    scale = GLA_DK ** -0.5
    gnorm = g_ref[...]

    nch = ts // c
    rows = [slice(ch * c, (ch + 1) * c) for ch in range(nch)]
    b_all = [jnp.dot(tri, gk_all[r], precision=HIGHEST, preferred_element_type=F32) for r in rows]

    cp = [(ch, p) for ch in range(nch) for p in range(2)]
    qt, qd, kd, kstack, dlast = {}, {}, {}, {}, {}
    for ch, p in cp:
        lanes = slice(128 * p, 128 * (p + 1))
        bp = b_all[ch][:, lanes]
        qp = q_ref[rows[ch], lanes].astype(F32) * scale
        kp = k_ref[rows[ch], lanes].astype(F32)
        c_row = jnp.concatenate(
            [jnp.broadcast_to(bp[SUB * i:SUB * i + 1, :], (SUB, 128)) for i in range(nsub)], axis=0)
        qt[ch, p] = (qp * jnp.exp(bp - c_row)).astype(BF16)
        qd[ch, p] = qp * jnp.exp(bp)
        b_last = bp[c - 1:c, :]
        dlast[ch, p] = jnp.exp(b_last)
        kd[ch, p] = kp * jnp.exp(b_last - bp)
        kts = []
        for i in range(nsub):
            e = jnp.exp(jnp.where(sub_valid[i], bp[SUB * i:SUB * i + 1, :] - bp, 0.0))
            kts.append(jnp.where(sub_valid[i], kp * e, 0.0))
        kstack[ch, p] = jnp.concatenate(kts, axis=0)

    cph = [(ch, p, hh) for ch, p in cp for hh in range(2)]
    vh = {(ch, p, hh): v_ref[rows[ch], 128 * (2 * p + hh):128 * (2 * p + hh + 1)] for ch, p, hh in cph}
    r_mat = {(ch, p, hh): lax.dot_general(qt[ch, p], jnp.where(lane_masks[hh], kstack[ch, p], 0.0).astype(BF16),
                                          _NT, preferred_element_type=F32) for ch, p, hh in cph}
    o_intra = {k: jnp.dot(jnp.where(amask, r_mat[k], 0.0).astype(BF16), jnp.concatenate([vh[k]] * nsub, axis=0),
                          preferred_element_type=F32) for k in cph}
    upd = {(ch, p, hh): lax.dot_general(vh[ch, p, hh], jnp.where(lane_masks[hh], kd[ch, p], 0.0).astype(BF16), _TN,
                                        preferred_element_type=F32) for ch, p, hh in cph}

    st = [st_ref[p] for p in range(2)]
    for ch in range(nch):
        st_b = [s.astype(BF16) for s in st]
        o_inter = {(p, hh): lax.dot_general(jnp.where(lane_masks[hh], qd[ch, p], 0.0).astype(BF16), st_b[p], _NT,
                                            preferred_element_type=F32) for p in range(2) for hh in range(2)}
        st = [st[p] * dlast[ch, p] + upd[ch, p, 0] + upd[ch, p, 1] for p in range(2)]
        for p in range(2):
            for hh in range(2):
                h = 2 * p + hh
                o = o_intra[ch, p, hh] + o_inter[p, hh]
                o = o * lax.rsqrt(jnp.mean(o * o, axis=-1, keepdims=True) + RMS_EPS) * gnorm
                gate = _silu(r_ref[rows[ch], 128 * h:128 * (h + 1)].astype(F32))
                o_ref[rows[ch], 128 * h:128 * (h + 1)] = (o * gate).astype(BF16)
    for p in range(2):
        st_ref[p] = st[p]


def _gla(main, aux, wgk_pad, b_gk, gla_norm_g, *, bsz, seq, ts=256):
    nt = seq // ts
    row = lambda b, t: b * nt + t
    return pl.pallas_call(
        functools.partial(_gla_kernel, ts=ts),
        grid=(bsz, nt),
        in_specs=[pl.BlockSpec((ts, 256), lambda b, t: (row(b, t), 0)),
                  pl.BlockSpec((ts, 256), lambda b, t: (row(b, t), 1)),
                  pl.BlockSpec((ts, 512), lambda b, t: (row(b, t), 1)),
                  pl.BlockSpec((ts, 512), lambda b, t: (row(b, t), 2)),
                  pl.BlockSpec((ts, AUX_W), lambda b, t: (row(b, t), 0)),
                  pl.BlockSpec((AUX_W, 256), lambda b, t: (0, 0)),
                  pl.BlockSpec((1, 256), lambda b, t: (0, 0)),
                  pl.BlockSpec((1, 128), lambda b, t: (0, 0))],
        out_specs=pl.BlockSpec((ts, GLA_WIDTH), lambda b, t: (row(b, t), 0)),
        out_shape=jax.ShapeDtypeStruct((bsz * seq, GLA_WIDTH), BF16),
        scratch_shapes=[pltpu.VMEM((2, 128, 128), F32)],
        compiler_params=_cparams("parallel", "arbitrary"),
        name="gla",
    )(main, main, main, main, aux, wgk_pad, b_gk, gla_norm_g)


def _gdn_kernel(q_ref, k_ref, v_ref, z_ref, aux_ref, alog_ref, dtb_ref, g_ref, o_ref, st_ref, *, ts):
    c = CHUNK
    nch = ts // c

    @pl.when(pl.program_id(1) == 0)
    def _():
        st_ref[...] = jnp.zeros_like(st_ref)

    aux = aux_ref[...]
    beta_full = _sigmoid(aux)
    g_full = -jnp.exp(alog_ref[...]) * _softplus(aux + dtb_ref[...])
    ri = lax.broadcasted_iota(jnp.int32, (ts, ts), 0)
    ci = lax.broadcasted_iota(jnp.int32, (ts, ts), 1)
    cum = (((ri // c) == (ci // c)) & (ci <= ri)).astype(F32)
    gam_full = jnp.dot(cum, g_full, precision=HIGHEST, preferred_element_type=F32)
    gam_t = gam_full.T
    gnorm = g_ref[...]

    row = lax.broadcasted_iota(jnp.int32, (c, 2 * c), 0)
    lane = lax.broadcasted_iota(jnp.int32, (c, 2 * c), 1)
    left = lane < c
    col = lane % c
    causal = col <= row
    strict = col < row
    same32 = (row // 32) == (col // 32)
    same16 = (row // 16) == (col // 16)
    lvl0 = same16 & strict
    lvl1 = same32 & jnp.logical_not(same16) & strict
    lvl2 = jnp.logical_not(same32) & strict

    def pack_rows(x2):
        return jnp.where(left, x2[0:c], x2[c:2 * c])

    def pack_cols(x1):
        return jnp.where(left, x1[0:c], x1[c:2 * c])

    def halves(x):
        return jnp.where(left, x, 0.0).astype(BF16), jnp.where(left, 0.0, x).astype(BF16)

    def mmp(x, y):
        return jnp.dot(x.astype(BF16), jnp.concatenate(halves(y), axis=0), preferred_element_type=F32)

    heads = range(GDN_HEADS)
    npair = nch // 2
    items = [(h, pr) for h in heads for pr in range(npair)]
    prow = [slice(2 * c * pr, 2 * c * (pr + 1)) for pr in range(npair)]

    qn, kn, beta, gam, gam_r, kb, rhs, rhs_b, qd, kd = ([None] * GDN_HEADS for _ in range(10))
    for h in heads:
        hl = slice(128 * h, 128 * (h + 1))
        kb[h] = k_ref[:, hl]
        qn[h] = q_ref[:, hl].astype(F32)
        kn[h] = kb[h].astype(F32)
        vh = v_ref[:, hl].astype(F32)
        beta[h] = beta_full[:, AUX_BB + h:AUX_BB + h + 1]
        gam[h] = gam_full[:, AUX_AB + h:AUX_AB + h + 1]
        gam_r[h] = gam_t[AUX_AB + h:AUX_AB + h + 1, :]
        egam = jnp.exp(gam[h])
        rhs[h] = jnp.concatenate([beta[h] * vh, beta[h] * egam * kn[h]], axis=1)
        rhs_b[h] = rhs[h].astype(BF16)
        qd[h] = (qn[h] * egam).astype(BF16)
        gl_rows = jnp.concatenate(
            [jnp.broadcast_to(gam[h][c * (j + 1) - 1:c * (j + 1), :], (c, 1)) for j in range(nch)], axis=0)
        kd[h] = (kn[h] * jnp.exp(gl_rows - gam[h])).astype(BF16)

    kk = [pack_rows(lax.dot_general(kb[h][prow[pr]], kb[h][prow[pr]], _NT, preferred_element_type=F32))
          for h, pr in items]
    qk = [pack_rows(lax.dot_general(qn[h][prow[pr]].astype(BF16), kb[h][prow[pr]], _NT,
                                    preferred_element_type=F32)) for h, pr in items]
    dec = [jnp.exp(jnp.where(causal, pack_cols(gam[h][prow[pr]]) - gam_r[h][:, prow[pr]], 0.0)) for h, pr in items]
    a = [pack_cols(beta[h][prow[pr]]) * kk[i] * dec[i] for i, (h, pr) in enumerate(items)]
    attn = [halves(jnp.where(causal, qk[i] * dec[i], 0.0)) for i in range(len(items))]

    x1 = [jnp.where(lvl0, -ai, 0.0) for ai in a]
    x2 = [mmp(x, x) for x in x1]
    y = [mmp(x, xx) for x, xx in zip(x1, x2)]
    p = [x + xx + yy for x, xx, yy in zip(x1, x2, y)]
    xk = x2
    for _ in range(2):
        xk = [mmp(x, x) for x in xk]
        y = [mmp(pp, x) for pp, x in zip(p, xk)]
        p = [pp + x + yy for pp, x, yy in zip(p, xk, y)]
    for lvl in (lvl1, lvl2):
        low = [jnp.where(lvl, ai, 0.0) for ai in a]
        y = [mmp(pp, lo) for pp, lo in zip(p, low)]
        zz = [lo + yy for lo, yy in zip(low, y)]
        y = [mmp(z, pp) for z, pp in zip(zz, p)]
        p = [pp - z - yy for pp, z, yy in zip(p, zz, y)]

    us = [[None] * nch for _ in heads]
    ws = [[None] * nch for _ in heads]
    for i, (h, pr) in enumerate(items):
        for half, p_half in enumerate(halves(p[i])):
            j = 2 * pr + half
            uw = rhs[h][c * j:c * (j + 1)] + jnp.dot(p_half, rhs_b[h][prow[pr]], preferred_element_type=F32)
            us[h][j] = uw[:, :GDN_DV]
            ws[h][j] = uw[:, GDN_DV:].astype(BF16)

    s = [st_ref[h] for h in heads]
    vprev = [None] * GDN_HEADS
    for j in range(nch):
        rows = slice(c * j, c * (j + 1))
        sb = [s[h].astype(BF16) for h in heads]
        vn16 = [(us[h][j] - jnp.dot(ws[h][j], sb[h], preferred_element_type=F32)).astype(BF16) for h in heads]
        o_inter = [jnp.dot(qd[h][rows], sb[h], preferred_element_type=F32) for h in heads]
        for h in heads:
            vpair = jnp.concatenate([vn16[h], jnp.zeros_like(vn16[h])] if j % 2 == 0 else [vprev[h], vn16[h]], axis=0)
            o = o_inter[h] + jnp.dot(attn[h * npair + j // 2][j % 2], vpair, preferred_element_type=F32)
            s[h] = jnp.exp(gam[h][c * (j + 1) - 1:c * (j + 1), :]) * s[h] + lax.dot_general(
                kd[h][rows], vn16[h], _TN, preferred_element_type=F32)
            o = o * lax.rsqrt(jnp.mean(o * o, axis=-1, keepdims=True) + RMS_EPS) * gnorm
            gate = _silu(z_ref[rows, 128 * h:128 * (h + 1)].astype(F32))
            o_ref[rows, 128 * h:128 * (h + 1)] = (o * gate).astype(BF16)
        vprev = vn16
    for h in heads:
        st_ref[h] = s[h]


def _gdn(main, aux, alog_pad, dtb_pad, gdn_norm_g, *, bsz, seq, ts=256):
    nt = seq // ts
    row = lambda b, t: b * nt + t
    hw = GDN_HEADS * GDN_DK
    return pl.pallas_call(
        functools.partial(_gdn_kernel, ts=ts),
        grid=(bsz, nt),
        in_specs=[pl.BlockSpec((ts, hw), lambda b, t: (row(b, t), 3)),
                  pl.BlockSpec((ts, hw), lambda b, t: (row(b, t), 4)),
                  pl.BlockSpec((ts, hw), lambda b, t: (row(b, t), 5)),
                  pl.BlockSpec((ts, hw), lambda b, t: (row(b, t), 6)),
                  pl.BlockSpec((ts, AUX_W), lambda b, t: (row(b, t), 0)),
                  pl.BlockSpec((1, AUX_W), lambda b, t: (0, 0)),
                  pl.BlockSpec((1, AUX_W), lambda b, t: (0, 0)),
                  pl.BlockSpec((1, GDN_DV), lambda b, t: (0, 0))],
        out_specs=pl.BlockSpec((ts, GDN_WIDTH), lambda b, t: (row(b, t), 0)),
        out_shape=jax.ShapeDtypeStruct((bsz * seq, GDN_WIDTH), BF16),
        scratch_shapes=[pltpu.VMEM((GDN_HEADS, GDN_DK, GDN_DV), F32)],
        compiler_params=_cparams("parallel", "arbitrary"),
        name="gdn",
    )(main, main, main, main, aux, alog_pad, dtb_pad, gdn_norm_g)


def _pad_aux_row(v, offset):
    return jnp.zeros((1, AUX_W), F32).at[0, offset:offset + v.shape[0]].set(v.astype(F32))


ROUTE_W = 128
R_E1, R_E2, R_RANK1, R_RANK2, R_G1, R_G2 = range(6)
ROUTER_GROUP_COL = N_EXPERTS


def _layer_norm(h, g, b):
    mu = jnp.mean(h, axis=-1, keepdims=True)
    hc = h - mu
    var = jnp.mean(hc * hc, axis=-1, keepdims=True)
    return hc * lax.rsqrt(var + LN_EPS) * g + b


def _pack_bf16_pairs(x):
    w = x.shape[1] // 2
    lo = lax.bitcast_convert_type(x[:, :w].astype(BF16).astype(F32), jnp.uint32)
    hi = lax.bitcast_convert_type(x[:, w:].astype(BF16).astype(F32), jnp.uint32)
    return (lo >> 16) | (hi & jnp.uint32(0xFFFF0000))


def _unpack_bf16_pairs(p):
    lo = lax.bitcast_convert_type(p << 16, F32)
    hi = lax.bitcast_convert_type(p & jnp.uint32(0xFFFF0000), F32)
    return lo, hi


HALF = D_MODEL // 2
TOK_ROWS = HALF // 128


def _store_token_rows(ref, packed):
    m = packed.shape[0]
    for c in range(TOK_ROWS):
        ref[pl.ds(c, m, stride=TOK_ROWS), :] = packed[:, 128 * c:128 * (c + 1)]


def _load_token_rows(ref, m):
    return jnp.concatenate([ref[pl.ds(c, m, stride=TOK_ROWS), :] for c in range(TOK_ROWS)], axis=1)


def _post_mix_kernel(oa_ref, ob_ref, x_ref, wo_ref, g_ref, b_ref, wr_ref, br_ref,
                     x1p_ref, route_ref, cnt_ref, carry_ref, *, tm):
    @pl.when(pl.program_id(0) == 0)
    def _():
        carry_ref[...] = jnp.zeros_like(carry_ref)

    y = jnp.dot(oa_ref[...], wo_ref[0:GLA_WIDTH, :], preferred_element_type=F32)
    y = y + jnp.dot(ob_ref[...], wo_ref[GLA_WIDTH:, :], preferred_element_type=F32)
    x1 = _layer_norm(ALPHA_DN * x_ref[...] + y, g_ref[...], b_ref[...])
    _store_token_rows(x1p_ref, _pack_bf16_pairs(x1))

    xh = x1.astype(BF16)
    xl = (x1 - xh.astype(F32)).astype(BF16)
    hw = jnp.dot(xh, wr_ref[...], preferred_element_type=F32)
    logits = (hw[:, :ROUTE_W] + hw[:, ROUTE_W:]
              + jnp.dot(xl, wr_ref[:, :ROUTE_W], preferred_element_type=F32) + br_ref[...])
    lane = lax.broadcasted_iota(jnp.int32, (tm, ROUTE_W), 1)
    big = jnp.int32(1 << 20)
    neg = jnp.float32(-jnp.inf)

    def first_argmax(vals):
        m = jnp.max(vals, axis=-1, keepdims=True)
        idx = jnp.min(jnp.where(vals == m, lane, big), axis=-1, keepdims=True)
        return m, idx

    is_group = (lane >= ROUTER_GROUP_COL) & (lane < ROUTER_GROUP_COL + N_GROUPS)
    gl = jnp.where(is_group, logits, neg)
    gmax, gidx = first_argmax(gl)
    p_group = 1.0 / jnp.sum(jnp.exp(gl - gmax), axis=-1, keepdims=True)
    g_sel = gidx - ROUTER_GROUP_COL
    el = jnp.where((lane // EXPERTS_PER_GROUP) == g_sel, logits, neg)
    t1, e1 = first_argmax(el)
    t2, e2 = first_argmax(jnp.where(lane == e1, neg, el))
    ex = jnp.exp(t2 - t1)
    w1 = 1.0 / (1.0 + ex)
    gate1 = p_group * w1
    gate2 = p_group * (ex * w1)

    oh1 = lane == e1
    oh2 = lane == e2
    ri = lax.broadcasted_iota(jnp.int32, (tm, tm), 0)
    ci = lax.broadcasted_iota(jnp.int32, (tm, tm), 1)
    lstrict = (ci < ri).astype(BF16)
    oh1f = oh1.astype(F32)
    oh2f = oh2.astype(F32)
    c1 = jnp.dot(lstrict, oh1f.astype(BF16), preferred_element_type=F32)
    c2 = jnp.dot(lstrict, oh2f.astype(BF16), preferred_element_type=F32)
    carry = carry_ref[...]
    tot1 = jnp.sum(oh1f, axis=0, keepdims=True)
    tot2 = jnp.sum(oh2f, axis=0, keepdims=True)
    rank1 = jnp.sum(jnp.where(oh1, c1 + carry, 0.0), axis=-1, keepdims=True)
    rank2 = jnp.sum(jnp.where(oh2, c2 + (carry + tot1), 0.0), axis=-1, keepdims=True)
    new_carry = carry + tot1 + tot2
    carry_ref[...] = new_carry
    cnt_ref[...] = new_carry

    cols = (e1.astype(F32), e2.astype(F32), rank1, rank2, gate1, gate2)
    route = jnp.zeros((tm, ROUTE_W), F32)
    for j, col in enumerate(cols):
        route = jnp.where(lane == j, col, route)
    route_ref[...] = route


def _post_mix(o_a, o_b, x2d, w_out_b, ln_g, ln_b, w_router, b_router, *, tm=512):
    n = x2d.shape[0]
    return pl.pallas_call(
        functools.partial(_post_mix_kernel, tm=tm),
        grid=(n // tm,),
        in_specs=[pl.BlockSpec((tm, GLA_WIDTH), lambda i: (i, 0)),
                  pl.BlockSpec((tm, GDN_WIDTH), lambda i: (i, 0)),
                  pl.BlockSpec((tm, D_MODEL), lambda i: (i, 0)),
                  pl.BlockSpec((GLA_WIDTH + GDN_WIDTH, D_MODEL), lambda i: (0, 0)),
                  pl.BlockSpec((1, D_MODEL), lambda i: (0, 0)),
                  pl.BlockSpec((1, D_MODEL), lambda i: (0, 0)),
                  pl.BlockSpec((D_MODEL, 2 * ROUTE_W), lambda i: (0, 0)),
                  pl.BlockSpec((1, ROUTE_W), lambda i: (0, 0))],
        out_specs=[pl.BlockSpec((tm * TOK_ROWS, 128), lambda i: (i, 0)),
                   pl.BlockSpec((tm, ROUTE_W), lambda i: (i, 0)),
                   pl.BlockSpec((1, ROUTE_W), lambda i: (0, 0))],
        out_shape=[jax.ShapeDtypeStruct((n * TOK_ROWS, 128), jnp.uint32),
                   jax.ShapeDtypeStruct((n, ROUTE_W), F32),
                   jax.ShapeDtypeStruct((1, ROUTE_W), F32)],
        scratch_shapes=[pltpu.VMEM((1, ROUTE_W), F32)],
        compiler_params=_cparams("arbitrary"),
        name="post_mix",
    )(o_a, o_b, x2d, w_out_b, ln_g, ln_b, w_router, b_router)


EXPERT_BLOCK = 256


ROW_UNROLL = 8


def _plan_kernel(route_ref, cnt_ref, dest_ref, *, tm, tb):
    counts = cnt_ref[...].astype(jnp.int32)
    padded = ((counts + (tb - 1)) & jnp.int32(-tb)).astype(F32)
    ri = lax.broadcasted_iota(jnp.int32, (ROUTE_W, ROUTE_W), 0)
    ci = lax.broadcasted_iota(jnp.int32, (ROUTE_W, ROUTE_W), 1)
    pstart = jnp.dot(jnp.broadcast_to(padded, (8, ROUTE_W)), (ri < ci).astype(F32),
                     precision=HIGHEST, preferred_element_type=F32)[0:1, :]
    route = route_ref[...]
    lane = lax.broadcasted_iota(jnp.int32, (tm, ROUTE_W), 1).astype(F32)
    out = jnp.zeros((tm, ROUTE_W), F32)
    for k, (ce, cr) in enumerate(((R_E1, R_RANK1), (R_E2, R_RANK2))):
        off = jnp.sum(jnp.where(lane == route[:, ce:ce + 1], pstart, 0.0), axis=-1, keepdims=True)
        out = jnp.where(lane == float(k), off + route[:, cr:cr + 1], out)
    dest_ref[...] = out.T[0:8, :].astype(jnp.int32) * TOK_ROWS


def _plan(route, counts, *, tb, tm=2048):
    n = route.shape[0]
    tm = min(tm, n)
    return pl.pallas_call(
        functools.partial(_plan_kernel, tm=tm, tb=tb),
        grid=(n // tm,),
        in_specs=[pl.BlockSpec((tm, ROUTE_W), lambda i: (i, 0)),
                  pl.BlockSpec((1, ROUTE_W), lambda i: (0, 0))],
        out_specs=pl.BlockSpec((8, tm), lambda i: (0, i)),
        out_shape=jax.ShapeDtypeStruct((8, n), jnp.int32),
        compiler_params=_cparams("parallel"),
        name="plan",
    )(route, counts)


def _dispatch_kernel(d1_ref, d2_ref, xp_ref, xs_init_ref, xs_ref, sem, *, tm):
    del xs_init_ref
    base = pl.program_id(0) * tm

    def body(j, carry):
        for u in range(ROW_UNROLL):
            r = j * ROW_UNROLL + u
            for k, d_ref in enumerate((d1_ref, d2_ref)):
                dst = pl.multiple_of(d_ref[base + r], TOK_ROWS)
                pltpu.make_async_copy(xp_ref.at[pl.ds(r * TOK_ROWS, TOK_ROWS), :],
                                      xs_ref.at[pl.ds(dst, TOK_ROWS), :], sem).start(priority=k)
        return carry
    lax.fori_loop(0, tm // ROW_UNROLL, body, 0)
    for _ in range(TOP_K):
        pltpu.make_async_copy(xp_ref, xs_ref.at[pl.ds(0, tm * TOK_ROWS), :], sem).wait()


def _dispatch(dest1, dest2, x1p, n_rows, *, tm=512):
    n = x1p.shape[0] // TOK_ROWS
    grid_spec = pltpu.PrefetchScalarGridSpec(
        num_scalar_prefetch=2,
        grid=(n // tm,),
        in_specs=[pl.BlockSpec((tm * TOK_ROWS, 128), lambda i, a, b: (i, 0)),
                  pl.BlockSpec(memory_space=pl.ANY)],
        out_specs=pl.BlockSpec(memory_space=pl.ANY),
        scratch_shapes=[pltpu.SemaphoreType.DMA(())],
    )
    return pl.pallas_call(
        functools.partial(_dispatch_kernel, tm=tm),
        grid_spec=grid_spec,
        out_shape=jax.ShapeDtypeStruct((n_rows * TOK_ROWS, 128), jnp.uint32),
        input_output_aliases={3: 0},
        compiler_params=_cparams("arbitrary"),
        name="dispatch",
    )(dest1, dest2, x1p, jnp.zeros((n_rows * TOK_ROWS, 128), jnp.uint32))


def _expert_kernel(start_ref, nblk_ref, nvalid_ref, wg_ref, wu_ref, wd_ref, xs_hbm, y_hbm,
                   wgb, wub, wdb, xbuf, ybuf, xsem, ysem, zsem, *, tb, nb):
    e = pl.program_id(0)
    nvalid = nvalid_ref[0]
    rows = tb * TOK_ROWS

    def x_copy(g, slot):
        return pltpu.make_async_copy(xs_hbm.at[pl.ds(pl.multiple_of(g * rows, rows), rows), :], xbuf.at[slot],
                                     xsem.at[slot])

    def y_copy(g, slot):
        return pltpu.make_async_copy(ybuf.at[slot], y_hbm.at[pl.ds(pl.multiple_of(g * rows, rows), rows), :],
                                     ysem.at[slot])

    @pl.when((e == 0) & (nvalid > 0))
    def _():
        x_copy(0, 0).start()

    @pl.when(nblk_ref[e] > 0)
    def _():
        wgb[...] = wg_ref[...].astype(BF16)
        wub[...] = wu_ref[...].astype(BF16)
        wdb[...] = wd_ref[...].astype(BF16)

    def block(b, carry):
        g = start_ref[e] + b
        slot = g % 2
        x_copy(g, slot).wait()

        @pl.when(g + 1 < nvalid)
        def _():
            x_copy(g + 1, 1 - slot).start()

        lo, hi = _unpack_bf16_pairs(_load_token_rows(xbuf.at[slot], tb))
        lo = lo.astype(BF16)
        hi = hi.astype(BF16)
        hg = jnp.dot(lo, wgb[0:HALF, :], preferred_element_type=F32)
        hg = hg + jnp.dot(hi, wgb[HALF:, :], preferred_element_type=F32)
        hu = jnp.dot(lo, wub[0:HALF, :], preferred_element_type=F32)
        hu = hu + jnp.dot(hi, wub[HALF:, :], preferred_element_type=F32)
        hid = (_silu(hg) * hu).astype(BF16)
        y = _pack_bf16_pairs(jnp.dot(hid, wdb[...], preferred_element_type=F32))

        @pl.when(g >= 2)
        def _():
            y_copy(g - 2, slot).wait()

        _store_token_rows(ybuf.at[slot], y)
        y_copy(g, slot).start()
        return carry

    lax.fori_loop(0, nblk_ref[e], block, 0)

    @pl.when(e == pl.num_programs(0) - 1)
    def _():
        for back in (1, 2):
            @pl.when(nvalid >= back)
            def _():
                y_copy(nvalid - back, (nvalid - back) % 2).wait()

        xbuf[0] = jnp.zeros((rows, 128), jnp.uint32)

        def zero_copy(g):
            return pltpu.make_async_copy(xbuf.at[0], y_hbm.at[pl.ds(pl.multiple_of(g * rows, rows), rows), :], zsem)

        def zstart(g, carry):
            zero_copy(g).start()
            return carry

        def zwait(g, carry):
            zero_copy(g).wait()
            return carry

        lax.fori_loop(nvalid, nb, zstart, 0)
        lax.fori_loop(nvalid, nb, zwait, 0)


def _experts(blk_start, nblk, nvalid, xs, w_gate, w_up, w_down, *, nb, tb=EXPERT_BLOCK):
    w_map = lambda e, st, nk, nv: (e, 0, 0)
    grid_spec = pltpu.PrefetchScalarGridSpec(
        num_scalar_prefetch=3,
        grid=(N_EXPERTS,),
        in_specs=[pl.BlockSpec((None, D_MODEL, D_EXPERT), w_map),
                  pl.BlockSpec((None, D_MODEL, D_EXPERT), w_map),
                  pl.BlockSpec((None, D_EXPERT, D_MODEL), w_map),
                  pl.BlockSpec(memory_space=pl.ANY)],
        out_specs=pl.BlockSpec(memory_space=pl.ANY),
        scratch_shapes=[pltpu.VMEM((D_MODEL, D_EXPERT), BF16),
                        pltpu.VMEM((D_MODEL, D_EXPERT), BF16),
                        pltpu.VMEM((D_EXPERT, D_MODEL), BF16),
                        pltpu.VMEM((2, tb * TOK_ROWS, 128), jnp.uint32),
                        pltpu.VMEM((2, tb * TOK_ROWS, 128), jnp.uint32),
                        pltpu.SemaphoreType.DMA((2,)),
                        pltpu.SemaphoreType.DMA((2,)),
                        pltpu.SemaphoreType.DMA(())],
    )
    return pl.pallas_call(
        functools.partial(_expert_kernel, tb=tb, nb=nb),
        grid_spec=grid_spec,
        out_shape=jax.ShapeDtypeStruct((nb * tb * TOK_ROWS, 128), jnp.uint32),
        compiler_params=_cparams("arbitrary"),
        name="experts",
    )(blk_start, nblk, nvalid, w_gate, w_up, w_down, xs)


def _combine_kernel(d1_ref, d2_ref, y_hbm, x1_ref, route_ref, g_ref, b_ref, o_ref, ybuf, sem, *, tt):
    i = pl.program_id(0)
    nsteps = pl.num_programs(0)
    slot = i % 2

    def issue(tile, slot_):
        def body(j, carry):
            for u in range(ROW_UNROLL):
                r = j * ROW_UNROLL + u
                for k, d_ref in enumerate((d1_ref, d2_ref)):
                    src = pl.multiple_of(d_ref[tile * tt + r], TOK_ROWS)
                    pltpu.make_async_copy(y_hbm.at[pl.ds(src, TOK_ROWS), :],
                                          ybuf.at[slot_, k, pl.ds(r * TOK_ROWS, TOK_ROWS), :],
                                          sem.at[slot_]).start(priority=k)
            return carry
        lax.fori_loop(0, tt // ROW_UNROLL, body, 0)

    @pl.when(i == 0)
    def _():
        issue(0, 0)

    @pl.when(i + 1 < nsteps)
    def _():
        issue(i + 1, 1 - slot)

    for k in range(TOP_K):
        pltpu.make_async_copy(y_hbm.at[pl.ds(0, tt * TOK_ROWS), :], ybuf.at[slot, k], sem.at[slot]).wait()
    route = route_ref[...]
    g1 = route[:, R_G1:R_G1 + 1]
    g2 = route[:, R_G2:R_G2 + 1]
    lo1, hi1 = _unpack_bf16_pairs(_load_token_rows(ybuf.at[slot, 0], tt))
    lo2, hi2 = _unpack_bf16_pairs(_load_token_rows(ybuf.at[slot, 1], tt))
    x_lo, x_hi = _unpack_bf16_pairs(_load_token_rows(x1_ref, tt))
    h_lo = ALPHA_DN * x_lo + (g1 * lo1 + g2 * lo2)
    h_hi = ALPHA_DN * x_hi + (g1 * hi1 + g2 * hi2)
    mu = (jnp.sum(h_lo, axis=-1, keepdims=True) + jnp.sum(h_hi, axis=-1, keepdims=True)) * (1.0 / D_MODEL)
    c_lo = h_lo - mu
    c_hi = h_hi - mu
    var = (jnp.sum(c_lo * c_lo, axis=-1, keepdims=True) + jnp.sum(c_hi * c_hi, axis=-1, keepdims=True)) * (1.0 / D_MODEL)
    inv = lax.rsqrt(var + LN_EPS)
    o_ref[:, 0:HALF] = c_lo * inv * g_ref[:, 0:HALF] + b_ref[:, 0:HALF]
    o_ref[:, HALF:] = c_hi * inv * g_ref[:, HALF:] + b_ref[:, HALF:]


def _combine(dest1, dest2, ybuf, x1p, route, ln_g, ln_b, *, tt=256):
    n = x1p.shape[0] // TOK_ROWS
    grid_spec = pltpu.PrefetchScalarGridSpec(
        num_scalar_prefetch=2,
        grid=(n // tt,),
        in_specs=[pl.BlockSpec(memory_space=pl.ANY),
                  pl.BlockSpec((tt * TOK_ROWS, 128), lambda i, a, b: (i, 0)),
                  pl.BlockSpec((tt, ROUTE_W), lambda i, a, b: (i, 0)),
                  pl.BlockSpec((1, D_MODEL), lambda i, a, b: (0, 0)),
                  pl.BlockSpec((1, D_MODEL), lambda i, a, b: (0, 0))],
        out_specs=pl.BlockSpec((tt, D_MODEL), lambda i, a, b: (i, 0)),
        scratch_shapes=[pltpu.VMEM((2, TOP_K, tt * TOK_ROWS, 128), jnp.uint32),
                        pltpu.SemaphoreType.DMA((2,))],
    )
    return pl.pallas_call(
        functools.partial(_combine_kernel, tt=tt),
        grid_spec=grid_spec,
        out_shape=jax.ShapeDtypeStruct((n, D_MODEL), F32),
        compiler_params=_cparams("arbitrary"),
        name="combine",
    )(dest1, dest2, ybuf, x1p, route, ln_g, ln_b)


def _block_plan(counts_row, n, tb):
    counts = counts_row[0, :N_EXPERTS].astype(jnp.int32)
    nblk = (counts + tb - 1) // tb
    ends = jnp.cumsum(nblk)
    return ends - nblk, nblk, ends[-1:]


def kernel(x, w_in, w_gk_up, b_gk, conv_w, a_log, dt_bias, gla_norm_g, gdn_norm_g, w_out, ln1_g, ln1_b, w_router_group, b_router_group, w_router_expert, b_router_expert, w_gate, w_up, w_down, ln2_g, ln2_b):
    bsz, seq, d = x.shape
    n = bsz * seq
    x2d = x.reshape(n, d)
    w_main, w_aux = _split_w_in(w_in[0])
    main, aux = _in_proj(x2d, w_main, w_aux, conv_w[0].astype(F32), seq=seq)
    wgk = jnp.zeros((AUX_W, 256), F32).at[:GLA_GATE_RANK].set(w_gk_up[0])
    o_a = _gla(main, aux, wgk, b_gk[0][None], gla_norm_g[0][None], bsz=bsz, seq=seq)
    o_b = _gdn(main, aux, _pad_aux_row(a_log[0], AUX_AB), _pad_aux_row(dt_bias[0], AUX_AB),
               gdn_norm_g[0][None], bsz=bsz, seq=seq)
    rpad = jnp.zeros((d, ROUTE_W - N_EXPERTS - N_GROUPS), F32)
    w_router = jnp.concatenate([w_router_expert[0], w_router_group[0], rpad], axis=1)
    w_router_hi = w_router.astype(BF16)
    w_router = jnp.concatenate([w_router_hi, (w_router - w_router_hi.astype(F32)).astype(BF16)], axis=1)
    b_router = jnp.concatenate([b_router_expert[0], b_router_group[0], rpad[0]])[None]
    x1p, route, counts = _post_mix(o_a, o_b, x2d, w_out[0].astype(BF16), ln1_g[0][None], ln1_b[0][None],
                                       w_router, b_router)
    dest = _plan(route, counts, tb=EXPERT_BLOCK)
    blk_start, nblk, nvalid = _block_plan(counts, n, EXPERT_BLOCK)
    nb = (n * TOP_K) // EXPERT_BLOCK + N_EXPERTS
    xs = _dispatch(dest[0], dest[1], x1p, nb * EXPERT_BLOCK)
    ybuf = _experts(blk_start, nblk, nvalid, xs, w_gate[0], w_up[0], w_down[0], nb=nb)
    out = _combine(dest[0], dest[1], ybuf, x1p, route, ln2_g[0][None], ln2_b[0][None])
    return out.reshape(bsz, seq, d)
```

```python
import functools

import jax
import jax.numpy as jnp
import numpy as np
from jax import lax
from jax.experimental import pallas as pl
from jax.experimental.pallas import tpu as pltpu

F32 = jnp.float32
BF16 = jnp.bfloat16
HIGHEST = lax.Precision.HIGHEST

D_MODEL = 1024
DEPTH = 1
GLA_HEADS = 4
GLA_DK = 64
GLA_DV = 128
GLA_WIDTH = GLA_HEADS * GLA_DV
GLA_GATE_RANK = 16
GLA_GATE_NORM = 16.0
GDN_HEADS = 4
GDN_DK = 128
GDN_DV = 128
GDN_WIDTH = GDN_HEADS * GDN_DV
CONV_W = 4
CHUNK = 64
N_GROUPS = 8
EXPERTS_PER_GROUP = 8
N_EXPERTS = N_GROUPS * EXPERTS_PER_GROUP
TOP_K = 2
D_EXPERT = 512
LN_EPS = 1e-5
RMS_EPS = 1e-6
ALPHA_DN = (2.0 * DEPTH) ** 0.25

MAIN_W = 2 * GLA_HEADS * GLA_DK + 2 * GLA_WIDTH + 4 * GDN_WIDTH
AUX_W = 128
AUX_LRA = 0
AUX_BB = GLA_GATE_RANK
AUX_AB = GLA_GATE_RANK + GDN_HEADS
SUB = 16
VMEM_LIMIT = 56 * 1024 * 1024


def _cparams(*sem):
    return pltpu.CompilerParams(dimension_semantics=sem, vmem_limit_bytes=VMEM_LIMIT)


def _sigmoid(x):
    return 1.0 / (1.0 + jnp.exp(-x))


def _silu(x):
    return x * _sigmoid(x)


def _log_sigmoid(x):
    return jnp.minimum(x, 0.0) - jnp.log(1.0 + jnp.exp(-jnp.abs(x)))


def _softplus(x):
    return jnp.maximum(x, 0.0) + jnp.log(1.0 + jnp.exp(-jnp.abs(x)))


CONV_PAD = 8
GDN_HW = GDN_HEADS * GDN_DK
GDN_QKV_BLOCK0 = (2 * GLA_HEADS * GLA_DK + 2 * GLA_WIDTH) // GDN_HW


def _in_proj_kernel(x_ref, wm_ref, wa_ref, cw_ref, main_ref, aux_ref, stage_ref, *, tm, tiles_per_seq):
    xb = x_ref[...].astype(BF16)

    @pl.when((pl.program_id(0) % tiles_per_seq) == 0)
    def _():
        stage_ref[:, 0:CONV_PAD, :] = jnp.zeros((3 * GDN_HEADS, CONV_PAD, GDN_DK), F32)

    for j in range(MAIN_W // GDN_HW):
        sl = slice(j * GDN_HW, (j + 1) * GDN_HW)
        r = jnp.dot(xb, wm_ref[:, sl], preferred_element_type=F32)
        b = j - GDN_QKV_BLOCK0
        if not 0 <= b < 3:
            main_ref[:, sl] = r.astype(BF16)
            continue
        for h in range(GDN_HEADS):
            st = stage_ref.at[b * GDN_HEADS + h]
            cols = slice(b * GDN_HW + h * GDN_DK, b * GDN_HW + (h + 1) * GDN_DK)
            st[CONV_PAD:, :] = r[:, h * GDN_DK:(h + 1) * GDN_DK]
            acc = jnp.zeros((tm, GDN_DK), F32)
            for i in range(CONV_W):
                lo = CONV_PAD - (CONV_W - 1) + i
                acc = acc + st[lo:lo + tm, :] * cw_ref[i:i + 1, cols]
            st[0:CONV_PAD, :] = st[tm:tm + CONV_PAD, :]
            yh = _silu(acc)
            if b < 2:
                yh = yh * lax.rsqrt(jnp.sum(yh * yh, axis=-1, keepdims=True) + RMS_EPS)
            if b == 0:
                yh = yh * (GDN_DK ** -0.5)
            main_ref[:, j * GDN_HW + h * GDN_DK:j * GDN_HW + (h + 1) * GDN_DK] = yh.astype(BF16)
    aux_ref[...] = jnp.dot(xb, wa_ref[...], preferred_element_type=F32)


def _in_proj(x2d, w_main, w_aux, conv_w, *, seq, tm=512):
    n = x2d.shape[0]
    return pl.pallas_call(
        functools.partial(_in_proj_kernel, tm=tm, tiles_per_seq=seq // tm),
        grid=(n // tm,),
        in_specs=[pl.BlockSpec((tm, D_MODEL), lambda i: (i, 0)),
                  pl.BlockSpec((D_MODEL, MAIN_W), lambda i: (0, 0)),
                  pl.BlockSpec((D_MODEL, AUX_W), lambda i: (0, 0)),
                  pl.BlockSpec((CONV_W, 3 * GDN_HW), lambda i: (0, 0))],
        out_specs=[pl.BlockSpec((tm, MAIN_W), lambda i: (i, 0)),
                   pl.BlockSpec((tm, AUX_W), lambda i: (i, 0))],
        out_shape=[jax.ShapeDtypeStruct((n, MAIN_W), BF16),
                   jax.ShapeDtypeStruct((n, AUX_W), F32)],
        scratch_shapes=[pltpu.VMEM((3 * GDN_HEADS, tm + CONV_PAD, GDN_DK), F32)],
        compiler_params=_cparams("arbitrary"),
        name="in_proj",
    )(x2d, w_main, w_aux, conv_w)


def _split_w_in(w_in):
    sizes = (GLA_HEADS * GLA_DK, GLA_HEADS * GLA_DK, GLA_WIDTH, GLA_WIDTH, GLA_GATE_RANK,
             GDN_HEADS * GDN_DK, GDN_HEADS * GDN_DK, GDN_WIDTH, GDN_WIDTH, GDN_HEADS, GDN_HEADS)
    offs = np.cumsum((0,) + sizes)
    seg = [w_in[:, offs[i]:offs[i + 1]] for i in range(len(sizes))]
    qa, ka, va, ra, lra, qb, kb, vb, zb, bb, ab = seg
    w_main = jnp.concatenate([qa, ka, va, ra, qb, kb, vb, zb], axis=1).astype(BF16)
    pad = jnp.zeros((w_in.shape[0], AUX_W - GLA_GATE_RANK - 2 * GDN_HEADS), w_in.dtype)
    w_aux = jnp.concatenate([lra, bb, ab, pad], axis=1).astype(BF16)
    return w_main, w_aux


_NT = (((1,), (1,)), ((), ()))
_TN = (((0,), (0,)), ((), ()))


def _gla_kernel(q_ref, k_ref, v_ref, r_ref, aux_ref, wgk_ref, bgk_ref, g_ref, o_ref, st_ref, *, ts):
    c = CHUNK
    nsub = c // SUB

    @pl.when(pl.program_id(1) == 0)
    def _():
        st_ref[...] = jnp.zeros_like(st_ref)

    pre = jnp.dot(aux_ref[...], wgk_ref[...], precision=HIGHEST, preferred_element_type=F32) + bgk_ref[...]
    gk_all = _log_sigmoid(pre) * (1.0 / GLA_GATE_NORM)

    ri = lax.broadcasted_iota(jnp.int32, (c, c), 0)
    ci = lax.broadcasted_iota(jnp.int32, (c, c), 1)
    tri = (ci <= ri).astype(F32)
    row128 = lax.broadcasted_iota(jnp.int32, (c, 128), 0)
    lane128 = lax.broadcasted_iota(jnp.int32, (1, 128), 1)
    lane_masks = [(lane128 // GLA_DK) == hh for hh in range(2)]
    sub_valid = [row128 < SUB * (i + 1) for i in range(nsub)]
    ar = lax.broadcasted_iota(jnp.int32, (c, nsub * c), 0)
    ac = lax.broadcasted_iota(jnp.int32, (c, nsub * c), 1)
    amask = ((ac // c) == (ar // SUB)) & ((ac % c) <= ar)
    scale = GLA_DK ** -0.5
    gnorm = g_ref[...]

    nch = ts // c
    rows = [slice(ch * c, (ch + 1) * c) for ch in range(nch)]
    b_all = [jnp.dot(tri, gk_all[r], precision=HIGHEST, preferred_element_type=F32) for r in rows]

    cp = [(ch, p) for ch in range(nch) for p in range(2)]
    qt, qd, kd, kstack, dlast = {}, {}, {}, {}, {}
    for ch, p in cp:
        lanes = slice(128 * p, 128 * (p + 1))
        bp = b_all[ch][:, lanes]
        qp = q_ref[rows[ch], lanes].astype(F32) * scale
        kp = k_ref[rows[ch], lanes].astype(F32)
        c_row = jnp.concatenate(
            [jnp.broadcast_to(bp[SUB * i:SUB * i + 1, :], (SUB, 128)) for i in range(nsub)], axis=0)
        qt[ch, p] = (qp * jnp.exp(bp - c_row)).astype(BF16)
        qd[ch, p] = qp * jnp.exp(bp)
        b_last = bp[c - 1:c, :]
        dlast[ch, p] = jnp.exp(b_last)
        kd[ch, p] = kp * jnp.exp(b_last - bp)
        kts = []
        for i in range(nsub):
            e = jnp.exp(jnp.where(sub_valid[i], bp[SUB * i:SUB * i + 1, :] - bp, 0.0))
            kts.append(jnp.where(sub_valid[i], kp * e, 0.0))
        kstack[ch, p] = jnp.concatenate(kts, axis=0)

    cph = [(ch, p, hh) for ch, p in cp for hh in range(2)]
    vh = {(ch, p, hh): v_ref[rows[ch], 128 * (2 * p + hh):128 * (2 * p + hh + 1)] for ch, p, hh in cph}
    r_mat = {(ch, p, hh): lax.dot_general(qt[ch, p], jnp.where(lane_masks[hh], kstack[ch, p], 0.0).astype(BF16),
                                          _NT, preferred_element_type=F32) for ch, p, hh in cph}
    o_intra = {k: jnp.dot(jnp.where(amask, r_mat[k], 0.0).astype(BF16), jnp.concatenate([vh[k]] * nsub, axis=0),
                          preferred_element_type=F32) for k in cph}
    upd = {(ch, p, hh): lax.dot_general(vh[ch, p, hh], jnp.where(lane_masks[hh], kd[ch, p], 0.0).astype(BF16), _TN,
                                        preferred_element_type=F32) for ch, p, hh in cph}

    st = [st_ref[p] for p in range(2)]
    for ch in range(nch):
        st_b = [s.astype(BF16) for s in st]
        o_inter = {(p, hh): lax.dot_general(jnp.where(lane_masks[hh], qd[ch, p], 0.0).astype(BF16), st_b[p], _NT,
                                            preferred_element_type=F32) for p in range(2) for hh in range(2)}
        st = [st[p] * dlast[ch, p] + upd[ch, p, 0] + upd[ch, p, 1] for p in range(2)]
        for p in range(2):
            for hh in range(2):
                h = 2 * p + hh
                o = o_intra[ch, p, hh] + o_inter[p, hh]
                o = o * lax.rsqrt(jnp.mean(o * o, axis=-1, keepdims=True) + RMS_EPS) * gnorm
                gate = _silu(r_ref[rows[ch], 128 * h:128 * (h + 1)].astype(F32))
                o_ref[rows[ch], 128 * h:128 * (h + 1)] = (o * gate).astype(BF16)
    for p in range(2):
        st_ref[p] = st[p]


def _gla(main, aux, wgk_pad, b_gk, gla_norm_g, *, bsz, seq, ts=256):
    nt = seq // ts
    row = lambda b, t: b * nt + t
    return pl.pallas_call(
        functools.partial(_gla_kernel, ts=ts),
        grid=(bsz, nt),
        in_specs=[pl.BlockSpec((ts, 256), lambda b, t: (row(b, t), 0)),
                  pl.BlockSpec((ts, 256), lambda b, t: (row(b, t), 1)),
                  pl.BlockSpec((ts, 512), lambda b, t: (row(b, t), 1)),
                  pl.BlockSpec((ts, 512), lambda b, t: (row(b, t), 2)),
                  pl.BlockSpec((ts, AUX_W), lambda b, t: (row(b, t), 0)),
                  pl.BlockSpec((AUX_W, 256), lambda b, t: (0, 0)),
                  pl.BlockSpec((1, 256), lambda b, t: (0, 0)),
                  pl.BlockSpec((1, 128), lambda b, t: (0, 0))],
        out_specs=pl.BlockSpec((ts, GLA_WIDTH), lambda b, t: (row(b, t), 0)),
        out_shape=jax.ShapeDtypeStruct((bsz * seq, GLA_WIDTH), BF16),
        scratch_shapes=[pltpu.VMEM((2, 128, 128), F32)],
        compiler_params=_cparams("parallel", "arbitrary"),
        name="gla",
    )(main, main, main, main, aux, wgk_pad, b_gk, gla_norm_g)


def _gdn_kernel(q_ref, k_ref, v_ref, z_ref, aux_ref, alog_ref, dtb_ref, g_ref, o_ref, st_ref, *, ts):
    c = CHUNK
    nch = ts // c

    @pl.when(pl.program_id(1) == 0)
    def _():
        st_ref[...] = jnp.zeros_like(st_ref)

    aux = aux_ref[...]
    beta_full = _sigmoid(aux)
    g_full = -jnp.exp(alog_ref[...]) * _softplus(aux + dtb_ref[...])
    ri = lax.broadcasted_iota(jnp.int32, (ts, ts), 0)
    ci = lax.broadcasted_iota(jnp.int32, (ts, ts), 1)
    cum = (((ri // c) == (ci // c)) & (ci <= ri)).astype(F32)
    gam_full = jnp.dot(cum, g_full, precision=HIGHEST, preferred_element_type=F32)
    gam_t = gam_full.T
    gnorm = g_ref[...]

    row = lax.broadcasted_iota(jnp.int32, (c, 2 * c), 0)
    lane = lax.broadcasted_iota(jnp.int32, (c, 2 * c), 1)
    left = lane < c
    col = lane % c
    causal = col <= row
    strict = col < row
    same32 = (row // 32) == (col // 32)
    same16 = (row // 16) == (col // 16)
    lvl0 = same16 & strict
    lvl1 = same32 & jnp.logical_not(same16) & strict
    lvl2 = jnp.logical_not(same32) & strict

    def pack_rows(x2):
        return jnp.where(left, x2[0:c], x2[c:2 * c])

    def pack_cols(x1):
        return jnp.where(left, x1[0:c], x1[c:2 * c])

    def halves(x):
        return jnp.where(left, x, 0.0).astype(BF16), jnp.where(left, 0.0, x).astype(BF16)

    def mmp(x, y):
        return jnp.dot(x.astype(BF16), jnp.concatenate(halves(y), axis=0), preferred_element_type=F32)

    heads = range(GDN_HEADS)
    npair = nch // 2
    items = [(h, pr) for h in heads for pr in range(npair)]
    prow = [slice(2 * c * pr, 2 * c * (pr + 1)) for pr in range(npair)]

    qn, kn, beta, gam, gam_r, kb, rhs, rhs_b, qd, kd = ([None] * GDN_HEADS for _ in range(10))
    for h in heads:
        hl = slice(128 * h, 128 * (h + 1))
        kb[h] = k_ref[:, hl]
        qn[h] = q_ref[:, hl].astype(F32)
        kn[h] = kb[h].astype(F32)
        vh = v_ref[:, hl].astype(F32)
        beta[h] = beta_full[:, AUX_BB + h:AUX_BB + h + 1]
        gam[h] = gam_full[:, AUX_AB + h:AUX_AB + h + 1]
        gam_r[h] = gam_t[AUX_AB + h:AUX_AB + h + 1, :]
        egam = jnp.exp(gam[h])
        rhs[h] = jnp.concatenate([beta[h] * vh, beta[h] * egam * kn[h]], axis=1)
        rhs_b[h] = rhs[h].astype(BF16)
        qd[h] = (qn[h] * egam).astype(BF16)
        gl_rows = jnp.concatenate(
            [jnp.broadcast_to(gam[h][c * (j + 1) - 1:c * (j + 1), :], (c, 1)) for j in range(nch)], axis=0)
        kd[h] = (kn[h] * jnp.exp(gl_rows - gam[h])).astype(BF16)

    kk = [pack_rows(lax.dot_general(kb[h][prow[pr]], kb[h][prow[pr]], _NT, preferred_element_type=F32))
          for h, pr in items]
    qk = [pack_rows(lax.dot_general(qn[h][prow[pr]].astype(BF16), kb[h][prow[pr]], _NT,
                                    preferred_element_type=F32)) for h, pr in items]
    dec = [jnp.exp(jnp.where(causal, pack_cols(gam[h][prow[pr]]) - gam_r[h][:, prow[pr]], 0.0)) for h, pr in items]
    a = [pack_cols(beta[h][prow[pr]]) * kk[i] * dec[i] for i, (h, pr) in enumerate(items)]
    attn = [halves(jnp.where(causal, qk[i] * dec[i], 0.0)) for i in range(len(items))]

    x1 = [jnp.where(lvl0, -ai, 0.0) for ai in a]
    x2 = [mmp(x, x) for x in x1]
    y = [mmp(x, xx) for x, xx in zip(x1, x2)]
    p = [x + xx + yy for x, xx, yy in zip(x1, x2, y)]
    xk = x2
    for _ in range(2):
        xk = [mmp(x, x) for x in xk]
        y = [mmp(pp, x) for pp, x in zip(p, xk)]
        p = [pp + x + yy for pp, x, yy in zip(p, xk, y)]
    for lvl in (lvl1, lvl2):
        low = [jnp.where(lvl, ai, 0.0) for ai in a]
        y = [mmp(pp, lo) for pp, lo in zip(p, low)]
        zz = [lo + yy for lo, yy in zip(low, y)]
        y = [mmp(z, pp) for z, pp in zip(zz, p)]
        p = [pp - z - yy for pp, z, yy in zip(p, zz, y)]

    us = [[None] * nch for _ in heads]
    ws = [[None] * nch for _ in heads]
    for i, (h, pr) in enumerate(items):
        for half, p_half in enumerate(halves(p[i])):
            j = 2 * pr + half
            uw = rhs[h][c * j:c * (j + 1)] + jnp.dot(p_half, rhs_b[h][prow[pr]], preferred_element_type=F32)
            us[h][j] = uw[:, :GDN_DV]
            ws[h][j] = uw[:, GDN_DV:].astype(BF16)

    s = [st_ref[h] for h in heads]
    vprev = [None] * GDN_HEADS
    for j in range(nch):
        rows = slice(c * j, c * (j + 1))
        sb = [s[h].astype(BF16) for h in heads]
        vn16 = [(us[h][j] - jnp.dot(ws[h][j], sb[h], preferred_element_type=F32)).astype(BF16) for h in heads]
        o_inter = [jnp.dot(qd[h][rows], sb[h], preferred_element_type=F32) for h in heads]
        for h in heads:
            vpair = jnp.concatenate([vn16[h], jnp.zeros_like(vn16[h])] if j % 2 == 0 else [vprev[h], vn16[h]], axis=0)
            o = o_inter[h] + jnp.dot(attn[h * npair + j // 2][j % 2], vpair, preferred_element_type=F32)
            s[h] = jnp.exp(gam[h][c * (j + 1) - 1:c * (j + 1), :]) * s[h] + lax.dot_general(
                kd[h][rows], vn16[h], _TN, preferred_element_type=F32)
            o = o * lax.rsqrt(jnp.mean(o * o, axis=-1, keepdims=True) + RMS_EPS) * gnorm
            gate = _silu(z_ref[rows, 128 * h:128 * (h + 1)].astype(F32))
            o_ref[rows, 128 * h:128 * (h + 1)] = (o * gate).astype(BF16)
        vprev = vn16
    for h in heads:
        st_ref[h] = s[h]


def _gdn(main, aux, alog_pad, dtb_pad, gdn_norm_g, *, bsz, seq, ts=256):
    nt = seq // ts
    row = lambda b, t: b * nt + t
    hw = GDN_HEADS * GDN_DK
    return pl.pallas_call(
        functools.partial(_gdn_kernel, ts=ts),
        grid=(bsz, nt),
        in_specs=[pl.BlockSpec((ts, hw), lambda b, t: (row(b, t), 3)),
                  pl.BlockSpec((ts, hw), lambda b, t: (row(b, t), 4)),
                  pl.BlockSpec((ts, hw), lambda b, t: (row(b, t), 5)),
                  pl.BlockSpec((ts, hw), lambda b, t: (row(b, t), 6)),
                  pl.BlockSpec((ts, AUX_W), lambda b, t: (row(b, t), 0)),
                  pl.BlockSpec((1, AUX_W), lambda b, t: (0, 0)),
                  pl.BlockSpec((1, AUX_W), lambda b, t: (0, 0)),
                  pl.BlockSpec((1, GDN_DV), lambda b, t: (0, 0))],
        out_specs=pl.BlockSpec((ts, GDN_WIDTH), lambda b, t: (row(b, t), 0)),
        out_shape=jax.ShapeDtypeStruct((bsz * seq, GDN_WIDTH), BF16),
        scratch_shapes=[pltpu.VMEM((GDN_HEADS, GDN_DK, GDN_DV), F32)],
        compiler_params=_cparams("parallel", "arbitrary"),
        name="gdn",
    )(main, main, main, main, aux, alog_pad, dtb_pad, gdn_norm_g)


def _pad_aux_row(v, offset):
    return jnp.zeros((1, AUX_W), F32).at[0, offset:offset + v.shape[0]].set(v.astype(F32))


ROUTE_W = 128
R_E1, R_E2, R_RANK1, R_RANK2, R_G1, R_G2 = range(6)
ROUTER_GROUP_COL = N_EXPERTS


def _layer_norm(h, g, b):
    mu = jnp.mean(h, axis=-1, keepdims=True)
    hc = h - mu
    var = jnp.mean(hc * hc, axis=-1, keepdims=True)
    return hc * lax.rsqrt(var + LN_EPS) * g + b


def _pack_bf16_pairs(x):
    w = x.shape[1] // 2
    lo = lax.bitcast_convert_type(x[:, :w].astype(BF16).astype(F32), jnp.uint32)
    hi = lax.bitcast_convert_type(x[:, w:].astype(BF16).astype(F32), jnp.uint32)
    return (lo >> 16) | (hi & jnp.uint32(0xFFFF0000))


def _unpack_bf16_pairs(p):
    lo = lax.bitcast_convert_type(p << 16, F32)
    hi = lax.bitcast_convert_type(p & jnp.uint32(0xFFFF0000), F32)
    return lo, hi


HALF = D_MODEL // 2
TOK_ROWS = HALF // 128


def _store_token_rows(ref, packed):
    m = packed.shape[0]
    for c in range(TOK_ROWS):
        ref[pl.ds(c, m, stride=TOK_ROWS), :] = packed[:, 128 * c:128 * (c + 1)]


def _load_token_rows(ref, m):
    return jnp.concatenate([ref[pl.ds(c, m, stride=TOK_ROWS), :] for c in range(TOK_ROWS)], axis=1)


def _post_mix_kernel(oa_ref, ob_ref, x_ref, wo_ref, g_ref, b_ref, wr_ref, br_ref,
                     x1p_ref, route_ref, cnt_ref, carry_ref, *, tm):
    @pl.when(pl.program_id(0) == 0)
    def _():
        carry_ref[...] = jnp.zeros_like(carry_ref)

    y = jnp.dot(oa_ref[...], wo_ref[0:GLA_WIDTH, :], preferred_element_type=F32)
    y = y + jnp.dot(ob_ref[...], wo_ref[GLA_WIDTH:, :], preferred_element_type=F32)
    x1 = _layer_norm(ALPHA_DN * x_ref[...] + y, g_ref[...], b_ref[...])
    _store_token_rows(x1p_ref, _pack_bf16_pairs(x1))

    xh = x1.astype(BF16)
    xl = (x1 - xh.astype(F32)).astype(BF16)
    hw = jnp.dot(xh, wr_ref[...], preferred_element_type=F32)
    logits = (hw[:, :ROUTE_W] + hw[:, ROUTE_W:]
              + jnp.dot(xl, wr_ref[:, :ROUTE_W], preferred_element_type=F32) + br_ref[...])
    lane = lax.broadcasted_iota(jnp.int32, (tm, ROUTE_W), 1)
    big = jnp.int32(1 << 20)
    neg = jnp.float32(-jnp.inf)

    def first_argmax(vals):
        m = jnp.max(vals, axis=-1, keepdims=True)
        idx = jnp.min(jnp.where(vals == m, lane, big), axis=-1, keepdims=True)
        return m, idx

    is_group = (lane >= ROUTER_GROUP_COL) & (lane < ROUTER_GROUP_COL + N_GROUPS)
    gl = jnp.where(is_group, logits, neg)
    gmax, gidx = first_argmax(gl)
    p_group = 1.0 / jnp.sum(jnp.exp(gl - gmax), axis=-1, keepdims=True)
    g_sel = gidx - ROUTER_GROUP_COL
    el = jnp.where((lane // EXPERTS_PER_GROUP) == g_sel, logits, neg)
    t1, e1 = first_argmax(el)
    t2, e2 = first_argmax(jnp.where(lane == e1, neg, el))
    ex = jnp.exp(t2 - t1)
    w1 = 1.0 / (1.0 + ex)
    gate1 = p_group * w1
    gate2 = p_group * (ex * w1)

    oh1 = lane == e1
    oh2 = lane == e2
    ri = lax.broadcasted_iota(jnp.int32, (tm, tm), 0)
    ci = lax.broadcasted_iota(jnp.int32, (tm, tm), 1)
    lstrict = (ci < ri).astype(BF16)
    oh1f = oh1.astype(F32)
    oh2f = oh2.astype(F32)
    c1 = jnp.dot(lstrict, oh1f.astype(BF16), preferred_element_type=F32)
    c2 = jnp.dot(lstrict, oh2f.astype(BF16), preferred_element_type=F32)
    carry = carry_ref[...]
    tot1 = jnp.sum(oh1f, axis=0, keepdims=True)
    tot2 = jnp.sum(oh2f, axis=0, keepdims=True)
    rank1 = jnp.sum(jnp.where(oh1, c1 + carry, 0.0), axis=-1, keepdims=True)
    rank2 = jnp.sum(jnp.where(oh2, c2 + (carry + tot1), 0.0), axis=-1, keepdims=True)
    new_carry = carry + tot1 + tot2
    carry_ref[...] = new_carry
    cnt_ref[...] = new_carry

    cols = (e1.astype(F32), e2.astype(F32), rank1, rank2, gate1, gate2)
    route = jnp.zeros((tm, ROUTE_W), F32)
    for j, col in enumerate(cols):
        route = jnp.where(lane == j, col, route)
    route_ref[...] = route


def _post_mix(o_a, o_b, x2d, w_out_b, ln_g, ln_b, w_router, b_router, *, tm=512):
    n = x2d.shape[0]
    return pl.pallas_call(
        functools.partial(_post_mix_kernel, tm=tm),
        grid=(n // tm,),
        in_specs=[pl.BlockSpec((tm, GLA_WIDTH), lambda i: (i, 0)),
                  pl.BlockSpec((tm, GDN_WIDTH), lambda i: (i, 0)),
                  pl.BlockSpec((tm, D_MODEL), lambda i: (i, 0)),
                  pl.BlockSpec((GLA_WIDTH + GDN_WIDTH, D_MODEL), lambda i: (0, 0)),
                  pl.BlockSpec((1, D_MODEL), lambda i: (0, 0)),
                  pl.BlockSpec((1, D_MODEL), lambda i: (0, 0)),
                  pl.BlockSpec((D_MODEL, 2 * ROUTE_W), lambda i: (0, 0)),
                  pl.BlockSpec((1, ROUTE_W), lambda i: (0, 0))],
        out_specs=[pl.BlockSpec((tm * TOK_ROWS, 128), lambda i: (i, 0)),
                   pl.BlockSpec((tm, ROUTE_W), lambda i: (i, 0)),
                   pl.BlockSpec((1, ROUTE_W), lambda i: (0, 0))],
        out_shape=[jax.ShapeDtypeStruct((n * TOK_ROWS, 128), jnp.uint32),
                   jax.ShapeDtypeStruct((n, ROUTE_W), F32),
                   jax.ShapeDtypeStruct((1, ROUTE_W), F32)],
        scratch_shapes=[pltpu.VMEM((1, ROUTE_W), F32)],
        compiler_params=_cparams("arbitrary"),
        name="post_mix",
    )(o_a, o_b, x2d, w_out_b, ln_g, ln_b, w_router, b_router)


EXPERT_BLOCK = 256


ROW_UNROLL = 8


def _plan_kernel(route_ref, cnt_ref, dest_ref, *, tm, tb):
    counts = cnt_ref[...].astype(jnp.int32)
    padded = ((counts + (tb - 1)) & jnp.int32(-tb)).astype(F32)
    ri = lax.broadcasted_iota(jnp.int32, (ROUTE_W, ROUTE_W), 0)
    ci = lax.broadcasted_iota(jnp.int32, (ROUTE_W, ROUTE_W), 1)
    pstart = jnp.dot(jnp.broadcast_to(padded, (8, ROUTE_W)), (ri < ci).astype(F32),
                     precision=HIGHEST, preferred_element_type=F32)[0:1, :]
    route = route_ref[...]
    lane = lax.broadcasted_iota(jnp.int32, (tm, ROUTE_W), 1).astype(F32)
    out = jnp.zeros((tm, ROUTE_W), F32)
    for k, (ce, cr) in enumerate(((R_E1, R_RANK1), (R_E2, R_RANK2))):
        off = jnp.sum(jnp.where(lane == route[:, ce:ce + 1], pstart, 0.0), axis=-1, keepdims=True)
        out = jnp.where(lane == float(k), off + route[:, cr:cr + 1], out)
    dest_ref[...] = out.T[0:8, :].astype(jnp.int32) * TOK_ROWS


def _plan(route, counts, *, tb, tm=2048):
    n = route.shape[0]
    tm = min(tm, n)
    return pl.pallas_call(
        functools.partial(_plan_kernel, tm=tm, tb=tb),
        grid=(n // tm,),
        in_specs=[pl.BlockSpec((tm, ROUTE_W), lambda i: (i, 0)),
                  pl.BlockSpec((1, ROUTE_W), lambda i: (0, 0))],
        out_specs=pl.BlockSpec((8, tm), lambda i: (0, i)),
        out_shape=jax.ShapeDtypeStruct((8, n), jnp.int32),
        compiler_params=_cparams("parallel"),
        name="plan",
    )(route, counts)


def _dispatch_kernel(d1_ref, d2_ref, xp_ref, xs_init_ref, xs_ref, sem, *, tm):
    del xs_init_ref
    base = pl.program_id(0) * tm

    def body(j, carry):
        for u in range(ROW_UNROLL):
            r = j * ROW_UNROLL + u
            for k, d_ref in enumerate((d1_ref, d2_ref)):
                dst = pl.multiple_of(d_ref[base + r], TOK_ROWS)
                pltpu.make_async_copy(xp_ref.at[pl.ds(r * TOK_ROWS, TOK_ROWS), :],
                                      xs_ref.at[pl.ds(dst, TOK_ROWS), :], sem).start(priority=k)
        return carry
    lax.fori_loop(0, tm // ROW_UNROLL, body, 0)
    for _ in range(TOP_K):
        pltpu.make_async_copy(xp_ref, xs_ref.at[pl.ds(0, tm * TOK_ROWS), :], sem).wait()


def _dispatch(dest1, dest2, x1p, n_rows, *, tm=512):
    n = x1p.shape[0] // TOK_ROWS
    grid_spec = pltpu.PrefetchScalarGridSpec(
        num_scalar_prefetch=2,
        grid=(n // tm,),
        in_specs=[pl.BlockSpec((tm * TOK_ROWS, 128), lambda i, a, b: (i, 0)),
                  pl.BlockSpec(memory_space=pl.ANY)],
        out_specs=pl.BlockSpec(memory_space=pl.ANY),
        scratch_shapes=[pltpu.SemaphoreType.DMA(())],
    )
    return pl.pallas_call(
        functools.partial(_dispatch_kernel, tm=tm),
        grid_spec=grid_spec,
        out_shape=jax.ShapeDtypeStruct((n_rows * TOK_ROWS, 128), jnp.uint32),
        input_output_aliases={3: 0},
        compiler_params=_cparams("arbitrary"),
        name="dispatch",
    )(dest1, dest2, x1p, jnp.zeros((n_rows * TOK_ROWS, 128), jnp.uint32))


def _expert_kernel(start_ref, nblk_ref, nvalid_ref, wg_ref, wu_ref, wd_ref, xs_hbm, y_hbm,
                   wgb, wub, wdb, xbuf, ybuf, xsem, ysem, zsem, *, tb, nb):
    e = pl.program_id(0)
    nvalid = nvalid_ref[0]
    rows = tb * TOK_ROWS

    def x_copy(g, slot):
        return pltpu.make_async_copy(xs_hbm.at[pl.ds(pl.multiple_of(g * rows, rows), rows), :], xbuf.at[slot],
                                     xsem.at[slot])

    def y_copy(g, slot):
        return pltpu.make_async_copy(ybuf.at[slot], y_hbm.at[pl.ds(pl.multiple_of(g * rows, rows), rows), :],
                                     ysem.at[slot])

    @pl.when((e == 0) & (nvalid > 0))
    def _():
        x_copy(0, 0).start()

    @pl.when(nblk_ref[e] > 0)
    def _():
        wgb[...] = wg_ref[...].astype(BF16)
        wub[...] = wu_ref[...].astype(BF16)
        wdb[...] = wd_ref[...].astype(BF16)

    def block(b, carry):
        g = start_ref[e] + b
        slot = g % 2
        x_copy(g, slot).wait()

        @pl.when(g + 1 < nvalid)
        def _():
            x_copy(g + 1, 1 - slot).start(priority=1)

        lo, hi = _unpack_bf16_pairs(_load_token_rows(xbuf.at[slot], tb))
        lo = lo.astype(BF16)
        hi = hi.astype(BF16)
        hg = jnp.dot(lo, wgb[0:HALF, :], preferred_element_type=F32)
        hg = hg + jnp.dot(hi, wgb[HALF:, :], preferred_element_type=F32)
        hu = jnp.dot(lo, wub[0:HALF, :], preferred_element_type=F32)
        hu = hu + jnp.dot(hi, wub[HALF:, :], preferred_element_type=F32)
        hid = (_silu(hg) * hu).astype(BF16)
        y = _pack_bf16_pairs(jnp.dot(hid, wdb[...], preferred_element_type=F32))

        @pl.when(g >= 2)
        def _():
            y_copy(g - 2, slot).wait()

        _store_token_rows(ybuf.at[slot], y)
        y_copy(g, slot).start()
        return carry

    lax.fori_loop(0, nblk_ref[e], block, 0)

    @pl.when(e == pl.num_programs(0) - 1)
    def _():
        for back in (1, 2):
            @pl.when(nvalid >= back)
            def _():
                y_copy(nvalid - back, (nvalid - back) % 2).wait()

        xbuf[0] = jnp.zeros((rows, 128), jnp.uint32)

        def zero_copy(g):
            return pltpu.make_async_copy(xbuf.at[0], y_hbm.at[pl.ds(pl.multiple_of(g * rows, rows), rows), :], zsem)

        def zstart(g, carry):
            zero_copy(g).start()
            return carry

        def zwait(g, carry):
            zero_copy(g).wait()
            return carry

        lax.fori_loop(nvalid, nb, zstart, 0)
        lax.fori_loop(nvalid, nb, zwait, 0)


def _experts(blk_start, nblk, nvalid, xs, w_gate, w_up, w_down, *, nb, tb=EXPERT_BLOCK):
    w_map = lambda e, st, nk, nv: (e, 0, 0)
    grid_spec = pltpu.PrefetchScalarGridSpec(
        num_scalar_prefetch=3,
        grid=(N_EXPERTS,),
        in_specs=[pl.BlockSpec((None, D_MODEL, D_EXPERT), w_map),
                  pl.BlockSpec((None, D_MODEL, D_EXPERT), w_map),
                  pl.BlockSpec((None, D_EXPERT, D_MODEL), w_map),
                  pl.BlockSpec(memory_space=pl.ANY)],
        out_specs=pl.BlockSpec(memory_space=pl.ANY),
        scratch_shapes=[pltpu.VMEM((D_MODEL, D_EXPERT), BF16),
                        pltpu.VMEM((D_MODEL, D_EXPERT), BF16),
                        pltpu.VMEM((D_EXPERT, D_MODEL), BF16),
                        pltpu.VMEM((2, tb * TOK_ROWS, 128), jnp.uint32),
                        pltpu.VMEM((2, tb * TOK_ROWS, 128), jnp.uint32),
                        pltpu.SemaphoreType.DMA((2,)),
                        pltpu.SemaphoreType.DMA((2,)),
                        pltpu.SemaphoreType.DMA(())],
    )
    return pl.pallas_call(
        functools.partial(_expert_kernel, tb=tb, nb=nb),
        grid_spec=grid_spec,
        out_shape=jax.ShapeDtypeStruct((nb * tb * TOK_ROWS, 128), jnp.uint32),
        compiler_params=_cparams("arbitrary"),
        name="experts",
    )(blk_start, nblk, nvalid, w_gate, w_up, w_down, xs)


def _combine_kernel(d1_ref, d2_ref, y_hbm, x1_ref, route_ref, g_ref, b_ref, o_ref, ybuf, sem, *, tt):
    i = pl.program_id(0)
    nsteps = pl.num_programs(0)
    slot = i % 2

    def issue(tile, slot_):
        def body(j, carry):
            for u in range(ROW_UNROLL):
                r = j * ROW_UNROLL + u
                for k, d_ref in enumerate((d1_ref, d2_ref)):
                    src = pl.multiple_of(d_ref[tile * tt + r], TOK_ROWS)
                    pltpu.make_async_copy(y_hbm.at[pl.ds(src, TOK_ROWS), :],
                                          ybuf.at[slot_, k, pl.ds(r * TOK_ROWS, TOK_ROWS), :],
                                          sem.at[slot_]).start(priority=k)
            return carry
        lax.fori_loop(0, tt // ROW_UNROLL, body, 0)

    @pl.when(i == 0)
    def _():
        issue(0, 0)

    @pl.when(i + 1 < nsteps)
    def _():
        issue(i + 1, 1 - slot)

    for k in range(TOP_K):
        pltpu.make_async_copy(y_hbm.at[pl.ds(0, tt * TOK_ROWS), :], ybuf.at[slot, k], sem.at[slot]).wait()
    route = route_ref[...]
    g1 = route[:, R_G1:R_G1 + 1]
    g2 = route[:, R_G2:R_G2 + 1]
    lo1, hi1 = _unpack_bf16_pairs(_load_token_rows(ybuf.at[slot, 0], tt))
    lo2, hi2 = _unpack_bf16_pairs(_load_token_rows(ybuf.at[slot, 1], tt))
    x_lo, x_hi = _unpack_bf16_pairs(_load_token_rows(x1_ref, tt))
    h_lo = ALPHA_DN * x_lo + (g1 * lo1 + g2 * lo2)
    h_hi = ALPHA_DN * x_hi + (g1 * hi1 + g2 * hi2)
    mu = (jnp.sum(h_lo, axis=-1, keepdims=True) + jnp.sum(h_hi, axis=-1, keepdims=True)) * (1.0 / D_MODEL)
    c_lo = h_lo - mu
    c_hi = h_hi - mu
    var = (jnp.sum(c_lo * c_lo, axis=-1, keepdims=True) + jnp.sum(c_hi * c_hi, axis=-1, keepdims=True)) * (1.0 / D_MODEL)
    inv = lax.rsqrt(var + LN_EPS)
    o_ref[:, 0:HALF] = c_lo * inv * g_ref[:, 0:HALF] + b_ref[:, 0:HALF]
    o_ref[:, HALF:] = c_hi * inv * g_ref[:, HALF:] + b_ref[:, HALF:]


def _combine(dest1, dest2, ybuf, x1p, route, ln_g, ln_b, *, tt=256):
    n = x1p.shape[0] // TOK_ROWS
    grid_spec = pltpu.PrefetchScalarGridSpec(
        num_scalar_prefetch=2,
        grid=(n // tt,),
        in_specs=[pl.BlockSpec(memory_space=pl.ANY),
                  pl.BlockSpec((tt * TOK_ROWS, 128), lambda i, a, b: (i, 0)),
                  pl.BlockSpec((tt, ROUTE_W), lambda i, a, b: (i, 0)),
                  pl.BlockSpec((1, D_MODEL), lambda i, a, b: (0, 0)),
                  pl.BlockSpec((1, D_MODEL), lambda i, a, b: (0, 0))],
        out_specs=pl.BlockSpec((tt, D_MODEL), lambda i, a, b: (i, 0)),
        scratch_shapes=[pltpu.VMEM((2, TOP_K, tt * TOK_ROWS, 128), jnp.uint32),
                        pltpu.SemaphoreType.DMA((2,))],
    )
    return pl.pallas_call(
        functools.partial(_combine_kernel, tt=tt),
        grid_spec=grid_spec,
        out_shape=jax.ShapeDtypeStruct((n, D_MODEL), F32),
        compiler_params=_cparams("arbitrary"),
        name="combine",
    )(dest1, dest2, ybuf, x1p, route, ln_g, ln_b)


def _block_plan(counts_row, n, tb):
    counts = counts_row[0, :N_EXPERTS].astype(jnp.int32)
    nblk = (counts + tb - 1) // tb
    ends = jnp.cumsum(nblk)
    return ends - nblk, nblk, ends[-1:]


def kernel(x, w_in, w_gk_up, b_gk, conv_w, a_log, dt_bias, gla_norm_g, gdn_norm_g, w_out, ln1_g, ln1_b, w_router_group, b_router_group, w_router_expert, b_router_expert, w_gate, w_up, w_down, ln2_g, ln2_b):
    bsz, seq, d = x.shape
    n = bsz * seq
    x2d = x.reshape(n, d)
    w_main, w_aux = _split_w_in(w_in[0])
    main, aux = _in_proj(x2d, w_main, w_aux, conv_w[0].astype(F32), seq=seq)
    wgk = jnp.zeros((AUX_W, 256), F32).at[:GLA_GATE_RANK].set(w_gk_up[0])
    o_a = _gla(main, aux, wgk, b_gk[0][None], gla_norm_g[0][None], bsz=bsz, seq=seq)
    o_b = _gdn(main, aux, _pad_aux_row(a_log[0], AUX_AB), _pad_aux_row(dt_bias[0], AUX_AB),
               gdn_norm_g[0][None], bsz=bsz, seq=seq)
    rpad = jnp.zeros((d, ROUTE_W - N_EXPERTS - N_GROUPS), F32)
    w_router = jnp.concatenate([w_router_expert[0], w_router_group[0], rpad], axis=1)
    w_router_hi = w_router.astype(BF16)
    w_router = jnp.concatenate([w_router_hi, (w_router - w_router_hi.astype(F32)).astype(BF16)], axis=1)
    b_router = jnp.concatenate([b_router_expert[0], b_router_group[0], rpad[0]])[None]
    x1p, route, counts = _post_mix(o_a, o_b, x2d, w_out[0].astype(BF16), ln1_g[0][None], ln1_b[0][None],
                                       w_router, b_router)
    dest = _plan(route, counts, tb=EXPERT_BLOCK)
    blk_start, nblk, nvalid = _block_plan(counts, n, EXPERT_BLOCK)
    nb = (n * TOP_K) // EXPERT_BLOCK + N_EXPERTS
    xs = _dispatch(dest[0], dest[1], x1p, nb * EXPERT_BLOCK)
    ybuf = _experts(blk_start, nblk, nvalid, xs, w_gate[0], w_up[0], w_down[0], nb=nb)
    out = _combine(dest[0], dest[1], ybuf, x1p, route, ln2_g[0][None], ln2_b[0][None])
    return out.reshape(bsz, seq, d)
```

```python
import functools

import jax
import jax.numpy as jnp
import numpy as np
from jax import lax
from jax.experimental import pallas as pl
from jax.experimental.pallas import tpu as pltpu

F32 = jnp.float32
BF16 = jnp.bfloat16
HIGHEST = lax.Precision.HIGHEST

D_MODEL = 1024
DEPTH = 1
GLA_HEADS = 4
GLA_DK = 64
GLA_DV = 128
GLA_WIDTH = GLA_HEADS * GLA_DV
GLA_GATE_RANK = 16
GLA_GATE_NORM = 16.0
GDN_HEADS = 4
GDN_DK = 128
GDN_DV = 128
GDN_WIDTH = GDN_HEADS * GDN_DV
CONV_W = 4
CHUNK = 64
N_GROUPS = 8
EXPERTS_PER_GROUP = 8
N_EXPERTS = N_GROUPS * EXPERTS_PER_GROUP
TOP_K = 2
D_EXPERT = 512
LN_EPS = 1e-5
RMS_EPS = 1e-6
ALPHA_DN = (2.0 * DEPTH) ** 0.25

MAIN_W = 2 * GLA_HEADS * GLA_DK + 2 * GLA_WIDTH + 4 * GDN_WIDTH
AUX_W = 128
AUX_LRA = 0
AUX_BB = GLA_GATE_RANK
AUX_AB = GLA_GATE_RANK + GDN_HEADS
SUB = 16
VMEM_LIMIT = 56 * 1024 * 1024


def _cparams(*sem):
    return pltpu.CompilerParams(dimension_semantics=sem, vmem_limit_bytes=VMEM_LIMIT)


def _sigmoid(x):
    return 1.0 / (1.0 + jnp.exp(-x))


def _silu(x):
    return x * _sigmoid(x)


def _log_sigmoid(x):
    return jnp.minimum(x, 0.0) - jnp.log(1.0 + jnp.exp(-jnp.abs(x)))


def _softplus(x):
    return jnp.maximum(x, 0.0) + jnp.log(1.0 + jnp.exp(-jnp.abs(x)))


CONV_PAD = 8
GDN_HW = GDN_HEADS * GDN_DK
GDN_QKV_BLOCK0 = (2 * GLA_HEADS * GLA_DK + 2 * GLA_WIDTH) // GDN_HW


def _in_proj_kernel(x_ref, wm_ref, wa_ref, cw_ref, main_ref, aux_ref, stage_ref, *, tm, tiles_per_seq):
    xb = x_ref[...].astype(BF16)

    @pl.when((pl.program_id(0) % tiles_per_seq) == 0)
    def _():
        stage_ref[:, 0:CONV_PAD, :] = jnp.zeros((3 * GDN_HEADS, CONV_PAD, GDN_DK), F32)

    for j in range(MAIN_W // GDN_HW):
        sl = slice(j * GDN_HW, (j + 1) * GDN_HW)
        r = jnp.dot(xb, wm_ref[:, sl], preferred_element_type=F32)
        b = j - GDN_QKV_BLOCK0
        if not 0 <= b < 3:
            main_ref[:, sl] = r.astype(BF16)
            continue
        for h in range(GDN_HEADS):
            st = stage_ref.at[b * GDN_HEADS + h]
            cols = slice(b * GDN_HW + h * GDN_DK, b * GDN_HW + (h + 1) * GDN_DK)
            st[CONV_PAD:, :] = r[:, h * GDN_DK:(h + 1) * GDN_DK]
            acc = jnp.zeros((tm, GDN_DK), F32)
            for i in range(CONV_W):
                lo = CONV_PAD - (CONV_W - 1) + i
                acc = acc + st[lo:lo + tm, :] * cw_ref[i:i + 1, cols]
            st[0:CONV_PAD, :] = st[tm:tm + CONV_PAD, :]
            yh = _silu(acc)
            if b < 2:
                yh = yh * lax.rsqrt(jnp.sum(yh * yh, axis=-1, keepdims=True) + RMS_EPS)
            if b == 0:
                yh = yh * (GDN_DK ** -0.5)
            main_ref[:, j * GDN_HW + h * GDN_DK:j * GDN_HW + (h + 1) * GDN_DK] = yh.astype(BF16)
    aux_ref[...] = jnp.dot(xb, wa_ref[...], preferred_element_type=F32)


def _in_proj(x2d, w_main, w_aux, conv_w, *, seq, tm=512):
    n = x2d.shape[0]
    return pl.pallas_call(
        functools.partial(_in_proj_kernel, tm=tm, tiles_per_seq=seq // tm),
        grid=(n // tm,),
        in_specs=[pl.BlockSpec((tm, D_MODEL), lambda i: (i, 0)),
                  pl.BlockSpec((D_MODEL, MAIN_W), lambda i: (0, 0)),
                  pl.BlockSpec((D_MODEL, AUX_W), lambda i: (0, 0)),
                  pl.BlockSpec((CONV_W, 3 * GDN_HW), lambda i: (0, 0))],
        out_specs=[pl.BlockSpec((tm, MAIN_W), lambda i: (i, 0)),
                   pl.BlockSpec((tm, AUX_W), lambda i: (i, 0))],
        out_shape=[jax.ShapeDtypeStruct((n, MAIN_W), BF16),
                   jax.ShapeDtypeStruct((n, AUX_W), F32)],
        scratch_shapes=[pltpu.VMEM((3 * GDN_HEADS, tm + CONV_PAD, GDN_DK), F32)],
        compiler_params=_cparams("arbitrary"),
        name="in_proj",
    )(x2d, w_main, w_aux, conv_w)


def _split_w_in(w_in):
    sizes = (GLA_HEADS * GLA_DK, GLA_HEADS * GLA_DK, GLA_WIDTH, GLA_WIDTH, GLA_GATE_RANK,
             GDN_HEADS * GDN_DK, GDN_HEADS * GDN_DK, GDN_WIDTH, GDN_WIDTH, GDN_HEADS, GDN_HEADS)
    offs = np.cumsum((0,) + sizes)
    seg = [w_in[:, offs[i]:offs[i + 1]] for i in range(len(sizes))]
    qa, ka, va, ra, lra, qb, kb, vb, zb, bb, ab = seg
    w_main = jnp.concatenate([qa, ka, va, ra, qb, kb, vb, zb], axis=1).astype(BF16)
    pad = jnp.zeros((w_in.shape[0], AUX_W - GLA_GATE_RANK - 2 * GDN_HEADS), w_in.dtype)
    w_aux = jnp.concatenate([lra, bb, ab, pad], axis=1).astype(BF16)
    return w_main, w_aux


_NT = (((1,), (1,)), ((), ()))
_TN = (((0,), (0,)), ((), ()))


def _gla_kernel(q_ref, k_ref, v_ref, r_ref, aux_ref, wgk_ref, bgk_ref, g_ref, o_ref, st_ref, *, ts):
    c = CHUNK
    nsub = c // SUB

    @pl.when(pl.program_id(1) == 0)
    def _():
        st_ref[...] = jnp.zeros_like(st_ref)

    pre = jnp.dot(aux_ref[...], wgk_ref[...], precision=HIGHEST, preferred_element_type=F32) + bgk_ref[...]
    gk_all = _log_sigmoid(pre) * (1.0 / GLA_GATE_NORM)

    ri = lax.broadcasted_iota(jnp.int32, (c, c), 0)
    ci = lax.broadcasted_iota(jnp.int32, (c, c), 1)
    tri = (ci <= ri).astype(F32)
    row128 = lax.broadcasted_iota(jnp.int32, (c, 128), 0)
    lane128 = lax.broadcasted_iota(jnp.int32, (1, 128), 1)
    lane_masks = [(lane128 // GLA_DK) == hh for hh in range(2)]
    sub_valid = [row128 < SUB * (i + 1) for i in range(nsub)]
    ar = lax.broadcasted_iota(jnp.int32, (c, nsub * c), 0)
    ac = lax.broadcasted_iota(jnp.int32, (c, nsub * c), 1)
    amask = ((ac // c) == (ar // SUB)) & ((ac % c) <= ar)
    scale = GLA_DK ** -0.5
    gnorm = g_ref[...]

    nch = ts // c
    rows = [slice(ch * c, (ch + 1) * c) for ch in range(nch)]
    b_all = [jnp.dot(tri, gk_all[r], precision=HIGHEST, preferred_element_type=F32) for r in rows]

    cp = [(ch, p) for ch in range(nch) for p in range(2)]
    qt, qd, kd, kstack, dlast = {}, {}, {}, {}, {}
    for ch, p in cp:
        lanes = slice(128 * p, 128 * (p + 1))
        bp = b_all[ch][:, lanes]
        qp = q_ref[rows[ch], lanes].astype(F32) * scale
        kp = k_ref[rows[ch], lanes].astype(F32)
        c_row = jnp.concatenate(
            [jnp.broadcast_to(bp[SUB * i:SUB * i + 1, :], (SUB, 128)) for i in range(nsub)], axis=0)
        qt[ch, p] = (qp * jnp.exp(bp - c_row)).astype(BF16)
        qd[ch, p] = qp * jnp.exp(bp)
        b_last = bp[c - 1:c, :]
        dlast[ch, p] = jnp.exp(b_last)
        kd[ch, p] = kp * jnp.exp(b_last - bp)
        kts = []
        for i in range(nsub):
            e = jnp.exp(jnp.where(sub_valid[i], bp[SUB * i:SUB * i + 1, :] - bp, 0.0))
            kts.append(jnp.where(sub_valid[i], kp * e, 0.0))
        kstack[ch, p] = jnp.concatenate(kts, axis=0)

    cph = [(ch, p, hh) for ch, p in cp for hh in range(2)]
    vh = {(ch, p, hh): v_ref[rows[ch], 128 * (2 * p + hh):128 * (2 * p + hh + 1)] for ch, p, hh in cph}
    r_mat = {(ch, p, hh): lax.dot_general(qt[ch, p], jnp.where(lane_masks[hh], kstack[ch, p], 0.0).astype(BF16),
                                          _NT, preferred_element_type=F32) for ch, p, hh in cph}
    o_intra = {k: jnp.dot(jnp.where(amask, r_mat[k], 0.0).astype(BF16), jnp.concatenate([vh[k]] * nsub, axis=0),
                          preferred_element_type=F32) for k in cph}
    upd = {(ch, p, hh): lax.dot_general(vh[ch, p, hh], jnp.where(lane_masks[hh], kd[ch, p], 0.0).astype(BF16), _TN,
                                        preferred_element_type=F32) for ch, p, hh in cph}

    st = [st_ref[p] for p in range(2)]
    for ch in range(nch):
        st_b = [s.astype(BF16) for s in st]
        o_inter = {(p, hh): lax.dot_general(jnp.where(lane_masks[hh], qd[ch, p], 0.0).astype(BF16), st_b[p], _NT,
                                            preferred_element_type=F32) for p in range(2) for hh in range(2)}
        st = [st[p] * dlast[ch, p] + upd[ch, p, 0] + upd[ch, p, 1] for p in range(2)]
        for p in range(2):
            for hh in range(2):
                h = 2 * p + hh
                o = o_intra[ch, p, hh] + o_inter[p, hh]
                o = o * lax.rsqrt(jnp.mean(o * o, axis=-1, keepdims=True) + RMS_EPS) * gnorm
                gate = _silu(r_ref[rows[ch], 128 * h:128 * (h + 1)].astype(F32))
                o_ref[rows[ch], 128 * h:128 * (h + 1)] = (o * gate).astype(BF16)
    for p in range(2):
        st_ref[p] = st[p]


def _gla(main, aux, wgk_pad, b_gk, gla_norm_g, *, bsz, seq, ts=256):
    nt = seq // ts
    row = lambda b, t: b * nt + t
    return pl.pallas_call(
        functools.partial(_gla_kernel, ts=ts),
        grid=(bsz, nt),
        in_specs=[pl.BlockSpec((ts, 256), lambda b, t: (row(b, t), 0)),
                  pl.BlockSpec((ts, 256), lambda b, t: (row(b, t), 1)),
                  pl.BlockSpec((ts, 512), lambda b, t: (row(b, t), 1)),
                  pl.BlockSpec((ts, 512), lambda b, t: (row(b, t), 2)),
                  pl.BlockSpec((ts, AUX_W), lambda b, t: (row(b, t), 0)),
                  pl.BlockSpec((AUX_W, 256), lambda b, t: (0, 0)),
                  pl.BlockSpec((1, 256), lambda b, t: (0, 0)),
                  pl.BlockSpec((1, 128), lambda b, t: (0, 0))],
        out_specs=pl.BlockSpec((ts, GLA_WIDTH), lambda b, t: (row(b, t), 0)),
        out_shape=jax.ShapeDtypeStruct((bsz * seq, GLA_WIDTH), BF16),
        scratch_shapes=[pltpu.VMEM((2, 128, 128), F32)],
        compiler_params=_cparams("parallel", "arbitrary"),
        name="gla",
    )(main, main, main, main, aux, wgk_pad, b_gk, gla_norm_g)


def _gdn_kernel(q_ref, k_ref, v_ref, z_ref, aux_ref, alog_ref, dtb_ref, g_ref, o_ref, st_ref, *, ts):
    c = CHUNK
    nch = ts // c

    @pl.when(pl.program_id(1) == 0)
    def _():
        st_ref[...] = jnp.zeros_like(st_ref)

    aux = aux_ref[...]
    beta_full = _sigmoid(aux)
    g_full = -jnp.exp(alog_ref[...]) * _softplus(aux + dtb_ref[...])
    ri = lax.broadcasted_iota(jnp.int32, (ts, ts), 0)
    ci = lax.broadcasted_iota(jnp.int32, (ts, ts), 1)
    cum = (((ri // c) == (ci // c)) & (ci <= ri)).astype(F32)
    gam_full = jnp.dot(cum, g_full, precision=HIGHEST, preferred_element_type=F32)
    gam_t = gam_full.T
    gnorm = g_ref[...]

    row = lax.broadcasted_iota(jnp.int32, (c, 2 * c), 0)
    lane = lax.broadcasted_iota(jnp.int32, (c, 2 * c), 1)
    left = lane < c
    col = lane % c
    causal = col <= row
    strict = col < row
    same32 = (row // 32) == (col // 32)
    same16 = (row // 16) == (col // 16)
    lvl0 = same16 & strict
    lvl1 = same32 & jnp.logical_not(same16) & strict
    lvl2 = jnp.logical_not(same32) & strict

    def pack_rows(x2):
        return jnp.where(left, x2[0:c], x2[c:2 * c])

    def pack_cols(x1):
        return jnp.where(left, x1[0:c], x1[c:2 * c])

    def halves(x):
        return jnp.where(left, x, 0.0).astype(BF16), jnp.where(left, 0.0, x).astype(BF16)

    def mmp(x, y):
        return jnp.dot(x.astype(BF16), jnp.concatenate(halves(y), axis=0), preferred_element_type=F32)

    heads = range(GDN_HEADS)
    npair = nch // 2
    items = [(h, pr) for h in heads for pr in range(npair)]
    prow = [slice(2 * c * pr, 2 * c * (pr + 1)) for pr in range(npair)]

    qn, kn, beta, gam, gam_r, kb, rhs, rhs_b, qd, kd = ([None] * GDN_HEADS for _ in range(10))
    for h in heads:
        hl = slice(128 * h, 128 * (h + 1))
        kb[h] = k_ref[:, hl]
        qn[h] = q_ref[:, hl].astype(F32)
        kn[h] = kb[h].astype(F32)
        vh = v_ref[:, hl].astype(F32)
        beta[h] = beta_full[:, AUX_BB + h:AUX_BB + h + 1]
        gam[h] = gam_full[:, AUX_AB + h:AUX_AB + h + 1]
        gam_r[h] = gam_t[AUX_AB + h:AUX_AB + h + 1, :]
        egam = jnp.exp(gam[h])
        rhs[h] = jnp.concatenate([beta[h] * vh, beta[h] * egam * kn[h]], axis=1)
        rhs_b[h] = rhs[h].astype(BF16)
        qd[h] = (qn[h] * egam).astype(BF16)
        gl_rows = jnp.concatenate(
            [jnp.broadcast_to(gam[h][c * (j + 1) - 1:c * (j + 1), :], (c, 1)) for j in range(nch)], axis=0)
        kd[h] = (kn[h] * jnp.exp(gl_rows - gam[h])).astype(BF16)

    kk = [pack_rows(lax.dot_general(kb[h][prow[pr]], kb[h][prow[pr]], _NT, preferred_element_type=F32))
          for h, pr in items]
    qk = [pack_rows(lax.dot_general(qn[h][prow[pr]].astype(BF16), kb[h][prow[pr]], _NT,
                                    preferred_element_type=F32)) for h, pr in items]
    dec = [jnp.exp(jnp.where(causal, pack_cols(gam[h][prow[pr]]) - gam_r[h][:, prow[pr]], 0.0)) for h, pr in items]
    a = [pack_cols(beta[h][prow[pr]]) * kk[i] * dec[i] for i, (h, pr) in enumerate(items)]
    attn = [halves(jnp.where(causal, qk[i] * dec[i], 0.0)) for i in range(len(items))]

    x1 = [jnp.where(lvl0, -ai, 0.0) for ai in a]
    x2 = [mmp(x, x) for x in x1]
    y = [mmp(x, xx) for x, xx in zip(x1, x2)]
    p = [x + xx + yy for x, xx, yy in zip(x1, x2, y)]
    xk = x2
    for _ in range(2):
        xk = [mmp(x, x) for x in xk]
        y = [mmp(pp, x) for pp, x in zip(p, xk)]
        p = [pp + x + yy for pp, x, yy in zip(p, xk, y)]
    for lvl in (lvl1, lvl2):
        low = [jnp.where(lvl, ai, 0.0) for ai in a]
        y = [mmp(pp, lo) for pp, lo in zip(p, low)]
        zz = [lo + yy for lo, yy in zip(low, y)]
        y = [mmp(z, pp) for z, pp in zip(zz, p)]
        p = [pp - z - yy for pp, z, yy in zip(p, zz, y)]

    us = [[None] * nch for _ in heads]
    ws = [[None] * nch for _ in heads]
    for i, (h, pr) in enumerate(items):
        for half, p_half in enumerate(halves(p[i])):
            j = 2 * pr + half
            uw = rhs[h][c * j:c * (j + 1)] + jnp.dot(p_half, rhs_b[h][prow[pr]], preferred_element_type=F32)
            us[h][j] = uw[:, :GDN_DV]
            ws[h][j] = uw[:, GDN_DV:].astype(BF16)

    s = [st_ref[h] for h in heads]
    vprev = [None] * GDN_HEADS
    for j in range(nch):
        rows = slice(c * j, c * (j + 1))
        sb = [s[h].astype(BF16) for h in heads]
        vn16 = [(us[h][j] - jnp.dot(ws[h][j], sb[h], preferred_element_type=F32)).astype(BF16) for h in heads]
        o_inter = [jnp.dot(qd[h][rows], sb[h], preferred_element_type=F32) for h in heads]
        for h in heads:
            vpair = jnp.concatenate([vn16[h], jnp.zeros_like(vn16[h])] if j % 2 == 0 else [vprev[h], vn16[h]], axis=0)
            o = o_inter[h] + jnp.dot(attn[h * npair + j // 2][j % 2], vpair, preferred_element_type=F32)
            s[h] = jnp.exp(gam[h][c * (j + 1) - 1:c * (j + 1), :]) * s[h] + lax.dot_general(
                kd[h][rows], vn16[h], _TN, preferred_element_type=F32)
            o = o * lax.rsqrt(jnp.mean(o * o, axis=-1, keepdims=True) + RMS_EPS) * gnorm
            gate = _silu(z_ref[rows, 128 * h:128 * (h + 1)].astype(F32))
            o_ref[rows, 128 * h:128 * (h + 1)] = (o * gate).astype(BF16)
        vprev = vn16
    for h in heads:
        st_ref[h] = s[h]


def _gdn(main, aux, alog_pad, dtb_pad, gdn_norm_g, *, bsz, seq, ts=256):
    nt = seq // ts
    row = lambda b, t: b * nt + t
    hw = GDN_HEADS * GDN_DK
    return pl.pallas_call(
        functools.partial(_gdn_kernel, ts=ts),
        grid=(bsz, nt),
        in_specs=[pl.BlockSpec((ts, hw), lambda b, t: (row(b, t), 3)),
                  pl.BlockSpec((ts, hw), lambda b, t: (row(b, t), 4)),
                  pl.BlockSpec((ts, hw), lambda b, t: (row(b, t), 5)),
                  pl.BlockSpec((ts, hw), lambda b, t: (row(b, t), 6)),
                  pl.BlockSpec((ts, AUX_W), lambda b, t: (row(b, t), 0)),
                  pl.BlockSpec((1, AUX_W), lambda b, t: (0, 0)),
                  pl.BlockSpec((1, AUX_W), lambda b, t: (0, 0)),
                  pl.BlockSpec((1, GDN_DV), lambda b, t: (0, 0))],
        out_specs=pl.BlockSpec((ts, GDN_WIDTH), lambda b, t: (row(b, t), 0)),
        out_shape=jax.ShapeDtypeStruct((bsz * seq, GDN_WIDTH), BF16),
        scratch_shapes=[pltpu.VMEM((GDN_HEADS, GDN_DK, GDN_DV), F32)],
        compiler_params=_cparams("parallel", "arbitrary"),
        name="gdn",
    )(main, main, main, main, aux, alog_pad, dtb_pad, gdn_norm_g)


def _pad_aux_row(v, offset):
    return jnp.zeros((1, AUX_W), F32).at[0, offset:offset + v.shape[0]].set(v.astype(F32))


ROUTE_W = 128
R_E1, R_E2, R_RANK1, R_RANK2, R_G1, R_G2 = range(6)
ROUTER_GROUP_COL = N_EXPERTS


def _layer_norm(h, g, b):
    mu = jnp.mean(h, axis=-1, keepdims=True)
    hc = h - mu
    var = jnp.mean(hc * hc, axis=-1, keepdims=True)
    return hc * lax.rsqrt(var + LN_EPS) * g + b


def _pack_bf16_pairs(x):
    w = x.shape[1] // 2
    lo = lax.bitcast_convert_type(x[:, :w].astype(BF16).astype(F32), jnp.uint32)
    hi = lax.bitcast_convert_type(x[:, w:].astype(BF16).astype(F32), jnp.uint32)
    return (lo >> 16) | (hi & jnp.uint32(0xFFFF0000))


def _unpack_bf16_pairs(p):
    lo = lax.bitcast_convert_type(p << 16, F32)
    hi = lax.bitcast_convert_type(p & jnp.uint32(0xFFFF0000), F32)
    return lo, hi


HALF = D_MODEL // 2
TOK_ROWS = HALF // 128


def _store_token_rows(ref, packed):
    m = packed.shape[0]
    for c in range(TOK_ROWS):
        ref[pl.ds(c, m, stride=TOK_ROWS), :] = packed[:, 128 * c:128 * (c + 1)]


def _load_token_rows(ref, m):
    return jnp.concatenate([ref[pl.ds(c, m, stride=TOK_ROWS), :] for c in range(TOK_ROWS)], axis=1)


def _post_mix_kernel(oa_ref, ob_ref, x_ref, wo_ref, g_ref, b_ref, wr_ref, br_ref,
                     x1p_ref, route_ref, cnt_ref, carry_ref, *, tm):
    @pl.when(pl.program_id(0) == 0)
    def _():
        carry_ref[...] = jnp.zeros_like(carry_ref)

    y = jnp.dot(oa_ref[...], wo_ref[0:GLA_WIDTH, :], preferred_element_type=F32)
    y = y + jnp.dot(ob_ref[...], wo_ref[GLA_WIDTH:, :], preferred_element_type=F32)
    x1 = _layer_norm(ALPHA_DN * x_ref[...] + y, g_ref[...], b_ref[...])
    _store_token_rows(x1p_ref, _pack_bf16_pairs(x1))

    xh = x1.astype(BF16)
    xl = (x1 - xh.astype(F32)).astype(BF16)
    hw = jnp.dot(xh, wr_ref[...], preferred_element_type=F32)
    logits = (hw[:, :ROUTE_W] + hw[:, ROUTE_W:]
              + jnp.dot(xl, wr_ref[:, :ROUTE_W], preferred_element_type=F32) + br_ref[...])
    lane = lax.broadcasted_iota(jnp.int32, (tm, ROUTE_W), 1)
    big = jnp.int32(1 << 20)
    neg = jnp.float32(-jnp.inf)

    def first_argmax(vals):
        m = jnp.max(vals, axis=-1, keepdims=True)
        idx = jnp.min(jnp.where(vals == m, lane, big), axis=-1, keepdims=True)
        return m, idx

    is_group = (lane >= ROUTER_GROUP_COL) & (lane < ROUTER_GROUP_COL + N_GROUPS)
    gl = jnp.where(is_group, logits, neg)
    gmax, gidx = first_argmax(gl)
    p_group = 1.0 / jnp.sum(jnp.exp(gl - gmax), axis=-1, keepdims=True)
    g_sel = gidx - ROUTER_GROUP_COL
    el = jnp.where((lane // EXPERTS_PER_GROUP) == g_sel, logits, neg)
    t1, e1 = first_argmax(el)
    t2, e2 = first_argmax(jnp.where(lane == e1, neg, el))
    ex = jnp.exp(t2 - t1)
    w1 = 1.0 / (1.0 + ex)
    gate1 = p_group * w1
    gate2 = p_group * (ex * w1)

    oh1 = lane == e1
    oh2 = lane == e2
    ri = lax.broadcasted_iota(jnp.int32, (tm, tm), 0)
    ci = lax.broadcasted_iota(jnp.int32, (tm, tm), 1)
    lstrict = (ci < ri).astype(BF16)
    oh1f = oh1.astype(F32)
    oh2f = oh2.astype(F32)
    c1 = jnp.dot(lstrict, oh1f.astype(BF16), preferred_element_type=F32)
    c2 = jnp.dot(lstrict, oh2f.astype(BF16), preferred_element_type=F32)
    carry = carry_ref[...]
    tot1 = jnp.sum(oh1f, axis=0, keepdims=True)
    tot2 = jnp.sum(oh2f, axis=0, keepdims=True)
    rank1 = jnp.sum(jnp.where(oh1, c1 + carry, 0.0), axis=-1, keepdims=True)
    rank2 = jnp.sum(jnp.where(oh2, c2 + (carry + tot1), 0.0), axis=-1, keepdims=True)
    new_carry = carry + tot1 + tot2
    carry_ref[...] = new_carry
    cnt_ref[...] = new_carry

    cols = (e1.astype(F32), e2.astype(F32), rank1, rank2, gate1, gate2)
    route = jnp.zeros((tm, ROUTE_W), F32)
    for j, col in enumerate(cols):
        route = jnp.where(lane == j, col, route)
    route_ref[...] = route


def _post_mix(o_a, o_b, x2d, w_out_b, ln_g, ln_b, w_router, b_router, *, tm=512):
    n = x2d.shape[0]
    return pl.pallas_call(
        functools.partial(_post_mix_kernel, tm=tm),
        grid=(n // tm,),
        in_specs=[pl.BlockSpec((tm, GLA_WIDTH), lambda i: (i, 0)),
                  pl.BlockSpec((tm, GDN_WIDTH), lambda i: (i, 0)),
                  pl.BlockSpec((tm, D_MODEL), lambda i: (i, 0)),
                  pl.BlockSpec((GLA_WIDTH + GDN_WIDTH, D_MODEL), lambda i: (0, 0)),
                  pl.BlockSpec((1, D_MODEL), lambda i: (0, 0)),
                  pl.BlockSpec((1, D_MODEL), lambda i: (0, 0)),
                  pl.BlockSpec((D_MODEL, 2 * ROUTE_W), lambda i: (0, 0)),
                  pl.BlockSpec((1, ROUTE_W), lambda i: (0, 0))],
        out_specs=[pl.BlockSpec((tm * TOK_ROWS, 128), lambda i: (i, 0)),
                   pl.BlockSpec((tm, ROUTE_W), lambda i: (i, 0)),
                   pl.BlockSpec((1, ROUTE_W), lambda i: (0, 0))],
        out_shape=[jax.ShapeDtypeStruct((n * TOK_ROWS, 128), jnp.uint32),
                   jax.ShapeDtypeStruct((n, ROUTE_W), F32),
                   jax.ShapeDtypeStruct((1, ROUTE_W), F32)],
        scratch_shapes=[pltpu.VMEM((1, ROUTE_W), F32)],
        compiler_params=_cparams("arbitrary"),
        name="post_mix",
    )(o_a, o_b, x2d, w_out_b, ln_g, ln_b, w_router, b_router)


EXPERT_BLOCK = 256


ROW_UNROLL = 8


def _plan_kernel(route_ref, cnt_ref, dest_ref, *, tm, tb):
    counts = cnt_ref[...].astype(jnp.int32)
    padded = ((counts + (tb - 1)) & jnp.int32(-tb)).astype(F32)
    ri = lax.broadcasted_iota(jnp.int32, (ROUTE_W, ROUTE_W), 0)
    ci = lax.broadcasted_iota(jnp.int32, (ROUTE_W, ROUTE_W), 1)
    pstart = jnp.dot(jnp.broadcast_to(padded, (8, ROUTE_W)), (ri < ci).astype(F32),
                     precision=HIGHEST, preferred_element_type=F32)[0:1, :]
    route = route_ref[...]
    lane = lax.broadcasted_iota(jnp.int32, (tm, ROUTE_W), 1).astype(F32)
    out = jnp.zeros((tm, ROUTE_W), F32)
    for k, (ce, cr) in enumerate(((R_E1, R_RANK1), (R_E2, R_RANK2))):
        off = jnp.sum(jnp.where(lane == route[:, ce:ce + 1], pstart, 0.0), axis=-1, keepdims=True)
        out = jnp.where(lane == float(k), off + route[:, cr:cr + 1], out)
    dest_ref[...] = out.T[0:8, :].astype(jnp.int32) * TOK_ROWS


def _plan(route, counts, *, tb, tm=2048):
    n = route.shape[0]
    tm = min(tm, n)
    return pl.pallas_call(
        functools.partial(_plan_kernel, tm=tm, tb=tb),
        grid=(n // tm,),
        in_specs=[pl.BlockSpec((tm, ROUTE_W), lambda i: (i, 0)),
                  pl.BlockSpec((1, ROUTE_W), lambda i: (0, 0))],
        out_specs=pl.BlockSpec((8, tm), lambda i: (0, i)),
        out_shape=jax.ShapeDtypeStruct((8, n), jnp.int32),
        compiler_params=_cparams("parallel"),
        name="plan",
    )(route, counts)


def _dispatch_kernel(d1_ref, d2_ref, xp_ref, xs_init_ref, xs_ref, sem, *, tm):
    del xs_init_ref
    base = pl.program_id(0) * tm

    def body(j, carry):
        for u in range(ROW_UNROLL):
            r = j * ROW_UNROLL + u
            for k, d_ref in enumerate((d1_ref, d2_ref)):
                dst = pl.multiple_of(d_ref[base + r], TOK_ROWS)
                pltpu.make_async_copy(xp_ref.at[pl.ds(r * TOK_ROWS, TOK_ROWS), :],
                                      xs_ref.at[pl.ds(dst, TOK_ROWS), :], sem).start(priority=k)
        return carry
    lax.fori_loop(0, tm // ROW_UNROLL, body, 0)
    for _ in range(TOP_K):
        pltpu.make_async_copy(xp_ref, xs_ref.at[pl.ds(0, tm * TOK_ROWS), :], sem).wait()


def _dispatch(dest1, dest2, x1p, n_rows, *, tm=512):
    n = x1p.shape[0] // TOK_ROWS
    grid_spec = pltpu.PrefetchScalarGridSpec(
        num_scalar_prefetch=2,
        grid=(n // tm,),
        in_specs=[pl.BlockSpec((tm * TOK_ROWS, 128), lambda i, a, b: (i, 0)),
                  pl.BlockSpec(memory_space=pl.ANY)],
        out_specs=pl.BlockSpec(memory_space=pl.ANY),
        scratch_shapes=[pltpu.SemaphoreType.DMA(())],
    )
    return pl.pallas_call(
        functools.partial(_dispatch_kernel, tm=tm),
        grid_spec=grid_spec,
        out_shape=jax.ShapeDtypeStruct((n_rows * TOK_ROWS, 128), jnp.uint32),
        input_output_aliases={3: 0},
        compiler_params=_cparams("arbitrary"),
        name="dispatch",
    )(dest1, dest2, x1p, jnp.zeros((n_rows * TOK_ROWS, 128), jnp.uint32))


W_SPLIT = 4


def _expert_kernel(start_ref, nblk_ref, next_e_ref, first_e_ref, nvalid_ref, wg_hbm, wu_hbm, wd_hbm, xs_hbm, y_hbm,
                   wgs, wus, wds, wgb, wub, wdb, xbuf, ybuf, wsem, xsem, ysem, zsem, *, tb, nb):
    e = pl.program_id(0)
    nvalid = nvalid_ref[0]
    rows = tb * TOK_ROWS

    def weight_copies(ex):
        cps = []
        for j, (src, dst) in enumerate(((wg_hbm, wgs), (wu_hbm, wus), (wd_hbm, wds))):
            slab = src.shape[1] // W_SPLIT
            for q in range(W_SPLIT):
                sl = pl.ds(q * slab, slab)
                cps.append(pltpu.make_async_copy(src.at[ex, sl, :], dst.at[sl, :], wsem.at[j]))
        return cps

    def start_weights(ex):
        for q, cp in enumerate(weight_copies(ex)):
            cp.start(priority=q % 2)

    def x_copy(g, slot):
        return pltpu.make_async_copy(xs_hbm.at[pl.ds(pl.multiple_of(g * rows, rows), rows), :], xbuf.at[slot],
                                     xsem.at[slot])

    def y_copy(g, slot):
        return pltpu.make_async_copy(ybuf.at[slot], y_hbm.at[pl.ds(pl.multiple_of(g * rows, rows), rows), :],
                                     ysem.at[slot])

    @pl.when((e == 0) & (nvalid > 0))
    def _():
        x_copy(0, 0).start()
        start_weights(first_e_ref[0])

    @pl.when(nblk_ref[e] > 0)
    def _():
        for cp in weight_copies(e):
            cp.wait()
        wgb[...] = wgs[...].astype(BF16)
        wub[...] = wus[...].astype(BF16)
        wdb[...] = wds[...].astype(BF16)

        @pl.when(next_e_ref[e] >= 0)
        def _():
            start_weights(next_e_ref[e])

    def block(b, carry):
        g = start_ref[e] + b
        slot = g % 2
        x_copy(g, slot).wait()

        @pl.when(g + 1 < nvalid)
        def _():
            x_copy(g + 1, 1 - slot).start(priority=1)

        lo, hi = _unpack_bf16_pairs(_load_token_rows(xbuf.at[slot], tb))
        lo = lo.astype(BF16)
        hi = hi.astype(BF16)
        hg = jnp.dot(lo, wgb[0:HALF, :], preferred_element_type=F32)
        hg = hg + jnp.dot(hi, wgb[HALF:, :], preferred_element_type=F32)
        hu = jnp.dot(lo, wub[0:HALF, :], preferred_element_type=F32)
        hu = hu + jnp.dot(hi, wub[HALF:, :], preferred_element_type=F32)
        hid = (_silu(hg) * hu).astype(BF16)
        y = _pack_bf16_pairs(jnp.dot(hid, wdb[...], preferred_element_type=F32))

        @pl.when(g >= 2)
        def _():
            y_copy(g - 2, slot).wait()

        _store_token_rows(ybuf.at[slot], y)
        y_copy(g, slot).start()
        return carry

    lax.fori_loop(0, nblk_ref[e], block, 0)

    @pl.when(e == pl.num_programs(0) - 1)
    def _():
        for back in (1, 2):
            @pl.when(nvalid >= back)
            def _():
                y_copy(nvalid - back, (nvalid - back) % 2).wait()

        xbuf[0] = jnp.zeros((rows, 128), jnp.uint32)

        def zero_copy(g):
            return pltpu.make_async_copy(xbuf.at[0], y_hbm.at[pl.ds(pl.multiple_of(g * rows, rows), rows), :], zsem)

        def zstart(g, carry):
            zero_copy(g).start()
            return carry

        def zwait(g, carry):
            zero_copy(g).wait()
            return carry

        lax.fori_loop(nvalid, nb, zstart, 0)
        lax.fori_loop(nvalid, nb, zwait, 0)


def _experts(blk_start, nblk, next_e, first_e, nvalid, xs, w_gate, w_up, w_down, *, nb, tb=EXPERT_BLOCK):
    grid_spec = pltpu.PrefetchScalarGridSpec(
        num_scalar_prefetch=5,
        grid=(N_EXPERTS,),
        in_specs=[pl.BlockSpec(memory_space=pl.ANY)] * 4,
        out_specs=pl.BlockSpec(memory_space=pl.ANY),
        scratch_shapes=[pltpu.VMEM((D_MODEL, D_EXPERT), F32),
                        pltpu.VMEM((D_MODEL, D_EXPERT), F32),
                        pltpu.VMEM((D_EXPERT, D_MODEL), F32),
                        pltpu.VMEM((D_MODEL, D_EXPERT), BF16),
                        pltpu.VMEM((D_MODEL, D_EXPERT), BF16),
                        pltpu.VMEM((D_EXPERT, D_MODEL), BF16),
                        pltpu.VMEM((2, tb * TOK_ROWS, 128), jnp.uint32),
                        pltpu.VMEM((2, tb * TOK_ROWS, 128), jnp.uint32),
                        pltpu.SemaphoreType.DMA((3,)),
                        pltpu.SemaphoreType.DMA((2,)),
                        pltpu.SemaphoreType.DMA((2,)),
                        pltpu.SemaphoreType.DMA(())],
    )
    return pl.pallas_call(
        functools.partial(_expert_kernel, tb=tb, nb=nb),
        grid_spec=grid_spec,
        out_shape=jax.ShapeDtypeStruct((nb * tb * TOK_ROWS, 128), jnp.uint32),
        compiler_params=_cparams("arbitrary"),
        name="experts",
    )(blk_start, nblk, next_e, first_e, nvalid, w_gate, w_up, w_down, xs)


def _combine_kernel(d1_ref, d2_ref, y_hbm, x1_ref, route_ref, g_ref, b_ref, o_ref, ybuf, sem, *, tt):
    i = pl.program_id(0)
    nsteps = pl.num_programs(0)
    slot = i % 2

    def issue(tile, slot_):
        def body(j, carry):
            for u in range(ROW_UNROLL):
                r = j * ROW_UNROLL + u
                for k, d_ref in enumerate((d1_ref, d2_ref)):
                    src = pl.multiple_of(d_ref[tile * tt + r], TOK_ROWS)
                    pltpu.make_async_copy(y_hbm.at[pl.ds(src, TOK_ROWS), :],
                                          ybuf.at[slot_, k, pl.ds(r * TOK_ROWS, TOK_ROWS), :],
                                          sem.at[slot_]).start(priority=k)
            return carry
        lax.fori_loop(0, tt // ROW_UNROLL, body, 0)

    @pl.when(i == 0)
    def _():
        issue(0, 0)

    @pl.when(i + 1 < nsteps)
    def _():
        issue(i + 1, 1 - slot)

    for k in range(TOP_K):
        pltpu.make_async_copy(y_hbm.at[pl.ds(0, tt * TOK_ROWS), :], ybuf.at[slot, k], sem.at[slot]).wait()
    route = route_ref[...]
    g1 = route[:, R_G1:R_G1 + 1]
    g2 = route[:, R_G2:R_G2 + 1]
    lo1, hi1 = _unpack_bf16_pairs(_load_token_rows(ybuf.at[slot, 0], tt))
    lo2, hi2 = _unpack_bf16_pairs(_load_token_rows(ybuf.at[slot, 1], tt))
    x_lo, x_hi = _unpack_bf16_pairs(_load_token_rows(x1_ref, tt))
    h_lo = ALPHA_DN * x_lo + (g1 * lo1 + g2 * lo2)
    h_hi = ALPHA_DN * x_hi + (g1 * hi1 + g2 * hi2)
    mu = (jnp.sum(h_lo, axis=-1, keepdims=True) + jnp.sum(h_hi, axis=-1, keepdims=True)) * (1.0 / D_MODEL)
    c_lo = h_lo - mu
    c_hi = h_hi - mu
    var = (jnp.sum(c_lo * c_lo, axis=-1, keepdims=True) + jnp.sum(c_hi * c_hi, axis=-1, keepdims=True)) * (1.0 / D_MODEL)
    inv = lax.rsqrt(var + LN_EPS)
    o_ref[:, 0:HALF] = c_lo * inv * g_ref[:, 0:HALF] + b_ref[:, 0:HALF]
    o_ref[:, HALF:] = c_hi * inv * g_ref[:, HALF:] + b_ref[:, HALF:]


def _combine(dest1, dest2, ybuf, x1p, route, ln_g, ln_b, *, tt=256):
    n = x1p.shape[0] // TOK_ROWS
    grid_spec = pltpu.PrefetchScalarGridSpec(
        num_scalar_prefetch=2,
        grid=(n // tt,),
        in_specs=[pl.BlockSpec(memory_space=pl.ANY),
                  pl.BlockSpec((tt * TOK_ROWS, 128), lambda i, a, b: (i, 0)),
                  pl.BlockSpec((tt, ROUTE_W), lambda i, a, b: (i, 0)),
                  pl.BlockSpec((1, D_MODEL), lambda i, a, b: (0, 0)),
                  pl.BlockSpec((1, D_MODEL), lambda i, a, b: (0, 0))],
        out_specs=pl.BlockSpec((tt, D_MODEL), lambda i, a, b: (i, 0)),
        scratch_shapes=[pltpu.VMEM((2, TOP_K, tt * TOK_ROWS, 128), jnp.uint32),
                        pltpu.SemaphoreType.DMA((2,))],
    )
    return pl.pallas_call(
        functools.partial(_combine_kernel, tt=tt),
        grid_spec=grid_spec,
        out_shape=jax.ShapeDtypeStruct((n, D_MODEL), F32),
        compiler_params=_cparams("arbitrary"),
        name="combine",
    )(dest1, dest2, ybuf, x1p, route, ln_g, ln_b)


def _block_plan(counts_row, n, tb):
    counts = counts_row[0, :N_EXPERTS].astype(jnp.int32)
    nblk = (counts + tb - 1) // tb
    ends = jnp.cumsum(nblk)
    ids = jnp.arange(N_EXPERTS, dtype=jnp.int32)
    later = lax.cummin(jnp.where(counts > 0, ids, N_EXPERTS), axis=0, reverse=True)
    nxt = jnp.concatenate([later[1:], jnp.full((1,), N_EXPERTS, jnp.int32)])
    nxt = jnp.where(nxt < N_EXPERTS, nxt, -1)
    first_e = jnp.minimum(later[:1], N_EXPERTS - 1)
    return ends - nblk, nblk, nxt, first_e, ends[-1:]


def kernel(x, w_in, w_gk_up, b_gk, conv_w, a_log, dt_bias, gla_norm_g, gdn_norm_g, w_out, ln1_g, ln1_b, w_router_group, b_router_group, w_router_expert, b_router_expert, w_gate, w_up, w_down, ln2_g, ln2_b):
    bsz, seq, d = x.shape
    n = bsz * seq
    x2d = x.reshape(n, d)
    w_main, w_aux = _split_w_in(w_in[0])
    main, aux = _in_proj(x2d, w_main, w_aux, conv_w[0].astype(F32), seq=seq)
    wgk = jnp.zeros((AUX_W, 256), F32).at[:GLA_GATE_RANK].set(w_gk_up[0])
    o_a = _gla(main, aux, wgk, b_gk[0][None], gla_norm_g[0][None], bsz=bsz, seq=seq)
    o_b = _gdn(main, aux, _pad_aux_row(a_log[0], AUX_AB), _pad_aux_row(dt_bias[0], AUX_AB),
               gdn_norm_g[0][None], bsz=bsz, seq=seq)
    rpad = jnp.zeros((d, ROUTE_W - N_EXPERTS - N_GROUPS), F32)
    w_router = jnp.concatenate([w_router_expert[0], w_router_group[0], rpad], axis=1)
    w_router_hi = w_router.astype(BF16)
    w_router = jnp.concatenate([w_router_hi, (w_router - w_router_hi.astype(F32)).astype(BF16)], axis=1)
    b_router = jnp.concatenate([b_router_expert[0], b_router_group[0], rpad[0]])[None]
    x1p, route, counts = _post_mix(o_a, o_b, x2d, w_out[0].astype(BF16), ln1_g[0][None], ln1_b[0][None],
                                       w_router, b_router)
    dest = _plan(route, counts, tb=EXPERT_BLOCK)
    blk_start, nblk, next_e, first_e, nvalid = _block_plan(counts, n, EXPERT_BLOCK)
    nb = (n * TOP_K) // EXPERT_BLOCK + N_EXPERTS
    xs = _dispatch(dest[0], dest[1], x1p, nb * EXPERT_BLOCK)
    ybuf = _experts(blk_start, nblk, next_e, first_e, nvalid, xs, w_gate[0], w_up[0], w_down[0], nb=nb)
    out = _combine(dest[0], dest[1], ybuf, x1p, route, ln2_g[0][None], ln2_b[0][None])
    return out.reshape(bsz, seq, d)
```

```python
import functools

import jax
import jax.numpy as jnp
import numpy as np
from jax import lax
from jax.experimental import pallas as pl
from jax.experimental.pallas import tpu as pltpu

F32 = jnp.float32
BF16 = jnp.bfloat16
HIGHEST = lax.Precision.HIGHEST

D_MODEL = 1024
DEPTH = 1
GLA_HEADS = 4
GLA_DK = 64
GLA_DV = 128
GLA_WIDTH = GLA_HEADS * GLA_DV
GLA_GATE_RANK = 16
GLA_GATE_NORM = 16.0
GDN_HEADS = 4
GDN_DK = 128
GDN_DV = 128
GDN_WIDTH = GDN_HEADS * GDN_DV
CONV_W = 4
CHUNK = 64
N_GROUPS = 8
EXPERTS_PER_GROUP = 8
N_EXPERTS = N_GROUPS * EXPERTS_PER_GROUP
TOP_K = 2
D_EXPERT = 512
LN_EPS = 1e-5
RMS_EPS = 1e-6
ALPHA_DN = (2.0 * DEPTH) ** 0.25

MAIN_W = 2 * GLA_HEADS * GLA_DK + 2 * GLA_WIDTH + 4 * GDN_WIDTH
AUX_W = 128
AUX_LRA = 0
AUX_BB = GLA_GATE_RANK
AUX_AB = GLA_GATE_RANK + GDN_HEADS
SUB = 16
VMEM_LIMIT = 56 * 1024 * 1024


def _cparams(*sem):
    return pltpu.CompilerParams(dimension_semantics=sem, vmem_limit_bytes=VMEM_LIMIT)


def _sigmoid(x):
    return 1.0 / (1.0 + jnp.exp(-x))


def _silu(x):
    return x * _sigmoid(x)


def _log_sigmoid(x):
    return jnp.minimum(x, 0.0) - jnp.log(1.0 + jnp.exp(-jnp.abs(x)))


def _softplus(x):
    return jnp.maximum(x, 0.0) + jnp.log(1.0 + jnp.exp(-jnp.abs(x)))


CONV_PAD = 8
GDN_HW = GDN_HEADS * GDN_DK
GDN_QKV_BLOCK0 = (2 * GLA_HEADS * GLA_DK + 2 * GLA_WIDTH) // GDN_HW


def _in_proj_kernel(x_ref, wm_ref, wa_ref, cw_ref, main_ref, aux_ref, stage_ref, *, tm, tiles_per_seq):
    xb = x_ref[...].astype(BF16)

    @pl.when((pl.program_id(0) % tiles_per_seq) == 0)
    def _():
        stage_ref[:, 0:CONV_PAD, :] = jnp.zeros((3 * GDN_HEADS, CONV_PAD, GDN_DK), F32)

    for j in range(MAIN_W // GDN_HW):
        sl = slice(j * GDN_HW, (j + 1) * GDN_HW)
        r = jnp.dot(xb, wm_ref[:, sl], preferred_element_type=F32)
        b = j - GDN_QKV_BLOCK0
        if not 0 <= b < 3:
            main_ref[:, sl] = r.astype(BF16)
            continue
        for h in range(GDN_HEADS):
            st = stage_ref.at[b * GDN_HEADS + h]
            cols = slice(b * GDN_HW + h * GDN_DK, b * GDN_HW + (h + 1) * GDN_DK)
            st[CONV_PAD:, :] = r[:, h * GDN_DK:(h + 1) * GDN_DK]
            acc = jnp.zeros((tm, GDN_DK), F32)
            for i in range(CONV_W):
                lo = CONV_PAD - (CONV_W - 1) + i
                acc = acc + st[lo:lo + tm, :] * cw_ref[i:i + 1, cols]
            st[0:CONV_PAD, :] = st[tm:tm + CONV_PAD, :]
            yh = _silu(acc)
            if b < 2:
                yh = yh * lax.rsqrt(jnp.sum(yh * yh, axis=-1, keepdims=True) + RMS_EPS)
            if b == 0:
                yh = yh * (GDN_DK ** -0.5)
            main_ref[:, j * GDN_HW + h * GDN_DK:j * GDN_HW + (h + 1) * GDN_DK] = yh.astype(BF16)
    aux_ref[...] = jnp.dot(xb, wa_ref[...], preferred_element_type=F32)


def _in_proj(x2d, w_main, w_aux, conv_w, *, seq, tm=512):
    n = x2d.shape[0]
    return pl.pallas_call(
        functools.partial(_in_proj_kernel, tm=tm, tiles_per_seq=seq // tm),
        grid=(n // tm,),
        in_specs=[pl.BlockSpec((tm, D_MODEL), lambda i: (i, 0)),
                  pl.BlockSpec((D_MODEL, MAIN_W), lambda i: (0, 0)),
                  pl.BlockSpec((D_MODEL, AUX_W), lambda i: (0, 0)),
                  pl.BlockSpec((CONV_W, 3 * GDN_HW), lambda i: (0, 0))],
        out_specs=[pl.BlockSpec((tm, MAIN_W), lambda i: (i, 0)),
                   pl.BlockSpec((tm, AUX_W), lambda i: (i, 0))],
        out_shape=[jax.ShapeDtypeStruct((n, MAIN_W), BF16),
                   jax.ShapeDtypeStruct((n, AUX_W), F32)],
        scratch_shapes=[pltpu.VMEM((3 * GDN_HEADS, tm + CONV_PAD, GDN_DK), F32)],
        compiler_params=_cparams("arbitrary"),
        name="in_proj",
    )(x2d, w_main, w_aux, conv_w)


def _split_w_in(w_in):
    sizes = (GLA_HEADS * GLA_DK, GLA_HEADS * GLA_DK, GLA_WIDTH, GLA_WIDTH, GLA_GATE_RANK,
             GDN_HEADS * GDN_DK, GDN_HEADS * GDN_DK, GDN_WIDTH, GDN_WIDTH, GDN_HEADS, GDN_HEADS)
    offs = np.cumsum((0,) + sizes)
    seg = [w_in[:, offs[i]:offs[i + 1]] for i in range(len(sizes))]
    qa, ka, va, ra, lra, qb, kb, vb, zb, bb, ab = seg
    w_main = jnp.concatenate([qa, ka, va, ra, qb, kb, vb, zb], axis=1).astype(BF16)
    pad = jnp.zeros((w_in.shape[0], AUX_W - GLA_GATE_RANK - 2 * GDN_HEADS), w_in.dtype)
    w_aux = jnp.concatenate([lra, bb, ab, pad], axis=1).astype(BF16)
    return w_main, w_aux


_NT = (((1,), (1,)), ((), ()))
_TN = (((0,), (0,)), ((), ()))


def _gla_kernel(q_ref, k_ref, v_ref, r_ref, aux_ref, wgk_ref, bgk_ref, g_ref, o_ref, st_ref, *, ts):
    c = CHUNK
    nsub = c // SUB

    @pl.when(pl.program_id(1) == 0)
    def _():
        st_ref[...] = jnp.zeros_like(st_ref)

    pre = jnp.dot(aux_ref[...], wgk_ref[...], precision=HIGHEST, preferred_element_type=F32) + bgk_ref[...]
    gk_all = _log_sigmoid(pre) * (1.0 / GLA_GATE_NORM)

    ri = lax.broadcasted_iota(jnp.int32, (c, c), 0)
    ci = lax.broadcasted_iota(jnp.int32, (c, c), 1)
    tri = (ci <= ri).astype(F32)
    row128 = lax.broadcasted_iota(jnp.int32, (c, 128), 0)
    lane128 = lax.broadcasted_iota(jnp.int32, (1, 128), 1)
    lane_masks = [(lane128 // GLA_DK) == hh for hh in range(2)]
    sub_valid = [row128 < SUB * (i + 1) for i in range(nsub)]
    ar = lax.broadcasted_iota(jnp.int32, (c, nsub * c), 0)
    ac = lax.broadcasted_iota(jnp.int32, (c, nsub * c), 1)
    amask = ((ac // c) == (ar // SUB)) & ((ac % c) <= ar)
    scale = GLA_DK ** -0.5
    gnorm = g_ref[...]

    nch = ts // c
    rows = [slice(ch * c, (ch + 1) * c) for ch in range(nch)]
    b_all = [jnp.dot(tri, gk_all[r], precision=HIGHEST, preferred_element_type=F32) for r in rows]

    cp = [(ch, p) for ch in range(nch) for p in range(2)]
    qt, qd, kd, kstack, dlast = {}, {}, {}, {}, {}
    for ch, p in cp:
        lanes = slice(128 * p, 128 * (p + 1))
        bp = b_all[ch][:, lanes]
        qp = q_ref[rows[ch], lanes].astype(F32) * scale
        kp = k_ref[rows[ch], lanes].astype(F32)
        c_row = jnp.concatenate(
            [jnp.broadcast_to(bp[SUB * i:SUB * i + 1, :], (SUB, 128)) for i in range(nsub)], axis=0)
        qt[ch, p] = (qp * jnp.exp(bp - c_row)).astype(BF16)
        qd[ch, p] = qp * jnp.exp(bp)
        b_last = bp[c - 1:c, :]
        dlast[ch, p] = jnp.exp(b_last)
        kd[ch, p] = kp * jnp.exp(b_last - bp)
        kts = []
        for i in range(nsub):
            e = jnp.exp(jnp.where(sub_valid[i], bp[SUB * i:SUB * i + 1, :] - bp, 0.0))
            kts.append(jnp.where(sub_valid[i], kp * e, 0.0))
        kstack[ch, p] = jnp.concatenate(kts, axis=0)

    cph = [(ch, p, hh) for ch, p in cp for hh in range(2)]
    vh = {(ch, p, hh): v_ref[rows[ch], 128 * (2 * p + hh):128 * (2 * p + hh + 1)] for ch, p, hh in cph}
    r_mat = {(ch, p, hh): lax.dot_general(qt[ch, p], jnp.where(lane_masks[hh], kstack[ch, p], 0.0).astype(BF16),
                                          _NT, preferred_element_type=F32) for ch, p, hh in cph}
    o_intra = {k: jnp.dot(jnp.where(amask, r_mat[k], 0.0).astype(BF16), jnp.concatenate([vh[k]] * nsub, axis=0),
                          preferred_element_type=F32) for k in cph}
    upd = {(ch, p, hh): lax.dot_general(vh[ch, p, hh], jnp.where(lane_masks[hh], kd[ch, p], 0.0).astype(BF16), _TN,
                                        preferred_element_type=F32) for ch, p, hh in cph}

    st = [st_ref[p] for p in range(2)]
    for ch in range(nch):
        st_b = [s.astype(BF16) for s in st]
        o_pair = [lax.dot_general(
            jnp.concatenate([jnp.where(lane_masks[hh], qd[ch, p], 0.0).astype(BF16) for hh in range(2)], axis=0),
            st_b[p], _NT, preferred_element_type=F32) for p in range(2)]
        o_inter = {(p, hh): o_pair[p][hh * c:(hh + 1) * c] for p in range(2) for hh in range(2)}
        st = [st[p] * dlast[ch, p] + upd[ch, p, 0] + upd[ch, p, 1] for p in range(2)]
        for p in range(2):
            for hh in range(2):
                h = 2 * p + hh
                o = o_intra[ch, p, hh] + o_inter[p, hh]
                o = o * lax.rsqrt(jnp.mean(o * o, axis=-1, keepdims=True) + RMS_EPS) * gnorm
                gate = _silu(r_ref[rows[ch], 128 * h:128 * (h + 1)].astype(F32))
                o_ref[rows[ch], 128 * h:128 * (h + 1)] = (o * gate).astype(BF16)
    for p in range(2):
        st_ref[p] = st[p]


def _gla(main, aux, wgk_pad, b_gk, gla_norm_g, *, bsz, seq, ts=256):
    nt = seq // ts
    row = lambda b, t: b * nt + t
    return pl.pallas_call(
        functools.partial(_gla_kernel, ts=ts),
        grid=(bsz, nt),
        in_specs=[pl.BlockSpec((ts, 256), lambda b, t: (row(b, t), 0)),
                  pl.BlockSpec((ts, 256), lambda b, t: (row(b, t), 1)),
                  pl.BlockSpec((ts, 512), lambda b, t: (row(b, t), 1)),
                  pl.BlockSpec((ts, 512), lambda b, t: (row(b, t), 2)),
                  pl.BlockSpec((ts, AUX_W), lambda b, t: (row(b, t), 0)),
                  pl.BlockSpec((AUX_W, 256), lambda b, t: (0, 0)),
                  pl.BlockSpec((1, 256), lambda b, t: (0, 0)),
                  pl.BlockSpec((1, 128), lambda b, t: (0, 0))],
        out_specs=pl.BlockSpec((ts, GLA_WIDTH), lambda b, t: (row(b, t), 0)),
        out_shape=jax.ShapeDtypeStruct((bsz * seq, GLA_WIDTH), BF16),
        scratch_shapes=[pltpu.VMEM((2, 128, 128), F32)],
        compiler_params=_cparams("parallel", "arbitrary"),
        name="gla",
    )(main, main, main, main, aux, wgk_pad, b_gk, gla_norm_g)


def _gdn_kernel(q_ref, k_ref, v_ref, z_ref, aux_ref, alog_ref, dtb_ref, g_ref, o_ref, st_ref, *, ts):
    c = CHUNK
    nch = ts // c

    @pl.when(pl.program_id(1) == 0)
    def _():
        st_ref[...] = jnp.zeros_like(st_ref)

    aux = aux_ref[...]
    beta_full = _sigmoid(aux)
    g_full = -jnp.exp(alog_ref[...]) * _softplus(aux + dtb_ref[...])
    ri = lax.broadcasted_iota(jnp.int32, (ts, ts), 0)
    ci = lax.broadcasted_iota(jnp.int32, (ts, ts), 1)
    cum = (((ri // c) == (ci // c)) & (ci <= ri)).astype(F32)
    gam_full = jnp.dot(cum, g_full, precision=HIGHEST, preferred_element_type=F32)
    gam_t = gam_full.T
    gnorm = g_ref[...]

    row = lax.broadcasted_iota(jnp.int32, (c, 2 * c), 0)
    lane = lax.broadcasted_iota(jnp.int32, (c, 2 * c), 1)
    left = lane < c
    col = lane % c
    causal = col <= row
    strict = col < row
    same32 = (row // 32) == (col // 32)
    same16 = (row // 16) == (col // 16)
    lvl0 = same16 & strict
    lvl1 = same32 & jnp.logical_not(same16) & strict
    lvl2 = jnp.logical_not(same32) & strict

    def pack_rows(x2):
        return jnp.where(left, x2[0:c], x2[c:2 * c])

    def pack_cols(x1):
        return jnp.where(left, x1[0:c], x1[c:2 * c])

    def halves(x):
        return jnp.where(left, x, 0.0).astype(BF16), jnp.where(left, 0.0, x).astype(BF16)

    def mmp(x, y):
        return jnp.dot(x.astype(BF16), jnp.concatenate(halves(y), axis=0), preferred_element_type=F32)

    heads = range(GDN_HEADS)
    npair = nch // 2
    items = [(h, pr) for h in heads for pr in range(npair)]
    prow = [slice(2 * c * pr, 2 * c * (pr + 1)) for pr in range(npair)]

    qn, kn, beta, gam, gam_r, kb, rhs, rhs_b, qd, kd = ([None] * GDN_HEADS for _ in range(10))
    for h in heads:
        hl = slice(128 * h, 128 * (h + 1))
        kb[h] = k_ref[:, hl]
        qn[h] = q_ref[:, hl].astype(F32)
        kn[h] = kb[h].astype(F32)
        vh = v_ref[:, hl].astype(F32)
        beta[h] = beta_full[:, AUX_BB + h:AUX_BB + h + 1]
        gam[h] = gam_full[:, AUX_AB + h:AUX_AB + h + 1]
        gam_r[h] = gam_t[AUX_AB + h:AUX_AB + h + 1, :]
        egam = jnp.exp(gam[h])
        rhs[h] = jnp.concatenate([beta[h] * vh, beta[h] * egam * kn[h]], axis=1)
        rhs_b[h] = rhs[h].astype(BF16)
        qd[h] = (qn[h] * egam).astype(BF16)
        gl_rows = jnp.concatenate(
            [jnp.broadcast_to(gam[h][c * (j + 1) - 1:c * (j + 1), :], (c, 1)) for j in range(nch)], axis=0)
        kd[h] = (kn[h] * jnp.exp(gl_rows - gam[h])).astype(BF16)

    kk = [pack_rows(lax.dot_general(kb[h][prow[pr]], kb[h][prow[pr]], _NT, preferred_element_type=F32))
          for h, pr in items]
    qk = [pack_rows(lax.dot_general(qn[h][prow[pr]].astype(BF16), kb[h][prow[pr]], _NT,
                                    preferred_element_type=F32)) for h, pr in items]
    dec = [jnp.exp(jnp.where(causal, pack_cols(gam[h][prow[pr]]) - gam_r[h][:, prow[pr]], 0.0)) for h, pr in items]
    a = [pack_cols(beta[h][prow[pr]]) * kk[i] * dec[i] for i, (h, pr) in enumerate(items)]
    attn = [halves(jnp.where(causal, qk[i] * dec[i], 0.0)) for i in range(len(items))]

    x1 = [jnp.where(lvl0, -ai, 0.0) for ai in a]
    x2 = [mmp(x, x) for x in x1]
    y = [mmp(x, xx) for x, xx in zip(x1, x2)]
    p = [x + xx + yy for x, xx, yy in zip(x1, x2, y)]
    xk = x2
    for _ in range(2):
        xk = [mmp(x, x) for x in xk]
        y = [mmp(pp, x) for pp, x in zip(p, xk)]
        p = [pp + x + yy for pp, x, yy in zip(p, xk, y)]
    for lvl in (lvl1, lvl2):
        low = [jnp.where(lvl, ai, 0.0) for ai in a]
        y = [mmp(pp, lo) for pp, lo in zip(p, low)]
        zz = [lo + yy for lo, yy in zip(low, y)]
        y = [mmp(z, pp) for z, pp in zip(zz, p)]
        p = [pp - z - yy for pp, z, yy in zip(p, zz, y)]

    us = [[None] * nch for _ in heads]
    ws = [[None] * nch for _ in heads]
    for i, (h, pr) in enumerate(items):
        for half, p_half in enumerate(halves(p[i])):
            j = 2 * pr + half
            uw = rhs[h][c * j:c * (j + 1)] + jnp.dot(p_half, rhs_b[h][prow[pr]], preferred_element_type=F32)
            us[h][j] = uw[:, :GDN_DV]
            ws[h][j] = uw[:, GDN_DV:].astype(BF16)

    s = [st_ref[h] for h in heads]
    vprev = [None] * GDN_HEADS
    for j in range(nch):
        rows = slice(c * j, c * (j + 1))
        sb = [s[h].astype(BF16) for h in heads]
        ws_qd = [jnp.dot(jnp.concatenate([ws[h][j], qd[h][rows]], axis=0), sb[h], preferred_element_type=F32)
                 for h in heads]
        vn16 = [(us[h][j] - ws_qd[h][0:c]).astype(BF16) for h in heads]
        o_inter = [ws_qd[h][c:2 * c] for h in heads]
        for h in heads:
            vpair = jnp.concatenate([vn16[h], jnp.zeros_like(vn16[h])] if j % 2 == 0 else [vprev[h], vn16[h]], axis=0)
            o = o_inter[h] + jnp.dot(attn[h * npair + j // 2][j % 2], vpair, preferred_element_type=F32)
            s[h] = jnp.exp(gam[h][c * (j + 1) - 1:c * (j + 1), :]) * s[h] + lax.dot_general(
                kd[h][rows], vn16[h], _TN, preferred_element_type=F32)
            o = o * lax.rsqrt(jnp.mean(o * o, axis=-1, keepdims=True) + RMS_EPS) * gnorm
            gate = _silu(z_ref[rows, 128 * h:128 * (h + 1)].astype(F32))
            o_ref[rows, 128 * h:128 * (h + 1)] = (o * gate).astype(BF16)
        vprev = vn16
    for h in heads:
        st_ref[h] = s[h]


def _gdn(main, aux, alog_pad, dtb_pad, gdn_norm_g, *, bsz, seq, ts=256):
    nt = seq // ts
    row = lambda b, t: b * nt + t
    hw = GDN_HEADS * GDN_DK
    return pl.pallas_call(
        functools.partial(_gdn_kernel, ts=ts),
        grid=(bsz, nt),
        in_specs=[pl.BlockSpec((ts, hw), lambda b, t: (row(b, t), 3)),
                  pl.BlockSpec((ts, hw), lambda b, t: (row(b, t), 4)),
                  pl.BlockSpec((ts, hw), lambda b, t: (row(b, t), 5)),
                  pl.BlockSpec((ts, hw), lambda b, t: (row(b, t), 6)),
                  pl.BlockSpec((ts, AUX_W), lambda b, t: (row(b, t), 0)),
                  pl.BlockSpec((1, AUX_W), lambda b, t: (0, 0)),
                  pl.BlockSpec((1, AUX_W), lambda b, t: (0, 0)),
                  pl.BlockSpec((1, GDN_DV), lambda b, t: (0, 0))],
        out_specs=pl.BlockSpec((ts, GDN_WIDTH), lambda b, t: (row(b, t), 0)),
        out_shape=jax.ShapeDtypeStruct((bsz * seq, GDN_WIDTH), BF16),
        scratch_shapes=[pltpu.VMEM((GDN_HEADS, GDN_DK, GDN_DV), F32)],
        compiler_params=_cparams("parallel", "arbitrary"),
        name="gdn",
    )(main, main, main, main, aux, alog_pad, dtb_pad, gdn_norm_g)


def _pad_aux_row(v, offset):
    return jnp.zeros((1, AUX_W), F32).at[0, offset:offset + v.shape[0]].set(v.astype(F32))


ROUTE_W = 128
R_E1, R_E2, R_RANK1, R_RANK2, R_G1, R_G2 = range(6)
ROUTER_GROUP_COL = N_EXPERTS


def _layer_norm(h, g, b):
    mu = jnp.mean(h, axis=-1, keepdims=True)
    hc = h - mu
    var = jnp.mean(hc * hc, axis=-1, keepdims=True)
    return hc * lax.rsqrt(var + LN_EPS) * g + b


def _pack_bf16_pairs(x):
    w = x.shape[1] // 2
    lo = lax.bitcast_convert_type(x[:, :w].astype(BF16).astype(F32), jnp.uint32)
    hi = lax.bitcast_convert_type(x[:, w:].astype(BF16).astype(F32), jnp.uint32)
    return (lo >> 16) | (hi & jnp.uint32(0xFFFF0000))


def _unpack_bf16_pairs(p):
    lo = lax.bitcast_convert_type(p << 16, F32)
    hi = lax.bitcast_convert_type(p & jnp.uint32(0xFFFF0000), F32)
    return lo, hi


HALF = D_MODEL // 2
TOK_ROWS = HALF // 128


def _store_token_rows(ref, packed):
    m = packed.shape[0]
    for c in range(TOK_ROWS):
        ref[pl.ds(c, m, stride=TOK_ROWS), :] = packed[:, 128 * c:128 * (c + 1)]


def _load_token_rows(ref, m):
    return jnp.concatenate([ref[pl.ds(c, m, stride=TOK_ROWS), :] for c in range(TOK_ROWS)], axis=1)


def _post_mix_kernel(oa_ref, ob_ref, x_ref, wo_ref, g_ref, b_ref, wr_ref, br_ref,
                     x1p_ref, route_ref, cnt_ref, carry_ref, *, tm):
    @pl.when(pl.program_id(0) == 0)
    def _():
        carry_ref[...] = jnp.zeros_like(carry_ref)

    y = jnp.dot(oa_ref[...], wo_ref[0:GLA_WIDTH, :], preferred_element_type=F32)
    y = y + jnp.dot(ob_ref[...], wo_ref[GLA_WIDTH:, :], preferred_element_type=F32)
    x1 = _layer_norm(ALPHA_DN * x_ref[...] + y, g_ref[...], b_ref[...])
    _store_token_rows(x1p_ref, _pack_bf16_pairs(x1))

    xh = x1.astype(BF16)
    xl = (x1 - xh.astype(F32)).astype(BF16)
    hw = jnp.dot(xh, wr_ref[...], preferred_element_type=F32)
    logits = (hw[:, :ROUTE_W] + hw[:, ROUTE_W:]
              + jnp.dot(xl, wr_ref[:, :ROUTE_W], preferred_element_type=F32) + br_ref[...])
    lane = lax.broadcasted_iota(jnp.int32, (tm, ROUTE_W), 1)
    big = jnp.int32(1 << 20)
    neg = jnp.float32(-jnp.inf)

    def first_argmax(vals):
        m = jnp.max(vals, axis=-1, keepdims=True)
        idx = jnp.min(jnp.where(vals == m, lane, big), axis=-1, keepdims=True)
        return m, idx

    is_group = (lane >= ROUTER_GROUP_COL) & (lane < ROUTER_GROUP_COL + N_GROUPS)
    gl = jnp.where(is_group, logits, neg)
    gmax, gidx = first_argmax(gl)
    p_group = 1.0 / jnp.sum(jnp.exp(gl - gmax), axis=-1, keepdims=True)
    g_sel = gidx - ROUTER_GROUP_COL
    el = jnp.where((lane // EXPERTS_PER_GROUP) == g_sel, logits, neg)
    t1, e1 = first_argmax(el)
    t2, e2 = first_argmax(jnp.where(lane == e1, neg, el))
    ex = jnp.exp(t2 - t1)
    w1 = 1.0 / (1.0 + ex)
    gate1 = p_group * w1
    gate2 = p_group * (ex * w1)

    oh1 = lane == e1
    oh2 = lane == e2
    ri = lax.broadcasted_iota(jnp.int32, (tm, tm), 0)
    ci = lax.broadcasted_iota(jnp.int32, (tm, tm), 1)
    lstrict = (ci < ri).astype(BF16)
    oh1f = oh1.astype(F32)
    oh2f = oh2.astype(F32)
    c1 = jnp.dot(lstrict, oh1f.astype(BF16), preferred_element_type=F32)
    c2 = jnp.dot(lstrict, oh2f.astype(BF16), preferred_element_type=F32)
    carry = carry_ref[...]
    tot1 = jnp.sum(oh1f, axis=0, keepdims=True)
    tot2 = jnp.sum(oh2f, axis=0, keepdims=True)
    rank1 = jnp.sum(jnp.where(oh1, c1 + carry, 0.0), axis=-1, keepdims=True)
    rank2 = jnp.sum(jnp.where(oh2, c2 + (carry + tot1), 0.0), axis=-1, keepdims=True)
    new_carry = carry + tot1 + tot2
    carry_ref[...] = new_carry
    cnt_ref[...] = new_carry

    cols = (e1.astype(F32), e2.astype(F32), rank1, rank2, gate1, gate2)
    route = jnp.zeros((tm, ROUTE_W), F32)
    for j, col in enumerate(cols):
        route = jnp.where(lane == j, col, route)
    route_ref[...] = route


def _post_mix(o_a, o_b, x2d, w_out_b, ln_g, ln_b, w_router, b_router, *, tm=512):
    n = x2d.shape[0]
    return pl.pallas_call(
        functools.partial(_post_mix_kernel, tm=tm),
        grid=(n // tm,),
        in_specs=[pl.BlockSpec((tm, GLA_WIDTH), lambda i: (i, 0)),
                  pl.BlockSpec((tm, GDN_WIDTH), lambda i: (i, 0)),
                  pl.BlockSpec((tm, D_MODEL), lambda i: (i, 0)),
                  pl.BlockSpec((GLA_WIDTH + GDN_WIDTH, D_MODEL), lambda i: (0, 0)),
                  pl.BlockSpec((1, D_MODEL), lambda i: (0, 0)),
                  pl.BlockSpec((1, D_MODEL), lambda i: (0, 0)),
                  pl.BlockSpec((D_MODEL, 2 * ROUTE_W), lambda i: (0, 0)),
                  pl.BlockSpec((1, ROUTE_W), lambda i: (0, 0))],
        out_specs=[pl.BlockSpec((tm * TOK_ROWS, 128), lambda i: (i, 0)),
                   pl.BlockSpec((tm, ROUTE_W), lambda i: (i, 0)),
                   pl.BlockSpec((1, ROUTE_W), lambda i: (0, 0))],
        out_shape=[jax.ShapeDtypeStruct((n * TOK_ROWS, 128), jnp.uint32),
                   jax.ShapeDtypeStruct((n, ROUTE_W), F32),
                   jax.ShapeDtypeStruct((1, ROUTE_W), F32)],
        scratch_shapes=[pltpu.VMEM((1, ROUTE_W), F32)],
        compiler_params=_cparams("arbitrary"),
        name="post_mix",
    )(o_a, o_b, x2d, w_out_b, ln_g, ln_b, w_router, b_router)


EXPERT_BLOCK = 256


ROW_UNROLL = 8


def _plan_kernel(route_ref, cnt_ref, dest_ref, *, tm, tb):
    counts = cnt_ref[...].astype(jnp.int32)
    padded = ((counts + (tb - 1)) & jnp.int32(-tb)).astype(F32)
    ri = lax.broadcasted_iota(jnp.int32, (ROUTE_W, ROUTE_W), 0)
    ci = lax.broadcasted_iota(jnp.int32, (ROUTE_W, ROUTE_W), 1)
    pstart = jnp.dot(jnp.broadcast_to(padded, (8, ROUTE_W)), (ri < ci).astype(F32),
                     precision=HIGHEST, preferred_element_type=F32)[0:1, :]
    route = route_ref[...]
    lane = lax.broadcasted_iota(jnp.int32, (tm, ROUTE_W), 1).astype(F32)
    out = jnp.zeros((tm, ROUTE_W), F32)
    for k, (ce, cr) in enumerate(((R_E1, R_RANK1), (R_E2, R_RANK2))):
        off = jnp.sum(jnp.where(lane == route[:, ce:ce + 1], pstart, 0.0), axis=-1, keepdims=True)
        out = jnp.where(lane == float(k), off + route[:, cr:cr + 1], out)
    dest_ref[...] = out.T[0:8, :].astype(jnp.int32) * TOK_ROWS


def _plan(route, counts, *, tb, tm=2048):
    n = route.shape[0]
    tm = min(tm, n)
    return pl.pallas_call(
        functools.partial(_plan_kernel, tm=tm, tb=tb),
        grid=(n // tm,),
        in_specs=[pl.BlockSpec((tm, ROUTE_W), lambda i: (i, 0)),
                  pl.BlockSpec((1, ROUTE_W), lambda i: (0, 0))],
        out_specs=pl.BlockSpec((8, tm), lambda i: (0, i)),
        out_shape=jax.ShapeDtypeStruct((8, n), jnp.int32),
        compiler_params=_cparams("parallel"),
        name="plan",
    )(route, counts)


def _dispatch_kernel(d1_ref, d2_ref, xp_ref, xs_init_ref, xs_ref, sem, *, tm):
    del xs_init_ref
    base = pl.program_id(0) * tm

    def body(j, carry):
        for u in range(ROW_UNROLL):
            r = j * ROW_UNROLL + u
            for k, d_ref in enumerate((d1_ref, d2_ref)):
                dst = pl.multiple_of(d_ref[base + r], TOK_ROWS)
                pltpu.make_async_copy(xp_ref.at[pl.ds(r * TOK_ROWS, TOK_ROWS), :],
                                      xs_ref.at[pl.ds(dst, TOK_ROWS), :], sem).start(priority=k)
        return carry
    lax.fori_loop(0, tm // ROW_UNROLL, body, 0)
    for _ in range(TOP_K):
        pltpu.make_async_copy(xp_ref, xs_ref.at[pl.ds(0, tm * TOK_ROWS), :], sem).wait()


def _dispatch(dest1, dest2, x1p, n_rows, *, tm=512):
    n = x1p.shape[0] // TOK_ROWS
    grid_spec = pltpu.PrefetchScalarGridSpec(
        num_scalar_prefetch=2,
        grid=(n // tm,),
        in_specs=[pl.BlockSpec((tm * TOK_ROWS, 128), lambda i, a, b: (i, 0)),
                  pl.BlockSpec(memory_space=pl.ANY)],
        out_specs=pl.BlockSpec(memory_space=pl.ANY),
        scratch_shapes=[pltpu.SemaphoreType.DMA(())],
    )
    return pl.pallas_call(
        functools.partial(_dispatch_kernel, tm=tm),
        grid_spec=grid_spec,
        out_shape=jax.ShapeDtypeStruct((n_rows * TOK_ROWS, 128), jnp.uint32),
        input_output_aliases={3: 0},
        compiler_params=_cparams("arbitrary"),
        name="dispatch",
    )(dest1, dest2, x1p, jnp.zeros((n_rows * TOK_ROWS, 128), jnp.uint32))


def _expert_kernel(blk_e_ref, first_ref, next_e_ref, nvalid_ref, xs_ref, wg_hbm, wu_hbm, wd_hbm, y_ref,
                   wgs, wus, wds, wgb, wub, wdb, sem, *, tb):
    i = pl.program_id(0)

    def weight_copies(e):
        return [pltpu.make_async_copy(src.at[e], dst, sem.at[j])
                for j, (src, dst) in enumerate(((wg_hbm, wgs), (wu_hbm, wus), (wd_hbm, wds)))]

    @pl.when((i == 0) & (nvalid_ref[0] > 0))
    def _():
        for cp in weight_copies(blk_e_ref[0]):
            cp.start()

    @pl.when(i < nvalid_ref[0])
    def _():
        @pl.when(first_ref[i] == 1)
        def _():
            for cp in weight_copies(blk_e_ref[i]):
                cp.wait()
            wgb[...] = wgs[...].astype(BF16)
            wub[...] = wus[...].astype(BF16)
            wdb[...] = wds[...].astype(BF16)

            @pl.when(next_e_ref[i] >= 0)
            def _():
                for cp in weight_copies(next_e_ref[i]):
                    cp.start()

        lo, hi = _unpack_bf16_pairs(_load_token_rows(xs_ref, tb))
        xb = jnp.concatenate([lo.astype(BF16), hi.astype(BF16)], axis=1)
        hg = jnp.dot(xb, wgb[...], preferred_element_type=F32)
        hu = jnp.dot(xb, wub[...], preferred_element_type=F32)
        hid = (_silu(hg) * hu).astype(BF16)
        _store_token_rows(y_ref, _pack_bf16_pairs(jnp.dot(hid, wdb[...], preferred_element_type=F32)))

    @pl.when(i >= nvalid_ref[0])
    def _():
        y_ref[...] = jnp.zeros_like(y_ref)


def _experts(blk_e, first, next_e, nvalid, xs, w_gate, w_up, w_down, *, tb=EXPERT_BLOCK):
    nb = blk_e.shape[0]
    grid_spec = pltpu.PrefetchScalarGridSpec(
        num_scalar_prefetch=4,
        grid=(nb,),
        in_specs=[pl.BlockSpec((tb * TOK_ROWS, 128),
                               lambda i, be, fr, ne, nv: (jnp.clip(i, 0, jnp.maximum(nv[0] - 1, 0)), 0)),
                  pl.BlockSpec(memory_space=pl.ANY),
                  pl.BlockSpec(memory_space=pl.ANY),
                  pl.BlockSpec(memory_space=pl.ANY)],
        out_specs=pl.BlockSpec((tb * TOK_ROWS, 128), lambda i, be, fr, ne, nv: (i, 0)),
        scratch_shapes=[pltpu.VMEM((D_MODEL, D_EXPERT), F32),
                        pltpu.VMEM((D_MODEL, D_EXPERT), F32),
                        pltpu.VMEM((D_EXPERT, D_MODEL), F32),
                        pltpu.VMEM((D_MODEL, D_EXPERT), BF16),
                        pltpu.VMEM((D_MODEL, D_EXPERT), BF16),
                        pltpu.VMEM((D_EXPERT, D_MODEL), BF16),
                        pltpu.SemaphoreType.DMA((3,))],
    )
    return pl.pallas_call(
        functools.partial(_expert_kernel, tb=tb),
        grid_spec=grid_spec,
        out_shape=jax.ShapeDtypeStruct((nb * tb * TOK_ROWS, 128), jnp.uint32),
        compiler_params=_cparams("arbitrary"),
        name="experts",
    )(blk_e, first, next_e, nvalid, xs, w_gate, w_up, w_down)


def _combine_kernel(d1_ref, d2_ref, y_hbm, x1_ref, route_ref, g_ref, b_ref, o_ref, ybuf, sem, *, tt):
    i = pl.program_id(0)
    nsteps = pl.num_programs(0)
    slot = i % 2

    def issue(tile, slot_):
        def body(j, carry):
            for u in range(ROW_UNROLL):
                r = j * ROW_UNROLL + u
                for k, d_ref in enumerate((d1_ref, d2_ref)):
                    src = pl.multiple_of(d_ref[tile * tt + r], TOK_ROWS)
                    pltpu.make_async_copy(y_hbm.at[pl.ds(src, TOK_ROWS), :],
                                          ybuf.at[slot_, k, pl.ds(r * TOK_ROWS, TOK_ROWS), :],
                                          sem.at[slot_]).start(priority=k)
            return carry
        lax.fori_loop(0, tt // ROW_UNROLL, body, 0)

    @pl.when(i == 0)
    def _():
        issue(0, 0)

    @pl.when(i + 1 < nsteps)
    def _():
        issue(i + 1, 1 - slot)

    for k in range(TOP_K):
        pltpu.make_async_copy(y_hbm.at[pl.ds(0, tt * TOK_ROWS), :], ybuf.at[slot, k], sem.at[slot]).wait()
    route = route_ref[...]
    g1 = route[:, R_G1:R_G1 + 1]
    g2 = route[:, R_G2:R_G2 + 1]
    lo1, hi1 = _unpack_bf16_pairs(_load_token_rows(ybuf.at[slot, 0], tt))
    lo2, hi2 = _unpack_bf16_pairs(_load_token_rows(ybuf.at[slot, 1], tt))
    x_lo, x_hi = _unpack_bf16_pairs(_load_token_rows(x1_ref, tt))
    h_lo = ALPHA_DN * x_lo + (g1 * lo1 + g2 * lo2)
    h_hi = ALPHA_DN * x_hi + (g1 * hi1 + g2 * hi2)
    mu = (jnp.sum(h_lo, axis=-1, keepdims=True) + jnp.sum(h_hi, axis=-1, keepdims=True)) * (1.0 / D_MODEL)
    c_lo = h_lo - mu
    c_hi = h_hi - mu
    var = (jnp.sum(c_lo * c_lo, axis=-1, keepdims=True) + jnp.sum(c_hi * c_hi, axis=-1, keepdims=True)) * (1.0 / D_MODEL)
    inv = lax.rsqrt(var + LN_EPS)
    o_ref[:, 0:HALF] = c_lo * inv * g_ref[:, 0:HALF] + b_ref[:, 0:HALF]
    o_ref[:, HALF:] = c_hi * inv * g_ref[:, HALF:] + b_ref[:, HALF:]


def _combine(dest1, dest2, ybuf, x1p, route, ln_g, ln_b, *, tt=256):
    n = x1p.shape[0] // TOK_ROWS
    grid_spec = pltpu.PrefetchScalarGridSpec(
        num_scalar_prefetch=2,
        grid=(n // tt,),
        in_specs=[pl.BlockSpec(memory_space=pl.ANY),
                  pl.BlockSpec((tt * TOK_ROWS, 128), lambda i, a, b: (i, 0)),
                  pl.BlockSpec((tt, ROUTE_W), lambda i, a, b: (i, 0)),
                  pl.BlockSpec((1, D_MODEL), lambda i, a, b: (0, 0)),
                  pl.BlockSpec((1, D_MODEL), lambda i, a, b: (0, 0))],
        out_specs=pl.BlockSpec((tt, D_MODEL), lambda i, a, b: (i, 0)),
        scratch_shapes=[pltpu.VMEM((2, TOP_K, tt * TOK_ROWS, 128), jnp.uint32),
                        pltpu.SemaphoreType.DMA((2,))],
    )
    return pl.pallas_call(
        functools.partial(_combine_kernel, tt=tt),
        grid_spec=grid_spec,
        out_shape=jax.ShapeDtypeStruct((n, D_MODEL), F32),
        compiler_params=_cparams("arbitrary"),
        name="combine",
    )(dest1, dest2, ybuf, x1p, route, ln_g, ln_b)


def _block_plan(counts_row, n, tb):
    counts = counts_row[0, :N_EXPERTS].astype(jnp.int32)
    pends = jnp.cumsum((counts + tb - 1) // tb * tb)
    nb = (n * TOP_K) // tb + N_EXPERTS
    nvalid = (pends[-1] // tb).astype(jnp.int32)
    blk = jnp.arange(nb, dtype=jnp.int32)
    blk_e = jnp.sum((pends[None, :] <= (blk * tb)[:, None]).astype(jnp.int32), axis=1)
    blk_e = jnp.minimum(blk_e, N_EXPERTS - 1)
    blk_e = jnp.where(blk < nvalid, blk_e, blk_e[jnp.maximum(nvalid - 1, 0)])
    first = jnp.concatenate([jnp.ones((1,), jnp.int32), (blk_e[1:] != blk_e[:-1]).astype(jnp.int32)])
    ids = jnp.arange(N_EXPERTS, dtype=jnp.int32)
    later = lax.cummin(jnp.where(counts > 0, ids, N_EXPERTS), axis=0, reverse=True)
    nxt = jnp.concatenate([later[1:], jnp.full((1,), N_EXPERTS, jnp.int32)])
    nxt = jnp.where(nxt < N_EXPERTS, nxt, -1)
    return blk_e, first, nxt[blk_e], nvalid.reshape(1)


def kernel(x, w_in, w_gk_up, b_gk, conv_w, a_log, dt_bias, gla_norm_g, gdn_norm_g, w_out, ln1_g, ln1_b, w_router_group, b_router_group, w_router_expert, b_router_expert, w_gate, w_up, w_down, ln2_g, ln2_b):
    bsz, seq, d = x.shape
    n = bsz * seq
    x2d = x.reshape(n, d)
    w_main, w_aux = _split_w_in(w_in[0])
    main, aux = _in_proj(x2d, w_main, w_aux, conv_w[0].astype(F32), seq=seq)
    wgk = jnp.zeros((AUX_W, 256), F32).at[:GLA_GATE_RANK].set(w_gk_up[0])
    o_a = _gla(main, aux, wgk, b_gk[0][None], gla_norm_g[0][None], bsz=bsz, seq=seq)
    o_b = _gdn(main, aux, _pad_aux_row(a_log[0], AUX_AB), _pad_aux_row(dt_bias[0], AUX_AB),
               gdn_norm_g[0][None], bsz=bsz, seq=seq)
    rpad = jnp.zeros((d, ROUTE_W - N_EXPERTS - N_GROUPS), F32)
    w_router = jnp.concatenate([w_router_expert[0], w_router_group[0], rpad], axis=1)
    w_router_hi = w_router.astype(BF16)
    w_router = jnp.concatenate([w_router_hi, (w_router - w_router_hi.astype(F32)).astype(BF16)], axis=1)
    b_router = jnp.concatenate([b_router_expert[0], b_router_group[0], rpad[0]])[None]
    x1p, route, counts = _post_mix(o_a, o_b, x2d, w_out[0].astype(BF16), ln1_g[0][None], ln1_b[0][None],
                                       w_router, b_router)
    dest = _plan(route, counts, tb=EXPERT_BLOCK)
    blk_e, first, next_e, nvalid = _block_plan(counts, n, EXPERT_BLOCK)
    xs = _dispatch(dest[0], dest[1], x1p, blk_e.shape[0] * EXPERT_BLOCK)
    ybuf = _experts(blk_e, first, next_e, nvalid, xs, w_gate[0], w_up[0], w_down[0])
    out = _combine(dest[0], dest[1], ybuf, x1p, route, ln2_g[0][None], ln2_b[0][None])
    return out.reshape(bsz, seq, d)
```

```python
import functools

import jax
import jax.numpy as jnp
import numpy as np
from jax import lax
from jax.experimental import pallas as pl
from jax.experimental.pallas import tpu as pltpu

F32 = jnp.float32
BF16 = jnp.bfloat16
HIGHEST = lax.Precision.HIGHEST

D_MODEL = 1024
DEPTH = 1
GLA_HEADS = 4
GLA_DK = 64
GLA_DV = 128
GLA_WIDTH = GLA_HEADS * GLA_DV
GLA_GATE_RANK = 16
GLA_GATE_NORM = 16.0
GDN_HEADS = 4
GDN_DK = 128
GDN_DV = 128
GDN_WIDTH = GDN_HEADS * GDN_DV
CONV_W = 4
CHUNK = 64
N_GROUPS = 8
EXPERTS_PER_GROUP = 8
N_EXPERTS = N_GROUPS * EXPERTS_PER_GROUP
TOP_K = 2
D_EXPERT = 512
LN_EPS = 1e-5
RMS_EPS = 1e-6
ALPHA_DN = (2.0 * DEPTH) ** 0.25

MAIN_W = 2 * GLA_HEADS * GLA_DK + 2 * GLA_WIDTH + 4 * GDN_WIDTH
AUX_W = 128
AUX_LRA = 0
AUX_BB = GLA_GATE_RANK
AUX_AB = GLA_GATE_RANK + GDN_HEADS
SUB = 16
VMEM_LIMIT = 56 * 1024 * 1024


def _cparams(*sem):
    return pltpu.CompilerParams(dimension_semantics=sem, vmem_limit_bytes=VMEM_LIMIT)


def _sigmoid(x):
    return 1.0 / (1.0 + jnp.exp(-x))


def _silu(x):
    return x * _sigmoid(x)


def _log_sigmoid(x):
    return jnp.minimum(x, 0.0) - jnp.log(1.0 + jnp.exp(-jnp.abs(x)))


def _softplus(x):
    return jnp.maximum(x, 0.0) + jnp.log(1.0 + jnp.exp(-jnp.abs(x)))


CONV_PAD = 8
GDN_HW = GDN_HEADS * GDN_DK
GDN_QKV_BLOCK0 = (2 * GLA_HEADS * GLA_DK + 2 * GLA_WIDTH) // GDN_HW


def _in_proj_kernel(x_ref, wm_ref, wa_ref, cw_ref, main_ref, aux_ref, stage_ref, *, tm, tiles_per_seq):
    xb = x_ref[...].astype(BF16)

    @pl.when((pl.program_id(0) % tiles_per_seq) == 0)
    def _():
        stage_ref[:, 0:CONV_PAD, :] = jnp.zeros((3 * GDN_HEADS, CONV_PAD, GDN_DK), F32)

    for j in range(MAIN_W // GDN_HW):
        sl = slice(j * GDN_HW, (j + 1) * GDN_HW)
        r = jnp.dot(xb, wm_ref[:, sl], preferred_element_type=F32)
        b = j - GDN_QKV_BLOCK0
        if not 0 <= b < 3:
            main_ref[:, sl] = r.astype(BF16)
            continue
        for h in range(GDN_HEADS):
            st = stage_ref.at[b * GDN_HEADS + h]
            cols = slice(b * GDN_HW + h * GDN_DK, b * GDN_HW + (h + 1) * GDN_DK)
            st[CONV_PAD:, :] = r[:, h * GDN_DK:(h + 1) * GDN_DK]
            acc = jnp.zeros((tm, GDN_DK), F32)
            for i in range(CONV_W):
                lo = CONV_PAD - (CONV_W - 1) + i
                acc = acc + st[lo:lo + tm, :] * cw_ref[i:i + 1, cols]
            st[0:CONV_PAD, :] = st[tm:tm + CONV_PAD, :]
            yh = _silu(acc)
            if b < 2:
                yh = yh * lax.rsqrt(jnp.sum(yh * yh, axis=-1, keepdims=True) + RMS_EPS)
            if b == 0:
                yh = yh * (GDN_DK ** -0.5)
            main_ref[:, j * GDN_HW + h * GDN_DK:j * GDN_HW + (h + 1) * GDN_DK] = yh.astype(BF16)
    aux_ref[...] = jnp.dot(xb, wa_ref[...], preferred_element_type=F32)


def _in_proj(x2d, w_main, w_aux, conv_w, *, seq, tm=512):
    n = x2d.shape[0]
    return pl.pallas_call(
        functools.partial(_in_proj_kernel, tm=tm, tiles_per_seq=seq // tm),
        grid=(n // tm,),
        in_specs=[pl.BlockSpec((tm, D_MODEL), lambda i: (i, 0)),
                  pl.BlockSpec((D_MODEL, MAIN_W), lambda i: (0, 0)),
                  pl.BlockSpec((D_MODEL, AUX_W), lambda i: (0, 0)),
                  pl.BlockSpec((CONV_W, 3 * GDN_HW), lambda i: (0, 0))],
        out_specs=[pl.BlockSpec((tm, MAIN_W), lambda i: (i, 0)),
                   pl.BlockSpec((tm, AUX_W), lambda i: (i, 0))],
        out_shape=[jax.ShapeDtypeStruct((n, MAIN_W), BF16),
                   jax.ShapeDtypeStruct((n, AUX_W), F32)],
        scratch_shapes=[pltpu.VMEM((3 * GDN_HEADS, tm + CONV_PAD, GDN_DK), F32)],
        compiler_params=_cparams("arbitrary"),
        name="in_proj",
    )(x2d, w_main, w_aux, conv_w)


def _split_w_in(w_in):
    sizes = (GLA_HEADS * GLA_DK, GLA_HEADS * GLA_DK, GLA_WIDTH, GLA_WIDTH, GLA_GATE_RANK,
             GDN_HEADS * GDN_DK, GDN_HEADS * GDN_DK, GDN_WIDTH, GDN_WIDTH, GDN_HEADS, GDN_HEADS)
    offs = np.cumsum((0,) + sizes)
    seg = [w_in[:, offs[i]:offs[i + 1]] for i in range(len(sizes))]
    qa, ka, va, ra, lra, qb, kb, vb, zb, bb, ab = seg
    w_main = jnp.concatenate([qa, ka, va, ra, qb, kb, vb, zb], axis=1).astype(BF16)
    pad = jnp.zeros((w_in.shape[0], AUX_W - GLA_GATE_RANK - 2 * GDN_HEADS), w_in.dtype)
    w_aux = jnp.concatenate([lra, bb, ab, pad], axis=1).astype(BF16)
    return w_main, w_aux


_NT = (((1,), (1,)), ((), ()))
_TN = (((0,), (0,)), ((), ()))


def _gla_kernel(q_ref, k_ref, v_ref, r_ref, aux_ref, wgk_ref, bgk_ref, g_ref, o_ref, st_ref, *, ts):
    c = CHUNK
    nsub = c // SUB

    @pl.when(pl.program_id(1) == 0)
    def _():
        st_ref[...] = jnp.zeros_like(st_ref)

    pre = jnp.dot(aux_ref[...], wgk_ref[...], precision=HIGHEST, preferred_element_type=F32) + bgk_ref[...]
    gk_all = _log_sigmoid(pre) * (1.0 / GLA_GATE_NORM)

    ri = lax.broadcasted_iota(jnp.int32, (c, c), 0)
    ci = lax.broadcasted_iota(jnp.int32, (c, c), 1)
    tri = (ci <= ri).astype(F32)
    row128 = lax.broadcasted_iota(jnp.int32, (c, 128), 0)
    lane128 = lax.broadcasted_iota(jnp.int32, (1, 128), 1)
    lane_masks = [(lane128 // GLA_DK) == hh for hh in range(2)]
    sub_valid = [row128 < SUB * (i + 1) for i in range(nsub)]
    ar = lax.broadcasted_iota(jnp.int32, (c, nsub * c), 0)
    ac = lax.broadcasted_iota(jnp.int32, (c, nsub * c), 1)
    amask = ((ac // c) == (ar // SUB)) & ((ac % c) <= ar)
    scale = GLA_DK ** -0.5
    gnorm = g_ref[...]

    nch = ts // c
    rows = [slice(ch * c, (ch + 1) * c) for ch in range(nch)]
    b_all = [jnp.dot(tri, gk_all[r], precision=HIGHEST, preferred_element_type=F32) for r in rows]

    cp = [(ch, p) for ch in range(nch) for p in range(2)]
    qt, qd, kd, kstack, dlast = {}, {}, {}, {}, {}
    for ch, p in cp:
        lanes = slice(128 * p, 128 * (p + 1))
        bp = b_all[ch][:, lanes]
        qp = q_ref[rows[ch], lanes].astype(F32) * scale
        kp = k_ref[rows[ch], lanes].astype(F32)
        c_row = jnp.concatenate(
            [jnp.broadcast_to(bp[SUB * i:SUB * i + 1, :], (SUB, 128)) for i in range(nsub)], axis=0)
        qt[ch, p] = (qp * jnp.exp(bp - c_row)).astype(BF16)
        qd[ch, p] = qp * jnp.exp(bp)
        b_last = bp[c - 1:c, :]
        dlast[ch, p] = jnp.exp(b_last)
        kd[ch, p] = kp * jnp.exp(b_last - bp)
        kts = []
        for i in range(nsub):
            e = jnp.exp(jnp.where(sub_valid[i], bp[SUB * i:SUB * i + 1, :] - bp, 0.0))
            kts.append(jnp.where(sub_valid[i], kp * e, 0.0))
        kstack[ch, p] = jnp.concatenate(kts, axis=0)

    cph = [(ch, p, hh) for ch, p in cp for hh in range(2)]
    vh = {(ch, p, hh): v_ref[rows[ch], 128 * (2 * p + hh):128 * (2 * p + hh + 1)] for ch, p, hh in cph}
    r_mat = {(ch, p, hh): lax.dot_general(qt[ch, p], jnp.where(lane_masks[hh], kstack[ch, p], 0.0).astype(BF16),
                                          _NT, preferred_element_type=F32) for ch, p, hh in cph}
    o_intra = {k: jnp.dot(jnp.where(amask, r_mat[k], 0.0).astype(BF16), jnp.concatenate([vh[k]] * nsub, axis=0),
                          preferred_element_type=F32) for k in cph}
    upd = {(ch, p, hh): lax.dot_general(vh[ch, p, hh], jnp.where(lane_masks[hh], kd[ch, p], 0.0).astype(BF16), _TN,
                                        preferred_element_type=F32) for ch, p, hh in cph}

    st = [st_ref[p] for p in range(2)]
    for ch in range(nch):
        st_b = [s.astype(BF16) for s in st]
        o_pair = [lax.dot_general(
            jnp.concatenate([jnp.where(lane_masks[hh], qd[ch, p], 0.0).astype(BF16) for hh in range(2)], axis=0),
            st_b[p], _NT, preferred_element_type=F32) for p in range(2)]
        o_inter = {(p, hh): o_pair[p][hh * c:(hh + 1) * c] for p in range(2) for hh in range(2)}
        st = [st[p] * dlast[ch, p] + upd[ch, p, 0] + upd[ch, p, 1] for p in range(2)]
        for p in range(2):
            for hh in range(2):
                h = 2 * p + hh
                o = o_intra[ch, p, hh] + o_inter[p, hh]
                o = o * lax.rsqrt(jnp.mean(o * o, axis=-1, keepdims=True) + RMS_EPS) * gnorm
                gate = _silu(r_ref[rows[ch], 128 * h:128 * (h + 1)].astype(F32))
                o_ref[rows[ch], 128 * h:128 * (h + 1)] = (o * gate).astype(BF16)
    for p in range(2):
        st_ref[p] = st[p]


def _gla(main, aux, wgk_pad, b_gk, gla_norm_g, *, bsz, seq, ts=512):
    nt = seq // ts
    row = lambda b, t: b * nt + t
    return pl.pallas_call(
        functools.partial(_gla_kernel, ts=ts),
        grid=(bsz, nt),
        in_specs=[pl.BlockSpec((ts, 256), lambda b, t: (row(b, t), 0)),
                  pl.BlockSpec((ts, 256), lambda b, t: (row(b, t), 1)),
                  pl.BlockSpec((ts, 512), lambda b, t: (row(b, t), 1)),
                  pl.BlockSpec((ts, 512), lambda b, t: (row(b, t), 2)),
                  pl.BlockSpec((ts, AUX_W), lambda b, t: (row(b, t), 0)),
                  pl.BlockSpec((AUX_W, 256), lambda b, t: (0, 0)),
                  pl.BlockSpec((1, 256), lambda b, t: (0, 0)),
                  pl.BlockSpec((1, 128), lambda b, t: (0, 0))],
        out_specs=pl.BlockSpec((ts, GLA_WIDTH), lambda b, t: (row(b, t), 0)),
        out_shape=jax.ShapeDtypeStruct((bsz * seq, GLA_WIDTH), BF16),
        scratch_shapes=[pltpu.VMEM((2, 128, 128), F32)],
        compiler_params=_cparams("parallel", "arbitrary"),
        name="gla",
    )(main, main, main, main, aux, wgk_pad, b_gk, gla_norm_g)


def _gdn_kernel(q_ref, k_ref, v_ref, z_ref, aux_ref, alog_ref, dtb_ref, g_ref, o_ref, st_ref, *, ts):
    c = CHUNK
    nch = ts // c

    @pl.when(pl.program_id(1) == 0)
    def _():
        st_ref[...] = jnp.zeros_like(st_ref)

    aux = aux_ref[...]
    beta_full = _sigmoid(aux)
    g_full = -jnp.exp(alog_ref[...]) * _softplus(aux + dtb_ref[...])
    ri = lax.broadcasted_iota(jnp.int32, (ts, ts), 0)
    ci = lax.broadcasted_iota(jnp.int32, (ts, ts), 1)
    cum = (((ri // c) == (ci // c)) & (ci <= ri)).astype(F32)
    gam_full = jnp.dot(cum, g_full, precision=HIGHEST, preferred_element_type=F32)
    gam_t = gam_full.T
    gnorm = g_ref[...]

    row = lax.broadcasted_iota(jnp.int32, (c, 2 * c), 0)
    lane = lax.broadcasted_iota(jnp.int32, (c, 2 * c), 1)
    left = lane < c
    col = lane % c
    causal = col <= row
    strict = col < row
    same32 = (row // 32) == (col // 32)
    same16 = (row // 16) == (col // 16)
    lvl0 = same16 & strict
    lvl1 = same32 & jnp.logical_not(same16) & strict
    lvl2 = jnp.logical_not(same32) & strict

    def pack_rows(x2):
        return jnp.where(left, x2[0:c], x2[c:2 * c])

    def pack_cols(x1):
        return jnp.where(left, x1[0:c], x1[c:2 * c])

    def halves(x):
        return jnp.where(left, x, 0.0).astype(BF16), jnp.where(left, 0.0, x).astype(BF16)

    def mmp(x, y):
        return jnp.dot(x.astype(BF16), jnp.concatenate(halves(y), axis=0), preferred_element_type=F32)

    heads = range(GDN_HEADS)
    npair = nch // 2
    items = [(h, pr) for h in heads for pr in range(npair)]
    prow = [slice(2 * c * pr, 2 * c * (pr + 1)) for pr in range(npair)]

    qn, kn, beta, gam, gam_r, kb, rhs, rhs_b, qd, kd = ([None] * GDN_HEADS for _ in range(10))
    for h in heads:
        hl = slice(128 * h, 128 * (h + 1))
        kb[h] = k_ref[:, hl]
        qn[h] = q_ref[:, hl].astype(F32)
        kn[h] = kb[h].astype(F32)
        vh = v_ref[:, hl].astype(F32)
        beta[h] = beta_full[:, AUX_BB + h:AUX_BB + h + 1]
        gam[h] = gam_full[:, AUX_AB + h:AUX_AB + h + 1]
        gam_r[h] = gam_t[AUX_AB + h:AUX_AB + h + 1, :]
        egam = jnp.exp(gam[h])
        rhs[h] = jnp.concatenate([beta[h] * vh, beta[h] * egam * kn[h]], axis=1)
        rhs_b[h] = rhs[h].astype(BF16)
        qd[h] = (qn[h] * egam).astype(BF16)
        gl_rows = jnp.concatenate(
            [jnp.broadcast_to(gam[h][c * (j + 1) - 1:c * (j + 1), :], (c, 1)) for j in range(nch)], axis=0)
        kd[h] = (kn[h] * jnp.exp(gl_rows - gam[h])).astype(BF16)

    kk = [pack_rows(lax.dot_general(kb[h][prow[pr]], kb[h][prow[pr]], _NT, preferred_element_type=F32))
          for h, pr in items]
    qk = [pack_rows(lax.dot_general(qn[h][prow[pr]].astype(BF16), kb[h][prow[pr]], _NT,
                                    preferred_element_type=F32)) for h, pr in items]
    dec = [jnp.exp(jnp.where(causal, pack_cols(gam[h][prow[pr]]) - gam_r[h][:, prow[pr]], 0.0)) for h, pr in items]
    a = [pack_cols(beta[h][prow[pr]]) * kk[i] * dec[i] for i, (h, pr) in enumerate(items)]
    attn = [halves(jnp.where(causal, qk[i] * dec[i], 0.0)) for i in range(len(items))]

    x1 = [jnp.where(lvl0, -ai, 0.0) for ai in a]
    x2 = [mmp(x, x) for x in x1]
    y = [mmp(x, xx) for x, xx in zip(x1, x2)]
    p = [x + xx + yy for x, xx, yy in zip(x1, x2, y)]
    xk = x2
    for _ in range(2):
        xk = [mmp(x, x) for x in xk]
        y = [mmp(pp, x) for pp, x in zip(p, xk)]
        p = [pp + x + yy for pp, x, yy in zip(p, xk, y)]
    for lvl in (lvl1, lvl2):
        low = [jnp.where(lvl, ai, 0.0) for ai in a]
        y = [mmp(pp, lo) for pp, lo in zip(p, low)]
        zz = [lo + yy for lo, yy in zip(low, y)]
        y = [mmp(z, pp) for z, pp in zip(zz, p)]
        p = [pp - z - yy for pp, z, yy in zip(p, zz, y)]

    us = [[None] * nch for _ in heads]
    ws = [[None] * nch for _ in heads]
    for i, (h, pr) in enumerate(items):
        for half, p_half in enumerate(halves(p[i])):
            j = 2 * pr + half
            uw = rhs[h][c * j:c * (j + 1)] + jnp.dot(p_half, rhs_b[h][prow[pr]], preferred_element_type=F32)
            us[h][j] = uw[:, :GDN_DV]
            ws[h][j] = uw[:, GDN_DV:].astype(BF16)

    s = [st_ref[h] for h in heads]
    vprev = [None] * GDN_HEADS
    for j in range(nch):
        rows = slice(c * j, c * (j + 1))
        sb = [s[h].astype(BF16) for h in heads]
        ws_qd = [jnp.dot(jnp.concatenate([ws[h][j], qd[h][rows]], axis=0), sb[h], preferred_element_type=F32)
                 for h in heads]
        vn16 = [(us[h][j] - ws_qd[h][0:c]).astype(BF16) for h in heads]
        o_inter = [ws_qd[h][c:2 * c] for h in heads]
        for h in heads:
            vpair = jnp.concatenate([vn16[h], jnp.zeros_like(vn16[h])] if j % 2 == 0 else [vprev[h], vn16[h]], axis=0)
            o = o_inter[h] + jnp.dot(attn[h * npair + j // 2][j % 2], vpair, preferred_element_type=F32)
            s[h] = jnp.exp(gam[h][c * (j + 1) - 1:c * (j + 1), :]) * s[h] + lax.dot_general(
                kd[h][rows], vn16[h], _TN, preferred_element_type=F32)
            o = o * lax.rsqrt(jnp.mean(o * o, axis=-1, keepdims=True) + RMS_EPS) * gnorm
            gate = _silu(z_ref[rows, 128 * h:128 * (h + 1)].astype(F32))
            o_ref[rows, 128 * h:128 * (h + 1)] = (o * gate).astype(BF16)
        vprev = vn16
    for h in heads:
        st_ref[h] = s[h]


def _gdn(main, aux, alog_pad, dtb_pad, gdn_norm_g, *, bsz, seq, ts=512):
    nt = seq // ts
    row = lambda b, t: b * nt + t
    hw = GDN_HEADS * GDN_DK
    return pl.pallas_call(
        functools.partial(_gdn_kernel, ts=ts),
        grid=(bsz, nt),
        in_specs=[pl.BlockSpec((ts, hw), lambda b, t: (row(b, t), 3)),
                  pl.BlockSpec((ts, hw), lambda b, t: (row(b, t), 4)),
                  pl.BlockSpec((ts, hw), lambda b, t: (row(b, t), 5)),
                  pl.BlockSpec((ts, hw), lambda b, t: (row(b, t), 6)),
                  pl.BlockSpec((ts, AUX_W), lambda b, t: (row(b, t), 0)),
                  pl.BlockSpec((1, AUX_W), lambda b, t: (0, 0)),
                  pl.BlockSpec((1, AUX_W), lambda b, t: (0, 0)),
                  pl.BlockSpec((1, GDN_DV), lambda b, t: (0, 0))],
        out_specs=pl.BlockSpec((ts, GDN_WIDTH), lambda b, t: (row(b, t), 0)),
        out_shape=jax.ShapeDtypeStruct((bsz * seq, GDN_WIDTH), BF16),
        scratch_shapes=[pltpu.VMEM((GDN_HEADS, GDN_DK, GDN_DV), F32)],
        compiler_params=_cparams("parallel", "arbitrary"),
        name="gdn",
    )(main, main, main, main, aux, alog_pad, dtb_pad, gdn_norm_g)


def _pad_aux_row(v, offset):
    return jnp.zeros((1, AUX_W), F32).at[0, offset:offset + v.shape[0]].set(v.astype(F32))


ROUTE_W = 128
R_E1, R_E2, R_RANK1, R_RANK2, R_G1, R_G2 = range(6)
ROUTER_GROUP_COL = N_EXPERTS


def _layer_norm(h, g, b):
    mu = jnp.mean(h, axis=-1, keepdims=True)
    hc = h - mu
    var = jnp.mean(hc * hc, axis=-1, keepdims=True)
    return hc * lax.rsqrt(var + LN_EPS) * g + b


def _pack_bf16_pairs(x):
    w = x.shape[1] // 2
    lo = lax.bitcast_convert_type(x[:, :w].astype(BF16).astype(F32), jnp.uint32)
    hi = lax.bitcast_convert_type(x[:, w:].astype(BF16).astype(F32), jnp.uint32)
    return (lo >> 16) | (hi & jnp.uint32(0xFFFF0000))


def _unpack_bf16_pairs(p):
    lo = lax.bitcast_convert_type(p << 16, F32)
    hi = lax.bitcast_convert_type(p & jnp.uint32(0xFFFF0000), F32)
    return lo, hi


HALF = D_MODEL // 2
TOK_ROWS = HALF // 128


def _store_token_rows(ref, packed):
    m = packed.shape[0]
    for c in range(TOK_ROWS):
        ref[pl.ds(c, m, stride=TOK_ROWS), :] = packed[:, 128 * c:128 * (c + 1)]


def _load_token_rows(ref, m):
    return jnp.concatenate([ref[pl.ds(c, m, stride=TOK_ROWS), :] for c in range(TOK_ROWS)], axis=1)


def _post_mix_kernel(oa_ref, ob_ref, x_ref, wo_ref, g_ref, b_ref, wr_ref, br_ref,
                     x1p_ref, route_ref, cnt_ref, carry_ref, *, tm):
    @pl.when(pl.program_id(0) == 0)
    def _():
        carry_ref[...] = jnp.zeros_like(carry_ref)

    y = jnp.dot(oa_ref[...], wo_ref[0:GLA_WIDTH, :], preferred_element_type=F32)
    y = y + jnp.dot(ob_ref[...], wo_ref[GLA_WIDTH:, :], preferred_element_type=F32)
    x1 = _layer_norm(ALPHA_DN * x_ref[...] + y, g_ref[...], b_ref[...])
    _store_token_rows(x1p_ref, _pack_bf16_pairs(x1))

    xh = x1.astype(BF16)
    xl = (x1 - xh.astype(F32)).astype(BF16)
    hw = jnp.dot(xh, wr_ref[...], preferred_element_type=F32)
    logits = (hw[:, :ROUTE_W] + hw[:, ROUTE_W:]
              + jnp.dot(xl, wr_ref[:, :ROUTE_W], preferred_element_type=F32) + br_ref[...])
    lane = lax.broadcasted_iota(jnp.int32, (tm, ROUTE_W), 1)
    big = jnp.int32(1 << 20)
    neg = jnp.float32(-jnp.inf)

    def first_argmax(vals):
        m = jnp.max(vals, axis=-1, keepdims=True)
        idx = jnp.min(jnp.where(vals == m, lane, big), axis=-1, keepdims=True)
        return m, idx

    is_group = (lane >= ROUTER_GROUP_COL) & (lane < ROUTER_GROUP_COL + N_GROUPS)
    gl = jnp.where(is_group, logits, neg)
    gmax, gidx = first_argmax(gl)
    p_group = 1.0 / jnp.sum(jnp.exp(gl - gmax), axis=-1, keepdims=True)
    g_sel = gidx - ROUTER_GROUP_COL
    el = jnp.where((lane // EXPERTS_PER_GROUP) == g_sel, logits, neg)
    t1, e1 = first_argmax(el)
    t2, e2 = first_argmax(jnp.where(lane == e1, neg, el))
    ex = jnp.exp(t2 - t1)
    w1 = 1.0 / (1.0 + ex)
    gate1 = p_group * w1
    gate2 = p_group * (ex * w1)

    oh1 = lane == e1
    oh2 = lane == e2
    ri = lax.broadcasted_iota(jnp.int32, (tm, tm), 0)
    ci = lax.broadcasted_iota(jnp.int32, (tm, tm), 1)
    lstrict = (ci < ri).astype(BF16)
    oh1f = oh1.astype(F32)
    oh2f = oh2.astype(F32)
    c1 = jnp.dot(lstrict, oh1f.astype(BF16), preferred_element_type=F32)
    c2 = jnp.dot(lstrict, oh2f.astype(BF16), preferred_element_type=F32)
    carry = carry_ref[...]
    tot1 = jnp.sum(oh1f, axis=0, keepdims=True)
    tot2 = jnp.sum(oh2f, axis=0, keepdims=True)
    rank1 = jnp.sum(jnp.where(oh1, c1 + carry, 0.0), axis=-1, keepdims=True)
    rank2 = jnp.sum(jnp.where(oh2, c2 + (carry + tot1), 0.0), axis=-1, keepdims=True)
    new_carry = carry + tot1 + tot2
    carry_ref[...] = new_carry
    cnt_ref[...] = new_carry

    cols = (e1.astype(F32), e2.astype(F32), rank1, rank2, gate1, gate2)
    route = jnp.zeros((tm, ROUTE_W), F32)
    for j, col in enumerate(cols):
        route = jnp.where(lane == j, col, route)
    route_ref[...] = route


def _post_mix(o_a, o_b, x2d, w_out_b, ln_g, ln_b, w_router, b_router, *, tm=512):
    n = x2d.shape[0]
    return pl.pallas_call(
        functools.partial(_post_mix_kernel, tm=tm),
        grid=(n // tm,),
        in_specs=[pl.BlockSpec((tm, GLA_WIDTH), lambda i: (i, 0)),
                  pl.BlockSpec((tm, GDN_WIDTH), lambda i: (i, 0)),
                  pl.BlockSpec((tm, D_MODEL), lambda i: (i, 0)),
                  pl.BlockSpec((GLA_WIDTH + GDN_WIDTH, D_MODEL), lambda i: (0, 0)),
                  pl.BlockSpec((1, D_MODEL), lambda i: (0, 0)),
                  pl.BlockSpec((1, D_MODEL), lambda i: (0, 0)),
                  pl.BlockSpec((D_MODEL, 2 * ROUTE_W), lambda i: (0, 0)),
                  pl.BlockSpec((1, ROUTE_W), lambda i: (0, 0))],
        out_specs=[pl.BlockSpec((tm * TOK_ROWS, 128), lambda i: (i, 0)),
                   pl.BlockSpec((tm, ROUTE_W), lambda i: (i, 0)),
                   pl.BlockSpec((1, ROUTE_W), lambda i: (0, 0))],
        out_shape=[jax.ShapeDtypeStruct((n * TOK_ROWS, 128), jnp.uint32),
                   jax.ShapeDtypeStruct((n, ROUTE_W), F32),
                   jax.ShapeDtypeStruct((1, ROUTE_W), F32)],
        scratch_shapes=[pltpu.VMEM((1, ROUTE_W), F32)],
        compiler_params=_cparams("arbitrary"),
        name="post_mix",
    )(o_a, o_b, x2d, w_out_b, ln_g, ln_b, w_router, b_router)


EXPERT_BLOCK = 256


ROW_UNROLL = 8


def _plan_kernel(route_ref, cnt_ref, dest_ref, *, tm, tb):
    counts = cnt_ref[...].astype(jnp.int32)
    padded = ((counts + (tb - 1)) & jnp.int32(-tb)).astype(F32)
    ri = lax.broadcasted_iota(jnp.int32, (ROUTE_W, ROUTE_W), 0)
    ci = lax.broadcasted_iota(jnp.int32, (ROUTE_W, ROUTE_W), 1)
    pstart = jnp.dot(jnp.broadcast_to(padded, (8, ROUTE_W)), (ri < ci).astype(F32),
                     precision=HIGHEST, preferred_element_type=F32)[0:1, :]
    route = route_ref[...]
    lane = lax.broadcasted_iota(jnp.int32, (tm, ROUTE_W), 1).astype(F32)
    out = jnp.zeros((tm, ROUTE_W), F32)
    for k, (ce, cr) in enumerate(((R_E1, R_RANK1), (R_E2, R_RANK2))):
        off = jnp.sum(jnp.where(lane == route[:, ce:ce + 1], pstart, 0.0), axis=-1, keepdims=True)
        out = jnp.where(lane == float(k), off + route[:, cr:cr + 1], out)
    dest_ref[...] = out.T[0:8, :].astype(jnp.int32) * TOK_ROWS


def _plan(route, counts, *, tb, tm=2048):
    n = route.shape[0]
    tm = min(tm, n)
    return pl.pallas_call(
        functools.partial(_plan_kernel, tm=tm, tb=tb),
        grid=(n // tm,),
        in_specs=[pl.BlockSpec((tm, ROUTE_W), lambda i: (i, 0)),
                  pl.BlockSpec((1, ROUTE_W), lambda i: (0, 0))],
        out_specs=pl.BlockSpec((8, tm), lambda i: (0, i)),
        out_shape=jax.ShapeDtypeStruct((8, n), jnp.int32),
        compiler_params=_cparams("parallel"),
        name="plan",
    )(route, counts)


def _dispatch_kernel(d1_ref, d2_ref, xp_ref, xs_init_ref, xs_ref, sem, *, tm):
    del xs_init_ref
    base = pl.program_id(0) * tm

    def body(j, carry):
        for u in range(ROW_UNROLL):
            r = j * ROW_UNROLL + u
            for k, d_ref in enumerate((d1_ref, d2_ref)):
                dst = pl.multiple_of(d_ref[base + r], TOK_ROWS)
                pltpu.make_async_copy(xp_ref.at[pl.ds(r * TOK_ROWS, TOK_ROWS), :],
                                      xs_ref.at[pl.ds(dst, TOK_ROWS), :], sem).start(priority=k)
        return carry
    lax.fori_loop(0, tm // ROW_UNROLL, body, 0)
    for _ in range(TOP_K):
        pltpu.make_async_copy(xp_ref, xs_ref.at[pl.ds(0, tm * TOK_ROWS), :], sem).wait()


def _dispatch(dest1, dest2, x1p, n_rows, *, tm=1024):
    n = x1p.shape[0] // TOK_ROWS
    grid_spec = pltpu.PrefetchScalarGridSpec(
        num_scalar_prefetch=2,
        grid=(n // tm,),
        in_specs=[pl.BlockSpec((tm * TOK_ROWS, 128), lambda i, a, b: (i, 0)),
                  pl.BlockSpec(memory_space=pl.ANY)],
        out_specs=pl.BlockSpec(memory_space=pl.ANY),
        scratch_shapes=[pltpu.SemaphoreType.DMA(())],
    )
    return pl.pallas_call(
        functools.partial(_dispatch_kernel, tm=tm),
        grid_spec=grid_spec,
        out_shape=jax.ShapeDtypeStruct((n_rows * TOK_ROWS, 128), jnp.uint32),
        input_output_aliases={3: 0},
        compiler_params=_cparams("arbitrary"),
        name="dispatch",
    )(dest1, dest2, x1p, jnp.zeros((n_rows * TOK_ROWS, 128), jnp.uint32))


def _expert_kernel(blk_e_ref, first_ref, next_e_ref, nvalid_ref, xs_ref, wg_hbm, wu_hbm, wd_hbm, y_ref,
                   wgs, wus, wds, wgb, wub, wdb, sem, *, tb):
    i = pl.program_id(0)

    def weight_copies(e):
        return [pltpu.make_async_copy(src.at[e], dst, sem.at[j])
                for j, (src, dst) in enumerate(((wg_hbm, wgs), (wu_hbm, wus), (wd_hbm, wds)))]

    @pl.when((i == 0) & (nvalid_ref[0] > 0))
    def _():
        for cp in weight_copies(blk_e_ref[0]):
            cp.start()

    @pl.when(i < nvalid_ref[0])
    def _():
        @pl.when(first_ref[i] == 1)
        def _():
            for cp in weight_copies(blk_e_ref[i]):
                cp.wait()
            wgb[...] = wgs[...].astype(BF16)
            wub[...] = wus[...].astype(BF16)
            wdb[...] = wds[...].astype(BF16)

            @pl.when(next_e_ref[i] >= 0)
            def _():
                for cp in weight_copies(next_e_ref[i]):
                    cp.start()

        lo, hi = _unpack_bf16_pairs(_load_token_rows(xs_ref, tb))
        xb = jnp.concatenate([lo.astype(BF16), hi.astype(BF16)], axis=1)
        hg = jnp.dot(xb, wgb[...], preferred_element_type=F32)
        hu = jnp.dot(xb, wub[...], preferred_element_type=F32)
        hid = (_silu(hg) * hu).astype(BF16)
        _store_token_rows(y_ref, _pack_bf16_pairs(jnp.dot(hid, wdb[...], preferred_element_type=F32)))

    @pl.when(i >= nvalid_ref[0])
    def _():
        y_ref[...] = jnp.zeros_like(y_ref)


def _experts(blk_e, first, next_e, nvalid, xs, w_gate, w_up, w_down, *, tb=EXPERT_BLOCK):
    nb = blk_e.shape[0]
    grid_spec = pltpu.PrefetchScalarGridSpec(
        num_scalar_prefetch=4,
        grid=(nb,),
        in_specs=[pl.BlockSpec((tb * TOK_ROWS, 128),
                               lambda i, be, fr, ne, nv: (jnp.clip(i, 0, jnp.maximum(nv[0] - 1, 0)), 0)),
                  pl.BlockSpec(memory_space=pl.ANY),
                  pl.BlockSpec(memory_space=pl.ANY),
                  pl.BlockSpec(memory_space=pl.ANY)],
        out_specs=pl.BlockSpec((tb * TOK_ROWS, 128), lambda i, be, fr, ne, nv: (i, 0)),
        scratch_shapes=[pltpu.VMEM((D_MODEL, D_EXPERT), F32),
                        pltpu.VMEM((D_MODEL, D_EXPERT), F32),
                        pltpu.VMEM((D_EXPERT, D_MODEL), F32),
                        pltpu.VMEM((D_MODEL, D_EXPERT), BF16),
                        pltpu.VMEM((D_MODEL, D_EXPERT), BF16),
                        pltpu.VMEM((D_EXPERT, D_MODEL), BF16),
                        pltpu.SemaphoreType.DMA((3,))],
    )
    return pl.pallas_call(
        functools.partial(_expert_kernel, tb=tb),
        grid_spec=grid_spec,
        out_shape=jax.ShapeDtypeStruct((nb * tb * TOK_ROWS, 128), jnp.uint32),
        compiler_params=_cparams("arbitrary"),
        name="experts",
    )(blk_e, first, next_e, nvalid, xs, w_gate, w_up, w_down)


def _combine_kernel(d1_ref, d2_ref, y_hbm, x1_ref, route_ref, g_ref, b_ref, o_ref, ybuf, sem, *, tt):
    i = pl.program_id(0)
    nsteps = pl.num_programs(0)
    slot = i % 2

    def issue(tile, slot_):
        def body(j, carry):
            for u in range(ROW_UNROLL):
                r = j * ROW_UNROLL + u
                for k, d_ref in enumerate((d1_ref, d2_ref)):
                    src = pl.multiple_of(d_ref[tile * tt + r], TOK_ROWS)
                    pltpu.make_async_copy(y_hbm.at[pl.ds(src, TOK_ROWS), :],
                                          ybuf.at[slot_, k, pl.ds(r * TOK_ROWS, TOK_ROWS), :],
                                          sem.at[slot_]).start(priority=k)
            return carry
        lax.fori_loop(0, tt // ROW_UNROLL, body, 0)

    @pl.when(i == 0)
    def _():
        issue(0, 0)

    @pl.when(i + 1 < nsteps)
    def _():
        issue(i + 1, 1 - slot)

    for k in range(TOP_K):
        pltpu.make_async_copy(y_hbm.at[pl.ds(0, tt * TOK_ROWS), :], ybuf.at[slot, k], sem.at[slot]).wait()
    route = route_ref[...]
    g1 = route[:, R_G1:R_G1 + 1]
    g2 = route[:, R_G2:R_G2 + 1]
    lo1, hi1 = _unpack_bf16_pairs(_load_token_rows(ybuf.at[slot, 0], tt))
    lo2, hi2 = _unpack_bf16_pairs(_load_token_rows(ybuf.at[slot, 1], tt))
    x_lo, x_hi = _unpack_bf16_pairs(_load_token_rows(x1_ref, tt))
    h_lo = ALPHA_DN * x_lo + (g1 * lo1 + g2 * lo2)
    h_hi = ALPHA_DN * x_hi + (g1 * hi1 + g2 * hi2)
    mu = (jnp.sum(h_lo, axis=-1, keepdims=True) + jnp.sum(h_hi, axis=-1, keepdims=True)) * (1.0 / D_MODEL)
    c_lo = h_lo - mu
    c_hi = h_hi - mu
    var = (jnp.sum(c_lo * c_lo, axis=-1, keepdims=True) + jnp.sum(c_hi * c_hi, axis=-1, keepdims=True)) * (1.0 / D_MODEL)
    inv = lax.rsqrt(var + LN_EPS)
    o_ref[:, 0:HALF] = c_lo * inv * g_ref[:, 0:HALF] + b_ref[:, 0:HALF]
    o_ref[:, HALF:] = c_hi * inv * g_ref[:, HALF:] + b_ref[:, HALF:]


def _combine(dest1, dest2, ybuf, x1p, route, ln_g, ln_b, *, tt=512):
    n = x1p.shape[0] // TOK_ROWS
    grid_spec = pltpu.PrefetchScalarGridSpec(
        num_scalar_prefetch=2,
        grid=(n // tt,),
        in_specs=[pl.BlockSpec(memory_space=pl.ANY),
                  pl.BlockSpec((tt * TOK_ROWS, 128), lambda i, a, b: (i, 0)),
                  pl.BlockSpec((tt, ROUTE_W), lambda i, a, b: (i, 0)),
                  pl.BlockSpec((1, D_MODEL), lambda i, a, b: (0, 0)),
                  pl.BlockSpec((1, D_MODEL), lambda i, a, b: (0, 0))],
        out_specs=pl.BlockSpec((tt, D_MODEL), lambda i, a, b: (i, 0)),
        scratch_shapes=[pltpu.VMEM((2, TOP_K, tt * TOK_ROWS, 128), jnp.uint32),
                        pltpu.SemaphoreType.DMA((2,))],
    )
    return pl.pallas_call(
        functools.partial(_combine_kernel, tt=tt),
        grid_spec=grid_spec,
        out_shape=jax.ShapeDtypeStruct((n, D_MODEL), F32),
        compiler_params=_cparams("arbitrary"),
        name="combine",
    )(dest1, dest2, ybuf, x1p, route, ln_g, ln_b)


def _block_plan(counts_row, n, tb):
    counts = counts_row[0, :N_EXPERTS].astype(jnp.int32)
    pends = jnp.cumsum((counts + tb - 1) // tb * tb)
    nb = (n * TOP_K) // tb + N_EXPERTS
    nvalid = (pends[-1] // tb).astype(jnp.int32)
    blk = jnp.arange(nb, dtype=jnp.int32)
    blk_e = jnp.sum((pends[None, :] <= (blk * tb)[:, None]).astype(jnp.int32), axis=1)
    blk_e = jnp.minimum(blk_e, N_EXPERTS - 1)
    blk_e = jnp.where(blk < nvalid, blk_e, blk_e[jnp.maximum(nvalid - 1, 0)])
    first = jnp.concatenate([jnp.ones((1,), jnp.int32), (blk_e[1:] != blk_e[:-1]).astype(jnp.int32)])
    ids = jnp.arange(N_EXPERTS, dtype=jnp.int32)
    later = lax.cummin(jnp.where(counts > 0, ids, N_EXPERTS), axis=0, reverse=True)
    nxt = jnp.concatenate([later[1:], jnp.full((1,), N_EXPERTS, jnp.int32)])
    nxt = jnp.where(nxt < N_EXPERTS, nxt, -1)
    return blk_e, first, nxt[blk_e], nvalid.reshape(1)


def kernel(x, w_in, w_gk_up, b_gk, conv_w, a_log, dt_bias, gla_norm_g, gdn_norm_g, w_out, ln1_g, ln1_b, w_router_group, b_router_group, w_router_expert, b_router_expert, w_gate, w_up, w_down, ln2_g, ln2_b):
    bsz, seq, d = x.shape
    n = bsz * seq
    x2d = x.reshape(n, d)
    w_main, w_aux = _split_w_in(w_in[0])
    main, aux = _in_proj(x2d, w_main, w_aux, conv_w[0].astype(F32), seq=seq)
    wgk = jnp.zeros((AUX_W, 256), F32).at[:GLA_GATE_RANK].set(w_gk_up[0])
    o_a = _gla(main, aux, wgk, b_gk[0][None], gla_norm_g[0][None], bsz=bsz, seq=seq)
    o_b = _gdn(main, aux, _pad_aux_row(a_log[0], AUX_AB), _pad_aux_row(dt_bias[0], AUX_AB),
               gdn_norm_g[0][None], bsz=bsz, seq=seq)
    rpad = jnp.zeros((d, ROUTE_W - N_EXPERTS - N_GROUPS), F32)
    w_router = jnp.concatenate([w_router_expert[0], w_router_group[0], rpad], axis=1)
    w_router_hi = w_router.astype(BF16)
    w_router = jnp.concatenate([w_router_hi, (w_router - w_router_hi.astype(F32)).astype(BF16)], axis=1)
    b_router = jnp.concatenate([b_router_expert[0], b_router_group[0], rpad[0]])[None]
    x1p, route, counts = _post_mix(o_a, o_b, x2d, w_out[0].astype(BF16), ln1_g[0][None], ln1_b[0][None],
                                       w_router, b_router)
    dest = _plan(route, counts, tb=EXPERT_BLOCK)
    blk_e, first, next_e, nvalid = _block_plan(counts, n, EXPERT_BLOCK)
    xs = _dispatch(dest[0], dest[1], x1p, blk_e.shape[0] * EXPERT_BLOCK)
    ybuf = _experts(blk_e, first, next_e, nvalid, xs, w_gate[0], w_up[0], w_down[0])
    out = _combine(dest[0], dest[1], ybuf, x1p, route, ln2_g[0][None], ln2_b[0][None])
    return out.reshape(bsz, seq, d)
```

```python
import functools

import jax
import jax.numpy as jnp
import numpy as np
from jax import lax
from jax.experimental import pallas as pl
from jax.experimental.pallas import tpu as pltpu

F32 = jnp.float32
BF16 = jnp.bfloat16
HIGHEST = lax.Precision.HIGHEST

D_MODEL = 1024
DEPTH = 1
GLA_HEADS = 4
GLA_DK = 64
GLA_DV = 128
GLA_WIDTH = GLA_HEADS * GLA_DV
GLA_GATE_RANK = 16
GLA_GATE_NORM = 16.0
GDN_HEADS = 4
GDN_DK = 128
GDN_DV = 128
GDN_WIDTH = GDN_HEADS * GDN_DV
CONV_W = 4
CHUNK = 64
N_GROUPS = 8
EXPERTS_PER_GROUP = 8
N_EXPERTS = N_GROUPS * EXPERTS_PER_GROUP
TOP_K = 2
D_EXPERT = 512
LN_EPS = 1e-5
RMS_EPS = 1e-6
ALPHA_DN = (2.0 * DEPTH) ** 0.25

MAIN_W = 2 * GLA_HEADS * GLA_DK + 2 * GLA_WIDTH + 4 * GDN_WIDTH
AUX_W = 128
AUX_LRA = 0
AUX_BB = GLA_GATE_RANK
AUX_AB = GLA_GATE_RANK + GDN_HEADS
SUB = 16
VMEM_LIMIT = 56 * 1024 * 1024


def _cparams(*sem):
    return pltpu.CompilerParams(dimension_semantics=sem, vmem_limit_bytes=VMEM_LIMIT)


def _sigmoid(x):
    return 1.0 / (1.0 + jnp.exp(-x))


def _silu(x):
    return x * _sigmoid(x)


def _log_sigmoid(x):
    return jnp.minimum(x, 0.0) - jnp.log(1.0 + jnp.exp(-jnp.abs(x)))


def _softplus(x):
    return jnp.maximum(x, 0.0) + jnp.log(1.0 + jnp.exp(-jnp.abs(x)))


def _chunk_cumsum(x, c):
    pos = lax.broadcasted_iota(jnp.int32, x.shape, 0) % c
    shift = 1
    while shift < c:
        x = x + jnp.where(pos >= shift, pltpu.roll(x, shift, 0), 0.0)
        shift *= 2
    return x


CONV_PAD = 8
GDN_HW = GDN_HEADS * GDN_DK
GDN_QKV_BLOCK0 = (2 * GLA_HEADS * GLA_DK + 2 * GLA_WIDTH) // GDN_HW


def _in_proj_kernel(x_ref, wm_ref, wa_ref, cw_ref, main_ref, aux_ref, stage_ref, *, tm, tiles_per_seq):
    xb = x_ref[...].astype(BF16)

    @pl.when((pl.program_id(0) % tiles_per_seq) == 0)
    def _():
        stage_ref[:, 0:CONV_PAD, :] = jnp.zeros((3 * GDN_HEADS, CONV_PAD, GDN_DK), F32)

    for j in range(MAIN_W // GDN_HW):
        sl = slice(j * GDN_HW, (j + 1) * GDN_HW)
        r = jnp.dot(xb, wm_ref[:, sl], preferred_element_type=F32)
        b = j - GDN_QKV_BLOCK0
        if not 0 <= b < 3:
            main_ref[:, sl] = r.astype(BF16)
            continue
        for h in range(GDN_HEADS):
            st = stage_ref.at[b * GDN_HEADS + h]
            cols = slice(b * GDN_HW + h * GDN_DK, b * GDN_HW + (h + 1) * GDN_DK)
            st[CONV_PAD:, :] = r[:, h * GDN_DK:(h + 1) * GDN_DK]
            acc = jnp.zeros((tm, GDN_DK), F32)
            for i in range(CONV_W):
                lo = CONV_PAD - (CONV_W - 1) + i
                acc = acc + st[lo:lo + tm, :] * cw_ref[i:i + 1, cols]
            st[0:CONV_PAD, :] = st[tm:tm + CONV_PAD, :]
            yh = _silu(acc)
            if b < 2:
                yh = yh * lax.rsqrt(jnp.sum(yh * yh, axis=-1, keepdims=True) + RMS_EPS)
            if b == 0:
                yh = yh * (GDN_DK ** -0.5)
            main_ref[:, j * GDN_HW + h * GDN_DK:j * GDN_HW + (h + 1) * GDN_DK] = yh.astype(BF16)
    aux_ref[...] = jnp.dot(xb, wa_ref[...], preferred_element_type=F32)


def _in_proj(x2d, w_main, w_aux, conv_w, *, seq, tm=512):
    n = x2d.shape[0]
    return pl.pallas_call(
        functools.partial(_in_proj_kernel, tm=tm, tiles_per_seq=seq // tm),
        grid=(n // tm,),
        in_specs=[pl.BlockSpec((tm, D_MODEL), lambda i: (i, 0)),
                  pl.BlockSpec((D_MODEL, MAIN_W), lambda i: (0, 0)),
                  pl.BlockSpec((D_MODEL, AUX_W), lambda i: (0, 0)),
                  pl.BlockSpec((CONV_W, 3 * GDN_HW), lambda i: (0, 0))],
        out_specs=[pl.BlockSpec((tm, MAIN_W), lambda i: (i, 0)),
                   pl.BlockSpec((tm, AUX_W), lambda i: (i, 0))],
        out_shape=[jax.ShapeDtypeStruct((n, MAIN_W), BF16),
                   jax.ShapeDtypeStruct((n, AUX_W), F32)],
        scratch_shapes=[pltpu.VMEM((3 * GDN_HEADS, tm + CONV_PAD, GDN_DK), F32)],
        compiler_params=_cparams("arbitrary"),
        name="in_proj",
    )(x2d, w_main, w_aux, conv_w)


def _split_w_in(w_in):
    sizes = (GLA_HEADS * GLA_DK, GLA_HEADS * GLA_DK, GLA_WIDTH, GLA_WIDTH, GLA_GATE_RANK,
             GDN_HEADS * GDN_DK, GDN_HEADS * GDN_DK, GDN_WIDTH, GDN_WIDTH, GDN_HEADS, GDN_HEADS)
    offs = np.cumsum((0,) + sizes)
    seg = [w_in[:, offs[i]:offs[i + 1]] for i in range(len(sizes))]
    qa, ka, va, ra, lra, qb, kb, vb, zb, bb, ab = seg
    w_main = jnp.concatenate([qa, ka, va, ra, qb, kb, vb, zb], axis=1).astype(BF16)
    pad = jnp.zeros((w_in.shape[0], AUX_W - GLA_GATE_RANK - 2 * GDN_HEADS), w_in.dtype)
    w_aux = jnp.concatenate([lra, bb, ab, pad], axis=1).astype(BF16)
    return w_main, w_aux


_NT = (((1,), (1,)), ((), ()))
_TN = (((0,), (0,)), ((), ()))


def _gla_kernel(q_ref, k_ref, v_ref, r_ref, aux_ref, wgk_ref, bgk_ref, g_ref, o_ref, st_ref, *, ts):
    c = CHUNK
    nsub = c // SUB

    @pl.when(pl.program_id(1) == 0)
    def _():
        st_ref[...] = jnp.zeros_like(st_ref)

    pre = jnp.dot(aux_ref[...], wgk_ref[...], precision=HIGHEST, preferred_element_type=F32) + bgk_ref[...]
    gk_all = _log_sigmoid(pre) * (1.0 / GLA_GATE_NORM)

    row128 = lax.broadcasted_iota(jnp.int32, (c, 128), 0)
    lane128 = lax.broadcasted_iota(jnp.int32, (1, 128), 1)
    lane_masks = [(lane128 // GLA_DK) == hh for hh in range(2)]
    sub_valid = [row128 < SUB * (i + 1) for i in range(nsub)]
    ar = lax.broadcasted_iota(jnp.int32, (c, nsub * c), 0)
    ac = lax.broadcasted_iota(jnp.int32, (c, nsub * c), 1)
    amask = ((ac // c) == (ar // SUB)) & ((ac % c) <= ar)
    scale = GLA_DK ** -0.5
    gnorm = g_ref[...]

    nch = ts // c
    rows = [slice(ch * c, (ch + 1) * c) for ch in range(nch)]
    b_cum = _chunk_cumsum(gk_all, c)
    b_all = [b_cum[r] for r in rows]

    cp = [(ch, p) for ch in range(nch) for p in range(2)]
    qt, qd, kd, kstack, dlast = {}, {}, {}, {}, {}
    for ch, p in cp:
        lanes = slice(128 * p, 128 * (p + 1))
        bp = b_all[ch][:, lanes]
        qp = q_ref[rows[ch], lanes].astype(F32) * scale
        kp = k_ref[rows[ch], lanes].astype(F32)
        c_row = jnp.concatenate(
            [jnp.broadcast_to(bp[SUB * i:SUB * i + 1, :], (SUB, 128)) for i in range(nsub)], axis=0)
        qt[ch, p] = (qp * jnp.exp(bp - c_row)).astype(BF16)
        qd[ch, p] = qp * jnp.exp(bp)
        b_last = bp[c - 1:c, :]
        dlast[ch, p] = jnp.exp(b_last)
        kd[ch, p] = kp * jnp.exp(b_last - bp)
        kts = []
        for i in range(nsub):
            e = jnp.exp(jnp.where(sub_valid[i], bp[SUB * i:SUB * i + 1, :] - bp, 0.0))
            kts.append(jnp.where(sub_valid[i], kp * e, 0.0))
        kstack[ch, p] = jnp.concatenate(kts, axis=0)

    cph = [(ch, p, hh) for ch, p in cp for hh in range(2)]
    vh = {(ch, p, hh): v_ref[rows[ch], 128 * (2 * p + hh):128 * (2 * p + hh + 1)] for ch, p, hh in cph}
    r_mat = {(ch, p, hh): lax.dot_general(qt[ch, p], jnp.where(lane_masks[hh], kstack[ch, p], 0.0).astype(BF16),
                                          _NT, preferred_element_type=F32) for ch, p, hh in cph}
    o_intra = {k: jnp.dot(jnp.where(amask, r_mat[k], 0.0).astype(BF16), jnp.concatenate([vh[k]] * nsub, axis=0),
                          preferred_element_type=F32) for k in cph}
    upd = {(ch, p, hh): lax.dot_general(vh[ch, p, hh], jnp.where(lane_masks[hh], kd[ch, p], 0.0).astype(BF16), _TN,
                                        preferred_element_type=F32) for ch, p, hh in cph}

    st = [st_ref[p] for p in range(2)]
    for ch in range(nch):
        st_b = [s.astype(BF16) for s in st]
        o_pair = [lax.dot_general(
            jnp.concatenate([jnp.where(lane_masks[hh], qd[ch, p], 0.0).astype(BF16) for hh in range(2)], axis=0),
            st_b[p], _NT, preferred_element_type=F32) for p in range(2)]
        o_inter = {(p, hh): o_pair[p][hh * c:(hh + 1) * c] for p in range(2) for hh in range(2)}
        st = [st[p] * dlast[ch, p] + upd[ch, p, 0] + upd[ch, p, 1] for p in range(2)]
        for p in range(2):
            for hh in range(2):
                h = 2 * p + hh
                o = o_intra[ch, p, hh] + o_inter[p, hh]
                o = o * lax.rsqrt(jnp.mean(o * o, axis=-1, keepdims=True) + RMS_EPS) * gnorm
                gate = _silu(r_ref[rows[ch], 128 * h:128 * (h + 1)].astype(F32))
                o_ref[rows[ch], 128 * h:128 * (h + 1)] = (o * gate).astype(BF16)
    for p in range(2):
        st_ref[p] = st[p]


def _gla(main, aux, wgk_pad, b_gk, gla_norm_g, *, bsz, seq, ts=512):
    nt = seq // ts
    row = lambda b, t: b * nt + t
    return pl.pallas_call(
        functools.partial(_gla_kernel, ts=ts),
        grid=(bsz, nt),
        in_specs=[pl.BlockSpec((ts, 256), lambda b, t: (row(b, t), 0)),
                  pl.BlockSpec((ts, 256), lambda b, t: (row(b, t), 1)),
                  pl.BlockSpec((ts, 512), lambda b, t: (row(b, t), 1)),
                  pl.BlockSpec((ts, 512), lambda b, t: (row(b, t), 2)),
                  pl.BlockSpec((ts, AUX_W), lambda b, t: (row(b, t), 0)),
                  pl.BlockSpec((AUX_W, 256), lambda b, t: (0, 0)),
                  pl.BlockSpec((1, 256), lambda b, t: (0, 0)),
                  pl.BlockSpec((1, 128), lambda b, t: (0, 0))],
        out_specs=pl.BlockSpec((ts, GLA_WIDTH), lambda b, t: (row(b, t), 0)),
        out_shape=jax.ShapeDtypeStruct((bsz * seq, GLA_WIDTH), BF16),
        scratch_shapes=[pltpu.VMEM((2, 128, 128), F32)],
        compiler_params=_cparams("parallel", "arbitrary"),
        name="gla",
    )(main, main, main, main, aux, wgk_pad, b_gk, gla_norm_g)


def _gdn_kernel(q_ref, k_ref, v_ref, z_ref, aux_ref, alog_ref, dtb_ref, g_ref, o_ref, st_ref, *, ts):
    c = CHUNK
    nch = ts // c

    @pl.when(pl.program_id(1) == 0)
    def _():
        st_ref[...] = jnp.zeros_like(st_ref)

    aux = aux_ref[...]
    beta_full = _sigmoid(aux)
    g_full = -jnp.exp(alog_ref[...]) * _softplus(aux + dtb_ref[...])
    gam_full = _chunk_cumsum(g_full, c)
    gam_t = gam_full.T
    gnorm = g_ref[...]

    row = lax.broadcasted_iota(jnp.int32, (c, 2 * c), 0)
    lane = lax.broadcasted_iota(jnp.int32, (c, 2 * c), 1)
    left = lane < c
    col = lane % c
    causal = col <= row
    strict = col < row
    same32 = (row // 32) == (col // 32)
    same16 = (row // 16) == (col // 16)
    lvl0 = same16 & strict
    lvl1 = same32 & jnp.logical_not(same16) & strict
    lvl2 = jnp.logical_not(same32) & strict

    def pack_rows(x2):
        return jnp.where(left, x2[0:c], x2[c:2 * c])

    def pack_cols(x1):
        return jnp.where(left, x1[0:c], x1[c:2 * c])

    def halves(x):
        return jnp.where(left, x, 0.0).astype(BF16), jnp.where(left, 0.0, x).astype(BF16)

    def mmp(x, y):
        return jnp.dot(x.astype(BF16), jnp.concatenate(halves(y), axis=0), preferred_element_type=F32)

    heads = range(GDN_HEADS)
    npair = nch // 2
    items = [(h, pr) for h in heads for pr in range(npair)]
    prow = [slice(2 * c * pr, 2 * c * (pr + 1)) for pr in range(npair)]

    qn, kn, beta, gam, gam_r, kb, rhs, rhs_b, qd, kd = ([None] * GDN_HEADS for _ in range(10))
    for h in heads:
        hl = slice(128 * h, 128 * (h + 1))
        kb[h] = k_ref[:, hl]
        qn[h] = q_ref[:, hl].astype(F32)
        kn[h] = kb[h].astype(F32)
        vh = v_ref[:, hl].astype(F32)
        beta[h] = beta_full[:, AUX_BB + h:AUX_BB + h + 1]
        gam[h] = gam_full[:, AUX_AB + h:AUX_AB + h + 1]
        gam_r[h] = gam_t[AUX_AB + h:AUX_AB + h + 1, :]
        egam = jnp.exp(gam[h])
        rhs[h] = jnp.concatenate([beta[h] * vh, beta[h] * egam * kn[h]], axis=1)
        rhs_b[h] = rhs[h].astype(BF16)
        qd[h] = (qn[h] * egam).astype(BF16)
        gl_rows = jnp.concatenate(
            [jnp.broadcast_to(gam[h][c * (j + 1) - 1:c * (j + 1), :], (c, 1)) for j in range(nch)], axis=0)
        kd[h] = (kn[h] * jnp.exp(gl_rows - gam[h])).astype(BF16)

    kk = [pack_rows(lax.dot_general(kb[h][prow[pr]], kb[h][prow[pr]], _NT, preferred_element_type=F32))
          for h, pr in items]
    qk = [pack_rows(lax.dot_general(qn[h][prow[pr]].astype(BF16), kb[h][prow[pr]], _NT,
                                    preferred_element_type=F32)) for h, pr in items]
    dec = [jnp.exp(jnp.where(causal, pack_cols(gam[h][prow[pr]]) - gam_r[h][:, prow[pr]], 0.0)) for h, pr in items]
    a = [pack_cols(beta[h][prow[pr]]) * kk[i] * dec[i] for i, (h, pr) in enumerate(items)]
    attn = [halves(jnp.where(causal, qk[i] * dec[i], 0.0)) for i in range(len(items))]

    x1 = [jnp.where(lvl0, -ai, 0.0) for ai in a]
    x2 = [mmp(x, x) for x in x1]
    y = [mmp(x, xx) for x, xx in zip(x1, x2)]
    p = [x + xx + yy for x, xx, yy in zip(x1, x2, y)]
    xk = x2
    for _ in range(2):
        xk = [mmp(x, x) for x in xk]
        y = [mmp(pp, x) for pp, x in zip(p, xk)]
        p = [pp + x + yy for pp, x, yy in zip(p, xk, y)]
    for lvl in (lvl1, lvl2):
        low = [jnp.where(lvl, ai, 0.0) for ai in a]
        y = [mmp(pp, lo) for pp, lo in zip(p, low)]
        zz = [lo + yy for lo, yy in zip(low, y)]
        y = [mmp(z, pp) for z, pp in zip(zz, p)]
        p = [pp - z - yy for pp, z, yy in zip(p, zz, y)]

    us = [[None] * nch for _ in heads]
    ws = [[None] * nch for _ in heads]
    for i, (h, pr) in enumerate(items):
        for half, p_half in enumerate(halves(p[i])):
            j = 2 * pr + half
            uw = rhs[h][c * j:c * (j + 1)] + jnp.dot(p_half, rhs_b[h][prow[pr]], preferred_element_type=F32)
            us[h][j] = uw[:, :GDN_DV]
            ws[h][j] = uw[:, GDN_DV:].astype(BF16)

    s = [st_ref[h] for h in heads]
    vprev = [None] * GDN_HEADS
    for j in range(nch):
        rows = slice(c * j, c * (j + 1))
        sb = [s[h].astype(BF16) for h in heads]
        ws_qd = [jnp.dot(jnp.concatenate([ws[h][j], qd[h][rows]], axis=0), sb[h], preferred_element_type=F32)
                 for h in heads]
        vn16 = [(us[h][j] - ws_qd[h][0:c]).astype(BF16) for h in heads]
        o_inter = [ws_qd[h][c:2 * c] for h in heads]
        for h in heads:
            vpair = jnp.concatenate([vn16[h], jnp.zeros_like(vn16[h])] if j % 2 == 0 else [vprev[h], vn16[h]], axis=0)
            o = o_inter[h] + jnp.dot(attn[h * npair + j // 2][j % 2], vpair, preferred_element_type=F32)
            s[h] = jnp.exp(gam[h][c * (j + 1) - 1:c * (j + 1), :]) * s[h] + lax.dot_general(
                kd[h][rows], vn16[h], _TN, preferred_element_type=F32)
            o = o * lax.rsqrt(jnp.mean(o * o, axis=-1, keepdims=True) + RMS_EPS) * gnorm
            gate = _silu(z_ref[rows, 128 * h:128 * (h + 1)].astype(F32))
            o_ref[rows, 128 * h:128 * (h + 1)] = (o * gate).astype(BF16)
        vprev = vn16
    for h in heads:
        st_ref[h] = s[h]


def _gdn(main, aux, alog_pad, dtb_pad, gdn_norm_g, *, bsz, seq, ts=512):
    nt = seq // ts
    row = lambda b, t: b * nt + t
    hw = GDN_HEADS * GDN_DK
    return pl.pallas_call(
        functools.partial(_gdn_kernel, ts=ts),
        grid=(bsz, nt),
        in_specs=[pl.BlockSpec((ts, hw), lambda b, t: (row(b, t), 3)),
                  pl.BlockSpec((ts, hw), lambda b, t: (row(b, t), 4)),
                  pl.BlockSpec((ts, hw), lambda b, t: (row(b, t), 5)),
                  pl.BlockSpec((ts, hw), lambda b, t: (row(b, t), 6)),
                  pl.BlockSpec((ts, AUX_W), lambda b, t: (row(b, t), 0)),
                  pl.BlockSpec((1, AUX_W), lambda b, t: (0, 0)),
                  pl.BlockSpec((1, AUX_W), lambda b, t: (0, 0)),
                  pl.BlockSpec((1, GDN_DV), lambda b, t: (0, 0))],
        out_specs=pl.BlockSpec((ts, GDN_WIDTH), lambda b, t: (row(b, t), 0)),
        out_shape=jax.ShapeDtypeStruct((bsz * seq, GDN_WIDTH), BF16),
        scratch_shapes=[pltpu.VMEM((GDN_HEADS, GDN_DK, GDN_DV), F32)],
        compiler_params=_cparams("parallel", "arbitrary"),
        name="gdn",
    )(main, main, main, main, aux, alog_pad, dtb_pad, gdn_norm_g)


def _pad_aux_row(v, offset):
    return jnp.zeros((1, AUX_W), F32).at[0, offset:offset + v.shape[0]].set(v.astype(F32))


ROUTE_W = 128
R_E1, R_E2, R_RANK1, R_RANK2, R_G1, R_G2 = range(6)
ROUTER_GROUP_COL = N_EXPERTS


def _layer_norm(h, g, b):
    mu = jnp.mean(h, axis=-1, keepdims=True)
    hc = h - mu
    var = jnp.mean(hc * hc, axis=-1, keepdims=True)
    return hc * lax.rsqrt(var + LN_EPS) * g + b


def _pack_bf16_pairs(x):
    w = x.shape[1] // 2
    lo = lax.bitcast_convert_type(x[:, :w].astype(BF16).astype(F32), jnp.uint32)
    hi = lax.bitcast_convert_type(x[:, w:].astype(BF16).astype(F32), jnp.uint32)
    return (lo >> 16) | (hi & jnp.uint32(0xFFFF0000))


def _unpack_bf16_pairs(p):
    lo = lax.bitcast_convert_type(p << 16, F32)
    hi = lax.bitcast_convert_type(p & jnp.uint32(0xFFFF0000), F32)
    return lo, hi


HALF = D_MODEL // 2
TOK_ROWS = HALF // 128


def _store_token_rows(ref, packed):
    m = packed.shape[0]
    for c in range(TOK_ROWS):
        ref[pl.ds(c, m, stride=TOK_ROWS), :] = packed[:, 128 * c:128 * (c + 1)]


def _load_token_rows(ref, m):
    return jnp.concatenate([ref[pl.ds(c, m, stride=TOK_ROWS), :] for c in range(TOK_ROWS)], axis=1)


def _post_mix_kernel(oa_ref, ob_ref, x_ref, wo_ref, g_ref, b_ref, wr_ref, br_ref,
                     x1p_ref, route_ref, cnt_ref, carry_ref, *, tm):
    @pl.when(pl.program_id(0) == 0)
    def _():
        carry_ref[...] = jnp.zeros_like(carry_ref)

    y = jnp.dot(oa_ref[...], wo_ref[0:GLA_WIDTH, :], preferred_element_type=F32)
    y = y + jnp.dot(ob_ref[...], wo_ref[GLA_WIDTH:, :], preferred_element_type=F32)
    x1 = _layer_norm(ALPHA_DN * x_ref[...] + y, g_ref[...], b_ref[...])
    _store_token_rows(x1p_ref, _pack_bf16_pairs(x1))

    xh = x1.astype(BF16)
    xl = (x1 - xh.astype(F32)).astype(BF16)
    hw = jnp.dot(xh, wr_ref[...], preferred_element_type=F32)
    logits = (hw[:, :ROUTE_W] + hw[:, ROUTE_W:]
              + jnp.dot(xl, wr_ref[:, :ROUTE_W], preferred_element_type=F32) + br_ref[...])
    lane = lax.broadcasted_iota(jnp.int32, (tm, ROUTE_W), 1)
    big = jnp.int32(1 << 20)
    neg = jnp.float32(-jnp.inf)

    def first_argmax(vals):
        m = jnp.max(vals, axis=-1, keepdims=True)
        idx = jnp.min(jnp.where(vals == m, lane, big), axis=-1, keepdims=True)
        return m, idx

    is_group = (lane >= ROUTER_GROUP_COL) & (lane < ROUTER_GROUP_COL + N_GROUPS)
    gl = jnp.where(is_group, logits, neg)
    gmax, gidx = first_argmax(gl)
    p_group = 1.0 / jnp.sum(jnp.exp(gl - gmax), axis=-1, keepdims=True)
    g_sel = gidx - ROUTER_GROUP_COL
    el = jnp.where((lane // EXPERTS_PER_GROUP) == g_sel, logits, neg)
    t1, e1 = first_argmax(el)
    t2, e2 = first_argmax(jnp.where(lane == e1, neg, el))
    ex = jnp.exp(t2 - t1)
    w1 = 1.0 / (1.0 + ex)
    gate1 = p_group * w1
    gate2 = p_group * (ex * w1)

    oh1 = lane == e1
    oh2 = lane == e2
    ri = lax.broadcasted_iota(jnp.int32, (tm, tm), 0)
    ci = lax.broadcasted_iota(jnp.int32, (tm, tm), 1)
    lstrict = (ci < ri).astype(BF16)
    oh1f = oh1.astype(F32)
    oh2f = oh2.astype(F32)
    c1 = jnp.dot(lstrict, oh1f.astype(BF16), preferred_element_type=F32)
    c2 = jnp.dot(lstrict, oh2f.astype(BF16), preferred_element_type=F32)
    carry = carry_ref[...]
    tot1 = jnp.sum(oh1f, axis=0, keepdims=True)
    tot2 = jnp.sum(oh2f, axis=0, keepdims=True)
    rank1 = jnp.sum(jnp.where(oh1, c1 + carry, 0.0), axis=-1, keepdims=True)
    rank2 = jnp.sum(jnp.where(oh2, c2 + (carry + tot1), 0.0), axis=-1, keepdims=True)
    new_carry = carry + tot1 + tot2
    carry_ref[...] = new_carry
    cnt_ref[...] = new_carry

    cols = (e1.astype(F32), e2.astype(F32), rank1, rank2, gate1, gate2)
    route = jnp.zeros((tm, ROUTE_W), F32)
    for j, col in enumerate(cols):
        route = jnp.where(lane == j, col, route)
    route_ref[...] = route


def _post_mix(o_a, o_b, x2d, w_out_b, ln_g, ln_b, w_router, b_router, *, tm=512):
    n = x2d.shape[0]
    return pl.pallas_call(
        functools.partial(_post_mix_kernel, tm=tm),
        grid=(n // tm,),
        in_specs=[pl.BlockSpec((tm, GLA_WIDTH), lambda i: (i, 0)),
                  pl.BlockSpec((tm, GDN_WIDTH), lambda i: (i, 0)),
                  pl.BlockSpec((tm, D_MODEL), lambda i: (i, 0)),
                  pl.BlockSpec((GLA_WIDTH + GDN_WIDTH, D_MODEL), lambda i: (0, 0)),
                  pl.BlockSpec((1, D_MODEL), lambda i: (0, 0)),
                  pl.BlockSpec((1, D_MODEL), lambda i: (0, 0)),
                  pl.BlockSpec((D_MODEL, 2 * ROUTE_W), lambda i: (0, 0)),
                  pl.BlockSpec((1, ROUTE_W), lambda i: (0, 0))],
        out_specs=[pl.BlockSpec((tm * TOK_ROWS, 128), lambda i: (i, 0)),
                   pl.BlockSpec((tm, ROUTE_W), lambda i: (i, 0)),
                   pl.BlockSpec((1, ROUTE_W), lambda i: (0, 0))],
        out_shape=[jax.ShapeDtypeStruct((n * TOK_ROWS, 128), jnp.uint32),
                   jax.ShapeDtypeStruct((n, ROUTE_W), F32),
                   jax.ShapeDtypeStruct((1, ROUTE_W), F32)],
        scratch_shapes=[pltpu.VMEM((1, ROUTE_W), F32)],
        compiler_params=_cparams("arbitrary"),
        name="post_mix",
    )(o_a, o_b, x2d, w_out_b, ln_g, ln_b, w_router, b_router)


EXPERT_BLOCK = 256


ROW_UNROLL = 8


def _plan_kernel(route_ref, cnt_ref, dest_ref, *, tm, tb):
    counts = cnt_ref[...].astype(jnp.int32)
    padded = ((counts + (tb - 1)) & jnp.int32(-tb)).astype(F32)
    ri = lax.broadcasted_iota(jnp.int32, (ROUTE_W, ROUTE_W), 0)
    ci = lax.broadcasted_iota(jnp.int32, (ROUTE_W, ROUTE_W), 1)
    pstart = jnp.dot(jnp.broadcast_to(padded, (8, ROUTE_W)), (ri < ci).astype(F32),
                     precision=HIGHEST, preferred_element_type=F32)[0:1, :]
    route = route_ref[...]
    lane = lax.broadcasted_iota(jnp.int32, (tm, ROUTE_W), 1).astype(F32)
    out = jnp.zeros((tm, ROUTE_W), F32)
    for k, (ce, cr) in enumerate(((R_E1, R_RANK1), (R_E2, R_RANK2))):
        off = jnp.sum(jnp.where(lane == route[:, ce:ce + 1], pstart, 0.0), axis=-1, keepdims=True)
        out = jnp.where(lane == float(k), off + route[:, cr:cr + 1], out)
    dest_ref[...] = out.T[0:8, :].astype(jnp.int32) * TOK_ROWS


def _plan(route, counts, *, tb, tm=2048):
    n = route.shape[0]
    tm = min(tm, n)
    return pl.pallas_call(
        functools.partial(_plan_kernel, tm=tm, tb=tb),
        grid=(n // tm,),
        in_specs=[pl.BlockSpec((tm, ROUTE_W), lambda i: (i, 0)),
                  pl.BlockSpec((1, ROUTE_W), lambda i: (0, 0))],
        out_specs=pl.BlockSpec((8, tm), lambda i: (0, i)),
        out_shape=jax.ShapeDtypeStruct((8, n), jnp.int32),
        compiler_params=_cparams("parallel"),
        name="plan",
    )(route, counts)


def _dispatch_kernel(d1_ref, d2_ref, xp_ref, xs_init_ref, xs_ref, sem, *, tm):
    del xs_init_ref
    base = pl.program_id(0) * tm

    def body(j, carry):
        for u in range(ROW_UNROLL):
            r = j * ROW_UNROLL + u
            for k, d_ref in enumerate((d1_ref, d2_ref)):
                dst = pl.multiple_of(d_ref[base + r], TOK_ROWS)
                pltpu.make_async_copy(xp_ref.at[pl.ds(r * TOK_ROWS, TOK_ROWS), :],
                                      xs_ref.at[pl.ds(dst, TOK_ROWS), :], sem).start(priority=k)
        return carry
    lax.fori_loop(0, tm // ROW_UNROLL, body, 0)
    for _ in range(TOP_K):
        pltpu.make_async_copy(xp_ref, xs_ref.at[pl.ds(0, tm * TOK_ROWS), :], sem).wait()


def _dispatch(dest1, dest2, x1p, n_rows, *, tm=1024):
    n = x1p.shape[0] // TOK_ROWS
    grid_spec = pltpu.PrefetchScalarGridSpec(
        num_scalar_prefetch=2,
        grid=(n // tm,),
        in_specs=[pl.BlockSpec((tm * TOK_ROWS, 128), lambda i, a, b: (i, 0)),
                  pl.BlockSpec(memory_space=pl.ANY)],
        out_specs=pl.BlockSpec(memory_space=pl.ANY),
        scratch_shapes=[pltpu.SemaphoreType.DMA(())],
    )
    return pl.pallas_call(
        functools.partial(_dispatch_kernel, tm=tm),
        grid_spec=grid_spec,
        out_shape=jax.ShapeDtypeStruct((n_rows * TOK_ROWS, 128), jnp.uint32),
        input_output_aliases={3: 0},
        compiler_params=_cparams("arbitrary"),
        name="dispatch",
    )(dest1, dest2, x1p, jnp.zeros((n_rows * TOK_ROWS, 128), jnp.uint32))


def _expert_kernel(blk_e_ref, first_ref, next_e_ref, nvalid_ref, xs_ref, wg_hbm, wu_hbm, wd_hbm, y_ref,
                   wgs, wus, wds, wgb, wub, wdb, sem, *, tb):
    i = pl.program_id(0)

    def weight_copies(e):
        return [pltpu.make_async_copy(src.at[e], dst, sem.at[j])
                for j, (src, dst) in enumerate(((wg_hbm, wgs), (wu_hbm, wus), (wd_hbm, wds)))]

    @pl.when((i == 0) & (nvalid_ref[0] > 0))
    def _():
        for cp in weight_copies(blk_e_ref[0]):
            cp.start()

    @pl.when(i < nvalid_ref[0])
    def _():
        @pl.when(first_ref[i] == 1)
        def _():
            for cp in weight_copies(blk_e_ref[i]):
                cp.wait()
            wgb[...] = wgs[...].astype(BF16)
            wub[...] = wus[...].astype(BF16)
            wdb[...] = wds[...].astype(BF16)

            @pl.when(next_e_ref[i] >= 0)
            def _():
                for cp in weight_copies(next_e_ref[i]):
                    cp.start()

        lo, hi = _unpack_bf16_pairs(_load_token_rows(xs_ref, tb))
        xb = jnp.concatenate([lo.astype(BF16), hi.astype(BF16)], axis=1)
        hg = jnp.dot(xb, wgb[...], preferred_element_type=F32)
        hu = jnp.dot(xb, wub[...], preferred_element_type=F32)
        hid = (_silu(hg) * hu).astype(BF16)
        _store_token_rows(y_ref, _pack_bf16_pairs(jnp.dot(hid, wdb[...], preferred_element_type=F32)))

    @pl.when(i >= nvalid_ref[0])
    def _():
        y_ref[...] = jnp.zeros_like(y_ref)


def _experts(blk_e, first, next_e, nvalid, xs, w_gate, w_up, w_down, *, tb=EXPERT_BLOCK):
    nb = blk_e.shape[0]
    grid_spec = pltpu.PrefetchScalarGridSpec(
        num_scalar_prefetch=4,
        grid=(nb,),
        in_specs=[pl.BlockSpec((tb * TOK_ROWS, 128),
                               lambda i, be, fr, ne, nv: (jnp.clip(i, 0, jnp.maximum(nv[0] - 1, 0)), 0)),
                  pl.BlockSpec(memory_space=pl.ANY),
                  pl.BlockSpec(memory_space=pl.ANY),
                  pl.BlockSpec(memory_space=pl.ANY)],
        out_specs=pl.BlockSpec((tb * TOK_ROWS, 128), lambda i, be, fr, ne, nv: (i, 0)),
        scratch_shapes=[pltpu.VMEM((D_MODEL, D_EXPERT), F32),
                        pltpu.VMEM((D_MODEL, D_EXPERT), F32),
                        pltpu.VMEM((D_EXPERT, D_MODEL), F32),
                        pltpu.VMEM((D_MODEL, D_EXPERT), BF16),
                        pltpu.VMEM((D_MODEL, D_EXPERT), BF16),
                        pltpu.VMEM((D_EXPERT, D_MODEL), BF16),
                        pltpu.SemaphoreType.DMA((3,))],
    )
    return pl.pallas_call(
        functools.partial(_expert_kernel, tb=tb),
        grid_spec=grid_spec,
        out_shape=jax.ShapeDtypeStruct((nb * tb * TOK_ROWS, 128), jnp.uint32),
        compiler_params=_cparams("arbitrary"),
        name="experts",
    )(blk_e, first, next_e, nvalid, xs, w_gate, w_up, w_down)


def _combine_kernel(d1_ref, d2_ref, y_hbm, x1_ref, route_ref, g_ref, b_ref, o_ref, ybuf, sem, *, tt):
    i = pl.program_id(0)
    nsteps = pl.num_programs(0)
    slot = i % 2

    def issue(tile, slot_):
        def body(j, carry):
            for u in range(ROW_UNROLL):
                r = j * ROW_UNROLL + u
                for k, d_ref in enumerate((d1_ref, d2_ref)):
                    src = pl.multiple_of(d_ref[tile * tt + r], TOK_ROWS)
                    pltpu.make_async_copy(y_hbm.at[pl.ds(src, TOK_ROWS), :],
                                          ybuf.at[slot_, k, pl.ds(r * TOK_ROWS, TOK_ROWS), :],
                                          sem.at[slot_]).start(priority=k)
            return carry
        lax.fori_loop(0, tt // ROW_UNROLL, body, 0)

    @pl.when(i == 0)
    def _():
        issue(0, 0)

    @pl.when(i + 1 < nsteps)
    def _():
        issue(i + 1, 1 - slot)

    for k in range(TOP_K):
        pltpu.make_async_copy(y_hbm.at[pl.ds(0, tt * TOK_ROWS), :], ybuf.at[slot, k], sem.at[slot]).wait()
    route = route_ref[...]
    g1 = route[:, R_G1:R_G1 + 1]
    g2 = route[:, R_G2:R_G2 + 1]
    lo1, hi1 = _unpack_bf16_pairs(_load_token_rows(ybuf.at[slot, 0], tt))
    lo2, hi2 = _unpack_bf16_pairs(_load_token_rows(ybuf.at[slot, 1], tt))
    x_lo, x_hi = _unpack_bf16_pairs(_load_token_rows(x1_ref, tt))
    h_lo = ALPHA_DN * x_lo + (g1 * lo1 + g2 * lo2)
    h_hi = ALPHA_DN * x_hi + (g1 * hi1 + g2 * hi2)
    mu = (jnp.sum(h_lo, axis=-1, keepdims=True) + jnp.sum(h_hi, axis=-1, keepdims=True)) * (1.0 / D_MODEL)
    c_lo = h_lo - mu
    c_hi = h_hi - mu
    var = (jnp.sum(c_lo * c_lo, axis=-1, keepdims=True) + jnp.sum(c_hi * c_hi, axis=-1, keepdims=True)) * (1.0 / D_MODEL)
    inv = lax.rsqrt(var + LN_EPS)
    o_ref[:, 0:HALF] = c_lo * inv * g_ref[:, 0:HALF] + b_ref[:, 0:HALF]
    o_ref[:, HALF:] = c_hi * inv * g_ref[:, HALF:] + b_ref[:, HALF:]


def _combine(dest1, dest2, ybuf, x1p, route, ln_g, ln_b, *, tt=512):
    n = x1p.shape[0] // TOK_ROWS
    grid_spec = pltpu.PrefetchScalarGridSpec(
        num_scalar_prefetch=2,
        grid=(n // tt,),
        in_specs=[pl.BlockSpec(memory_space=pl.ANY),
                  pl.BlockSpec((tt * TOK_ROWS, 128), lambda i, a, b: (i, 0)),
                  pl.BlockSpec((tt, ROUTE_W), lambda i, a, b: (i, 0)),
                  pl.BlockSpec((1, D_MODEL), lambda i, a, b: (0, 0)),
                  pl.BlockSpec((1, D_MODEL), lambda i, a, b: (0, 0))],
        out_specs=pl.BlockSpec((tt, D_MODEL), lambda i, a, b: (i, 0)),
        scratch_shapes=[pltpu.VMEM((2, TOP_K, tt * TOK_ROWS, 128), jnp.uint32),
                        pltpu.SemaphoreType.DMA((2,))],
    )
    return pl.pallas_call(
        functools.partial(_combine_kernel, tt=tt),
        grid_spec=grid_spec,
        out_shape=jax.ShapeDtypeStruct((n, D_MODEL), F32),
        compiler_params=_cparams("arbitrary"),
        name="combine",
    )(dest1, dest2, ybuf, x1p, route, ln_g, ln_b)


def _block_plan(counts_row, n, tb):
    counts = counts_row[0, :N_EXPERTS].astype(jnp.int32)
    pends = jnp.cumsum((counts + tb - 1) // tb * tb)
    nb = (n * TOP_K) // tb + N_EXPERTS
    nvalid = (pends[-1] // tb).astype(jnp.int32)
    blk = jnp.arange(nb, dtype=jnp.int32)
    blk_e = jnp.sum((pends[None, :] <= (blk * tb)[:, None]).astype(jnp.int32), axis=1)
    blk_e = jnp.minimum(blk_e, N_EXPERTS - 1)
    blk_e = jnp.where(blk < nvalid, blk_e, blk_e[jnp.maximum(nvalid - 1, 0)])
    first = jnp.concatenate([jnp.ones((1,), jnp.int32), (blk_e[1:] != blk_e[:-1]).astype(jnp.int32)])
    ids = jnp.arange(N_EXPERTS, dtype=jnp.int32)
    later = lax.cummin(jnp.where(counts > 0, ids, N_EXPERTS), axis=0, reverse=True)
    nxt = jnp.concatenate([later[1:], jnp.full((1,), N_EXPERTS, jnp.int32)])
    nxt = jnp.where(nxt < N_EXPERTS, nxt, -1)
    return blk_e, first, nxt[blk_e], nvalid.reshape(1)


def kernel(x, w_in, w_gk_up, b_gk, conv_w, a_log, dt_bias, gla_norm_g, gdn_norm_g, w_out, ln1_g, ln1_b, w_router_group, b_router_group, w_router_expert, b_router_expert, w_gate, w_up, w_down, ln2_g, ln2_b):
    bsz, seq, d = x.shape
    n = bsz * seq
    x2d = x.reshape(n, d)
    w_main, w_aux = _split_w_in(w_in[0])
    main, aux = _in_proj(x2d, w_main, w_aux, conv_w[0].astype(F32), seq=seq)
    wgk = jnp.zeros((AUX_W, 256), F32).at[:GLA_GATE_RANK].set(w_gk_up[0])
    o_a = _gla(main, aux, wgk, b_gk[0][None], gla_norm_g[0][None], bsz=bsz, seq=seq)
    o_b = _gdn(main, aux, _pad_aux_row(a_log[0], AUX_AB), _pad_aux_row(dt_bias[0], AUX_AB),
               gdn_norm_g[0][None], bsz=bsz, seq=seq)
    rpad = jnp.zeros((d, ROUTE_W - N_EXPERTS - N_GROUPS), F32)
    w_router = jnp.concatenate([w_router_expert[0], w_router_group[0], rpad], axis=1)
    w_router_hi = w_router.astype(BF16)
    w_router = jnp.concatenate([w_router_hi, (w_router - w_router_hi.astype(F32)).astype(BF16)], axis=1)
    b_router = jnp.concatenate([b_router_expert[0], b_router_group[0], rpad[0]])[None]
    x1p, route, counts = _post_mix(o_a, o_b, x2d, w_out[0].astype(BF16), ln1_g[0][None], ln1_b[0][None],
                                       w_router, b_router)
    dest = _plan(route, counts, tb=EXPERT_BLOCK)
    blk_e, first, next_e, nvalid = _block_plan(counts, n, EXPERT_BLOCK)
    xs = _dispatch(dest[0], dest[1], x1p, blk_e.shape[0] * EXPERT_BLOCK)
    ybuf = _experts(blk_e, first, next_e, nvalid, xs, w_gate[0], w_up[0], w_down[0])
    out = _combine(dest[0], dest[1], ybuf, x1p, route, ln2_g[0][None], ln2_b[0][None])
    return out.reshape(bsz, seq, d)
```

```python
import functools

import jax
import jax.numpy as jnp
import numpy as np
from jax import lax
from jax.experimental import pallas as pl
from jax.experimental.pallas import tpu as pltpu

F32 = jnp.float32
BF16 = jnp.bfloat16
HIGHEST = lax.Precision.HIGHEST

D_MODEL = 1024
DEPTH = 1
GLA_HEADS = 4
GLA_DK = 64
GLA_DV = 128
GLA_WIDTH = GLA_HEADS * GLA_DV
GLA_GATE_RANK = 16
GLA_GATE_NORM = 16.0
GDN_HEADS = 4
GDN_DK = 128
GDN_DV = 128
GDN_WIDTH = GDN_HEADS * GDN_DV
CONV_W = 4
CHUNK = 64
N_GROUPS = 8
EXPERTS_PER_GROUP = 8
N_EXPERTS = N_GROUPS * EXPERTS_PER_GROUP
TOP_K = 2
D_EXPERT = 512
LN_EPS = 1e-5
RMS_EPS = 1e-6
ALPHA_DN = (2.0 * DEPTH) ** 0.25

MAIN_W = 2 * GLA_HEADS * GLA_DK + 2 * GLA_WIDTH + 4 * GDN_WIDTH
AUX_W = 128
AUX_LRA = 0
AUX_BB = GLA_GATE_RANK
AUX_AB = GLA_GATE_RANK + GDN_HEADS
SUB = 16
VMEM_LIMIT = 56 * 1024 * 1024


def _cparams(*sem):
    return pltpu.CompilerParams(dimension_semantics=sem, vmem_limit_bytes=VMEM_LIMIT)


def _sigmoid(x):
    return 1.0 / (1.0 + jnp.exp(-x))


def _silu(x):
    return x * _sigmoid(x)


def _log_sigmoid(x):
    return jnp.minimum(x, 0.0) - jnp.log(1.0 + jnp.exp(-jnp.abs(x)))


def _softplus(x):
    return jnp.maximum(x, 0.0) + jnp.log(1.0 + jnp.exp(-jnp.abs(x)))


def _chunk_cumsum(x, c):
    pos = lax.broadcasted_iota(jnp.int32, x.shape, 0) % c
    shift = 1
    while shift < c:
        x = x + jnp.where(pos >= shift, pltpu.roll(x, shift, 0), 0.0)
        shift *= 2
    return x


CONV_PAD = 8
GDN_HW = GDN_HEADS * GDN_DK
GDN_QKV_BLOCK0 = (2 * GLA_HEADS * GLA_DK + 2 * GLA_WIDTH) // GDN_HW


def _in_proj_kernel(x_ref, wm_ref, wa_ref, cw_ref, main_ref, aux_ref, stage_ref, *, tm, tiles_per_seq):
    xb = x_ref[...].astype(BF16)

    @pl.when((pl.program_id(0) % tiles_per_seq) == 0)
    def _():
        stage_ref[:, 0:CONV_PAD, :] = jnp.zeros((3 * GDN_HEADS, CONV_PAD, GDN_DK), F32)

    for j in range(MAIN_W // GDN_HW):
        sl = slice(j * GDN_HW, (j + 1) * GDN_HW)
        r = jnp.dot(xb, wm_ref[:, sl], preferred_element_type=F32)
        b = j - GDN_QKV_BLOCK0
        if not 0 <= b < 3:
            main_ref[:, sl] = r.astype(BF16)
            continue
        for h in range(GDN_HEADS):
            st = stage_ref.at[b * GDN_HEADS + h]
            cols = slice(b * GDN_HW + h * GDN_DK, b * GDN_HW + (h + 1) * GDN_DK)
            st[CONV_PAD:, :] = r[:, h * GDN_DK:(h + 1) * GDN_DK]
            acc = jnp.zeros((tm, GDN_DK), F32)
            for i in range(CONV_W):
                lo = CONV_PAD - (CONV_W - 1) + i
                acc = acc + st[lo:lo + tm, :] * cw_ref[i:i + 1, cols]
            st[0:CONV_PAD, :] = st[tm:tm + CONV_PAD, :]
            yh = _silu(acc)
            if b < 2:
                yh = yh * lax.rsqrt(jnp.sum(yh * yh, axis=-1, keepdims=True) + RMS_EPS)
            if b == 0:
                yh = yh * (GDN_DK ** -0.5)
            main_ref[:, j * GDN_HW + h * GDN_DK:j * GDN_HW + (h + 1) * GDN_DK] = yh.astype(BF16)
    aux_ref[...] = jnp.dot(xb, wa_ref[...], preferred_element_type=F32)


def _in_proj(x2d, w_main, w_aux, conv_w, *, seq, tm=512):
    n = x2d.shape[0]
    return pl.pallas_call(
        functools.partial(_in_proj_kernel, tm=tm, tiles_per_seq=seq // tm),
        grid=(n // tm,),
        in_specs=[pl.BlockSpec((tm, D_MODEL), lambda i: (i, 0)),
                  pl.BlockSpec((D_MODEL, MAIN_W), lambda i: (0, 0)),
                  pl.BlockSpec((D_MODEL, AUX_W), lambda i: (0, 0)),
                  pl.BlockSpec((CONV_W, 3 * GDN_HW), lambda i: (0, 0))],
        out_specs=[pl.BlockSpec((tm, MAIN_W), lambda i: (i, 0)),
                   pl.BlockSpec((tm, AUX_W), lambda i: (i, 0))],
        out_shape=[jax.ShapeDtypeStruct((n, MAIN_W), BF16),
                   jax.ShapeDtypeStruct((n, AUX_W), F32)],
        scratch_shapes=[pltpu.VMEM((3 * GDN_HEADS, tm + CONV_PAD, GDN_DK), F32)],
        compiler_params=_cparams("arbitrary"),
        name="in_proj",
    )(x2d, w_main, w_aux, conv_w)


def _split_w_in(w_in):
    sizes = (GLA_HEADS * GLA_DK, GLA_HEADS * GLA_DK, GLA_WIDTH, GLA_WIDTH, GLA_GATE_RANK,
             GDN_HEADS * GDN_DK, GDN_HEADS * GDN_DK, GDN_WIDTH, GDN_WIDTH, GDN_HEADS, GDN_HEADS)
    offs = np.cumsum((0,) + sizes)
    seg = [w_in[:, offs[i]:offs[i + 1]] for i in range(len(sizes))]
    qa, ka, va, ra, lra, qb, kb, vb, zb, bb, ab = seg
    w_main = jnp.concatenate([qa, ka, va, ra, qb, kb, vb, zb], axis=1).astype(BF16)
    pad = jnp.zeros((w_in.shape[0], AUX_W - GLA_GATE_RANK - 2 * GDN_HEADS), w_in.dtype)
    w_aux = jnp.concatenate([lra, bb, ab, pad], axis=1).astype(BF16)
    return w_main, w_aux


_NT = (((1,), (1,)), ((), ()))
_TN = (((0,), (0,)), ((), ()))


def _gla_kernel(q_ref, k_ref, v_ref, r_ref, aux_ref, wgk_ref, bgk_ref, g_ref, o_ref, st_ref, *, ts):
    c = CHUNK
    nsub = c // SUB

    @pl.when(pl.program_id(1) == 0)
    def _():
        st_ref[...] = jnp.zeros_like(st_ref)

    pre = jnp.dot(aux_ref[...], wgk_ref[...], precision=HIGHEST, preferred_element_type=F32) + bgk_ref[...]
    gk_all = _log_sigmoid(pre) * (1.0 / GLA_GATE_NORM)

    row128 = lax.broadcasted_iota(jnp.int32, (c, 128), 0)
    lane128 = lax.broadcasted_iota(jnp.int32, (1, 128), 1)
    lane_masks = [(lane128 // GLA_DK) == hh for hh in range(2)]
    sub_valid = [row128 < SUB * (i + 1) for i in range(nsub)]
    ar = lax.broadcasted_iota(jnp.int32, (c, nsub * c), 0)
    ac = lax.broadcasted_iota(jnp.int32, (c, nsub * c), 1)
    amask = ((ac // c) == (ar // SUB)) & ((ac % c) <= ar)
    scale = GLA_DK ** -0.5
    gnorm = g_ref[...]

    nch = ts // c
    rows = [slice(ch * c, (ch + 1) * c) for ch in range(nch)]
    b_cum = _chunk_cumsum(gk_all, c)
    b_all = [b_cum[r] for r in rows]

    cp = [(ch, p) for ch in range(nch) for p in range(2)]
    qt, qd, kd, kstack, dlast = {}, {}, {}, {}, {}
    for ch, p in cp:
        lanes = slice(128 * p, 128 * (p + 1))
        bp = b_all[ch][:, lanes]
        qp = q_ref[rows[ch], lanes].astype(F32) * scale
        kp = k_ref[rows[ch], lanes].astype(F32)
        c_row = jnp.concatenate(
            [jnp.broadcast_to(bp[SUB * i:SUB * i + 1, :], (SUB, 128)) for i in range(nsub)], axis=0)
        qt[ch, p] = (qp * jnp.exp(bp - c_row)).astype(BF16)
        qd[ch, p] = qp * jnp.exp(bp)
        b_last = bp[c - 1:c, :]
        dlast[ch, p] = jnp.exp(b_last)
        kd[ch, p] = kp * jnp.exp(b_last - bp)
        kts = []
        for i in range(nsub):
            e = jnp.exp(jnp.where(sub_valid[i], bp[SUB * i:SUB * i + 1, :] - bp, 0.0))
            kts.append(jnp.where(sub_valid[i], kp * e, 0.0))
        kstack[ch, p] = jnp.concatenate(kts, axis=0)

    cph = [(ch, p, hh) for ch, p in cp for hh in range(2)]
    vh = {(ch, p, hh): v_ref[rows[ch], 128 * (2 * p + hh):128 * (2 * p + hh + 1)] for ch, p, hh in cph}
    r_mat = {(ch, p, hh): lax.dot_general(qt[ch, p], jnp.where(lane_masks[hh], kstack[ch, p], 0.0).astype(BF16),
                                          _NT, preferred_element_type=F32) for ch, p, hh in cph}
    o_intra = {k: jnp.dot(jnp.where(amask, r_mat[k], 0.0).astype(BF16), jnp.concatenate([vh[k]] * nsub, axis=0),
                          preferred_element_type=F32) for k in cph}
    upd = {(ch, p, hh): lax.dot_general(vh[ch, p, hh], jnp.where(lane_masks[hh], kd[ch, p], 0.0).astype(BF16), _TN,
                                        preferred_element_type=F32) for ch, p, hh in cph}

    st = [st_ref[p] for p in range(2)]
    for ch in range(nch):
        st_b = [s.astype(BF16) for s in st]
        o_pair = [lax.dot_general(
            jnp.concatenate([jnp.where(lane_masks[hh], qd[ch, p], 0.0).astype(BF16) for hh in range(2)], axis=0),
            st_b[p], _NT, preferred_element_type=F32) for p in range(2)]
        o_inter = {(p, hh): o_pair[p][hh * c:(hh + 1) * c] for p in range(2) for hh in range(2)}
        st = [st[p] * dlast[ch, p] + upd[ch, p, 0] + upd[ch, p, 1] for p in range(2)]
        for p in range(2):
            for hh in range(2):
                h = 2 * p + hh
                o = o_intra[ch, p, hh] + o_inter[p, hh]
                o = o * lax.rsqrt(jnp.mean(o * o, axis=-1, keepdims=True) + RMS_EPS) * gnorm
                gate = _silu(r_ref[rows[ch], 128 * h:128 * (h + 1)].astype(F32))
                o_ref[rows[ch], 128 * h:128 * (h + 1)] = (o * gate).astype(BF16)
    for p in range(2):
        st_ref[p] = st[p]


def _gla(main, aux, wgk_pad, b_gk, gla_norm_g, *, bsz, seq, ts=512):
    nt = seq // ts
    row = lambda b, t: b * nt + t
    return pl.pallas_call(
        functools.partial(_gla_kernel, ts=ts),
        grid=(bsz, nt),
        in_specs=[pl.BlockSpec((ts, 256), lambda b, t: (row(b, t), 0)),
                  pl.BlockSpec((ts, 256), lambda b, t: (row(b, t), 1)),
                  pl.BlockSpec((ts, 512), lambda b, t: (row(b, t), 1)),
                  pl.BlockSpec((ts, 512), lambda b, t: (row(b, t), 2)),
                  pl.BlockSpec((ts, AUX_W), lambda b, t: (row(b, t), 0)),
                  pl.BlockSpec((AUX_W, 256), lambda b, t: (0, 0)),
                  pl.BlockSpec((1, 256), lambda b, t: (0, 0)),
                  pl.BlockSpec((1, 128), lambda b, t: (0, 0))],
        out_specs=pl.BlockSpec((ts, GLA_WIDTH), lambda b, t: (row(b, t), 0)),
        out_shape=jax.ShapeDtypeStruct((bsz * seq, GLA_WIDTH), BF16),
        scratch_shapes=[pltpu.VMEM((2, 128, 128), F32)],
        compiler_params=_cparams("parallel", "arbitrary"),
        name="gla",
    )(main, main, main, main, aux, wgk_pad, b_gk, gla_norm_g)


def _gdn_kernel(q_ref, k_ref, v_ref, z_ref, aux_ref, alog_ref, dtb_ref, g_ref, o_ref, st_ref, *, ts):
    c = CHUNK
    nch = ts // c

    @pl.when(pl.program_id(1) == 0)
    def _():
        st_ref[...] = jnp.zeros_like(st_ref)

    aux = aux_ref[...]
    beta_full = _sigmoid(aux)
    g_full = -jnp.exp(alog_ref[...]) * _softplus(aux + dtb_ref[...])
    gam_full = _chunk_cumsum(g_full, c)
    gam_t = gam_full.T
    gnorm = g_ref[...]

    row = lax.broadcasted_iota(jnp.int32, (c, 2 * c), 0)
    lane = lax.broadcasted_iota(jnp.int32, (c, 2 * c), 1)
    left = lane < c
    col = lane % c
    causal = col <= row
    strict = col < row
    same32 = (row // 32) == (col // 32)
    same16 = (row // 16) == (col // 16)
    lvl0 = same16 & strict
    lvl1 = same32 & jnp.logical_not(same16) & strict
    lvl2 = jnp.logical_not(same32) & strict

    def pack_rows(x2):
        return jnp.where(left, x2[0:c], x2[c:2 * c])

    def pack_cols(x1):
        return jnp.where(left, x1[0:c], x1[c:2 * c])

    def halves(x):
        return jnp.where(left, x, 0.0).astype(BF16), jnp.where(left, 0.0, x).astype(BF16)

    def mmp(x, y):
        return jnp.dot(x.astype(BF16), jnp.concatenate(halves(y), axis=0), preferred_element_type=F32)

    heads = range(GDN_HEADS)
    npair = nch // 2
    items = [(h, pr) for h in heads for pr in range(npair)]
    prow = [slice(2 * c * pr, 2 * c * (pr + 1)) for pr in range(npair)]

    qn, kn, beta, gam, gam_r, kb, rhs, rhs_b, qd, kd = ([None] * GDN_HEADS for _ in range(10))
    for h in heads:
        hl = slice(128 * h, 128 * (h + 1))
        kb[h] = k_ref[:, hl]
        qn[h] = q_ref[:, hl].astype(F32)
        kn[h] = kb[h].astype(F32)
        vh = v_ref[:, hl].astype(F32)
        beta[h] = beta_full[:, AUX_BB + h:AUX_BB + h + 1]
        gam[h] = gam_full[:, AUX_AB + h:AUX_AB + h + 1]
        gam_r[h] = gam_t[AUX_AB + h:AUX_AB + h + 1, :]
        egam = jnp.exp(gam[h])
        rhs[h] = jnp.concatenate([beta[h] * vh, beta[h] * egam * kn[h]], axis=1)
        rhs_b[h] = rhs[h].astype(BF16)
        qd[h] = (qn[h] * egam).astype(BF16)
        gl_rows = jnp.concatenate(
            [jnp.broadcast_to(gam[h][c * (j + 1) - 1:c * (j + 1), :], (c, 1)) for j in range(nch)], axis=0)
        kd[h] = (kn[h] * jnp.exp(gl_rows - gam[h])).astype(BF16)

    kk = [pack_rows(lax.dot_general(kb[h][prow[pr]], kb[h][prow[pr]], _NT, preferred_element_type=F32))
          for h, pr in items]
    qk = [pack_rows(lax.dot_general(qn[h][prow[pr]].astype(BF16), kb[h][prow[pr]], _NT,
                                    preferred_element_type=F32)) for h, pr in items]
    dec = [jnp.exp(jnp.where(causal, pack_cols(gam[h][prow[pr]]) - gam_r[h][:, prow[pr]], 0.0)) for h, pr in items]
    a = [pack_cols(beta[h][prow[pr]]) * kk[i] * dec[i] for i, (h, pr) in enumerate(items)]
    attn = [halves(jnp.where(causal, qk[i] * dec[i], 0.0)) for i in range(len(items))]

    x1 = [jnp.where(lvl0, -ai, 0.0) for ai in a]
    x2 = [mmp(x, x) for x in x1]
    y = [mmp(x, xx) for x, xx in zip(x1, x2)]
    p = [x + xx + yy for x, xx, yy in zip(x1, x2, y)]
    xk = x2
    for _ in range(2):
        xk = [mmp(x, x) for x in xk]
        y = [mmp(pp, x) for pp, x in zip(p, xk)]
        p = [pp + x + yy for pp, x, yy in zip(p, xk, y)]
    for lvl in (lvl1, lvl2):
        low = [jnp.where(lvl, ai, 0.0) for ai in a]
        y = [mmp(pp, lo) for pp, lo in zip(p, low)]
        zz = [lo + yy for lo, yy in zip(low, y)]
        y = [mmp(z, pp) for z, pp in zip(zz, p)]
        p = [pp - z - yy for pp, z, yy in zip(p, zz, y)]

    us = [[None] * nch for _ in heads]
    ws = [[None] * nch for _ in heads]
    for i, (h, pr) in enumerate(items):
        for half, p_half in enumerate(halves(p[i])):
            j = 2 * pr + half
            uw = rhs[h][c * j:c * (j + 1)] + jnp.dot(p_half, rhs_b[h][prow[pr]], preferred_element_type=F32)
            us[h][j] = uw[:, :GDN_DV]
            ws[h][j] = uw[:, GDN_DV:].astype(BF16)

    s = [st_ref[h] for h in heads]
    vprev = [None] * GDN_HEADS
    for j in range(nch):
        rows = slice(c * j, c * (j + 1))
        sb = [s[h].astype(BF16) for h in heads]
        ws_qd = [jnp.dot(jnp.concatenate([ws[h][j], qd[h][rows]], axis=0), sb[h], preferred_element_type=F32)
                 for h in heads]
        vn16 = [(us[h][j] - ws_qd[h][0:c]).astype(BF16) for h in heads]
        o_inter = [ws_qd[h][c:2 * c] for h in heads]
        for h in heads:
            vpair = jnp.concatenate([vn16[h], jnp.zeros_like(vn16[h])] if j % 2 == 0 else [vprev[h], vn16[h]], axis=0)
            o = o_inter[h] + jnp.dot(attn[h * npair + j // 2][j % 2], vpair, preferred_element_type=F32)
            s[h] = jnp.exp(gam[h][c * (j + 1) - 1:c * (j + 1), :]) * s[h] + lax.dot_general(
                kd[h][rows], vn16[h], _TN, preferred_element_type=F32)
            o = o * lax.rsqrt(jnp.mean(o * o, axis=-1, keepdims=True) + RMS_EPS) * gnorm
            gate = _silu(z_ref[rows, 128 * h:128 * (h + 1)].astype(F32))
            o_ref[rows, 128 * h:128 * (h + 1)] = (o * gate).astype(BF16)
        vprev = vn16
    for h in heads:
        st_ref[h] = s[h]


def _gdn(main, aux, alog_pad, dtb_pad, gdn_norm_g, *, bsz, seq, ts=512):
    nt = seq // ts
    row = lambda b, t: b * nt + t
    hw = GDN_HEADS * GDN_DK
    return pl.pallas_call(
        functools.partial(_gdn_kernel, ts=ts),
        grid=(bsz, nt),
        in_specs=[pl.BlockSpec((ts, hw), lambda b, t: (row(b, t), 3)),
                  pl.BlockSpec((ts, hw), lambda b, t: (row(b, t), 4)),
                  pl.BlockSpec((ts, hw), lambda b, t: (row(b, t), 5)),
                  pl.BlockSpec((ts, hw), lambda b, t: (row(b, t), 6)),
                  pl.BlockSpec((ts, AUX_W), lambda b, t: (row(b, t), 0)),
                  pl.BlockSpec((1, AUX_W), lambda b, t: (0, 0)),
                  pl.BlockSpec((1, AUX_W), lambda b, t: (0, 0)),
                  pl.BlockSpec((1, GDN_DV), lambda b, t: (0, 0))],
        out_specs=pl.BlockSpec((ts, GDN_WIDTH), lambda b, t: (row(b, t), 0)),
        out_shape=jax.ShapeDtypeStruct((bsz * seq, GDN_WIDTH), BF16),
        scratch_shapes=[pltpu.VMEM((GDN_HEADS, GDN_DK, GDN_DV), F32)],
        compiler_params=_cparams("parallel", "arbitrary"),
        name="gdn",
    )(main, main, main, main, aux, alog_pad, dtb_pad, gdn_norm_g)


def _pad_aux_row(v, offset):
    return jnp.zeros((1, AUX_W), F32).at[0, offset:offset + v.shape[0]].set(v.astype(F32))


ROUTE_W = 128
R_E1, R_E2, R_RANK1, R_RANK2, R_G1, R_G2 = range(6)
ROUTER_GROUP_COL = N_EXPERTS


def _layer_norm(h, g, b):
    mu = jnp.mean(h, axis=-1, keepdims=True)
    hc = h - mu
    var = jnp.mean(hc * hc, axis=-1, keepdims=True)
    return hc * lax.rsqrt(var + LN_EPS) * g + b


def _pack_bf16_pairs(x):
    w = x.shape[1] // 2
    lo = lax.bitcast_convert_type(x[:, :w].astype(BF16).astype(F32), jnp.uint32)
    hi = lax.bitcast_convert_type(x[:, w:].astype(BF16).astype(F32), jnp.uint32)
    return (lo >> 16) | (hi & jnp.uint32(0xFFFF0000))


def _unpack_bf16_pairs(p):
    lo = lax.bitcast_convert_type(p << 16, F32)
    hi = lax.bitcast_convert_type(p & jnp.uint32(0xFFFF0000), F32)
    return lo, hi


HALF = D_MODEL // 2
TOK_ROWS = HALF // 128


def _store_token_rows(ref, packed):
    m = packed.shape[0]
    for c in range(TOK_ROWS):
        ref[pl.ds(c, m, stride=TOK_ROWS), :] = packed[:, 128 * c:128 * (c + 1)]


def _load_token_rows(ref, m):
    return jnp.concatenate([ref[pl.ds(c, m, stride=TOK_ROWS), :] for c in range(TOK_ROWS)], axis=1)


def _post_mix_kernel(oa_ref, ob_ref, x_ref, wo_ref, g_ref, b_ref, wr_ref, br_ref,
                     x1p_ref, route_ref, cnt_ref, carry_ref, *, tm):
    @pl.when(pl.program_id(0) == 0)
    def _():
        carry_ref[...] = jnp.zeros_like(carry_ref)

    y = jnp.dot(oa_ref[...], wo_ref[0:GLA_WIDTH, :], preferred_element_type=F32)
    y = y + jnp.dot(ob_ref[...], wo_ref[GLA_WIDTH:, :], preferred_element_type=F32)
    x1 = _layer_norm(ALPHA_DN * x_ref[...] + y, g_ref[...], b_ref[...])
    _store_token_rows(x1p_ref, _pack_bf16_pairs(x1))

    xh = x1.astype(BF16)
    xl = (x1 - xh.astype(F32)).astype(BF16)
    hw = jnp.dot(xh, wr_ref[...], preferred_element_type=F32)
    logits = (hw[:, :ROUTE_W] + hw[:, ROUTE_W:]
              + jnp.dot(xl, wr_ref[:, :ROUTE_W], preferred_element_type=F32) + br_ref[...])
    lane = lax.broadcasted_iota(jnp.int32, (tm, ROUTE_W), 1)
    big = jnp.int32(1 << 20)
    neg = jnp.float32(-jnp.inf)

    def first_argmax(vals):
        m = jnp.max(vals, axis=-1, keepdims=True)
        idx = jnp.min(jnp.where(vals == m, lane, big), axis=-1, keepdims=True)
        return m, idx

    is_group = (lane >= ROUTER_GROUP_COL) & (lane < ROUTER_GROUP_COL + N_GROUPS)
    gl = jnp.where(is_group, logits, neg)
    gmax, gidx = first_argmax(gl)
    p_group = 1.0 / jnp.sum(jnp.exp(gl - gmax), axis=-1, keepdims=True)
    g_sel = gidx - ROUTER_GROUP_COL
    el = jnp.where((lane // EXPERTS_PER_GROUP) == g_sel, logits, neg)
    t1, e1 = first_argmax(el)
    t2, e2 = first_argmax(jnp.where(lane == e1, neg, el))
    ex = jnp.exp(t2 - t1)
    w1 = 1.0 / (1.0 + ex)
    gate1 = p_group * w1
    gate2 = p_group * (ex * w1)

    oh1 = lane == e1
    oh2 = lane == e2
    ri = lax.broadcasted_iota(jnp.int32, (tm, tm), 0)
    ci = lax.broadcasted_iota(jnp.int32, (tm, tm), 1)
    lstrict = (ci < ri).astype(BF16)
    oh1f = oh1.astype(F32)
    oh2f = oh2.astype(F32)
    c1 = jnp.dot(lstrict, oh1f.astype(BF16), preferred_element_type=F32)
    c2 = jnp.dot(lstrict, oh2f.astype(BF16), preferred_element_type=F32)
    carry = carry_ref[...]
    tot1 = jnp.sum(oh1f, axis=0, keepdims=True)
    tot2 = jnp.sum(oh2f, axis=0, keepdims=True)
    rank1 = jnp.sum(jnp.where(oh1, c1 + carry, 0.0), axis=-1, keepdims=True)
    rank2 = jnp.sum(jnp.where(oh2, c2 + (carry + tot1), 0.0), axis=-1, keepdims=True)
    new_carry = carry + tot1 + tot2
    carry_ref[...] = new_carry
    cnt_ref[...] = new_carry

    cols = (e1.astype(F32), e2.astype(F32), rank1, rank2, gate1, gate2)
    route = jnp.zeros((tm, ROUTE_W), F32)
    for j, col in enumerate(cols):
        route = jnp.where(lane == j, col, route)
    route_ref[...] = route


def _post_mix(o_a, o_b, x2d, w_out_b, ln_g, ln_b, w_router, b_router, *, tm=512):
    n = x2d.shape[0]
    return pl.pallas_call(
        functools.partial(_post_mix_kernel, tm=tm),
        grid=(n // tm,),
        in_specs=[pl.BlockSpec((tm, GLA_WIDTH), lambda i: (i, 0)),
                  pl.BlockSpec((tm, GDN_WIDTH), lambda i: (i, 0)),
                  pl.BlockSpec((tm, D_MODEL), lambda i: (i, 0)),
                  pl.BlockSpec((GLA_WIDTH + GDN_WIDTH, D_MODEL), lambda i: (0, 0)),
                  pl.BlockSpec((1, D_MODEL), lambda i: (0, 0)),
                  pl.BlockSpec((1, D_MODEL), lambda i: (0, 0)),
                  pl.BlockSpec((D_MODEL, 2 * ROUTE_W), lambda i: (0, 0)),
                  pl.BlockSpec((1, ROUTE_W), lambda i: (0, 0))],
        out_specs=[pl.BlockSpec((tm * TOK_ROWS, 128), lambda i: (i, 0)),
                   pl.BlockSpec((tm, ROUTE_W), lambda i: (i, 0)),
                   pl.BlockSpec((1, ROUTE_W), lambda i: (0, 0))],
        out_shape=[jax.ShapeDtypeStruct((n * TOK_ROWS, 128), jnp.uint32),
                   jax.ShapeDtypeStruct((n, ROUTE_W), F32),
                   jax.ShapeDtypeStruct((1, ROUTE_W), F32)],
        scratch_shapes=[pltpu.VMEM((1, ROUTE_W), F32)],
        compiler_params=_cparams("arbitrary"),
        name="post_mix",
    )(o_a, o_b, x2d, w_out_b, ln_g, ln_b, w_router, b_router)


EXPERT_BLOCK = 256
BLOCKS_PER_STEP = 4

ROW_UNROLL = 8


def _plan_kernel(route_ref, cnt_ref, dest_ref, *, tm, tb):
    counts = cnt_ref[...].astype(jnp.int32)
    padded = ((counts + (tb - 1)) & jnp.int32(-tb)).astype(F32)
    ri = lax.broadcasted_iota(jnp.int32, (ROUTE_W, ROUTE_W), 0)
    ci = lax.broadcasted_iota(jnp.int32, (ROUTE_W, ROUTE_W), 1)
    pstart = jnp.dot(jnp.broadcast_to(padded, (8, ROUTE_W)), (ri < ci).astype(F32),
                     precision=HIGHEST, preferred_element_type=F32)[0:1, :]
    route = route_ref[...]
    lane = lax.broadcasted_iota(jnp.int32, (tm, ROUTE_W), 1).astype(F32)
    out = jnp.zeros((tm, ROUTE_W), F32)
    for k, (ce, cr) in enumerate(((R_E1, R_RANK1), (R_E2, R_RANK2))):
        off = jnp.sum(jnp.where(lane == route[:, ce:ce + 1], pstart, 0.0), axis=-1, keepdims=True)
        out = jnp.where(lane == float(k), off + route[:, cr:cr + 1], out)
    dest_ref[...] = out.T[0:8, :].astype(jnp.int32) * TOK_ROWS


def _plan(route, counts, *, tb, tm=2048):
    n = route.shape[0]
    tm = min(tm, n)
    return pl.pallas_call(
        functools.partial(_plan_kernel, tm=tm, tb=tb),
        grid=(n // tm,),
        in_specs=[pl.BlockSpec((tm, ROUTE_W), lambda i: (i, 0)),
                  pl.BlockSpec((1, ROUTE_W), lambda i: (0, 0))],
        out_specs=pl.BlockSpec((8, tm), lambda i: (0, i)),
        out_shape=jax.ShapeDtypeStruct((8, n), jnp.int32),
        compiler_params=_cparams("parallel"),
        name="plan",
    )(route, counts)


def _dispatch_kernel(d1_ref, d2_ref, xp_ref, xs_init_ref, xs_ref, sem, *, tm):
    del xs_init_ref
    base = pl.program_id(0) * tm

    def body(j, carry):
        for u in range(ROW_UNROLL):
            r = j * ROW_UNROLL + u
            for k, d_ref in enumerate((d1_ref, d2_ref)):
                dst = pl.multiple_of(d_ref[base + r], TOK_ROWS)
                pltpu.make_async_copy(xp_ref.at[pl.ds(r * TOK_ROWS, TOK_ROWS), :],
                                      xs_ref.at[pl.ds(dst, TOK_ROWS), :], sem).start(priority=k)
        return carry
    lax.fori_loop(0, tm // ROW_UNROLL, body, 0)
    for _ in range(TOP_K):
        pltpu.make_async_copy(xp_ref, xs_ref.at[pl.ds(0, tm * TOK_ROWS), :], sem).wait()


def _dispatch(dest1, dest2, x1p, n_rows, *, tm=1024):
    n = x1p.shape[0] // TOK_ROWS
    grid_spec = pltpu.PrefetchScalarGridSpec(
        num_scalar_prefetch=2,
        grid=(n // tm,),
        in_specs=[pl.BlockSpec((tm * TOK_ROWS, 128), lambda i, a, b: (i, 0)),
                  pl.BlockSpec(memory_space=pl.ANY)],
        out_specs=pl.BlockSpec(memory_space=pl.ANY),
        scratch_shapes=[pltpu.SemaphoreType.DMA(())],
    )
    return pl.pallas_call(
        functools.partial(_dispatch_kernel, tm=tm),
        grid_spec=grid_spec,
        out_shape=jax.ShapeDtypeStruct((n_rows * TOK_ROWS, 128), jnp.uint32),
        input_output_aliases={3: 0},
        compiler_params=_cparams("arbitrary"),
        name="dispatch",
    )(dest1, dest2, x1p, jnp.zeros((n_rows * TOK_ROWS, 128), jnp.uint32))


def _expert_kernel(blk_e_ref, first_ref, next_e_ref, nvalid_ref, xs_ref, wg_hbm, wu_hbm, wd_hbm, y_ref,
                   wgs, wus, wds, wgb, wub, wdb, sem, *, tb):
    rows = tb * TOK_ROWS

    def weight_copies(e):
        return [pltpu.make_async_copy(src.at[e], dst, sem.at[j])
                for j, (src, dst) in enumerate(((wg_hbm, wgs), (wu_hbm, wus), (wd_hbm, wds)))]

    @pl.when((pl.program_id(0) == 0) & (nvalid_ref[0] > 0))
    def _():
        for cp in weight_copies(blk_e_ref[0]):
            cp.start()

    for sub in range(BLOCKS_PER_STEP):
        i = pl.program_id(0) * BLOCKS_PER_STEP + sub
        xs_blk = xs_ref.at[pl.ds(sub * rows, rows), :]
        y_blk = y_ref.at[pl.ds(sub * rows, rows), :]

        @pl.when(i < nvalid_ref[0])
        def _():
            @pl.when(first_ref[i] == 1)
            def _():
                for cp in weight_copies(blk_e_ref[i]):
                    cp.wait()
                wgb[...] = wgs[...].astype(BF16)
                wub[...] = wus[...].astype(BF16)
                wdb[...] = wds[...].astype(BF16)

                @pl.when(next_e_ref[i] >= 0)
                def _():
                    for cp in weight_copies(next_e_ref[i]):
                        cp.start()

            lo, hi = _unpack_bf16_pairs(_load_token_rows(xs_blk, tb))
            xb = jnp.concatenate([lo.astype(BF16), hi.astype(BF16)], axis=1)
            hg = jnp.dot(xb, wgb[...], preferred_element_type=F32)
            hu = jnp.dot(xb, wub[...], preferred_element_type=F32)
            hid = (_silu(hg) * hu).astype(BF16)
            _store_token_rows(y_blk, _pack_bf16_pairs(jnp.dot(hid, wdb[...], preferred_element_type=F32)))

        @pl.when(i >= nvalid_ref[0])
        def _():
            y_blk[...] = jnp.zeros((rows, 128), jnp.uint32)


def _experts(blk_e, first, next_e, nvalid, xs, w_gate, w_up, w_down, *, tb=EXPERT_BLOCK):
    nb = blk_e.shape[0]
    assert nb % BLOCKS_PER_STEP == 0
    step_rows = BLOCKS_PER_STEP * tb * TOK_ROWS
    last_step = lambda nv: jnp.maximum((nv[0] - 1) // BLOCKS_PER_STEP, 0)
    grid_spec = pltpu.PrefetchScalarGridSpec(
        num_scalar_prefetch=4,
        grid=(nb // BLOCKS_PER_STEP,),
        in_specs=[pl.BlockSpec((step_rows, 128), lambda s, be, fr, ne, nv: (jnp.minimum(s, last_step(nv)), 0)),
                  pl.BlockSpec(memory_space=pl.ANY),
                  pl.BlockSpec(memory_space=pl.ANY),
                  pl.BlockSpec(memory_space=pl.ANY)],
        out_specs=pl.BlockSpec((step_rows, 128), lambda s, be, fr, ne, nv: (s, 0)),
        scratch_shapes=[pltpu.VMEM((D_MODEL, D_EXPERT), F32),
                        pltpu.VMEM((D_MODEL, D_EXPERT), F32),
                        pltpu.VMEM((D_EXPERT, D_MODEL), F32),
                        pltpu.VMEM((D_MODEL, D_EXPERT), BF16),
                        pltpu.VMEM((D_MODEL, D_EXPERT), BF16),
                        pltpu.VMEM((D_EXPERT, D_MODEL), BF16),
                        pltpu.SemaphoreType.DMA((3,))],
    )
    return pl.pallas_call(
        functools.partial(_expert_kernel, tb=tb),
        grid_spec=grid_spec,
        out_shape=jax.ShapeDtypeStruct((nb * tb * TOK_ROWS, 128), jnp.uint32),
        compiler_params=_cparams("arbitrary"),
        name="experts",
    )(blk_e, first, next_e, nvalid, xs, w_gate, w_up, w_down)


def _combine_kernel(d1_ref, d2_ref, y_hbm, x1_ref, route_ref, g_ref, b_ref, o_ref, ybuf, sem, *, tt):
    i = pl.program_id(0)
    nsteps = pl.num_programs(0)
    slot = i % 2

    def issue(tile, slot_):
        def body(j, carry):
            for u in range(ROW_UNROLL):
                r = j * ROW_UNROLL + u
                for k, d_ref in enumerate((d1_ref, d2_ref)):
                    src = pl.multiple_of(d_ref[tile * tt + r], TOK_ROWS)
                    pltpu.make_async_copy(y_hbm.at[pl.ds(src, TOK_ROWS), :],
                                          ybuf.at[slot_, k, pl.ds(r * TOK_ROWS, TOK_ROWS), :],
                                          sem.at[slot_]).start(priority=k)
            return carry
        lax.fori_loop(0, tt // ROW_UNROLL, body, 0)

    @pl.when(i == 0)
    def _():
        issue(0, 0)

    @pl.when(i + 1 < nsteps)
    def _():
        issue(i + 1, 1 - slot)

    for k in range(TOP_K):
        pltpu.make_async_copy(y_hbm.at[pl.ds(0, tt * TOK_ROWS), :], ybuf.at[slot, k], sem.at[slot]).wait()
    route = route_ref[...]
    g1 = route[:, R_G1:R_G1 + 1]
    g2 = route[:, R_G2:R_G2 + 1]
    lo1, hi1 = _unpack_bf16_pairs(_load_token_rows(ybuf.at[slot, 0], tt))
    lo2, hi2 = _unpack_bf16_pairs(_load_token_rows(ybuf.at[slot, 1], tt))
    x_lo, x_hi = _unpack_bf16_pairs(_load_token_rows(x1_ref, tt))
    h_lo = ALPHA_DN * x_lo + (g1 * lo1 + g2 * lo2)
    h_hi = ALPHA_DN * x_hi + (g1 * hi1 + g2 * hi2)
    mu = (jnp.sum(h_lo, axis=-1, keepdims=True) + jnp.sum(h_hi, axis=-1, keepdims=True)) * (1.0 / D_MODEL)
    c_lo = h_lo - mu
    c_hi = h_hi - mu
    var = (jnp.sum(c_lo * c_lo, axis=-1, keepdims=True) + jnp.sum(c_hi * c_hi, axis=-1, keepdims=True)) * (1.0 / D_MODEL)
    inv = lax.rsqrt(var + LN_EPS)
    o_ref[:, 0:HALF] = c_lo * inv * g_ref[:, 0:HALF] + b_ref[:, 0:HALF]
    o_ref[:, HALF:] = c_hi * inv * g_ref[:, HALF:] + b_ref[:, HALF:]


def _combine(dest1, dest2, ybuf, x1p, route, ln_g, ln_b, *, tt=512):
    n = x1p.shape[0] // TOK_ROWS
    grid_spec = pltpu.PrefetchScalarGridSpec(
        num_scalar_prefetch=2,
        grid=(n // tt,),
        in_specs=[pl.BlockSpec(memory_space=pl.ANY),
                  pl.BlockSpec((tt * TOK_ROWS, 128), lambda i, a, b: (i, 0)),
                  pl.BlockSpec((tt, ROUTE_W), lambda i, a, b: (i, 0)),
                  pl.BlockSpec((1, D_MODEL), lambda i, a, b: (0, 0)),
                  pl.BlockSpec((1, D_MODEL), lambda i, a, b: (0, 0))],
        out_specs=pl.BlockSpec((tt, D_MODEL), lambda i, a, b: (i, 0)),
        scratch_shapes=[pltpu.VMEM((2, TOP_K, tt * TOK_ROWS, 128), jnp.uint32),
                        pltpu.SemaphoreType.DMA((2,))],
    )
    return pl.pallas_call(
        functools.partial(_combine_kernel, tt=tt),
        grid_spec=grid_spec,
        out_shape=jax.ShapeDtypeStruct((n, D_MODEL), F32),
        compiler_params=_cparams("arbitrary"),
        name="combine",
    )(dest1, dest2, ybuf, x1p, route, ln_g, ln_b)


def _block_plan(counts_row, n, tb):
    counts = counts_row[0, :N_EXPERTS].astype(jnp.int32)
    pends = jnp.cumsum((counts + tb - 1) // tb * tb)
    nb = (n * TOP_K) // tb + N_EXPERTS
    nvalid = (pends[-1] // tb).astype(jnp.int32)
    blk = jnp.arange(nb, dtype=jnp.int32)
    blk_e = jnp.sum((pends[None, :] <= (blk * tb)[:, None]).astype(jnp.int32), axis=1)
    blk_e = jnp.minimum(blk_e, N_EXPERTS - 1)
    blk_e = jnp.where(blk < nvalid, blk_e, blk_e[jnp.maximum(nvalid - 1, 0)])
    first = jnp.concatenate([jnp.ones((1,), jnp.int32), (blk_e[1:] != blk_e[:-1]).astype(jnp.int32)])
    ids = jnp.arange(N_EXPERTS, dtype=jnp.int32)
    later = lax.cummin(jnp.where(counts > 0, ids, N_EXPERTS), axis=0, reverse=True)
    nxt = jnp.concatenate([later[1:], jnp.full((1,), N_EXPERTS, jnp.int32)])
    nxt = jnp.where(nxt < N_EXPERTS, nxt, -1)
    return blk_e, first, nxt[blk_e], nvalid.reshape(1)


def kernel(x, w_in, w_gk_up, b_gk, conv_w, a_log, dt_bias, gla_norm_g, gdn_norm_g, w_out, ln1_g, ln1_b, w_router_group, b_router_group, w_router_expert, b_router_expert, w_gate, w_up, w_down, ln2_g, ln2_b):
    bsz, seq, d = x.shape
    n = bsz * seq
    x2d = x.reshape(n, d)
    w_main, w_aux = _split_w_in(w_in[0])
    main, aux = _in_proj(x2d, w_main, w_aux, conv_w[0].astype(F32), seq=seq)
    wgk = jnp.zeros((AUX_W, 256), F32).at[:GLA_GATE_RANK].set(w_gk_up[0])
    o_a = _gla(main, aux, wgk, b_gk[0][None], gla_norm_g[0][None], bsz=bsz, seq=seq)
    o_b = _gdn(main, aux, _pad_aux_row(a_log[0], AUX_AB), _pad_aux_row(dt_bias[0], AUX_AB),
               gdn_norm_g[0][None], bsz=bsz, seq=seq)
    rpad = jnp.zeros((d, ROUTE_W - N_EXPERTS - N_GROUPS), F32)
    w_router = jnp.concatenate([w_router_expert[0], w_router_group[0], rpad], axis=1)
    w_router_hi = w_router.astype(BF16)
    w_router = jnp.concatenate([w_router_hi, (w_router - w_router_hi.astype(F32)).astype(BF16)], axis=1)
    b_router = jnp.concatenate([b_router_expert[0], b_router_group[0], rpad[0]])[None]
    x1p, route, counts = _post_mix(o_a, o_b, x2d, w_out[0].astype(BF16), ln1_g[0][None], ln1_b[0][None],
                                       w_router, b_router)
    dest = _plan(route, counts, tb=EXPERT_BLOCK)
    blk_e, first, next_e, nvalid = _block_plan(counts, n, EXPERT_BLOCK)
    xs = _dispatch(dest[0], dest[1], x1p, blk_e.shape[0] * EXPERT_BLOCK)
    ybuf = _experts(blk_e, first, next_e, nvalid, xs, w_gate[0], w_up[0], w_down[0])
    out = _combine(dest[0], dest[1], ybuf, x1p, route, ln2_g[0][None], ln2_b[0][None])
    return out.reshape(bsz, seq, d)
```

```python
import functools

import jax
import jax.numpy as jnp
import numpy as np
from jax import lax
from jax.experimental import pallas as pl
from jax.experimental.pallas import tpu as pltpu

F32 = jnp.float32
BF16 = jnp.bfloat16
HIGHEST = lax.Precision.HIGHEST

D_MODEL = 1024
DEPTH = 1
GLA_HEADS = 4
GLA_DK = 64
GLA_DV = 128
GLA_WIDTH = GLA_HEADS * GLA_DV
GLA_GATE_RANK = 16
GLA_GATE_NORM = 16.0
GDN_HEADS = 4
GDN_DK = 128
GDN_DV = 128
GDN_WIDTH = GDN_HEADS * GDN_DV
CONV_W = 4
CHUNK = 64
N_GROUPS = 8
EXPERTS_PER_GROUP = 8
N_EXPERTS = N_GROUPS * EXPERTS_PER_GROUP
TOP_K = 2
D_EXPERT = 512
LN_EPS = 1e-5
RMS_EPS = 1e-6
ALPHA_DN = (2.0 * DEPTH) ** 0.25

MAIN_W = 2 * GLA_HEADS * GLA_DK + 2 * GLA_WIDTH + 4 * GDN_WIDTH
AUX_W = 128
AUX_LRA = 0
AUX_BB = GLA_GATE_RANK
AUX_AB = GLA_GATE_RANK + GDN_HEADS
SUB = 16
VMEM_LIMIT = 56 * 1024 * 1024


def _cparams(*sem):
    return pltpu.CompilerParams(dimension_semantics=sem, vmem_limit_bytes=VMEM_LIMIT)


def _sigmoid(x):
    return 1.0 / (1.0 + jnp.exp(-x))


def _silu(x):
    return x * _sigmoid(x)


def _log_sigmoid(x):
    return jnp.minimum(x, 0.0) - jnp.log(1.0 + jnp.exp(-jnp.abs(x)))


def _softplus(x):
    return jnp.maximum(x, 0.0) + jnp.log(1.0 + jnp.exp(-jnp.abs(x)))


def _chunk_cumsum(x, c):
    pos = lax.broadcasted_iota(jnp.int32, x.shape, 0) % c
    shift = 1
    while shift < c:
        x = x + jnp.where(pos >= shift, pltpu.roll(x, shift, 0), 0.0)
        shift *= 2
    return x


CONV_PAD = 8
GDN_HW = GDN_HEADS * GDN_DK
GDN_QKV_BLOCK0 = (2 * GLA_HEADS * GLA_DK + 2 * GLA_WIDTH) // GDN_HW


def _in_proj_kernel(x_ref, wm_ref, wa_ref, cw_ref, main_ref, aux_ref, stage_ref, *, tm, tiles_per_seq):
    xb = x_ref[...].astype(BF16)

    @pl.when((pl.program_id(0) % tiles_per_seq) == 0)
    def _():
        stage_ref[:, 0:CONV_PAD, :] = jnp.zeros((3 * GDN_HEADS, CONV_PAD, GDN_DK), F32)

    for j in range(MAIN_W // GDN_HW):
        sl = slice(j * GDN_HW, (j + 1) * GDN_HW)
        r = jnp.dot(xb, wm_ref[:, sl], preferred_element_type=F32)
        b = j - GDN_QKV_BLOCK0
        if not 0 <= b < 3:
            main_ref[:, sl] = r.astype(BF16)
            continue
        for h in range(GDN_HEADS):
            st = stage_ref.at[b * GDN_HEADS + h]
            cols = slice(b * GDN_HW + h * GDN_DK, b * GDN_HW + (h + 1) * GDN_DK)
            st[CONV_PAD:, :] = r[:, h * GDN_DK:(h + 1) * GDN_DK]
            acc = jnp.zeros((tm, GDN_DK), F32)
            for i in range(CONV_W):
                lo = CONV_PAD - (CONV_W - 1) + i
                acc = acc + st[lo:lo + tm, :] * cw_ref[i:i + 1, cols]
            st[0:CONV_PAD, :] = st[tm:tm + CONV_PAD, :]
            yh = _silu(acc)
            if b < 2:
                yh = yh * lax.rsqrt(jnp.sum(yh * yh, axis=-1, keepdims=True) + RMS_EPS)
            if b == 0:
                yh = yh * (GDN_DK ** -0.5)
            main_ref[:, j * GDN_HW + h * GDN_DK:j * GDN_HW + (h + 1) * GDN_DK] = yh.astype(BF16)
    aux_ref[...] = jnp.dot(xb, wa_ref[...], preferred_element_type=F32)


def _in_proj(x2d, w_main, w_aux, conv_w, *, seq, tm=512):
    n = x2d.shape[0]
    return pl.pallas_call(
        functools.partial(_in_proj_kernel, tm=tm, tiles_per_seq=seq // tm),
        grid=(n // tm,),
        in_specs=[pl.BlockSpec((tm, D_MODEL), lambda i: (i, 0)),
                  pl.BlockSpec((D_MODEL, MAIN_W), lambda i: (0, 0)),
                  pl.BlockSpec((D_MODEL, AUX_W), lambda i: (0, 0)),
                  pl.BlockSpec((CONV_W, 3 * GDN_HW), lambda i: (0, 0))],
        out_specs=[pl.BlockSpec((tm, MAIN_W), lambda i: (i, 0)),
                   pl.BlockSpec((tm, AUX_W), lambda i: (i, 0))],
        out_shape=[jax.ShapeDtypeStruct((n, MAIN_W), BF16),
                   jax.ShapeDtypeStruct((n, AUX_W), F32)],
        scratch_shapes=[pltpu.VMEM((3 * GDN_HEADS, tm + CONV_PAD, GDN_DK), F32)],
        compiler_params=_cparams("arbitrary"),
        name="in_proj",
    )(x2d, w_main, w_aux, conv_w)


def _split_w_in(w_in):
    sizes = (GLA_HEADS * GLA_DK, GLA_HEADS * GLA_DK, GLA_WIDTH, GLA_WIDTH, GLA_GATE_RANK,
             GDN_HEADS * GDN_DK, GDN_HEADS * GDN_DK, GDN_WIDTH, GDN_WIDTH, GDN_HEADS, GDN_HEADS)
    offs = np.cumsum((0,) + sizes)
    seg = [w_in[:, offs[i]:offs[i + 1]] for i in range(len(sizes))]
    qa, ka, va, ra, lra, qb, kb, vb, zb, bb, ab = seg
    w_main = jnp.concatenate([qa, ka, va, ra, qb, kb, vb, zb], axis=1).astype(BF16)
    pad = jnp.zeros((w_in.shape[0], AUX_W - GLA_GATE_RANK - 2 * GDN_HEADS), w_in.dtype)
    w_aux = jnp.concatenate([lra, bb, ab, pad], axis=1).astype(BF16)
    return w_main, w_aux


_NT = (((1,), (1,)), ((), ()))
_TN = (((0,), (0,)), ((), ()))


def _gla_kernel(q_ref, k_ref, v_ref, r_ref, aux_ref, wgk_ref, bgk_ref, g_ref, o_ref, st_ref, *, ts):
    c = CHUNK
    nsub = c // SUB

    @pl.when(pl.program_id(1) == 0)
    def _():
        st_ref[...] = jnp.zeros_like(st_ref)

    pre = jnp.dot(aux_ref[...], wgk_ref[...], precision=HIGHEST, preferred_element_type=F32) + bgk_ref[...]
    gk_all = _log_sigmoid(pre) * (1.0 / GLA_GATE_NORM)

    row128 = lax.broadcasted_iota(jnp.int32, (c, 128), 0)
    lane128 = lax.broadcasted_iota(jnp.int32, (1, 128), 1)
    lane_masks = [(lane128 // GLA_DK) == hh for hh in range(2)]
    sub_valid = [row128 < SUB * (i + 1) for i in range(nsub)]
    ar = lax.broadcasted_iota(jnp.int32, (c, nsub * c), 0)
    ac = lax.broadcasted_iota(jnp.int32, (c, nsub * c), 1)
    amask = ((ac // c) == (ar // SUB)) & ((ac % c) <= ar)
    scale = GLA_DK ** -0.5
    gnorm = g_ref[...]

    nch = ts // c
    rows = [slice(ch * c, (ch + 1) * c) for ch in range(nch)]
    b_cum = _chunk_cumsum(gk_all, c)
    b_all = [b_cum[r] for r in rows]

    cp = [(ch, p) for ch in range(nch) for p in range(2)]
    qt, qd, kd, kstack, dlast = {}, {}, {}, {}, {}
    for ch, p in cp:
        lanes = slice(128 * p, 128 * (p + 1))
        bp = b_all[ch][:, lanes]
        qp = q_ref[rows[ch], lanes].astype(F32) * scale
        kp = k_ref[rows[ch], lanes].astype(F32)
        c_row = jnp.concatenate(
            [jnp.broadcast_to(bp[SUB * i:SUB * i + 1, :], (SUB, 128)) for i in range(nsub)], axis=0)
        qt[ch, p] = (qp * jnp.exp(bp - c_row)).astype(BF16)
        qd[ch, p] = qp * jnp.exp(bp)
        b_last = bp[c - 1:c, :]
        dlast[ch, p] = jnp.exp(b_last)
        kd[ch, p] = kp * jnp.exp(b_last - bp)
        kts = []
        for i in range(nsub):
            e = jnp.exp(jnp.where(sub_valid[i], bp[SUB * i:SUB * i + 1, :] - bp, 0.0))
            kts.append(jnp.where(sub_valid[i], kp * e, 0.0))
        kstack[ch, p] = jnp.concatenate(kts, axis=0)

    cph = [(ch, p, hh) for ch, p in cp for hh in range(2)]
    vh = {(ch, p, hh): v_ref[rows[ch], 128 * (2 * p + hh):128 * (2 * p + hh + 1)] for ch, p, hh in cph}
    r_mat = {(ch, p, hh): lax.dot_general(qt[ch, p], jnp.where(lane_masks[hh], kstack[ch, p], 0.0).astype(BF16),
                                          _NT, preferred_element_type=F32) for ch, p, hh in cph}
    o_intra = {k: jnp.dot(jnp.where(amask, r_mat[k], 0.0).astype(BF16), jnp.concatenate([vh[k]] * nsub, axis=0),
                          preferred_element_type=F32) for k in cph}
    upd = {(ch, p, hh): lax.dot_general(vh[ch, p, hh], jnp.where(lane_masks[hh], kd[ch, p], 0.0).astype(BF16), _TN,
                                        preferred_element_type=F32) for ch, p, hh in cph}

    st = [st_ref[p] for p in range(2)]
    for ch in range(nch):
        st_b = [s.astype(BF16) for s in st]
        o_pair = [lax.dot_general(
            jnp.concatenate([jnp.where(lane_masks[hh], qd[ch, p], 0.0).astype(BF16) for hh in range(2)], axis=0),
            st_b[p], _NT, preferred_element_type=F32) for p in range(2)]
        o_inter = {(p, hh): o_pair[p][hh * c:(hh + 1) * c] for p in range(2) for hh in range(2)}
        st = [st[p] * dlast[ch, p] + upd[ch, p, 0] + upd[ch, p, 1] for p in range(2)]
        for p in range(2):
            for hh in range(2):
                h = 2 * p + hh
                o = o_intra[ch, p, hh] + o_inter[p, hh]
                o = o * lax.rsqrt(jnp.mean(o * o, axis=-1, keepdims=True) + RMS_EPS) * gnorm
                gate = _silu(r_ref[rows[ch], 128 * h:128 * (h + 1)].astype(F32))
                o_ref[rows[ch], 128 * h:128 * (h + 1)] = (o * gate).astype(BF16)
    for p in range(2):
        st_ref[p] = st[p]


def _gla(main, aux, wgk_pad, b_gk, gla_norm_g, *, bsz, seq, ts=512):
    nt = seq // ts
    row = lambda b, t: b * nt + t
    return pl.pallas_call(
        functools.partial(_gla_kernel, ts=ts),
        grid=(bsz, nt),
        in_specs=[pl.BlockSpec((ts, 256), lambda b, t: (row(b, t), 0)),
                  pl.BlockSpec((ts, 256), lambda b, t: (row(b, t), 1)),
                  pl.BlockSpec((ts, 512), lambda b, t: (row(b, t), 1)),
                  pl.BlockSpec((ts, 512), lambda b, t: (row(b, t), 2)),
                  pl.BlockSpec((ts, AUX_W), lambda b, t: (row(b, t), 0)),
                  pl.BlockSpec((AUX_W, 256), lambda b, t: (0, 0)),
                  pl.BlockSpec((1, 256), lambda b, t: (0, 0)),
                  pl.BlockSpec((1, 128), lambda b, t: (0, 0))],
        out_specs=pl.BlockSpec((ts, GLA_WIDTH), lambda b, t: (row(b, t), 0)),
        out_shape=jax.ShapeDtypeStruct((bsz * seq, GLA_WIDTH), BF16),
        scratch_shapes=[pltpu.VMEM((2, 128, 128), F32)],
        compiler_params=_cparams("parallel", "arbitrary"),
        name="gla",
    )(main, main, main, main, aux, wgk_pad, b_gk, gla_norm_g)


def _gdn_kernel(q_ref, k_ref, v_ref, z_ref, aux_ref, alog_ref, dtb_ref, g_ref, o_ref, st_ref, *, ts):
    c = CHUNK
    nch = ts // c

    @pl.when(pl.program_id(1) == 0)
    def _():
        st_ref[...] = jnp.zeros_like(st_ref)

    aux = aux_ref[...]
    beta_full = _sigmoid(aux)
    g_full = -jnp.exp(alog_ref[...]) * _softplus(aux + dtb_ref[...])
    gam_full = _chunk_cumsum(g_full, c)
    gam_t = gam_full.T
    gnorm = g_ref[...]

    row = lax.broadcasted_iota(jnp.int32, (c, 2 * c), 0)
    lane = lax.broadcasted_iota(jnp.int32, (c, 2 * c), 1)
    left = lane < c
    col = lane % c
    causal = col <= row
    strict = col < row
    same32 = (row // 32) == (col // 32)
    same16 = (row // 16) == (col // 16)
    lvl0 = same16 & strict
    lvl1 = same32 & jnp.logical_not(same16) & strict
    lvl2 = jnp.logical_not(same32) & strict

    def pack_rows(x2):
        return jnp.where(left, x2[0:c], x2[c:2 * c])

    def pack_cols(x1):
        return jnp.where(left, x1[0:c], x1[c:2 * c])

    def halves(x):
        return jnp.where(left, x, 0.0).astype(BF16), jnp.where(left, 0.0, x).astype(BF16)

    def mmp(x, y):
        return jnp.dot(x.astype(BF16), jnp.concatenate(halves(y), axis=0), preferred_element_type=F32)

    heads = range(GDN_HEADS)
    npair = nch // 2
    items = [(h, pr) for h in heads for pr in range(npair)]
    prow = [slice(2 * c * pr, 2 * c * (pr + 1)) for pr in range(npair)]

    qn, kn, beta, gam, gam_r, kb, rhs, rhs_b, qd, kd = ([None] * GDN_HEADS for _ in range(10))
    for h in heads:
        hl = slice(128 * h, 128 * (h + 1))
        kb[h] = k_ref[:, hl]
        qn[h] = q_ref[:, hl].astype(F32)
        kn[h] = kb[h].astype(F32)
        vh = v_ref[:, hl].astype(F32)
        beta[h] = beta_full[:, AUX_BB + h:AUX_BB + h + 1]
        gam[h] = gam_full[:, AUX_AB + h:AUX_AB + h + 1]
        gam_r[h] = gam_t[AUX_AB + h:AUX_AB + h + 1, :]
        egam = jnp.exp(gam[h])
        rhs[h] = jnp.concatenate([beta[h] * vh, beta[h] * egam * kn[h]], axis=1)
        rhs_b[h] = rhs[h].astype(BF16)
        qd[h] = (qn[h] * egam).astype(BF16)
        gl_rows = jnp.concatenate(
            [jnp.broadcast_to(gam[h][c * (j + 1) - 1:c * (j + 1), :], (c, 1)) for j in range(nch)], axis=0)
        kd[h] = (kn[h] * jnp.exp(gl_rows - gam[h])).astype(BF16)

    kk = [pack_rows(lax.dot_general(kb[h][prow[pr]], kb[h][prow[pr]], _NT, preferred_element_type=F32))
          for h, pr in items]
    qk = [pack_rows(lax.dot_general(qn[h][prow[pr]].astype(BF16), kb[h][prow[pr]], _NT,
                                    preferred_element_type=F32)) for h, pr in items]
    dec = [jnp.exp(jnp.where(causal, pack_cols(gam[h][prow[pr]]) - gam_r[h][:, prow[pr]], 0.0)) for h, pr in items]
    a = [pack_cols(beta[h][prow[pr]]) * kk[i] * dec[i] for i, (h, pr) in enumerate(items)]
    attn = [halves(jnp.where(causal, qk[i] * dec[i], 0.0)) for i in range(len(items))]

    x1 = [jnp.where(lvl0, -ai, 0.0) for ai in a]
    x2 = [mmp(x, x) for x in x1]
    y = [mmp(x, xx) for x, xx in zip(x1, x2)]
    p = [x + xx + yy for x, xx, yy in zip(x1, x2, y)]
    xk = x2
    for _ in range(2):
        xk = [mmp(x, x) for x in xk]
        y = [mmp(pp, x) for pp, x in zip(p, xk)]
        p = [pp + x + yy for pp, x, yy in zip(p, xk, y)]
    for lvl in (lvl1, lvl2):
        low = [jnp.where(lvl, ai, 0.0) for ai in a]
        y = [mmp(pp, lo) for pp, lo in zip(p, low)]
        zz = [lo + yy for lo, yy in zip(low, y)]
        y = [mmp(z, pp) for z, pp in zip(zz, p)]
        p = [pp - z - yy for pp, z, yy in zip(p, zz, y)]

    us = [[None] * nch for _ in heads]
    ws = [[None] * nch for _ in heads]
    for i, (h, pr) in enumerate(items):
        for half, p_half in enumerate(halves(p[i])):
            j = 2 * pr + half
            uw = rhs[h][c * j:c * (j + 1)] + jnp.dot(p_half, rhs_b[h][prow[pr]], preferred_element_type=F32)
            us[h][j] = uw[:, :GDN_DV]
            ws[h][j] = uw[:, GDN_DV:].astype(BF16)

    s = [st_ref[h] for h in heads]
    vprev = [None] * GDN_HEADS
    for j in range(nch):
        rows = slice(c * j, c * (j + 1))
        sb = [s[h].astype(BF16) for h in heads]
        ws_qd = [jnp.dot(jnp.concatenate([ws[h][j], qd[h][rows]], axis=0), sb[h], preferred_element_type=F32)
                 for h in heads]
        vn16 = [(us[h][j] - ws_qd[h][0:c]).astype(BF16) for h in heads]
        o_inter = [ws_qd[h][c:2 * c] for h in heads]
        for h in heads:
            vpair = jnp.concatenate([vn16[h], jnp.zeros_like(vn16[h])] if j % 2 == 0 else [vprev[h], vn16[h]], axis=0)
            o = o_inter[h] + jnp.dot(attn[h * npair + j // 2][j % 2], vpair, preferred_element_type=F32)
            s[h] = jnp.exp(gam[h][c * (j + 1) - 1:c * (j + 1), :]) * s[h] + lax.dot_general(
                kd[h][rows], vn16[h], _TN, preferred_element_type=F32)
            o = o * lax.rsqrt(jnp.mean(o * o, axis=-1, keepdims=True) + RMS_EPS) * gnorm
            gate = _silu(z_ref[rows, 128 * h:128 * (h + 1)].astype(F32))
            o_ref[rows, 128 * h:128 * (h + 1)] = (o * gate).astype(BF16)
        vprev = vn16
    for h in heads:
        st_ref[h] = s[h]


def _gdn(main, aux, alog_pad, dtb_pad, gdn_norm_g, *, bsz, seq, ts=512):
    nt = seq // ts
    row = lambda b, t: b * nt + t
    hw = GDN_HEADS * GDN_DK
    return pl.pallas_call(
        functools.partial(_gdn_kernel, ts=ts),
        grid=(bsz, nt),
        in_specs=[pl.BlockSpec((ts, hw), lambda b, t: (row(b, t), 3)),
                  pl.BlockSpec((ts, hw), lambda b, t: (row(b, t), 4)),
                  pl.BlockSpec((ts, hw), lambda b, t: (row(b, t), 5)),
                  pl.BlockSpec((ts, hw), lambda b, t: (row(b, t), 6)),
                  pl.BlockSpec((ts, AUX_W), lambda b, t: (row(b, t), 0)),
                  pl.BlockSpec((1, AUX_W), lambda b, t: (0, 0)),
                  pl.BlockSpec((1, AUX_W), lambda b, t: (0, 0)),
                  pl.BlockSpec((1, GDN_DV), lambda b, t: (0, 0))],
        out_specs=pl.BlockSpec((ts, GDN_WIDTH), lambda b, t: (row(b, t), 0)),
        out_shape=jax.ShapeDtypeStruct((bsz * seq, GDN_WIDTH), BF16),
        scratch_shapes=[pltpu.VMEM((GDN_HEADS, GDN_DK, GDN_DV), F32)],
        compiler_params=_cparams("parallel", "arbitrary"),
        name="gdn",
    )(main, main, main, main, aux, alog_pad, dtb_pad, gdn_norm_g)


def _pad_aux_row(v, offset):
    return jnp.zeros((1, AUX_W), F32).at[0, offset:offset + v.shape[0]].set(v.astype(F32))


ROUTE_W = 128
R_E1, R_E2, R_RANK1, R_RANK2, R_G1, R_G2 = range(6)
ROUTER_GROUP_COL = N_EXPERTS


def _layer_norm(h, g, b):
    mu = jnp.mean(h, axis=-1, keepdims=True)
    hc = h - mu
    var = jnp.mean(hc * hc, axis=-1, keepdims=True)
    return hc * lax.rsqrt(var + LN_EPS) * g + b


def _pack_bf16_pairs(x):
    w = x.shape[1] // 2
    lo = lax.bitcast_convert_type(x[:, :w].astype(BF16).astype(F32), jnp.uint32)
    hi = lax.bitcast_convert_type(x[:, w:].astype(BF16).astype(F32), jnp.uint32)
    return (lo >> 16) | (hi & jnp.uint32(0xFFFF0000))


def _unpack_bf16_pairs(p):
    lo = lax.bitcast_convert_type(p << 16, F32)
    hi = lax.bitcast_convert_type(p & jnp.uint32(0xFFFF0000), F32)
    return lo, hi


HALF = D_MODEL // 2
TOK_ROWS = HALF // 128


def _store_token_rows(ref, packed):
    m = packed.shape[0]
    for c in range(TOK_ROWS):
        ref[pl.ds(c, m, stride=TOK_ROWS), :] = packed[:, 128 * c:128 * (c + 1)]


def _load_token_rows(ref, m):
    return jnp.concatenate([ref[pl.ds(c, m, stride=TOK_ROWS), :] for c in range(TOK_ROWS)], axis=1)


def _post_mix_kernel(oa_ref, ob_ref, x_ref, wo_ref, g_ref, b_ref, wr_ref, br_ref,
                     x1p_ref, route_ref, cnt_ref, carry_ref, *, tm):
    @pl.when(pl.program_id(0) == 0)
    def _():
        carry_ref[...] = jnp.zeros_like(carry_ref)

    y = jnp.dot(oa_ref[...], wo_ref[0:GLA_WIDTH, :], preferred_element_type=F32)
    y = y + jnp.dot(ob_ref[...], wo_ref[GLA_WIDTH:, :], preferred_element_type=F32)
    x1 = _layer_norm(ALPHA_DN * x_ref[...] + y, g_ref[...], b_ref[...])
    _store_token_rows(x1p_ref, _pack_bf16_pairs(x1))

    xh = x1.astype(BF16)
    xl = (x1 - xh.astype(F32)).astype(BF16)
    hw = jnp.dot(xh, wr_ref[...], preferred_element_type=F32)
    logits = (hw[:, :ROUTE_W] + hw[:, ROUTE_W:]
              + jnp.dot(xl, wr_ref[:, :ROUTE_W], preferred_element_type=F32) + br_ref[...])
    lane = lax.broadcasted_iota(jnp.int32, (tm, ROUTE_W), 1)
    big = jnp.int32(1 << 20)
    neg = jnp.float32(-jnp.inf)

    def first_argmax(vals):
        m = jnp.max(vals, axis=-1, keepdims=True)
        idx = jnp.min(jnp.where(vals == m, lane, big), axis=-1, keepdims=True)
        return m, idx

    is_group = (lane >= ROUTER_GROUP_COL) & (lane < ROUTER_GROUP_COL + N_GROUPS)
    gl = jnp.where(is_group, logits, neg)
    gmax, gidx = first_argmax(gl)
    p_group = 1.0 / jnp.sum(jnp.exp(gl - gmax), axis=-1, keepdims=True)
    g_sel = gidx - ROUTER_GROUP_COL
    el = jnp.where((lane // EXPERTS_PER_GROUP) == g_sel, logits, neg)
    t1, e1 = first_argmax(el)
    t2, e2 = first_argmax(jnp.where(lane == e1, neg, el))
    ex = jnp.exp(t2 - t1)
    w1 = 1.0 / (1.0 + ex)
    gate1 = p_group * w1
    gate2 = p_group * (ex * w1)

    oh1 = lane == e1
    oh2 = lane == e2
    ri = lax.broadcasted_iota(jnp.int32, (tm, tm), 0)
    ci = lax.broadcasted_iota(jnp.int32, (tm, tm), 1)
    lstrict = (ci < ri).astype(BF16)
    oh1f = oh1.astype(F32)
    oh2f = oh2.astype(F32)
    c1 = jnp.dot(lstrict, oh1f.astype(BF16), preferred_element_type=F32)
    c2 = jnp.dot(lstrict, oh2f.astype(BF16), preferred_element_type=F32)
    carry = carry_ref[...]
    tot1 = jnp.sum(oh1f, axis=0, keepdims=True)
    tot2 = jnp.sum(oh2f, axis=0, keepdims=True)
    rank1 = jnp.sum(jnp.where(oh1, c1 + carry, 0.0), axis=-1, keepdims=True)
    rank2 = jnp.sum(jnp.where(oh2, c2 + (carry + tot1), 0.0), axis=-1, keepdims=True)
    new_carry = carry + tot1 + tot2
    carry_ref[...] = new_carry
    cnt_ref[...] = new_carry

    cols = (e1.astype(F32), e2.astype(F32), rank1, rank2, gate1, gate2)
    route = jnp.zeros((tm, ROUTE_W), F32)
    for j, col in enumerate(cols):
        route = jnp.where(lane == j, col, route)
    route_ref[...] = route


def _post_mix(o_a, o_b, x2d, w_out_b, ln_g, ln_b, w_router, b_router, *, tm=512):
    n = x2d.shape[0]
    return pl.pallas_call(
        functools.partial(_post_mix_kernel, tm=tm),
        grid=(n // tm,),
        in_specs=[pl.BlockSpec((tm, GLA_WIDTH), lambda i: (i, 0)),
                  pl.BlockSpec((tm, GDN_WIDTH), lambda i: (i, 0)),
                  pl.BlockSpec((tm, D_MODEL), lambda i: (i, 0)),
                  pl.BlockSpec((GLA_WIDTH + GDN_WIDTH, D_MODEL), lambda i: (0, 0)),
                  pl.BlockSpec((1, D_MODEL), lambda i: (0, 0)),
                  pl.BlockSpec((1, D_MODEL), lambda i: (0, 0)),
                  pl.BlockSpec((D_MODEL, 2 * ROUTE_W), lambda i: (0, 0)),
                  pl.BlockSpec((1, ROUTE_W), lambda i: (0, 0))],
        out_specs=[pl.BlockSpec((tm * TOK_ROWS, 128), lambda i: (i, 0)),
                   pl.BlockSpec((tm, ROUTE_W), lambda i: (i, 0)),
                   pl.BlockSpec((1, ROUTE_W), lambda i: (0, 0))],
        out_shape=[jax.ShapeDtypeStruct((n * TOK_ROWS, 128), jnp.uint32),
                   jax.ShapeDtypeStruct((n, ROUTE_W), F32),
                   jax.ShapeDtypeStruct((1, ROUTE_W), F32)],
        scratch_shapes=[pltpu.VMEM((1, ROUTE_W), F32)],
        compiler_params=_cparams("arbitrary"),
        name="post_mix",
    )(o_a, o_b, x2d, w_out_b, ln_g, ln_b, w_router, b_router)


EXPERT_BLOCK = 256
BLOCKS_PER_STEP = 4

ROW_UNROLL = 8


def _plan_kernel(route_ref, cnt_ref, dest_ref, *, tm, tb):
    counts = cnt_ref[...].astype(jnp.int32)
    padded = ((counts + (tb - 1)) & jnp.int32(-tb)).astype(F32)
    ri = lax.broadcasted_iota(jnp.int32, (ROUTE_W, ROUTE_W), 0)
    ci = lax.broadcasted_iota(jnp.int32, (ROUTE_W, ROUTE_W), 1)
    pstart = jnp.dot(jnp.broadcast_to(padded, (8, ROUTE_W)), (ri < ci).astype(F32),
                     precision=HIGHEST, preferred_element_type=F32)[0:1, :]
    route = route_ref[...]
    lane = lax.broadcasted_iota(jnp.int32, (tm, ROUTE_W), 1).astype(F32)
    out = jnp.zeros((tm, ROUTE_W), F32)
    for k, (ce, cr) in enumerate(((R_E1, R_RANK1), (R_E2, R_RANK2))):
        off = jnp.sum(jnp.where(lane == route[:, ce:ce + 1], pstart, 0.0), axis=-1, keepdims=True)
        out = jnp.where(lane == float(k), off + route[:, cr:cr + 1], out)
    dest_ref[...] = out.T[0:8, :].astype(jnp.int32) * TOK_ROWS


def _plan(route, counts, *, tb, tm=2048):
    n = route.shape[0]
    tm = min(tm, n)
    return pl.pallas_call(
        functools.partial(_plan_kernel, tm=tm, tb=tb),
        grid=(n // tm,),
        in_specs=[pl.BlockSpec((tm, ROUTE_W), lambda i: (i, 0)),
                  pl.BlockSpec((1, ROUTE_W), lambda i: (0, 0))],
        out_specs=pl.BlockSpec((8, tm), lambda i: (0, i)),
        out_shape=jax.ShapeDtypeStruct((8, n), jnp.int32),
        compiler_params=_cparams("parallel"),
        name="plan",
    )(route, counts)


def _dispatch_kernel(d1_ref, d2_ref, last_blk_ref, nvalid_ref, xp_ref, xs_ref, zbuf, sem, zsem, *, tm, tb, nb):
    base = pl.program_id(0) * tm
    rows = tb * TOK_ROWS

    @pl.when(pl.program_id(0) == 0)
    def _():
        zbuf[...] = jnp.zeros_like(zbuf)

        def zero_copy(blk):
            return pltpu.make_async_copy(zbuf, xs_ref.at[pl.ds(pl.multiple_of(blk * rows, rows), rows), :], zsem)

        def for_each_block(fn):
            def per_expert(e, carry):
                @pl.when(last_blk_ref[e] >= 0)
                def _():
                    fn(zero_copy(last_blk_ref[e]))
                return carry

            def per_tail(blk, carry):
                fn(zero_copy(blk))
                return carry
            lax.fori_loop(0, N_EXPERTS, per_expert, 0)
            lax.fori_loop(nvalid_ref[0], nb, per_tail, 0)

        for_each_block(lambda cp: cp.start())
        for_each_block(lambda cp: cp.wait())

    def body(j, carry):
        for u in range(ROW_UNROLL):
            r = j * ROW_UNROLL + u
            for k, d_ref in enumerate((d1_ref, d2_ref)):
                dst = pl.multiple_of(d_ref[base + r], TOK_ROWS)
                pltpu.make_async_copy(xp_ref.at[pl.ds(r * TOK_ROWS, TOK_ROWS), :],
                                      xs_ref.at[pl.ds(dst, TOK_ROWS), :], sem).start(priority=k)
        return carry
    lax.fori_loop(0, tm // ROW_UNROLL, body, 0)
    for _ in range(TOP_K):
        pltpu.make_async_copy(xp_ref, xs_ref.at[pl.ds(0, tm * TOK_ROWS), :], sem).wait()


def _dispatch(dest1, dest2, last_blk, nvalid, x1p, *, nb, tb=EXPERT_BLOCK, tm=1024):
    n = x1p.shape[0] // TOK_ROWS
    grid_spec = pltpu.PrefetchScalarGridSpec(
        num_scalar_prefetch=4,
        grid=(n // tm,),
        in_specs=[pl.BlockSpec((tm * TOK_ROWS, 128), lambda i, a, b, lb, nv: (i, 0))],
        out_specs=pl.BlockSpec(memory_space=pl.ANY),
        scratch_shapes=[pltpu.VMEM((tb * TOK_ROWS, 128), jnp.uint32),
                        pltpu.SemaphoreType.DMA(()),
                        pltpu.SemaphoreType.DMA(())],
    )
    return pl.pallas_call(
        functools.partial(_dispatch_kernel, tm=tm, tb=tb, nb=nb),
        grid_spec=grid_spec,
        out_shape=jax.ShapeDtypeStruct((nb * tb * TOK_ROWS, 128), jnp.uint32),
        compiler_params=_cparams("arbitrary"),
        name="dispatch",
    )(dest1, dest2, last_blk, nvalid, x1p)


def _expert_kernel(blk_e_ref, first_ref, next_e_ref, nvalid_ref, xs_ref, wg_hbm, wu_hbm, wd_hbm, y_ref,
                   wgs, wus, wds, wgb, wub, wdb, sem, *, tb):
    rows = tb * TOK_ROWS

    def weight_copies(e):
        return [pltpu.make_async_copy(src.at[e], dst, sem.at[j])
                for j, (src, dst) in enumerate(((wg_hbm, wgs), (wu_hbm, wus), (wd_hbm, wds)))]

    @pl.when((pl.program_id(0) == 0) & (nvalid_ref[0] > 0))
    def _():
        for cp in weight_copies(blk_e_ref[0]):
            cp.start()

    for sub in range(BLOCKS_PER_STEP):
        i = pl.program_id(0) * BLOCKS_PER_STEP + sub
        xs_blk = xs_ref.at[pl.ds(sub * rows, rows), :]
        y_blk = y_ref.at[pl.ds(sub * rows, rows), :]

        @pl.when(i < nvalid_ref[0])
        def _():
            @pl.when(first_ref[i] == 1)
            def _():
                for cp in weight_copies(blk_e_ref[i]):
                    cp.wait()
                wgb[...] = wgs[...].astype(BF16)
                wub[...] = wus[...].astype(BF16)
                wdb[...] = wds[...].astype(BF16)

                @pl.when(next_e_ref[i] >= 0)
                def _():
                    for cp in weight_copies(next_e_ref[i]):
                        cp.start()

            lo, hi = _unpack_bf16_pairs(_load_token_rows(xs_blk, tb))
            xb = jnp.concatenate([lo.astype(BF16), hi.astype(BF16)], axis=1)
            hg = jnp.dot(xb, wgb[...], preferred_element_type=F32)
            hu = jnp.dot(xb, wub[...], preferred_element_type=F32)
            hid = (_silu(hg) * hu).astype(BF16)
            _store_token_rows(y_blk, _pack_bf16_pairs(jnp.dot(hid, wdb[...], preferred_element_type=F32)))

        @pl.when(i >= nvalid_ref[0])
        def _():
            y_blk[...] = jnp.zeros((rows, 128), jnp.uint32)


def _experts(blk_e, first, next_e, nvalid, xs, w_gate, w_up, w_down, *, tb=EXPERT_BLOCK):
    nb = blk_e.shape[0]
    assert nb % BLOCKS_PER_STEP == 0
    step_rows = BLOCKS_PER_STEP * tb * TOK_ROWS
    last_step = lambda nv: jnp.maximum((nv[0] - 1) // BLOCKS_PER_STEP, 0)
    grid_spec = pltpu.PrefetchScalarGridSpec(
        num_scalar_prefetch=4,
        grid=(nb // BLOCKS_PER_STEP,),
        in_specs=[pl.BlockSpec((step_rows, 128), lambda s, be, fr, ne, nv: (jnp.minimum(s, last_step(nv)), 0)),
                  pl.BlockSpec(memory_space=pl.ANY),
                  pl.BlockSpec(memory_space=pl.ANY),
                  pl.BlockSpec(memory_space=pl.ANY)],
        out_specs=pl.BlockSpec((step_rows, 128), lambda s, be, fr, ne, nv: (s, 0)),
        scratch_shapes=[pltpu.VMEM((D_MODEL, D_EXPERT), F32),
                        pltpu.VMEM((D_MODEL, D_EXPERT), F32),
                        pltpu.VMEM((D_EXPERT, D_MODEL), F32),
                        pltpu.VMEM((D_MODEL, D_EXPERT), BF16),
                        pltpu.VMEM((D_MODEL, D_EXPERT), BF16),
                        pltpu.VMEM((D_EXPERT, D_MODEL), BF16),
                        pltpu.SemaphoreType.DMA((3,))],
    )
    return pl.pallas_call(
        functools.partial(_expert_kernel, tb=tb),
        grid_spec=grid_spec,
        out_shape=jax.ShapeDtypeStruct((nb * tb * TOK_ROWS, 128), jnp.uint32),
        compiler_params=_cparams("arbitrary"),
        name="experts",
    )(blk_e, first, next_e, nvalid, xs, w_gate, w_up, w_down)


def _combine_kernel(d1_ref, d2_ref, y_hbm, x1_ref, route_ref, g_ref, b_ref, o_ref, ybuf, sem, *, tt):
    i = pl.program_id(0)
    nsteps = pl.num_programs(0)
    slot = i % 2

    def issue(tile, slot_):
        def body(j, carry):
            for u in range(ROW_UNROLL):
                r = j * ROW_UNROLL + u
                for k, d_ref in enumerate((d1_ref, d2_ref)):
                    src = pl.multiple_of(d_ref[tile * tt + r], TOK_ROWS)
                    pltpu.make_async_copy(y_hbm.at[pl.ds(src, TOK_ROWS), :],
                                          ybuf.at[slot_, k, pl.ds(r * TOK_ROWS, TOK_ROWS), :],
                                          sem.at[slot_]).start(priority=k)
            return carry
        lax.fori_loop(0, tt // ROW_UNROLL, body, 0)

    @pl.when(i == 0)
    def _():
        issue(0, 0)

    @pl.when(i + 1 < nsteps)
    def _():
        issue(i + 1, 1 - slot)

    for k in range(TOP_K):
        pltpu.make_async_copy(y_hbm.at[pl.ds(0, tt * TOK_ROWS), :], ybuf.at[slot, k], sem.at[slot]).wait()
    route = route_ref[...]
    g1 = route[:, R_G1:R_G1 + 1]
    g2 = route[:, R_G2:R_G2 + 1]
    lo1, hi1 = _unpack_bf16_pairs(_load_token_rows(ybuf.at[slot, 0], tt))
    lo2, hi2 = _unpack_bf16_pairs(_load_token_rows(ybuf.at[slot, 1], tt))
    x_lo, x_hi = _unpack_bf16_pairs(_load_token_rows(x1_ref, tt))
    h_lo = ALPHA_DN * x_lo + (g1 * lo1 + g2 * lo2)
    h_hi = ALPHA_DN * x_hi + (g1 * hi1 + g2 * hi2)
    mu = (jnp.sum(h_lo, axis=-1, keepdims=True) + jnp.sum(h_hi, axis=-1, keepdims=True)) * (1.0 / D_MODEL)
    c_lo = h_lo - mu
    c_hi = h_hi - mu
    var = (jnp.sum(c_lo * c_lo, axis=-1, keepdims=True) + jnp.sum(c_hi * c_hi, axis=-1, keepdims=True)) * (1.0 / D_MODEL)
    inv = lax.rsqrt(var + LN_EPS)
    o_ref[:, 0:HALF] = c_lo * inv * g_ref[:, 0:HALF] + b_ref[:, 0:HALF]
    o_ref[:, HALF:] = c_hi * inv * g_ref[:, HALF:] + b_ref[:, HALF:]


def _combine(dest1, dest2, ybuf, x1p, route, ln_g, ln_b, *, tt=512):
    n = x1p.shape[0] // TOK_ROWS
    grid_spec = pltpu.PrefetchScalarGridSpec(
        num_scalar_prefetch=2,
        grid=(n // tt,),
        in_specs=[pl.BlockSpec(memory_space=pl.ANY),
                  pl.BlockSpec((tt * TOK_ROWS, 128), lambda i, a, b: (i, 0)),
                  pl.BlockSpec((tt, ROUTE_W), lambda i, a, b: (i, 0)),
                  pl.BlockSpec((1, D_MODEL), lambda i, a, b: (0, 0)),
                  pl.BlockSpec((1, D_MODEL), lambda i, a, b: (0, 0))],
        out_specs=pl.BlockSpec((tt, D_MODEL), lambda i, a, b: (i, 0)),
        scratch_shapes=[pltpu.VMEM((2, TOP_K, tt * TOK_ROWS, 128), jnp.uint32),
                        pltpu.SemaphoreType.DMA((2,))],
    )
    return pl.pallas_call(
        functools.partial(_combine_kernel, tt=tt),
        grid_spec=grid_spec,
        out_shape=jax.ShapeDtypeStruct((n, D_MODEL), F32),
        compiler_params=_cparams("arbitrary"),
        name="combine",
    )(dest1, dest2, ybuf, x1p, route, ln_g, ln_b)


def _block_plan(counts_row, n, tb):
    counts = counts_row[0, :N_EXPERTS].astype(jnp.int32)
    pends = jnp.cumsum((counts + tb - 1) // tb * tb)
    nb = (n * TOP_K) // tb + N_EXPERTS
    nvalid = (pends[-1] // tb).astype(jnp.int32)
    blk = jnp.arange(nb, dtype=jnp.int32)
    blk_e = jnp.sum((pends[None, :] <= (blk * tb)[:, None]).astype(jnp.int32), axis=1)
    blk_e = jnp.minimum(blk_e, N_EXPERTS - 1)
    blk_e = jnp.where(blk < nvalid, blk_e, blk_e[jnp.maximum(nvalid - 1, 0)])
    first = jnp.concatenate([jnp.ones((1,), jnp.int32), (blk_e[1:] != blk_e[:-1]).astype(jnp.int32)])
    ids = jnp.arange(N_EXPERTS, dtype=jnp.int32)
    later = lax.cummin(jnp.where(counts > 0, ids, N_EXPERTS), axis=0, reverse=True)
    nxt = jnp.concatenate([later[1:], jnp.full((1,), N_EXPERTS, jnp.int32)])
    nxt = jnp.where(nxt < N_EXPERTS, nxt, -1)
    last_blk = jnp.where(counts > 0, pends // tb - 1, -1).astype(jnp.int32)
    return blk_e, first, nxt[blk_e], nvalid.reshape(1), last_blk


def kernel(x, w_in, w_gk_up, b_gk, conv_w, a_log, dt_bias, gla_norm_g, gdn_norm_g, w_out, ln1_g, ln1_b, w_router_group, b_router_group, w_router_expert, b_router_expert, w_gate, w_up, w_down, ln2_g, ln2_b):
    bsz, seq, d = x.shape
    n = bsz * seq
    x2d = x.reshape(n, d)
    w_main, w_aux = _split_w_in(w_in[0])
    main, aux = _in_proj(x2d, w_main, w_aux, conv_w[0].astype(F32), seq=seq)
    wgk = jnp.zeros((AUX_W, 256), F32).at[:GLA_GATE_RANK].set(w_gk_up[0])
    o_a = _gla(main, aux, wgk, b_gk[0][None], gla_norm_g[0][None], bsz=bsz, seq=seq)
    o_b = _gdn(main, aux, _pad_aux_row(a_log[0], AUX_AB), _pad_aux_row(dt_bias[0], AUX_AB),
               gdn_norm_g[0][None], bsz=bsz, seq=seq)
    rpad = jnp.zeros((d, ROUTE_W - N_EXPERTS - N_GROUPS), F32)
    w_router = jnp.concatenate([w_router_expert[0], w_router_group[0], rpad], axis=1)
    w_router_hi = w_router.astype(BF16)
    w_router = jnp.concatenate([w_router_hi, (w_router - w_router_hi.astype(F32)).astype(BF16)], axis=1)
    b_router = jnp.concatenate([b_router_expert[0], b_router_group[0], rpad[0]])[None]
    x1p, route, counts = _post_mix(o_a, o_b, x2d, w_out[0].astype(BF16), ln1_g[0][None], ln1_b[0][None],
                                       w_router, b_router)
    dest = _plan(route, counts, tb=EXPERT_BLOCK)
    blk_e, first, next_e, nvalid, last_blk = _block_plan(counts, n, EXPERT_BLOCK)
    xs = _dispatch(dest[0], dest[1], last_blk, nvalid, x1p, nb=blk_e.shape[0])
    ybuf = _experts(blk_e, first, next_e, nvalid, xs, w_gate[0], w_up[0], w_down[0])
    out = _combine(dest[0], dest[1], ybuf, x1p, route, ln2_g[0][None], ln2_b[0][None])
    return out.reshape(bsz, seq, d)
```

```python
import functools

import jax
import jax.numpy as jnp
import numpy as np
from jax import lax
from jax.experimental import pallas as pl
from jax.experimental.pallas import tpu as pltpu

F32 = jnp.float32
BF16 = jnp.bfloat16
HIGHEST = lax.Precision.HIGHEST

D_MODEL = 1024
DEPTH = 1
GLA_HEADS = 4
GLA_DK = 64
GLA_DV = 128
GLA_WIDTH = GLA_HEADS * GLA_DV
GLA_GATE_RANK = 16
GLA_GATE_NORM = 16.0
GDN_HEADS = 4
GDN_DK = 128
GDN_DV = 128
GDN_WIDTH = GDN_HEADS * GDN_DV
CONV_W = 4
CHUNK = 64
N_GROUPS = 8
EXPERTS_PER_GROUP = 8
N_EXPERTS = N_GROUPS * EXPERTS_PER_GROUP
TOP_K = 2
D_EXPERT = 512
LN_EPS = 1e-5
RMS_EPS = 1e-6
ALPHA_DN = (2.0 * DEPTH) ** 0.25

MAIN_W = 2 * GLA_HEADS * GLA_DK + 2 * GLA_WIDTH + 4 * GDN_WIDTH
AUX_W = 128
AUX_LRA = 0
AUX_BB = GLA_GATE_RANK
AUX_AB = GLA_GATE_RANK + GDN_HEADS
SUB = 16
VMEM_LIMIT = 56 * 1024 * 1024


def _cparams(*sem):
    return pltpu.CompilerParams(dimension_semantics=sem, vmem_limit_bytes=VMEM_LIMIT)


def _sigmoid(x):
    return 1.0 / (1.0 + jnp.exp(-x))


def _silu(x):
    return x * _sigmoid(x)


def _log_sigmoid(x):
    return jnp.minimum(x, 0.0) - jnp.log(1.0 + jnp.exp(-jnp.abs(x)))


def _softplus(x):
    return jnp.maximum(x, 0.0) + jnp.log(1.0 + jnp.exp(-jnp.abs(x)))


def _chunk_cumsum(x, c):
    pos = lax.broadcasted_iota(jnp.int32, x.shape, 0) % c
    shift = 1
    while shift < c:
        x = x + jnp.where(pos >= shift, pltpu.roll(x, shift, 0), 0.0)
        shift *= 2
    return x


CONV_PAD = 8
GDN_HW = GDN_HEADS * GDN_DK
GDN_QKV_BLOCK0 = (2 * GLA_HEADS * GLA_DK + 2 * GLA_WIDTH) // GDN_HW


def _in_proj_kernel(x_ref, wm_ref, wa_ref, cw_ref, main_ref, aux_ref, stage_ref, *, tm, tiles_per_seq):
    xb = x_ref[...].astype(BF16)

    @pl.when((pl.program_id(0) % tiles_per_seq) == 0)
    def _():
        stage_ref[:, 0:CONV_PAD, :] = jnp.zeros((3 * GDN_HEADS, CONV_PAD, GDN_DK), F32)

    for j in range(MAIN_W // GDN_HW):
        sl = slice(j * GDN_HW, (j + 1) * GDN_HW)
        r = jnp.dot(xb, wm_ref[:, sl], preferred_element_type=F32)
        b = j - GDN_QKV_BLOCK0
        if not 0 <= b < 3:
            main_ref[:, sl] = r.astype(BF16)
            continue
        for h in range(GDN_HEADS):
            st = stage_ref.at[b * GDN_HEADS + h]
            cols = slice(b * GDN_HW + h * GDN_DK, b * GDN_HW + (h + 1) * GDN_DK)
            st[CONV_PAD:, :] = r[:, h * GDN_DK:(h + 1) * GDN_DK]
            acc = jnp.zeros((tm, GDN_DK), F32)
            for i in range(CONV_W):
                lo = CONV_PAD - (CONV_W - 1) + i
                acc = acc + st[lo:lo + tm, :] * cw_ref[i:i + 1, cols]
            st[0:CONV_PAD, :] = st[tm:tm + CONV_PAD, :]
            yh = _silu(acc)
            if b < 2:
                yh = yh * lax.rsqrt(jnp.sum(yh * yh, axis=-1, keepdims=True) + RMS_EPS)
            if b == 0:
                yh = yh * (GDN_DK ** -0.5)
            main_ref[:, j * GDN_HW + h * GDN_DK:j * GDN_HW + (h + 1) * GDN_DK] = yh.astype(BF16)
    aux_ref[...] = jnp.dot(xb, wa_ref[...], preferred_element_type=F32)


def _in_proj(x2d, w_main, w_aux, conv_w, *, seq, tm=512):
    n = x2d.shape[0]
    return pl.pallas_call(
        functools.partial(_in_proj_kernel, tm=tm, tiles_per_seq=seq // tm),
        grid=(n // tm,),
        in_specs=[pl.BlockSpec((tm, D_MODEL), lambda i: (i, 0)),
                  pl.BlockSpec((D_MODEL, MAIN_W), lambda i: (0, 0)),
                  pl.BlockSpec((D_MODEL, AUX_W), lambda i: (0, 0)),
                  pl.BlockSpec((CONV_W, 3 * GDN_HW), lambda i: (0, 0))],
        out_specs=[pl.BlockSpec((tm, MAIN_W), lambda i: (i, 0)),
                   pl.BlockSpec((tm, AUX_W), lambda i: (i, 0))],
        out_shape=[jax.ShapeDtypeStruct((n, MAIN_W), BF16),
                   jax.ShapeDtypeStruct((n, AUX_W), F32)],
        scratch_shapes=[pltpu.VMEM((3 * GDN_HEADS, tm + CONV_PAD, GDN_DK), F32)],
        compiler_params=_cparams("arbitrary"),
        name="in_proj",
    )(x2d, w_main, w_aux, conv_w)


def _split_w_in(w_in):
    sizes = (GLA_HEADS * GLA_DK, GLA_HEADS * GLA_DK, GLA_WIDTH, GLA_WIDTH, GLA_GATE_RANK,
             GDN_HEADS * GDN_DK, GDN_HEADS * GDN_DK, GDN_WIDTH, GDN_WIDTH, GDN_HEADS, GDN_HEADS)
    offs = np.cumsum((0,) + sizes)
    seg = [w_in[:, offs[i]:offs[i + 1]] for i in range(len(sizes))]
    qa, ka, va, ra, lra, qb, kb, vb, zb, bb, ab = seg
    w_main = jnp.concatenate([qa, ka, va, ra, qb, kb, vb, zb], axis=1).astype(BF16)
    pad = jnp.zeros((w_in.shape[0], AUX_W - GLA_GATE_RANK - 2 * GDN_HEADS), w_in.dtype)
    w_aux = jnp.concatenate([lra, bb, ab, pad], axis=1).astype(BF16)
    return w_main, w_aux


_NT = (((1,), (1,)), ((), ()))
_TN = (((0,), (0,)), ((), ()))


def _gla_kernel(q_ref, k_ref, v_ref, r_ref, aux_ref, wgk_ref, bgk_ref, g_ref, o_ref, st_ref, *, ts):
    c = CHUNK
    nsub = c // SUB

    @pl.when(pl.program_id(1) == 0)
    def _():
        st_ref[...] = jnp.zeros_like(st_ref)

    pre = jnp.dot(aux_ref[...], wgk_ref[...], precision=HIGHEST, preferred_element_type=F32) + bgk_ref[...]
    gk_all = _log_sigmoid(pre) * (1.0 / GLA_GATE_NORM)

    row128 = lax.broadcasted_iota(jnp.int32, (c, 128), 0)
    lane128 = lax.broadcasted_iota(jnp.int32, (1, 128), 1)
    lane_masks = [(lane128 // GLA_DK) == hh for hh in range(2)]
    sub_valid = [row128 < SUB * (i + 1) for i in range(nsub)]
    ar = lax.broadcasted_iota(jnp.int32, (c, nsub * c), 0)
    ac = lax.broadcasted_iota(jnp.int32, (c, nsub * c), 1)
    amask = ((ac // c) == (ar // SUB)) & ((ac % c) <= ar)
    scale = GLA_DK ** -0.5
    gnorm = g_ref[...]

    nch = ts // c
    rows = [slice(ch * c, (ch + 1) * c) for ch in range(nch)]
    b_cum = _chunk_cumsum(gk_all, c)
    b_all = [b_cum[r] for r in rows]

    cp = [(ch, p) for ch in range(nch) for p in range(2)]
    qt, qd, kd, kstack, dlast = {}, {}, {}, {}, {}
    for ch, p in cp:
        lanes = slice(128 * p, 128 * (p + 1))
        bp = b_all[ch][:, lanes]
        qp = q_ref[rows[ch], lanes].astype(F32) * scale
        kp = k_ref[rows[ch], lanes].astype(F32)
        c_row = jnp.concatenate(
            [jnp.broadcast_to(bp[SUB * i:SUB * i + 1, :], (SUB, 128)) for i in range(nsub)], axis=0)
        qt[ch, p] = (qp * jnp.exp(bp - c_row)).astype(BF16)
        qd[ch, p] = qp * jnp.exp(bp)
        b_last = bp[c - 1:c, :]
        dlast[ch, p] = jnp.exp(b_last)
        kd[ch, p] = kp * jnp.exp(b_last - bp)
        kts = []
        for i in range(nsub):
            e = jnp.exp(jnp.where(sub_valid[i], bp[SUB * i:SUB * i + 1, :] - bp, 0.0))
            kts.append(jnp.where(sub_valid[i], kp * e, 0.0))
        kstack[ch, p] = jnp.concatenate(kts, axis=0)

    cph = [(ch, p, hh) for ch, p in cp for hh in range(2)]
    vh = {(ch, p, hh): v_ref[rows[ch], 128 * (2 * p + hh):128 * (2 * p + hh + 1)] for ch, p, hh in cph}
    r_mat = {(ch, p, hh): lax.dot_general(qt[ch, p], jnp.where(lane_masks[hh], kstack[ch, p], 0.0).astype(BF16),
                                          _NT, preferred_element_type=F32) for ch, p, hh in cph}
    o_intra = {k: jnp.dot(jnp.where(amask, r_mat[k], 0.0).astype(BF16), jnp.concatenate([vh[k]] * nsub, axis=0),
                          preferred_element_type=F32) for k in cph}
    upd = {(ch, p, hh): lax.dot_general(vh[ch, p, hh], jnp.where(lane_masks[hh], kd[ch, p], 0.0).astype(BF16), _TN,
                                        preferred_element_type=F32) for ch, p, hh in cph}

    st = [st_ref[p] for p in range(2)]
    for ch in range(nch):
        st_b = [s.astype(BF16) for s in st]
        o_pair = [lax.dot_general(
            jnp.concatenate([jnp.where(lane_masks[hh], qd[ch, p], 0.0).astype(BF16) for hh in range(2)], axis=0),
            st_b[p], _NT, preferred_element_type=F32) for p in range(2)]
        o_inter = {(p, hh): o_pair[p][hh * c:(hh + 1) * c] for p in range(2) for hh in range(2)}
        st = [st[p] * dlast[ch, p] + upd[ch, p, 0] + upd[ch, p, 1] for p in range(2)]
        for p in range(2):
            for hh in range(2):
                h = 2 * p + hh
                o = o_intra[ch, p, hh] + o_inter[p, hh]
                o = o * lax.rsqrt(jnp.mean(o * o, axis=-1, keepdims=True) + RMS_EPS) * gnorm
                gate = _silu(r_ref[rows[ch], 128 * h:128 * (h + 1)].astype(F32))
                o_ref[rows[ch], 128 * h:128 * (h + 1)] = (o * gate).astype(BF16)
    for p in range(2):
        st_ref[p] = st[p]


def _gla(main, aux, wgk_pad, b_gk, gla_norm_g, *, bsz, seq, ts=512):
    nt = seq // ts
    row = lambda b, t: b * nt + t
    return pl.pallas_call(
        functools.partial(_gla_kernel, ts=ts),
        grid=(bsz, nt),
        in_specs=[pl.BlockSpec((ts, 256), lambda b, t: (row(b, t), 0)),
                  pl.BlockSpec((ts, 256), lambda b, t: (row(b, t), 1)),
                  pl.BlockSpec((ts, 512), lambda b, t: (row(b, t), 1)),
                  pl.BlockSpec((ts, 512), lambda b, t: (row(b, t), 2)),
                  pl.BlockSpec((ts, AUX_W), lambda b, t: (row(b, t), 0)),
                  pl.BlockSpec((AUX_W, 256), lambda b, t: (0, 0)),
                  pl.BlockSpec((1, 256), lambda b, t: (0, 0)),
                  pl.BlockSpec((1, 128), lambda b, t: (0, 0))],
        out_specs=pl.BlockSpec((ts, GLA_WIDTH), lambda b, t: (row(b, t), 0)),
        out_shape=jax.ShapeDtypeStruct((bsz * seq, GLA_WIDTH), BF16),
        scratch_shapes=[pltpu.VMEM((2, 128, 128), F32)],
        compiler_params=_cparams("parallel", "arbitrary"),
        name="gla",
    )(main, main, main, main, aux, wgk_pad, b_gk, gla_norm_g)


def _gdn_kernel(q_ref, k_ref, v_ref, z_ref, aux_ref, alog_ref, dtb_ref, g_ref, o_ref, st_ref, *, ts):
    c = CHUNK
    nch = ts // c

    @pl.when(pl.program_id(1) == 0)
    def _():
        st_ref[...] = jnp.zeros_like(st_ref)

    aux = aux_ref[...]
    beta_full = _sigmoid(aux)
    g_full = -jnp.exp(alog_ref[...]) * _softplus(aux + dtb_ref[...])
    gam_full = _chunk_cumsum(g_full, c)
    gam_t = gam_full.T
    gnorm = g_ref[...]

    row = lax.broadcasted_iota(jnp.int32, (c, 2 * c), 0)
    lane = lax.broadcasted_iota(jnp.int32, (c, 2 * c), 1)
    left = lane < c
    col = lane % c
    causal = col <= row
    strict = col < row
    same32 = (row // 32) == (col // 32)
    same16 = (row // 16) == (col // 16)
    lvl0 = same16 & strict
    lvl1 = same32 & jnp.logical_not(same16) & strict
    lvl2 = jnp.logical_not(same32) & strict

    def pack_rows(x2):
        return jnp.where(left, x2[0:c], x2[c:2 * c])

    def pack_cols(x1):
        return jnp.where(left, x1[0:c], x1[c:2 * c])

    def halves(x):
        return jnp.where(left, x, 0.0).astype(BF16), jnp.where(left, 0.0, x).astype(BF16)

    def mmp(x, y):
        return jnp.dot(x.astype(BF16), jnp.concatenate(halves(y), axis=0), preferred_element_type=F32)

    heads = range(GDN_HEADS)
    npair = nch // 2
    items = [(h, pr) for h in heads for pr in range(npair)]
    prow = [slice(2 * c * pr, 2 * c * (pr + 1)) for pr in range(npair)]

    qn, kn, beta, gam, gam_r, kb, rhs, rhs_b, qd, kd = ([None] * GDN_HEADS for _ in range(10))
    for h in heads:
        hl = slice(128 * h, 128 * (h + 1))
        kb[h] = k_ref[:, hl]
        qn[h] = q_ref[:, hl].astype(F32)
        kn[h] = kb[h].astype(F32)
        vh = v_ref[:, hl].astype(F32)
        beta[h] = beta_full[:, AUX_BB + h:AUX_BB + h + 1]
        gam[h] = gam_full[:, AUX_AB + h:AUX_AB + h + 1]
        gam_r[h] = gam_t[AUX_AB + h:AUX_AB + h + 1, :]
        egam = jnp.exp(gam[h])
        rhs[h] = jnp.concatenate([beta[h] * vh, beta[h] * egam * kn[h]], axis=1)
        rhs_b[h] = rhs[h].astype(BF16)
        qd[h] = (qn[h] * egam).astype(BF16)
        gl_rows = jnp.concatenate(
            [jnp.broadcast_to(gam[h][c * (j + 1) - 1:c * (j + 1), :], (c, 1)) for j in range(nch)], axis=0)
        kd[h] = (kn[h] * jnp.exp(gl_rows - gam[h])).astype(BF16)

    kk = [pack_rows(lax.dot_general(kb[h][prow[pr]], kb[h][prow[pr]], _NT, preferred_element_type=F32))
          for h, pr in items]
    qk = [pack_rows(lax.dot_general(qn[h][prow[pr]].astype(BF16), kb[h][prow[pr]], _NT,
                                    preferred_element_type=F32)) for h, pr in items]
    dec = [jnp.exp(jnp.where(causal, pack_cols(gam[h][prow[pr]]) - gam_r[h][:, prow[pr]], 0.0)) for h, pr in items]
    a = [pack_cols(beta[h][prow[pr]]) * kk[i] * dec[i] for i, (h, pr) in enumerate(items)]
    attn = [halves(jnp.where(causal, qk[i] * dec[i], 0.0)) for i in range(len(items))]

    x1 = [jnp.where(lvl0, -ai, 0.0) for ai in a]
    x2 = [mmp(x, x) for x in x1]
    y = [mmp(x, xx) for x, xx in zip(x1, x2)]
    p = [x + xx + yy for x, xx, yy in zip(x1, x2, y)]
    xk = x2
    for _ in range(2):
        xk = [mmp(x, x) for x in xk]
        y = [mmp(pp, x) for pp, x in zip(p, xk)]
        p = [pp + x + yy for pp, x, yy in zip(p, xk, y)]
    for lvl in (lvl1, lvl2):
        low = [jnp.where(lvl, ai, 0.0) for ai in a]
        y = [mmp(pp, lo) for pp, lo in zip(p, low)]
        zz = [lo + yy for lo, yy in zip(low, y)]
        y = [mmp(z, pp) for z, pp in zip(zz, p)]
        p = [pp - z - yy for pp, z, yy in zip(p, zz, y)]

    us = [[None] * nch for _ in heads]
    ws = [[None] * nch for _ in heads]
    for i, (h, pr) in enumerate(items):
        for half, p_half in enumerate(halves(p[i])):
            j = 2 * pr + half
            uw = rhs[h][c * j:c * (j + 1)] + jnp.dot(p_half, rhs_b[h][prow[pr]], preferred_element_type=F32)
            us[h][j] = uw[:, :GDN_DV]
            ws[h][j] = uw[:, GDN_DV:].astype(BF16)

    s = [st_ref[h] for h in heads]
    vprev = [None] * GDN_HEADS
    for j in range(nch):
        rows = slice(c * j, c * (j + 1))
        sb = [s[h].astype(BF16) for h in heads]
        ws_qd = [jnp.dot(jnp.concatenate([ws[h][j], qd[h][rows]], axis=0), sb[h], preferred_element_type=F32)
                 for h in heads]
        vn16 = [(us[h][j] - ws_qd[h][0:c]).astype(BF16) for h in heads]
        o_inter = [ws_qd[h][c:2 * c] for h in heads]
        for h in heads:
            vpair = jnp.concatenate([vn16[h], jnp.zeros_like(vn16[h])] if j % 2 == 0 else [vprev[h], vn16[h]], axis=0)
            o = o_inter[h] + jnp.dot(attn[h * npair + j // 2][j % 2], vpair, preferred_element_type=F32)
            s[h] = jnp.exp(gam[h][c * (j + 1) - 1:c * (j + 1), :]) * s[h] + lax.dot_general(
                kd[h][rows], vn16[h], _TN, preferred_element_type=F32)
            o = o * lax.rsqrt(jnp.mean(o * o, axis=-1, keepdims=True) + RMS_EPS) * gnorm
            gate = _silu(z_ref[rows, 128 * h:128 * (h + 1)].astype(F32))
            o_ref[rows, 128 * h:128 * (h + 1)] = (o * gate).astype(BF16)
        vprev = vn16
    for h in heads:
        st_ref[h] = s[h]


def _gdn(main, aux, alog_pad, dtb_pad, gdn_norm_g, *, bsz, seq, ts=512):
    nt = seq // ts
    row = lambda b, t: b * nt + t
    hw = GDN_HEADS * GDN_DK
    return pl.pallas_call(
        functools.partial(_gdn_kernel, ts=ts),
        grid=(bsz, nt),
        in_specs=[pl.BlockSpec((ts, hw), lambda b, t: (row(b, t), 3)),
                  pl.BlockSpec((ts, hw), lambda b, t: (row(b, t), 4)),
                  pl.BlockSpec((ts, hw), lambda b, t: (row(b, t), 5)),
                  pl.BlockSpec((ts, hw), lambda b, t: (row(b, t), 6)),
                  pl.BlockSpec((ts, AUX_W), lambda b, t: (row(b, t), 0)),
                  pl.BlockSpec((1, AUX_W), lambda b, t: (0, 0)),
                  pl.BlockSpec((1, AUX_W), lambda b, t: (0, 0)),
                  pl.BlockSpec((1, GDN_DV), lambda b, t: (0, 0))],
        out_specs=pl.BlockSpec((ts, GDN_WIDTH), lambda b, t: (row(b, t), 0)),
        out_shape=jax.ShapeDtypeStruct((bsz * seq, GDN_WIDTH), BF16),
        scratch_shapes=[pltpu.VMEM((GDN_HEADS, GDN_DK, GDN_DV), F32)],
        compiler_params=_cparams("parallel", "arbitrary"),
        name="gdn",
    )(main, main, main, main, aux, alog_pad, dtb_pad, gdn_norm_g)


def _pad_aux_row(v, offset):
    return jnp.zeros((1, AUX_W), F32).at[0, offset:offset + v.shape[0]].set(v.astype(F32))


ROUTE_W = 128
R_E1, R_E2, R_RANK1, R_RANK2, R_G1, R_G2 = range(6)
ROUTER_GROUP_COL = N_EXPERTS
POST_MIX_SPLIT = 4


def _layer_norm(h, g, b):
    mu = jnp.mean(h, axis=-1, keepdims=True)
    hc = h - mu
    var = jnp.mean(hc * hc, axis=-1, keepdims=True)
    return hc * lax.rsqrt(var + LN_EPS) * g + b


def _pack_bf16_pairs(x):
    w = x.shape[1] // 2
    lo = lax.bitcast_convert_type(x[:, :w].astype(BF16).astype(F32), jnp.uint32)
    hi = lax.bitcast_convert_type(x[:, w:].astype(BF16).astype(F32), jnp.uint32)
    return (lo >> 16) | (hi & jnp.uint32(0xFFFF0000))


def _unpack_bf16_pairs(p):
    lo = lax.bitcast_convert_type(p << 16, F32)
    hi = lax.bitcast_convert_type(p & jnp.uint32(0xFFFF0000), F32)
    return lo, hi


HALF = D_MODEL // 2
TOK_ROWS = HALF // 128


def _store_token_rows(ref, packed):
    m = packed.shape[0]
    for c in range(TOK_ROWS):
        ref[pl.ds(c, m, stride=TOK_ROWS), :] = packed[:, 128 * c:128 * (c + 1)]


def _load_token_rows(ref, m):
    return jnp.concatenate([ref[pl.ds(c, m, stride=TOK_ROWS), :] for c in range(TOK_ROWS)], axis=1)


def _post_mix_kernel(oa_ref, ob_ref, x_ref, wo_ref, g_ref, b_ref, wr_ref, br_ref,
                     x1p_ref, route_ref, gates_ref, cnt_ref, carry_ref, *, tm):
    @pl.when(pl.program_id(0) == 0)
    def _():
        carry_ref[...] = jnp.zeros_like(carry_ref)

    m = tm // POST_MIX_SPLIT
    groups = range(POST_MIX_SPLIT)
    rs = [slice(h * m, (h + 1) * m) for h in groups]
    y = [jnp.dot(oa_ref[rs[h], :], wo_ref[0:GLA_WIDTH, :], preferred_element_type=F32)
         + jnp.dot(ob_ref[rs[h], :], wo_ref[GLA_WIDTH:, :], preferred_element_type=F32) for h in groups]
    x1 = [_layer_norm(ALPHA_DN * x_ref[rs[h], :] + y[h], g_ref[...], b_ref[...]) for h in groups]
    for h in groups:
        _store_token_rows(x1p_ref.at[pl.ds(h * m * TOK_ROWS, m * TOK_ROWS), :], _pack_bf16_pairs(x1[h]))

    xh = [x.astype(BF16) for x in x1]
    xl = [(x - xb.astype(F32)).astype(BF16) for x, xb in zip(x1, xh)]
    hw = [lax.dot_general(wr_ref[...], xb, _NT, preferred_element_type=F32) for xb in xh]
    lw = [lax.dot_general(wr_ref[0:ROUTE_W, :], xb, _NT, preferred_element_type=F32) for xb in xl]
    logits = [a[:ROUTE_W] + a[ROUTE_W:] + b + br_ref[...] for a, b in zip(hw, lw)]
    sub = lax.broadcasted_iota(jnp.int32, (ROUTE_W, m), 0)
    big = jnp.int32(1 << 20)
    neg = jnp.float32(-jnp.inf)

    def first_argmax(vals):
        mx = jnp.max(vals, axis=0, keepdims=True)
        idx = jnp.min(jnp.where(vals == mx, sub, big), axis=0, keepdims=True)
        return mx, idx

    is_group = (sub >= ROUTER_GROUP_COL) & (sub < ROUTER_GROUP_COL + N_GROUPS)
    gl = [jnp.where(is_group, lg, neg) for lg in logits]
    gtop = [first_argmax(v) for v in gl]
    p_group = [1.0 / jnp.sum(jnp.exp(v - t[0]), axis=0, keepdims=True) for v, t in zip(gl, gtop)]
    el = [jnp.where((sub // EXPERTS_PER_GROUP) == (t[1] - ROUTER_GROUP_COL), lg, neg) for lg, t in zip(logits, gtop)]
    top1 = [first_argmax(v) for v in el]
    top2 = [first_argmax(jnp.where(sub == t[1], neg, v)) for v, t in zip(el, top1)]
    ex = [jnp.exp(b[0] - a[0]) for a, b in zip(top1, top2)]
    w1 = [1.0 / (1.0 + e) for e in ex]
    gate1 = [p * w for p, w in zip(p_group, w1)]
    gate2 = [p * (e * w) for p, e, w in zip(p_group, ex, w1)]

    oh1 = [sub == t[1] for t in top1]
    oh2 = [sub == t[1] for t in top2]
    ri = lax.broadcasted_iota(jnp.int32, (m, m), 0)
    ci = lax.broadcasted_iota(jnp.int32, (m, m), 1)
    ustrict = (ri < ci).astype(BF16)
    oh1f = [o.astype(F32) for o in oh1]
    oh2f = [o.astype(F32) for o in oh2]
    c1 = [jnp.dot(o.astype(BF16), ustrict, preferred_element_type=F32) for o in oh1f]
    c2 = [jnp.dot(o.astype(BF16), ustrict, preferred_element_type=F32) for o in oh2f]
    tot1 = [jnp.sum(o, axis=1, keepdims=True) for o in oh1f]
    tot2 = [jnp.sum(o, axis=1, keepdims=True) for o in oh2f]
    carry = carry_ref[...]
    eye8 = jnp.eye(8, ROUTE_W, dtype=F32)
    zero = jnp.zeros((1, m), F32)
    for h in groups:
        rank1 = jnp.sum(jnp.where(oh1[h], c1[h] + carry, 0.0), axis=0, keepdims=True)
        rank2 = jnp.sum(jnp.where(oh2[h], c2[h] + (carry + tot1[h]), 0.0), axis=0, keepdims=True)
        carry = carry + tot1[h] + tot2[h]
        route_t = jnp.concatenate([top1[h][1].astype(F32), top2[h][1].astype(F32), rank1, rank2,
                                   gate1[h], gate2[h], zero, zero], axis=0)
        route_ref[:, rs[h]] = route_t
        gates_ref[rs[h], :] = lax.dot_general(route_t, eye8, _TN, precision=HIGHEST, preferred_element_type=F32)
    carry_ref[...] = carry
    cnt_ref[...] = carry


def _post_mix(o_a, o_b, x2d, w_out_b, ln_g, ln_b, w_router, b_router, *, tm=1024):
    n = x2d.shape[0]
    return pl.pallas_call(
        functools.partial(_post_mix_kernel, tm=tm),
        grid=(n // tm,),
        in_specs=[pl.BlockSpec((tm, GLA_WIDTH), lambda i: (i, 0)),
                  pl.BlockSpec((tm, GDN_WIDTH), lambda i: (i, 0)),
                  pl.BlockSpec((tm, D_MODEL), lambda i: (i, 0)),
                  pl.BlockSpec((GLA_WIDTH + GDN_WIDTH, D_MODEL), lambda i: (0, 0)),
                  pl.BlockSpec((1, D_MODEL), lambda i: (0, 0)),
                  pl.BlockSpec((1, D_MODEL), lambda i: (0, 0)),
                  pl.BlockSpec((2 * ROUTE_W, D_MODEL), lambda i: (0, 0)),
                  pl.BlockSpec((ROUTE_W, 1), lambda i: (0, 0))],
        out_specs=[pl.BlockSpec((tm * TOK_ROWS, 128), lambda i: (i, 0)),
                   pl.BlockSpec((8, tm), lambda i: (0, i)),
                   pl.BlockSpec((tm, ROUTE_W), lambda i: (i, 0)),
                   pl.BlockSpec((ROUTE_W, 1), lambda i: (0, 0))],
        out_shape=[jax.ShapeDtypeStruct((n * TOK_ROWS, 128), jnp.uint32),
                   jax.ShapeDtypeStruct((8, n), F32),
                   jax.ShapeDtypeStruct((n, ROUTE_W), F32),
                   jax.ShapeDtypeStruct((ROUTE_W, 1), F32)],
        scratch_shapes=[pltpu.VMEM((ROUTE_W, 1), F32)],
        compiler_params=_cparams("arbitrary"),
        name="post_mix",
    )(o_a, o_b, x2d, w_out_b, ln_g, ln_b, w_router, b_router)


EXPERT_BLOCK = 256
BLOCKS_PER_STEP = 4

ROW_UNROLL = 8


def _plan_kernel(route_ref, cnt_ref, dest_ref, *, tm, tb):
    counts = cnt_ref[...].astype(jnp.int32)
    padded = jnp.broadcast_to(((counts + (tb - 1)) & jnp.int32(-tb)).astype(F32), (ROUTE_W, ROUTE_W))
    pstart = (_chunk_cumsum(padded, ROUTE_W) - padded)[:, 0:1]
    route = route_ref[...]
    sub = lax.broadcasted_iota(jnp.int32, (ROUTE_W, tm), 0).astype(F32)
    rows = []
    for ce, cr in ((R_E1, R_RANK1), (R_E2, R_RANK2)):
        off = jnp.sum(jnp.where(sub == route[ce:ce + 1, :], pstart, 0.0), axis=0, keepdims=True)
        rows.append(off + route[cr:cr + 1, :])
    rows.extend([jnp.zeros((1, tm), F32)] * (8 - len(rows)))
    dest_ref[...] = jnp.concatenate(rows, axis=0).astype(jnp.int32) * TOK_ROWS


def _plan(route, counts, *, tb, tm=2048):
    n = route.shape[1]
    tm = min(tm, n)
    return pl.pallas_call(
        functools.partial(_plan_kernel, tm=tm, tb=tb),
        grid=(n // tm,),
        in_specs=[pl.BlockSpec((8, tm), lambda i: (0, i)),
                  pl.BlockSpec((ROUTE_W, 1), lambda i: (0, 0))],
        out_specs=pl.BlockSpec((8, tm), lambda i: (0, i)),
        out_shape=jax.ShapeDtypeStruct((8, n), jnp.int32),
        compiler_params=_cparams("parallel"),
        name="plan",
    )(route, counts)


def _dispatch_kernel(d1_ref, d2_ref, last_blk_ref, nvalid_ref, xp_ref, xs_ref, zbuf, sem, zsem, *, tm, tb, nb):
    base = pl.program_id(0) * tm
    rows = tb * TOK_ROWS

    @pl.when(pl.program_id(0) == 0)
    def _():
        zbuf[...] = jnp.zeros_like(zbuf)

        def zero_copy(blk):
            return pltpu.make_async_copy(zbuf, xs_ref.at[pl.ds(pl.multiple_of(blk * rows, rows), rows), :], zsem)

        def for_each_block(fn):
            def per_expert(e, carry):
                @pl.when(last_blk_ref[e] >= 0)
                def _():
                    fn(zero_copy(last_blk_ref[e]))
                return carry

            def per_tail(blk, carry):
                fn(zero_copy(blk))
                return carry
            lax.fori_loop(0, N_EXPERTS, per_expert, 0)
            lax.fori_loop(nvalid_ref[0], nb, per_tail, 0)

        for_each_block(lambda cp: cp.start())
        for_each_block(lambda cp: cp.wait())

    def body(j, carry):
        for u in range(ROW_UNROLL):
            r = j * ROW_UNROLL + u
            for k, d_ref in enumerate((d1_ref, d2_ref)):
                dst = pl.multiple_of(d_ref[base + r], TOK_ROWS)
                pltpu.make_async_copy(xp_ref.at[pl.ds(r * TOK_ROWS, TOK_ROWS), :],
                                      xs_ref.at[pl.ds(dst, TOK_ROWS), :], sem).start(priority=k)
        return carry
    lax.fori_loop(0, tm // ROW_UNROLL, body, 0)
    for _ in range(TOP_K):
        pltpu.make_async_copy(xp_ref, xs_ref.at[pl.ds(0, tm * TOK_ROWS), :], sem).wait()


def _dispatch(dest1, dest2, last_blk, nvalid, x1p, *, nb, tb=EXPERT_BLOCK, tm=1024):
    n = x1p.shape[0] // TOK_ROWS
    grid_spec = pltpu.PrefetchScalarGridSpec(
        num_scalar_prefetch=4,
        grid=(n // tm,),
        in_specs=[pl.BlockSpec((tm * TOK_ROWS, 128), lambda i, a, b, lb, nv: (i, 0))],
        out_specs=pl.BlockSpec(memory_space=pl.ANY),
        scratch_shapes=[pltpu.VMEM((tb * TOK_ROWS, 128), jnp.uint32),
                        pltpu.SemaphoreType.DMA(()),
                        pltpu.SemaphoreType.DMA(())],
    )
    return pl.pallas_call(
        functools.partial(_dispatch_kernel, tm=tm, tb=tb, nb=nb),
        grid_spec=grid_spec,
        out_shape=jax.ShapeDtypeStruct((nb * tb * TOK_ROWS, 128), jnp.uint32),
        compiler_params=_cparams("arbitrary"),
        name="dispatch",
    )(dest1, dest2, last_blk, nvalid, x1p)


def _expert_kernel(blk_e_ref, first_ref, next_e_ref, nvalid_ref, xs_ref, wg_hbm, wu_hbm, wd_hbm, y_ref,
                   wgs, wus, wds, wgb, wub, wdb, sem, *, tb):
    rows = tb * TOK_ROWS

    def weight_copies(e):
        return [pltpu.make_async_copy(src.at[e], dst, sem.at[j])
                for j, (src, dst) in enumerate(((wg_hbm, wgs), (wu_hbm, wus), (wd_hbm, wds)))]

    @pl.when((pl.program_id(0) == 0) & (nvalid_ref[0] > 0))
    def _():
        for cp in weight_copies(blk_e_ref[0]):
            cp.start()

    for sub in range(BLOCKS_PER_STEP):
        i = pl.program_id(0) * BLOCKS_PER_STEP + sub
        xs_blk = xs_ref.at[pl.ds(sub * rows, rows), :]
        y_blk = y_ref.at[pl.ds(sub * rows, rows), :]

        @pl.when(i < nvalid_ref[0])
        def _():
            @pl.when(first_ref[i] == 1)
            def _():
                for cp in weight_copies(blk_e_ref[i]):
                    cp.wait()
                wgb[...] = wgs[...].astype(BF16)
                wub[...] = wus[...].astype(BF16)
                wdb[...] = wds[...].astype(BF16)

                @pl.when(next_e_ref[i] >= 0)
                def _():
                    for cp in weight_copies(next_e_ref[i]):
                        cp.start()

            lo, hi = _unpack_bf16_pairs(_load_token_rows(xs_blk, tb))
            xb = jnp.concatenate([lo.astype(BF16), hi.astype(BF16)], axis=1)
            hg = jnp.dot(xb, wgb[...], preferred_element_type=F32)
            hu = jnp.dot(xb, wub[...], preferred_element_type=F32)
            hid = (_silu(hg) * hu).astype(BF16)
            _store_token_rows(y_blk, _pack_bf16_pairs(jnp.dot(hid, wdb[...], preferred_element_type=F32)))

        @pl.when(i >= nvalid_ref[0])
        def _():
            y_blk[...] = jnp.zeros((rows, 128), jnp.uint32)


def _experts(blk_e, first, next_e, nvalid, xs, w_gate, w_up, w_down, *, tb=EXPERT_BLOCK):
    nb = blk_e.shape[0]
    assert nb % BLOCKS_PER_STEP == 0
    step_rows = BLOCKS_PER_STEP * tb * TOK_ROWS
    last_step = lambda nv: jnp.maximum((nv[0] - 1) // BLOCKS_PER_STEP, 0)
    grid_spec = pltpu.PrefetchScalarGridSpec(
        num_scalar_prefetch=4,
        grid=(nb // BLOCKS_PER_STEP,),
        in_specs=[pl.BlockSpec((step_rows, 128), lambda s, be, fr, ne, nv: (jnp.minimum(s, last_step(nv)), 0)),
                  pl.BlockSpec(memory_space=pl.ANY),
                  pl.BlockSpec(memory_space=pl.ANY),
                  pl.BlockSpec(memory_space=pl.ANY)],
        out_specs=pl.BlockSpec((step_rows, 128), lambda s, be, fr, ne, nv: (s, 0)),
        scratch_shapes=[pltpu.VMEM((D_MODEL, D_EXPERT), F32),
                        pltpu.VMEM((D_MODEL, D_EXPERT), F32),
                        pltpu.VMEM((D_EXPERT, D_MODEL), F32),
                        pltpu.VMEM((D_MODEL, D_EXPERT), BF16),
                        pltpu.VMEM((D_MODEL, D_EXPERT), BF16),
                        pltpu.VMEM((D_EXPERT, D_MODEL), BF16),
                        pltpu.SemaphoreType.DMA((3,))],
    )
    return pl.pallas_call(
        functools.partial(_expert_kernel, tb=tb),
        grid_spec=grid_spec,
        out_shape=jax.ShapeDtypeStruct((nb * tb * TOK_ROWS, 128), jnp.uint32),
        compiler_params=_cparams("arbitrary"),
        name="experts",
    )(blk_e, first, next_e, nvalid, xs, w_gate, w_up, w_down)


def _combine_kernel(d1_ref, d2_ref, y_hbm, x1_ref, route_ref, g_ref, b_ref, o_ref, ybuf, sem, *, tt):
    i = pl.program_id(0)
    nsteps = pl.num_programs(0)
    slot = i % 2

    def issue(tile, slot_):
        def body(j, carry):
            for u in range(ROW_UNROLL):
                r = j * ROW_UNROLL + u
                for k, d_ref in enumerate((d1_ref, d2_ref)):
                    src = pl.multiple_of(d_ref[tile * tt + r], TOK_ROWS)
                    pltpu.make_async_copy(y_hbm.at[pl.ds(src, TOK_ROWS), :],
                                          ybuf.at[slot_, k, pl.ds(r * TOK_ROWS, TOK_ROWS), :],
                                          sem.at[slot_]).start(priority=k)
            return carry
        lax.fori_loop(0, tt // ROW_UNROLL, body, 0)

    @pl.when(i == 0)
    def _():
        issue(0, 0)

    @pl.when(i + 1 < nsteps)
    def _():
        issue(i + 1, 1 - slot)

    for k in range(TOP_K):
        pltpu.make_async_copy(y_hbm.at[pl.ds(0, tt * TOK_ROWS), :], ybuf.at[slot, k], sem.at[slot]).wait()
    route = route_ref[...]
    g1 = route[:, R_G1:R_G1 + 1]
    g2 = route[:, R_G2:R_G2 + 1]
    lo1, hi1 = _unpack_bf16_pairs(_load_token_rows(ybuf.at[slot, 0], tt))
    lo2, hi2 = _unpack_bf16_pairs(_load_token_rows(ybuf.at[slot, 1], tt))
    x_lo, x_hi = _unpack_bf16_pairs(_load_token_rows(x1_ref, tt))
    h_lo = ALPHA_DN * x_lo + (g1 * lo1 + g2 * lo2)
    h_hi = ALPHA_DN * x_hi + (g1 * hi1 + g2 * hi2)
    mu = (jnp.sum(h_lo, axis=-1, keepdims=True) + jnp.sum(h_hi, axis=-1, keepdims=True)) * (1.0 / D_MODEL)
    c_lo = h_lo - mu
    c_hi = h_hi - mu
    var = (jnp.sum(c_lo * c_lo, axis=-1, keepdims=True) + jnp.sum(c_hi * c_hi, axis=-1, keepdims=True)) * (1.0 / D_MODEL)
    inv = lax.rsqrt(var + LN_EPS)
    o_ref[:, 0:HALF] = c_lo * inv * g_ref[:, 0:HALF] + b_ref[:, 0:HALF]
    o_ref[:, HALF:] = c_hi * inv * g_ref[:, HALF:] + b_ref[:, HALF:]


def _combine(dest1, dest2, ybuf, x1p, route, ln_g, ln_b, *, tt=512):
    n = x1p.shape[0] // TOK_ROWS
    grid_spec = pltpu.PrefetchScalarGridSpec(
        num_scalar_prefetch=2,
        grid=(n // tt,),
        in_specs=[pl.BlockSpec(memory_space=pl.ANY),
                  pl.BlockSpec((tt * TOK_ROWS, 128), lambda i, a, b: (i, 0)),
                  pl.BlockSpec((tt, ROUTE_W), lambda i, a, b: (i, 0)),
                  pl.BlockSpec((1, D_MODEL), lambda i, a, b: (0, 0)),
                  pl.BlockSpec((1, D_MODEL), lambda i, a, b: (0, 0))],
        out_specs=pl.BlockSpec((tt, D_MODEL), lambda i, a, b: (i, 0)),
        scratch_shapes=[pltpu.VMEM((2, TOP_K, tt * TOK_ROWS, 128), jnp.uint32),
                        pltpu.SemaphoreType.DMA((2,))],
    )
    return pl.pallas_call(
        functools.partial(_combine_kernel, tt=tt),
        grid_spec=grid_spec,
        out_shape=jax.ShapeDtypeStruct((n, D_MODEL), F32),
        compiler_params=_cparams("arbitrary"),
        name="combine",
    )(dest1, dest2, ybuf, x1p, route, ln_g, ln_b)


def _block_plan(counts_row, n, tb):
    counts = counts_row[:N_EXPERTS, 0].astype(jnp.int32)
    pends = jnp.cumsum((counts + tb - 1) // tb * tb)
    nb = (n * TOP_K) // tb + N_EXPERTS
    nvalid = (pends[-1] // tb).astype(jnp.int32)
    blk = jnp.arange(nb, dtype=jnp.int32)
    blk_e = jnp.sum((pends[None, :] <= (blk * tb)[:, None]).astype(jnp.int32), axis=1)
    blk_e = jnp.minimum(blk_e, N_EXPERTS - 1)
    blk_e = jnp.where(blk < nvalid, blk_e, blk_e[jnp.maximum(nvalid - 1, 0)])
    first = jnp.concatenate([jnp.ones((1,), jnp.int32), (blk_e[1:] != blk_e[:-1]).astype(jnp.int32)])
    ids = jnp.arange(N_EXPERTS, dtype=jnp.int32)
    later = lax.cummin(jnp.where(counts > 0, ids, N_EXPERTS), axis=0, reverse=True)
    nxt = jnp.concatenate([later[1:], jnp.full((1,), N_EXPERTS, jnp.int32)])
    nxt = jnp.where(nxt < N_EXPERTS, nxt, -1)
    last_blk = jnp.where(counts > 0, pends // tb - 1, -1).astype(jnp.int32)
    return blk_e, first, nxt[blk_e], nvalid.reshape(1), last_blk


def kernel(x, w_in, w_gk_up, b_gk, conv_w, a_log, dt_bias, gla_norm_g, gdn_norm_g, w_out, ln1_g, ln1_b, w_router_group, b_router_group, w_router_expert, b_router_expert, w_gate, w_up, w_down, ln2_g, ln2_b):
    bsz, seq, d = x.shape
    n = bsz * seq
    x2d = x.reshape(n, d)
    w_main, w_aux = _split_w_in(w_in[0])
    main, aux = _in_proj(x2d, w_main, w_aux, conv_w[0].astype(F32), seq=seq)
    wgk = jnp.zeros((AUX_W, 256), F32).at[:GLA_GATE_RANK].set(w_gk_up[0])
    o_a = _gla(main, aux, wgk, b_gk[0][None], gla_norm_g[0][None], bsz=bsz, seq=seq)
    o_b = _gdn(main, aux, _pad_aux_row(a_log[0], AUX_AB), _pad_aux_row(dt_bias[0], AUX_AB),
               gdn_norm_g[0][None], bsz=bsz, seq=seq)
    rpad = jnp.zeros((ROUTE_W - N_EXPERTS - N_GROUPS, d), F32)
    w_router = jnp.concatenate([w_router_expert[0].T, w_router_group[0].T, rpad], axis=0)
    w_router_hi = w_router.astype(BF16)
    w_router = jnp.concatenate([w_router_hi, (w_router - w_router_hi.astype(F32)).astype(BF16)], axis=0)
    b_router = jnp.concatenate([b_router_expert[0], b_router_group[0], rpad[:, 0]])[:, None]
    x1p, route, gates, counts = _post_mix(o_a, o_b, x2d, w_out[0].astype(BF16), ln1_g[0][None], ln1_b[0][None],
                                          w_router, b_router)
    dest = _plan(route, counts, tb=EXPERT_BLOCK)
    blk_e, first, next_e, nvalid, last_blk = _block_plan(counts, n, EXPERT_BLOCK)
    xs = _dispatch(dest[0], dest[1], last_blk, nvalid, x1p, nb=blk_e.shape[0])
    ybuf = _experts(blk_e, first, next_e, nvalid, xs, w_gate[0], w_up[0], w_down[0])
    out = _combine(dest[0], dest[1], ybuf, x1p, gates, ln2_g[0][None], ln2_b[0][None])
    return out.reshape(bsz, seq, d)
```

```python
import functools

import jax
import jax.numpy as jnp
import numpy as np
from jax import lax
from jax.experimental import pallas as pl
from jax.experimental.pallas import tpu as pltpu

F32 = jnp.float32
BF16 = jnp.bfloat16
HIGHEST = lax.Precision.HIGHEST

D_MODEL = 1024
DEPTH = 1
GLA_HEADS = 4
GLA_DK = 64
GLA_DV = 128
GLA_WIDTH = GLA_HEADS * GLA_DV
GLA_GATE_RANK = 16
GLA_GATE_NORM = 16.0
GDN_HEADS = 4
GDN_DK = 128
GDN_DV = 128
GDN_WIDTH = GDN_HEADS * GDN_DV
CONV_W = 4
CHUNK = 64
N_GROUPS = 8
EXPERTS_PER_GROUP = 8
N_EXPERTS = N_GROUPS * EXPERTS_PER_GROUP
TOP_K = 2
D_EXPERT = 512
LN_EPS = 1e-5
RMS_EPS = 1e-6
ALPHA_DN = (2.0 * DEPTH) ** 0.25

MAIN_W = 2 * GLA_HEADS * GLA_DK + 2 * GLA_WIDTH + 4 * GDN_WIDTH
AUX_W = 128
AUX_LRA = 0
AUX_BB = GLA_GATE_RANK
AUX_AB = GLA_GATE_RANK + GDN_HEADS
SUB = 16
VMEM_LIMIT = 56 * 1024 * 1024


def _cparams(*sem):
    return pltpu.CompilerParams(dimension_semantics=sem, vmem_limit_bytes=VMEM_LIMIT)


def _sigmoid(x):
    return 1.0 / (1.0 + jnp.exp(-x))


def _silu(x):
    return x * _sigmoid(x)


def _log_sigmoid(x):
    return jnp.minimum(x, 0.0) - jnp.log(1.0 + jnp.exp(-jnp.abs(x)))


def _softplus(x):
    return jnp.maximum(x, 0.0) + jnp.log(1.0 + jnp.exp(-jnp.abs(x)))


def _chunk_cumsum(x, c):
    pos = lax.broadcasted_iota(jnp.int32, x.shape, 0) % c
    shift = 1
    while shift < c:
        x = x + jnp.where(pos >= shift, pltpu.roll(x, shift, 0), 0.0)
        shift *= 2
    return x


CONV_PAD = 8
GDN_HW = GDN_HEADS * GDN_DK
GDN_QKV_BLOCK0 = (2 * GLA_HEADS * GLA_DK + 2 * GLA_WIDTH) // GDN_HW


def _in_proj_kernel(x_ref, wm_ref, wa_ref, cw_ref, main_ref, aux_ref, stage_ref, *, tm, tiles_per_seq):
    xb = x_ref[...].astype(BF16)

    @pl.when((pl.program_id(0) % tiles_per_seq) == 0)
    def _():
        stage_ref[:, 0:CONV_PAD, :] = jnp.zeros((3 * GDN_HEADS, CONV_PAD, GDN_DK), F32)

    for j in range(MAIN_W // GDN_HW):
        sl = slice(j * GDN_HW, (j + 1) * GDN_HW)
        r = jnp.dot(xb, wm_ref[:, sl], preferred_element_type=F32)
        b = j - GDN_QKV_BLOCK0
        if not 0 <= b < 3:
            main_ref[:, sl] = r.astype(BF16)
            continue
        for h in range(GDN_HEADS):
            st = stage_ref.at[b * GDN_HEADS + h]
            cols = slice(b * GDN_HW + h * GDN_DK, b * GDN_HW + (h + 1) * GDN_DK)
            st[CONV_PAD:, :] = r[:, h * GDN_DK:(h + 1) * GDN_DK]
            acc = jnp.zeros((tm, GDN_DK), F32)
            for i in range(CONV_W):
                lo = CONV_PAD - (CONV_W - 1) + i
                acc = acc + st[lo:lo + tm, :] * cw_ref[i:i + 1, cols]
            st[0:CONV_PAD, :] = st[tm:tm + CONV_PAD, :]
            yh = _silu(acc)
            if b < 2:
                yh = yh * lax.rsqrt(jnp.sum(yh * yh, axis=-1, keepdims=True) + RMS_EPS)
            if b == 0:
                yh = yh * (GDN_DK ** -0.5)
            main_ref[:, j * GDN_HW + h * GDN_DK:j * GDN_HW + (h + 1) * GDN_DK] = yh.astype(BF16)
    aux_ref[...] = jnp.dot(xb, wa_ref[...], preferred_element_type=F32)


def _in_proj(x2d, w_main, w_aux, conv_w, *, seq, tm=512):
    n = x2d.shape[0]
    return pl.pallas_call(
        functools.partial(_in_proj_kernel, tm=tm, tiles_per_seq=seq // tm),
        grid=(n // tm,),
        in_specs=[pl.BlockSpec((tm, D_MODEL), lambda i: (i, 0)),
                  pl.BlockSpec((D_MODEL, MAIN_W), lambda i: (0, 0)),
                  pl.BlockSpec((D_MODEL, AUX_W), lambda i: (0, 0)),
                  pl.BlockSpec((CONV_W, 3 * GDN_HW), lambda i: (0, 0))],
        out_specs=[pl.BlockSpec((tm, MAIN_W), lambda i: (i, 0)),
                   pl.BlockSpec((tm, AUX_W), lambda i: (i, 0))],
        out_shape=[jax.ShapeDtypeStruct((n, MAIN_W), BF16),
                   jax.ShapeDtypeStruct((n, AUX_W), F32)],
        scratch_shapes=[pltpu.VMEM((3 * GDN_HEADS, tm + CONV_PAD, GDN_DK), F32)],
        compiler_params=_cparams("arbitrary"),
        name="in_proj",
    )(x2d, w_main, w_aux, conv_w)


def _split_w_in(w_in):
    sizes = (GLA_HEADS * GLA_DK, GLA_HEADS * GLA_DK, GLA_WIDTH, GLA_WIDTH, GLA_GATE_RANK,
             GDN_HEADS * GDN_DK, GDN_HEADS * GDN_DK, GDN_WIDTH, GDN_WIDTH, GDN_HEADS, GDN_HEADS)
    offs = np.cumsum((0,) + sizes)
    w_b = w_in.astype(BF16)
    seg = [w_b[:, offs[i]:offs[i + 1]] for i in range(len(sizes))]
    qa, ka, va, ra, lra, qb, kb, vb, zb, bb, ab = seg
    w_main = jnp.concatenate([qa, ka, va, ra, qb, kb, vb, zb], axis=1)
    pad = jnp.zeros((w_in.shape[0], AUX_W - GLA_GATE_RANK - 2 * GDN_HEADS), BF16)
    w_aux = jnp.concatenate([lra, bb, ab, pad], axis=1)
    return w_main, w_aux


_NT = (((1,), (1,)), ((), ()))
_TN = (((0,), (0,)), ((), ()))


def _gla_kernel(q_ref, k_ref, v_ref, r_ref, aux_ref, wgk_ref, bgk_ref, g_ref, o_ref, st_ref, *, ts):
    c = CHUNK
    nsub = c // SUB

    @pl.when(pl.program_id(1) == 0)
    def _():
        st_ref[...] = jnp.zeros_like(st_ref)

    pre = jnp.dot(aux_ref[...], wgk_ref[...], precision=HIGHEST, preferred_element_type=F32) + bgk_ref[...]
    gk_all = _log_sigmoid(pre) * (1.0 / GLA_GATE_NORM)

    row128 = lax.broadcasted_iota(jnp.int32, (c, 128), 0)
    lane128 = lax.broadcasted_iota(jnp.int32, (1, 128), 1)
    lane_masks = [(lane128 // GLA_DK) == hh for hh in range(2)]
    sub_valid = [row128 < SUB * (i + 1) for i in range(nsub)]
    ar = lax.broadcasted_iota(jnp.int32, (c, nsub * c), 0)
    ac = lax.broadcasted_iota(jnp.int32, (c, nsub * c), 1)
    amask = ((ac // c) == (ar // SUB)) & ((ac % c) <= ar)
    scale = GLA_DK ** -0.5
    gnorm = g_ref[...]

    nch = ts // c
    rows = [slice(ch * c, (ch + 1) * c) for ch in range(nch)]
    b_cum = _chunk_cumsum(gk_all, c)
    b_all = [b_cum[r] for r in rows]

    cp = [(ch, p) for ch in range(nch) for p in range(2)]
    qt, qd, kd, kstack, dlast = {}, {}, {}, {}, {}
    for ch, p in cp:
        lanes = slice(128 * p, 128 * (p + 1))
        bp = b_all[ch][:, lanes]
        qp = q_ref[rows[ch], lanes].astype(F32) * scale
        kp = k_ref[rows[ch], lanes].astype(F32)
        c_row = jnp.concatenate(
            [jnp.broadcast_to(bp[SUB * i:SUB * i + 1, :], (SUB, 128)) for i in range(nsub)], axis=0)
        qt[ch, p] = (qp * jnp.exp(bp - c_row)).astype(BF16)
        qd[ch, p] = qp * jnp.exp(bp)
        b_last = bp[c - 1:c, :]
        dlast[ch, p] = jnp.exp(b_last)
        kd[ch, p] = kp * jnp.exp(b_last - bp)
        kts = []
        for i in range(nsub):
            e = jnp.exp(jnp.where(sub_valid[i], bp[SUB * i:SUB * i + 1, :] - bp, 0.0))
            kts.append(jnp.where(sub_valid[i], kp * e, 0.0))
        kstack[ch, p] = jnp.concatenate(kts, axis=0)

    cph = [(ch, p, hh) for ch, p in cp for hh in range(2)]
    vh = {(ch, p, hh): v_ref[rows[ch], 128 * (2 * p + hh):128 * (2 * p + hh + 1)] for ch, p, hh in cph}
    r_mat = {(ch, p, hh): lax.dot_general(qt[ch, p], jnp.where(lane_masks[hh], kstack[ch, p], 0.0).astype(BF16),
                                          _NT, preferred_element_type=F32) for ch, p, hh in cph}
    o_intra = {k: jnp.dot(jnp.where(amask, r_mat[k], 0.0).astype(BF16), jnp.concatenate([vh[k]] * nsub, axis=0),
                          preferred_element_type=F32) for k in cph}
    upd = {(ch, p, hh): lax.dot_general(vh[ch, p, hh], jnp.where(lane_masks[hh], kd[ch, p], 0.0).astype(BF16), _TN,
                                        preferred_element_type=F32) for ch, p, hh in cph}

    st = [st_ref[p] for p in range(2)]
    for ch in range(nch):
        st_b = [s.astype(BF16) for s in st]
        o_pair = [lax.dot_general(
            jnp.concatenate([jnp.where(lane_masks[hh], qd[ch, p], 0.0).astype(BF16) for hh in range(2)], axis=0),
            st_b[p], _NT, preferred_element_type=F32) for p in range(2)]
        o_inter = {(p, hh): o_pair[p][hh * c:(hh + 1) * c] for p in range(2) for hh in range(2)}
        st = [st[p] * dlast[ch, p] + upd[ch, p, 0] + upd[ch, p, 1] for p in range(2)]
        for p in range(2):
            for hh in range(2):
                h = 2 * p + hh
                o = o_intra[ch, p, hh] + o_inter[p, hh]
                o = o * lax.rsqrt(jnp.mean(o * o, axis=-1, keepdims=True) + RMS_EPS) * gnorm
                gate = _silu(r_ref[rows[ch], 128 * h:128 * (h + 1)].astype(F32))
                o_ref[rows[ch], 128 * h:128 * (h + 1)] = (o * gate).astype(BF16)
    for p in range(2):
        st_ref[p] = st[p]


def _gla(main, aux, wgk_pad, b_gk, gla_norm_g, *, bsz, seq, ts=512):
    nt = seq // ts
    row = lambda b, t: b * nt + t
    return pl.pallas_call(
        functools.partial(_gla_kernel, ts=ts),
        grid=(bsz, nt),
        in_specs=[pl.BlockSpec((ts, 256), lambda b, t: (row(b, t), 0)),
                  pl.BlockSpec((ts, 256), lambda b, t: (row(b, t), 1)),
                  pl.BlockSpec((ts, 512), lambda b, t: (row(b, t), 1)),
                  pl.BlockSpec((ts, 512), lambda b, t: (row(b, t), 2)),
                  pl.BlockSpec((ts, AUX_W), lambda b, t: (row(b, t), 0)),
                  pl.BlockSpec((AUX_W, 256), lambda b, t: (0, 0)),
                  pl.BlockSpec((1, 256), lambda b, t: (0, 0)),
                  pl.BlockSpec((1, 128), lambda b, t: (0, 0))],
        out_specs=pl.BlockSpec((ts, GLA_WIDTH), lambda b, t: (row(b, t), 0)),
        out_shape=jax.ShapeDtypeStruct((bsz * seq, GLA_WIDTH), BF16),
        scratch_shapes=[pltpu.VMEM((2, 128, 128), F32)],
        compiler_params=_cparams("parallel", "arbitrary"),
        name="gla",
    )(main, main, main, main, aux, wgk_pad, b_gk, gla_norm_g)


def _gdn_kernel(q_ref, k_ref, v_ref, z_ref, aux_ref, alog_ref, dtb_ref, g_ref, o_ref, st_ref, *, ts):
    c = CHUNK
    nch = ts // c

    @pl.when(pl.program_id(1) == 0)
    def _():
        st_ref[...] = jnp.zeros_like(st_ref)

    aux = aux_ref[...]
    beta_full = _sigmoid(aux)
    g_full = -jnp.exp(alog_ref[...]) * _softplus(aux + dtb_ref[...])
    gam_full = _chunk_cumsum(g_full, c)
    gam_t = gam_full.T
    gnorm = g_ref[...]

    row = lax.broadcasted_iota(jnp.int32, (c, 2 * c), 0)
    lane = lax.broadcasted_iota(jnp.int32, (c, 2 * c), 1)
    left = lane < c
    col = lane % c
    causal = col <= row
    strict = col < row
    same32 = (row // 32) == (col // 32)
    same16 = (row // 16) == (col // 16)
    lvl0 = same16 & strict
    lvl1 = same32 & jnp.logical_not(same16) & strict
    lvl2 = jnp.logical_not(same32) & strict

    def pack_rows(x2):
        return jnp.where(left, x2[0:c], x2[c:2 * c])

    def pack_cols(x1):
        return jnp.where(left, x1[0:c], x1[c:2 * c])

    def halves(x):
        return jnp.where(left, x, 0.0).astype(BF16), jnp.where(left, 0.0, x).astype(BF16)

    def mmp(x, y):
        return jnp.dot(x.astype(BF16), jnp.concatenate(halves(y), axis=0), preferred_element_type=F32)

    heads = range(GDN_HEADS)
    npair = nch // 2
    items = [(h, pr) for h in heads for pr in range(npair)]
    prow = [slice(2 * c * pr, 2 * c * (pr + 1)) for pr in range(npair)]

    qn, kn, beta, gam, gam_r, kb, rhs, rhs_b, qd, kd = ([None] * GDN_HEADS for _ in range(10))
    for h in heads:
        hl = slice(128 * h, 128 * (h + 1))
        kb[h] = k_ref[:, hl]
        qn[h] = q_ref[:, hl].astype(F32)
        kn[h] = kb[h].astype(F32)
        vh = v_ref[:, hl].astype(F32)
        beta[h] = beta_full[:, AUX_BB + h:AUX_BB + h + 1]
        gam[h] = gam_full[:, AUX_AB + h:AUX_AB + h + 1]
        gam_r[h] = gam_t[AUX_AB + h:AUX_AB + h + 1, :]
        egam = jnp.exp(gam[h])
        rhs[h] = jnp.concatenate([beta[h] * vh, beta[h] * egam * kn[h]], axis=1)
        rhs_b[h] = rhs[h].astype(BF16)
        qd[h] = (qn[h] * egam).astype(BF16)
        gl_rows = jnp.concatenate(
            [jnp.broadcast_to(gam[h][c * (j + 1) - 1:c * (j + 1), :], (c, 1)) for j in range(nch)], axis=0)
        kd[h] = (kn[h] * jnp.exp(gl_rows - gam[h])).astype(BF16)

    kk = [pack_rows(lax.dot_general(kb[h][prow[pr]], kb[h][prow[pr]], _NT, preferred_element_type=F32))
          for h, pr in items]
    qk = [pack_rows(lax.dot_general(qn[h][prow[pr]].astype(BF16), kb[h][prow[pr]], _NT,
                                    preferred_element_type=F32)) for h, pr in items]
    dec = [jnp.exp(jnp.where(causal, pack_cols(gam[h][prow[pr]]) - gam_r[h][:, prow[pr]], 0.0)) for h, pr in items]
    a = [pack_cols(beta[h][prow[pr]]) * kk[i] * dec[i] for i, (h, pr) in enumerate(items)]
    attn = [halves(jnp.where(causal, qk[i] * dec[i], 0.0)) for i in range(len(items))]

    x1 = [jnp.where(lvl0, -ai, 0.0) for ai in a]
    x2 = [mmp(x, x) for x in x1]
    y = [mmp(x, xx) for x, xx in zip(x1, x2)]
    p = [x + xx + yy for x, xx, yy in zip(x1, x2, y)]
    xk = x2
    for _ in range(2):
        xk = [mmp(x, x) for x in xk]
        y = [mmp(pp, x) for pp, x in zip(p, xk)]
        p = [pp + x + yy for pp, x, yy in zip(p, xk, y)]
    for lvl in (lvl1, lvl2):
        low = [jnp.where(lvl, ai, 0.0) for ai in a]
        y = [mmp(pp, lo) for pp, lo in zip(p, low)]
        zz = [lo + yy for lo, yy in zip(low, y)]
        y = [mmp(z, pp) for z, pp in zip(zz, p)]
        p = [pp - z - yy for pp, z, yy in zip(p, zz, y)]

    us = [[None] * nch for _ in heads]
    ws = [[None] * nch for _ in heads]
    for i, (h, pr) in enumerate(items):
        for half, p_half in enumerate(halves(p[i])):
            j = 2 * pr + half
            uw = rhs[h][c * j:c * (j + 1)] + jnp.dot(p_half, rhs_b[h][prow[pr]], preferred_element_type=F32)
            us[h][j] = uw[:, :GDN_DV]
            ws[h][j] = uw[:, GDN_DV:].astype(BF16)

    s = [st_ref[h] for h in heads]
    vprev = [None] * GDN_HEADS
    for j in range(nch):
        rows = slice(c * j, c * (j + 1))
        sb = [s[h].astype(BF16) for h in heads]
        ws_qd = [jnp.dot(jnp.concatenate([ws[h][j], qd[h][rows]], axis=0), sb[h], preferred_element_type=F32)
                 for h in heads]
        vn16 = [(us[h][j] - ws_qd[h][0:c]).astype(BF16) for h in heads]
        o_inter = [ws_qd[h][c:2 * c] for h in heads]
        for h in heads:
            vpair = jnp.concatenate([vn16[h], jnp.zeros_like(vn16[h])] if j % 2 == 0 else [vprev[h], vn16[h]], axis=0)
            o = o_inter[h] + jnp.dot(attn[h * npair + j // 2][j % 2], vpair, preferred_element_type=F32)
            s[h] = jnp.exp(gam[h][c * (j + 1) - 1:c * (j + 1), :]) * s[h] + lax.dot_general(
                kd[h][rows], vn16[h], _TN, preferred_element_type=F32)
            o = o * lax.rsqrt(jnp.mean(o * o, axis=-1, keepdims=True) + RMS_EPS) * gnorm
            gate = _silu(z_ref[rows, 128 * h:128 * (h + 1)].astype(F32))
            o_ref[rows, 128 * h:128 * (h + 1)] = (o * gate).astype(BF16)
        vprev = vn16
    for h in heads:
        st_ref[h] = s[h]


def _gdn(main, aux, alog_pad, dtb_pad, gdn_norm_g, *, bsz, seq, ts=512):
    nt = seq // ts
    row = lambda b, t: b * nt + t
    hw = GDN_HEADS * GDN_DK
    return pl.pallas_call(
        functools.partial(_gdn_kernel, ts=ts),
        grid=(bsz, nt),
        in_specs=[pl.BlockSpec((ts, hw), lambda b, t: (row(b, t), 3)),
                  pl.BlockSpec((ts, hw), lambda b, t: (row(b, t), 4)),
                  pl.BlockSpec((ts, hw), lambda b, t: (row(b, t), 5)),
                  pl.BlockSpec((ts, hw), lambda b, t: (row(b, t), 6)),
                  pl.BlockSpec((ts, AUX_W), lambda b, t: (row(b, t), 0)),
                  pl.BlockSpec((1, AUX_W), lambda b, t: (0, 0)),
                  pl.BlockSpec((1, AUX_W), lambda b, t: (0, 0)),
                  pl.BlockSpec((1, GDN_DV), lambda b, t: (0, 0))],
        out_specs=pl.BlockSpec((ts, GDN_WIDTH), lambda b, t: (row(b, t), 0)),
        out_shape=jax.ShapeDtypeStruct((bsz * seq, GDN_WIDTH), BF16),
        scratch_shapes=[pltpu.VMEM((GDN_HEADS, GDN_DK, GDN_DV), F32)],
        compiler_params=_cparams("parallel", "arbitrary"),
        name="gdn",
    )(main, main, main, main, aux, alog_pad, dtb_pad, gdn_norm_g)


def _pad_aux_row(v, offset):
    return jnp.zeros((1, AUX_W), F32).at[0, offset:offset + v.shape[0]].set(v.astype(F32))


ROUTE_W = 128
R_E1, R_E2, R_RANK1, R_RANK2, R_G1, R_G2 = range(6)
ROUTER_GROUP_COL = N_EXPERTS
POST_MIX_SPLIT = 4


def _layer_norm(h, g, b):
    mu = jnp.mean(h, axis=-1, keepdims=True)
    hc = h - mu
    var = jnp.mean(hc * hc, axis=-1, keepdims=True)
    return hc * lax.rsqrt(var + LN_EPS) * g + b


def _pack_bf16_pairs(x):
    w = x.shape[1] // 2
    lo = lax.bitcast_convert_type(x[:, :w].astype(BF16).astype(F32), jnp.uint32)
    hi = lax.bitcast_convert_type(x[:, w:].astype(BF16).astype(F32), jnp.uint32)
    return (lo >> 16) | (hi & jnp.uint32(0xFFFF0000))


def _unpack_bf16_pairs(p):
    lo = lax.bitcast_convert_type(p << 16, F32)
    hi = lax.bitcast_convert_type(p & jnp.uint32(0xFFFF0000), F32)
    return lo, hi


HALF = D_MODEL // 2
TOK_ROWS = HALF // 128


def _store_token_rows(ref, packed):
    m = packed.shape[0]
    for c in range(TOK_ROWS):
        ref[pl.ds(c, m, stride=TOK_ROWS), :] = packed[:, 128 * c:128 * (c + 1)]


def _load_token_rows(ref, m):
    return jnp.concatenate([ref[pl.ds(c, m, stride=TOK_ROWS), :] for c in range(TOK_ROWS)], axis=1)


def _post_mix_kernel(oa_ref, ob_ref, x_ref, wo_ref, g_ref, b_ref, wr_ref, br_ref,
                     x1p_ref, route_ref, gates_ref, cnt_ref, carry_ref, *, tm):
    @pl.when(pl.program_id(0) == 0)
    def _():
        carry_ref[...] = jnp.zeros_like(carry_ref)

    m = tm // POST_MIX_SPLIT
    groups = range(POST_MIX_SPLIT)
    rs = [slice(h * m, (h + 1) * m) for h in groups]
    y = [jnp.dot(oa_ref[rs[h], :], wo_ref[0:GLA_WIDTH, :], preferred_element_type=F32)
         + jnp.dot(ob_ref[rs[h], :], wo_ref[GLA_WIDTH:, :], preferred_element_type=F32) for h in groups]
    x1 = [_layer_norm(ALPHA_DN * x_ref[rs[h], :] + y[h], g_ref[...], b_ref[...]) for h in groups]
    for h in groups:
        _store_token_rows(x1p_ref.at[pl.ds(h * m * TOK_ROWS, m * TOK_ROWS), :], _pack_bf16_pairs(x1[h]))

    xh = [x.astype(BF16) for x in x1]
    xl = [(x - xb.astype(F32)).astype(BF16) for x, xb in zip(x1, xh)]
    hw = [lax.dot_general(wr_ref[...], xb, _NT, preferred_element_type=F32) for xb in xh]
    lw = [lax.dot_general(wr_ref[0:ROUTE_W, :], xb, _NT, preferred_element_type=F32) for xb in xl]
    logits = [a[:ROUTE_W] + a[ROUTE_W:] + b + br_ref[...] for a, b in zip(hw, lw)]
    sub = lax.broadcasted_iota(jnp.int32, (ROUTE_W, m), 0)
    big = jnp.int32(1 << 20)
    neg = jnp.float32(-jnp.inf)

    def first_argmax(vals):
        mx = jnp.max(vals, axis=0, keepdims=True)
        idx = jnp.min(jnp.where(vals == mx, sub, big), axis=0, keepdims=True)
        return mx, idx

    is_group = (sub >= ROUTER_GROUP_COL) & (sub < ROUTER_GROUP_COL + N_GROUPS)
    gl = [jnp.where(is_group, lg, neg) for lg in logits]
    gtop = [first_argmax(v) for v in gl]
    p_group = [1.0 / jnp.sum(jnp.exp(v - t[0]), axis=0, keepdims=True) for v, t in zip(gl, gtop)]
    el = [jnp.where((sub // EXPERTS_PER_GROUP) == (t[1] - ROUTER_GROUP_COL), lg, neg) for lg, t in zip(logits, gtop)]
    top1 = [first_argmax(v) for v in el]
    top2 = [first_argmax(jnp.where(sub == t[1], neg, v)) for v, t in zip(el, top1)]
    ex = [jnp.exp(b[0] - a[0]) for a, b in zip(top1, top2)]
    w1 = [1.0 / (1.0 + e) for e in ex]
    gate1 = [p * w for p, w in zip(p_group, w1)]
    gate2 = [p * (e * w) for p, e, w in zip(p_group, ex, w1)]

    oh1 = [sub == t[1] for t in top1]
    oh2 = [sub == t[1] for t in top2]
    ri = lax.broadcasted_iota(jnp.int32, (m, m), 0)
    ci = lax.broadcasted_iota(jnp.int32, (m, m), 1)
    ustrict = (ri < ci).astype(BF16)
    oh1f = [o.astype(F32) for o in oh1]
    oh2f = [o.astype(F32) for o in oh2]
    c1 = [jnp.dot(o.astype(BF16), ustrict, preferred_element_type=F32) for o in oh1f]
    c2 = [jnp.dot(o.astype(BF16), ustrict, preferred_element_type=F32) for o in oh2f]
    tot1 = [jnp.sum(o, axis=1, keepdims=True) for o in oh1f]
    tot2 = [jnp.sum(o, axis=1, keepdims=True) for o in oh2f]
    carry = carry_ref[...]
    eye8 = jnp.eye(8, ROUTE_W, dtype=F32)
    zero = jnp.zeros((1, m), F32)
    for h in groups:
        rank1 = jnp.sum(jnp.where(oh1[h], c1[h] + carry, 0.0), axis=0, keepdims=True)
        rank2 = jnp.sum(jnp.where(oh2[h], c2[h] + (carry + tot1[h]), 0.0), axis=0, keepdims=True)
        carry = carry + tot1[h] + tot2[h]
        route_t = jnp.concatenate([top1[h][1].astype(F32), top2[h][1].astype(F32), rank1, rank2,
                                   gate1[h], gate2[h], zero, zero], axis=0)
        route_ref[:, rs[h]] = route_t
        gates_ref[rs[h], :] = lax.dot_general(route_t, eye8, _TN, precision=HIGHEST, preferred_element_type=F32)
    carry_ref[...] = carry
    cnt_ref[...] = carry


def _post_mix(o_a, o_b, x2d, w_out_b, ln_g, ln_b, w_router, b_router, *, tm=1024):
    n = x2d.shape[0]
    return pl.pallas_call(
        functools.partial(_post_mix_kernel, tm=tm),
        grid=(n // tm,),
        in_specs=[pl.BlockSpec((tm, GLA_WIDTH), lambda i: (i, 0)),
                  pl.BlockSpec((tm, GDN_WIDTH), lambda i: (i, 0)),
                  pl.BlockSpec((tm, D_MODEL), lambda i: (i, 0)),
                  pl.BlockSpec((GLA_WIDTH + GDN_WIDTH, D_MODEL), lambda i: (0, 0)),
                  pl.BlockSpec((1, D_MODEL), lambda i: (0, 0)),
                  pl.BlockSpec((1, D_MODEL), lambda i: (0, 0)),
                  pl.BlockSpec((2 * ROUTE_W, D_MODEL), lambda i: (0, 0)),
                  pl.BlockSpec((ROUTE_W, 1), lambda i: (0, 0))],
        out_specs=[pl.BlockSpec((tm * TOK_ROWS, 128), lambda i: (i, 0)),
                   pl.BlockSpec((8, tm), lambda i: (0, i)),
                   pl.BlockSpec((tm, ROUTE_W), lambda i: (i, 0)),
                   pl.BlockSpec((ROUTE_W, 1), lambda i: (0, 0))],
        out_shape=[jax.ShapeDtypeStruct((n * TOK_ROWS, 128), jnp.uint32),
                   jax.ShapeDtypeStruct((8, n), F32),
                   jax.ShapeDtypeStruct((n, ROUTE_W), F32),
                   jax.ShapeDtypeStruct((ROUTE_W, 1), F32)],
        scratch_shapes=[pltpu.VMEM((ROUTE_W, 1), F32)],
        compiler_params=_cparams("arbitrary"),
        name="post_mix",
    )(o_a, o_b, x2d, w_out_b, ln_g, ln_b, w_router, b_router)


EXPERT_BLOCK = 256
BLOCKS_PER_STEP = 8

ROW_UNROLL = 8


def _plan_kernel(route_ref, cnt_ref, dest_ref, *, tm, tb):
    counts = cnt_ref[...].astype(jnp.int32)
    padded = jnp.broadcast_to(((counts + (tb - 1)) & jnp.int32(-tb)).astype(F32), (ROUTE_W, ROUTE_W))
    pstart = (_chunk_cumsum(padded, ROUTE_W) - padded)[:, 0:1]
    route = route_ref[...]
    sub = lax.broadcasted_iota(jnp.int32, (ROUTE_W, tm), 0).astype(F32)
    rows = []
    for ce, cr in ((R_E1, R_RANK1), (R_E2, R_RANK2)):
        off = jnp.sum(jnp.where(sub == route[ce:ce + 1, :], pstart, 0.0), axis=0, keepdims=True)
        rows.append(off + route[cr:cr + 1, :])
    rows.extend([jnp.zeros((1, tm), F32)] * (8 - len(rows)))
    dest_ref[...] = jnp.concatenate(rows, axis=0).astype(jnp.int32) * TOK_ROWS


def _plan(route, counts, *, tb, tm=2048):
    n = route.shape[1]
    tm = min(tm, n)
    return pl.pallas_call(
        functools.partial(_plan_kernel, tm=tm, tb=tb),
        grid=(n // tm,),
        in_specs=[pl.BlockSpec((8, tm), lambda i: (0, i)),
                  pl.BlockSpec((ROUTE_W, 1), lambda i: (0, 0))],
        out_specs=pl.BlockSpec((8, tm), lambda i: (0, i)),
        out_shape=jax.ShapeDtypeStruct((8, n), jnp.int32),
        compiler_params=_cparams("parallel"),
        name="plan",
    )(route, counts)


def _dispatch_kernel(d1_ref, d2_ref, last_blk_ref, nvalid_ref, xp_ref, xs_ref, zbuf, sem, zsem, *, tm, tb, nb):
    base = pl.program_id(0) * tm
    rows = tb * TOK_ROWS

    @pl.when(pl.program_id(0) == 0)
    def _():
        zbuf[...] = jnp.zeros_like(zbuf)

        def zero_copy(blk):
            return pltpu.make_async_copy(zbuf, xs_ref.at[pl.ds(pl.multiple_of(blk * rows, rows), rows), :], zsem)

        def for_each_block(fn):
            def per_expert(e, carry):
                @pl.when(last_blk_ref[e] >= 0)
                def _():
                    fn(zero_copy(last_blk_ref[e]))
                return carry

            def per_tail(blk, carry):
                fn(zero_copy(blk))
                return carry
            lax.fori_loop(0, N_EXPERTS, per_expert, 0)
            lax.fori_loop(nvalid_ref[0], nb, per_tail, 0)

        for_each_block(lambda cp: cp.start())
        for_each_block(lambda cp: cp.wait())

    def body(j, carry):
        for u in range(ROW_UNROLL):
            r = j * ROW_UNROLL + u
            for k, d_ref in enumerate((d1_ref, d2_ref)):
                dst = pl.multiple_of(d_ref[base + r], TOK_ROWS)
                pltpu.make_async_copy(xp_ref.at[pl.ds(r * TOK_ROWS, TOK_ROWS), :],
                                      xs_ref.at[pl.ds(dst, TOK_ROWS), :], sem).start(priority=k)
        return carry
    lax.fori_loop(0, tm // ROW_UNROLL, body, 0)
    for _ in range(TOP_K):
        pltpu.make_async_copy(xp_ref, xs_ref.at[pl.ds(0, tm * TOK_ROWS), :], sem).wait()


def _dispatch(dest1, dest2, last_blk, nvalid, x1p, *, nb, tb=EXPERT_BLOCK, tm=1024):
    n = x1p.shape[0] // TOK_ROWS
    grid_spec = pltpu.PrefetchScalarGridSpec(
        num_scalar_prefetch=4,
        grid=(n // tm,),
        in_specs=[pl.BlockSpec((tm * TOK_ROWS, 128), lambda i, a, b, lb, nv: (i, 0))],
        out_specs=pl.BlockSpec(memory_space=pl.ANY),
        scratch_shapes=[pltpu.VMEM((tb * TOK_ROWS, 128), jnp.uint32),
                        pltpu.SemaphoreType.DMA(()),
                        pltpu.SemaphoreType.DMA(())],
    )
    return pl.pallas_call(
        functools.partial(_dispatch_kernel, tm=tm, tb=tb, nb=nb),
        grid_spec=grid_spec,
        out_shape=jax.ShapeDtypeStruct((nb * tb * TOK_ROWS, 128), jnp.uint32),
        compiler_params=_cparams("arbitrary"),
        name="dispatch",
    )(dest1, dest2, last_blk, nvalid, x1p)


def _expert_kernel(blk_e_ref, first_ref, next_e_ref, nvalid_ref, xs_ref, wg_hbm, wu_hbm, wd_hbm, y_ref,
                   wgs, wus, wds, wgb, wub, wdb, sem, *, tb):
    rows = tb * TOK_ROWS

    def weight_copies(e):
        return [pltpu.make_async_copy(src.at[e], dst, sem.at[j])
                for j, (src, dst) in enumerate(((wg_hbm, wgs), (wu_hbm, wus), (wd_hbm, wds)))]

    @pl.when((pl.program_id(0) == 0) & (nvalid_ref[0] > 0))
    def _():
        for cp in weight_copies(blk_e_ref[0]):
            cp.start()

    for sub in range(BLOCKS_PER_STEP):
        i = pl.program_id(0) * BLOCKS_PER_STEP + sub
        xs_blk = xs_ref.at[pl.ds(sub * rows, rows), :]
        y_blk = y_ref.at[pl.ds(sub * rows, rows), :]

        @pl.when(i < nvalid_ref[0])
        def _():
            @pl.when(first_ref[i] == 1)
            def _():
                for cp in weight_copies(blk_e_ref[i]):
                    cp.wait()
                wgb[...] = wgs[...].astype(BF16)
                wub[...] = wus[...].astype(BF16)
                wdb[...] = wds[...].astype(BF16)

                @pl.when(next_e_ref[i] >= 0)
                def _():
                    for cp in weight_copies(next_e_ref[i]):
                        cp.start()

            lo, hi = _unpack_bf16_pairs(_load_token_rows(xs_blk, tb))
            xb = jnp.concatenate([lo.astype(BF16), hi.astype(BF16)], axis=1)
            hg = jnp.dot(xb, wgb[...], preferred_element_type=F32)
            hu = jnp.dot(xb, wub[...], preferred_element_type=F32)
            hid = (_silu(hg) * hu).astype(BF16)
            _store_token_rows(y_blk, _pack_bf16_pairs(jnp.dot(hid, wdb[...], preferred_element_type=F32)))

        @pl.when(i >= nvalid_ref[0])
        def _():
            y_blk[...] = jnp.zeros((rows, 128), jnp.uint32)


def _experts(blk_e, first, next_e, nvalid, xs, w_gate, w_up, w_down, *, tb=EXPERT_BLOCK):
    nb = blk_e.shape[0]
    assert nb % BLOCKS_PER_STEP == 0
    step_rows = BLOCKS_PER_STEP * tb * TOK_ROWS
    last_step = lambda nv: jnp.maximum((nv[0] - 1) // BLOCKS_PER_STEP, 0)
    grid_spec = pltpu.PrefetchScalarGridSpec(
        num_scalar_prefetch=4,
        grid=(nb // BLOCKS_PER_STEP,),
        in_specs=[pl.BlockSpec((step_rows, 128), lambda s, be, fr, ne, nv: (jnp.minimum(s, last_step(nv)), 0)),
                  pl.BlockSpec(memory_space=pl.ANY),
                  pl.BlockSpec(memory_space=pl.ANY),
                  pl.BlockSpec(memory_space=pl.ANY)],
        out_specs=pl.BlockSpec((step_rows, 128), lambda s, be, fr, ne, nv: (s, 0)),
        scratch_shapes=[pltpu.VMEM((D_MODEL, D_EXPERT), F32),
                        pltpu.VMEM((D_MODEL, D_EXPERT), F32),
                        pltpu.VMEM((D_EXPERT, D_MODEL), F32),
                        pltpu.VMEM((D_MODEL, D_EXPERT), BF16),
                        pltpu.VMEM((D_MODEL, D_EXPERT), BF16),
                        pltpu.VMEM((D_EXPERT, D_MODEL), BF16),
                        pltpu.SemaphoreType.DMA((3,))],
    )
    return pl.pallas_call(
        functools.partial(_expert_kernel, tb=tb),
        grid_spec=grid_spec,
        out_shape=jax.ShapeDtypeStruct((nb * tb * TOK_ROWS, 128), jnp.uint32),
        compiler_params=_cparams("arbitrary"),
        name="experts",
    )(blk_e, first, next_e, nvalid, xs, w_gate, w_up, w_down)


def _combine_kernel(d1_ref, d2_ref, y_hbm, x1_ref, route_ref, g_ref, b_ref, o_ref, ybuf, sem, *, tt):
    i = pl.program_id(0)
    nsteps = pl.num_programs(0)
    slot = i % 2

    def issue(tile, slot_):
        def body(j, carry):
            for u in range(ROW_UNROLL):
                r = j * ROW_UNROLL + u
                for k, d_ref in enumerate((d1_ref, d2_ref)):
                    src = pl.multiple_of(d_ref[tile * tt + r], TOK_ROWS)
                    pltpu.make_async_copy(y_hbm.at[pl.ds(src, TOK_ROWS), :],
                                          ybuf.at[slot_, k, pl.ds(r * TOK_ROWS, TOK_ROWS), :],
                                          sem.at[slot_]).start(priority=k)
            return carry
        lax.fori_loop(0, tt // ROW_UNROLL, body, 0)

    @pl.when(i == 0)
    def _():
        issue(0, 0)

    @pl.when(i + 1 < nsteps)
    def _():
        issue(i + 1, 1 - slot)

    for k in range(TOP_K):
        pltpu.make_async_copy(y_hbm.at[pl.ds(0, tt * TOK_ROWS), :], ybuf.at[slot, k], sem.at[slot]).wait()
    route = route_ref[...]
    g1 = route[:, R_G1:R_G1 + 1]
    g2 = route[:, R_G2:R_G2 + 1]
    lo1, hi1 = _unpack_bf16_pairs(_load_token_rows(ybuf.at[slot, 0], tt))
    lo2, hi2 = _unpack_bf16_pairs(_load_token_rows(ybuf.at[slot, 1], tt))
    x_lo, x_hi = _unpack_bf16_pairs(_load_token_rows(x1_ref, tt))
    h_lo = ALPHA_DN * x_lo + (g1 * lo1 + g2 * lo2)
    h_hi = ALPHA_DN * x_hi + (g1 * hi1 + g2 * hi2)
    mu = (jnp.sum(h_lo, axis=-1, keepdims=True) + jnp.sum(h_hi, axis=-1, keepdims=True)) * (1.0 / D_MODEL)
    c_lo = h_lo - mu
    c_hi = h_hi - mu
    var = (jnp.sum(c_lo * c_lo, axis=-1, keepdims=True) + jnp.sum(c_hi * c_hi, axis=-1, keepdims=True)) * (1.0 / D_MODEL)
    inv = lax.rsqrt(var + LN_EPS)
    o_ref[:, 0:HALF] = c_lo * inv * g_ref[:, 0:HALF] + b_ref[:, 0:HALF]
    o_ref[:, HALF:] = c_hi * inv * g_ref[:, HALF:] + b_ref[:, HALF:]


def _combine(dest1, dest2, ybuf, x1p, route, ln_g, ln_b, *, tt=512):
    n = x1p.shape[0] // TOK_ROWS
    grid_spec = pltpu.PrefetchScalarGridSpec(
        num_scalar_prefetch=2,
        grid=(n // tt,),
        in_specs=[pl.BlockSpec(memory_space=pl.ANY),
                  pl.BlockSpec((tt * TOK_ROWS, 128), lambda i, a, b: (i, 0)),
                  pl.BlockSpec((tt, ROUTE_W), lambda i, a, b: (i, 0)),
                  pl.BlockSpec((1, D_MODEL), lambda i, a, b: (0, 0)),
                  pl.BlockSpec((1, D_MODEL), lambda i, a, b: (0, 0))],
        out_specs=pl.BlockSpec((tt, D_MODEL), lambda i, a, b: (i, 0)),
        scratch_shapes=[pltpu.VMEM((2, TOP_K, tt * TOK_ROWS, 128), jnp.uint32),
                        pltpu.SemaphoreType.DMA((2,))],
    )
    return pl.pallas_call(
        functools.partial(_combine_kernel, tt=tt),
        grid_spec=grid_spec,
        out_shape=jax.ShapeDtypeStruct((n, D_MODEL), F32),
        compiler_params=_cparams("arbitrary"),
        name="combine",
    )(dest1, dest2, ybuf, x1p, route, ln_g, ln_b)


def _block_plan(counts_row, n, tb):
    counts = counts_row[:N_EXPERTS, 0].astype(jnp.int32)
    pends = jnp.cumsum((counts + tb - 1) // tb * tb)
    nb = (n * TOP_K) // tb + N_EXPERTS
    nvalid = (pends[-1] // tb).astype(jnp.int32)
    blk = jnp.arange(nb, dtype=jnp.int32)
    blk_e = jnp.sum((pends[None, :] <= (blk * tb)[:, None]).astype(jnp.int32), axis=1)
    blk_e = jnp.minimum(blk_e, N_EXPERTS - 1)
    blk_e = jnp.where(blk < nvalid, blk_e, blk_e[jnp.maximum(nvalid - 1, 0)])
    first = jnp.concatenate([jnp.ones((1,), jnp.int32), (blk_e[1:] != blk_e[:-1]).astype(jnp.int32)])
    ids = jnp.arange(N_EXPERTS, dtype=jnp.int32)
    later = lax.cummin(jnp.where(counts > 0, ids, N_EXPERTS), axis=0, reverse=True)
    nxt = jnp.concatenate([later[1:], jnp.full((1,), N_EXPERTS, jnp.int32)])
    nxt = jnp.where(nxt < N_EXPERTS, nxt, -1)
    last_blk = jnp.where(counts > 0, pends // tb - 1, -1).astype(jnp.int32)
    return blk_e, first, nxt[blk_e], nvalid.reshape(1), last_blk


def kernel(x, w_in, w_gk_up, b_gk, conv_w, a_log, dt_bias, gla_norm_g, gdn_norm_g, w_out, ln1_g, ln1_b, w_router_group, b_router_group, w_router_expert, b_router_expert, w_gate, w_up, w_down, ln2_g, ln2_b):
    bsz, seq, d = x.shape
    n = bsz * seq
    x2d = x.reshape(n, d)
    w_main, w_aux = _split_w_in(w_in[0])
    main, aux = _in_proj(x2d, w_main, w_aux, conv_w[0].astype(F32), seq=seq)
    wgk = jnp.zeros((AUX_W, 256), F32).at[:GLA_GATE_RANK].set(w_gk_up[0])
    o_a = _gla(main, aux, wgk, b_gk[0][None], gla_norm_g[0][None], bsz=bsz, seq=seq)
    o_b = _gdn(main, aux, _pad_aux_row(a_log[0], AUX_AB), _pad_aux_row(dt_bias[0], AUX_AB),
               gdn_norm_g[0][None], bsz=bsz, seq=seq)
    rpad = jnp.zeros((ROUTE_W - N_EXPERTS - N_GROUPS, d), F32)
    w_router = jnp.concatenate([w_router_expert[0].T, w_router_group[0].T, rpad], axis=0)
    w_router_hi = w_router.astype(BF16)
    w_router = jnp.concatenate([w_router_hi, (w_router - w_router_hi.astype(F32)).astype(BF16)], axis=0)
    b_router = jnp.concatenate([b_router_expert[0], b_router_group[0], rpad[:, 0]])[:, None]
    x1p, route, gates, counts = _post_mix(o_a, o_b, x2d, w_out[0].astype(BF16), ln1_g[0][None], ln1_b[0][None],
                                          w_router, b_router)
    dest = _plan(route, counts, tb=EXPERT_BLOCK)
    blk_e, first, next_e, nvalid, last_blk = _block_plan(counts, n, EXPERT_BLOCK)
    xs = _dispatch(dest[0], dest[1], last_blk, nvalid, x1p, nb=blk_e.shape[0])
    ybuf = _experts(blk_e, first, next_e, nvalid, xs, w_gate[0], w_up[0], w_down[0])
    out = _combine(dest[0], dest[1], ybuf, x1p, gates, ln2_g[0][None], ln2_b[0][None])
    return out.reshape(bsz, seq, d)
```

```python
import functools

import jax
import jax.numpy as jnp
import numpy as np
from jax import lax
from jax.experimental import pallas as pl
from jax.experimental.pallas import tpu as pltpu

F32 = jnp.float32
BF16 = jnp.bfloat16
HIGHEST = lax.Precision.HIGHEST

D_MODEL = 1024
DEPTH = 1
GLA_HEADS = 4
GLA_DK = 64
GLA_DV = 128
GLA_WIDTH = GLA_HEADS * GLA_DV
GLA_GATE_RANK = 16
GLA_GATE_NORM = 16.0
GDN_HEADS = 4
GDN_DK = 128
GDN_DV = 128
GDN_WIDTH = GDN_HEADS * GDN_DV
CONV_W = 4
CHUNK = 64
N_GROUPS = 8
EXPERTS_PER_GROUP = 8
N_EXPERTS = N_GROUPS * EXPERTS_PER_GROUP
TOP_K = 2
D_EXPERT = 512
LN_EPS = 1e-5
RMS_EPS = 1e-6
ALPHA_DN = (2.0 * DEPTH) ** 0.25

MAIN_W = 2 * GLA_HEADS * GLA_DK + 2 * GLA_WIDTH + 4 * GDN_WIDTH
AUX_W = 128
AUX_LRA = 0
AUX_BB = GLA_GATE_RANK
AUX_AB = GLA_GATE_RANK + GDN_HEADS
SUB = 16
VMEM_LIMIT = 56 * 1024 * 1024


def _cparams(*sem):
    return pltpu.CompilerParams(dimension_semantics=sem, vmem_limit_bytes=VMEM_LIMIT)


def _sigmoid(x):
    return 1.0 / (1.0 + jnp.exp(-x))


def _silu(x):
    return x * _sigmoid(x)


def _log_sigmoid(x):
    return jnp.minimum(x, 0.0) - jnp.log(1.0 + jnp.exp(-jnp.abs(x)))


def _softplus(x):
    return jnp.maximum(x, 0.0) + jnp.log(1.0 + jnp.exp(-jnp.abs(x)))


def _chunk_cumsum(x, c):
    pos = lax.broadcasted_iota(jnp.int32, x.shape, 0) % c
    shift = 1
    while shift < c:
        x = x + jnp.where(pos >= shift, pltpu.roll(x, shift, 0), 0.0)
        shift *= 2
    return x


CONV_PAD = 8
GDN_HW = GDN_HEADS * GDN_DK
GDN_QKV_BLOCK0 = (2 * GLA_HEADS * GLA_DK + 2 * GLA_WIDTH) // GDN_HW


def _in_proj_kernel(x_ref, wm_ref, wa_ref, cw_ref, main_ref, aux_ref, stage_ref, *, tm, tiles_per_seq):
    xb = x_ref[...].astype(BF16)

    @pl.when((pl.program_id(0) % tiles_per_seq) == 0)
    def _():
        stage_ref[:, 0:CONV_PAD, :] = jnp.zeros((3 * GDN_HEADS, CONV_PAD, GDN_DK), F32)

    for j in range(MAIN_W // GDN_HW):
        sl = slice(j * GDN_HW, (j + 1) * GDN_HW)
        r = jnp.dot(xb, wm_ref[:, sl], preferred_element_type=F32)
        b = j - GDN_QKV_BLOCK0
        if not 0 <= b < 3:
            main_ref[:, sl] = r.astype(BF16)
            continue
        for h in range(GDN_HEADS):
            st = stage_ref.at[b * GDN_HEADS + h]
            cols = slice(b * GDN_HW + h * GDN_DK, b * GDN_HW + (h + 1) * GDN_DK)
            st[CONV_PAD:, :] = r[:, h * GDN_DK:(h + 1) * GDN_DK]
            acc = jnp.zeros((tm, GDN_DK), F32)
            for i in range(CONV_W):
                lo = CONV_PAD - (CONV_W - 1) + i
                acc = acc + st[lo:lo + tm, :] * cw_ref[i:i + 1, cols]
            st[0:CONV_PAD, :] = st[tm:tm + CONV_PAD, :]
            yh = _silu(acc)
            if b < 2:
                yh = yh * lax.rsqrt(jnp.sum(yh * yh, axis=-1, keepdims=True) + RMS_EPS)
            if b == 0:
                yh = yh * (GDN_DK ** -0.5)
            main_ref[:, j * GDN_HW + h * GDN_DK:j * GDN_HW + (h + 1) * GDN_DK] = yh.astype(BF16)
    aux_ref[...] = jnp.dot(xb, wa_ref[...], preferred_element_type=F32)


def _in_proj(x2d, w_main, w_aux, conv_w, *, seq, tm=512):
    n = x2d.shape[0]
    return pl.pallas_call(
        functools.partial(_in_proj_kernel, tm=tm, tiles_per_seq=seq // tm),
        grid=(n // tm,),
        in_specs=[pl.BlockSpec((tm, D_MODEL), lambda i: (i, 0)),
                  pl.BlockSpec((D_MODEL, MAIN_W), lambda i: (0, 0)),
                  pl.BlockSpec((D_MODEL, AUX_W), lambda i: (0, 0)),
                  pl.BlockSpec((CONV_W, 3 * GDN_HW), lambda i: (0, 0))],
        out_specs=[pl.BlockSpec((tm, MAIN_W), lambda i: (i, 0)),
                   pl.BlockSpec((tm, AUX_W), lambda i: (i, 0))],
        out_shape=[jax.ShapeDtypeStruct((n, MAIN_W), BF16),
                   jax.ShapeDtypeStruct((n, AUX_W), F32)],
        scratch_shapes=[pltpu.VMEM((3 * GDN_HEADS, tm + CONV_PAD, GDN_DK), F32)],
        compiler_params=_cparams("arbitrary"),
        name="in_proj",
    )(x2d, w_main, w_aux, conv_w)


def _split_w_in(w_in):
    sizes = (GLA_HEADS * GLA_DK, GLA_HEADS * GLA_DK, GLA_WIDTH, GLA_WIDTH, GLA_GATE_RANK,
             GDN_HEADS * GDN_DK, GDN_HEADS * GDN_DK, GDN_WIDTH, GDN_WIDTH, GDN_HEADS, GDN_HEADS)
    offs = np.cumsum((0,) + sizes)
    seg = [w_in[:, offs[i]:offs[i + 1]] for i in range(len(sizes))]
    qa, ka, va, ra, lra, qb, kb, vb, zb, bb, ab = seg
    w_main = jnp.concatenate([qa, ka, va, ra, qb, kb, vb, zb], axis=1).astype(BF16)
    pad = jnp.zeros((w_in.shape[0], AUX_W - GLA_GATE_RANK - 2 * GDN_HEADS), w_in.dtype)
    w_aux = jnp.concatenate([lra, bb, ab, pad], axis=1).astype(BF16)
    return w_main, w_aux


_NT = (((1,), (1,)), ((), ()))
_TN = (((0,), (0,)), ((), ()))


def _gla_kernel(q_ref, k_ref, v_ref, r_ref, aux_ref, wgk_ref, bgk_ref, g_ref, o_ref, st_ref, *, ts, nseq):
    npairs = 2 * nseq
    c = CHUNK
    nsub = c // SUB

    @pl.when(pl.program_id(1) == 0)
    def _():
        st_ref[...] = jnp.zeros_like(st_ref)

    gk_all = [_log_sigmoid(jnp.dot(aux_ref[sq], wgk_ref[...], precision=HIGHEST, preferred_element_type=F32)
                           + bgk_ref[...]) * (1.0 / GLA_GATE_NORM) for sq in range(nseq)]

    row128 = lax.broadcasted_iota(jnp.int32, (c, 128), 0)
    lane128 = lax.broadcasted_iota(jnp.int32, (1, 128), 1)
    lane_masks = [(lane128 // GLA_DK) == hh for hh in range(2)]
    sub_valid = [row128 < SUB * (i + 1) for i in range(nsub)]
    ar = lax.broadcasted_iota(jnp.int32, (c, nsub * c), 0)
    ac = lax.broadcasted_iota(jnp.int32, (c, nsub * c), 1)
    amask = ((ac // c) == (ar // SUB)) & ((ac % c) <= ar)
    scale = GLA_DK ** -0.5
    gnorm = g_ref[...]

    nch = ts // c
    rows = [slice(ch * c, (ch + 1) * c) for ch in range(nch)]
    b_cum = [_chunk_cumsum(g, c) for g in gk_all]

    cp = [(ch, p) for ch in range(nch) for p in range(npairs)]
    qt, qd, kd, kstack, dlast = {}, {}, {}, {}, {}
    for ch, p in cp:
        sq, pp = divmod(p, 2)
        lanes = slice(128 * pp, 128 * (pp + 1))
        bp = b_cum[sq][rows[ch], lanes]
        qp = q_ref[sq, rows[ch], lanes].astype(F32) * scale
        kp = k_ref[sq, rows[ch], lanes].astype(F32)
        c_row = jnp.concatenate(
            [jnp.broadcast_to(bp[SUB * i:SUB * i + 1, :], (SUB, 128)) for i in range(nsub)], axis=0)
        qt[ch, p] = (qp * jnp.exp(bp - c_row)).astype(BF16)
        qd[ch, p] = qp * jnp.exp(bp)
        b_last = bp[c - 1:c, :]
        dlast[ch, p] = jnp.exp(b_last)
        kd[ch, p] = kp * jnp.exp(b_last - bp)
        kts = []
        for i in range(nsub):
            e = jnp.exp(jnp.where(sub_valid[i], bp[SUB * i:SUB * i + 1, :] - bp, 0.0))
            kts.append(jnp.where(sub_valid[i], kp * e, 0.0))
        kstack[ch, p] = jnp.concatenate(kts, axis=0)

    cph = [(ch, p, hh) for ch, p in cp for hh in range(2)]
    vh = {(ch, p, hh): v_ref[p // 2, rows[ch], 128 * (2 * (p % 2) + hh):128 * (2 * (p % 2) + hh + 1)]
          for ch, p, hh in cph}
    r_mat = {(ch, p, hh): lax.dot_general(qt[ch, p], jnp.where(lane_masks[hh], kstack[ch, p], 0.0).astype(BF16),
                                          _NT, preferred_element_type=F32) for ch, p, hh in cph}
    o_intra = {k: jnp.dot(jnp.where(amask, r_mat[k], 0.0).astype(BF16), jnp.concatenate([vh[k]] * nsub, axis=0),
                          preferred_element_type=F32) for k in cph}
    upd = {(ch, p, hh): lax.dot_general(vh[ch, p, hh], jnp.where(lane_masks[hh], kd[ch, p], 0.0).astype(BF16), _TN,
                                        preferred_element_type=F32) for ch, p, hh in cph}

    st = [st_ref[p] for p in range(npairs)]
    for ch in range(nch):
        st_b = [s.astype(BF16) for s in st]
        o_pair = [lax.dot_general(
            jnp.concatenate([jnp.where(lane_masks[hh], qd[ch, p], 0.0).astype(BF16) for hh in range(2)], axis=0),
            st_b[p], _NT, preferred_element_type=F32) for p in range(npairs)]
        o_inter = {(p, hh): o_pair[p][hh * c:(hh + 1) * c] for p in range(npairs) for hh in range(2)}
        st = [st[p] * dlast[ch, p] + upd[ch, p, 0] + upd[ch, p, 1] for p in range(npairs)]
        for p in range(npairs):
            for hh in range(2):
                sq, h = p // 2, 2 * (p % 2) + hh
                o = o_intra[ch, p, hh] + o_inter[p, hh]
                o = o * lax.rsqrt(jnp.mean(o * o, axis=-1, keepdims=True) + RMS_EPS) * gnorm
                gate = _silu(r_ref[sq, rows[ch], 128 * h:128 * (h + 1)].astype(F32))
                o_ref[sq, rows[ch], 128 * h:128 * (h + 1)] = (o * gate).astype(BF16)
    for p in range(npairs):
        st_ref[p] = st[p]


def _gla(main, aux, wgk_pad, b_gk, gla_norm_g, *, bsz, seq, ts=512, nseq=2):
    nt = seq // ts
    assert bsz % nseq == 0
    main4 = main.reshape(bsz // nseq, nseq, seq, MAIN_W)
    aux4 = aux.reshape(bsz // nseq, nseq, seq, AUX_W)
    tok = lambda width, col: pl.BlockSpec((None, nseq, ts, width), lambda b, t: (b, 0, t, col))
    out = pl.pallas_call(
        functools.partial(_gla_kernel, ts=ts, nseq=nseq),
        grid=(bsz // nseq, nt),
        in_specs=[tok(256, 0), tok(256, 1), tok(512, 1), tok(512, 2),
                  tok(AUX_W, 0),
                  pl.BlockSpec((AUX_W, 256), lambda b, t: (0, 0)),
                  pl.BlockSpec((1, 256), lambda b, t: (0, 0)),
                  pl.BlockSpec((1, 128), lambda b, t: (0, 0))],
        out_specs=tok(GLA_WIDTH, 0),
        out_shape=jax.ShapeDtypeStruct((bsz // nseq, nseq, seq, GLA_WIDTH), BF16),
        scratch_shapes=[pltpu.VMEM((2 * nseq, 128, 128), F32)],
        compiler_params=_cparams("parallel", "arbitrary"),
        name="gla",
    )(main4, main4, main4, main4, aux4, wgk_pad, b_gk, gla_norm_g)
    return out.reshape(bsz * seq, GLA_WIDTH)


def _gdn_kernel(q_ref, k_ref, v_ref, z_ref, aux_ref, alog_ref, dtb_ref, g_ref, o_ref, st_ref, *, ts, nseq):
    c = CHUNK
    nch = ts // c

    @pl.when(pl.program_id(1) == 0)
    def _():
        st_ref[...] = jnp.zeros_like(st_ref)

    beta_full, gam_full, gam_t = [], [], []
    for sq in range(nseq):
        aux = aux_ref[sq]
        beta_full.append(_sigmoid(aux))
        g_full = -jnp.exp(alog_ref[...]) * _softplus(aux + dtb_ref[...])
        gam_full.append(_chunk_cumsum(g_full, c))
        gam_t.append(gam_full[-1].T)
    gnorm = g_ref[...]

    row = lax.broadcasted_iota(jnp.int32, (c, 2 * c), 0)
    lane = lax.broadcasted_iota(jnp.int32, (c, 2 * c), 1)
    left = lane < c
    col = lane % c
    causal = col <= row
    strict = col < row
    same32 = (row // 32) == (col // 32)
    same16 = (row // 16) == (col // 16)
    lvl0 = same16 & strict
    lvl1 = same32 & jnp.logical_not(same16) & strict
    lvl2 = jnp.logical_not(same32) & strict

    def pack_rows(x2):
        return jnp.where(left, x2[0:c], x2[c:2 * c])

    def pack_cols(x1):
        return jnp.where(left, x1[0:c], x1[c:2 * c])

    def halves(x):
        return jnp.where(left, x, 0.0).astype(BF16), jnp.where(left, 0.0, x).astype(BF16)

    def mmp(x, y):
        return jnp.dot(x.astype(BF16), jnp.concatenate(halves(y), axis=0), preferred_element_type=F32)

    heads = range(nseq * GDN_HEADS)
    npair = nch // 2
    items = [(h, pr) for h in heads for pr in range(npair)]
    prow = [slice(2 * c * pr, 2 * c * (pr + 1)) for pr in range(npair)]

    qn, kn, beta, gam, gam_r, kb, rhs, rhs_b, qd, kd = ([None] * len(heads) for _ in range(10))
    for h in heads:
        sq, hh = divmod(h, GDN_HEADS)
        hl = slice(128 * hh, 128 * (hh + 1))
        kb[h] = k_ref[sq, :, hl]
        qn[h] = q_ref[sq, :, hl].astype(F32)
        kn[h] = kb[h].astype(F32)
        vh = v_ref[sq, :, hl].astype(F32)
        beta[h] = beta_full[sq][:, AUX_BB + hh:AUX_BB + hh + 1]
        gam[h] = gam_full[sq][:, AUX_AB + hh:AUX_AB + hh + 1]
        gam_r[h] = gam_t[sq][AUX_AB + hh:AUX_AB + hh + 1, :]
        egam = jnp.exp(gam[h])
        rhs[h] = jnp.concatenate([beta[h] * vh, beta[h] * egam * kn[h]], axis=1)
        rhs_b[h] = rhs[h].astype(BF16)
        qd[h] = (qn[h] * egam).astype(BF16)
        gl_rows = jnp.concatenate(
            [jnp.broadcast_to(gam[h][c * (j + 1) - 1:c * (j + 1), :], (c, 1)) for j in range(nch)], axis=0)
        kd[h] = (kn[h] * jnp.exp(gl_rows - gam[h])).astype(BF16)

    kk = [pack_rows(lax.dot_general(kb[h][prow[pr]], kb[h][prow[pr]], _NT, preferred_element_type=F32))
          for h, pr in items]
    qk = [pack_rows(lax.dot_general(qn[h][prow[pr]].astype(BF16), kb[h][prow[pr]], _NT,
                                    preferred_element_type=F32)) for h, pr in items]
    dec = [jnp.exp(jnp.where(causal, pack_cols(gam[h][prow[pr]]) - gam_r[h][:, prow[pr]], 0.0)) for h, pr in items]
    a = [pack_cols(beta[h][prow[pr]]) * kk[i] * dec[i] for i, (h, pr) in enumerate(items)]
    attn = [halves(jnp.where(causal, qk[i] * dec[i], 0.0)) for i in range(len(items))]

    x1 = [jnp.where(lvl0, -ai, 0.0) for ai in a]
    x2 = [mmp(x, x) for x in x1]
    y = [mmp(x, xx) for x, xx in zip(x1, x2)]
    p = [x + xx + yy for x, xx, yy in zip(x1, x2, y)]
    xk = x2
    for _ in range(2):
        xk = [mmp(x, x) for x in xk]
        y = [mmp(pp, x) for pp, x in zip(p, xk)]
        p = [pp + x + yy for pp, x, yy in zip(p, xk, y)]
    for lvl in (lvl1, lvl2):
        low = [jnp.where(lvl, ai, 0.0) for ai in a]
        y = [mmp(pp, lo) for pp, lo in zip(p, low)]
        zz = [lo + yy for lo, yy in zip(low, y)]
        y = [mmp(z, pp) for z, pp in zip(zz, p)]
        p = [pp - z - yy for pp, z, yy in zip(p, zz, y)]

    us = [[None] * nch for _ in heads]
    ws = [[None] * nch for _ in heads]
    for i, (h, pr) in enumerate(items):
        for half, p_half in enumerate(halves(p[i])):
            j = 2 * pr + half
            uw = rhs[h][c * j:c * (j + 1)] + jnp.dot(p_half, rhs_b[h][prow[pr]], preferred_element_type=F32)
            us[h][j] = uw[:, :GDN_DV]
            ws[h][j] = uw[:, GDN_DV:].astype(BF16)

    s = [st_ref[h] for h in heads]
    vprev = [None] * len(heads)
    for j in range(nch):
        rows = slice(c * j, c * (j + 1))
        sb = [s[h].astype(BF16) for h in heads]
        ws_qd = [jnp.dot(jnp.concatenate([ws[h][j], qd[h][rows]], axis=0), sb[h], preferred_element_type=F32)
                 for h in heads]
        vn16 = [(us[h][j] - ws_qd[h][0:c]).astype(BF16) for h in heads]
        o_inter = [ws_qd[h][c:2 * c] for h in heads]
        for h in heads:
            vpair = jnp.concatenate([vn16[h], jnp.zeros_like(vn16[h])] if j % 2 == 0 else [vprev[h], vn16[h]], axis=0)
            o = o_inter[h] + jnp.dot(attn[h * npair + j // 2][j % 2], vpair, preferred_element_type=F32)
            s[h] = jnp.exp(gam[h][c * (j + 1) - 1:c * (j + 1), :]) * s[h] + lax.dot_general(
                kd[h][rows], vn16[h], _TN, preferred_element_type=F32)
            o = o * lax.rsqrt(jnp.mean(o * o, axis=-1, keepdims=True) + RMS_EPS) * gnorm
            sq, hh = divmod(h, GDN_HEADS)
            gate = _silu(z_ref[sq, rows, 128 * hh:128 * (hh + 1)].astype(F32))
            o_ref[sq, rows, 128 * hh:128 * (hh + 1)] = (o * gate).astype(BF16)
        vprev = vn16
    for h in heads:
        st_ref[h] = s[h]


def _gdn(main, aux, alog_pad, dtb_pad, gdn_norm_g, *, bsz, seq, ts=512, nseq=2):
    nt = seq // ts
    hw = GDN_HEADS * GDN_DK
    assert bsz % nseq == 0
    main4 = main.reshape(bsz // nseq, nseq, seq, MAIN_W)
    aux4 = aux.reshape(bsz // nseq, nseq, seq, AUX_W)
    tok = lambda col: pl.BlockSpec((None, nseq, ts, hw), lambda b, t: (b, 0, t, col))
    out = pl.pallas_call(
        functools.partial(_gdn_kernel, ts=ts, nseq=nseq),
        grid=(bsz // nseq, nt),
        in_specs=[tok(3), tok(4), tok(5), tok(6),
                  pl.BlockSpec((None, nseq, ts, AUX_W), lambda b, t: (b, 0, t, 0)),
                  pl.BlockSpec((1, AUX_W), lambda b, t: (0, 0)),
                  pl.BlockSpec((1, AUX_W), lambda b, t: (0, 0)),
                  pl.BlockSpec((1, GDN_DV), lambda b, t: (0, 0))],
        out_specs=pl.BlockSpec((None, nseq, ts, GDN_WIDTH), lambda b, t: (b, 0, t, 0)),
        out_shape=jax.ShapeDtypeStruct((bsz // nseq, nseq, seq, GDN_WIDTH), BF16),
        scratch_shapes=[pltpu.VMEM((nseq * GDN_HEADS, GDN_DK, GDN_DV), F32)],
        compiler_params=_cparams("parallel", "arbitrary"),
        name="gdn",
    )(main4, main4, main4, main4, aux4, alog_pad, dtb_pad, gdn_norm_g)
    return out.reshape(bsz * seq, GDN_WIDTH)


def _pad_aux_row(v, offset):
    return jnp.zeros((1, AUX_W), F32).at[0, offset:offset + v.shape[0]].set(v.astype(F32))


ROUTE_W = 128
R_E1, R_E2, R_RANK1, R_RANK2, R_G1, R_G2 = range(6)
ROUTER_GROUP_COL = N_EXPERTS
POST_MIX_SPLIT = 4


def _layer_norm(h, g, b):
    mu = jnp.mean(h, axis=-1, keepdims=True)
    hc = h - mu
    var = jnp.mean(hc * hc, axis=-1, keepdims=True)
    return hc * lax.rsqrt(var + LN_EPS) * g + b


def _pack_bf16_pairs(x):
    w = x.shape[1] // 2
    lo = lax.bitcast_convert_type(x[:, :w].astype(BF16).astype(F32), jnp.uint32)
    hi = lax.bitcast_convert_type(x[:, w:].astype(BF16).astype(F32), jnp.uint32)
    return (lo >> 16) | (hi & jnp.uint32(0xFFFF0000))


def _unpack_bf16_pairs(p):
    lo = lax.bitcast_convert_type(p << 16, F32)
    hi = lax.bitcast_convert_type(p & jnp.uint32(0xFFFF0000), F32)
    return lo, hi


HALF = D_MODEL // 2
TOK_ROWS = HALF // 128


def _store_token_rows(ref, packed):
    m = packed.shape[0]
    for c in range(TOK_ROWS):
        ref[pl.ds(c, m, stride=TOK_ROWS), :] = packed[:, 128 * c:128 * (c + 1)]


def _load_token_rows(ref, m):
    return jnp.concatenate([ref[pl.ds(c, m, stride=TOK_ROWS), :] for c in range(TOK_ROWS)], axis=1)


def _post_mix_kernel(oa_ref, ob_ref, x_ref, wo_ref, g_ref, b_ref, wr_ref, br_ref,
                     x1p_ref, route_ref, gates_ref, cnt_ref, carry_ref, *, tm):
    @pl.when(pl.program_id(0) == 0)
    def _():
        carry_ref[...] = jnp.zeros_like(carry_ref)

    m = tm // POST_MIX_SPLIT
    groups = range(POST_MIX_SPLIT)
    rs = [slice(h * m, (h + 1) * m) for h in groups]
    y = [jnp.dot(oa_ref[rs[h], :], wo_ref[0:GLA_WIDTH, :], preferred_element_type=F32)
         + jnp.dot(ob_ref[rs[h], :], wo_ref[GLA_WIDTH:, :], preferred_element_type=F32) for h in groups]
    x1 = [_layer_norm(ALPHA_DN * x_ref[rs[h], :] + y[h], g_ref[...], b_ref[...]) for h in groups]
    for h in groups:
        _store_token_rows(x1p_ref.at[pl.ds(h * m * TOK_ROWS, m * TOK_ROWS), :], _pack_bf16_pairs(x1[h]))

    xh = [x.astype(BF16) for x in x1]
    xl = [(x - xb.astype(F32)).astype(BF16) for x, xb in zip(x1, xh)]
    hw = [lax.dot_general(wr_ref[...], xb, _NT, preferred_element_type=F32) for xb in xh]
    lw = [lax.dot_general(wr_ref[0:ROUTE_W, :], xb, _NT, preferred_element_type=F32) for xb in xl]
    logits = [a[:ROUTE_W] + a[ROUTE_W:] + b + br_ref[...] for a, b in zip(hw, lw)]
    sub = lax.broadcasted_iota(jnp.int32, (ROUTE_W, m), 0)
    big = jnp.int32(1 << 20)
    neg = jnp.float32(-jnp.inf)

    def first_argmax(vals):
        mx = jnp.max(vals, axis=0, keepdims=True)
        idx = jnp.min(jnp.where(vals == mx, sub, big), axis=0, keepdims=True)
        return mx, idx

    is_group = (sub >= ROUTER_GROUP_COL) & (sub < ROUTER_GROUP_COL + N_GROUPS)
    gl = [jnp.where(is_group, lg, neg) for lg in logits]
    gtop = [first_argmax(v) for v in gl]
    p_group = [1.0 / jnp.sum(jnp.exp(v - t[0]), axis=0, keepdims=True) for v, t in zip(gl, gtop)]
    el = [jnp.where((sub // EXPERTS_PER_GROUP) == (t[1] - ROUTER_GROUP_COL), lg, neg) for lg, t in zip(logits, gtop)]
    top1 = [first_argmax(v) for v in el]
    top2 = [first_argmax(jnp.where(sub == t[1], neg, v)) for v, t in zip(el, top1)]
    ex = [jnp.exp(b[0] - a[0]) for a, b in zip(top1, top2)]
    w1 = [1.0 / (1.0 + e) for e in ex]
    gate1 = [p * w for p, w in zip(p_group, w1)]
    gate2 = [p * (e * w) for p, e, w in zip(p_group, ex, w1)]

    oh1 = [sub == t[1] for t in top1]
    oh2 = [sub == t[1] for t in top2]
    ri = lax.broadcasted_iota(jnp.int32, (m, m), 0)
    ci = lax.broadcasted_iota(jnp.int32, (m, m), 1)
    ustrict = (ri < ci).astype(BF16)
    oh1f = [o.astype(F32) for o in oh1]
    oh2f = [o.astype(F32) for o in oh2]
    c1 = [jnp.dot(o.astype(BF16), ustrict, preferred_element_type=F32) for o in oh1f]
    c2 = [jnp.dot(o.astype(BF16), ustrict, preferred_element_type=F32) for o in oh2f]
    tot1 = [jnp.sum(o, axis=1, keepdims=True) for o in oh1f]
    tot2 = [jnp.sum(o, axis=1, keepdims=True) for o in oh2f]
    carry = carry_ref[...]
    eye8 = jnp.eye(8, ROUTE_W, dtype=F32)
    zero = jnp.zeros((1, m), F32)
    for h in groups:
        rank1 = jnp.sum(jnp.where(oh1[h], c1[h] + carry, 0.0), axis=0, keepdims=True)
        rank2 = jnp.sum(jnp.where(oh2[h], c2[h] + (carry + tot1[h]), 0.0), axis=0, keepdims=True)
        carry = carry + tot1[h] + tot2[h]
        route_t = jnp.concatenate([top1[h][1].astype(F32), top2[h][1].astype(F32), rank1, rank2,
                                   gate1[h], gate2[h], zero, zero], axis=0)
        route_ref[:, rs[h]] = route_t
        gates_ref[rs[h], :] = lax.dot_general(route_t, eye8, _TN, precision=HIGHEST, preferred_element_type=F32)
    carry_ref[...] = carry
    cnt_ref[...] = carry


def _post_mix(o_a, o_b, x2d, w_out_b, ln_g, ln_b, w_router, b_router, *, tm=1024):
    n = x2d.shape[0]
    return pl.pallas_call(
        functools.partial(_post_mix_kernel, tm=tm),
        grid=(n // tm,),
        in_specs=[pl.BlockSpec((tm, GLA_WIDTH), lambda i: (i, 0)),
                  pl.BlockSpec((tm, GDN_WIDTH), lambda i: (i, 0)),
                  pl.BlockSpec((tm, D_MODEL), lambda i: (i, 0)),
                  pl.BlockSpec((GLA_WIDTH + GDN_WIDTH, D_MODEL), lambda i: (0, 0)),
                  pl.BlockSpec((1, D_MODEL), lambda i: (0, 0)),
                  pl.BlockSpec((1, D_MODEL), lambda i: (0, 0)),
                  pl.BlockSpec((2 * ROUTE_W, D_MODEL), lambda i: (0, 0)),
                  pl.BlockSpec((ROUTE_W, 1), lambda i: (0, 0))],
        out_specs=[pl.BlockSpec((tm * TOK_ROWS, 128), lambda i: (i, 0)),
                   pl.BlockSpec((8, tm), lambda i: (0, i)),
                   pl.BlockSpec((tm, ROUTE_W), lambda i: (i, 0)),
                   pl.BlockSpec((ROUTE_W, 1), lambda i: (0, 0))],
        out_shape=[jax.ShapeDtypeStruct((n * TOK_ROWS, 128), jnp.uint32),
                   jax.ShapeDtypeStruct((8, n), F32),
                   jax.ShapeDtypeStruct((n, ROUTE_W), F32),
                   jax.ShapeDtypeStruct((ROUTE_W, 1), F32)],
        scratch_shapes=[pltpu.VMEM((ROUTE_W, 1), F32)],
        compiler_params=_cparams("arbitrary"),
        name="post_mix",
    )(o_a, o_b, x2d, w_out_b, ln_g, ln_b, w_router, b_router)


EXPERT_BLOCK = 256
BLOCKS_PER_STEP = 4

ROW_UNROLL = 8


def _plan_kernel(route_ref, cnt_ref, dest_ref, *, tm, tb):
    counts = cnt_ref[...].astype(jnp.int32)
    padded = jnp.broadcast_to(((counts + (tb - 1)) & jnp.int32(-tb)).astype(F32), (ROUTE_W, ROUTE_W))
    pstart = (_chunk_cumsum(padded, ROUTE_W) - padded)[:, 0:1]
    route = route_ref[...]
    sub = lax.broadcasted_iota(jnp.int32, (ROUTE_W, tm), 0).astype(F32)
    rows = []
    for ce, cr in ((R_E1, R_RANK1), (R_E2, R_RANK2)):
        off = jnp.sum(jnp.where(sub == route[ce:ce + 1, :], pstart, 0.0), axis=0, keepdims=True)
        rows.append(off + route[cr:cr + 1, :])
    rows.extend([jnp.zeros((1, tm), F32)] * (8 - len(rows)))
    dest_ref[...] = jnp.concatenate(rows, axis=0).astype(jnp.int32) * TOK_ROWS


def _plan(route, counts, *, tb, tm=2048):
    n = route.shape[1]
    tm = min(tm, n)
    return pl.pallas_call(
        functools.partial(_plan_kernel, tm=tm, tb=tb),
        grid=(n // tm,),
        in_specs=[pl.BlockSpec((8, tm), lambda i: (0, i)),
                  pl.BlockSpec((ROUTE_W, 1), lambda i: (0, 0))],
        out_specs=pl.BlockSpec((8, tm), lambda i: (0, i)),
        out_shape=jax.ShapeDtypeStruct((8, n), jnp.int32),
        compiler_params=_cparams("parallel"),
        name="plan",
    )(route, counts)


def _dispatch_kernel(d1_ref, d2_ref, last_blk_ref, nvalid_ref, xp_ref, xs_ref, zbuf, sem, zsem, *, tm, tb, nb):
    base = pl.program_id(0) * tm
    rows = tb * TOK_ROWS

    @pl.when(pl.program_id(0) == 0)
    def _():
        zbuf[...] = jnp.zeros_like(zbuf)

        def zero_copy(blk):
            return pltpu.make_async_copy(zbuf, xs_ref.at[pl.ds(pl.multiple_of(blk * rows, rows), rows), :], zsem)

        def for_each_block(fn):
            def per_expert(e, carry):
                @pl.when(last_blk_ref[e] >= 0)
                def _():
                    fn(zero_copy(last_blk_ref[e]))
                return carry

            def per_tail(blk, carry):
                fn(zero_copy(blk))
                return carry
            lax.fori_loop(0, N_EXPERTS, per_expert, 0)
            lax.fori_loop(nvalid_ref[0], nb, per_tail, 0)

        for_each_block(lambda cp: cp.start())
        for_each_block(lambda cp: cp.wait())

    def body(j, carry):
        for u in range(ROW_UNROLL):
            r = j * ROW_UNROLL + u
            for k, d_ref in enumerate((d1_ref, d2_ref)):
                dst = pl.multiple_of(d_ref[base + r], TOK_ROWS)
                pltpu.make_async_copy(xp_ref.at[pl.ds(r * TOK_ROWS, TOK_ROWS), :],
                                      xs_ref.at[pl.ds(dst, TOK_ROWS), :], sem).start(priority=k)
        return carry
    lax.fori_loop(0, tm // ROW_UNROLL, body, 0)
    for _ in range(TOP_K):
        pltpu.make_async_copy(xp_ref, xs_ref.at[pl.ds(0, tm * TOK_ROWS), :], sem).wait()


def _dispatch(dest1, dest2, last_blk, nvalid, x1p, *, nb, tb=EXPERT_BLOCK, tm=1024):
    n = x1p.shape[0] // TOK_ROWS
    grid_spec = pltpu.PrefetchScalarGridSpec(
        num_scalar_prefetch=4,
        grid=(n // tm,),
        in_specs=[pl.BlockSpec((tm * TOK_ROWS, 128), lambda i, a, b, lb, nv: (i, 0))],
        out_specs=pl.BlockSpec(memory_space=pl.ANY),
        scratch_shapes=[pltpu.VMEM((tb * TOK_ROWS, 128), jnp.uint32),
                        pltpu.SemaphoreType.DMA(()),
                        pltpu.SemaphoreType.DMA(())],
    )
    return pl.pallas_call(
        functools.partial(_dispatch_kernel, tm=tm, tb=tb, nb=nb),
        grid_spec=grid_spec,
        out_shape=jax.ShapeDtypeStruct((nb * tb * TOK_ROWS, 128), jnp.uint32),
        compiler_params=_cparams("arbitrary"),
        name="dispatch",
    )(dest1, dest2, last_blk, nvalid, x1p)


def _expert_kernel(blk_e_ref, first_ref, next_e_ref, nvalid_ref, xs_ref, wg_hbm, wu_hbm, wd_hbm, y_ref,
                   wgs, wus, wds, wgb, wub, wdb, sem, *, tb):
    rows = tb * TOK_ROWS

    def weight_copies(e):
        return [pltpu.make_async_copy(src.at[e], dst, sem.at[j])
                for j, (src, dst) in enumerate(((wg_hbm, wgs), (wu_hbm, wus), (wd_hbm, wds)))]

    @pl.when((pl.program_id(0) == 0) & (nvalid_ref[0] > 0))
    def _():
        for cp in weight_copies(blk_e_ref[0]):
            cp.start()

    for sub in range(BLOCKS_PER_STEP):
        i = pl.program_id(0) * BLOCKS_PER_STEP + sub
        xs_blk = xs_ref.at[pl.ds(sub * rows, rows), :]
        y_blk = y_ref.at[pl.ds(sub * rows, rows), :]

        @pl.when(i < nvalid_ref[0])
        def _():
            @pl.when(first_ref[i] == 1)
            def _():
                for cp in weight_copies(blk_e_ref[i]):
                    cp.wait()
                wgb[...] = wgs[...].astype(BF16)
                wub[...] = wus[...].astype(BF16)
                wdb[...] = wds[...].astype(BF16)

                @pl.when(next_e_ref[i] >= 0)
                def _():
                    for cp in weight_copies(next_e_ref[i]):
                        cp.start()

            lo, hi = _unpack_bf16_pairs(_load_token_rows(xs_blk, tb))
            xb = jnp.concatenate([lo.astype(BF16), hi.astype(BF16)], axis=1)
            hg = jnp.dot(xb, wgb[...], preferred_element_type=F32)
            hu = jnp.dot(xb, wub[...], preferred_element_type=F32)
            hid = (_silu(hg) * hu).astype(BF16)
            _store_token_rows(y_blk, _pack_bf16_pairs(jnp.dot(hid, wdb[...], preferred_element_type=F32)))

        @pl.when(i >= nvalid_ref[0])
        def _():
            y_blk[...] = jnp.zeros((rows, 128), jnp.uint32)


def _experts(blk_e, first, next_e, nvalid, xs, w_gate, w_up, w_down, *, tb=EXPERT_BLOCK):
    nb = blk_e.shape[0]
    assert nb % BLOCKS_PER_STEP == 0
    step_rows = BLOCKS_PER_STEP * tb * TOK_ROWS
    last_step = lambda nv: jnp.maximum((nv[0] - 1) // BLOCKS_PER_STEP, 0)
    grid_spec = pltpu.PrefetchScalarGridSpec(
        num_scalar_prefetch=4,
        grid=(nb // BLOCKS_PER_STEP,),
        in_specs=[pl.BlockSpec((step_rows, 128), lambda s, be, fr, ne, nv: (jnp.minimum(s, last_step(nv)), 0)),
                  pl.BlockSpec(memory_space=pl.ANY),
                  pl.BlockSpec(memory_space=pl.ANY),
                  pl.BlockSpec(memory_space=pl.ANY)],
        out_specs=pl.BlockSpec((step_rows, 128), lambda s, be, fr, ne, nv: (s, 0)),
        scratch_shapes=[pltpu.VMEM((D_MODEL, D_EXPERT), F32),
                        pltpu.VMEM((D_MODEL, D_EXPERT), F32),
                        pltpu.VMEM((D_EXPERT, D_MODEL), F32),
                        pltpu.VMEM((D_MODEL, D_EXPERT), BF16),
                        pltpu.VMEM((D_MODEL, D_EXPERT), BF16),
                        pltpu.VMEM((D_EXPERT, D_MODEL), BF16),
                        pltpu.SemaphoreType.DMA((3,))],
    )
    return pl.pallas_call(
        functools.partial(_expert_kernel, tb=tb),
        grid_spec=grid_spec,
        out_shape=jax.ShapeDtypeStruct((nb * tb * TOK_ROWS, 128), jnp.uint32),
        compiler_params=_cparams("arbitrary"),
        name="experts",
    )(blk_e, first, next_e, nvalid, xs, w_gate, w_up, w_down)


def _combine_kernel(d1_ref, d2_ref, y_hbm, x1_ref, route_ref, g_ref, b_ref, o_ref, ybuf, sem, *, tt):
    i = pl.program_id(0)
    nsteps = pl.num_programs(0)
    slot = i % 2

    def issue(tile, slot_):
        def body(j, carry):
            for u in range(ROW_UNROLL):
                r = j * ROW_UNROLL + u
                for k, d_ref in enumerate((d1_ref, d2_ref)):
                    src = pl.multiple_of(d_ref[tile * tt + r], TOK_ROWS)
                    pltpu.make_async_copy(y_hbm.at[pl.ds(src, TOK_ROWS), :],
                                          ybuf.at[slot_, k, pl.ds(r * TOK_ROWS, TOK_ROWS), :],
                                          sem.at[slot_]).start(priority=k)
            return carry
        lax.fori_loop(0, tt // ROW_UNROLL, body, 0)

    @pl.when(i == 0)
    def _():
        issue(0, 0)

    @pl.when(i + 1 < nsteps)
    def _():
        issue(i + 1, 1 - slot)

    for k in range(TOP_K):
        pltpu.make_async_copy(y_hbm.at[pl.ds(0, tt * TOK_ROWS), :], ybuf.at[slot, k], sem.at[slot]).wait()
    route = route_ref[...]
    g1 = route[:, R_G1:R_G1 + 1]
    g2 = route[:, R_G2:R_G2 + 1]
    lo1, hi1 = _unpack_bf16_pairs(_load_token_rows(ybuf.at[slot, 0], tt))
    lo2, hi2 = _unpack_bf16_pairs(_load_token_rows(ybuf.at[slot, 1], tt))
    x_lo, x_hi = _unpack_bf16_pairs(_load_token_rows(x1_ref, tt))
    h_lo = ALPHA_DN * x_lo + (g1 * lo1 + g2 * lo2)
    h_hi = ALPHA_DN * x_hi + (g1 * hi1 + g2 * hi2)
    mu = (jnp.sum(h_lo, axis=-1, keepdims=True) + jnp.sum(h_hi, axis=-1, keepdims=True)) * (1.0 / D_MODEL)
    c_lo = h_lo - mu
    c_hi = h_hi - mu
    var = (jnp.sum(c_lo * c_lo, axis=-1, keepdims=True) + jnp.sum(c_hi * c_hi, axis=-1, keepdims=True)) * (1.0 / D_MODEL)
    inv = lax.rsqrt(var + LN_EPS)
    o_ref[:, 0:HALF] = c_lo * inv * g_ref[:, 0:HALF] + b_ref[:, 0:HALF]
    o_ref[:, HALF:] = c_hi * inv * g_ref[:, HALF:] + b_ref[:, HALF:]


def _combine(dest1, dest2, ybuf, x1p, route, ln_g, ln_b, *, tt=512):
    n = x1p.shape[0] // TOK_ROWS
    grid_spec = pltpu.PrefetchScalarGridSpec(
        num_scalar_prefetch=2,
        grid=(n // tt,),
        in_specs=[pl.BlockSpec(memory_space=pl.ANY),
                  pl.BlockSpec((tt * TOK_ROWS, 128), lambda i, a, b: (i, 0)),
                  pl.BlockSpec((tt, ROUTE_W), lambda i, a, b: (i, 0)),
                  pl.BlockSpec((1, D_MODEL), lambda i, a, b: (0, 0)),
                  pl.BlockSpec((1, D_MODEL), lambda i, a, b: (0, 0))],
        out_specs=pl.BlockSpec((tt, D_MODEL), lambda i, a, b: (i, 0)),
        scratch_shapes=[pltpu.VMEM((2, TOP_K, tt * TOK_ROWS, 128), jnp.uint32),
                        pltpu.SemaphoreType.DMA((2,))],
    )
    return pl.pallas_call(
        functools.partial(_combine_kernel, tt=tt),
        grid_spec=grid_spec,
        out_shape=jax.ShapeDtypeStruct((n, D_MODEL), F32),
        compiler_params=_cparams("arbitrary"),
        name="combine",
    )(dest1, dest2, ybuf, x1p, route, ln_g, ln_b)


def _block_plan(counts_row, n, tb):
    counts = counts_row[:N_EXPERTS, 0].astype(jnp.int32)
    pends = jnp.cumsum((counts + tb - 1) // tb * tb)
    nb = (n * TOP_K) // tb + N_EXPERTS
    nvalid = (pends[-1] // tb).astype(jnp.int32)
    blk = jnp.arange(nb, dtype=jnp.int32)
    blk_e = jnp.sum((pends[None, :] <= (blk * tb)[:, None]).astype(jnp.int32), axis=1)
    blk_e = jnp.minimum(blk_e, N_EXPERTS - 1)
    blk_e = jnp.where(blk < nvalid, blk_e, blk_e[jnp.maximum(nvalid - 1, 0)])
    first = jnp.concatenate([jnp.ones((1,), jnp.int32), (blk_e[1:] != blk_e[:-1]).astype(jnp.int32)])
    ids = jnp.arange(N_EXPERTS, dtype=jnp.int32)
    later = lax.cummin(jnp.where(counts > 0, ids, N_EXPERTS), axis=0, reverse=True)
    nxt = jnp.concatenate([later[1:], jnp.full((1,), N_EXPERTS, jnp.int32)])
    nxt = jnp.where(nxt < N_EXPERTS, nxt, -1)
    last_blk = jnp.where(counts > 0, pends // tb - 1, -1).astype(jnp.int32)
    return blk_e, first, nxt[blk_e], nvalid.reshape(1), last_blk


def kernel(x, w_in, w_gk_up, b_gk, conv_w, a_log, dt_bias, gla_norm_g, gdn_norm_g, w_out, ln1_g, ln1_b, w_router_group, b_router_group, w_router_expert, b_router_expert, w_gate, w_up, w_down, ln2_g, ln2_b):
    bsz, seq, d = x.shape
    n = bsz * seq
    x2d = x.reshape(n, d)
    w_main, w_aux = _split_w_in(w_in[0])
    main, aux = _in_proj(x2d, w_main, w_aux, conv_w[0].astype(F32), seq=seq)
    wgk = jnp.zeros((AUX_W, 256), F32).at[:GLA_GATE_RANK].set(w_gk_up[0])
    o_a = _gla(main, aux, wgk, b_gk[0][None], gla_norm_g[0][None], bsz=bsz, seq=seq)
    o_b = _gdn(main, aux, _pad_aux_row(a_log[0], AUX_AB), _pad_aux_row(dt_bias[0], AUX_AB),
               gdn_norm_g[0][None], bsz=bsz, seq=seq)
    rpad = jnp.zeros((ROUTE_W - N_EXPERTS - N_GROUPS, d), F32)
    w_router = jnp.concatenate([w_router_expert[0].T, w_router_group[0].T, rpad], axis=0)
    w_router_hi = w_router.astype(BF16)
    w_router = jnp.concatenate([w_router_hi, (w_router - w_router_hi.astype(F32)).astype(BF16)], axis=0)
    b_router = jnp.concatenate([b_router_expert[0], b_router_group[0], rpad[:, 0]])[:, None]
    x1p, route, gates, counts = _post_mix(o_a, o_b, x2d, w_out[0].astype(BF16), ln1_g[0][None], ln1_b[0][None],
                                          w_router, b_router)
    dest = _plan(route, counts, tb=EXPERT_BLOCK)
    blk_e, first, next_e, nvalid, last_blk = _block_plan(counts, n, EXPERT_BLOCK)
    xs = _dispatch(dest[0], dest[1], last_blk, nvalid, x1p, nb=blk_e.shape[0])
    ybuf = _experts(blk_e, first, next_e, nvalid, xs, w_gate[0], w_up[0], w_down[0])
    out = _combine(dest[0], dest[1], ybuf, x1p, gates, ln2_g[0][None], ln2_b[0][None])
    return out.reshape(bsz, seq, d)
```

```python
import functools

import jax
import jax.numpy as jnp
import numpy as np
from jax import lax
from jax.experimental import pallas as pl
from jax.experimental.pallas import tpu as pltpu

F32 = jnp.float32
BF16 = jnp.bfloat16
HIGHEST = lax.Precision.HIGHEST

D_MODEL = 1024
DEPTH = 1
GLA_HEADS = 4
GLA_DK = 64
GLA_DV = 128
GLA_WIDTH = GLA_HEADS * GLA_DV
GLA_GATE_RANK = 16
GLA_GATE_NORM = 16.0
GDN_HEADS = 4
GDN_DK = 128
GDN_DV = 128
GDN_WIDTH = GDN_HEADS * GDN_DV
CONV_W = 4
CHUNK = 64
N_GROUPS = 8
EXPERTS_PER_GROUP = 8
N_EXPERTS = N_GROUPS * EXPERTS_PER_GROUP
TOP_K = 2
D_EXPERT = 512
LN_EPS = 1e-5
RMS_EPS = 1e-6
ALPHA_DN = (2.0 * DEPTH) ** 0.25

MAIN_W = 2 * GLA_HEADS * GLA_DK + 2 * GLA_WIDTH + 4 * GDN_WIDTH
AUX_W = 128
AUX_LRA = 0
AUX_BB = GLA_GATE_RANK
AUX_AB = GLA_GATE_RANK + GDN_HEADS
SUB = 16
VMEM_LIMIT = 56 * 1024 * 1024


def _cparams(*sem):
    return pltpu.CompilerParams(dimension_semantics=sem, vmem_limit_bytes=VMEM_LIMIT)


def _sigmoid(x):
    return 1.0 / (1.0 + jnp.exp(-x))


def _silu(x):
    return x * _sigmoid(x)


def _log_sigmoid(x):
    return jnp.minimum(x, 0.0) - jnp.log(1.0 + jnp.exp(-jnp.abs(x)))


def _softplus(x):
    return jnp.maximum(x, 0.0) + jnp.log(1.0 + jnp.exp(-jnp.abs(x)))


def _chunk_cumsum(x, c):
    pos = lax.broadcasted_iota(jnp.int32, x.shape, 0) % c
    shift = 1
    while shift < c:
        x = x + jnp.where(pos >= shift, pltpu.roll(x, shift, 0), 0.0)
        shift *= 2
    return x


CONV_PAD = 8
GDN_HW = GDN_HEADS * GDN_DK
GDN_QKV_BLOCK0 = (2 * GLA_HEADS * GLA_DK + 2 * GLA_WIDTH) // GDN_HW


def _in_proj_kernel(x_ref, wm_ref, wa_ref, cw_ref, main_ref, aux_ref, stage_ref, *, tm, tiles_per_seq):
    xb = x_ref[...].astype(BF16)

    @pl.when((pl.program_id(0) % tiles_per_seq) == 0)
    def _():
        stage_ref[:, 0:CONV_PAD, :] = jnp.zeros((3 * GDN_HEADS, CONV_PAD, GDN_DK), F32)

    for j in range(MAIN_W // GDN_HW):
        sl = slice(j * GDN_HW, (j + 1) * GDN_HW)
        r = jnp.dot(xb, wm_ref[:, sl], preferred_element_type=F32)
        b = j - GDN_QKV_BLOCK0
        if not 0 <= b < 3:
            main_ref[:, sl] = r.astype(BF16)
            continue
        for h in range(GDN_HEADS):
            st = stage_ref.at[b * GDN_HEADS + h]
            cols = slice(b * GDN_HW + h * GDN_DK, b * GDN_HW + (h + 1) * GDN_DK)
            st[CONV_PAD:, :] = r[:, h * GDN_DK:(h + 1) * GDN_DK]
            acc = jnp.zeros((tm, GDN_DK), F32)
            for i in range(CONV_W):
                lo = CONV_PAD - (CONV_W - 1) + i
                acc = acc + st[lo:lo + tm, :] * cw_ref[i:i + 1, cols]
            st[0:CONV_PAD, :] = st[tm:tm + CONV_PAD, :]
            yh = _silu(acc)
            if b < 2:
                yh = yh * lax.rsqrt(jnp.sum(yh * yh, axis=-1, keepdims=True) + RMS_EPS)
            if b == 0:
                yh = yh * (GDN_DK ** -0.5)
            main_ref[:, j * GDN_HW + h * GDN_DK:j * GDN_HW + (h + 1) * GDN_DK] = yh.astype(BF16)
    aux_ref[...] = jnp.dot(xb, wa_ref[...], preferred_element_type=F32)


def _in_proj(x2d, w_main, w_aux, conv_w, *, seq, tm=512):
    n = x2d.shape[0]
    return pl.pallas_call(
        functools.partial(_in_proj_kernel, tm=tm, tiles_per_seq=seq // tm),
        grid=(n // tm,),
        in_specs=[pl.BlockSpec((tm, D_MODEL), lambda i: (i, 0)),
                  pl.BlockSpec((D_MODEL, MAIN_W), lambda i: (0, 0)),
                  pl.BlockSpec((D_MODEL, AUX_W), lambda i: (0, 0)),
                  pl.BlockSpec((CONV_W, 3 * GDN_HW), lambda i: (0, 0))],
        out_specs=[pl.BlockSpec((tm, MAIN_W), lambda i: (i, 0)),
                   pl.BlockSpec((tm, AUX_W), lambda i: (i, 0))],
        out_shape=[jax.ShapeDtypeStruct((n, MAIN_W), BF16),
                   jax.ShapeDtypeStruct((n, AUX_W), F32)],
        scratch_shapes=[pltpu.VMEM((3 * GDN_HEADS, tm + CONV_PAD, GDN_DK), F32)],
        compiler_params=_cparams("arbitrary"),
        name="in_proj",
    )(x2d, w_main, w_aux, conv_w)


def _split_w_in(w_in):
    sizes = (GLA_HEADS * GLA_DK, GLA_HEADS * GLA_DK, GLA_WIDTH, GLA_WIDTH, GLA_GATE_RANK,
             GDN_HEADS * GDN_DK, GDN_HEADS * GDN_DK, GDN_WIDTH, GDN_WIDTH, GDN_HEADS, GDN_HEADS)
    offs = np.cumsum((0,) + sizes)
    seg = [w_in[:, offs[i]:offs[i + 1]] for i in range(len(sizes))]
    qa, ka, va, ra, lra, qb, kb, vb, zb, bb, ab = seg
    w_main = jnp.concatenate([qa, ka, va, ra, qb, kb, vb, zb], axis=1).astype(BF16)
    pad = jnp.zeros((w_in.shape[0], AUX_W - GLA_GATE_RANK - 2 * GDN_HEADS), w_in.dtype)
    w_aux = jnp.concatenate([lra, bb, ab, pad], axis=1).astype(BF16)
    return w_main, w_aux


_NT = (((1,), (1,)), ((), ()))
_TN = (((0,), (0,)), ((), ()))


def _gla_kernel(q_ref, k_ref, v_ref, r_ref, aux_ref, wgk_ref, bgk_ref, g_ref, o_ref, st_ref, *, ts, nseq):
    npairs = 2 * nseq
    c = CHUNK
    nsub = c // SUB

    @pl.when(pl.program_id(1) == 0)
    def _():
        st_ref[...] = jnp.zeros_like(st_ref)

    gk_all = [_log_sigmoid(jnp.dot(aux_ref[sq], wgk_ref[...], precision=HIGHEST, preferred_element_type=F32)
                           + bgk_ref[...]) * (1.0 / GLA_GATE_NORM) for sq in range(nseq)]

    row128 = lax.broadcasted_iota(jnp.int32, (c, 128), 0)
    lane128 = lax.broadcasted_iota(jnp.int32, (1, 128), 1)
    lane_masks = [(lane128 // GLA_DK) == hh for hh in range(2)]
    sub_valid = [row128 < SUB * (i + 1) for i in range(nsub)]
    ar = lax.broadcasted_iota(jnp.int32, (c, nsub * c), 0)
    ac = lax.broadcasted_iota(jnp.int32, (c, nsub * c), 1)
    amask = ((ac // c) == (ar // SUB)) & ((ac % c) <= ar)
    scale = GLA_DK ** -0.5
    gnorm = g_ref[...]

    nch = ts // c
    rows = [slice(ch * c, (ch + 1) * c) for ch in range(nch)]
    b_cum = [_chunk_cumsum(g, c) for g in gk_all]

    cp = [(ch, p) for ch in range(nch) for p in range(npairs)]
    qt, qd, kd, kstack, dlast = {}, {}, {}, {}, {}
    for ch, p in cp:
        sq, pp = divmod(p, 2)
        lanes = slice(128 * pp, 128 * (pp + 1))
        bp = b_cum[sq][rows[ch], lanes]
        qp = q_ref[sq, rows[ch], lanes].astype(F32) * scale
        kp = k_ref[sq, rows[ch], lanes].astype(F32)
        c_row = jnp.concatenate(
            [jnp.broadcast_to(bp[SUB * i:SUB * i + 1, :], (SUB, 128)) for i in range(nsub)], axis=0)
        qt[ch, p] = (qp * jnp.exp(bp - c_row)).astype(BF16)
        qd[ch, p] = qp * jnp.exp(bp)
        b_last = bp[c - 1:c, :]
        dlast[ch, p] = jnp.exp(b_last)
        kd[ch, p] = kp * jnp.exp(b_last - bp)
        kts = []
        for i in range(nsub):
            e = jnp.exp(jnp.where(sub_valid[i], bp[SUB * i:SUB * i + 1, :] - bp, 0.0))
            kts.append(jnp.where(sub_valid[i], kp * e, 0.0))
        kstack[ch, p] = jnp.concatenate(kts, axis=0)

    cph = [(ch, p, hh) for ch, p in cp for hh in range(2)]
    vh = {(ch, p, hh): v_ref[p // 2, rows[ch], 128 * (2 * (p % 2) + hh):128 * (2 * (p % 2) + hh + 1)]
          for ch, p, hh in cph}
    r_mat = {(ch, p, hh): lax.dot_general(qt[ch, p], jnp.where(lane_masks[hh], kstack[ch, p], 0.0).astype(BF16),
                                          _NT, preferred_element_type=F32) for ch, p, hh in cph}
    o_intra = {k: jnp.dot(jnp.where(amask, r_mat[k], 0.0).astype(BF16), jnp.concatenate([vh[k]] * nsub, axis=0),
                          preferred_element_type=F32) for k in cph}
    upd = {(ch, p, hh): lax.dot_general(vh[ch, p, hh], jnp.where(lane_masks[hh], kd[ch, p], 0.0).astype(BF16), _TN,
                                        preferred_element_type=F32) for ch, p, hh in cph}

    st = [st_ref[p] for p in range(npairs)]
    for ch in range(nch):
        st_b = [s.astype(BF16) for s in st]
        o_pair = [lax.dot_general(
            jnp.concatenate([jnp.where(lane_masks[hh], qd[ch, p], 0.0).astype(BF16) for hh in range(2)], axis=0),
            st_b[p], _NT, preferred_element_type=F32) for p in range(npairs)]
        o_inter = {(p, hh): o_pair[p][hh * c:(hh + 1) * c] for p in range(npairs) for hh in range(2)}
        st = [st[p] * dlast[ch, p] + upd[ch, p, 0] + upd[ch, p, 1] for p in range(npairs)]
        for p in range(npairs):
            for hh in range(2):
                sq, h = p // 2, 2 * (p % 2) + hh
                o = o_intra[ch, p, hh] + o_inter[p, hh]
                o = o * lax.rsqrt(jnp.mean(o * o, axis=-1, keepdims=True) + RMS_EPS) * gnorm
                gate = _silu(r_ref[sq, rows[ch], 128 * h:128 * (h + 1)].astype(F32))
                o_ref[sq, rows[ch], 128 * h:128 * (h + 1)] = (o * gate).astype(BF16)
    for p in range(npairs):
        st_ref[p] = st[p]


def _gla(main, aux, wgk_pad, b_gk, gla_norm_g, *, bsz, seq, ts=512, nseq=2):
    nt = seq // ts
    assert bsz % nseq == 0
    main4 = main.reshape(bsz // nseq, nseq, seq, MAIN_W)
    aux4 = aux.reshape(bsz // nseq, nseq, seq, AUX_W)
    tok = lambda width, col: pl.BlockSpec((None, nseq, ts, width), lambda b, t: (b, 0, t, col))
    out = pl.pallas_call(
        functools.partial(_gla_kernel, ts=ts, nseq=nseq),
        grid=(bsz // nseq, nt),
        in_specs=[tok(256, 0), tok(256, 1), tok(512, 1), tok(512, 2),
                  tok(AUX_W, 0),
                  pl.BlockSpec((AUX_W, 256), lambda b, t: (0, 0)),
                  pl.BlockSpec((1, 256), lambda b, t: (0, 0)),
                  pl.BlockSpec((1, 128), lambda b, t: (0, 0))],
        out_specs=tok(GLA_WIDTH, 0),
        out_shape=jax.ShapeDtypeStruct((bsz // nseq, nseq, seq, GLA_WIDTH), BF16),
        scratch_shapes=[pltpu.VMEM((2 * nseq, 128, 128), F32)],
        compiler_params=_cparams("parallel", "arbitrary"),
        name="gla",
    )(main4, main4, main4, main4, aux4, wgk_pad, b_gk, gla_norm_g)
    return out.reshape(bsz * seq, GLA_WIDTH)


def _gdn_kernel(q_ref, k_ref, v_ref, z_ref, aux_ref, alog_ref, dtb_ref, g_ref, o_ref, st_ref, *, ts, nseq):
    c = CHUNK
    nch = ts // c

    @pl.when(pl.program_id(1) == 0)
    def _():
        st_ref[...] = jnp.zeros_like(st_ref)

    beta_full, gam_full, gam_t = [], [], []
    for sq in range(nseq):
        aux = aux_ref[sq]
        beta_full.append(_sigmoid(aux))
        g_full = -jnp.exp(alog_ref[...]) * _softplus(aux + dtb_ref[...])
        gam_full.append(_chunk_cumsum(g_full, c))
        gam_t.append(gam_full[-1].T)
    gnorm = g_ref[...]

    row = lax.broadcasted_iota(jnp.int32, (c, 2 * c), 0)
    lane = lax.broadcasted_iota(jnp.int32, (c, 2 * c), 1)
    left = lane < c
    col = lane % c
    causal = col <= row
    strict = col < row
    same32 = (row // 32) == (col // 32)
    same16 = (row // 16) == (col // 16)
    lvl0 = same16 & strict
    lvl1 = same32 & jnp.logical_not(same16) & strict
    lvl2 = jnp.logical_not(same32) & strict

    def pack_rows(x2):
        return jnp.where(left, x2[0:c], x2[c:2 * c])

    def pack_cols(x1):
        return jnp.where(left, x1[0:c], x1[c:2 * c])

    def halves(x):
        return jnp.where(left, x, 0.0).astype(BF16), jnp.where(left, 0.0, x).astype(BF16)

    def mmp(x, y):
        return jnp.dot(x.astype(BF16), jnp.concatenate(halves(y), axis=0), preferred_element_type=F32)

    heads = range(nseq * GDN_HEADS)
    npair = nch // 2
    items = [(h, pr) for h in heads for pr in range(npair)]
    prow = [slice(2 * c * pr, 2 * c * (pr + 1)) for pr in range(npair)]

    qn, kn, beta, gam, gam_r, kb, rhs, rhs_b, qd, kd = ([None] * len(heads) for _ in range(10))
    for h in heads:
        sq, hh = divmod(h, GDN_HEADS)
        hl = slice(128 * hh, 128 * (hh + 1))
        kb[h] = k_ref[sq, :, hl]
        qn[h] = q_ref[sq, :, hl].astype(F32)
        kn[h] = kb[h].astype(F32)
        vh = v_ref[sq, :, hl].astype(F32)
        beta[h] = beta_full[sq][:, AUX_BB + hh:AUX_BB + hh + 1]
        gam[h] = gam_full[sq][:, AUX_AB + hh:AUX_AB + hh + 1]
        gam_r[h] = gam_t[sq][AUX_AB + hh:AUX_AB + hh + 1, :]
        egam = jnp.exp(gam[h])
        rhs[h] = jnp.concatenate([beta[h] * vh, beta[h] * egam * kn[h]], axis=1)
        rhs_b[h] = rhs[h].astype(BF16)
        qd[h] = (qn[h] * egam).astype(BF16)
        gl_rows = jnp.concatenate(
            [jnp.broadcast_to(gam[h][c * (j + 1) - 1:c * (j + 1), :], (c, 1)) for j in range(nch)], axis=0)
        kd[h] = (kn[h] * jnp.exp(gl_rows - gam[h])).astype(BF16)

    kk = [pack_rows(lax.dot_general(kb[h][prow[pr]], kb[h][prow[pr]], _NT, preferred_element_type=F32))
          for h, pr in items]
    qk = [pack_rows(lax.dot_general(qn[h][prow[pr]].astype(BF16), kb[h][prow[pr]], _NT,
                                    preferred_element_type=F32)) for h, pr in items]
    dec = [jnp.exp(jnp.where(causal, pack_cols(gam[h][prow[pr]]) - gam_r[h][:, prow[pr]], 0.0)) for h, pr in items]
    a = [pack_cols(beta[h][prow[pr]]) * kk[i] * dec[i] for i, (h, pr) in enumerate(items)]
    attn = [halves(jnp.where(causal, qk[i] * dec[i], 0.0)) for i in range(len(items))]

    x1 = [jnp.where(lvl0, -ai, 0.0) for ai in a]
    x2 = [mmp(x, x) for x in x1]
    y = [mmp(x, xx) for x, xx in zip(x1, x2)]
    p = [x + xx + yy for x, xx, yy in zip(x1, x2, y)]
    xk = x2
    for _ in range(2):
        xk = [mmp(x, x) for x in xk]
        y = [mmp(pp, x) for pp, x in zip(p, xk)]
        p = [pp + x + yy for pp, x, yy in zip(p, xk, y)]
    for lvl in (lvl1, lvl2):
        low = [jnp.where(lvl, ai, 0.0) for ai in a]
        y = [mmp(pp, lo) for pp, lo in zip(p, low)]
        zz = [lo + yy for lo, yy in zip(low, y)]
        y = [mmp(z, pp) for z, pp in zip(zz, p)]
        p = [pp - z - yy for pp, z, yy in zip(p, zz, y)]

    us = [[None] * nch for _ in heads]
    ws = [[None] * nch for _ in heads]
    for i, (h, pr) in enumerate(items):
        for half, p_half in enumerate(halves(p[i])):
            j = 2 * pr + half
            uw = rhs[h][c * j:c * (j + 1)] + jnp.dot(p_half, rhs_b[h][prow[pr]], preferred_element_type=F32)
            us[h][j] = uw[:, :GDN_DV]
            ws[h][j] = uw[:, GDN_DV:].astype(BF16)

    s = [st_ref[h] for h in heads]
    vprev = [None] * len(heads)
    for j in range(nch):
        rows = slice(c * j, c * (j + 1))
        sb = [s[h].astype(BF16) for h in heads]
        ws_qd = [jnp.dot(jnp.concatenate([ws[h][j], qd[h][rows]], axis=0), sb[h], preferred_element_type=F32)
                 for h in heads]
        vn16 = [(us[h][j] - ws_qd[h][0:c]).astype(BF16) for h in heads]
        o_inter = [ws_qd[h][c:2 * c] for h in heads]
        for h in heads:
            vpair = jnp.concatenate([vn16[h], jnp.zeros_like(vn16[h])] if j % 2 == 0 else [vprev[h], vn16[h]], axis=0)
            o = o_inter[h] + jnp.dot(attn[h * npair + j // 2][j % 2], vpair, preferred_element_type=F32)
            s[h] = jnp.exp(gam[h][c * (j + 1) - 1:c * (j + 1), :]) * s[h] + lax.dot_general(
                kd[h][rows], vn16[h], _TN, preferred_element_type=F32)
            o = o * lax.rsqrt(jnp.mean(o * o, axis=-1, keepdims=True) + RMS_EPS) * gnorm
            sq, hh = divmod(h, GDN_HEADS)
            gate = _silu(z_ref[sq, rows, 128 * hh:128 * (hh + 1)].astype(F32))
            o_ref[sq, rows, 128 * hh:128 * (hh + 1)] = (o * gate).astype(BF16)
        vprev = vn16
    for h in heads:
        st_ref[h] = s[h]


def _gdn(main, aux, alog_pad, dtb_pad, gdn_norm_g, *, bsz, seq, ts=512, nseq=2):
    nt = seq // ts
    hw = GDN_HEADS * GDN_DK
    assert bsz % nseq == 0
    main4 = main.reshape(bsz // nseq, nseq, seq, MAIN_W)
    aux4 = aux.reshape(bsz // nseq, nseq, seq, AUX_W)
    tok = lambda col: pl.BlockSpec((None, nseq, ts, hw), lambda b, t: (b, 0, t, col))
    out = pl.pallas_call(
        functools.partial(_gdn_kernel, ts=ts, nseq=nseq),
        grid=(bsz // nseq, nt),
        in_specs=[tok(3), tok(4), tok(5), tok(6),
                  pl.BlockSpec((None, nseq, ts, AUX_W), lambda b, t: (b, 0, t, 0)),
                  pl.BlockSpec((1, AUX_W), lambda b, t: (0, 0)),
                  pl.BlockSpec((1, AUX_W), lambda b, t: (0, 0)),
                  pl.BlockSpec((1, GDN_DV), lambda b, t: (0, 0))],
        out_specs=pl.BlockSpec((None, nseq, ts, GDN_WIDTH), lambda b, t: (b, 0, t, 0)),
        out_shape=jax.ShapeDtypeStruct((bsz // nseq, nseq, seq, GDN_WIDTH), BF16),
        scratch_shapes=[pltpu.VMEM((nseq * GDN_HEADS, GDN_DK, GDN_DV), F32)],
        compiler_params=_cparams("parallel", "arbitrary"),
        name="gdn",
    )(main4, main4, main4, main4, aux4, alog_pad, dtb_pad, gdn_norm_g)
    return out.reshape(bsz * seq, GDN_WIDTH)


def _pad_aux_row(v, offset):
    return jnp.zeros((1, AUX_W), F32).at[0, offset:offset + v.shape[0]].set(v.astype(F32))


ROUTE_W = 128
R_E1, R_E2, R_RANK1, R_RANK2, R_G1, R_G2 = range(6)
ROUTER_GROUP_COL = N_EXPERTS
POST_MIX_SPLIT = 4


def _layer_norm(h, g, b):
    mu = jnp.mean(h, axis=-1, keepdims=True)
    hc = h - mu
    var = jnp.mean(hc * hc, axis=-1, keepdims=True)
    return hc * lax.rsqrt(var + LN_EPS) * g + b


def _pack_bf16_pairs(x):
    w = x.shape[1] // 2
    lo = lax.bitcast_convert_type(x[:, :w].astype(BF16).astype(F32), jnp.uint32)
    hi = lax.bitcast_convert_type(x[:, w:].astype(BF16).astype(F32), jnp.uint32)
    return (lo >> 16) | (hi & jnp.uint32(0xFFFF0000))


def _unpack_bf16_pairs(p):
    lo = lax.bitcast_convert_type(p << 16, F32)
    hi = lax.bitcast_convert_type(p & jnp.uint32(0xFFFF0000), F32)
    return lo, hi


HALF = D_MODEL // 2
TOK_ROWS = HALF // 128


def _store_token_rows(ref, packed):
    m = packed.shape[0]
    for c in range(TOK_ROWS):
        ref[pl.ds(c, m, stride=TOK_ROWS), :] = packed[:, 128 * c:128 * (c + 1)]


def _load_token_rows(ref, m):
    return jnp.concatenate([ref[pl.ds(c, m, stride=TOK_ROWS), :] for c in range(TOK_ROWS)], axis=1)


def _post_mix_kernel(oa_ref, ob_ref, x_ref, wo_ref, g_ref, b_ref, wr_ref, br_ref,
                     x1p_ref, route_ref, gates_ref, cnt_ref, carry_ref, *, tm):
    @pl.when(pl.program_id(0) == 0)
    def _():
        carry_ref[...] = jnp.zeros_like(carry_ref)

    m = tm // POST_MIX_SPLIT
    groups = range(POST_MIX_SPLIT)
    rs = [slice(h * m, (h + 1) * m) for h in groups]
    y = [jnp.dot(oa_ref[rs[h], :], wo_ref[0:GLA_WIDTH, :], preferred_element_type=F32)
         + jnp.dot(ob_ref[rs[h], :], wo_ref[GLA_WIDTH:, :], preferred_element_type=F32) for h in groups]
    x1 = [_layer_norm(ALPHA_DN * x_ref[rs[h], :] + y[h], g_ref[...], b_ref[...]) for h in groups]
    for h in groups:
        _store_token_rows(x1p_ref.at[pl.ds(h * m * TOK_ROWS, m * TOK_ROWS), :], _pack_bf16_pairs(x1[h]))

    xh = [x.astype(BF16) for x in x1]
    xl = [(x - xb.astype(F32)).astype(BF16) for x, xb in zip(x1, xh)]
    hw = [lax.dot_general(wr_ref[...], xb, _NT, preferred_element_type=F32) for xb in xh]
    lw = [lax.dot_general(wr_ref[0:ROUTE_W, :], xb, _NT, preferred_element_type=F32) for xb in xl]
    logits = [a[:ROUTE_W] + a[ROUTE_W:] + b + br_ref[...] for a, b in zip(hw, lw)]
    sub = lax.broadcasted_iota(jnp.int32, (ROUTE_W, m), 0)
    big = jnp.int32(1 << 20)
    neg = jnp.float32(-jnp.inf)

    def first_argmax(vals):
        mx = jnp.max(vals, axis=0, keepdims=True)
        idx = jnp.min(jnp.where(vals == mx, sub, big), axis=0, keepdims=True)
        return mx, idx

    is_group = (sub >= ROUTER_GROUP_COL) & (sub < ROUTER_GROUP_COL + N_GROUPS)
    gl = [jnp.where(is_group, lg, neg) for lg in logits]
    gtop = [first_argmax(v) for v in gl]
    p_group = [1.0 / jnp.sum(jnp.exp(v - t[0]), axis=0, keepdims=True) for v, t in zip(gl, gtop)]
    el = [jnp.where((sub // EXPERTS_PER_GROUP) == (t[1] - ROUTER_GROUP_COL), lg, neg) for lg, t in zip(logits, gtop)]
    top1 = [first_argmax(v) for v in el]
    top2 = [first_argmax(jnp.where(sub == t[1], neg, v)) for v, t in zip(el, top1)]
    ex = [jnp.exp(b[0] - a[0]) for a, b in zip(top1, top2)]
    w1 = [1.0 / (1.0 + e) for e in ex]
    gate1 = [p * w for p, w in zip(p_group, w1)]
    gate2 = [p * (e * w) for p, e, w in zip(p_group, ex, w1)]

    oh1 = [sub == t[1] for t in top1]
    oh2 = [sub == t[1] for t in top2]
    ri = lax.broadcasted_iota(jnp.int32, (m, m), 0)
    ci = lax.broadcasted_iota(jnp.int32, (m, m), 1)
    ustrict = (ri < ci).astype(BF16)
    oh1f = [o.astype(F32) for o in oh1]
    oh2f = [o.astype(F32) for o in oh2]
    c1 = [jnp.dot(o.astype(BF16), ustrict, preferred_element_type=F32) for o in oh1f]
    c2 = [jnp.dot(o.astype(BF16), ustrict, preferred_element_type=F32) for o in oh2f]
    tot1 = [jnp.sum(o, axis=1, keepdims=True) for o in oh1f]
    tot2 = [jnp.sum(o, axis=1, keepdims=True) for o in oh2f]
    carry = carry_ref[...]
    eye8 = jnp.eye(8, ROUTE_W, dtype=F32)
    zero = jnp.zeros((1, m), F32)
    for h in groups:
        rank1 = jnp.sum(jnp.where(oh1[h], c1[h] + carry, 0.0), axis=0, keepdims=True)
        rank2 = jnp.sum(jnp.where(oh2[h], c2[h] + (carry + tot1[h]), 0.0), axis=0, keepdims=True)
        carry = carry + tot1[h] + tot2[h]
        route_t = jnp.concatenate([top1[h][1].astype(F32), top2[h][1].astype(F32), rank1, rank2,
                                   gate1[h], gate2[h], zero, zero], axis=0)
        route_ref[:, rs[h]] = route_t
        gates_ref[rs[h], :] = lax.dot_general(route_t, eye8, _TN, precision=HIGHEST, preferred_element_type=F32)
    carry_ref[...] = carry
    cnt_ref[...] = carry


def _post_mix(o_a, o_b, x2d, w_out_b, ln_g, ln_b, w_router, b_router, *, tm=1024):
    n = x2d.shape[0]
    return pl.pallas_call(
        functools.partial(_post_mix_kernel, tm=tm),
        grid=(n // tm,),
        in_specs=[pl.BlockSpec((tm, GLA_WIDTH), lambda i: (i, 0)),
                  pl.BlockSpec((tm, GDN_WIDTH), lambda i: (i, 0)),
                  pl.BlockSpec((tm, D_MODEL), lambda i: (i, 0)),
                  pl.BlockSpec((GLA_WIDTH + GDN_WIDTH, D_MODEL), lambda i: (0, 0)),
                  pl.BlockSpec((1, D_MODEL), lambda i: (0, 0)),
                  pl.BlockSpec((1, D_MODEL), lambda i: (0, 0)),
                  pl.BlockSpec((2 * ROUTE_W, D_MODEL), lambda i: (0, 0)),
                  pl.BlockSpec((ROUTE_W, 1), lambda i: (0, 0))],
        out_specs=[pl.BlockSpec((tm * TOK_ROWS, 128), lambda i: (i, 0)),
                   pl.BlockSpec((8, tm), lambda i: (0, i)),
                   pl.BlockSpec((tm, ROUTE_W), lambda i: (i, 0)),
                   pl.BlockSpec((ROUTE_W, 1), lambda i: (0, 0))],
        out_shape=[jax.ShapeDtypeStruct((n * TOK_ROWS, 128), jnp.uint32),
                   jax.ShapeDtypeStruct((8, n), F32),
                   jax.ShapeDtypeStruct((n, ROUTE_W), F32),
                   jax.ShapeDtypeStruct((ROUTE_W, 1), F32)],
        scratch_shapes=[pltpu.VMEM((ROUTE_W, 1), F32)],
        compiler_params=_cparams("arbitrary"),
        name="post_mix",
    )(o_a, o_b, x2d, w_out_b, ln_g, ln_b, w_router, b_router)


EXPERT_BLOCK = 256
BLOCKS_PER_STEP = 4

ROW_UNROLL = 8


def _plan_kernel(route_ref, cnt_ref, dest_ref, *, tm, tb):
    counts = cnt_ref[...].astype(jnp.int32)
    padded = jnp.broadcast_to(((counts + (tb - 1)) & jnp.int32(-tb)).astype(F32), (ROUTE_W, ROUTE_W))
    pstart = (_chunk_cumsum(padded, ROUTE_W) - padded)[:, 0:1]
    route = route_ref[...]
    sub = lax.broadcasted_iota(jnp.int32, (ROUTE_W, tm), 0).astype(F32)
    rows = []
    for ce, cr in ((R_E1, R_RANK1), (R_E2, R_RANK2)):
        off = jnp.sum(jnp.where(sub == route[ce:ce + 1, :], pstart, 0.0), axis=0, keepdims=True)
        rows.append(off + route[cr:cr + 1, :])
    rows.extend([jnp.zeros((1, tm), F32)] * (8 - len(rows)))
    dest_ref[...] = jnp.concatenate(rows, axis=0).astype(jnp.int32) * TOK_ROWS


def _plan(route, counts, *, tb, tm=2048):
    n = route.shape[1]
    tm = min(tm, n)
    return pl.pallas_call(
        functools.partial(_plan_kernel, tm=tm, tb=tb),
        grid=(n // tm,),
        in_specs=[pl.BlockSpec((8, tm), lambda i: (0, i)),
                  pl.BlockSpec((ROUTE_W, 1), lambda i: (0, 0))],
        out_specs=pl.BlockSpec((8, tm), lambda i: (0, i)),
        out_shape=jax.ShapeDtypeStruct((8, n), jnp.int32),
        compiler_params=_cparams("parallel"),
        name="plan",
    )(route, counts)


def _dispatch_kernel(d1_ref, d2_ref, last_blk_ref, nvalid_ref, xp_ref, xs_ref, zbuf, sem, zsem, *, tm, tb, nb):
    base = pl.program_id(0) * tm
    rows = tb * TOK_ROWS

    @pl.when(pl.program_id(0) == 0)
    def _():
        zbuf[...] = jnp.zeros_like(zbuf)

        def zero_copy(blk):
            return pltpu.make_async_copy(zbuf, xs_ref.at[pl.ds(pl.multiple_of(blk * rows, rows), rows), :], zsem)

        def for_each_block(fn):
            def per_expert(e, carry):
                @pl.when(last_blk_ref[e] >= 0)
                def _():
                    fn(zero_copy(last_blk_ref[e]))
                return carry

            def per_tail(blk, carry):
                fn(zero_copy(blk))
                return carry
            lax.fori_loop(0, N_EXPERTS, per_expert, 0)
            lax.fori_loop(nvalid_ref[0], nb, per_tail, 0)

        for_each_block(lambda cp: cp.start())
        for_each_block(lambda cp: cp.wait())

    def body(j, carry):
        for u in range(ROW_UNROLL):
            r = j * ROW_UNROLL + u
            for k, d_ref in enumerate((d1_ref, d2_ref)):
                dst = pl.multiple_of(d_ref[base + r], TOK_ROWS)
                pltpu.make_async_copy(xp_ref.at[pl.ds(r * TOK_ROWS, TOK_ROWS), :],
                                      xs_ref.at[pl.ds(dst, TOK_ROWS), :], sem).start(priority=k)
        return carry
    lax.fori_loop(0, tm // ROW_UNROLL, body, 0)
    for _ in range(TOP_K):
        pltpu.make_async_copy(xp_ref, xs_ref.at[pl.ds(0, tm * TOK_ROWS), :], sem).wait()


def _dispatch(dest1, dest2, last_blk, nvalid, x1p, *, nb, tb=EXPERT_BLOCK, tm=1024):
    n = x1p.shape[0] // TOK_ROWS
    grid_spec = pltpu.PrefetchScalarGridSpec(
        num_scalar_prefetch=4,
        grid=(n // tm,),
        in_specs=[pl.BlockSpec((tm * TOK_ROWS, 128), lambda i, a, b, lb, nv: (i, 0))],
        out_specs=pl.BlockSpec(memory_space=pl.ANY),
        scratch_shapes=[pltpu.VMEM((tb * TOK_ROWS, 128), jnp.uint32),
                        pltpu.SemaphoreType.DMA(()),
                        pltpu.SemaphoreType.DMA(())],
    )
    return pl.pallas_call(
        functools.partial(_dispatch_kernel, tm=tm, tb=tb, nb=nb),
        grid_spec=grid_spec,
        out_shape=jax.ShapeDtypeStruct((nb * tb * TOK_ROWS, 128), jnp.uint32),
        compiler_params=_cparams("arbitrary"),
        name="dispatch",
    )(dest1, dest2, last_blk, nvalid, x1p)


def _expert_kernel(blk_e_ref, first_ref, next_e_ref, nvalid_ref, xs_ref, wg_hbm, wu_hbm, wd_hbm, y_ref,
                   wgs, wus, wds, wgb, wub, wdb, sem, *, tb):
    rows = tb * TOK_ROWS

    def weight_copies(e):
        return [pltpu.make_async_copy(src.at[e], dst, sem.at[j])
                for j, (src, dst) in enumerate(((wg_hbm, wgs), (wu_hbm, wus), (wd_hbm, wds)))]

    @pl.when((pl.program_id(0) == 0) & (nvalid_ref[0] > 0))
    def _():
        for cp in weight_copies(blk_e_ref[0]):
            cp.start()

    def switch_expert(i):
        @pl.when(first_ref[i] == 1)
        def _():
            for cp in weight_copies(blk_e_ref[i]):
                cp.wait()
            wgb[...] = wgs[...].astype(BF16)
            wub[...] = wus[...].astype(BF16)
            wdb[...] = wds[...].astype(BF16)

            @pl.when(next_e_ref[i] >= 0)
            def _():
                for cp in weight_copies(next_e_ref[i]):
                    cp.start()

    def mlp(sub, nblk):
        m = nblk * tb
        span = pl.ds(sub * rows, nblk * rows)
        lo, hi = _unpack_bf16_pairs(_load_token_rows(xs_ref.at[span, :], m))
        xb = jnp.concatenate([lo.astype(BF16), hi.astype(BF16)], axis=1)
        hg = jnp.dot(xb, wgb[...], preferred_element_type=F32)
        hu = jnp.dot(xb, wub[...], preferred_element_type=F32)
        hid = (_silu(hg) * hu).astype(BF16)
        _store_token_rows(y_ref.at[span, :], _pack_bf16_pairs(jnp.dot(hid, wdb[...], preferred_element_type=F32)))

    nvalid = nvalid_ref[0]
    for sub in range(0, BLOCKS_PER_STEP, 2):
        i0 = pl.program_id(0) * BLOCKS_PER_STEP + sub
        i1 = i0 + 1
        paired = (i1 < nvalid) & (first_ref[i1] == 0)

        @pl.when(i0 < nvalid)
        def _():
            switch_expert(i0)

        @pl.when(paired)
        def _():
            mlp(sub, 2)

        @pl.when(jnp.logical_not(paired))
        def _():
            @pl.when(i0 < nvalid)
            def _():
                mlp(sub, 1)

            @pl.when(i1 < nvalid)
            def _():
                switch_expert(i1)
                mlp(sub + 1, 1)

        for s_, i in ((sub, i0), (sub + 1, i1)):
            @pl.when(i >= nvalid)
            def _():
                y_ref[pl.ds(s_ * rows, rows), :] = jnp.zeros((rows, 128), jnp.uint32)


def _experts(blk_e, first, next_e, nvalid, xs, w_gate, w_up, w_down, *, tb=EXPERT_BLOCK):
    nb = blk_e.shape[0]
    assert nb % BLOCKS_PER_STEP == 0
    step_rows = BLOCKS_PER_STEP * tb * TOK_ROWS
    last_step = lambda nv: jnp.maximum((nv[0] - 1) // BLOCKS_PER_STEP, 0)
    grid_spec = pltpu.PrefetchScalarGridSpec(
        num_scalar_prefetch=4,
        grid=(nb // BLOCKS_PER_STEP,),
        in_specs=[pl.BlockSpec((step_rows, 128), lambda s, be, fr, ne, nv: (jnp.minimum(s, last_step(nv)), 0)),
                  pl.BlockSpec(memory_space=pl.ANY),
                  pl.BlockSpec(memory_space=pl.ANY),
                  pl.BlockSpec(memory_space=pl.ANY)],
        out_specs=pl.BlockSpec((step_rows, 128), lambda s, be, fr, ne, nv: (s, 0)),
        scratch_shapes=[pltpu.VMEM((D_MODEL, D_EXPERT), F32),
                        pltpu.VMEM((D_MODEL, D_EXPERT), F32),
                        pltpu.VMEM((D_EXPERT, D_MODEL), F32),
                        pltpu.VMEM((D_MODEL, D_EXPERT), BF16),
                        pltpu.VMEM((D_MODEL, D_EXPERT), BF16),
                        pltpu.VMEM((D_EXPERT, D_MODEL), BF16),
                        pltpu.SemaphoreType.DMA((3,))],
    )
    return pl.pallas_call(
        functools.partial(_expert_kernel, tb=tb),
        grid_spec=grid_spec,
        out_shape=jax.ShapeDtypeStruct((nb * tb * TOK_ROWS, 128), jnp.uint32),
        compiler_params=_cparams("arbitrary"),
        name="experts",
    )(blk_e, first, next_e, nvalid, xs, w_gate, w_up, w_down)


def _combine_kernel(d1_ref, d2_ref, y_hbm, x1_ref, route_ref, g_ref, b_ref, o_ref, ybuf, sem, *, tt):
    i = pl.program_id(0)
    nsteps = pl.num_programs(0)
    slot = i % 2

    def issue(tile, slot_):
        def body(j, carry):
            for u in range(ROW_UNROLL):
                r = j * ROW_UNROLL + u
                for k, d_ref in enumerate((d1_ref, d2_ref)):
                    src = pl.multiple_of(d_ref[tile * tt + r], TOK_ROWS)
                    pltpu.make_async_copy(y_hbm.at[pl.ds(src, TOK_ROWS), :],
                                          ybuf.at[slot_, k, pl.ds(r * TOK_ROWS, TOK_ROWS), :],
                                          sem.at[slot_]).start(priority=k)
            return carry
        lax.fori_loop(0, tt // ROW_UNROLL, body, 0)

    @pl.when(i == 0)
    def _():
        issue(0, 0)

    @pl.when(i + 1 < nsteps)
    def _():
        issue(i + 1, 1 - slot)

    for k in range(TOP_K):
        pltpu.make_async_copy(y_hbm.at[pl.ds(0, tt * TOK_ROWS), :], ybuf.at[slot, k], sem.at[slot]).wait()
    route = route_ref[...]
    g1 = route[:, R_G1:R_G1 + 1]
    g2 = route[:, R_G2:R_G2 + 1]
    lo1, hi1 = _unpack_bf16_pairs(_load_token_rows(ybuf.at[slot, 0], tt))
    lo2, hi2 = _unpack_bf16_pairs(_load_token_rows(ybuf.at[slot, 1], tt))
    x_lo, x_hi = _unpack_bf16_pairs(_load_token_rows(x1_ref, tt))
    h_lo = ALPHA_DN * x_lo + (g1 * lo1 + g2 * lo2)
    h_hi = ALPHA_DN * x_hi + (g1 * hi1 + g2 * hi2)
    mu = (jnp.sum(h_lo, axis=-1, keepdims=True) + jnp.sum(h_hi, axis=-1, keepdims=True)) * (1.0 / D_MODEL)
    c_lo = h_lo - mu
    c_hi = h_hi - mu
    var = (jnp.sum(c_lo * c_lo, axis=-1, keepdims=True) + jnp.sum(c_hi * c_hi, axis=-1, keepdims=True)) * (1.0 / D_MODEL)
    inv = lax.rsqrt(var + LN_EPS)
    o_ref[:, 0:HALF] = c_lo * inv * g_ref[:, 0:HALF] + b_ref[:, 0:HALF]
    o_ref[:, HALF:] = c_hi * inv * g_ref[:, HALF:] + b_ref[:, HALF:]


def _combine(dest1, dest2, ybuf, x1p, route, ln_g, ln_b, *, tt=512):
    n = x1p.shape[0] // TOK_ROWS
    grid_spec = pltpu.PrefetchScalarGridSpec(
        num_scalar_prefetch=2,
        grid=(n // tt,),
        in_specs=[pl.BlockSpec(memory_space=pl.ANY),
                  pl.BlockSpec((tt * TOK_ROWS, 128), lambda i, a, b: (i, 0)),
                  pl.BlockSpec((tt, ROUTE_W), lambda i, a, b: (i, 0)),
                  pl.BlockSpec((1, D_MODEL), lambda i, a, b: (0, 0)),
                  pl.BlockSpec((1, D_MODEL), lambda i, a, b: (0, 0))],
        out_specs=pl.BlockSpec((tt, D_MODEL), lambda i, a, b: (i, 0)),
        scratch_shapes=[pltpu.VMEM((2, TOP_K, tt * TOK_ROWS, 128), jnp.uint32),
                        pltpu.SemaphoreType.DMA((2,))],
    )
    return pl.pallas_call(
        functools.partial(_combine_kernel, tt=tt),
        grid_spec=grid_spec,
        out_shape=jax.ShapeDtypeStruct((n, D_MODEL), F32),
        compiler_params=_cparams("arbitrary"),
        name="combine",
    )(dest1, dest2, ybuf, x1p, route, ln_g, ln_b)


def _block_plan(counts_row, n, tb):
    counts = counts_row[:N_EXPERTS, 0].astype(jnp.int32)
    pends = jnp.cumsum((counts + tb - 1) // tb * tb)
    nb = (n * TOP_K) // tb + N_EXPERTS
    nvalid = (pends[-1] // tb).astype(jnp.int32)
    blk = jnp.arange(nb, dtype=jnp.int32)
    blk_e = jnp.sum((pends[None, :] <= (blk * tb)[:, None]).astype(jnp.int32), axis=1)
    blk_e = jnp.minimum(blk_e, N_EXPERTS - 1)
    blk_e = jnp.where(blk < nvalid, blk_e, blk_e[jnp.maximum(nvalid - 1, 0)])
    first = jnp.concatenate([jnp.ones((1,), jnp.int32), (blk_e[1:] != blk_e[:-1]).astype(jnp.int32)])
    ids = jnp.arange(N_EXPERTS, dtype=jnp.int32)
    later = lax.cummin(jnp.where(counts > 0, ids, N_EXPERTS), axis=0, reverse=True)
    nxt = jnp.concatenate([later[1:], jnp.full((1,), N_EXPERTS, jnp.int32)])
    nxt = jnp.where(nxt < N_EXPERTS, nxt, -1)
    last_blk = jnp.where(counts > 0, pends // tb - 1, -1).astype(jnp.int32)
    return blk_e, first, nxt[blk_e], nvalid.reshape(1), last_blk


def kernel(x, w_in, w_gk_up, b_gk, conv_w, a_log, dt_bias, gla_norm_g, gdn_norm_g, w_out, ln1_g, ln1_b, w_router_group, b_router_group, w_router_expert, b_router_expert, w_gate, w_up, w_down, ln2_g, ln2_b):
    bsz, seq, d = x.shape
    n = bsz * seq
    x2d = x.reshape(n, d)
    w_main, w_aux = _split_w_in(w_in[0])
    main, aux = _in_proj(x2d, w_main, w_aux, conv_w[0].astype(F32), seq=seq)
    wgk = jnp.zeros((AUX_W, 256), F32).at[:GLA_GATE_RANK].set(w_gk_up[0])
    o_a = _gla(main, aux, wgk, b_gk[0][None], gla_norm_g[0][None], bsz=bsz, seq=seq)
    o_b = _gdn(main, aux, _pad_aux_row(a_log[0], AUX_AB), _pad_aux_row(dt_bias[0], AUX_AB),
               gdn_norm_g[0][None], bsz=bsz, seq=seq)
    rpad = jnp.zeros((ROUTE_W - N_EXPERTS - N_GROUPS, d), F32)
    w_router = jnp.concatenate([w_router_expert[0].T, w_router_group[0].T, rpad], axis=0)
    w_router_hi = w_router.astype(BF16)
    w_router = jnp.concatenate([w_router_hi, (w_router - w_router_hi.astype(F32)).astype(BF16)], axis=0)
    b_router = jnp.concatenate([b_router_expert[0], b_router_group[0], rpad[:, 0]])[:, None]
    x1p, route, gates, counts = _post_mix(o_a, o_b, x2d, w_out[0].astype(BF16), ln1_g[0][None], ln1_b[0][None],
                                          w_router, b_router)
    dest = _plan(route, counts, tb=EXPERT_BLOCK)
    blk_e, first, next_e, nvalid, last_blk = _block_plan(counts, n, EXPERT_BLOCK)
    xs = _dispatch(dest[0], dest[1], last_blk, nvalid, x1p, nb=blk_e.shape[0])
    ybuf = _experts(blk_e, first, next_e, nvalid, xs, w_gate[0], w_up[0], w_down[0])
    out = _combine(dest[0], dest[1], ybuf, x1p, gates, ln2_g[0][None], ln2_b[0][None])
    return out.reshape(bsz, seq, d)
```

```python
import functools

import jax
import jax.numpy as jnp
import numpy as np
from jax import lax
from jax.experimental import pallas as pl
from jax.experimental.pallas import tpu as pltpu

F32 = jnp.float32
BF16 = jnp.bfloat16
HIGHEST = lax.Precision.HIGHEST

D_MODEL = 1024
DEPTH = 1
GLA_HEADS = 4
GLA_DK = 64
GLA_DV = 128
GLA_WIDTH = GLA_HEADS * GLA_DV
GLA_GATE_RANK = 16
GLA_GATE_NORM = 16.0
GDN_HEADS = 4
GDN_DK = 128
GDN_DV = 128
GDN_WIDTH = GDN_HEADS * GDN_DV
CONV_W = 4
CHUNK = 64
N_GROUPS = 8
EXPERTS_PER_GROUP = 8
N_EXPERTS = N_GROUPS * EXPERTS_PER_GROUP
TOP_K = 2
D_EXPERT = 512
LN_EPS = 1e-5
RMS_EPS = 1e-6
ALPHA_DN = (2.0 * DEPTH) ** 0.25

MAIN_W = 2 * GLA_HEADS * GLA_DK + 2 * GLA_WIDTH + 4 * GDN_WIDTH
AUX_W = 128
AUX_LRA = 0
AUX_BB = GLA_GATE_RANK
AUX_AB = GLA_GATE_RANK + GDN_HEADS
SUB = 16
VMEM_LIMIT = 56 * 1024 * 1024


def _cparams(*sem):
    return pltpu.CompilerParams(dimension_semantics=sem, vmem_limit_bytes=VMEM_LIMIT)


def _sigmoid(x):
    return 1.0 / (1.0 + jnp.exp(-x))


def _silu(x):
    return x * _sigmoid(x)


def _log_sigmoid(x):
    return jnp.minimum(x, 0.0) - jnp.log(1.0 + jnp.exp(-jnp.abs(x)))


def _softplus(x):
    return jnp.maximum(x, 0.0) + jnp.log(1.0 + jnp.exp(-jnp.abs(x)))


def _chunk_cumsum(x, c):
    pos = lax.broadcasted_iota(jnp.int32, x.shape, 0) % c
    shift = 1
    while shift < c:
        x = x + jnp.where(pos >= shift, pltpu.roll(x, shift, 0), 0.0)
        shift *= 2
    return x


CONV_PAD = 8
GDN_HW = GDN_HEADS * GDN_DK
GDN_QKV_BLOCK0 = (2 * GLA_HEADS * GLA_DK + 2 * GLA_WIDTH) // GDN_HW


def _in_proj_kernel(x_ref, wm_ref, wa_ref, cw_ref, main_ref, aux_ref, stage_ref, *, tm, tiles_per_seq):
    xb = x_ref[...].astype(BF16)

    @pl.when((pl.program_id(0) % tiles_per_seq) == 0)
    def _():
        stage_ref[:, 0:CONV_PAD, :] = jnp.zeros((3 * GDN_HEADS, CONV_PAD, GDN_DK), F32)

    for j in range(MAIN_W // GDN_HW):
        sl = slice(j * GDN_HW, (j + 1) * GDN_HW)
        r = jnp.dot(xb, wm_ref[:, sl], preferred_element_type=F32)
        b = j - GDN_QKV_BLOCK0
        if not 0 <= b < 3:
            main_ref[:, sl] = r.astype(BF16)
            continue
        for h in range(GDN_HEADS):
            st = stage_ref.at[b * GDN_HEADS + h]
            cols = slice(b * GDN_HW + h * GDN_DK, b * GDN_HW + (h + 1) * GDN_DK)
            st[CONV_PAD:, :] = r[:, h * GDN_DK:(h + 1) * GDN_DK]
            acc = jnp.zeros((tm, GDN_DK), F32)
            for i in range(CONV_W):
                lo = CONV_PAD - (CONV_W - 1) + i
                acc = acc + st[lo:lo + tm, :] * cw_ref[i:i + 1, cols]
            st[0:CONV_PAD, :] = st[tm:tm + CONV_PAD, :]
            yh = _silu(acc)
            if b < 2:
                yh = yh * lax.rsqrt(jnp.sum(yh * yh, axis=-1, keepdims=True) + RMS_EPS)
            if b == 0:
                yh = yh * (GDN_DK ** -0.5)
            main_ref[:, j * GDN_HW + h * GDN_DK:j * GDN_HW + (h + 1) * GDN_DK] = yh.astype(BF16)
    aux_ref[...] = jnp.dot(xb, wa_ref[...], preferred_element_type=F32)


def _in_proj(x2d, w_main, w_aux, conv_w, *, seq, tm=512):
    n = x2d.shape[0]
    return pl.pallas_call(
        functools.partial(_in_proj_kernel, tm=tm, tiles_per_seq=seq // tm),
        grid=(n // tm,),
        in_specs=[pl.BlockSpec((tm, D_MODEL), lambda i: (i, 0)),
                  pl.BlockSpec((D_MODEL, MAIN_W), lambda i: (0, 0)),
                  pl.BlockSpec((D_MODEL, AUX_W), lambda i: (0, 0)),
                  pl.BlockSpec((CONV_W, 3 * GDN_HW), lambda i: (0, 0))],
        out_specs=[pl.BlockSpec((tm, MAIN_W), lambda i: (i, 0)),
                   pl.BlockSpec((tm, AUX_W), lambda i: (i, 0))],
        out_shape=[jax.ShapeDtypeStruct((n, MAIN_W), BF16),
                   jax.ShapeDtypeStruct((n, AUX_W), F32)],
        scratch_shapes=[pltpu.VMEM((3 * GDN_HEADS, tm + CONV_PAD, GDN_DK), F32)],
        compiler_params=_cparams("arbitrary"),
        name="in_proj",
    )(x2d, w_main, w_aux, conv_w)


def _split_w_in(w_in):
    sizes = (GLA_HEADS * GLA_DK, GLA_HEADS * GLA_DK, GLA_WIDTH, GLA_WIDTH, GLA_GATE_RANK,
             GDN_HEADS * GDN_DK, GDN_HEADS * GDN_DK, GDN_WIDTH, GDN_WIDTH, GDN_HEADS, GDN_HEADS)
    offs = np.cumsum((0,) + sizes)
    seg = [w_in[:, offs[i]:offs[i + 1]] for i in range(len(sizes))]
    qa, ka, va, ra, lra, qb, kb, vb, zb, bb, ab = seg
    w_main = jnp.concatenate([qa, ka, va, ra, qb, kb, vb, zb], axis=1).astype(BF16)
    pad = jnp.zeros((w_in.shape[0], AUX_W - GLA_GATE_RANK - 2 * GDN_HEADS), w_in.dtype)
    w_aux = jnp.concatenate([lra, bb, ab, pad], axis=1).astype(BF16)
    return w_main, w_aux


_NT = (((1,), (1,)), ((), ()))
_TN = (((0,), (0,)), ((), ()))


def _gla_kernel(q_ref, k_ref, v_ref, r_ref, aux_ref, wgk_ref, bgk_ref, g_ref, o_ref, st_ref, *, ts, nseq):
    npairs = 2 * nseq
    c = CHUNK
    nsub = c // SUB

    @pl.when(pl.program_id(1) == 0)
    def _():
        st_ref[...] = jnp.zeros_like(st_ref)

    gk_all = [_log_sigmoid(jnp.dot(aux_ref[sq], wgk_ref[...], precision=HIGHEST, preferred_element_type=F32)
                           + bgk_ref[...]) * (1.0 / GLA_GATE_NORM) for sq in range(nseq)]

    row128 = lax.broadcasted_iota(jnp.int32, (c, 128), 0)
    lane128 = lax.broadcasted_iota(jnp.int32, (1, 128), 1)
    lane_masks = [(lane128 // GLA_DK) == hh for hh in range(2)]
    sub_valid = [row128 < SUB * (i + 1) for i in range(nsub)]
    ar = lax.broadcasted_iota(jnp.int32, (c, nsub * c), 0)
    ac = lax.broadcasted_iota(jnp.int32, (c, nsub * c), 1)
    amask = ((ac // c) == (ar // SUB)) & ((ac % c) <= ar)
    scale = GLA_DK ** -0.5
    gnorm = g_ref[...]

    nch = ts // c
    rows = [slice(ch * c, (ch + 1) * c) for ch in range(nch)]
    b_cum = [_chunk_cumsum(g, c) for g in gk_all]

    cp = [(ch, p) for ch in range(nch) for p in range(npairs)]
    qt, qd, kd, kstack, dlast = {}, {}, {}, {}, {}
    for ch, p in cp:
        sq, pp = divmod(p, 2)
        lanes = slice(128 * pp, 128 * (pp + 1))
        bp = b_cum[sq][rows[ch], lanes]
        qp = q_ref[sq, rows[ch], lanes].astype(F32) * scale
        kp = k_ref[sq, rows[ch], lanes].astype(F32)
        c_row = jnp.concatenate(
            [jnp.broadcast_to(bp[SUB * i:SUB * i + 1, :], (SUB, 128)) for i in range(nsub)], axis=0)
        qt[ch, p] = (qp * jnp.exp(bp - c_row)).astype(BF16)
        qd[ch, p] = qp * jnp.exp(bp)
        b_last = bp[c - 1:c, :]
        dlast[ch, p] = jnp.exp(b_last)
        kd[ch, p] = kp * jnp.exp(b_last - bp)
        kts = []
        for i in range(nsub):
            e = jnp.exp(jnp.where(sub_valid[i], bp[SUB * i:SUB * i + 1, :] - bp, 0.0))
            kts.append(jnp.where(sub_valid[i], kp * e, 0.0))
        kstack[ch, p] = jnp.concatenate(kts, axis=0)

    cph = [(ch, p, hh) for ch, p in cp for hh in range(2)]
    vh = {(ch, p, hh): v_ref[p // 2, rows[ch], 128 * (2 * (p % 2) + hh):128 * (2 * (p % 2) + hh + 1)]
          for ch, p, hh in cph}
    r_mat = {(ch, p, hh): lax.dot_general(qt[ch, p], jnp.where(lane_masks[hh], kstack[ch, p], 0.0).astype(BF16),
                                          _NT, preferred_element_type=F32) for ch, p, hh in cph}
    o_intra = {k: jnp.dot(jnp.where(amask, r_mat[k], 0.0).astype(BF16), jnp.concatenate([vh[k]] * nsub, axis=0),
                          preferred_element_type=F32) for k in cph}
    upd = {(ch, p, hh): lax.dot_general(vh[ch, p, hh], jnp.where(lane_masks[hh], kd[ch, p], 0.0).astype(BF16), _TN,
                                        preferred_element_type=F32) for ch, p, hh in cph}

    st = [st_ref[p] for p in range(npairs)]
    for ch in range(nch):
        st_b = [s.astype(BF16) for s in st]
        o_pair = [lax.dot_general(
            jnp.concatenate([jnp.where(lane_masks[hh], qd[ch, p], 0.0).astype(BF16) for hh in range(2)], axis=0),
            st_b[p], _NT, preferred_element_type=F32) for p in range(npairs)]
        o_inter = {(p, hh): o_pair[p][hh * c:(hh + 1) * c] for p in range(npairs) for hh in range(2)}
        st = [st[p] * dlast[ch, p] + upd[ch, p, 0] + upd[ch, p, 1] for p in range(npairs)]
        for p in range(npairs):
            for hh in range(2):
                sq, h = p // 2, 2 * (p % 2) + hh
                o = o_intra[ch, p, hh] + o_inter[p, hh]
                o = o * lax.rsqrt(jnp.mean(o * o, axis=-1, keepdims=True) + RMS_EPS) * gnorm
                gate = _silu(r_ref[sq, rows[ch], 128 * h:128 * (h + 1)].astype(F32))
                o_ref[sq, rows[ch], 128 * h:128 * (h + 1)] = (o * gate).astype(BF16)
    for p in range(npairs):
        st_ref[p] = st[p]


def _gla(main, aux, wgk_pad, b_gk, gla_norm_g, *, bsz, seq, ts=512, nseq=2):
    nt = seq // ts
    assert bsz % nseq == 0
    main4 = main.reshape(bsz // nseq, nseq, seq, MAIN_W)
    aux4 = aux.reshape(bsz // nseq, nseq, seq, AUX_W)
    tok = lambda width, col: pl.BlockSpec((None, nseq, ts, width), lambda b, t: (b, 0, t, col))
    out = pl.pallas_call(
        functools.partial(_gla_kernel, ts=ts, nseq=nseq),
        grid=(bsz // nseq, nt),
        in_specs=[tok(256, 0), tok(256, 1), tok(512, 1), tok(512, 2),
                  tok(AUX_W, 0),
                  pl.BlockSpec((AUX_W, 256), lambda b, t: (0, 0)),
                  pl.BlockSpec((1, 256), lambda b, t: (0, 0)),
                  pl.BlockSpec((1, 128), lambda b, t: (0, 0))],
        out_specs=tok(GLA_WIDTH, 0),
        out_shape=jax.ShapeDtypeStruct((bsz // nseq, nseq, seq, GLA_WIDTH), BF16),
        scratch_shapes=[pltpu.VMEM((2 * nseq, 128, 128), F32)],
        compiler_params=_cparams("parallel", "arbitrary"),
        name="gla",
    )(main4, main4, main4, main4, aux4, wgk_pad, b_gk, gla_norm_g)
    return out.reshape(bsz * seq, GLA_WIDTH)


def _gdn_kernel(q_ref, k_ref, v_ref, z_ref, aux_ref, alog_ref, dtb_ref, g_ref, o_ref, st_ref, *, ts, nseq):
    c = CHUNK
    nch = ts // c

    @pl.when(pl.program_id(1) == 0)
    def _():
        st_ref[...] = jnp.zeros_like(st_ref)

    beta_full, gam_full, gam_t = [], [], []
    for sq in range(nseq):
        aux = aux_ref[sq]
        beta_full.append(_sigmoid(aux))
        g_full = -jnp.exp(alog_ref[...]) * _softplus(aux + dtb_ref[...])
        gam_full.append(_chunk_cumsum(g_full, c))
        gam_t.append(gam_full[-1].T)
    gnorm = g_ref[...]

    row = lax.broadcasted_iota(jnp.int32, (c, 2 * c), 0)
    lane = lax.broadcasted_iota(jnp.int32, (c, 2 * c), 1)
    left = lane < c
    col = lane % c
    causal = col <= row
    strict = col < row
    same32 = (row // 32) == (col // 32)
    same16 = (row // 16) == (col // 16)
    lvl0 = same16 & strict
    lvl1 = same32 & jnp.logical_not(same16) & strict
    lvl2 = jnp.logical_not(same32) & strict

    def pack_rows(x2):
        return jnp.where(left, x2[0:c], x2[c:2 * c])

    def pack_cols(x1):
        return jnp.where(left, x1[0:c], x1[c:2 * c])

    def halves(x):
        return jnp.where(left, x, 0.0).astype(BF16), jnp.where(left, 0.0, x).astype(BF16)

    def mmp(x, y):
        return jnp.dot(x.astype(BF16), jnp.concatenate(halves(y), axis=0), preferred_element_type=F32)

    heads = range(nseq * GDN_HEADS)
    npair = nch // 2
    items = [(h, pr) for h in heads for pr in range(npair)]
    prow = [slice(2 * c * pr, 2 * c * (pr + 1)) for pr in range(npair)]

    qn, kn, beta, gam, gam_r, kb, rhs, rhs_b, qd, kd = ([None] * len(heads) for _ in range(10))
    for h in heads:
        sq, hh = divmod(h, GDN_HEADS)
        hl = slice(128 * hh, 128 * (hh + 1))
        kb[h] = k_ref[sq, :, hl]
        qn[h] = q_ref[sq, :, hl].astype(F32)
        kn[h] = kb[h].astype(F32)
        vh = v_ref[sq, :, hl].astype(F32)
        beta[h] = beta_full[sq][:, AUX_BB + hh:AUX_BB + hh + 1]
        gam[h] = gam_full[sq][:, AUX_AB + hh:AUX_AB + hh + 1]
        gam_r[h] = gam_t[sq][AUX_AB + hh:AUX_AB + hh + 1, :]
        egam = jnp.exp(gam[h])
        rhs[h] = jnp.concatenate([beta[h] * vh, beta[h] * egam * kn[h]], axis=1)
        rhs_b[h] = rhs[h].astype(BF16)
        qd[h] = (qn[h] * egam).astype(BF16)
        gl_rows = jnp.concatenate(
            [jnp.broadcast_to(gam[h][c * (j + 1) - 1:c * (j + 1), :], (c, 1)) for j in range(nch)], axis=0)
        kd[h] = (kn[h] * jnp.exp(gl_rows - gam[h])).astype(BF16)

    kk = [pack_rows(lax.dot_general(kb[h][prow[pr]], kb[h][prow[pr]], _NT, preferred_element_type=F32))
          for h, pr in items]
    qk = [pack_rows(lax.dot_general(qn[h][prow[pr]].astype(BF16), kb[h][prow[pr]], _NT,
                                    preferred_element_type=F32)) for h, pr in items]
    dec = [jnp.exp(jnp.where(causal, pack_cols(gam[h][prow[pr]]) - gam_r[h][:, prow[pr]], 0.0)) for h, pr in items]
    a = [pack_cols(beta[h][prow[pr]]) * kk[i] * dec[i] for i, (h, pr) in enumerate(items)]
    attn = [halves(jnp.where(causal, qk[i] * dec[i], 0.0)) for i in range(len(items))]

    x1 = [jnp.where(lvl0, -ai, 0.0) for ai in a]
    x2 = [mmp(x, x) for x in x1]
    y = [mmp(x, xx) for x, xx in zip(x1, x2)]
    p = [x + xx + yy for x, xx, yy in zip(x1, x2, y)]
    xk = x2
    for _ in range(2):
        xk = [mmp(x, x) for x in xk]
        y = [mmp(pp, x) for pp, x in zip(p, xk)]
        p = [pp + x + yy for pp, x, yy in zip(p, xk, y)]
    for lvl in (lvl1, lvl2):
        low = [jnp.where(lvl, ai, 0.0) for ai in a]
        y = [mmp(pp, lo) for pp, lo in zip(p, low)]
        zz = [lo + yy for lo, yy in zip(low, y)]
        y = [mmp(z, pp) for z, pp in zip(zz, p)]
        p = [pp - z - yy for pp, z, yy in zip(p, zz, y)]

    us = [[None] * nch for _ in heads]
    ws = [[None] * nch for _ in heads]
    for i, (h, pr) in enumerate(items):
        for half, p_half in enumerate(halves(p[i])):
            j = 2 * pr + half
            uw = rhs[h][c * j:c * (j + 1)] + jnp.dot(p_half, rhs_b[h][prow[pr]], preferred_element_type=F32)
            us[h][j] = uw[:, :GDN_DV]
            ws[h][j] = uw[:, GDN_DV:].astype(BF16)

    s = [st_ref[h] for h in heads]
    vprev = [None] * len(heads)
    for j in range(nch):
        rows = slice(c * j, c * (j + 1))
        sb = [s[h].astype(BF16) for h in heads]
        ws_qd = [jnp.dot(jnp.concatenate([ws[h][j], qd[h][rows]], axis=0), sb[h], preferred_element_type=F32)
                 for h in heads]
        vn16 = [(us[h][j] - ws_qd[h][0:c]).astype(BF16) for h in heads]
        o_inter = [ws_qd[h][c:2 * c] for h in heads]
        for h in heads:
            vpair = jnp.concatenate([vn16[h], jnp.zeros_like(vn16[h])] if j % 2 == 0 else [vprev[h], vn16[h]], axis=0)
            o = o_inter[h] + jnp.dot(attn[h * npair + j // 2][j % 2], vpair, preferred_element_type=F32)
            s[h] = jnp.exp(gam[h][c * (j + 1) - 1:c * (j + 1), :]) * s[h] + lax.dot_general(
                kd[h][rows], vn16[h], _TN, preferred_element_type=F32)
            o = o * lax.rsqrt(jnp.mean(o * o, axis=-1, keepdims=True) + RMS_EPS) * gnorm
            sq, hh = divmod(h, GDN_HEADS)
            gate = _silu(z_ref[sq, rows, 128 * hh:128 * (hh + 1)].astype(F32))
            o_ref[sq, rows, 128 * hh:128 * (hh + 1)] = (o * gate).astype(BF16)
        vprev = vn16
    for h in heads:
        st_ref[h] = s[h]


def _gdn(main, aux, alog_pad, dtb_pad, gdn_norm_g, *, bsz, seq, ts=512, nseq=2):
    nt = seq // ts
    hw = GDN_HEADS * GDN_DK
    assert bsz % nseq == 0
    main4 = main.reshape(bsz // nseq, nseq, seq, MAIN_W)
    aux4 = aux.reshape(bsz // nseq, nseq, seq, AUX_W)
    tok = lambda col: pl.BlockSpec((None, nseq, ts, hw), lambda b, t: (b, 0, t, col))
    out = pl.pallas_call(
        functools.partial(_gdn_kernel, ts=ts, nseq=nseq),
        grid=(bsz // nseq, nt),
        in_specs=[tok(3), tok(4), tok(5), tok(6),
                  pl.BlockSpec((None, nseq, ts, AUX_W), lambda b, t: (b, 0, t, 0)),
                  pl.BlockSpec((1, AUX_W), lambda b, t: (0, 0)),
                  pl.BlockSpec((1, AUX_W), lambda b, t: (0, 0)),
                  pl.BlockSpec((1, GDN_DV), lambda b, t: (0, 0))],
        out_specs=pl.BlockSpec((None, nseq, ts, GDN_WIDTH), lambda b, t: (b, 0, t, 0)),
        out_shape=jax.ShapeDtypeStruct((bsz // nseq, nseq, seq, GDN_WIDTH), BF16),
        scratch_shapes=[pltpu.VMEM((nseq * GDN_HEADS, GDN_DK, GDN_DV), F32)],
        compiler_params=_cparams("parallel", "arbitrary"),
        name="gdn",
    )(main4, main4, main4, main4, aux4, alog_pad, dtb_pad, gdn_norm_g)
    return out.reshape(bsz * seq, GDN_WIDTH)


def _pad_aux_row(v, offset):
    return jnp.zeros((1, AUX_W), F32).at[0, offset:offset + v.shape[0]].set(v.astype(F32))


ROUTE_W = 128
R_E1, R_E2, R_RANK1, R_RANK2, R_G1, R_G2 = range(6)
ROUTER_GROUP_COL = N_EXPERTS
POST_MIX_SPLIT = 4


def _layer_norm(h, g, b):
    mu = jnp.mean(h, axis=-1, keepdims=True)
    hc = h - mu
    var = jnp.mean(hc * hc, axis=-1, keepdims=True)
    return hc * lax.rsqrt(var + LN_EPS) * g + b


def _pack_bf16_pairs(x):
    w = x.shape[1] // 2
    lo = lax.bitcast_convert_type(x[:, :w].astype(BF16).astype(F32), jnp.uint32)
    hi = lax.bitcast_convert_type(x[:, w:].astype(BF16).astype(F32), jnp.uint32)
    return (lo >> 16) | (hi & jnp.uint32(0xFFFF0000))


def _unpack_bf16_pairs(p):
    lo = lax.bitcast_convert_type(p << 16, F32)
    hi = lax.bitcast_convert_type(p & jnp.uint32(0xFFFF0000), F32)
    return lo, hi


HALF = D_MODEL // 2
LANES = 128
TOK_ROWS = HALF // LANES


def _store_token_rows(ref, packed):
    m = packed.shape[0]
    for c in range(TOK_ROWS):
        ref[pl.ds(c, m, stride=TOK_ROWS), :] = packed[:, LANES * c:LANES * (c + 1)]


def _load_token_rows(ref, m):
    return jnp.concatenate([ref[pl.ds(c, m, stride=TOK_ROWS), :] for c in range(TOK_ROWS)], axis=1)


def _post_mix_kernel(oa_ref, ob_ref, x_ref, wo_ref, g_ref, b_ref, wr_ref, br_ref,
                     x1p_ref, route_ref, gates_ref, cnt_ref, carry_ref, *, tm):
    @pl.when(pl.program_id(0) == 0)
    def _():
        carry_ref[...] = jnp.zeros_like(carry_ref)

    m = tm // POST_MIX_SPLIT
    groups = range(POST_MIX_SPLIT)
    rs = [slice(h * m, (h + 1) * m) for h in groups]
    y = [jnp.dot(oa_ref[rs[h], :], wo_ref[0:GLA_WIDTH, :], preferred_element_type=F32)
         + jnp.dot(ob_ref[rs[h], :], wo_ref[GLA_WIDTH:, :], preferred_element_type=F32) for h in groups]
    x1 = [_layer_norm(ALPHA_DN * x_ref[rs[h], :] + y[h], g_ref[...], b_ref[...]) for h in groups]
    for h in groups:
        _store_token_rows(x1p_ref.at[pl.ds(h * m * TOK_ROWS, m * TOK_ROWS), :], _pack_bf16_pairs(x1[h]))

    xh = [x.astype(BF16) for x in x1]
    xl = [(x - xb.astype(F32)).astype(BF16) for x, xb in zip(x1, xh)]
    hw = [lax.dot_general(wr_ref[...], xb, _NT, preferred_element_type=F32) for xb in xh]
    lw = [lax.dot_general(wr_ref[0:ROUTE_W, :], xb, _NT, preferred_element_type=F32) for xb in xl]
    logits = [a[:ROUTE_W] + a[ROUTE_W:] + b + br_ref[...] for a, b in zip(hw, lw)]
    sub = lax.broadcasted_iota(jnp.int32, (ROUTE_W, m), 0)
    big = jnp.int32(1 << 20)
    neg = jnp.float32(-jnp.inf)

    def first_argmax(vals):
        mx = jnp.max(vals, axis=0, keepdims=True)
        idx = jnp.min(jnp.where(vals == mx, sub, big), axis=0, keepdims=True)
        return mx, idx

    is_group = (sub >= ROUTER_GROUP_COL) & (sub < ROUTER_GROUP_COL + N_GROUPS)
    gl = [jnp.where(is_group, lg, neg) for lg in logits]
    gtop = [first_argmax(v) for v in gl]
    p_group = [1.0 / jnp.sum(jnp.exp(v - t[0]), axis=0, keepdims=True) for v, t in zip(gl, gtop)]
    el = [jnp.where((sub // EXPERTS_PER_GROUP) == (t[1] - ROUTER_GROUP_COL), lg, neg) for lg, t in zip(logits, gtop)]
    top1 = [first_argmax(v) for v in el]
    top2 = [first_argmax(jnp.where(sub == t[1], neg, v)) for v, t in zip(el, top1)]
    ex = [jnp.exp(b[0] - a[0]) for a, b in zip(top1, top2)]
    w1 = [1.0 / (1.0 + e) for e in ex]
    gate1 = [p * w for p, w in zip(p_group, w1)]
    gate2 = [p * (e * w) for p, e, w in zip(p_group, ex, w1)]

    oh1 = [sub == t[1] for t in top1]
    oh2 = [sub == t[1] for t in top2]
    ri = lax.broadcasted_iota(jnp.int32, (m, m), 0)
    ci = lax.broadcasted_iota(jnp.int32, (m, m), 1)
    ustrict = (ri < ci).astype(BF16)
    oh1f = [o.astype(F32) for o in oh1]
    oh2f = [o.astype(F32) for o in oh2]
    c1 = [jnp.dot(o.astype(BF16), ustrict, preferred_element_type=F32) for o in oh1f]
    c2 = [jnp.dot(o.astype(BF16), ustrict, preferred_element_type=F32) for o in oh2f]
    tot1 = [jnp.sum(o, axis=1, keepdims=True) for o in oh1f]
    tot2 = [jnp.sum(o, axis=1, keepdims=True) for o in oh2f]
    carry = carry_ref[...]
    eye8 = jnp.eye(8, ROUTE_W, dtype=F32)
    zero = jnp.zeros((1, m), F32)
    for h in groups:
        rank1 = jnp.sum(jnp.where(oh1[h], c1[h] + carry, 0.0), axis=0, keepdims=True)
        rank2 = jnp.sum(jnp.where(oh2[h], c2[h] + (carry + tot1[h]), 0.0), axis=0, keepdims=True)
        carry = carry + tot1[h] + tot2[h]
        route_t = jnp.concatenate([top1[h][1].astype(F32), top2[h][1].astype(F32), rank1, rank2,
                                   gate1[h], gate2[h], zero, zero], axis=0)
        route_ref[:, rs[h]] = route_t
        gates_ref[rs[h], :] = lax.dot_general(route_t, eye8, _TN, precision=HIGHEST, preferred_element_type=F32)
    carry_ref[...] = carry
    cnt_ref[...] = carry


def _post_mix(o_a, o_b, x2d, w_out_b, ln_g, ln_b, w_router, b_router, *, tm=1024):
    n = x2d.shape[0]
    return pl.pallas_call(
        functools.partial(_post_mix_kernel, tm=tm),
        grid=(n // tm,),
        in_specs=[pl.BlockSpec((tm, GLA_WIDTH), lambda i: (i, 0)),
                  pl.BlockSpec((tm, GDN_WIDTH), lambda i: (i, 0)),
                  pl.BlockSpec((tm, D_MODEL), lambda i: (i, 0)),
                  pl.BlockSpec((GLA_WIDTH + GDN_WIDTH, D_MODEL), lambda i: (0, 0)),
                  pl.BlockSpec((1, D_MODEL), lambda i: (0, 0)),
                  pl.BlockSpec((1, D_MODEL), lambda i: (0, 0)),
                  pl.BlockSpec((2 * ROUTE_W, D_MODEL), lambda i: (0, 0)),
                  pl.BlockSpec((ROUTE_W, 1), lambda i: (0, 0))],
        out_specs=[pl.BlockSpec((tm * TOK_ROWS, LANES), lambda i: (i, 0)),
                   pl.BlockSpec((8, tm), lambda i: (0, i)),
                   pl.BlockSpec((tm, ROUTE_W), lambda i: (i, 0)),
                   pl.BlockSpec((ROUTE_W, 1), lambda i: (0, 0))],
        out_shape=[jax.ShapeDtypeStruct((n * TOK_ROWS, LANES), jnp.uint32),
                   jax.ShapeDtypeStruct((8, n), F32),
                   jax.ShapeDtypeStruct((n, ROUTE_W), F32),
                   jax.ShapeDtypeStruct((ROUTE_W, 1), F32)],
        scratch_shapes=[pltpu.VMEM((ROUTE_W, 1), F32)],
        compiler_params=_cparams("arbitrary"),
        name="post_mix",
    )(o_a, o_b, x2d, w_out_b, ln_g, ln_b, w_router, b_router)


EXPERT_BLOCK = 256
BLOCKS_PER_STEP = 4

ROW_UNROLL = 8


def _plan_kernel(route_ref, cnt_ref, dest_ref, *, tm, tb):
    counts = cnt_ref[...].astype(jnp.int32)
    padded = jnp.broadcast_to(((counts + (tb - 1)) & jnp.int32(-tb)).astype(F32), (ROUTE_W, ROUTE_W))
    pstart = (_chunk_cumsum(padded, ROUTE_W) - padded)[:, 0:1]
    route = route_ref[...]
    sub = lax.broadcasted_iota(jnp.int32, (ROUTE_W, tm), 0).astype(F32)
    rows = []
    for ce, cr in ((R_E1, R_RANK1), (R_E2, R_RANK2)):
        off = jnp.sum(jnp.where(sub == route[ce:ce + 1, :], pstart, 0.0), axis=0, keepdims=True)
        rows.append(off + route[cr:cr + 1, :])
    rows.extend([jnp.zeros((1, tm), F32)] * (8 - len(rows)))
    dest_ref[...] = jnp.concatenate(rows, axis=0).astype(jnp.int32) * TOK_ROWS


def _plan(route, counts, *, tb, tm=2048):
    n = route.shape[1]
    tm = min(tm, n)
    return pl.pallas_call(
        functools.partial(_plan_kernel, tm=tm, tb=tb),
        grid=(n // tm,),
        in_specs=[pl.BlockSpec((8, tm), lambda i: (0, i)),
                  pl.BlockSpec((ROUTE_W, 1), lambda i: (0, 0))],
        out_specs=pl.BlockSpec((8, tm), lambda i: (0, i)),
        out_shape=jax.ShapeDtypeStruct((8, n), jnp.int32),
        compiler_params=_cparams("parallel"),
        name="plan",
    )(route, counts)


def _dispatch_kernel(d1_ref, d2_ref, last_blk_ref, nvalid_ref, xp_ref, xs_ref, zbuf, sem, zsem, *, tm, tb, nb):
    base = pl.program_id(0) * tm
    rows = tb * TOK_ROWS

    @pl.when(pl.program_id(0) == 0)
    def _():
        zbuf[...] = jnp.zeros_like(zbuf)

        def zero_copy(blk):
            return pltpu.make_async_copy(zbuf, xs_ref.at[pl.ds(pl.multiple_of(blk * rows, rows), rows), :], zsem)

        def for_each_block(fn):
            def per_expert(e, carry):
                @pl.when(last_blk_ref[e] >= 0)
                def _():
                    fn(zero_copy(last_blk_ref[e]))
                return carry

            def per_tail(blk, carry):
                fn(zero_copy(blk))
                return carry
            lax.fori_loop(0, N_EXPERTS, per_expert, 0)
            lax.fori_loop(nvalid_ref[0], nb, per_tail, 0)

        for_each_block(lambda cp: cp.start())
        for_each_block(lambda cp: cp.wait())

    def body(j, carry):
        for u in range(ROW_UNROLL):
            r = j * ROW_UNROLL + u
            for k, d_ref in enumerate((d1_ref, d2_ref)):
                dst = pl.multiple_of(d_ref[base + r], TOK_ROWS)
                pltpu.make_async_copy(xp_ref.at[pl.ds(r * TOK_ROWS, TOK_ROWS), :],
                                      xs_ref.at[pl.ds(dst, TOK_ROWS), :], sem).start(priority=k)
        return carry
    lax.fori_loop(0, tm // ROW_UNROLL, body, 0)
    for _ in range(TOP_K):
        pltpu.make_async_copy(xp_ref, xs_ref.at[pl.ds(0, tm * TOK_ROWS), :], sem).wait()


def _dispatch(dest1, dest2, last_blk, nvalid, x1p, *, nb, tb=EXPERT_BLOCK, tm=1024):
    n = x1p.shape[0] // TOK_ROWS
    grid_spec = pltpu.PrefetchScalarGridSpec(
        num_scalar_prefetch=4,
        grid=(n // tm,),
        in_specs=[pl.BlockSpec((tm * TOK_ROWS, LANES), lambda i, a, b, lb, nv: (i, 0))],
        out_specs=pl.BlockSpec(memory_space=pl.ANY),
        scratch_shapes=[pltpu.VMEM((tb * TOK_ROWS, LANES), jnp.uint32),
                        pltpu.SemaphoreType.DMA(()),
                        pltpu.SemaphoreType.DMA(())],
    )
    return pl.pallas_call(
        functools.partial(_dispatch_kernel, tm=tm, tb=tb, nb=nb),
        grid_spec=grid_spec,
        out_shape=jax.ShapeDtypeStruct((nb * tb * TOK_ROWS, LANES), jnp.uint32),
        compiler_params=_cparams("arbitrary"),
        name="dispatch",
    )(dest1, dest2, last_blk, nvalid, x1p)


def _expert_kernel(blk_e_ref, first_ref, next_e_ref, nvalid_ref, xs_ref, wg_hbm, wu_hbm, wd_hbm, y_ref,
                   wgs, wus, wds, wgb, wub, wdb, sem, *, tb):
    rows = tb * TOK_ROWS

    def weight_copies(e):
        return [pltpu.make_async_copy(src.at[e], dst, sem.at[j])
                for j, (src, dst) in enumerate(((wg_hbm, wgs), (wu_hbm, wus), (wd_hbm, wds)))]

    @pl.when((pl.program_id(0) == 0) & (nvalid_ref[0] > 0))
    def _():
        for cp in weight_copies(blk_e_ref[0]):
            cp.start()

    def switch_expert(i):
        @pl.when(first_ref[i] == 1)
        def _():
            for cp in weight_copies(blk_e_ref[i]):
                cp.wait()
            wgb[...] = wgs[...].astype(BF16)
            wub[...] = wus[...].astype(BF16)
            wdb[...] = wds[...].astype(BF16)

            @pl.when(next_e_ref[i] >= 0)
            def _():
                for cp in weight_copies(next_e_ref[i]):
                    cp.start()

    def mlp(sub, nblk):
        m = nblk * tb
        span = pl.ds(sub * rows, nblk * rows)
        lo, hi = _unpack_bf16_pairs(_load_token_rows(xs_ref.at[span, :], m))
        xb = jnp.concatenate([lo.astype(BF16), hi.astype(BF16)], axis=1)
        hg = jnp.dot(xb, wgb[...], preferred_element_type=F32)
        hu = jnp.dot(xb, wub[...], preferred_element_type=F32)
        hid = (_silu(hg) * hu).astype(BF16)
        _store_token_rows(y_ref.at[span, :], _pack_bf16_pairs(jnp.dot(hid, wdb[...], preferred_element_type=F32)))

    nvalid = nvalid_ref[0]
    for sub in range(0, BLOCKS_PER_STEP, 2):
        i0 = pl.program_id(0) * BLOCKS_PER_STEP + sub
        i1 = i0 + 1
        paired = (i1 < nvalid) & (first_ref[i1] == 0)

        @pl.when(i0 < nvalid)
        def _():
            switch_expert(i0)

        @pl.when(paired)
        def _():
            mlp(sub, 2)

        @pl.when(jnp.logical_not(paired))
        def _():
            @pl.when(i0 < nvalid)
            def _():
                mlp(sub, 1)

            @pl.when(i1 < nvalid)
            def _():
                switch_expert(i1)
                mlp(sub + 1, 1)

        for s_, i in ((sub, i0), (sub + 1, i1)):
            @pl.when(i >= nvalid)
            def _():
                y_ref[pl.ds(s_ * rows, rows), :] = jnp.zeros((rows, LANES), jnp.uint32)


def _experts(blk_e, first, next_e, nvalid, xs, w_gate, w_up, w_down, *, tb=EXPERT_BLOCK):
    nb = blk_e.shape[0]
    assert nb % BLOCKS_PER_STEP == 0
    step_rows = BLOCKS_PER_STEP * tb * TOK_ROWS
    last_step = lambda nv: jnp.maximum((nv[0] - 1) // BLOCKS_PER_STEP, 0)
    grid_spec = pltpu.PrefetchScalarGridSpec(
        num_scalar_prefetch=4,
        grid=(nb // BLOCKS_PER_STEP,),
        in_specs=[pl.BlockSpec((step_rows, LANES), lambda s, be, fr, ne, nv: (jnp.minimum(s, last_step(nv)), 0)),
                  pl.BlockSpec(memory_space=pl.ANY),
                  pl.BlockSpec(memory_space=pl.ANY),
                  pl.BlockSpec(memory_space=pl.ANY)],
        out_specs=pl.BlockSpec((step_rows, LANES), lambda s, be, fr, ne, nv: (s, 0)),
        scratch_shapes=[pltpu.VMEM((D_MODEL, D_EXPERT), F32),
                        pltpu.VMEM((D_MODEL, D_EXPERT), F32),
                        pltpu.VMEM((D_EXPERT, D_MODEL), F32),
                        pltpu.VMEM((D_MODEL, D_EXPERT), BF16),
                        pltpu.VMEM((D_MODEL, D_EXPERT), BF16),
                        pltpu.VMEM((D_EXPERT, D_MODEL), BF16),
                        pltpu.SemaphoreType.DMA((3,))],
    )
    return pl.pallas_call(
        functools.partial(_expert_kernel, tb=tb),
        grid_spec=grid_spec,
        out_shape=jax.ShapeDtypeStruct((nb * tb * TOK_ROWS, LANES), jnp.uint32),
        compiler_params=_cparams("arbitrary"),
        name="experts",
    )(blk_e, first, next_e, nvalid, xs, w_gate, w_up, w_down)


def _combine_kernel(d1_ref, d2_ref, y_hbm, x1_ref, route_ref, g_ref, b_ref, o_ref, ybuf, sem, *, tt):
    i = pl.program_id(0)
    nsteps = pl.num_programs(0)
    slot = i % 2

    def issue(tile, slot_):
        def body(j, carry):
            for u in range(ROW_UNROLL):
                r = j * ROW_UNROLL + u
                for k, d_ref in enumerate((d1_ref, d2_ref)):
                    src = pl.multiple_of(d_ref[tile * tt + r], TOK_ROWS)
                    pltpu.make_async_copy(y_hbm.at[pl.ds(src, TOK_ROWS), :],
                                          ybuf.at[slot_, k, pl.ds(r * TOK_ROWS, TOK_ROWS), :],
                                          sem.at[slot_]).start(priority=k)
            return carry
        lax.fori_loop(0, tt // ROW_UNROLL, body, 0)

    @pl.when(i == 0)
    def _():
        issue(0, 0)

    @pl.when(i + 1 < nsteps)
    def _():
        issue(i + 1, 1 - slot)

    for k in range(TOP_K):
        pltpu.make_async_copy(y_hbm.at[pl.ds(0, tt * TOK_ROWS), :], ybuf.at[slot, k], sem.at[slot]).wait()
    route = route_ref[...]
    g1 = route[:, R_G1:R_G1 + 1]
    g2 = route[:, R_G2:R_G2 + 1]
    lo1, hi1 = _unpack_bf16_pairs(_load_token_rows(ybuf.at[slot, 0], tt))
    lo2, hi2 = _unpack_bf16_pairs(_load_token_rows(ybuf.at[slot, 1], tt))
    x_lo, x_hi = _unpack_bf16_pairs(_load_token_rows(x1_ref, tt))
    h_lo = ALPHA_DN * x_lo + (g1 * lo1 + g2 * lo2)
    h_hi = ALPHA_DN * x_hi + (g1 * hi1 + g2 * hi2)
    mu = (jnp.sum(h_lo, axis=-1, keepdims=True) + jnp.sum(h_hi, axis=-1, keepdims=True)) * (1.0 / D_MODEL)
    c_lo = h_lo - mu
    c_hi = h_hi - mu
    var = (jnp.sum(c_lo * c_lo, axis=-1, keepdims=True) + jnp.sum(c_hi * c_hi, axis=-1, keepdims=True)) * (1.0 / D_MODEL)
    inv = lax.rsqrt(var + LN_EPS)
    o_ref[:, 0:HALF] = c_lo * inv * g_ref[:, 0:HALF] + b_ref[:, 0:HALF]
    o_ref[:, HALF:] = c_hi * inv * g_ref[:, HALF:] + b_ref[:, HALF:]


def _combine(dest1, dest2, ybuf, x1p, route, ln_g, ln_b, *, tt=512):
    n = x1p.shape[0] // TOK_ROWS
    grid_spec = pltpu.PrefetchScalarGridSpec(
        num_scalar_prefetch=2,
        grid=(n // tt,),
        in_specs=[pl.BlockSpec(memory_space=pl.ANY),
                  pl.BlockSpec((tt * TOK_ROWS, LANES), lambda i, a, b: (i, 0)),
                  pl.BlockSpec((tt, ROUTE_W), lambda i, a, b: (i, 0)),
                  pl.BlockSpec((1, D_MODEL), lambda i, a, b: (0, 0)),
                  pl.BlockSpec((1, D_MODEL), lambda i, a, b: (0, 0))],
        out_specs=pl.BlockSpec((tt, D_MODEL), lambda i, a, b: (i, 0)),
        scratch_shapes=[pltpu.VMEM((2, TOP_K, tt * TOK_ROWS, LANES), jnp.uint32),
                        pltpu.SemaphoreType.DMA((2,))],
    )
    return pl.pallas_call(
        functools.partial(_combine_kernel, tt=tt),
        grid_spec=grid_spec,
        out_shape=jax.ShapeDtypeStruct((n, D_MODEL), F32),
        compiler_params=_cparams("arbitrary"),
        name="combine",
    )(dest1, dest2, ybuf, x1p, route, ln_g, ln_b)


def _block_plan(counts_row, n, tb):
    counts = counts_row[:N_EXPERTS, 0].astype(jnp.int32)
    pends = jnp.cumsum((counts + tb - 1) // tb * tb)
    nb = (n * TOP_K) // tb + N_EXPERTS
    nvalid = (pends[-1] // tb).astype(jnp.int32)
    blk = jnp.arange(nb, dtype=jnp.int32)
    blk_e = jnp.sum((pends[None, :] <= (blk * tb)[:, None]).astype(jnp.int32), axis=1)
    blk_e = jnp.minimum(blk_e, N_EXPERTS - 1)
    blk_e = jnp.where(blk < nvalid, blk_e, blk_e[jnp.maximum(nvalid - 1, 0)])
    first = jnp.concatenate([jnp.ones((1,), jnp.int32), (blk_e[1:] != blk_e[:-1]).astype(jnp.int32)])
    ids = jnp.arange(N_EXPERTS, dtype=jnp.int32)
    later = lax.cummin(jnp.where(counts > 0, ids, N_EXPERTS), axis=0, reverse=True)
    nxt = jnp.concatenate([later[1:], jnp.full((1,), N_EXPERTS, jnp.int32)])
    nxt = jnp.where(nxt < N_EXPERTS, nxt, -1)
    last_blk = jnp.where(counts > 0, pends // tb - 1, -1).astype(jnp.int32)
    return blk_e, first, nxt[blk_e], nvalid.reshape(1), last_blk


def kernel(x, w_in, w_gk_up, b_gk, conv_w, a_log, dt_bias, gla_norm_g, gdn_norm_g, w_out, ln1_g, ln1_b, w_router_group, b_router_group, w_router_expert, b_router_expert, w_gate, w_up, w_down, ln2_g, ln2_b):
    bsz, seq, d = x.shape
    n = bsz * seq
    x2d = x.reshape(n, d)
    w_main, w_aux = _split_w_in(w_in[0])
    main, aux = _in_proj(x2d, w_main, w_aux, conv_w[0].astype(F32), seq=seq)
    wgk = jnp.zeros((AUX_W, 256), F32).at[:GLA_GATE_RANK].set(w_gk_up[0])
    o_a = _gla(main, aux, wgk, b_gk[0][None], gla_norm_g[0][None], bsz=bsz, seq=seq)
    o_b = _gdn(main, aux, _pad_aux_row(a_log[0], AUX_AB), _pad_aux_row(dt_bias[0], AUX_AB),
               gdn_norm_g[0][None], bsz=bsz, seq=seq)
    rpad = jnp.zeros((ROUTE_W - N_EXPERTS - N_GROUPS, d), F32)
    w_router = jnp.concatenate([w_router_expert[0].T, w_router_group[0].T, rpad], axis=0)
    w_router_hi = w_router.astype(BF16)
    w_router = jnp.concatenate([w_router_hi, (w_router - w_router_hi.astype(F32)).astype(BF16)], axis=0)
    b_router = jnp.concatenate([b_router_expert[0], b_router_group[0], rpad[:, 0]])[:, None]
    x1p, route, gates, counts = _post_mix(o_a, o_b, x2d, w_out[0].astype(BF16), ln1_g[0][None], ln1_b[0][None],
                                          w_router, b_router)
    dest = _plan(route, counts, tb=EXPERT_BLOCK)
    blk_e, first, next_e, nvalid, last_blk = _block_plan(counts, n, EXPERT_BLOCK)
    xs = _dispatch(dest[0], dest[1], last_blk, nvalid, x1p, nb=blk_e.shape[0])
    ybuf = _experts(blk_e, first, next_e, nvalid, xs, w_gate[0], w_up[0], w_down[0])
    out = _combine(dest[0], dest[1], ybuf, x1p, gates, ln2_g[0][None], ln2_b[0][None])
    return out.reshape(bsz, seq, d)
```

```python
import functools

import jax
import jax.numpy as jnp
import numpy as np
from jax import lax
from jax.experimental import pallas as pl
from jax.experimental.pallas import tpu as pltpu

F32 = jnp.float32
BF16 = jnp.bfloat16
HIGHEST = lax.Precision.HIGHEST

D_MODEL = 1024
DEPTH = 1
GLA_HEADS = 4
GLA_DK = 64
GLA_DV = 128
GLA_WIDTH = GLA_HEADS * GLA_DV
GLA_GATE_RANK = 16
GLA_GATE_NORM = 16.0
GDN_HEADS = 4
GDN_DK = 128
GDN_DV = 128
GDN_WIDTH = GDN_HEADS * GDN_DV
CONV_W = 4
CHUNK = 64
N_GROUPS = 8
EXPERTS_PER_GROUP = 8
N_EXPERTS = N_GROUPS * EXPERTS_PER_GROUP
TOP_K = 2
D_EXPERT = 512
LN_EPS = 1e-5
RMS_EPS = 1e-6
ALPHA_DN = (2.0 * DEPTH) ** 0.25

MAIN_W = 2 * GLA_HEADS * GLA_DK + 2 * GLA_WIDTH + 4 * GDN_WIDTH
AUX_W = 128
AUX_LRA = 0
AUX_BB = GLA_GATE_RANK
AUX_AB = GLA_GATE_RANK + GDN_HEADS
SUB = 16
VMEM_LIMIT = 56 * 1024 * 1024


def _cparams(*sem):
    return pltpu.CompilerParams(dimension_semantics=sem, vmem_limit_bytes=VMEM_LIMIT)


def _sigmoid(x):
    return 1.0 / (1.0 + jnp.exp(-x))


def _silu(x):
    return x * _sigmoid(x)


def _log_sigmoid(x):
    return jnp.minimum(x, 0.0) - jnp.log(1.0 + jnp.exp(-jnp.abs(x)))


def _softplus(x):
    return jnp.maximum(x, 0.0) + jnp.log(1.0 + jnp.exp(-jnp.abs(x)))


def _chunk_cumsum(x, c):
    pos = lax.broadcasted_iota(jnp.int32, x.shape, 0) % c
    shift = 1
    while shift < c:
        x = x + jnp.where(pos >= shift, pltpu.roll(x, shift, 0), 0.0)
        shift *= 2
    return x


CONV_PAD = 8
GDN_HW = GDN_HEADS * GDN_DK
GDN_QKV_BLOCK0 = (2 * GLA_HEADS * GLA_DK + 2 * GLA_WIDTH) // GDN_HW


def _in_proj_kernel(x_ref, wm_ref, wa_ref, cw_ref, main_ref, aux_ref, stage_ref, *, tm, tiles_per_seq):
    xb = x_ref[...].astype(BF16)

    @pl.when((pl.program_id(0) % tiles_per_seq) == 0)
    def _():
        stage_ref[:, 0:CONV_PAD, :] = jnp.zeros((3 * GDN_HEADS, CONV_PAD, GDN_DK), F32)

    for j in range(MAIN_W // GDN_HW):
        sl = slice(j * GDN_HW, (j + 1) * GDN_HW)
        r = jnp.dot(xb, wm_ref[:, sl], preferred_element_type=F32)
        b = j - GDN_QKV_BLOCK0
        if not 0 <= b < 3:
            main_ref[:, sl] = r.astype(BF16)
            continue
        for h in range(GDN_HEADS):
            st = stage_ref.at[b * GDN_HEADS + h]
            cols = slice(b * GDN_HW + h * GDN_DK, b * GDN_HW + (h + 1) * GDN_DK)
            st[CONV_PAD:, :] = r[:, h * GDN_DK:(h + 1) * GDN_DK]
            acc = jnp.zeros((tm, GDN_DK), F32)
            for i in range(CONV_W):
                lo = CONV_PAD - (CONV_W - 1) + i
                acc = acc + st[lo:lo + tm, :] * cw_ref[i:i + 1, cols]
            st[0:CONV_PAD, :] = st[tm:tm + CONV_PAD, :]
            yh = _silu(acc)
            if b < 2:
                yh = yh * lax.rsqrt(jnp.sum(yh * yh, axis=-1, keepdims=True) + RMS_EPS)
            if b == 0:
                yh = yh * (GDN_DK ** -0.5)
            main_ref[:, j * GDN_HW + h * GDN_DK:j * GDN_HW + (h + 1) * GDN_DK] = yh.astype(BF16)
    aux_ref[...] = jnp.dot(xb, wa_ref[...], preferred_element_type=F32)


def _in_proj(x2d, w_main, w_aux, conv_w, *, seq, tm=512):
    n = x2d.shape[0]
    return pl.pallas_call(
        functools.partial(_in_proj_kernel, tm=tm, tiles_per_seq=seq // tm),
        grid=(n // tm,),
        in_specs=[pl.BlockSpec((tm, D_MODEL), lambda i: (i, 0)),
                  pl.BlockSpec((D_MODEL, MAIN_W), lambda i: (0, 0)),
                  pl.BlockSpec((D_MODEL, AUX_W), lambda i: (0, 0)),
                  pl.BlockSpec((CONV_W, 3 * GDN_HW), lambda i: (0, 0))],
        out_specs=[pl.BlockSpec((tm, MAIN_W), lambda i: (i, 0)),
                   pl.BlockSpec((tm, AUX_W), lambda i: (i, 0))],
        out_shape=[jax.ShapeDtypeStruct((n, MAIN_W), BF16),
                   jax.ShapeDtypeStruct((n, AUX_W), F32)],
        scratch_shapes=[pltpu.VMEM((3 * GDN_HEADS, tm + CONV_PAD, GDN_DK), F32)],
        compiler_params=_cparams("arbitrary"),
        name="in_proj",
    )(x2d, w_main, w_aux, conv_w)


def _split_w_in(w_in):
    sizes = (GLA_HEADS * GLA_DK, GLA_HEADS * GLA_DK, GLA_WIDTH, GLA_WIDTH, GLA_GATE_RANK,
             GDN_HEADS * GDN_DK, GDN_HEADS * GDN_DK, GDN_WIDTH, GDN_WIDTH, GDN_HEADS, GDN_HEADS)
    offs = np.cumsum((0,) + sizes)
    seg = [w_in[:, offs[i]:offs[i + 1]] for i in range(len(sizes))]
    qa, ka, va, ra, lra, qb, kb, vb, zb, bb, ab = seg
    w_main = jnp.concatenate([qa, ka, va, ra, qb, kb, vb, zb], axis=1).astype(BF16)
    pad = jnp.zeros((w_in.shape[0], AUX_W - GLA_GATE_RANK - 2 * GDN_HEADS), w_in.dtype)
    w_aux = jnp.concatenate([lra, bb, ab, pad], axis=1).astype(BF16)
    return w_main, w_aux


_NT = (((1,), (1,)), ((), ()))
_TN = (((0,), (0,)), ((), ()))


def _gla_kernel(q_ref, k_ref, v_ref, r_ref, aux_ref, wgk_ref, bgk_ref, g_ref, o_ref, st_ref, *, ts, nseq):
    npairs = 2 * nseq
    c = CHUNK
    nsub = c // SUB

    @pl.when(pl.program_id(1) == 0)
    def _():
        st_ref[...] = jnp.zeros_like(st_ref)

    gk_all = [_log_sigmoid(jnp.dot(aux_ref[sq], wgk_ref[...], precision=HIGHEST, preferred_element_type=F32)
                           + bgk_ref[...]) * (1.0 / GLA_GATE_NORM) for sq in range(nseq)]

    row128 = lax.broadcasted_iota(jnp.int32, (c, 128), 0)
    lane128 = lax.broadcasted_iota(jnp.int32, (1, 128), 1)
    lane_masks = [(lane128 // GLA_DK) == hh for hh in range(2)]
    sub_valid = [row128 < SUB * (i + 1) for i in range(nsub)]
    ar = lax.broadcasted_iota(jnp.int32, (c, nsub * c), 0)
    ac = lax.broadcasted_iota(jnp.int32, (c, nsub * c), 1)
    amask = ((ac // c) == (ar // SUB)) & ((ac % c) <= ar)
    scale = GLA_DK ** -0.5
    gnorm = g_ref[...]

    nch = ts // c
    rows = [slice(ch * c, (ch + 1) * c) for ch in range(nch)]
    b_cum = [_chunk_cumsum(g, c) for g in gk_all]

    cp = [(ch, p) for ch in range(nch) for p in range(npairs)]
    qt, qd, kd, kstack, dlast = {}, {}, {}, {}, {}
    for ch, p in cp:
        sq, pp = divmod(p, 2)
        lanes = slice(128 * pp, 128 * (pp + 1))
        bp = b_cum[sq][rows[ch], lanes]
        qp = q_ref[sq, rows[ch], lanes].astype(F32) * scale
        kp = k_ref[sq, rows[ch], lanes].astype(F32)
        c_row = jnp.concatenate(
            [jnp.broadcast_to(bp[SUB * i:SUB * i + 1, :], (SUB, 128)) for i in range(nsub)], axis=0)
        qt[ch, p] = (qp * jnp.exp(bp - c_row)).astype(BF16)
        qd[ch, p] = qp * jnp.exp(bp)
        b_last = bp[c - 1:c, :]
        dlast[ch, p] = jnp.exp(b_last)
        kd[ch, p] = kp * jnp.exp(b_last - bp)
        kts = []
        for i in range(nsub):
            e = jnp.exp(jnp.where(sub_valid[i], bp[SUB * i:SUB * i + 1, :] - bp, 0.0))
            kts.append(jnp.where(sub_valid[i], kp * e, 0.0))
        kstack[ch, p] = jnp.concatenate(kts, axis=0)

    cph = [(ch, p, hh) for ch, p in cp for hh in range(2)]
    vh = {(ch, p, hh): v_ref[p // 2, rows[ch], 128 * (2 * (p % 2) + hh):128 * (2 * (p % 2) + hh + 1)]
          for ch, p, hh in cph}
    r_mat = {(ch, p, hh): lax.dot_general(qt[ch, p], jnp.where(lane_masks[hh], kstack[ch, p], 0.0).astype(BF16),
                                          _NT, preferred_element_type=F32) for ch, p, hh in cph}
    o_intra = {k: jnp.dot(jnp.where(amask, r_mat[k], 0.0).astype(BF16), jnp.concatenate([vh[k]] * nsub, axis=0),
                          preferred_element_type=F32) for k in cph}
    upd = {(ch, p, hh): lax.dot_general(vh[ch, p, hh], jnp.where(lane_masks[hh], kd[ch, p], 0.0).astype(BF16), _TN,
                                        preferred_element_type=F32) for ch, p, hh in cph}

    st = [st_ref[p] for p in range(npairs)]
    for ch in range(nch):
        st_b = [s.astype(BF16) for s in st]
        o_pair = [lax.dot_general(
            jnp.concatenate([jnp.where(lane_masks[hh], qd[ch, p], 0.0).astype(BF16) for hh in range(2)], axis=0),
            st_b[p], _NT, preferred_element_type=F32) for p in range(npairs)]
        o_inter = {(p, hh): o_pair[p][hh * c:(hh + 1) * c] for p in range(npairs) for hh in range(2)}
        st = [st[p] * dlast[ch, p] + upd[ch, p, 0] + upd[ch, p, 1] for p in range(npairs)]
        for p in range(npairs):
            for hh in range(2):
                sq, h = p // 2, 2 * (p % 2) + hh
                o = o_intra[ch, p, hh] + o_inter[p, hh]
                o = o * lax.rsqrt(jnp.mean(o * o, axis=-1, keepdims=True) + RMS_EPS) * gnorm
                gate = _silu(r_ref[sq, rows[ch], 128 * h:128 * (h + 1)].astype(F32))
                o_ref[sq, rows[ch], 128 * h:128 * (h + 1)] = (o * gate).astype(BF16)
    for p in range(npairs):
        st_ref[p] = st[p]


def _gla(main, aux, wgk_pad, b_gk, gla_norm_g, *, bsz, seq, ts=512, nseq=2):
    nt = seq // ts
    assert bsz % nseq == 0
    main4 = main.reshape(bsz // nseq, nseq, seq, MAIN_W)
    aux4 = aux.reshape(bsz // nseq, nseq, seq, AUX_W)
    tok = lambda width, col: pl.BlockSpec((None, nseq, ts, width), lambda b, t: (b, 0, t, col))
    out = pl.pallas_call(
        functools.partial(_gla_kernel, ts=ts, nseq=nseq),
        grid=(bsz // nseq, nt),
        in_specs=[tok(256, 0), tok(256, 1), tok(512, 1), tok(512, 2),
                  tok(AUX_W, 0),
                  pl.BlockSpec((AUX_W, 256), lambda b, t: (0, 0)),
                  pl.BlockSpec((1, 256), lambda b, t: (0, 0)),
                  pl.BlockSpec((1, 128), lambda b, t: (0, 0))],
        out_specs=tok(GLA_WIDTH, 0),
        out_shape=jax.ShapeDtypeStruct((bsz // nseq, nseq, seq, GLA_WIDTH), BF16),
        scratch_shapes=[pltpu.VMEM((2 * nseq, 128, 128), F32)],
        compiler_params=_cparams("parallel", "arbitrary"),
        name="gla",
    )(main4, main4, main4, main4, aux4, wgk_pad, b_gk, gla_norm_g)
    return out.reshape(bsz * seq, GLA_WIDTH)


def _gdn_kernel(q_ref, k_ref, v_ref, z_ref, aux_ref, alog_ref, dtb_ref, g_ref, o_ref, st_ref, *, ts, nseq):
    c = CHUNK
    nch = ts // c

    @pl.when(pl.program_id(1) == 0)
    def _():
        st_ref[...] = jnp.zeros_like(st_ref)

    beta_full, gam_full, gam_t = [], [], []
    for sq in range(nseq):
        aux = aux_ref[sq]
        beta_full.append(_sigmoid(aux))
        g_full = -jnp.exp(alog_ref[...]) * _softplus(aux + dtb_ref[...])
        gam_full.append(_chunk_cumsum(g_full, c))
        gam_t.append(gam_full[-1].T)
    gnorm = g_ref[...]

    row = lax.broadcasted_iota(jnp.int32, (c, 2 * c), 0)
    lane = lax.broadcasted_iota(jnp.int32, (c, 2 * c), 1)
    left = lane < c
    col = lane % c
    causal = col <= row
    strict = col < row
    same32 = (row // 32) == (col // 32)
    same16 = (row // 16) == (col // 16)
    lvl0 = same16 & strict
    lvl1 = same32 & jnp.logical_not(same16) & strict
    lvl2 = jnp.logical_not(same32) & strict

    def pack_rows(x2):
        return jnp.where(left, x2[0:c], x2[c:2 * c])

    def pack_cols(x1):
        return jnp.where(left, x1[0:c], x1[c:2 * c])

    def halves(x):
        return jnp.where(left, x, 0.0).astype(BF16), jnp.where(left, 0.0, x).astype(BF16)

    def mmp(x, y):
        return jnp.dot(x.astype(BF16), jnp.concatenate(halves(y), axis=0), preferred_element_type=F32)

    heads = range(nseq * GDN_HEADS)
    npair = nch // 2
    items = [(h, pr) for h in heads for pr in range(npair)]
    prow = [slice(2 * c * pr, 2 * c * (pr + 1)) for pr in range(npair)]

    qn, kn, beta, gam, gam_r, kb, rhs, rhs_b, qd, kd = ([None] * len(heads) for _ in range(10))
    for h in heads:
        sq, hh = divmod(h, GDN_HEADS)
        hl = slice(128 * hh, 128 * (hh + 1))
        kb[h] = k_ref[sq, :, hl]
        qn[h] = q_ref[sq, :, hl].astype(F32)
        kn[h] = kb[h].astype(F32)
        vh = v_ref[sq, :, hl].astype(F32)
        beta[h] = beta_full[sq][:, AUX_BB + hh:AUX_BB + hh + 1]
        gam[h] = gam_full[sq][:, AUX_AB + hh:AUX_AB + hh + 1]
        gam_r[h] = gam_t[sq][AUX_AB + hh:AUX_AB + hh + 1, :]
        egam = jnp.exp(gam[h])
        rhs[h] = jnp.concatenate([beta[h] * vh, beta[h] * egam * kn[h]], axis=1)
        rhs_b[h] = rhs[h].astype(BF16)
        qd[h] = (qn[h] * egam).astype(BF16)
        gl_rows = jnp.concatenate(
            [jnp.broadcast_to(gam[h][c * (j + 1) - 1:c * (j + 1), :], (c, 1)) for j in range(nch)], axis=0)
        kd[h] = (kn[h] * jnp.exp(gl_rows - gam[h])).astype(BF16)

    kk = [pack_rows(lax.dot_general(kb[h][prow[pr]], kb[h][prow[pr]], _NT, preferred_element_type=F32))
          for h, pr in items]
    qk = [pack_rows(lax.dot_general(qn[h][prow[pr]].astype(BF16), kb[h][prow[pr]], _NT,
                                    preferred_element_type=F32)) for h, pr in items]
    dec = [jnp.exp(jnp.where(causal, pack_cols(gam[h][prow[pr]]) - gam_r[h][:, prow[pr]], 0.0)) for h, pr in items]
    a = [pack_cols(beta[h][prow[pr]]) * kk[i] * dec[i] for i, (h, pr) in enumerate(items)]
    attn = [halves(jnp.where(causal, qk[i] * dec[i], 0.0)) for i in range(len(items))]

    x1 = [jnp.where(lvl0, -ai, 0.0) for ai in a]
    x2 = [mmp(x, x) for x in x1]
    y = [mmp(x, xx) for x, xx in zip(x1, x2)]
    p = [x + xx + yy for x, xx, yy in zip(x1, x2, y)]
    xk = x2
    for _ in range(2):
        xk = [mmp(x, x) for x in xk]
        y = [mmp(pp, x) for pp, x in zip(p, xk)]
        p = [pp + x + yy for pp, x, yy in zip(p, xk, y)]
    for lvl in (lvl1, lvl2):
        low = [jnp.where(lvl, ai, 0.0) for ai in a]
        y = [mmp(pp, lo) for pp, lo in zip(p, low)]
        zz = [lo + yy for lo, yy in zip(low, y)]
        y = [mmp(z, pp) for z, pp in zip(zz, p)]
        p = [pp - z - yy for pp, z, yy in zip(p, zz, y)]

    us = [[None] * nch for _ in heads]
    ws = [[None] * nch for _ in heads]
    for i, (h, pr) in enumerate(items):
        for half, p_half in enumerate(halves(p[i])):
            j = 2 * pr + half
            uw = rhs[h][c * j:c * (j + 1)] + jnp.dot(p_half, rhs_b[h][prow[pr]], preferred_element_type=F32)
            us[h][j] = uw[:, :GDN_DV]
            ws[h][j] = uw[:, GDN_DV:].astype(BF16)

    s = [st_ref[h] for h in heads]
    vprev = [None] * len(heads)
    for j in range(nch):
        rows = slice(c * j, c * (j + 1))
        sb = [s[h].astype(BF16) for h in heads]
        ws_qd = [jnp.dot(jnp.concatenate([ws[h][j], qd[h][rows]], axis=0), sb[h], preferred_element_type=F32)
                 for h in heads]
        vn16 = [(us[h][j] - ws_qd[h][0:c]).astype(BF16) for h in heads]
        o_inter = [ws_qd[h][c:2 * c] for h in heads]
        for h in heads:
            vpair = jnp.concatenate([vn16[h], jnp.zeros_like(vn16[h])] if j % 2 == 0 else [vprev[h], vn16[h]], axis=0)
            o = o_inter[h] + jnp.dot(attn[h * npair + j // 2][j % 2], vpair, preferred_element_type=F32)
            s[h] = jnp.exp(gam[h][c * (j + 1) - 1:c * (j + 1), :]) * s[h] + lax.dot_general(
                kd[h][rows], vn16[h], _TN, preferred_element_type=F32)
            o = o * lax.rsqrt(jnp.mean(o * o, axis=-1, keepdims=True) + RMS_EPS) * gnorm
            sq, hh = divmod(h, GDN_HEADS)
            gate = _silu(z_ref[sq, rows, 128 * hh:128 * (hh + 1)].astype(F32))
            o_ref[sq, rows, 128 * hh:128 * (hh + 1)] = (o * gate).astype(BF16)
        vprev = vn16
    for h in heads:
        st_ref[h] = s[h]


def _gdn(main, aux, alog_pad, dtb_pad, gdn_norm_g, *, bsz, seq, ts=512, nseq=2):
    nt = seq // ts
    hw = GDN_HEADS * GDN_DK
    assert bsz % nseq == 0
    main4 = main.reshape(bsz // nseq, nseq, seq, MAIN_W)
    aux4 = aux.reshape(bsz // nseq, nseq, seq, AUX_W)
    tok = lambda col: pl.BlockSpec((None, nseq, ts, hw), lambda b, t: (b, 0, t, col))
    out = pl.pallas_call(
        functools.partial(_gdn_kernel, ts=ts, nseq=nseq),
        grid=(bsz // nseq, nt),
        in_specs=[tok(3), tok(4), tok(5), tok(6),
                  pl.BlockSpec((None, nseq, ts, AUX_W), lambda b, t: (b, 0, t, 0)),
                  pl.BlockSpec((1, AUX_W), lambda b, t: (0, 0)),
                  pl.BlockSpec((1, AUX_W), lambda b, t: (0, 0)),
                  pl.BlockSpec((1, GDN_DV), lambda b, t: (0, 0))],
        out_specs=pl.BlockSpec((None, nseq, ts, GDN_WIDTH), lambda b, t: (b, 0, t, 0)),
        out_shape=jax.ShapeDtypeStruct((bsz // nseq, nseq, seq, GDN_WIDTH), BF16),
        scratch_shapes=[pltpu.VMEM((nseq * GDN_HEADS, GDN_DK, GDN_DV), F32)],
        compiler_params=_cparams("parallel", "arbitrary"),
        name="gdn",
    )(main4, main4, main4, main4, aux4, alog_pad, dtb_pad, gdn_norm_g)
    return out.reshape(bsz * seq, GDN_WIDTH)


def _pad_aux_row(v, offset):
    return jnp.zeros((1, AUX_W), F32).at[0, offset:offset + v.shape[0]].set(v.astype(F32))


ROUTE_W = 128
R_E1, R_E2, R_RANK1, R_RANK2, R_G1, R_G2 = range(6)
ROUTER_GROUP_COL = N_EXPERTS
POST_MIX_SPLIT = 4


def _layer_norm(h, g, b):
    mu = jnp.mean(h, axis=-1, keepdims=True)
    hc = h - mu
    var = jnp.mean(hc * hc, axis=-1, keepdims=True)
    return hc * lax.rsqrt(var + LN_EPS) * g + b


def _pack_bf16_pairs(x):
    w = x.shape[1] // 2
    lo = lax.bitcast_convert_type(x[:, :w].astype(BF16).astype(F32), jnp.uint32)
    hi = lax.bitcast_convert_type(x[:, w:].astype(BF16).astype(F32), jnp.uint32)
    return (lo >> 16) | (hi & jnp.uint32(0xFFFF0000))


def _unpack_bf16_pairs(p):
    lo = lax.bitcast_convert_type(p << 16, F32)
    hi = lax.bitcast_convert_type(p & jnp.uint32(0xFFFF0000), F32)
    return lo, hi


HALF = D_MODEL // 2
LANES = 128
TOK_ROWS = HALF // LANES


def _store_token_rows(ref, packed):
    m = packed.shape[0]
    for c in range(TOK_ROWS):
        ref[pl.ds(c, m, stride=TOK_ROWS), :] = packed[:, LANES * c:LANES * (c + 1)]


def _load_token_rows(ref, m):
    return jnp.concatenate([ref[pl.ds(c, m, stride=TOK_ROWS), :] for c in range(TOK_ROWS)], axis=1)


def _post_mix_kernel(oa_ref, ob_ref, x_ref, wo_ref, g_ref, b_ref, wr_ref, br_ref,
                     x1p_ref, route_ref, gates_ref, cnt_ref, carry_ref, *, tm):
    @pl.when(pl.program_id(0) == 0)
    def _():
        carry_ref[...] = jnp.zeros_like(carry_ref)

    m = tm // POST_MIX_SPLIT
    groups = range(POST_MIX_SPLIT)
    rs = [slice(h * m, (h + 1) * m) for h in groups]
    y = [jnp.dot(oa_ref[rs[h], :], wo_ref[0:GLA_WIDTH, :], preferred_element_type=F32)
         + jnp.dot(ob_ref[rs[h], :], wo_ref[GLA_WIDTH:, :], preferred_element_type=F32) for h in groups]
    x1 = [_layer_norm(ALPHA_DN * x_ref[rs[h], :] + y[h], g_ref[...], b_ref[...]) for h in groups]
    for h in groups:
        _store_token_rows(x1p_ref.at[pl.ds(h * m * TOK_ROWS, m * TOK_ROWS), :], _pack_bf16_pairs(x1[h]))

    xh = [x.astype(BF16) for x in x1]
    xl = [(x - xb.astype(F32)).astype(BF16) for x, xb in zip(x1, xh)]
    hw = [lax.dot_general(wr_ref[...], xb, _NT, preferred_element_type=F32) for xb in xh]
    lw = [lax.dot_general(wr_ref[0:ROUTE_W, :], xb, _NT, preferred_element_type=F32) for xb in xl]
    logits = [a[:ROUTE_W] + a[ROUTE_W:] + b + br_ref[...] for a, b in zip(hw, lw)]
    sub = lax.broadcasted_iota(jnp.int32, (ROUTE_W, m), 0)
    big = jnp.int32(1 << 20)
    neg = jnp.float32(-jnp.inf)

    def first_argmax(vals):
        mx = jnp.max(vals, axis=0, keepdims=True)
        idx = jnp.min(jnp.where(vals == mx, sub, big), axis=0, keepdims=True)
        return mx, idx

    is_group = (sub >= ROUTER_GROUP_COL) & (sub < ROUTER_GROUP_COL + N_GROUPS)
    gl = [jnp.where(is_group, lg, neg) for lg in logits]
    gtop = [first_argmax(v) for v in gl]
    p_group = [1.0 / jnp.sum(jnp.exp(v - t[0]), axis=0, keepdims=True) for v, t in zip(gl, gtop)]
    el = [jnp.where((sub // EXPERTS_PER_GROUP) == (t[1] - ROUTER_GROUP_COL), lg, neg) for lg, t in zip(logits, gtop)]
    top1 = [first_argmax(v) for v in el]
    top2 = [first_argmax(jnp.where(sub == t[1], neg, v)) for v, t in zip(el, top1)]
    ex = [jnp.exp(b[0] - a[0]) for a, b in zip(top1, top2)]
    w1 = [1.0 / (1.0 + e) for e in ex]
    gate1 = [p * w for p, w in zip(p_group, w1)]
    gate2 = [p * (e * w) for p, e, w in zip(p_group, ex, w1)]

    oh1 = [sub == t[1] for t in top1]
    oh2 = [sub == t[1] for t in top2]
    ri = lax.broadcasted_iota(jnp.int32, (m, m), 0)
    ci = lax.broadcasted_iota(jnp.int32, (m, m), 1)
    ustrict = (ri < ci).astype(BF16)
    oh1f = [o.astype(F32) for o in oh1]
    oh2f = [o.astype(F32) for o in oh2]
    c1 = [jnp.dot(o.astype(BF16), ustrict, preferred_element_type=F32) for o in oh1f]
    c2 = [jnp.dot(o.astype(BF16), ustrict, preferred_element_type=F32) for o in oh2f]
    tot1 = [jnp.sum(o, axis=1, keepdims=True) for o in oh1f]
    tot2 = [jnp.sum(o, axis=1, keepdims=True) for o in oh2f]
    carry = carry_ref[...]
    eye8 = jnp.eye(8, ROUTE_W, dtype=F32)
    zero = jnp.zeros((1, m), F32)
    for h in groups:
        rank1 = jnp.sum(jnp.where(oh1[h], c1[h] + carry, 0.0), axis=0, keepdims=True)
        rank2 = jnp.sum(jnp.where(oh2[h], c2[h] + (carry + tot1[h]), 0.0), axis=0, keepdims=True)
        carry = carry + tot1[h] + tot2[h]
        route_t = jnp.concatenate([top1[h][1].astype(F32), top2[h][1].astype(F32), rank1, rank2,
                                   gate1[h], gate2[h], zero, zero], axis=0)
        route_ref[:, rs[h]] = route_t
        gates_ref[rs[h], :] = lax.dot_general(route_t, eye8, _TN, precision=HIGHEST, preferred_element_type=F32)
    carry_ref[...] = carry
    cnt_ref[...] = carry


def _post_mix(o_a, o_b, x2d, w_out_b, ln_g, ln_b, w_router, b_router, *, tm=1024):
    n = x2d.shape[0]
    return pl.pallas_call(
        functools.partial(_post_mix_kernel, tm=tm),
        grid=(n // tm,),
        in_specs=[pl.BlockSpec((tm, GLA_WIDTH), lambda i: (i, 0)),
                  pl.BlockSpec((tm, GDN_WIDTH), lambda i: (i, 0)),
                  pl.BlockSpec((tm, D_MODEL), lambda i: (i, 0)),
                  pl.BlockSpec((GLA_WIDTH + GDN_WIDTH, D_MODEL), lambda i: (0, 0)),
                  pl.BlockSpec((1, D_MODEL), lambda i: (0, 0)),
                  pl.BlockSpec((1, D_MODEL), lambda i: (0, 0)),
                  pl.BlockSpec((2 * ROUTE_W, D_MODEL), lambda i: (0, 0)),
                  pl.BlockSpec((ROUTE_W, 1), lambda i: (0, 0))],
        out_specs=[pl.BlockSpec((tm * TOK_ROWS, LANES), lambda i: (i, 0)),
                   pl.BlockSpec((8, tm), lambda i: (0, i)),
                   pl.BlockSpec((tm, ROUTE_W), lambda i: (i, 0)),
                   pl.BlockSpec((ROUTE_W, 1), lambda i: (0, 0))],
        out_shape=[jax.ShapeDtypeStruct((n * TOK_ROWS, LANES), jnp.uint32),
                   jax.ShapeDtypeStruct((8, n), F32),
                   jax.ShapeDtypeStruct((n, ROUTE_W), F32),
                   jax.ShapeDtypeStruct((ROUTE_W, 1), F32)],
        scratch_shapes=[pltpu.VMEM((ROUTE_W, 1), F32)],
        compiler_params=_cparams("arbitrary"),
        name="post_mix",
    )(o_a, o_b, x2d, w_out_b, ln_g, ln_b, w_router, b_router)


EXPERT_BLOCK = 256
BLOCKS_PER_STEP = 4

ROW_UNROLL = 8


def _plan_kernel(route_ref, cnt_ref, dest_ref, *, tm, tb):
    counts = cnt_ref[...].astype(jnp.int32)
    padded = jnp.broadcast_to(((counts + (tb - 1)) & jnp.int32(-tb)).astype(F32), (ROUTE_W, ROUTE_W))
    pstart = (_chunk_cumsum(padded, ROUTE_W) - padded)[:, 0:1]
    route = route_ref[...]
    sub = lax.broadcasted_iota(jnp.int32, (ROUTE_W, tm), 0).astype(F32)
    rows = []
    for ce, cr in ((R_E1, R_RANK1), (R_E2, R_RANK2)):
        off = jnp.sum(jnp.where(sub == route[ce:ce + 1, :], pstart, 0.0), axis=0, keepdims=True)
        rows.append(off + route[cr:cr + 1, :])
    rows.extend([jnp.zeros((1, tm), F32)] * (8 - len(rows)))
    dest_ref[...] = jnp.concatenate(rows, axis=0).astype(jnp.int32) * TOK_ROWS


def _plan(route, counts, *, tb, tm=2048):
    n = route.shape[1]
    tm = min(tm, n)
    return pl.pallas_call(
        functools.partial(_plan_kernel, tm=tm, tb=tb),
        grid=(n // tm,),
        in_specs=[pl.BlockSpec((8, tm), lambda i: (0, i)),
                  pl.BlockSpec((ROUTE_W, 1), lambda i: (0, 0))],
        out_specs=pl.BlockSpec((8, tm), lambda i: (0, i)),
        out_shape=jax.ShapeDtypeStruct((8, n), jnp.int32),
        compiler_params=_cparams("parallel"),
        name="plan",
    )(route, counts)


def _dispatch_kernel(d1_ref, d2_ref, last_blk_ref, nvalid_ref, xp_ref, xs_ref, zbuf, sem, zsem, *, tm, tb, nb):
    base = pl.program_id(0) * tm
    rows = tb * TOK_ROWS

    @pl.when(pl.program_id(0) == 0)
    def _():
        zbuf[...] = jnp.zeros_like(zbuf)

        def zero_copy(blk):
            return pltpu.make_async_copy(zbuf, xs_ref.at[pl.ds(pl.multiple_of(blk * rows, rows), rows), :], zsem)

        def for_each_block(fn):
            def per_expert(e, carry):
                @pl.when(last_blk_ref[e] >= 0)
                def _():
                    fn(zero_copy(last_blk_ref[e]))
                return carry

            def per_tail(blk, carry):
                fn(zero_copy(blk))
                return carry
            lax.fori_loop(0, N_EXPERTS, per_expert, 0)
            lax.fori_loop(nvalid_ref[0], nb, per_tail, 0)

        for_each_block(lambda cp: cp.start())
        for_each_block(lambda cp: cp.wait())

    def body(j, carry):
        for u in range(ROW_UNROLL):
            r = j * ROW_UNROLL + u
            for k, d_ref in enumerate((d1_ref, d2_ref)):
                dst = pl.multiple_of(d_ref[base + r], TOK_ROWS)
                pltpu.make_async_copy(xp_ref.at[pl.ds(r * TOK_ROWS, TOK_ROWS), :],
                                      xs_ref.at[pl.ds(dst, TOK_ROWS), :], sem).start(priority=k)
        return carry
    lax.fori_loop(0, tm // ROW_UNROLL, body, 0)
    for _ in range(TOP_K):
        pltpu.make_async_copy(xp_ref, xs_ref.at[pl.ds(0, tm * TOK_ROWS), :], sem).wait()


def _dispatch(dest1, dest2, last_blk, nvalid, x1p, *, nb, tb=EXPERT_BLOCK, tm=1024):
    n = x1p.shape[0] // TOK_ROWS
    grid_spec = pltpu.PrefetchScalarGridSpec(
        num_scalar_prefetch=4,
        grid=(n // tm,),
        in_specs=[pl.BlockSpec((tm * TOK_ROWS, LANES), lambda i, a, b, lb, nv: (i, 0))],
        out_specs=pl.BlockSpec(memory_space=pl.ANY),
        scratch_shapes=[pltpu.VMEM((tb * TOK_ROWS, LANES), jnp.uint32),
                        pltpu.SemaphoreType.DMA(()),
                        pltpu.SemaphoreType.DMA(())],
    )
    return pl.pallas_call(
        functools.partial(_dispatch_kernel, tm=tm, tb=tb, nb=nb),
        grid_spec=grid_spec,
        out_shape=jax.ShapeDtypeStruct((nb * tb * TOK_ROWS, LANES), jnp.uint32),
        compiler_params=_cparams("arbitrary"),
        name="dispatch",
    )(dest1, dest2, last_blk, nvalid, x1p)


def _expert_kernel(blk_e_ref, first_ref, next_e_ref, nvalid_ref, xs_ref, wg_hbm, wu_hbm, wd_hbm, y_ref,
                   wgs, wus, wds, wgb, wub, wdb, sem, *, tb):
    rows = tb * TOK_ROWS

    def weight_copies(e):
        return [pltpu.make_async_copy(src.at[e], dst, sem.at[j])
                for j, (src, dst) in enumerate(((wg_hbm, wgs), (wu_hbm, wus), (wd_hbm, wds)))]

    @pl.when((pl.program_id(0) == 0) & (nvalid_ref[0] > 0))
    def _():
        for cp in weight_copies(blk_e_ref[0]):
            cp.start()

    def run(i, sub, nblk):
        @pl.when(first_ref[i] == 1)
        def _():
            for cp in weight_copies(blk_e_ref[i]):
                cp.wait()
            wgb[...] = wgs[...].astype(BF16)
            wub[...] = wus[...].astype(BF16)
            wdb[...] = wds[...].astype(BF16)
            mlp(sub, nblk)

            @pl.when(next_e_ref[i] >= 0)
            def _():
                for cp in weight_copies(next_e_ref[i]):
                    cp.start()

        @pl.when(first_ref[i] == 0)
        def _():
            mlp(sub, nblk)

    def mlp(sub, nblk):
        m = nblk * tb
        span = pl.ds(sub * rows, nblk * rows)
        lo, hi = _unpack_bf16_pairs(_load_token_rows(xs_ref.at[span, :], m))
        xb = jnp.concatenate([lo.astype(BF16), hi.astype(BF16)], axis=1)
        hg = jnp.dot(xb, wgb[...], preferred_element_type=F32)
        hu = jnp.dot(xb, wub[...], preferred_element_type=F32)
        hid = (_silu(hg) * hu).astype(BF16)
        _store_token_rows(y_ref.at[span, :], _pack_bf16_pairs(jnp.dot(hid, wdb[...], preferred_element_type=F32)))

    nvalid = nvalid_ref[0]
    for sub in range(0, BLOCKS_PER_STEP, 2):
        i0 = pl.program_id(0) * BLOCKS_PER_STEP + sub
        i1 = i0 + 1
        paired = (i1 < nvalid) & (first_ref[i1] == 0)

        @pl.when(paired)
        def _():
            run(i0, sub, 2)

        @pl.when(jnp.logical_not(paired))
        def _():
            @pl.when(i0 < nvalid)
            def _():
                run(i0, sub, 1)

            @pl.when(i1 < nvalid)
            def _():
                run(i1, sub + 1, 1)

        for s_, i in ((sub, i0), (sub + 1, i1)):
            @pl.when(i >= nvalid)
            def _():
                y_ref[pl.ds(s_ * rows, rows), :] = jnp.zeros((rows, LANES), jnp.uint32)


def _experts(blk_e, first, next_e, nvalid, xs, w_gate, w_up, w_down, *, tb=EXPERT_BLOCK):
    nb = blk_e.shape[0]
    assert nb % BLOCKS_PER_STEP == 0
    step_rows = BLOCKS_PER_STEP * tb * TOK_ROWS
    last_step = lambda nv: jnp.maximum((nv[0] - 1) // BLOCKS_PER_STEP, 0)
    grid_spec = pltpu.PrefetchScalarGridSpec(
        num_scalar_prefetch=4,
        grid=(nb // BLOCKS_PER_STEP,),
        in_specs=[pl.BlockSpec((step_rows, LANES), lambda s, be, fr, ne, nv: (jnp.minimum(s, last_step(nv)), 0)),
                  pl.BlockSpec(memory_space=pl.ANY),
                  pl.BlockSpec(memory_space=pl.ANY),
                  pl.BlockSpec(memory_space=pl.ANY)],
        out_specs=pl.BlockSpec((step_rows, LANES), lambda s, be, fr, ne, nv: (s, 0)),
        scratch_shapes=[pltpu.VMEM((D_MODEL, D_EXPERT), F32),
                        pltpu.VMEM((D_MODEL, D_EXPERT), F32),
                        pltpu.VMEM((D_EXPERT, D_MODEL), F32),
                        pltpu.VMEM((D_MODEL, D_EXPERT), BF16),
                        pltpu.VMEM((D_MODEL, D_EXPERT), BF16),
                        pltpu.VMEM((D_EXPERT, D_MODEL), BF16),
                        pltpu.SemaphoreType.DMA((3,))],
    )
    return pl.pallas_call(
        functools.partial(_expert_kernel, tb=tb),
        grid_spec=grid_spec,
        out_shape=jax.ShapeDtypeStruct((nb * tb * TOK_ROWS, LANES), jnp.uint32),
        compiler_params=_cparams("arbitrary"),
        name="experts",
    )(blk_e, first, next_e, nvalid, xs, w_gate, w_up, w_down)


def _combine_kernel(d1_ref, d2_ref, y_hbm, x1_ref, route_ref, g_ref, b_ref, o_ref, ybuf, sem, *, tt):
    i = pl.program_id(0)
    nsteps = pl.num_programs(0)
    slot = i % 2

    def issue(tile, slot_):
        def body(j, carry):
            for u in range(ROW_UNROLL):
                r = j * ROW_UNROLL + u
                for k, d_ref in enumerate((d1_ref, d2_ref)):
                    src = pl.multiple_of(d_ref[tile * tt + r], TOK_ROWS)
                    pltpu.make_async_copy(y_hbm.at[pl.ds(src, TOK_ROWS), :],
                                          ybuf.at[slot_, k, pl.ds(r * TOK_ROWS, TOK_ROWS), :],
                                          sem.at[slot_]).start(priority=k)
            return carry
        lax.fori_loop(0, tt // ROW_UNROLL, body, 0)

    @pl.when(i == 0)
    def _():
        issue(0, 0)

    @pl.when(i + 1 < nsteps)
    def _():
        issue(i + 1, 1 - slot)

    for k in range(TOP_K):
        pltpu.make_async_copy(y_hbm.at[pl.ds(0, tt * TOK_ROWS), :], ybuf.at[slot, k], sem.at[slot]).wait()
    route = route_ref[...]
    g1 = route[:, R_G1:R_G1 + 1]
    g2 = route[:, R_G2:R_G2 + 1]
    lo1, hi1 = _unpack_bf16_pairs(_load_token_rows(ybuf.at[slot, 0], tt))
    lo2, hi2 = _unpack_bf16_pairs(_load_token_rows(ybuf.at[slot, 1], tt))
    x_lo, x_hi = _unpack_bf16_pairs(_load_token_rows(x1_ref, tt))
    h_lo = ALPHA_DN * x_lo + (g1 * lo1 + g2 * lo2)
    h_hi = ALPHA_DN * x_hi + (g1 * hi1 + g2 * hi2)
    mu = (jnp.sum(h_lo, axis=-1, keepdims=True) + jnp.sum(h_hi, axis=-1, keepdims=True)) * (1.0 / D_MODEL)
    c_lo = h_lo - mu
    c_hi = h_hi - mu
    var = (jnp.sum(c_lo * c_lo, axis=-1, keepdims=True) + jnp.sum(c_hi * c_hi, axis=-1, keepdims=True)) * (1.0 / D_MODEL)
    inv = lax.rsqrt(var + LN_EPS)
    o_ref[:, 0:HALF] = c_lo * inv * g_ref[:, 0:HALF] + b_ref[:, 0:HALF]
    o_ref[:, HALF:] = c_hi * inv * g_ref[:, HALF:] + b_ref[:, HALF:]


def _combine(dest1, dest2, ybuf, x1p, route, ln_g, ln_b, *, tt=512):
    n = x1p.shape[0] // TOK_ROWS
    grid_spec = pltpu.PrefetchScalarGridSpec(
        num_scalar_prefetch=2,
        grid=(n // tt,),
        in_specs=[pl.BlockSpec(memory_space=pl.ANY),
                  pl.BlockSpec((tt * TOK_ROWS, LANES), lambda i, a, b: (i, 0)),
                  pl.BlockSpec((tt, ROUTE_W), lambda i, a, b: (i, 0)),
                  pl.BlockSpec((1, D_MODEL), lambda i, a, b: (0, 0)),
                  pl.BlockSpec((1, D_MODEL), lambda i, a, b: (0, 0))],
        out_specs=pl.BlockSpec((tt, D_MODEL), lambda i, a, b: (i, 0)),
        scratch_shapes=[pltpu.VMEM((2, TOP_K, tt * TOK_ROWS, LANES), jnp.uint32),
                        pltpu.SemaphoreType.DMA((2,))],
    )
    return pl.pallas_call(
        functools.partial(_combine_kernel, tt=tt),
        grid_spec=grid_spec,
        out_shape=jax.ShapeDtypeStruct((n, D_MODEL), F32),
        compiler_params=_cparams("arbitrary"),
        name="combine",
    )(dest1, dest2, ybuf, x1p, route, ln_g, ln_b)


def _block_plan(counts_row, n, tb):
    counts = counts_row[:N_EXPERTS, 0].astype(jnp.int32)
    pends = jnp.cumsum((counts + tb - 1) // tb * tb)
    nb = (n * TOP_K) // tb + N_EXPERTS
    nvalid = (pends[-1] // tb).astype(jnp.int32)
    blk = jnp.arange(nb, dtype=jnp.int32)
    blk_e = jnp.sum((pends[None, :] <= (blk * tb)[:, None]).astype(jnp.int32), axis=1)
    blk_e = jnp.minimum(blk_e, N_EXPERTS - 1)
    blk_e = jnp.where(blk < nvalid, blk_e, blk_e[jnp.maximum(nvalid - 1, 0)])
    first = jnp.concatenate([jnp.ones((1,), jnp.int32), (blk_e[1:] != blk_e[:-1]).astype(jnp.int32)])
    ids = jnp.arange(N_EXPERTS, dtype=jnp.int32)
    later = lax.cummin(jnp.where(counts > 0, ids, N_EXPERTS), axis=0, reverse=True)
    nxt = jnp.concatenate([later[1:], jnp.full((1,), N_EXPERTS, jnp.int32)])
    nxt = jnp.where(nxt < N_EXPERTS, nxt, -1)
    last_blk = jnp.where(counts > 0, pends // tb - 1, -1).astype(jnp.int32)
    return blk_e, first, nxt[blk_e], nvalid.reshape(1), last_blk


def kernel(x, w_in, w_gk_up, b_gk, conv_w, a_log, dt_bias, gla_norm_g, gdn_norm_g, w_out, ln1_g, ln1_b, w_router_group, b_router_group, w_router_expert, b_router_expert, w_gate, w_up, w_down, ln2_g, ln2_b):
    bsz, seq, d = x.shape
    n = bsz * seq
    x2d = x.reshape(n, d)
    w_main, w_aux = _split_w_in(w_in[0])
    main, aux = _in_proj(x2d, w_main, w_aux, conv_w[0].astype(F32), seq=seq)
    wgk = jnp.zeros((AUX_W, 256), F32).at[:GLA_GATE_RANK].set(w_gk_up[0])
    o_a = _gla(main, aux, wgk, b_gk[0][None], gla_norm_g[0][None], bsz=bsz, seq=seq)
    o_b = _gdn(main, aux, _pad_aux_row(a_log[0], AUX_AB), _pad_aux_row(dt_bias[0], AUX_AB),
               gdn_norm_g[0][None], bsz=bsz, seq=seq)
    rpad = jnp.zeros((ROUTE_W - N_EXPERTS - N_GROUPS, d), F32)
    w_router = jnp.concatenate([w_router_expert[0].T, w_router_group[0].T, rpad], axis=0)
    w_router_hi = w_router.astype(BF16)
    w_router = jnp.concatenate([w_router_hi, (w_router - w_router_hi.astype(F32)).astype(BF16)], axis=0)
    b_router = jnp.concatenate([b_router_expert[0], b_router_group[0], rpad[:, 0]])[:, None]
    x1p, route, gates, counts = _post_mix(o_a, o_b, x2d, w_out[0].astype(BF16), ln1_g[0][None], ln1_b[0][None],
                                          w_router, b_router)
    dest = _plan(route, counts, tb=EXPERT_BLOCK)
    blk_e, first, next_e, nvalid, last_blk = _block_plan(counts, n, EXPERT_BLOCK)
    xs = _dispatch(dest[0], dest[1], last_blk, nvalid, x1p, nb=blk_e.shape[0])
    ybuf = _experts(blk_e, first, next_e, nvalid, xs, w_gate[0], w_up[0], w_down[0])
    out = _combine(dest[0], dest[1], ybuf, x1p, gates, ln2_g[0][None], ln2_b[0][None])
    return out.reshape(bsz, seq, d)
```

```python
import functools

import jax
import jax.numpy as jnp
import numpy as np
from jax import lax
from jax.experimental import pallas as pl
from jax.experimental.pallas import tpu as pltpu

F32 = jnp.float32
BF16 = jnp.bfloat16
HIGHEST = lax.Precision.HIGHEST

D_MODEL = 1024
DEPTH = 1
GLA_HEADS = 4
GLA_DK = 64
GLA_DV = 128
GLA_WIDTH = GLA_HEADS * GLA_DV
GLA_GATE_RANK = 16
GLA_GATE_NORM = 16.0
GDN_HEADS = 4
GDN_DK = 128
GDN_DV = 128
GDN_WIDTH = GDN_HEADS * GDN_DV
CONV_W = 4
CHUNK = 64
N_GROUPS = 8
EXPERTS_PER_GROUP = 8
N_EXPERTS = N_GROUPS * EXPERTS_PER_GROUP
TOP_K = 2
D_EXPERT = 512
LN_EPS = 1e-5
RMS_EPS = 1e-6
ALPHA_DN = (2.0 * DEPTH) ** 0.25

MAIN_W = 2 * GLA_HEADS * GLA_DK + 2 * GLA_WIDTH + 4 * GDN_WIDTH
AUX_W = 128
AUX_LRA = 0
AUX_BB = GLA_GATE_RANK
AUX_AB = GLA_GATE_RANK + GDN_HEADS
SUB = 16
VMEM_LIMIT = 56 * 1024 * 1024


def _cparams(*sem):
    return pltpu.CompilerParams(dimension_semantics=sem, vmem_limit_bytes=VMEM_LIMIT)


def _sigmoid(x):
    return 1.0 / (1.0 + jnp.exp(-x))


def _silu(x):
    return x * _sigmoid(x)


def _log_sigmoid(x):
    return jnp.minimum(x, 0.0) - jnp.log(1.0 + jnp.exp(-jnp.abs(x)))


def _softplus(x):
    return jnp.maximum(x, 0.0) + jnp.log(1.0 + jnp.exp(-jnp.abs(x)))


def _chunk_cumsum(x, c):
    pos = lax.broadcasted_iota(jnp.int32, x.shape, 0) % c
    shift = 1
    while shift < c:
        x = x + jnp.where(pos >= shift, pltpu.roll(x, shift, 0), 0.0)
        shift *= 2
    return x


CONV_PAD = 8
GDN_HW = GDN_HEADS * GDN_DK
GDN_QKV_BLOCK0 = (2 * GLA_HEADS * GLA_DK + 2 * GLA_WIDTH) // GDN_HW


def _in_proj_kernel(x_ref, wm_ref, wa_ref, cw_ref, main_ref, aux_ref, stage_ref, *, tm, tiles_per_seq):
    xb = x_ref[...].astype(BF16)

    @pl.when((pl.program_id(0) % tiles_per_seq) == 0)
    def _():
        stage_ref[:, 0:CONV_PAD, :] = jnp.zeros((3 * GDN_HEADS, CONV_PAD, GDN_DK), F32)

    for j in range(MAIN_W // GDN_HW):
        sl = slice(j * GDN_HW, (j + 1) * GDN_HW)
        r = jnp.dot(xb, wm_ref[:, sl], preferred_element_type=F32)
        b = j - GDN_QKV_BLOCK0
        if not 0 <= b < 3:
            main_ref[:, sl] = r.astype(BF16)
            continue
        for h in range(GDN_HEADS):
            st = stage_ref.at[b * GDN_HEADS + h]
            cols = slice(b * GDN_HW + h * GDN_DK, b * GDN_HW + (h + 1) * GDN_DK)
            st[CONV_PAD:, :] = r[:, h * GDN_DK:(h + 1) * GDN_DK]
            acc = jnp.zeros((tm, GDN_DK), F32)
            for i in range(CONV_W):
                lo = CONV_PAD - (CONV_W - 1) + i
                acc = acc + st[lo:lo + tm, :] * cw_ref[i:i + 1, cols]
            st[0:CONV_PAD, :] = st[tm:tm + CONV_PAD, :]
            yh = _silu(acc)
            if b < 2:
                yh = yh * lax.rsqrt(jnp.sum(yh * yh, axis=-1, keepdims=True) + RMS_EPS)
            if b == 0:
                yh = yh * (GDN_DK ** -0.5)
            main_ref[:, j * GDN_HW + h * GDN_DK:j * GDN_HW + (h + 1) * GDN_DK] = yh.astype(BF16)
    aux_ref[...] = jnp.dot(xb, wa_ref[...], preferred_element_type=F32)


def _in_proj(x2d, w_main, w_aux, conv_w, *, seq, tm=512):
    n = x2d.shape[0]
    return pl.pallas_call(
        functools.partial(_in_proj_kernel, tm=tm, tiles_per_seq=seq // tm),
        grid=(n // tm,),
        in_specs=[pl.BlockSpec((tm, D_MODEL), lambda i: (i, 0)),
                  pl.BlockSpec((D_MODEL, MAIN_W), lambda i: (0, 0)),
                  pl.BlockSpec((D_MODEL, AUX_W), lambda i: (0, 0)),
                  pl.BlockSpec((CONV_W, 3 * GDN_HW), lambda i: (0, 0))],
        out_specs=[pl.BlockSpec((tm, MAIN_W), lambda i: (i, 0)),
                   pl.BlockSpec((tm, AUX_W), lambda i: (i, 0))],
        out_shape=[jax.ShapeDtypeStruct((n, MAIN_W), BF16),
                   jax.ShapeDtypeStruct((n, AUX_W), F32)],
        scratch_shapes=[pltpu.VMEM((3 * GDN_HEADS, tm + CONV_PAD, GDN_DK), F32)],
        compiler_params=_cparams("arbitrary"),
        name="in_proj",
    )(x2d, w_main, w_aux, conv_w)


def _split_w_in(w_in):
    sizes = (GLA_HEADS * GLA_DK, GLA_HEADS * GLA_DK, GLA_WIDTH, GLA_WIDTH, GLA_GATE_RANK,
             GDN_HEADS * GDN_DK, GDN_HEADS * GDN_DK, GDN_WIDTH, GDN_WIDTH, GDN_HEADS, GDN_HEADS)
    offs = np.cumsum((0,) + sizes)
    seg = [w_in[:, offs[i]:offs[i + 1]] for i in range(len(sizes))]
    qa, ka, va, ra, lra, qb, kb, vb, zb, bb, ab = seg
    w_main = jnp.concatenate([qa, ka, va, ra, qb, kb, vb, zb], axis=1).astype(BF16)
    pad = jnp.zeros((w_in.shape[0], AUX_W - GLA_GATE_RANK - 2 * GDN_HEADS), w_in.dtype)
    w_aux = jnp.concatenate([lra, bb, ab, pad], axis=1).astype(BF16)
    return w_main, w_aux


_NT = (((1,), (1,)), ((), ()))
_TN = (((0,), (0,)), ((), ()))


def _gla_kernel(q_ref, k_ref, v_ref, r_ref, aux_ref, wgk_ref, bgk_ref, g_ref, o_ref, st_ref, *, ts, nseq):
    npairs = 2 * nseq
    c = CHUNK
    nsub = c // SUB

    @pl.when(pl.program_id(1) == 0)
    def _():
        st_ref[...] = jnp.zeros_like(st_ref)

    gk_all = [_log_sigmoid(jnp.dot(aux_ref[sq], wgk_ref[...], precision=HIGHEST, preferred_element_type=F32)
                           + bgk_ref[...]) * (1.0 / GLA_GATE_NORM) for sq in range(nseq)]

    row128 = lax.broadcasted_iota(jnp.int32, (c, 128), 0)
    lane128 = lax.broadcasted_iota(jnp.int32, (1, 128), 1)
    lane_masks = [(lane128 // GLA_DK) == hh for hh in range(2)]
    sub_valid = [row128 < SUB * (i + 1) for i in range(nsub)]
    ar = lax.broadcasted_iota(jnp.int32, (c, nsub * c), 0)
    ac = lax.broadcasted_iota(jnp.int32, (c, nsub * c), 1)
    amask = ((ac // c) == (ar // SUB)) & ((ac % c) <= ar)
    scale = GLA_DK ** -0.5
    gnorm = g_ref[...]

    nch = ts // c
    rows = [slice(ch * c, (ch + 1) * c) for ch in range(nch)]
    b_cum = [_chunk_cumsum(g, c) for g in gk_all]

    cp = [(ch, p) for ch in range(nch) for p in range(npairs)]
    qt, qd, kd, kstack, dlast = {}, {}, {}, {}, {}
    for ch, p in cp:
        sq, pp = divmod(p, 2)
        lanes = slice(128 * pp, 128 * (pp + 1))
        bp = b_cum[sq][rows[ch], lanes]
        qp = q_ref[sq, rows[ch], lanes].astype(F32) * scale
        kp = k_ref[sq, rows[ch], lanes].astype(F32)
        c_row = jnp.concatenate(
            [jnp.broadcast_to(bp[SUB * i:SUB * i + 1, :], (SUB, 128)) for i in range(nsub)], axis=0)
        qt[ch, p] = (qp * jnp.exp(bp - c_row)).astype(BF16)
        qd[ch, p] = qp * jnp.exp(bp)
        b_last = bp[c - 1:c, :]
        dlast[ch, p] = jnp.exp(b_last)
        kd[ch, p] = kp * jnp.exp(b_last - bp)
        kts = []
        for i in range(nsub):
            e = jnp.exp(jnp.where(sub_valid[i], bp[SUB * i:SUB * i + 1, :] - bp, 0.0))
            kts.append(jnp.where(sub_valid[i], kp * e, 0.0))
        kstack[ch, p] = jnp.concatenate(kts, axis=0)

    cph = [(ch, p, hh) for ch, p in cp for hh in range(2)]
    vh = {(ch, p, hh): v_ref[p // 2, rows[ch], 128 * (2 * (p % 2) + hh):128 * (2 * (p % 2) + hh + 1)]
          for ch, p, hh in cph}
    r_mat = {(ch, p, hh): lax.dot_general(qt[ch, p], jnp.where(lane_masks[hh], kstack[ch, p], 0.0).astype(BF16),
                                          _NT, preferred_element_type=F32) for ch, p, hh in cph}
    o_intra = {k: jnp.dot(jnp.where(amask, r_mat[k], 0.0).astype(BF16), jnp.concatenate([vh[k]] * nsub, axis=0),
                          preferred_element_type=F32) for k in cph}
    upd = {(ch, p, hh): lax.dot_general(vh[ch, p, hh], jnp.where(lane_masks[hh], kd[ch, p], 0.0).astype(BF16), _TN,
                                        preferred_element_type=F32) for ch, p, hh in cph}

    st = [st_ref[p] for p in range(npairs)]
    for ch in range(nch):
        st_b = [s.astype(BF16) for s in st]
        o_pair = [lax.dot_general(
            jnp.concatenate([jnp.where(lane_masks[hh], qd[ch, p], 0.0).astype(BF16) for hh in range(2)], axis=0),
            st_b[p], _NT, preferred_element_type=F32) for p in range(npairs)]
        o_inter = {(p, hh): o_pair[p][hh * c:(hh + 1) * c] for p in range(npairs) for hh in range(2)}
        st = [st[p] * dlast[ch, p] + upd[ch, p, 0] + upd[ch, p, 1] for p in range(npairs)]
        for p in range(npairs):
            for hh in range(2):
                sq, h = p // 2, 2 * (p % 2) + hh
                o = o_intra[ch, p, hh] + o_inter[p, hh]
                o = o * lax.rsqrt(jnp.mean(o * o, axis=-1, keepdims=True) + RMS_EPS) * gnorm
                gate = _silu(r_ref[sq, rows[ch], 128 * h:128 * (h + 1)].astype(F32))
                o_ref[sq, rows[ch], 128 * h:128 * (h + 1)] = (o * gate).astype(BF16)
    for p in range(npairs):
        st_ref[p] = st[p]


def _gla(main, aux, wgk_pad, b_gk, gla_norm_g, *, bsz, seq, ts=512, nseq=2):
    nt = seq // ts
    assert bsz % nseq == 0
    main4 = main.reshape(bsz // nseq, nseq, seq, MAIN_W)
    aux4 = aux.reshape(bsz // nseq, nseq, seq, AUX_W)
    tok = lambda width, col: pl.BlockSpec((None, nseq, ts, width), lambda b, t: (b, 0, t, col))
    out = pl.pallas_call(
        functools.partial(_gla_kernel, ts=ts, nseq=nseq),
        grid=(bsz // nseq, nt),
        in_specs=[tok(256, 0), tok(256, 1), tok(512, 1), tok(512, 2),
                  tok(AUX_W, 0),
                  pl.BlockSpec((AUX_W, 256), lambda b, t: (0, 0)),
                  pl.BlockSpec((1, 256), lambda b, t: (0, 0)),
                  pl.BlockSpec((1, 128), lambda b, t: (0, 0))],
        out_specs=tok(GLA_WIDTH, 0),
        out_shape=jax.ShapeDtypeStruct((bsz // nseq, nseq, seq, GLA_WIDTH), BF16),
        scratch_shapes=[pltpu.VMEM((2 * nseq, 128, 128), F32)],
        compiler_params=_cparams("parallel", "arbitrary"),
        name="gla",
    )(main4, main4, main4, main4, aux4, wgk_pad, b_gk, gla_norm_g)
    return out.reshape(bsz * seq, GLA_WIDTH)


def _gdn_kernel(q_ref, k_ref, v_ref, z_ref, aux_ref, alog_ref, dtb_ref, g_ref, o_ref, st_ref, *, ts, nseq):
    c = CHUNK
    nch = ts // c

    @pl.when(pl.program_id(1) == 0)
    def _():
        st_ref[...] = jnp.zeros_like(st_ref)

    beta_full, gam_full, gam_t = [], [], []
    for sq in range(nseq):
        aux = aux_ref[sq]
        beta_full.append(_sigmoid(aux))
        g_full = -jnp.exp(alog_ref[...]) * _softplus(aux + dtb_ref[...])
        gam_full.append(_chunk_cumsum(g_full, c))
        gam_t.append(gam_full[-1].T)
    gnorm = g_ref[...]

    row = lax.broadcasted_iota(jnp.int32, (c, 2 * c), 0)
    lane = lax.broadcasted_iota(jnp.int32, (c, 2 * c), 1)
    left = lane < c
    col = lane % c
    causal = col <= row
    strict = col < row
    same32 = (row // 32) == (col // 32)
    same16 = (row // 16) == (col // 16)
    lvl0 = same16 & strict
    lvl1 = same32 & jnp.logical_not(same16) & strict
    lvl2 = jnp.logical_not(same32) & strict

    def pack_rows(x2):
        return jnp.where(left, x2[0:c], x2[c:2 * c])

    def pack_cols(x1):
        return jnp.where(left, x1[0:c], x1[c:2 * c])

    def halves(x):
        return jnp.where(left, x, 0.0).astype(BF16), jnp.where(left, 0.0, x).astype(BF16)

    def mmp(x, y):
        return jnp.dot(x.astype(BF16), jnp.concatenate(halves(y), axis=0), preferred_element_type=F32)

    heads = range(nseq * GDN_HEADS)
    npair = nch // 2
    items = [(h, pr) for h in heads for pr in range(npair)]
    prow = [slice(2 * c * pr, 2 * c * (pr + 1)) for pr in range(npair)]

    qn, kn, beta, gam, gam_r, kb, rhs, rhs_b, qd, kd = ([None] * len(heads) for _ in range(10))
    for h in heads:
        sq, hh = divmod(h, GDN_HEADS)
        hl = slice(128 * hh, 128 * (hh + 1))
        kb[h] = k_ref[sq, :, hl]
        qn[h] = q_ref[sq, :, hl].astype(F32)
        kn[h] = kb[h].astype(F32)
        vh = v_ref[sq, :, hl].astype(F32)
        beta[h] = beta_full[sq][:, AUX_BB + hh:AUX_BB + hh + 1]
        gam[h] = gam_full[sq][:, AUX_AB + hh:AUX_AB + hh + 1]
        gam_r[h] = gam_t[sq][AUX_AB + hh:AUX_AB + hh + 1, :]
        egam = jnp.exp(gam[h])
        rhs[h] = jnp.concatenate([beta[h] * vh, beta[h] * egam * kn[h]], axis=1)
        rhs_b[h] = rhs[h].astype(BF16)
        qd[h] = (qn[h] * egam).astype(BF16)
        gl_rows = jnp.concatenate(
            [jnp.broadcast_to(gam[h][c * (j + 1) - 1:c * (j + 1), :], (c, 1)) for j in range(nch)], axis=0)
        kd[h] = (kn[h] * jnp.exp(gl_rows - gam[h])).astype(BF16)

    kk = [pack_rows(lax.dot_general(kb[h][prow[pr]], kb[h][prow[pr]], _NT, preferred_element_type=F32))
          for h, pr in items]
    qk = [pack_rows(lax.dot_general(qn[h][prow[pr]].astype(BF16), kb[h][prow[pr]], _NT,
                                    preferred_element_type=F32)) for h, pr in items]
    dec = [jnp.exp(jnp.where(causal, pack_cols(gam[h][prow[pr]]) - gam_r[h][:, prow[pr]], 0.0)) for h, pr in items]
    a = [pack_cols(beta[h][prow[pr]]) * kk[i] * dec[i] for i, (h, pr) in enumerate(items)]
    attn = [halves(jnp.where(causal, qk[i] * dec[i], 0.0)) for i in range(len(items))]

    x1 = [jnp.where(lvl0, -ai, 0.0) for ai in a]
    x2 = [mmp(x, x) for x in x1]
    y = [mmp(x, xx) for x, xx in zip(x1, x2)]
    p = [x + xx + yy for x, xx, yy in zip(x1, x2, y)]
    xk = x2
    for _ in range(2):
        xk = [mmp(x, x) for x in xk]
        y = [mmp(pp, x) for pp, x in zip(p, xk)]
        p = [pp + x + yy for pp, x, yy in zip(p, xk, y)]
    for lvl in (lvl1, lvl2):
        low = [jnp.where(lvl, ai, 0.0) for ai in a]
        y = [mmp(pp, lo) for pp, lo in zip(p, low)]
        zz = [lo + yy for lo, yy in zip(low, y)]
        y = [mmp(z, pp) for z, pp in zip(zz, p)]
        p = [pp - z - yy for pp, z, yy in zip(p, zz, y)]

    us = [[None] * nch for _ in heads]
    ws = [[None] * nch for _ in heads]
    for i, (h, pr) in enumerate(items):
        for half, p_half in enumerate(halves(p[i])):
            j = 2 * pr + half
            uw = rhs[h][c * j:c * (j + 1)] + jnp.dot(p_half, rhs_b[h][prow[pr]], preferred_element_type=F32)
            us[h][j] = uw[:, :GDN_DV]
            ws[h][j] = uw[:, GDN_DV:].astype(BF16)

    s = [st_ref[h] for h in heads]
    vprev = [None] * len(heads)
    for j in range(nch):
        rows = slice(c * j, c * (j + 1))
        sb = [s[h].astype(BF16) for h in heads]
        ws_qd = [jnp.dot(jnp.concatenate([ws[h][j], qd[h][rows]], axis=0), sb[h], preferred_element_type=F32)
                 for h in heads]
        vn16 = [(us[h][j] - ws_qd[h][0:c]).astype(BF16) for h in heads]
        o_inter = [ws_qd[h][c:2 * c] for h in heads]
        for h in heads:
            vpair = jnp.concatenate([vn16[h], jnp.zeros_like(vn16[h])] if j % 2 == 0 else [vprev[h], vn16[h]], axis=0)
            o = o_inter[h] + jnp.dot(attn[h * npair + j // 2][j % 2], vpair, preferred_element_type=F32)
            s[h] = jnp.exp(gam[h][c * (j + 1) - 1:c * (j + 1), :]) * s[h] + lax.dot_general(
                kd[h][rows], vn16[h], _TN, preferred_element_type=F32)
            o = o * lax.rsqrt(jnp.mean(o * o, axis=-1, keepdims=True) + RMS_EPS) * gnorm
            sq, hh = divmod(h, GDN_HEADS)
            gate = _silu(z_ref[sq, rows, 128 * hh:128 * (hh + 1)].astype(F32))
            o_ref[sq, rows, 128 * hh:128 * (hh + 1)] = (o * gate).astype(BF16)
        vprev = vn16
    for h in heads:
        st_ref[h] = s[h]


def _gdn(main, aux, alog_pad, dtb_pad, gdn_norm_g, *, bsz, seq, ts=512, nseq=2):
    nt = seq // ts
    hw = GDN_HEADS * GDN_DK
    assert bsz % nseq == 0
    main4 = main.reshape(bsz // nseq, nseq, seq, MAIN_W)
    aux4 = aux.reshape(bsz // nseq, nseq, seq, AUX_W)
    tok = lambda col: pl.BlockSpec((None, nseq, ts, hw), lambda b, t: (b, 0, t, col))
    out = pl.pallas_call(
        functools.partial(_gdn_kernel, ts=ts, nseq=nseq),
        grid=(bsz // nseq, nt),
        in_specs=[tok(3), tok(4), tok(5), tok(6),
                  pl.BlockSpec((None, nseq, ts, AUX_W), lambda b, t: (b, 0, t, 0)),
                  pl.BlockSpec((1, AUX_W), lambda b, t: (0, 0)),
                  pl.BlockSpec((1, AUX_W), lambda b, t: (0, 0)),
                  pl.BlockSpec((1, GDN_DV), lambda b, t: (0, 0))],
        out_specs=pl.BlockSpec((None, nseq, ts, GDN_WIDTH), lambda b, t: (b, 0, t, 0)),
        out_shape=jax.ShapeDtypeStruct((bsz // nseq, nseq, seq, GDN_WIDTH), BF16),
        scratch_shapes=[pltpu.VMEM((nseq * GDN_HEADS, GDN_DK, GDN_DV), F32)],
        compiler_params=_cparams("parallel", "arbitrary"),
        name="gdn",
    )(main4, main4, main4, main4, aux4, alog_pad, dtb_pad, gdn_norm_g)
    return out.reshape(bsz * seq, GDN_WIDTH)


def _pad_aux_row(v, offset):
    return jnp.zeros((1, AUX_W), F32).at[0, offset:offset + v.shape[0]].set(v.astype(F32))


ROUTE_W = 128
R_E1, R_E2, R_RANK1, R_RANK2, R_G1, R_G2 = range(6)
ROUTER_GROUP_COL = N_EXPERTS
POST_MIX_SPLIT = 4


def _layer_norm(h, g, b):
    mu = jnp.mean(h, axis=-1, keepdims=True)
    hc = h - mu
    var = jnp.mean(hc * hc, axis=-1, keepdims=True)
    return hc * lax.rsqrt(var + LN_EPS) * g + b


def _pack_bf16_pairs(x):
    w = x.shape[1] // 2
    lo = lax.bitcast_convert_type(x[:, :w].astype(BF16).astype(F32), jnp.uint32)
    hi = lax.bitcast_convert_type(x[:, w:].astype(BF16).astype(F32), jnp.uint32)
    return (lo >> 16) | (hi & jnp.uint32(0xFFFF0000))


def _unpack_bf16_pairs(p):
    lo = lax.bitcast_convert_type(p << 16, F32)
    hi = lax.bitcast_convert_type(p & jnp.uint32(0xFFFF0000), F32)
    return lo, hi


HALF = D_MODEL // 2
LANES = 128
TOK_ROWS = HALF // LANES


def _store_token_rows(ref, packed):
    m = packed.shape[0]
    for c in range(TOK_ROWS):
        ref[pl.ds(c, m, stride=TOK_ROWS), :] = packed[:, LANES * c:LANES * (c + 1)]


def _load_token_rows(ref, m):
    return jnp.concatenate([ref[pl.ds(c, m, stride=TOK_ROWS), :] for c in range(TOK_ROWS)], axis=1)


def _post_mix_kernel(oa_ref, ob_ref, x_ref, wo_ref, g_ref, b_ref, wr_ref, br_ref,
                     x1p_ref, route_ref, gates_ref, cnt_ref, carry_ref, *, tm):
    @pl.when(pl.program_id(0) == 0)
    def _():
        carry_ref[...] = jnp.zeros_like(carry_ref)

    m = tm // POST_MIX_SPLIT
    groups = range(POST_MIX_SPLIT)
    rs = [slice(h * m, (h + 1) * m) for h in groups]
    y = [jnp.dot(oa_ref[rs[h], :], wo_ref[0:GLA_WIDTH, :], preferred_element_type=F32)
         + jnp.dot(ob_ref[rs[h], :], wo_ref[GLA_WIDTH:, :], preferred_element_type=F32) for h in groups]
    x1 = [_layer_norm(ALPHA_DN * x_ref[rs[h], :] + y[h], g_ref[...], b_ref[...]) for h in groups]
    for h in groups:
        _store_token_rows(x1p_ref.at[pl.ds(h * m * TOK_ROWS, m * TOK_ROWS), :], _pack_bf16_pairs(x1[h]))

    xh = [x.astype(BF16) for x in x1]
    xl = [(x - xb.astype(F32)).astype(BF16) for x, xb in zip(x1, xh)]
    hw = [lax.dot_general(wr_ref[...], xb, _NT, preferred_element_type=F32) for xb in xh]
    lw = [lax.dot_general(wr_ref[0:ROUTE_W, :], xb, _NT, preferred_element_type=F32) for xb in xl]
    logits = [a[:ROUTE_W] + a[ROUTE_W:] + b + br_ref[...] for a, b in zip(hw, lw)]
    sub = lax.broadcasted_iota(jnp.int32, (ROUTE_W, m), 0)
    big = jnp.int32(1 << 20)
    neg = jnp.float32(-jnp.inf)

    def first_argmax(vals):
        mx = jnp.max(vals, axis=0, keepdims=True)
        idx = jnp.min(jnp.where(vals == mx, sub, big), axis=0, keepdims=True)
        return mx, idx

    is_group = (sub >= ROUTER_GROUP_COL) & (sub < ROUTER_GROUP_COL + N_GROUPS)
    gl = [jnp.where(is_group, lg, neg) for lg in logits]
    gtop = [first_argmax(v) for v in gl]
    p_group = [1.0 / jnp.sum(jnp.exp(v - t[0]), axis=0, keepdims=True) for v, t in zip(gl, gtop)]
    el = [jnp.where((sub // EXPERTS_PER_GROUP) == (t[1] - ROUTER_GROUP_COL), lg, neg) for lg, t in zip(logits, gtop)]
    top1 = [first_argmax(v) for v in el]
    top2 = [first_argmax(jnp.where(sub == t[1], neg, v)) for v, t in zip(el, top1)]
    ex = [jnp.exp(b[0] - a[0]) for a, b in zip(top1, top2)]
    w1 = [1.0 / (1.0 + e) for e in ex]
    gate1 = [p * w for p, w in zip(p_group, w1)]
    gate2 = [p * (e * w) for p, e, w in zip(p_group, ex, w1)]

    oh1 = [sub == t[1] for t in top1]
    oh2 = [sub == t[1] for t in top2]
    ri = lax.broadcasted_iota(jnp.int32, (m, m), 0)
    ci = lax.broadcasted_iota(jnp.int32, (m, m), 1)
    ustrict = (ri < ci).astype(BF16)
    oh1f = [o.astype(F32) for o in oh1]
    oh2f = [o.astype(F32) for o in oh2]
    c1 = [jnp.dot(o.astype(BF16), ustrict, preferred_element_type=F32) for o in oh1f]
    c2 = [jnp.dot(o.astype(BF16), ustrict, preferred_element_type=F32) for o in oh2f]
    tot1 = [jnp.sum(o, axis=1, keepdims=True) for o in oh1f]
    tot2 = [jnp.sum(o, axis=1, keepdims=True) for o in oh2f]
    carry = carry_ref[...]
    eye8 = jnp.eye(8, ROUTE_W, dtype=F32)
    zero = jnp.zeros((1, m), F32)
    for h in groups:
        rank1 = jnp.sum(jnp.where(oh1[h], c1[h] + carry, 0.0), axis=0, keepdims=True)
        rank2 = jnp.sum(jnp.where(oh2[h], c2[h] + (carry + tot1[h]), 0.0), axis=0, keepdims=True)
        carry = carry + tot1[h] + tot2[h]
        route_t = jnp.concatenate([top1[h][1].astype(F32), top2[h][1].astype(F32), rank1, rank2,
                                   gate1[h], gate2[h], zero, zero], axis=0)
        route_ref[:, rs[h]] = route_t
        gates_ref[rs[h], :] = lax.dot_general(route_t, eye8, _TN, precision=HIGHEST, preferred_element_type=F32)
    carry_ref[...] = carry
    cnt_ref[...] = carry


def _post_mix(o_a, o_b, x2d, w_out_b, ln_g, ln_b, w_router, b_router, *, tm=1024):
    n = x2d.shape[0]
    return pl.pallas_call(
        functools.partial(_post_mix_kernel, tm=tm),
        grid=(n // tm,),
        in_specs=[pl.BlockSpec((tm, GLA_WIDTH), lambda i: (i, 0)),
                  pl.BlockSpec((tm, GDN_WIDTH), lambda i: (i, 0)),
                  pl.BlockSpec((tm, D_MODEL), lambda i: (i, 0)),
                  pl.BlockSpec((GLA_WIDTH + GDN_WIDTH, D_MODEL), lambda i: (0, 0)),
                  pl.BlockSpec((1, D_MODEL), lambda i: (0, 0)),
                  pl.BlockSpec((1, D_MODEL), lambda i: (0, 0)),
                  pl.BlockSpec((2 * ROUTE_W, D_MODEL), lambda i: (0, 0)),
                  pl.BlockSpec((ROUTE_W, 1), lambda i: (0, 0))],
        out_specs=[pl.BlockSpec((tm * TOK_ROWS, LANES), lambda i: (i, 0)),
                   pl.BlockSpec((8, tm), lambda i: (0, i)),
                   pl.BlockSpec((tm, ROUTE_W), lambda i: (i, 0)),
                   pl.BlockSpec((ROUTE_W, 1), lambda i: (0, 0))],
        out_shape=[jax.ShapeDtypeStruct((n * TOK_ROWS, LANES), jnp.uint32),
                   jax.ShapeDtypeStruct((8, n), F32),
                   jax.ShapeDtypeStruct((n, ROUTE_W), F32),
                   jax.ShapeDtypeStruct((ROUTE_W, 1), F32)],
        scratch_shapes=[pltpu.VMEM((ROUTE_W, 1), F32)],
        compiler_params=_cparams("arbitrary"),
        name="post_mix",
    )(o_a, o_b, x2d, w_out_b, ln_g, ln_b, w_router, b_router)


EXPERT_BLOCK = 256
BLOCKS_PER_STEP = 2

ROW_UNROLL = 8


def _plan_kernel(route_ref, cnt_ref, dest_ref, *, tm, tb):
    counts = cnt_ref[...].astype(jnp.int32)
    padded = jnp.broadcast_to(((counts + (tb - 1)) & jnp.int32(-tb)).astype(F32), (ROUTE_W, ROUTE_W))
    pstart = (_chunk_cumsum(padded, ROUTE_W) - padded)[:, 0:1]
    route = route_ref[...]
    sub = lax.broadcasted_iota(jnp.int32, (ROUTE_W, tm), 0).astype(F32)
    rows = []
    for ce, cr in ((R_E1, R_RANK1), (R_E2, R_RANK2)):
        off = jnp.sum(jnp.where(sub == route[ce:ce + 1, :], pstart, 0.0), axis=0, keepdims=True)
        rows.append(off + route[cr:cr + 1, :])
    rows.extend([jnp.zeros((1, tm), F32)] * (8 - len(rows)))
    dest_ref[...] = jnp.concatenate(rows, axis=0).astype(jnp.int32) * TOK_ROWS


def _plan(route, counts, *, tb, tm=2048):
    n = route.shape[1]
    tm = min(tm, n)
    return pl.pallas_call(
        functools.partial(_plan_kernel, tm=tm, tb=tb),
        grid=(n // tm,),
        in_specs=[pl.BlockSpec((8, tm), lambda i: (0, i)),
                  pl.BlockSpec((ROUTE_W, 1), lambda i: (0, 0))],
        out_specs=pl.BlockSpec((8, tm), lambda i: (0, i)),
        out_shape=jax.ShapeDtypeStruct((8, n), jnp.int32),
        compiler_params=_cparams("parallel"),
        name="plan",
    )(route, counts)


def _dispatch_kernel(d1_ref, d2_ref, last_blk_ref, nvalid_ref, xp_ref, xs_ref, zbuf, sem, zsem, *, tm, tb, nb):
    base = pl.program_id(0) * tm
    rows = tb * TOK_ROWS

    @pl.when(pl.program_id(0) == 0)
    def _():
        zbuf[...] = jnp.zeros_like(zbuf)

        def zero_copy(blk):
            return pltpu.make_async_copy(zbuf, xs_ref.at[pl.ds(pl.multiple_of(blk * rows, rows), rows), :], zsem)

        def for_each_block(fn):
            def per_expert(e, carry):
                @pl.when(last_blk_ref[e] >= 0)
                def _():
                    fn(zero_copy(last_blk_ref[e]))
                return carry

            def per_tail(blk, carry):
                fn(zero_copy(blk))
                return carry
            lax.fori_loop(0, N_EXPERTS, per_expert, 0)
            lax.fori_loop(nvalid_ref[0], nb, per_tail, 0)

        for_each_block(lambda cp: cp.start())
        for_each_block(lambda cp: cp.wait())

    def body(j, carry):
        for u in range(ROW_UNROLL):
            r = j * ROW_UNROLL + u
            for k, d_ref in enumerate((d1_ref, d2_ref)):
                dst = pl.multiple_of(d_ref[base + r], TOK_ROWS)
                pltpu.make_async_copy(xp_ref.at[pl.ds(r * TOK_ROWS, TOK_ROWS), :],
                                      xs_ref.at[pl.ds(dst, TOK_ROWS), :], sem).start(priority=k)
        return carry
    lax.fori_loop(0, tm // ROW_UNROLL, body, 0)
    for _ in range(TOP_K):
        pltpu.make_async_copy(xp_ref, xs_ref.at[pl.ds(0, tm * TOK_ROWS), :], sem).wait()


def _dispatch(dest1, dest2, last_blk, nvalid, x1p, *, nb, tb=EXPERT_BLOCK, tm=1024):
    n = x1p.shape[0] // TOK_ROWS
    grid_spec = pltpu.PrefetchScalarGridSpec(
        num_scalar_prefetch=4,
        grid=(n // tm,),
        in_specs=[pl.BlockSpec((tm * TOK_ROWS, LANES), lambda i, a, b, lb, nv: (i, 0))],
        out_specs=pl.BlockSpec(memory_space=pl.ANY),
        scratch_shapes=[pltpu.VMEM((tb * TOK_ROWS, LANES), jnp.uint32),
                        pltpu.SemaphoreType.DMA(()),
                        pltpu.SemaphoreType.DMA(())],
    )
    return pl.pallas_call(
        functools.partial(_dispatch_kernel, tm=tm, tb=tb, nb=nb),
        grid_spec=grid_spec,
        out_shape=jax.ShapeDtypeStruct((nb * tb * TOK_ROWS, LANES), jnp.uint32),
        compiler_params=_cparams("arbitrary"),
        name="dispatch",
    )(dest1, dest2, last_blk, nvalid, x1p)


def _expert_kernel(blk_e_ref, first_ref, next_e_ref, nvalid_ref, xs_ref, wg_hbm, wu_hbm, wd_hbm, y_ref,
                   wgs, wus, wds, wgb, wub, wdb, sem, *, tb):
    rows = tb * TOK_ROWS

    def weight_copies(e):
        return [pltpu.make_async_copy(src.at[e], dst, sem.at[j])
                for j, (src, dst) in enumerate(((wg_hbm, wgs), (wu_hbm, wus), (wd_hbm, wds)))]

    @pl.when((pl.program_id(0) == 0) & (nvalid_ref[0] > 0))
    def _():
        for cp in weight_copies(blk_e_ref[0]):
            cp.start()

    def run(i, sub, nblk):
        @pl.when(first_ref[i] == 1)
        def _():
            for cp in weight_copies(blk_e_ref[i]):
                cp.wait()
            wgb[...] = wgs[...].astype(BF16)
            wub[...] = wus[...].astype(BF16)
            wdb[...] = wds[...].astype(BF16)
            mlp(sub, nblk)

            @pl.when(next_e_ref[i] >= 0)
            def _():
                for cp in weight_copies(next_e_ref[i]):
                    cp.start()

        @pl.when(first_ref[i] == 0)
        def _():
            mlp(sub, nblk)

    def mlp(sub, nblk):
        m = nblk * tb
        span = pl.ds(sub * rows, nblk * rows)
        lo, hi = _unpack_bf16_pairs(_load_token_rows(xs_ref.at[span, :], m))
        xb = jnp.concatenate([lo.astype(BF16), hi.astype(BF16)], axis=1)
        hg = jnp.dot(xb, wgb[...], preferred_element_type=F32)
        hu = jnp.dot(xb, wub[...], preferred_element_type=F32)
        hid = (_silu(hg) * hu).astype(BF16)
        _store_token_rows(y_ref.at[span, :], _pack_bf16_pairs(jnp.dot(hid, wdb[...], preferred_element_type=F32)))

    nvalid = nvalid_ref[0]
    for sub in range(0, BLOCKS_PER_STEP, 2):
        i0 = pl.program_id(0) * BLOCKS_PER_STEP + sub
        i1 = i0 + 1
        paired = (i1 < nvalid) & (first_ref[i1] == 0)

        @pl.when(paired)
        def _():
            run(i0, sub, 2)

        @pl.when(jnp.logical_not(paired))
        def _():
            @pl.when(i0 < nvalid)
            def _():
                run(i0, sub, 1)

            @pl.when(i1 < nvalid)
            def _():
                run(i1, sub + 1, 1)

        for s_, i in ((sub, i0), (sub + 1, i1)):
            @pl.when(i >= nvalid)
            def _():
                y_ref[pl.ds(s_ * rows, rows), :] = jnp.zeros((rows, LANES), jnp.uint32)


def _experts(blk_e, first, next_e, nvalid, xs, w_gate, w_up, w_down, *, tb=EXPERT_BLOCK):
    nb = blk_e.shape[0]
    assert nb % BLOCKS_PER_STEP == 0
    step_rows = BLOCKS_PER_STEP * tb * TOK_ROWS
    last_step = lambda nv: jnp.maximum((nv[0] - 1) // BLOCKS_PER_STEP, 0)
    grid_spec = pltpu.PrefetchScalarGridSpec(
        num_scalar_prefetch=4,
        grid=(nb // BLOCKS_PER_STEP,),
        in_specs=[pl.BlockSpec((step_rows, LANES), lambda s, be, fr, ne, nv: (jnp.minimum(s, last_step(nv)), 0)),
                  pl.BlockSpec(memory_space=pl.ANY),
                  pl.BlockSpec(memory_space=pl.ANY),
                  pl.BlockSpec(memory_space=pl.ANY)],
        out_specs=pl.BlockSpec((step_rows, LANES), lambda s, be, fr, ne, nv: (s, 0)),
        scratch_shapes=[pltpu.VMEM((D_MODEL, D_EXPERT), F32),
                        pltpu.VMEM((D_MODEL, D_EXPERT), F32),
                        pltpu.VMEM((D_EXPERT, D_MODEL), F32),
                        pltpu.VMEM((D_MODEL, D_EXPERT), BF16),
                        pltpu.VMEM((D_MODEL, D_EXPERT), BF16),
                        pltpu.VMEM((D_EXPERT, D_MODEL), BF16),
                        pltpu.SemaphoreType.DMA((3,))],
    )
    return pl.pallas_call(
        functools.partial(_expert_kernel, tb=tb),
        grid_spec=grid_spec,
        out_shape=jax.ShapeDtypeStruct((nb * tb * TOK_ROWS, LANES), jnp.uint32),
        compiler_params=_cparams("arbitrary"),
        name="experts",
    )(blk_e, first, next_e, nvalid, xs, w_gate, w_up, w_down)


def _combine_kernel(d1_ref, d2_ref, y_hbm, x1_ref, route_ref, g_ref, b_ref, o_ref, ybuf, sem, *, tt):
    i = pl.program_id(0)
    nsteps = pl.num_programs(0)
    slot = i % 2

    def issue(tile, slot_):
        def body(j, carry):
            for u in range(ROW_UNROLL):
                r = j * ROW_UNROLL + u
                for k, d_ref in enumerate((d1_ref, d2_ref)):
                    src = pl.multiple_of(d_ref[tile * tt + r], TOK_ROWS)
                    pltpu.make_async_copy(y_hbm.at[pl.ds(src, TOK_ROWS), :],
                                          ybuf.at[slot_, k, pl.ds(r * TOK_ROWS, TOK_ROWS), :],
                                          sem.at[slot_]).start(priority=k)
            return carry
        lax.fori_loop(0, tt // ROW_UNROLL, body, 0)

    @pl.when(i == 0)
    def _():
        issue(0, 0)

    @pl.when(i + 1 < nsteps)
    def _():
        issue(i + 1, 1 - slot)

    for k in range(TOP_K):
        pltpu.make_async_copy(y_hbm.at[pl.ds(0, tt * TOK_ROWS), :], ybuf.at[slot, k], sem.at[slot]).wait()
    route = route_ref[...]
    g1 = route[:, R_G1:R_G1 + 1]
    g2 = route[:, R_G2:R_G2 + 1]
    lo1, hi1 = _unpack_bf16_pairs(_load_token_rows(ybuf.at[slot, 0], tt))
    lo2, hi2 = _unpack_bf16_pairs(_load_token_rows(ybuf.at[slot, 1], tt))
    x_lo, x_hi = _unpack_bf16_pairs(_load_token_rows(x1_ref, tt))
    h_lo = ALPHA_DN * x_lo + (g1 * lo1 + g2 * lo2)
    h_hi = ALPHA_DN * x_hi + (g1 * hi1 + g2 * hi2)
    mu = (jnp.sum(h_lo, axis=-1, keepdims=True) + jnp.sum(h_hi, axis=-1, keepdims=True)) * (1.0 / D_MODEL)
    c_lo = h_lo - mu
    c_hi = h_hi - mu
    var = (jnp.sum(c_lo * c_lo, axis=-1, keepdims=True) + jnp.sum(c_hi * c_hi, axis=-1, keepdims=True)) * (1.0 / D_MODEL)
    inv = lax.rsqrt(var + LN_EPS)
    o_ref[:, 0:HALF] = c_lo * inv * g_ref[:, 0:HALF] + b_ref[:, 0:HALF]
    o_ref[:, HALF:] = c_hi * inv * g_ref[:, HALF:] + b_ref[:, HALF:]


def _combine(dest1, dest2, ybuf, x1p, route, ln_g, ln_b, *, tt=512):
    n = x1p.shape[0] // TOK_ROWS
    grid_spec = pltpu.PrefetchScalarGridSpec(
        num_scalar_prefetch=2,
        grid=(n // tt,),
        in_specs=[pl.BlockSpec(memory_space=pl.ANY),
                  pl.BlockSpec((tt * TOK_ROWS, LANES), lambda i, a, b: (i, 0)),
                  pl.BlockSpec((tt, ROUTE_W), lambda i, a, b: (i, 0)),
                  pl.BlockSpec((1, D_MODEL), lambda i, a, b: (0, 0)),
                  pl.BlockSpec((1, D_MODEL), lambda i, a, b: (0, 0))],
        out_specs=pl.BlockSpec((tt, D_MODEL), lambda i, a, b: (i, 0)),
        scratch_shapes=[pltpu.VMEM((2, TOP_K, tt * TOK_ROWS, LANES), jnp.uint32),
                        pltpu.SemaphoreType.DMA((2,))],
    )
    return pl.pallas_call(
        functools.partial(_combine_kernel, tt=tt),
        grid_spec=grid_spec,
        out_shape=jax.ShapeDtypeStruct((n, D_MODEL), F32),
        compiler_params=_cparams("arbitrary"),
        name="combine",
    )(dest1, dest2, ybuf, x1p, route, ln_g, ln_b)


def _block_plan(counts_row, n, tb):
    counts = counts_row[:N_EXPERTS, 0].astype(jnp.int32)
    pends = jnp.cumsum((counts + tb - 1) // tb * tb)
    nb = (n * TOP_K) // tb + N_EXPERTS
    nvalid = (pends[-1] // tb).astype(jnp.int32)
    blk = jnp.arange(nb, dtype=jnp.int32)
    blk_e = jnp.sum((pends[None, :] <= (blk * tb)[:, None]).astype(jnp.int32), axis=1)
    blk_e = jnp.minimum(blk_e, N_EXPERTS - 1)
    blk_e = jnp.where(blk < nvalid, blk_e, blk_e[jnp.maximum(nvalid - 1, 0)])
    first = jnp.concatenate([jnp.ones((1,), jnp.int32), (blk_e[1:] != blk_e[:-1]).astype(jnp.int32)])
    ids = jnp.arange(N_EXPERTS, dtype=jnp.int32)
    later = lax.cummin(jnp.where(counts > 0, ids, N_EXPERTS), axis=0, reverse=True)
    nxt = jnp.concatenate([later[1:], jnp.full((1,), N_EXPERTS, jnp.int32)])
    nxt = jnp.where(nxt < N_EXPERTS, nxt, -1)
    last_blk = jnp.where(counts > 0, pends // tb - 1, -1).astype(jnp.int32)
    return blk_e, first, nxt[blk_e], nvalid.reshape(1), last_blk


def kernel(x, w_in, w_gk_up, b_gk, conv_w, a_log, dt_bias, gla_norm_g, gdn_norm_g, w_out, ln1_g, ln1_b, w_router_group, b_router_group, w_router_expert, b_router_expert, w_gate, w_up, w_down, ln2_g, ln2_b):
    bsz, seq, d = x.shape
    n = bsz * seq
    x2d = x.reshape(n, d)
    w_main, w_aux = _split_w_in(w_in[0])
    main, aux = _in_proj(x2d, w_main, w_aux, conv_w[0].astype(F32), seq=seq)
    wgk = jnp.zeros((AUX_W, 256), F32).at[:GLA_GATE_RANK].set(w_gk_up[0])
    o_a = _gla(main, aux, wgk, b_gk[0][None], gla_norm_g[0][None], bsz=bsz, seq=seq)
    o_b = _gdn(main, aux, _pad_aux_row(a_log[0], AUX_AB), _pad_aux_row(dt_bias[0], AUX_AB),
               gdn_norm_g[0][None], bsz=bsz, seq=seq)
    rpad = jnp.zeros((ROUTE_W - N_EXPERTS - N_GROUPS, d), F32)
    w_router = jnp.concatenate([w_router_expert[0].T, w_router_group[0].T, rpad], axis=0)
    w_router_hi = w_router.astype(BF16)
    w_router = jnp.concatenate([w_router_hi, (w_router - w_router_hi.astype(F32)).astype(BF16)], axis=0)
    b_router = jnp.concatenate([b_router_expert[0], b_router_group[0], rpad[:, 0]])[:, None]
    x1p, route, gates, counts = _post_mix(o_a, o_b, x2d, w_out[0].astype(BF16), ln1_g[0][None], ln1_b[0][None],
                                          w_router, b_router)
    dest = _plan(route, counts, tb=EXPERT_BLOCK)
    blk_e, first, next_e, nvalid, last_blk = _block_plan(counts, n, EXPERT_BLOCK)
    xs = _dispatch(dest[0], dest[1], last_blk, nvalid, x1p, nb=blk_e.shape[0])
    ybuf = _experts(blk_e, first, next_e, nvalid, xs, w_gate[0], w_up[0], w_down[0])
    out = _combine(dest[0], dest[1], ybuf, x1p, gates, ln2_g[0][None], ln2_b[0][None])
    return out.reshape(bsz, seq, d)
```

```python
import functools

import jax
import jax.numpy as jnp
import numpy as np
from jax import lax
from jax.experimental import pallas as pl
from jax.experimental.pallas import tpu as pltpu

F32 = jnp.float32
BF16 = jnp.bfloat16
HIGHEST = lax.Precision.HIGHEST

D_MODEL = 1024
DEPTH = 1
GLA_HEADS = 4
GLA_DK = 64
GLA_DV = 128
GLA_WIDTH = GLA_HEADS * GLA_DV
GLA_GATE_RANK = 16
GLA_GATE_NORM = 16.0
GDN_HEADS = 4
GDN_DK = 128
GDN_DV = 128
GDN_WIDTH = GDN_HEADS * GDN_DV
CONV_W = 4
CHUNK = 64
N_GROUPS = 8
EXPERTS_PER_GROUP = 8
N_EXPERTS = N_GROUPS * EXPERTS_PER_GROUP
TOP_K = 2
D_EXPERT = 512
LN_EPS = 1e-5
RMS_EPS = 1e-6
ALPHA_DN = (2.0 * DEPTH) ** 0.25

MAIN_W = 2 * GLA_HEADS * GLA_DK + 2 * GLA_WIDTH + 4 * GDN_WIDTH
AUX_W = 128
AUX_LRA = 0
AUX_BB = GLA_GATE_RANK
AUX_AB = GLA_GATE_RANK + GDN_HEADS
SUB = 16
VMEM_LIMIT = 56 * 1024 * 1024


def _cparams(*sem):
    return pltpu.CompilerParams(dimension_semantics=sem, vmem_limit_bytes=VMEM_LIMIT)


def _sigmoid(x):
    return 1.0 / (1.0 + jnp.exp(-x))


def _silu(x):
    return x * _sigmoid(x)


def _log_sigmoid(x):
    return jnp.minimum(x, 0.0) - jnp.log(1.0 + jnp.exp(-jnp.abs(x)))


def _softplus(x):
    return jnp.maximum(x, 0.0) + jnp.log(1.0 + jnp.exp(-jnp.abs(x)))


def _chunk_cumsum(x, c):
    pos = lax.broadcasted_iota(jnp.int32, x.shape, 0) % c
    shift = 1
    while shift < c:
        x = x + jnp.where(pos >= shift, pltpu.roll(x, shift, 0), 0.0)
        shift *= 2
    return x


CONV_PAD = 8
GDN_HW = GDN_HEADS * GDN_DK
GDN_QKV_BLOCK0 = (2 * GLA_HEADS * GLA_DK + 2 * GLA_WIDTH) // GDN_HW


def _in_proj_kernel(x_ref, wm_ref, wa_ref, cw_ref, main_ref, aux_ref, stage_ref, *, tm, tiles_per_seq):
    xb = x_ref[...].astype(BF16)

    @pl.when((pl.program_id(0) % tiles_per_seq) == 0)
    def _():
        stage_ref[:, 0:CONV_PAD, :] = jnp.zeros((3 * GDN_HEADS, CONV_PAD, GDN_DK), F32)

    for j in range(MAIN_W // GDN_HW):
        sl = slice(j * GDN_HW, (j + 1) * GDN_HW)
        r = jnp.dot(xb, wm_ref[:, sl], preferred_element_type=F32)
        b = j - GDN_QKV_BLOCK0
        if not 0 <= b < 3:
            main_ref[:, sl] = r.astype(BF16)
            continue
        for h in range(GDN_HEADS):
            st = stage_ref.at[b * GDN_HEADS + h]
            cols = slice(b * GDN_HW + h * GDN_DK, b * GDN_HW + (h + 1) * GDN_DK)
            st[CONV_PAD:, :] = r[:, h * GDN_DK:(h + 1) * GDN_DK]
            acc = jnp.zeros((tm, GDN_DK), F32)
            for i in range(CONV_W):
                lo = CONV_PAD - (CONV_W - 1) + i
                acc = acc + st[lo:lo + tm, :] * cw_ref[i:i + 1, cols]
            st[0:CONV_PAD, :] = st[tm:tm + CONV_PAD, :]
            yh = _silu(acc)
            if b < 2:
                yh = yh * lax.rsqrt(jnp.sum(yh * yh, axis=-1, keepdims=True) + RMS_EPS)
            if b == 0:
                yh = yh * (GDN_DK ** -0.5)
            main_ref[:, j * GDN_HW + h * GDN_DK:j * GDN_HW + (h + 1) * GDN_DK] = yh.astype(BF16)
    aux_ref[...] = jnp.dot(xb, wa_ref[...], preferred_element_type=F32)


def _in_proj(x2d, w_main, w_aux, conv_w, *, seq, tm=512):
    n = x2d.shape[0]
    return pl.pallas_call(
        functools.partial(_in_proj_kernel, tm=tm, tiles_per_seq=seq // tm),
        grid=(n // tm,),
        in_specs=[pl.BlockSpec((tm, D_MODEL), lambda i: (i, 0)),
                  pl.BlockSpec((D_MODEL, MAIN_W), lambda i: (0, 0)),
                  pl.BlockSpec((D_MODEL, AUX_W), lambda i: (0, 0)),
                  pl.BlockSpec((CONV_W, 3 * GDN_HW), lambda i: (0, 0))],
        out_specs=[pl.BlockSpec((tm, MAIN_W), lambda i: (i, 0)),
                   pl.BlockSpec((tm, AUX_W), lambda i: (i, 0))],
        out_shape=[jax.ShapeDtypeStruct((n, MAIN_W), BF16),
                   jax.ShapeDtypeStruct((n, AUX_W), F32)],
        scratch_shapes=[pltpu.VMEM((3 * GDN_HEADS, tm + CONV_PAD, GDN_DK), F32)],
        compiler_params=_cparams("arbitrary"),
        name="in_proj",
    )(x2d, w_main, w_aux, conv_w)


def _split_w_in(w_in):
    sizes = (GLA_HEADS * GLA_DK, GLA_HEADS * GLA_DK, GLA_WIDTH, GLA_WIDTH, GLA_GATE_RANK,
             GDN_HEADS * GDN_DK, GDN_HEADS * GDN_DK, GDN_WIDTH, GDN_WIDTH, GDN_HEADS, GDN_HEADS)
    offs = np.cumsum((0,) + sizes)
    seg = [w_in[:, offs[i]:offs[i + 1]] for i in range(len(sizes))]
    qa, ka, va, ra, lra, qb, kb, vb, zb, bb, ab = seg
    w_main = jnp.concatenate([qa, ka, va, ra, qb, kb, vb, zb], axis=1).astype(BF16)
    pad = jnp.zeros((w_in.shape[0], AUX_W - GLA_GATE_RANK - 2 * GDN_HEADS), w_in.dtype)
    w_aux = jnp.concatenate([lra, bb, ab, pad], axis=1).astype(BF16)
    return w_main, w_aux


_NT = (((1,), (1,)), ((), ()))
_TN = (((0,), (0,)), ((), ()))


def _gla_kernel(q_ref, k_ref, v_ref, r_ref, aux_ref, wgk_ref, bgk_ref, g_ref, o_ref, st_ref, *, ts, nseq):
    npairs = 2 * nseq
    c = CHUNK
    nsub = c // SUB

    @pl.when(pl.program_id(1) == 0)
    def _():
        st_ref[...] = jnp.zeros_like(st_ref)

    gk_all = [_log_sigmoid(jnp.dot(aux_ref[sq], wgk_ref[...], precision=HIGHEST, preferred_element_type=F32)
                           + bgk_ref[...]) * (1.0 / GLA_GATE_NORM) for sq in range(nseq)]

    row128 = lax.broadcasted_iota(jnp.int32, (c, 128), 0)
    lane128 = lax.broadcasted_iota(jnp.int32, (1, 128), 1)
    lane_masks = [(lane128 // GLA_DK) == hh for hh in range(2)]
    sub_valid = [row128 < SUB * (i + 1) for i in range(nsub)]
    ar = lax.broadcasted_iota(jnp.int32, (c, nsub * c), 0)
    ac = lax.broadcasted_iota(jnp.int32, (c, nsub * c), 1)
    amask = ((ac // c) == (ar // SUB)) & ((ac % c) <= ar)
    scale = GLA_DK ** -0.5
    gnorm = g_ref[...]

    nch = ts // c
    rows = [slice(ch * c, (ch + 1) * c) for ch in range(nch)]
    b_cum = [_chunk_cumsum(g, c) for g in gk_all]

    cp = [(ch, p) for ch in range(nch) for p in range(npairs)]
    qt, qd, kd, kstack, dlast = {}, {}, {}, {}, {}
    for ch, p in cp:
        sq, pp = divmod(p, 2)
        lanes = slice(128 * pp, 128 * (pp + 1))
        bp = b_cum[sq][rows[ch], lanes]
        qp = q_ref[sq, rows[ch], lanes].astype(F32) * scale
        kp = k_ref[sq, rows[ch], lanes].astype(F32)
        c_row = jnp.concatenate(
            [jnp.broadcast_to(bp[SUB * i:SUB * i + 1, :], (SUB, 128)) for i in range(nsub)], axis=0)
        qt[ch, p] = (qp * jnp.exp(bp - c_row)).astype(BF16)
        qd[ch, p] = qp * jnp.exp(bp)
        b_last = bp[c - 1:c, :]
        dlast[ch, p] = jnp.exp(b_last)
        kd[ch, p] = kp * jnp.exp(b_last - bp)
        kts = []
        for i in range(nsub):
            e = jnp.exp(jnp.where(sub_valid[i], bp[SUB * i:SUB * i + 1, :] - bp, 0.0))
            kts.append(jnp.where(sub_valid[i], kp * e, 0.0))
        kstack[ch, p] = jnp.concatenate(kts, axis=0)

    cph = [(ch, p, hh) for ch, p in cp for hh in range(2)]
    vh = {(ch, p, hh): v_ref[p // 2, rows[ch], 128 * (2 * (p % 2) + hh):128 * (2 * (p % 2) + hh + 1)]
          for ch, p, hh in cph}
    r_mat = {(ch, p, hh): lax.dot_general(qt[ch, p], jnp.where(lane_masks[hh], kstack[ch, p], 0.0).astype(BF16),
                                          _NT, preferred_element_type=F32) for ch, p, hh in cph}
    o_intra = {k: jnp.dot(jnp.where(amask, r_mat[k], 0.0).astype(BF16), jnp.concatenate([vh[k]] * nsub, axis=0),
                          preferred_element_type=F32) for k in cph}
    upd = {(ch, p, hh): lax.dot_general(vh[ch, p, hh], jnp.where(lane_masks[hh], kd[ch, p], 0.0).astype(BF16), _TN,
                                        preferred_element_type=F32) for ch, p, hh in cph}

    st = [st_ref[p] for p in range(npairs)]
    for ch in range(nch):
        st_b = [s.astype(BF16) for s in st]
        o_pair = [lax.dot_general(
            jnp.concatenate([jnp.where(lane_masks[hh], qd[ch, p], 0.0).astype(BF16) for hh in range(2)], axis=0),
            st_b[p], _NT, preferred_element_type=F32) for p in range(npairs)]
        o_inter = {(p, hh): o_pair[p][hh * c:(hh + 1) * c] for p in range(npairs) for hh in range(2)}
        st = [st[p] * dlast[ch, p] + upd[ch, p, 0] + upd[ch, p, 1] for p in range(npairs)]
        for p in range(npairs):
            for hh in range(2):
                sq, h = p // 2, 2 * (p % 2) + hh
                o = o_intra[ch, p, hh] + o_inter[p, hh]
                o = o * lax.rsqrt(jnp.mean(o * o, axis=-1, keepdims=True) + RMS_EPS) * gnorm
                gate = _silu(r_ref[sq, rows[ch], 128 * h:128 * (h + 1)].astype(F32))
                o_ref[sq, rows[ch], 128 * h:128 * (h + 1)] = (o * gate).astype(BF16)
    for p in range(npairs):
        st_ref[p] = st[p]


def _gla(main, aux, wgk_pad, b_gk, gla_norm_g, *, bsz, seq, ts=512, nseq=2):
    nt = seq // ts
    assert bsz % nseq == 0
    main4 = main.reshape(bsz // nseq, nseq, seq, MAIN_W)
    aux4 = aux.reshape(bsz // nseq, nseq, seq, AUX_W)
    tok = lambda width, col: pl.BlockSpec((None, nseq, ts, width), lambda b, t: (b, 0, t, col))
    out = pl.pallas_call(
        functools.partial(_gla_kernel, ts=ts, nseq=nseq),
        grid=(bsz // nseq, nt),
        in_specs=[tok(256, 0), tok(256, 1), tok(512, 1), tok(512, 2),
                  tok(AUX_W, 0),
                  pl.BlockSpec((AUX_W, 256), lambda b, t: (0, 0)),
                  pl.BlockSpec((1, 256), lambda b, t: (0, 0)),
                  pl.BlockSpec((1, 128), lambda b, t: (0, 0))],
        out_specs=tok(GLA_WIDTH, 0),
        out_shape=jax.ShapeDtypeStruct((bsz // nseq, nseq, seq, GLA_WIDTH), BF16),
        scratch_shapes=[pltpu.VMEM((2 * nseq, 128, 128), F32)],
        compiler_params=_cparams("parallel", "arbitrary"),
        name="gla",
    )(main4, main4, main4, main4, aux4, wgk_pad, b_gk, gla_norm_g)
    return out.reshape(bsz * seq, GLA_WIDTH)


def _gdn_kernel(q_ref, k_ref, v_ref, z_ref, aux_ref, alog_ref, dtb_ref, g_ref, o_ref, st_ref, *, ts, nseq):
    c = CHUNK
    nch = ts // c

    @pl.when(pl.program_id(1) == 0)
    def _():
        st_ref[...] = jnp.zeros_like(st_ref)

    beta_full, gam_full, gam_t = [], [], []
    for sq in range(nseq):
        aux = aux_ref[sq]
        beta_full.append(_sigmoid(aux))
        g_full = -jnp.exp(alog_ref[...]) * _softplus(aux + dtb_ref[...])
        gam_full.append(_chunk_cumsum(g_full, c))
        gam_t.append(gam_full[-1].T)
    gnorm = g_ref[...]

    row = lax.broadcasted_iota(jnp.int32, (c, 2 * c), 0)
    lane = lax.broadcasted_iota(jnp.int32, (c, 2 * c), 1)
    left = lane < c
    col = lane % c
    causal = col <= row
    strict = col < row
    same32 = (row // 32) == (col // 32)
    same16 = (row // 16) == (col // 16)
    lvl0 = same16 & strict
    lvl1 = same32 & jnp.logical_not(same16) & strict
    lvl2 = jnp.logical_not(same32) & strict

    def pack_rows(x2):
        return jnp.where(left, x2[0:c], x2[c:2 * c])

    def pack_cols(x1):
        return jnp.where(left, x1[0:c], x1[c:2 * c])

    def halves(x):
        return jnp.where(left, x, 0.0).astype(BF16), jnp.where(left, 0.0, x).astype(BF16)

    def mmp(x, y):
        return jnp.dot(x.astype(BF16), jnp.concatenate(halves(y), axis=0), preferred_element_type=F32)

    heads = range(nseq * GDN_HEADS)
    npair = nch // 2
    items = [(h, pr) for h in heads for pr in range(npair)]
    prow = [slice(2 * c * pr, 2 * c * (pr + 1)) for pr in range(npair)]

    qn, kn, beta, gam, gam_r, kb, rhs, rhs_b, qd, kd = ([None] * len(heads) for _ in range(10))
    for h in heads:
        sq, hh = divmod(h, GDN_HEADS)
        hl = slice(128 * hh, 128 * (hh + 1))
        kb[h] = k_ref[sq, :, hl]
        qn[h] = q_ref[sq, :, hl].astype(F32)
        kn[h] = kb[h].astype(F32)
        vh = v_ref[sq, :, hl].astype(F32)
        beta[h] = beta_full[sq][:, AUX_BB + hh:AUX_BB + hh + 1]
        gam[h] = gam_full[sq][:, AUX_AB + hh:AUX_AB + hh + 1]
        gam_r[h] = gam_t[sq][AUX_AB + hh:AUX_AB + hh + 1, :]
        egam = jnp.exp(gam[h])
        rhs[h] = jnp.concatenate([beta[h] * vh, beta[h] * egam * kn[h]], axis=1)
        rhs_b[h] = rhs[h].astype(BF16)
        qd[h] = (qn[h] * egam).astype(BF16)
        gl_rows = jnp.concatenate(
            [jnp.broadcast_to(gam[h][c * (j + 1) - 1:c * (j + 1), :], (c, 1)) for j in range(nch)], axis=0)
        kd[h] = (kn[h] * jnp.exp(gl_rows - gam[h])).astype(BF16)

    kk = [pack_rows(lax.dot_general(kb[h][prow[pr]], kb[h][prow[pr]], _NT, preferred_element_type=F32))
          for h, pr in items]
    qk = [pack_rows(lax.dot_general(qn[h][prow[pr]].astype(BF16), kb[h][prow[pr]], _NT,
                                    preferred_element_type=F32)) for h, pr in items]
    dec = [jnp.exp(jnp.where(causal, pack_cols(gam[h][prow[pr]]) - gam_r[h][:, prow[pr]], 0.0)) for h, pr in items]
    a = [pack_cols(beta[h][prow[pr]]) * kk[i] * dec[i] for i, (h, pr) in enumerate(items)]
    attn = [halves(jnp.where(causal, qk[i] * dec[i], 0.0)) for i in range(len(items))]

    x1 = [jnp.where(lvl0, -ai, 0.0) for ai in a]
    x2 = [mmp(x, x) for x in x1]
    y = [mmp(x, xx) for x, xx in zip(x1, x2)]
    p = [x + xx + yy for x, xx, yy in zip(x1, x2, y)]
    xk = x2
    for _ in range(2):
        xk = [mmp(x, x) for x in xk]
        y = [mmp(pp, x) for pp, x in zip(p, xk)]
        p = [pp + x + yy for pp, x, yy in zip(p, xk, y)]
    for lvl in (lvl1, lvl2):
        low = [jnp.where(lvl, ai, 0.0) for ai in a]
        y = [mmp(pp, lo) for pp, lo in zip(p, low)]
        zz = [lo + yy for lo, yy in zip(low, y)]
        y = [mmp(z, pp) for z, pp in zip(zz, p)]
        p = [pp - z - yy for pp, z, yy in zip(p, zz, y)]

    us = [[None] * nch for _ in heads]
    ws = [[None] * nch for _ in heads]
    for i, (h, pr) in enumerate(items):
        for half, p_half in enumerate(halves(p[i])):
            j = 2 * pr + half
            uw = rhs[h][c * j:c * (j + 1)] + jnp.dot(p_half, rhs_b[h][prow[pr]], preferred_element_type=F32)
            us[h][j] = uw[:, :GDN_DV]
            ws[h][j] = uw[:, GDN_DV:].astype(BF16)

    s = [st_ref[h] for h in heads]
    vprev = [None] * len(heads)
    for j in range(nch):
        rows = slice(c * j, c * (j + 1))
        sb = [s[h].astype(BF16) for h in heads]
        ws_qd = [jnp.dot(jnp.concatenate([ws[h][j], qd[h][rows]], axis=0), sb[h], preferred_element_type=F32)
                 for h in heads]
        vn16 = [(us[h][j] - ws_qd[h][0:c]).astype(BF16) for h in heads]
        o_inter = [ws_qd[h][c:2 * c] for h in heads]
        for h in heads:
            vpair = jnp.concatenate([vn16[h], jnp.zeros_like(vn16[h])] if j % 2 == 0 else [vprev[h], vn16[h]], axis=0)
            o = o_inter[h] + jnp.dot(attn[h * npair + j // 2][j % 2], vpair, preferred_element_type=F32)
            s[h] = jnp.exp(gam[h][c * (j + 1) - 1:c * (j + 1), :]) * s[h] + lax.dot_general(
                kd[h][rows], vn16[h], _TN, preferred_element_type=F32)
            o = o * lax.rsqrt(jnp.mean(o * o, axis=-1, keepdims=True) + RMS_EPS) * gnorm
            sq, hh = divmod(h, GDN_HEADS)
            gate = _silu(z_ref[sq, rows, 128 * hh:128 * (hh + 1)].astype(F32))
            o_ref[sq, rows, 128 * hh:128 * (hh + 1)] = (o * gate).astype(BF16)
        vprev = vn16
    for h in heads:
        st_ref[h] = s[h]


def _gdn(main, aux, alog_pad, dtb_pad, gdn_norm_g, *, bsz, seq, ts=512, nseq=2):
    nt = seq // ts
    hw = GDN_HEADS * GDN_DK
    assert bsz % nseq == 0
    main4 = main.reshape(bsz // nseq, nseq, seq, MAIN_W)
    aux4 = aux.reshape(bsz // nseq, nseq, seq, AUX_W)
    tok = lambda col: pl.BlockSpec((None, nseq, ts, hw), lambda b, t: (b, 0, t, col))
    out = pl.pallas_call(
        functools.partial(_gdn_kernel, ts=ts, nseq=nseq),
        grid=(bsz // nseq, nt),
        in_specs=[tok(3), tok(4), tok(5), tok(6),
                  pl.BlockSpec((None, nseq, ts, AUX_W), lambda b, t: (b, 0, t, 0)),
                  pl.BlockSpec((1, AUX_W), lambda b, t: (0, 0)),
                  pl.BlockSpec((1, AUX_W), lambda b, t: (0, 0)),
                  pl.BlockSpec((1, GDN_DV), lambda b, t: (0, 0))],
        out_specs=pl.BlockSpec((None, nseq, ts, GDN_WIDTH), lambda b, t: (b, 0, t, 0)),
        out_shape=jax.ShapeDtypeStruct((bsz // nseq, nseq, seq, GDN_WIDTH), BF16),
        scratch_shapes=[pltpu.VMEM((nseq * GDN_HEADS, GDN_DK, GDN_DV), F32)],
        compiler_params=_cparams("parallel", "arbitrary"),
        name="gdn",
    )(main4, main4, main4, main4, aux4, alog_pad, dtb_pad, gdn_norm_g)
    return out.reshape(bsz * seq, GDN_WIDTH)


def _pad_aux_row(v, offset):
    return jnp.zeros((1, AUX_W), F32).at[0, offset:offset + v.shape[0]].set(v.astype(F32))


ROUTE_W = 128
R_E1, R_E2, R_RANK1, R_RANK2, R_G1, R_G2 = range(6)
ROUTER_GROUP_COL = N_EXPERTS
POST_MIX_SPLIT = 4


def _layer_norm(h, g, b):
    mu = jnp.mean(h, axis=-1, keepdims=True)
    hc = h - mu
    var = jnp.mean(hc * hc, axis=-1, keepdims=True)
    return hc * lax.rsqrt(var + LN_EPS) * g + b


def _pack_bf16_pairs(x):
    w = x.shape[1] // 2
    lo = lax.bitcast_convert_type(x[:, :w].astype(BF16).astype(F32), jnp.uint32)
    hi = lax.bitcast_convert_type(x[:, w:].astype(BF16).astype(F32), jnp.uint32)
    return (lo >> 16) | (hi & jnp.uint32(0xFFFF0000))


def _unpack_bf16_pairs(p):
    lo = lax.bitcast_convert_type(p << 16, F32)
    hi = lax.bitcast_convert_type(p & jnp.uint32(0xFFFF0000), F32)
    return lo, hi


HALF = D_MODEL // 2
LANES = 128
TOK_ROWS = HALF // LANES


def _store_token_rows(ref, packed):
    m = packed.shape[0]
    for c in range(TOK_ROWS):
        ref[pl.ds(c, m, stride=TOK_ROWS), :] = packed[:, LANES * c:LANES * (c + 1)]


def _load_token_rows(ref, m):
    return jnp.concatenate([ref[pl.ds(c, m, stride=TOK_ROWS), :] for c in range(TOK_ROWS)], axis=1)


def _post_mix_kernel(oa_ref, ob_ref, x_ref, wo_ref, g_ref, b_ref, wr_ref, br_ref,
                     x1p_ref, route_ref, gates_ref, cnt_ref, carry_ref, *, tm):
    @pl.when(pl.program_id(0) == 0)
    def _():
        carry_ref[...] = jnp.zeros_like(carry_ref)

    m = tm // POST_MIX_SPLIT
    groups = range(POST_MIX_SPLIT)
    rs = [slice(h * m, (h + 1) * m) for h in groups]
    y = [jnp.dot(oa_ref[rs[h], :], wo_ref[0:GLA_WIDTH, :], preferred_element_type=F32)
         + jnp.dot(ob_ref[rs[h], :], wo_ref[GLA_WIDTH:, :], preferred_element_type=F32) for h in groups]
    x1 = [_layer_norm(ALPHA_DN * x_ref[rs[h], :] + y[h], g_ref[...], b_ref[...]) for h in groups]
    for h in groups:
        _store_token_rows(x1p_ref.at[pl.ds(h * m * TOK_ROWS, m * TOK_ROWS), :], _pack_bf16_pairs(x1[h]))

    xh = [x.astype(BF16) for x in x1]
    xl = [(x - xb.astype(F32)).astype(BF16) for x, xb in zip(x1, xh)]
    hw = [lax.dot_general(wr_ref[...], xb, _NT, preferred_element_type=F32) for xb in xh]
    lw = [lax.dot_general(wr_ref[0:ROUTE_W, :], xb, _NT, preferred_element_type=F32) for xb in xl]
    logits = [a[:ROUTE_W] + a[ROUTE_W:] + b + br_ref[...] for a, b in zip(hw, lw)]
    sub = lax.broadcasted_iota(jnp.int32, (ROUTE_W, m), 0)
    big = jnp.int32(1 << 20)
    neg = jnp.float32(-jnp.inf)

    def first_argmax(vals):
        mx = jnp.max(vals, axis=0, keepdims=True)
        idx = jnp.min(jnp.where(vals == mx, sub, big), axis=0, keepdims=True)
        return mx, idx

    is_group = (sub >= ROUTER_GROUP_COL) & (sub < ROUTER_GROUP_COL + N_GROUPS)
    gl = [jnp.where(is_group, lg, neg) for lg in logits]
    gtop = [first_argmax(v) for v in gl]
    p_group = [1.0 / jnp.sum(jnp.exp(v - t[0]), axis=0, keepdims=True) for v, t in zip(gl, gtop)]
    el = [jnp.where((sub // EXPERTS_PER_GROUP) == (t[1] - ROUTER_GROUP_COL), lg, neg) for lg, t in zip(logits, gtop)]
    top1 = [first_argmax(v) for v in el]
    top2 = [first_argmax(jnp.where(sub == t[1], neg, v)) for v, t in zip(el, top1)]
    ex = [jnp.exp(b[0] - a[0]) for a, b in zip(top1, top2)]
    w1 = [1.0 / (1.0 + e) for e in ex]
    gate1 = [p * w for p, w in zip(p_group, w1)]
    gate2 = [p * (e * w) for p, e, w in zip(p_group, ex, w1)]

    oh1 = [sub == t[1] for t in top1]
    oh2 = [sub == t[1] for t in top2]
    ri = lax.broadcasted_iota(jnp.int32, (m, m), 0)
    ci = lax.broadcasted_iota(jnp.int32, (m, m), 1)
    ustrict = (ri < ci).astype(BF16)
    oh1f = [o.astype(F32) for o in oh1]
    oh2f = [o.astype(F32) for o in oh2]
    c1 = [jnp.dot(o.astype(BF16), ustrict, preferred_element_type=F32) for o in oh1f]
    c2 = [jnp.dot(o.astype(BF16), ustrict, preferred_element_type=F32) for o in oh2f]
    tot1 = [jnp.sum(o, axis=1, keepdims=True) for o in oh1f]
    tot2 = [jnp.sum(o, axis=1, keepdims=True) for o in oh2f]
    carry = carry_ref[...]
    eye8 = jnp.eye(8, ROUTE_W, dtype=F32)
    zero = jnp.zeros((1, m), F32)
    for h in groups:
        rank1 = jnp.sum(jnp.where(oh1[h], c1[h] + carry, 0.0), axis=0, keepdims=True)
        rank2 = jnp.sum(jnp.where(oh2[h], c2[h] + (carry + tot1[h]), 0.0), axis=0, keepdims=True)
        carry = carry + tot1[h] + tot2[h]
        route_t = jnp.concatenate([top1[h][1].astype(F32), top2[h][1].astype(F32), rank1, rank2,
                                   gate1[h], gate2[h], zero, zero], axis=0)
        route_ref[:, rs[h]] = route_t
        gates_ref[rs[h], :] = lax.dot_general(route_t, eye8, _TN, precision=HIGHEST, preferred_element_type=F32)
    carry_ref[...] = carry
    cnt_ref[...] = carry


def _post_mix(o_a, o_b, x2d, w_out_b, ln_g, ln_b, w_router, b_router, *, tm=1024):
    n = x2d.shape[0]
    return pl.pallas_call(
        functools.partial(_post_mix_kernel, tm=tm),
        grid=(n // tm,),
        in_specs=[pl.BlockSpec((tm, GLA_WIDTH), lambda i: (i, 0)),
                  pl.BlockSpec((tm, GDN_WIDTH), lambda i: (i, 0)),
                  pl.BlockSpec((tm, D_MODEL), lambda i: (i, 0)),
                  pl.BlockSpec((GLA_WIDTH + GDN_WIDTH, D_MODEL), lambda i: (0, 0)),
                  pl.BlockSpec((1, D_MODEL), lambda i: (0, 0)),
                  pl.BlockSpec((1, D_MODEL), lambda i: (0, 0)),
                  pl.BlockSpec((2 * ROUTE_W, D_MODEL), lambda i: (0, 0)),
                  pl.BlockSpec((ROUTE_W, 1), lambda i: (0, 0))],
        out_specs=[pl.BlockSpec((tm * TOK_ROWS, LANES), lambda i: (i, 0)),
                   pl.BlockSpec((8, tm), lambda i: (0, i)),
                   pl.BlockSpec((tm, ROUTE_W), lambda i: (i, 0)),
                   pl.BlockSpec((ROUTE_W, 1), lambda i: (0, 0))],
        out_shape=[jax.ShapeDtypeStruct((n * TOK_ROWS, LANES), jnp.uint32),
                   jax.ShapeDtypeStruct((8, n), F32),
                   jax.ShapeDtypeStruct((n, ROUTE_W), F32),
                   jax.ShapeDtypeStruct((ROUTE_W, 1), F32)],
        scratch_shapes=[pltpu.VMEM((ROUTE_W, 1), F32)],
        compiler_params=_cparams("arbitrary"),
        name="post_mix",
    )(o_a, o_b, x2d, w_out_b, ln_g, ln_b, w_router, b_router)


EXPERT_BLOCK = 256
BLOCKS_PER_STEP = 4

ROW_UNROLL = 8


def _plan_kernel(route_ref, cnt_ref, dest_ref, *, tm, tb):
    counts = cnt_ref[...].astype(jnp.int32)
    padded = jnp.broadcast_to(((counts + (tb - 1)) & jnp.int32(-tb)).astype(F32), (ROUTE_W, ROUTE_W))
    pstart = (_chunk_cumsum(padded, ROUTE_W) - padded)[:, 0:1]
    route = route_ref[...]
    sub = lax.broadcasted_iota(jnp.int32, (ROUTE_W, tm), 0).astype(F32)
    rows = []
    for ce, cr in ((R_E1, R_RANK1), (R_E2, R_RANK2)):
        off = jnp.sum(jnp.where(sub == route[ce:ce + 1, :], pstart, 0.0), axis=0, keepdims=True)
        rows.append(off + route[cr:cr + 1, :])
    rows.extend([jnp.zeros((1, tm), F32)] * (8 - len(rows)))
    dest_ref[...] = jnp.concatenate(rows, axis=0).astype(jnp.int32) * TOK_ROWS


def _plan(route, counts, *, tb, tm=2048):
    n = route.shape[1]
    tm = min(tm, n)
    return pl.pallas_call(
        functools.partial(_plan_kernel, tm=tm, tb=tb),
        grid=(n // tm,),
        in_specs=[pl.BlockSpec((8, tm), lambda i: (0, i)),
                  pl.BlockSpec((ROUTE_W, 1), lambda i: (0, 0))],
        out_specs=pl.BlockSpec((8, tm), lambda i: (0, i)),
        out_shape=jax.ShapeDtypeStruct((8, n), jnp.int32),
        compiler_params=_cparams("parallel"),
        name="plan",
    )(route, counts)


def _dispatch_kernel(d1_ref, d2_ref, last_blk_ref, nvalid_ref, xp_ref, xs_ref, zbuf, sem, zsem, *, tm, tb, nb):
    base = pl.program_id(0) * tm
    rows = tb * TOK_ROWS

    @pl.when(pl.program_id(0) == 0)
    def _():
        zbuf[...] = jnp.zeros_like(zbuf)

        def zero_copy(blk):
            return pltpu.make_async_copy(zbuf, xs_ref.at[pl.ds(pl.multiple_of(blk * rows, rows), rows), :], zsem)

        def for_each_block(fn):
            def per_expert(e, carry):
                @pl.when(last_blk_ref[e] >= 0)
                def _():
                    fn(zero_copy(last_blk_ref[e]))
                return carry

            def per_tail(blk, carry):
                fn(zero_copy(blk))
                return carry
            lax.fori_loop(0, N_EXPERTS, per_expert, 0)
            lax.fori_loop(nvalid_ref[0], nb, per_tail, 0)

        for_each_block(lambda cp: cp.start())
        for_each_block(lambda cp: cp.wait())

    def body(j, carry):
        for u in range(ROW_UNROLL):
            r = j * ROW_UNROLL + u
            for k, d_ref in enumerate((d1_ref, d2_ref)):
                dst = pl.multiple_of(d_ref[base + r], TOK_ROWS)
                pltpu.make_async_copy(xp_ref.at[pl.ds(r * TOK_ROWS, TOK_ROWS), :],
                                      xs_ref.at[pl.ds(dst, TOK_ROWS), :], sem).start(priority=k)
        return carry
    lax.fori_loop(0, tm // ROW_UNROLL, body, 0)
    for _ in range(TOP_K):
        pltpu.make_async_copy(xp_ref, xs_ref.at[pl.ds(0, tm * TOK_ROWS), :], sem).wait()


def _dispatch(dest1, dest2, last_blk, nvalid, x1p, *, nb, tb=EXPERT_BLOCK, tm=1024):
    n = x1p.shape[0] // TOK_ROWS
    grid_spec = pltpu.PrefetchScalarGridSpec(
        num_scalar_prefetch=4,
        grid=(n // tm,),
        in_specs=[pl.BlockSpec((tm * TOK_ROWS, LANES), lambda i, a, b, lb, nv: (i, 0))],
        out_specs=pl.BlockSpec(memory_space=pl.ANY),
        scratch_shapes=[pltpu.VMEM((tb * TOK_ROWS, LANES), jnp.uint32),
                        pltpu.SemaphoreType.DMA(()),
                        pltpu.SemaphoreType.DMA(())],
    )
    return pl.pallas_call(
        functools.partial(_dispatch_kernel, tm=tm, tb=tb, nb=nb),
        grid_spec=grid_spec,
        out_shape=jax.ShapeDtypeStruct((nb * tb * TOK_ROWS, LANES), jnp.uint32),
        compiler_params=_cparams("arbitrary"),
        name="dispatch",
    )(dest1, dest2, last_blk, nvalid, x1p)


def _expert_kernel(blk_e_ref, first_ref, next_e_ref, nvalid_ref, xs_ref, wg_hbm, wu_hbm, wd_hbm, y_ref,
                   wgs, wus, wds, wgb, wub, wdb, sem, *, tb):
    rows = tb * TOK_ROWS

    def weight_copies(e, slot):
        return [pltpu.make_async_copy(src.at[e], dst.at[slot], sem.at[slot, j])
                for j, (src, dst) in enumerate(((wg_hbm, wgs), (wu_hbm, wus), (wd_hbm, wds)))]

    @pl.when((pl.program_id(0) == 0) & (nvalid_ref[0] > 0))
    def _():
        for cp in weight_copies(blk_e_ref[0], first_ref[0] - 1):
            cp.start()

    def run(i, sub, nblk):
        @pl.when(first_ref[i] > 0)
        def _():
            slot = first_ref[i] - 1

            @pl.when(next_e_ref[i] >= 0)
            def _():
                for cp in weight_copies(next_e_ref[i], 1 - slot):
                    cp.start()

            for cp in weight_copies(blk_e_ref[i], slot):
                cp.wait()
            wgb[...] = wgs[slot].astype(BF16)
            wub[...] = wus[slot].astype(BF16)
            wdb[...] = wds[slot].astype(BF16)
            mlp(sub, nblk)

        @pl.when(first_ref[i] == 0)
        def _():
            mlp(sub, nblk)

    def mlp(sub, nblk):
        m = nblk * tb
        span = pl.ds(sub * rows, nblk * rows)
        lo, hi = _unpack_bf16_pairs(_load_token_rows(xs_ref.at[span, :], m))
        xb = jnp.concatenate([lo.astype(BF16), hi.astype(BF16)], axis=1)
        hg = jnp.dot(xb, wgb[...], preferred_element_type=F32)
        hu = jnp.dot(xb, wub[...], preferred_element_type=F32)
        hid = (_silu(hg) * hu).astype(BF16)
        _store_token_rows(y_ref.at[span, :], _pack_bf16_pairs(jnp.dot(hid, wdb[...], preferred_element_type=F32)))

    nvalid = nvalid_ref[0]
    for sub in range(0, BLOCKS_PER_STEP, 2):
        i0 = pl.program_id(0) * BLOCKS_PER_STEP + sub
        i1 = i0 + 1
        paired = (i1 < nvalid) & (first_ref[i1] == 0)

        @pl.when(paired)
        def _():
            run(i0, sub, 2)

        @pl.when(jnp.logical_not(paired))
        def _():
            @pl.when(i0 < nvalid)
            def _():
                run(i0, sub, 1)

            @pl.when(i1 < nvalid)
            def _():
                run(i1, sub + 1, 1)

        for s_, i in ((sub, i0), (sub + 1, i1)):
            @pl.when(i >= nvalid)
            def _():
                y_ref[pl.ds(s_ * rows, rows), :] = jnp.zeros((rows, LANES), jnp.uint32)


def _experts(blk_e, first, next_e, nvalid, xs, w_gate, w_up, w_down, *, tb=EXPERT_BLOCK):
    nb = blk_e.shape[0]
    assert nb % BLOCKS_PER_STEP == 0
    step_rows = BLOCKS_PER_STEP * tb * TOK_ROWS
    last_step = lambda nv: jnp.maximum((nv[0] - 1) // BLOCKS_PER_STEP, 0)
    grid_spec = pltpu.PrefetchScalarGridSpec(
        num_scalar_prefetch=4,
        grid=(nb // BLOCKS_PER_STEP,),
        in_specs=[pl.BlockSpec((step_rows, LANES), lambda s, be, fr, ne, nv: (jnp.minimum(s, last_step(nv)), 0)),
                  pl.BlockSpec(memory_space=pl.ANY),
                  pl.BlockSpec(memory_space=pl.ANY),
                  pl.BlockSpec(memory_space=pl.ANY)],
        out_specs=pl.BlockSpec((step_rows, LANES), lambda s, be, fr, ne, nv: (s, 0)),
        scratch_shapes=[pltpu.VMEM((2, D_MODEL, D_EXPERT), F32),
                        pltpu.VMEM((2, D_MODEL, D_EXPERT), F32),
                        pltpu.VMEM((2, D_EXPERT, D_MODEL), F32),
                        pltpu.VMEM((D_MODEL, D_EXPERT), BF16),
                        pltpu.VMEM((D_MODEL, D_EXPERT), BF16),
                        pltpu.VMEM((D_EXPERT, D_MODEL), BF16),
                        pltpu.SemaphoreType.DMA((2, 3))],
    )
    return pl.pallas_call(
        functools.partial(_expert_kernel, tb=tb),
        grid_spec=grid_spec,
        out_shape=jax.ShapeDtypeStruct((nb * tb * TOK_ROWS, LANES), jnp.uint32),
        compiler_params=_cparams("arbitrary"),
        name="experts",
    )(blk_e, first, next_e, nvalid, xs, w_gate, w_up, w_down)


def _combine_kernel(d1_ref, d2_ref, y_hbm, x1_ref, route_ref, g_ref, b_ref, o_ref, ybuf, sem, *, tt):
    i = pl.program_id(0)
    nsteps = pl.num_programs(0)
    slot = i % 2

    def issue(tile, slot_):
        def body(j, carry):
            for u in range(ROW_UNROLL):
                r = j * ROW_UNROLL + u
                for k, d_ref in enumerate((d1_ref, d2_ref)):
                    src = pl.multiple_of(d_ref[tile * tt + r], TOK_ROWS)
                    pltpu.make_async_copy(y_hbm.at[pl.ds(src, TOK_ROWS), :],
                                          ybuf.at[slot_, k, pl.ds(r * TOK_ROWS, TOK_ROWS), :],
                                          sem.at[slot_]).start(priority=k)
            return carry
        lax.fori_loop(0, tt // ROW_UNROLL, body, 0)

    @pl.when(i == 0)
    def _():
        issue(0, 0)

    @pl.when(i + 1 < nsteps)
    def _():
        issue(i + 1, 1 - slot)

    for k in range(TOP_K):
        pltpu.make_async_copy(y_hbm.at[pl.ds(0, tt * TOK_ROWS), :], ybuf.at[slot, k], sem.at[slot]).wait()
    route = route_ref[...]
    g1 = route[:, R_G1:R_G1 + 1]
    g2 = route[:, R_G2:R_G2 + 1]
    lo1, hi1 = _unpack_bf16_pairs(_load_token_rows(ybuf.at[slot, 0], tt))
    lo2, hi2 = _unpack_bf16_pairs(_load_token_rows(ybuf.at[slot, 1], tt))
    x_lo, x_hi = _unpack_bf16_pairs(_load_token_rows(x1_ref, tt))
    h_lo = ALPHA_DN * x_lo + (g1 * lo1 + g2 * lo2)
    h_hi = ALPHA_DN * x_hi + (g1 * hi1 + g2 * hi2)
    mu = (jnp.sum(h_lo, axis=-1, keepdims=True) + jnp.sum(h_hi, axis=-1, keepdims=True)) * (1.0 / D_MODEL)
    c_lo = h_lo - mu
    c_hi = h_hi - mu
    var = (jnp.sum(c_lo * c_lo, axis=-1, keepdims=True) + jnp.sum(c_hi * c_hi, axis=-1, keepdims=True)) * (1.0 / D_MODEL)
    inv = lax.rsqrt(var + LN_EPS)
    o_ref[:, 0:HALF] = c_lo * inv * g_ref[:, 0:HALF] + b_ref[:, 0:HALF]
    o_ref[:, HALF:] = c_hi * inv * g_ref[:, HALF:] + b_ref[:, HALF:]


def _combine(dest1, dest2, ybuf, x1p, route, ln_g, ln_b, *, tt=512):
    n = x1p.shape[0] // TOK_ROWS
    grid_spec = pltpu.PrefetchScalarGridSpec(
        num_scalar_prefetch=2,
        grid=(n // tt,),
        in_specs=[pl.BlockSpec(memory_space=pl.ANY),
                  pl.BlockSpec((tt * TOK_ROWS, LANES), lambda i, a, b: (i, 0)),
                  pl.BlockSpec((tt, ROUTE_W), lambda i, a, b: (i, 0)),
                  pl.BlockSpec((1, D_MODEL), lambda i, a, b: (0, 0)),
                  pl.BlockSpec((1, D_MODEL), lambda i, a, b: (0, 0))],
        out_specs=pl.BlockSpec((tt, D_MODEL), lambda i, a, b: (i, 0)),
        scratch_shapes=[pltpu.VMEM((2, TOP_K, tt * TOK_ROWS, LANES), jnp.uint32),
                        pltpu.SemaphoreType.DMA((2,))],
    )
    return pl.pallas_call(
        functools.partial(_combine_kernel, tt=tt),
        grid_spec=grid_spec,
        out_shape=jax.ShapeDtypeStruct((n, D_MODEL), F32),
        compiler_params=_cparams("arbitrary"),
        name="combine",
    )(dest1, dest2, ybuf, x1p, route, ln_g, ln_b)


def _block_plan(counts_row, n, tb):
    counts = counts_row[:N_EXPERTS, 0].astype(jnp.int32)
    pends = jnp.cumsum((counts + tb - 1) // tb * tb)
    nb = (n * TOP_K) // tb + N_EXPERTS
    nvalid = (pends[-1] // tb).astype(jnp.int32)
    blk = jnp.arange(nb, dtype=jnp.int32)
    blk_e = jnp.sum((pends[None, :] <= (blk * tb)[:, None]).astype(jnp.int32), axis=1)
    blk_e = jnp.minimum(blk_e, N_EXPERTS - 1)
    blk_e = jnp.where(blk < nvalid, blk_e, blk_e[jnp.maximum(nvalid - 1, 0)])
    first = jnp.concatenate([jnp.ones((1,), jnp.int32), (blk_e[1:] != blk_e[:-1]).astype(jnp.int32)])
    slot_e = (jnp.cumsum((counts > 0).astype(jnp.int32)) - 1) % 2
    first = first * (1 + slot_e[blk_e])
    ids = jnp.arange(N_EXPERTS, dtype=jnp.int32)
    later = lax.cummin(jnp.where(counts > 0, ids, N_EXPERTS), axis=0, reverse=True)
    nxt = jnp.concatenate([later[1:], jnp.full((1,), N_EXPERTS, jnp.int32)])
    nxt = jnp.where(nxt < N_EXPERTS, nxt, -1)
    last_blk = jnp.where(counts > 0, pends // tb - 1, -1).astype(jnp.int32)
    return blk_e, first, nxt[blk_e], nvalid.reshape(1), last_blk


def kernel(x, w_in, w_gk_up, b_gk, conv_w, a_log, dt_bias, gla_norm_g, gdn_norm_g, w_out, ln1_g, ln1_b, w_router_group, b_router_group, w_router_expert, b_router_expert, w_gate, w_up, w_down, ln2_g, ln2_b):
    bsz, seq, d = x.shape
    n = bsz * seq
    x2d = x.reshape(n, d)
    w_main, w_aux = _split_w_in(w_in[0])
    main, aux = _in_proj(x2d, w_main, w_aux, conv_w[0].astype(F32), seq=seq)
    wgk = jnp.zeros((AUX_W, 256), F32).at[:GLA_GATE_RANK].set(w_gk_up[0])
    o_a = _gla(main, aux, wgk, b_gk[0][None], gla_norm_g[0][None], bsz=bsz, seq=seq)
    o_b = _gdn(main, aux, _pad_aux_row(a_log[0], AUX_AB), _pad_aux_row(dt_bias[0], AUX_AB),
               gdn_norm_g[0][None], bsz=bsz, seq=seq)
    rpad = jnp.zeros((ROUTE_W - N_EXPERTS - N_GROUPS, d), F32)
    w_router = jnp.concatenate([w_router_expert[0].T, w_router_group[0].T, rpad], axis=0)
    w_router_hi = w_router.astype(BF16)
    w_router = jnp.concatenate([w_router_hi, (w_router - w_router_hi.astype(F32)).astype(BF16)], axis=0)
    b_router = jnp.concatenate([b_router_expert[0], b_router_group[0], rpad[:, 0]])[:, None]
    x1p, route, gates, counts = _post_mix(o_a, o_b, x2d, w_out[0].astype(BF16), ln1_g[0][None], ln1_b[0][None],
                                          w_router, b_router)
    dest = _plan(route, counts, tb=EXPERT_BLOCK)
    blk_e, first, next_e, nvalid, last_blk = _block_plan(counts, n, EXPERT_BLOCK)
    xs = _dispatch(dest[0], dest[1], last_blk, nvalid, x1p, nb=blk_e.shape[0])
    ybuf = _experts(blk_e, first, next_e, nvalid, xs, w_gate[0], w_up[0], w_down[0])
    out = _combine(dest[0], dest[1], ybuf, x1p, gates, ln2_g[0][None], ln2_b[0][None])
    return out.reshape(bsz, seq, d)
```

```python
import functools

import jax
import jax.numpy as jnp
import numpy as np
from jax import lax
from jax.experimental import pallas as pl
from jax.experimental.pallas import tpu as pltpu

F32 = jnp.float32
BF16 = jnp.bfloat16
HIGHEST = lax.Precision.HIGHEST

D_MODEL = 1024
DEPTH = 1
GLA_HEADS = 4
GLA_DK = 64
GLA_DV = 128
GLA_WIDTH = GLA_HEADS * GLA_DV
GLA_GATE_RANK = 16
GLA_GATE_NORM = 16.0
GDN_HEADS = 4
GDN_DK = 128
GDN_DV = 128
GDN_WIDTH = GDN_HEADS * GDN_DV
CONV_W = 4
CHUNK = 64
N_GROUPS = 8
EXPERTS_PER_GROUP = 8
N_EXPERTS = N_GROUPS * EXPERTS_PER_GROUP
TOP_K = 2
D_EXPERT = 512
LN_EPS = 1e-5
RMS_EPS = 1e-6
ALPHA_DN = (2.0 * DEPTH) ** 0.25

MAIN_W = 2 * GLA_HEADS * GLA_DK + 2 * GLA_WIDTH + 4 * GDN_WIDTH
AUX_W = 128
AUX_LRA = 0
AUX_BB = GLA_GATE_RANK
AUX_AB = GLA_GATE_RANK + GDN_HEADS
SUB = 16
VMEM_LIMIT = 56 * 1024 * 1024


def _cparams(*sem):
    return pltpu.CompilerParams(dimension_semantics=sem, vmem_limit_bytes=VMEM_LIMIT)


def _sigmoid(x):
    return 1.0 / (1.0 + jnp.exp(-x))


def _silu(x):
    return x * _sigmoid(x)


def _log_sigmoid(x):
    return jnp.minimum(x, 0.0) - jnp.log(1.0 + jnp.exp(-jnp.abs(x)))


def _softplus(x):
    return jnp.maximum(x, 0.0) + jnp.log(1.0 + jnp.exp(-jnp.abs(x)))


def _chunk_cumsum(x, c):
    pos = lax.broadcasted_iota(jnp.int32, x.shape, 0) % c
    shift = 1
    while shift < c:
        x = x + jnp.where(pos >= shift, pltpu.roll(x, shift, 0), 0.0)
        shift *= 2
    return x


CONV_PAD = 8
GDN_HW = GDN_HEADS * GDN_DK
GDN_QKV_BLOCK0 = (2 * GLA_HEADS * GLA_DK + 2 * GLA_WIDTH) // GDN_HW


def _in_proj_kernel(x_ref, wm_ref, wa_ref, cw_ref, main_ref, aux_ref, stage_ref, *, tm, tiles_per_seq):
    xb = x_ref[...].astype(BF16)

    @pl.when((pl.program_id(0) % tiles_per_seq) == 0)
    def _():
        stage_ref[:, 0:CONV_PAD, :] = jnp.zeros((3 * GDN_HEADS, CONV_PAD, GDN_DK), F32)

    for j in range(MAIN_W // GDN_HW):
        sl = slice(j * GDN_HW, (j + 1) * GDN_HW)
        r = jnp.dot(xb, wm_ref[:, sl], preferred_element_type=F32)
        b = j - GDN_QKV_BLOCK0
        if not 0 <= b < 3:
            main_ref[:, sl] = r.astype(BF16)
            continue
        for h in range(GDN_HEADS):
            st = stage_ref.at[b * GDN_HEADS + h]
            cols = slice(b * GDN_HW + h * GDN_DK, b * GDN_HW + (h + 1) * GDN_DK)
            st[CONV_PAD:, :] = r[:, h * GDN_DK:(h + 1) * GDN_DK]
            acc = jnp.zeros((tm, GDN_DK), F32)
            for i in range(CONV_W):
                lo = CONV_PAD - (CONV_W - 1) + i
                acc = acc + st[lo:lo + tm, :] * cw_ref[i:i + 1, cols]
            st[0:CONV_PAD, :] = st[tm:tm + CONV_PAD, :]
            yh = _silu(acc)
            if b < 2:
                yh = yh * lax.rsqrt(jnp.sum(yh * yh, axis=-1, keepdims=True) + RMS_EPS)
            if b == 0:
                yh = yh * (GDN_DK ** -0.5)
            main_ref[:, j * GDN_HW + h * GDN_DK:j * GDN_HW + (h + 1) * GDN_DK] = yh.astype(BF16)
    aux_ref[...] = jnp.dot(xb, wa_ref[...], preferred_element_type=F32)


def _in_proj(x2d, w_main, w_aux, conv_w, *, seq, tm=512):
    n = x2d.shape[0]
    return pl.pallas_call(
        functools.partial(_in_proj_kernel, tm=tm, tiles_per_seq=seq // tm),
        grid=(n // tm,),
        in_specs=[pl.BlockSpec((tm, D_MODEL), lambda i: (i, 0)),
                  pl.BlockSpec((D_MODEL, MAIN_W), lambda i: (0, 0)),
                  pl.BlockSpec((D_MODEL, AUX_W), lambda i: (0, 0)),
                  pl.BlockSpec((CONV_W, 3 * GDN_HW), lambda i: (0, 0))],
        out_specs=[pl.BlockSpec((tm, MAIN_W), lambda i: (i, 0)),
                   pl.BlockSpec((tm, AUX_W), lambda i: (i, 0))],
        out_shape=[jax.ShapeDtypeStruct((n, MAIN_W), BF16),
                   jax.ShapeDtypeStruct((n, AUX_W), F32)],
        scratch_shapes=[pltpu.VMEM((3 * GDN_HEADS, tm + CONV_PAD, GDN_DK), F32)],
        compiler_params=_cparams("arbitrary"),
        name="in_proj",
    )(x2d, w_main, w_aux, conv_w)


def _split_w_in(w_in):
    sizes = (GLA_HEADS * GLA_DK, GLA_HEADS * GLA_DK, GLA_WIDTH, GLA_WIDTH, GLA_GATE_RANK,
             GDN_HEADS * GDN_DK, GDN_HEADS * GDN_DK, GDN_WIDTH, GDN_WIDTH, GDN_HEADS, GDN_HEADS)
    offs = np.cumsum((0,) + sizes)
    seg = [w_in[:, offs[i]:offs[i + 1]] for i in range(len(sizes))]
    qa, ka, va, ra, lra, qb, kb, vb, zb, bb, ab = seg
    w_main = jnp.concatenate([qa, ka, va, ra, qb, kb, vb, zb], axis=1).astype(BF16)
    pad = jnp.zeros((w_in.shape[0], AUX_W - GLA_GATE_RANK - 2 * GDN_HEADS), w_in.dtype)
    w_aux = jnp.concatenate([lra, bb, ab, pad], axis=1).astype(BF16)
    return w_main, w_aux


_NT = (((1,), (1,)), ((), ()))
_TN = (((0,), (0,)), ((), ()))


def _gla_kernel(q_ref, k_ref, v_ref, r_ref, aux_ref, wgk_ref, bgk_ref, g_ref, o_ref, st_ref, *, ts, nseq):
    npairs = 2 * nseq
    c = CHUNK
    nsub = c // SUB

    @pl.when(pl.program_id(1) == 0)
    def _():
        st_ref[...] = jnp.zeros_like(st_ref)

    gk_all = [_log_sigmoid(jnp.dot(aux_ref[sq], wgk_ref[...], precision=HIGHEST, preferred_element_type=F32)
                           + bgk_ref[...]) * (1.0 / GLA_GATE_NORM) for sq in range(nseq)]

    row128 = lax.broadcasted_iota(jnp.int32, (c, 128), 0)
    lane128 = lax.broadcasted_iota(jnp.int32, (1, 128), 1)
    lane_masks = [(lane128 // GLA_DK) == hh for hh in range(2)]
    sub_valid = [row128 < SUB * (i + 1) for i in range(nsub)]
    ar = lax.broadcasted_iota(jnp.int32, (c, nsub * c), 0)
    ac = lax.broadcasted_iota(jnp.int32, (c, nsub * c), 1)
    amask = ((ac // c) == (ar // SUB)) & ((ac % c) <= ar)
    scale = GLA_DK ** -0.5
    gnorm = g_ref[...]

    nch = ts // c
    rows = [slice(ch * c, (ch + 1) * c) for ch in range(nch)]
    b_cum = [_chunk_cumsum(g, c) for g in gk_all]

    cp = [(ch, p) for ch in range(nch) for p in range(npairs)]
    qt, qd, kd, kstack, dlast = {}, {}, {}, {}, {}
    for ch, p in cp:
        sq, pp = divmod(p, 2)
        lanes = slice(128 * pp, 128 * (pp + 1))
        bp = b_cum[sq][rows[ch], lanes]
        qp = q_ref[sq, rows[ch], lanes].astype(F32) * scale
        kp = k_ref[sq, rows[ch], lanes].astype(F32)
        c_row = jnp.concatenate(
            [jnp.broadcast_to(bp[SUB * i:SUB * i + 1, :], (SUB, 128)) for i in range(nsub)], axis=0)
        qt[ch, p] = (qp * jnp.exp(bp - c_row)).astype(BF16)
        qd[ch, p] = qp * jnp.exp(bp)
        b_last = bp[c - 1:c, :]
        dlast[ch, p] = jnp.exp(b_last)
        kd[ch, p] = kp * jnp.exp(b_last - bp)
        kts = []
        for i in range(nsub):
            e = jnp.exp(jnp.where(sub_valid[i], bp[SUB * i:SUB * i + 1, :] - bp, 0.0))
            kts.append(jnp.where(sub_valid[i], kp * e, 0.0))
        kstack[ch, p] = jnp.concatenate(kts, axis=0)

    cph = [(ch, p, hh) for ch, p in cp for hh in range(2)]
    vh = {(ch, p, hh): v_ref[p // 2, rows[ch], 128 * (2 * (p % 2) + hh):128 * (2 * (p % 2) + hh + 1)]
          for ch, p, hh in cph}
    r_mat = {(ch, p, hh): lax.dot_general(qt[ch, p], jnp.where(lane_masks[hh], kstack[ch, p], 0.0).astype(BF16),
                                          _NT, preferred_element_type=F32) for ch, p, hh in cph}
    o_intra = {k: jnp.dot(jnp.where(amask, r_mat[k], 0.0).astype(BF16), jnp.concatenate([vh[k]] * nsub, axis=0),
                          preferred_element_type=F32) for k in cph}
    upd = {(ch, p, hh): lax.dot_general(vh[ch, p, hh], jnp.where(lane_masks[hh], kd[ch, p], 0.0).astype(BF16), _TN,
                                        preferred_element_type=F32) for ch, p, hh in cph}

    st = [st_ref[p] for p in range(npairs)]
    for ch in range(nch):
        st_b = [s.astype(BF16) for s in st]
        o_pair = [lax.dot_general(
            jnp.concatenate([jnp.where(lane_masks[hh], qd[ch, p], 0.0).astype(BF16) for hh in range(2)], axis=0),
            st_b[p], _NT, preferred_element_type=F32) for p in range(npairs)]
        o_inter = {(p, hh): o_pair[p][hh * c:(hh + 1) * c] for p in range(npairs) for hh in range(2)}
        st = [st[p] * dlast[ch, p] + upd[ch, p, 0] + upd[ch, p, 1] for p in range(npairs)]
        for p in range(npairs):
            for hh in range(2):
                sq, h = p // 2, 2 * (p % 2) + hh
                o = o_intra[ch, p, hh] + o_inter[p, hh]
                o = o * lax.rsqrt(jnp.mean(o * o, axis=-1, keepdims=True) + RMS_EPS) * gnorm
                gate = _silu(r_ref[sq, rows[ch], 128 * h:128 * (h + 1)].astype(F32))
                o_ref[sq, rows[ch], 128 * h:128 * (h + 1)] = (o * gate).astype(BF16)
    for p in range(npairs):
        st_ref[p] = st[p]


def _gla(main, aux, wgk_pad, b_gk, gla_norm_g, *, bsz, seq, ts=512, nseq=2):
    nt = seq // ts
    assert bsz % nseq == 0
    main4 = main.reshape(bsz // nseq, nseq, seq, MAIN_W)
    aux4 = aux.reshape(bsz // nseq, nseq, seq, AUX_W)
    tok = lambda width, col: pl.BlockSpec((None, nseq, ts, width), lambda b, t: (b, 0, t, col))
    out = pl.pallas_call(
        functools.partial(_gla_kernel, ts=ts, nseq=nseq),
        grid=(bsz // nseq, nt),
        in_specs=[tok(256, 0), tok(256, 1), tok(512, 1), tok(512, 2),
                  tok(AUX_W, 0),
                  pl.BlockSpec((AUX_W, 256), lambda b, t: (0, 0)),
                  pl.BlockSpec((1, 256), lambda b, t: (0, 0)),
                  pl.BlockSpec((1, 128), lambda b, t: (0, 0))],
        out_specs=tok(GLA_WIDTH, 0),
        out_shape=jax.ShapeDtypeStruct((bsz // nseq, nseq, seq, GLA_WIDTH), BF16),
        scratch_shapes=[pltpu.VMEM((2 * nseq, 128, 128), F32)],
        compiler_params=_cparams("parallel", "arbitrary"),
        name="gla",
    )(main4, main4, main4, main4, aux4, wgk_pad, b_gk, gla_norm_g)
    return out.reshape(bsz * seq, GLA_WIDTH)


def _gdn_kernel(q_ref, k_ref, v_ref, z_ref, aux_ref, alog_ref, dtb_ref, g_ref, o_ref, st_ref, *, ts, nseq):
    c = CHUNK
    nch = ts // c

    @pl.when(pl.program_id(1) == 0)
    def _():
        st_ref[...] = jnp.zeros_like(st_ref)

    beta_full, gam_full, gam_t = [], [], []
    for sq in range(nseq):
        aux = aux_ref[sq]
        beta_full.append(_sigmoid(aux))
        g_full = -jnp.exp(alog_ref[...]) * _softplus(aux + dtb_ref[...])
        gam_full.append(_chunk_cumsum(g_full, c))
        gam_t.append(gam_full[-1].T)
    gnorm = g_ref[...]

    row = lax.broadcasted_iota(jnp.int32, (c, 2 * c), 0)
    lane = lax.broadcasted_iota(jnp.int32, (c, 2 * c), 1)
    left = lane < c
    col = lane % c
    causal = col <= row
    strict = col < row
    same32 = (row // 32) == (col // 32)
    same16 = (row // 16) == (col // 16)
    lvl0 = same16 & strict
    lvl1 = same32 & jnp.logical_not(same16) & strict
    lvl2 = jnp.logical_not(same32) & strict

    def pack_rows(x2):
        return jnp.where(left, x2[0:c], x2[c:2 * c])

    def pack_cols(x1):
        return jnp.where(left, x1[0:c], x1[c:2 * c])

    def halves(x):
        return jnp.where(left, x, 0.0).astype(BF16), jnp.where(left, 0.0, x).astype(BF16)

    def mmp(x, y):
        return jnp.dot(x.astype(BF16), jnp.concatenate(halves(y), axis=0), preferred_element_type=F32)

    heads = range(nseq * GDN_HEADS)
    npair = nch // 2
    items = [(h, pr) for h in heads for pr in range(npair)]
    prow = [slice(2 * c * pr, 2 * c * (pr + 1)) for pr in range(npair)]

    qn, kn, beta, gam, gam_r, kb, rhs, rhs_b, qd, kd = ([None] * len(heads) for _ in range(10))
    for h in heads:
        sq, hh = divmod(h, GDN_HEADS)
        hl = slice(128 * hh, 128 * (hh + 1))
        kb[h] = k_ref[sq, :, hl]
        qn[h] = q_ref[sq, :, hl].astype(F32)
        kn[h] = kb[h].astype(F32)
        vh = v_ref[sq, :, hl].astype(F32)
        beta[h] = beta_full[sq][:, AUX_BB + hh:AUX_BB + hh + 1]
        gam[h] = gam_full[sq][:, AUX_AB + hh:AUX_AB + hh + 1]
        gam_r[h] = gam_t[sq][AUX_AB + hh:AUX_AB + hh + 1, :]
        egam = jnp.exp(gam[h])
        rhs[h] = jnp.concatenate([beta[h] * vh, beta[h] * egam * kn[h]], axis=1)
        rhs_b[h] = rhs[h].astype(BF16)
        qd[h] = (qn[h] * egam).astype(BF16)
        gl_rows = jnp.concatenate(
            [jnp.broadcast_to(gam[h][c * (j + 1) - 1:c * (j + 1), :], (c, 1)) for j in range(nch)], axis=0)
        kd[h] = (kn[h] * jnp.exp(gl_rows - gam[h])).astype(BF16)

    kk = [pack_rows(lax.dot_general(kb[h][prow[pr]], kb[h][prow[pr]], _NT, preferred_element_type=F32))
          for h, pr in items]
    qk = [pack_rows(lax.dot_general(qn[h][prow[pr]].astype(BF16), kb[h][prow[pr]], _NT,
                                    preferred_element_type=F32)) for h, pr in items]
    dec = [jnp.exp(jnp.where(causal, pack_cols(gam[h][prow[pr]]) - gam_r[h][:, prow[pr]], 0.0)) for h, pr in items]
    a = [pack_cols(beta[h][prow[pr]]) * kk[i] * dec[i] for i, (h, pr) in enumerate(items)]
    attn = [halves(jnp.where(causal, qk[i] * dec[i], 0.0)) for i in range(len(items))]

    x1 = [jnp.where(lvl0, -ai, 0.0) for ai in a]
    x2 = [mmp(x, x) for x in x1]
    y = [mmp(x, xx) for x, xx in zip(x1, x2)]
    p = [x + xx + yy for x, xx, yy in zip(x1, x2, y)]
    xk = x2
    for _ in range(2):
        xk = [mmp(x, x) for x in xk]
        y = [mmp(pp, x) for pp, x in zip(p, xk)]
        p = [pp + x + yy for pp, x, yy in zip(p, xk, y)]
    for lvl in (lvl1, lvl2):
        low = [jnp.where(lvl, ai, 0.0) for ai in a]
        y = [mmp(pp, lo) for pp, lo in zip(p, low)]
        zz = [lo + yy for lo, yy in zip(low, y)]
        y = [mmp(z, pp) for z, pp in zip(zz, p)]
        p = [pp - z - yy for pp, z, yy in zip(p, zz, y)]

    us = [[None] * nch for _ in heads]
    ws = [[None] * nch for _ in heads]
    for i, (h, pr) in enumerate(items):
        for half, p_half in enumerate(halves(p[i])):
            j = 2 * pr + half
            uw = rhs[h][c * j:c * (j + 1)] + jnp.dot(p_half, rhs_b[h][prow[pr]], preferred_element_type=F32)
            us[h][j] = uw[:, :GDN_DV]
            ws[h][j] = uw[:, GDN_DV:].astype(BF16)

    s = [st_ref[h] for h in heads]
    vprev = [None] * len(heads)
    for j in range(nch):
        rows = slice(c * j, c * (j + 1))
        sb = [s[h].astype(BF16) for h in heads]
        ws_qd = [jnp.dot(jnp.concatenate([ws[h][j], qd[h][rows]], axis=0), sb[h], preferred_element_type=F32)
                 for h in heads]
        vn16 = [(us[h][j] - ws_qd[h][0:c]).astype(BF16) for h in heads]
        o_inter = [ws_qd[h][c:2 * c] for h in heads]
        for h in heads:
            vpair = jnp.concatenate([vn16[h], jnp.zeros_like(vn16[h])] if j % 2 == 0 else [vprev[h], vn16[h]], axis=0)
            o = o_inter[h] + jnp.dot(attn[h * npair + j // 2][j % 2], vpair, preferred_element_type=F32)
            s[h] = jnp.exp(gam[h][c * (j + 1) - 1:c * (j + 1), :]) * s[h] + lax.dot_general(
                kd[h][rows], vn16[h], _TN, preferred_element_type=F32)
            o = o * lax.rsqrt(jnp.mean(o * o, axis=-1, keepdims=True) + RMS_EPS) * gnorm
            sq, hh = divmod(h, GDN_HEADS)
            gate = _silu(z_ref[sq, rows, 128 * hh:128 * (hh + 1)].astype(F32))
            o_ref[sq, rows, 128 * hh:128 * (hh + 1)] = (o * gate).astype(BF16)
        vprev = vn16
    for h in heads:
        st_ref[h] = s[h]


def _gdn(main, aux, alog_pad, dtb_pad, gdn_norm_g, *, bsz, seq, ts=512, nseq=2):
    nt = seq // ts
    hw = GDN_HEADS * GDN_DK
    assert bsz % nseq == 0
    main4 = main.reshape(bsz // nseq, nseq, seq, MAIN_W)
    aux4 = aux.reshape(bsz // nseq, nseq, seq, AUX_W)
    tok = lambda col: pl.BlockSpec((None, nseq, ts, hw), lambda b, t: (b, 0, t, col))
    out = pl.pallas_call(
        functools.partial(_gdn_kernel, ts=ts, nseq=nseq),
        grid=(bsz // nseq, nt),
        in_specs=[tok(3), tok(4), tok(5), tok(6),
                  pl.BlockSpec((None, nseq, ts, AUX_W), lambda b, t: (b, 0, t, 0)),
                  pl.BlockSpec((1, AUX_W), lambda b, t: (0, 0)),
                  pl.BlockSpec((1, AUX_W), lambda b, t: (0, 0)),
                  pl.BlockSpec((1, GDN_DV), lambda b, t: (0, 0))],
        out_specs=pl.BlockSpec((None, nseq, ts, GDN_WIDTH), lambda b, t: (b, 0, t, 0)),
        out_shape=jax.ShapeDtypeStruct((bsz // nseq, nseq, seq, GDN_WIDTH), BF16),
        scratch_shapes=[pltpu.VMEM((nseq * GDN_HEADS, GDN_DK, GDN_DV), F32)],
        compiler_params=_cparams("parallel", "arbitrary"),
        name="gdn",
    )(main4, main4, main4, main4, aux4, alog_pad, dtb_pad, gdn_norm_g)
    return out.reshape(bsz * seq, GDN_WIDTH)


def _pad_aux_row(v, offset):
    return jnp.zeros((1, AUX_W), F32).at[0, offset:offset + v.shape[0]].set(v.astype(F32))


ROUTE_W = 128
R_E1, R_E2, R_RANK1, R_RANK2, R_G1, R_G2 = range(6)
ROUTER_GROUP_COL = N_EXPERTS
POST_MIX_SPLIT = 4


def _layer_norm(h, g, b):
    mu = jnp.mean(h, axis=-1, keepdims=True)
    hc = h - mu
    var = jnp.mean(hc * hc, axis=-1, keepdims=True)
    return hc * lax.rsqrt(var + LN_EPS) * g + b


def _pack_bf16_pairs(x):
    w = x.shape[1] // 2
    lo = lax.bitcast_convert_type(x[:, :w].astype(BF16).astype(F32), jnp.uint32)
    hi = lax.bitcast_convert_type(x[:, w:].astype(BF16).astype(F32), jnp.uint32)
    return (lo >> 16) | (hi & jnp.uint32(0xFFFF0000))


def _unpack_bf16_pairs(p):
    lo = lax.bitcast_convert_type(p << 16, F32)
    hi = lax.bitcast_convert_type(p & jnp.uint32(0xFFFF0000), F32)
    return lo, hi


HALF = D_MODEL // 2
LANES = 128
TOK_ROWS = HALF // LANES


def _store_token_rows(ref, packed):
    m = packed.shape[0]
    for c in range(TOK_ROWS):
        ref[pl.ds(c, m, stride=TOK_ROWS), :] = packed[:, LANES * c:LANES * (c + 1)]


def _load_token_rows(ref, m):
    return jnp.concatenate([ref[pl.ds(c, m, stride=TOK_ROWS), :] for c in range(TOK_ROWS)], axis=1)


def _post_mix_kernel(oa_ref, ob_ref, x_ref, wo_ref, g_ref, b_ref, wr_ref, br_ref,
                     x1p_ref, route_ref, gates_ref, cnt_ref, carry_ref, *, tm):
    @pl.when(pl.program_id(0) == 0)
    def _():
        carry_ref[...] = jnp.zeros_like(carry_ref)

    m = tm // POST_MIX_SPLIT
    groups = range(POST_MIX_SPLIT)
    rs = [slice(h * m, (h + 1) * m) for h in groups]
    y = [jnp.dot(oa_ref[rs[h], :], wo_ref[0:GLA_WIDTH, :], preferred_element_type=F32)
         + jnp.dot(ob_ref[rs[h], :], wo_ref[GLA_WIDTH:, :], preferred_element_type=F32) for h in groups]
    x1 = [_layer_norm(ALPHA_DN * x_ref[rs[h], :] + y[h], g_ref[...], b_ref[...]) for h in groups]
    for h in groups:
        _store_token_rows(x1p_ref.at[pl.ds(h * m * TOK_ROWS, m * TOK_ROWS), :], _pack_bf16_pairs(x1[h]))

    xh = [x.astype(BF16) for x in x1]
    xl = [(x - xb.astype(F32)).astype(BF16) for x, xb in zip(x1, xh)]
    hw = [lax.dot_general(wr_ref[...], xb, _NT, preferred_element_type=F32) for xb in xh]
    lw = [lax.dot_general(wr_ref[0:ROUTE_W, :], xb, _NT, preferred_element_type=F32) for xb in xl]
    logits = [a[:ROUTE_W] + a[ROUTE_W:] + b + br_ref[...] for a, b in zip(hw, lw)]
    sub = lax.broadcasted_iota(jnp.int32, (ROUTE_W, m), 0)
    big = jnp.int32(1 << 20)
    neg = jnp.float32(-jnp.inf)

    def first_argmax(vals):
        mx = jnp.max(vals, axis=0, keepdims=True)
        idx = jnp.min(jnp.where(vals == mx, sub, big), axis=0, keepdims=True)
        return mx, idx

    is_group = (sub >= ROUTER_GROUP_COL) & (sub < ROUTER_GROUP_COL + N_GROUPS)
    gl = [jnp.where(is_group, lg, neg) for lg in logits]
    gtop = [first_argmax(v) for v in gl]
    p_group = [1.0 / jnp.sum(jnp.exp(v - t[0]), axis=0, keepdims=True) for v, t in zip(gl, gtop)]
    el = [jnp.where((sub // EXPERTS_PER_GROUP) == (t[1] - ROUTER_GROUP_COL), lg, neg) for lg, t in zip(logits, gtop)]
    top1 = [first_argmax(v) for v in el]
    top2 = [first_argmax(jnp.where(sub == t[1], neg, v)) for v, t in zip(el, top1)]
    ex = [jnp.exp(b[0] - a[0]) for a, b in zip(top1, top2)]
    w1 = [1.0 / (1.0 + e) for e in ex]
    gate1 = [p * w for p, w in zip(p_group, w1)]
    gate2 = [p * (e * w) for p, e, w in zip(p_group, ex, w1)]

    oh1 = [sub == t[1] for t in top1]
    oh2 = [sub == t[1] for t in top2]
    ri = lax.broadcasted_iota(jnp.int32, (m, m), 0)
    ci = lax.broadcasted_iota(jnp.int32, (m, m), 1)
    ustrict = (ri < ci).astype(BF16)
    oh1f = [o.astype(F32) for o in oh1]
    oh2f = [o.astype(F32) for o in oh2]
    c1 = [jnp.dot(o.astype(BF16), ustrict, preferred_element_type=F32) for o in oh1f]
    c2 = [jnp.dot(o.astype(BF16), ustrict, preferred_element_type=F32) for o in oh2f]
    tot1 = [jnp.sum(o, axis=1, keepdims=True) for o in oh1f]
    tot2 = [jnp.sum(o, axis=1, keepdims=True) for o in oh2f]
    carry = carry_ref[...]
    eye8 = jnp.eye(8, ROUTE_W, dtype=F32)
    zero = jnp.zeros((1, m), F32)
    for h in groups:
        rank1 = jnp.sum(jnp.where(oh1[h], c1[h] + carry, 0.0), axis=0, keepdims=True)
        rank2 = jnp.sum(jnp.where(oh2[h], c2[h] + (carry + tot1[h]), 0.0), axis=0, keepdims=True)
        carry = carry + tot1[h] + tot2[h]
        route_t = jnp.concatenate([top1[h][1].astype(F32), top2[h][1].astype(F32), rank1, rank2,
                                   gate1[h], gate2[h], zero, zero], axis=0)
        route_ref[:, rs[h]] = route_t
        gates_ref[rs[h], :] = lax.dot_general(route_t, eye8, _TN, precision=HIGHEST, preferred_element_type=F32)
    carry_ref[...] = carry
    cnt_ref[...] = carry


def _post_mix(o_a, o_b, x2d, w_out_b, ln_g, ln_b, w_router, b_router, *, tm=1024):
    n = x2d.shape[0]
    return pl.pallas_call(
        functools.partial(_post_mix_kernel, tm=tm),
        grid=(n // tm,),
        in_specs=[pl.BlockSpec((tm, GLA_WIDTH), lambda i: (i, 0)),
                  pl.BlockSpec((tm, GDN_WIDTH), lambda i: (i, 0)),
                  pl.BlockSpec((tm, D_MODEL), lambda i: (i, 0)),
                  pl.BlockSpec((GLA_WIDTH + GDN_WIDTH, D_MODEL), lambda i: (0, 0)),
                  pl.BlockSpec((1, D_MODEL), lambda i: (0, 0)),
                  pl.BlockSpec((1, D_MODEL), lambda i: (0, 0)),
                  pl.BlockSpec((2 * ROUTE_W, D_MODEL), lambda i: (0, 0)),
                  pl.BlockSpec((ROUTE_W, 1), lambda i: (0, 0))],
        out_specs=[pl.BlockSpec((tm * TOK_ROWS, LANES), lambda i: (i, 0)),
                   pl.BlockSpec((8, tm), lambda i: (0, i)),
                   pl.BlockSpec((tm, ROUTE_W), lambda i: (i, 0)),
                   pl.BlockSpec((ROUTE_W, 1), lambda i: (0, 0))],
        out_shape=[jax.ShapeDtypeStruct((n * TOK_ROWS, LANES), jnp.uint32),
                   jax.ShapeDtypeStruct((8, n), F32),
                   jax.ShapeDtypeStruct((n, ROUTE_W), F32),
                   jax.ShapeDtypeStruct((ROUTE_W, 1), F32)],
        scratch_shapes=[pltpu.VMEM((ROUTE_W, 1), F32)],
        compiler_params=_cparams("arbitrary"),
        name="post_mix",
    )(o_a, o_b, x2d, w_out_b, ln_g, ln_b, w_router, b_router)


EXPERT_BLOCK = 256
BLOCKS_PER_STEP = 4

ROW_UNROLL = 8


def _plan_kernel(route_ref, cnt_ref, dest_ref, *, tm, tb):
    counts = cnt_ref[...].astype(jnp.int32)
    padded = jnp.broadcast_to(((counts + (tb - 1)) & jnp.int32(-tb)).astype(F32), (ROUTE_W, ROUTE_W))
    pstart = (_chunk_cumsum(padded, ROUTE_W) - padded)[:, 0:1]
    route = route_ref[...]
    sub = lax.broadcasted_iota(jnp.int32, (ROUTE_W, tm), 0).astype(F32)
    rows = []
    for ce, cr in ((R_E1, R_RANK1), (R_E2, R_RANK2)):
        off = jnp.sum(jnp.where(sub == route[ce:ce + 1, :], pstart, 0.0), axis=0, keepdims=True)
        rows.append(off + route[cr:cr + 1, :])
    rows.extend([jnp.zeros((1, tm), F32)] * (8 - len(rows)))
    dest_ref[...] = jnp.concatenate(rows, axis=0).astype(jnp.int32) * TOK_ROWS


def _plan(route, counts, *, tb, tm=2048):
    n = route.shape[1]
    tm = min(tm, n)
    return pl.pallas_call(
        functools.partial(_plan_kernel, tm=tm, tb=tb),
        grid=(n // tm,),
        in_specs=[pl.BlockSpec((8, tm), lambda i: (0, i)),
                  pl.BlockSpec((ROUTE_W, 1), lambda i: (0, 0))],
        out_specs=pl.BlockSpec((8, tm), lambda i: (0, i)),
        out_shape=jax.ShapeDtypeStruct((8, n), jnp.int32),
        compiler_params=_cparams("parallel"),
        name="plan",
    )(route, counts)


def _dispatch_kernel(d1_ref, d2_ref, last_blk_ref, nvalid_ref, xp_ref, xs_ref, zbuf, sem, zsem, *, tm, tb, nb):
    base = pl.program_id(0) * tm
    rows = tb * TOK_ROWS

    @pl.when(pl.program_id(0) == 0)
    def _():
        zbuf[...] = jnp.zeros_like(zbuf)

        def zero_copy(blk):
            return pltpu.make_async_copy(zbuf, xs_ref.at[pl.ds(pl.multiple_of(blk * rows, rows), rows), :], zsem)

        def for_each_block(fn):
            def per_expert(e, carry):
                @pl.when(last_blk_ref[e] >= 0)
                def _():
                    fn(zero_copy(last_blk_ref[e]))
                return carry

            def per_tail(blk, carry):
                fn(zero_copy(blk))
                return carry
            lax.fori_loop(0, N_EXPERTS, per_expert, 0)
            lax.fori_loop(nvalid_ref[0], nb, per_tail, 0)

        for_each_block(lambda cp: cp.start())
        for_each_block(lambda cp: cp.wait())

    def body(j, carry):
        for u in range(ROW_UNROLL):
            r = j * ROW_UNROLL + u
            for k, d_ref in enumerate((d1_ref, d2_ref)):
                dst = pl.multiple_of(d_ref[base + r], TOK_ROWS)
                pltpu.make_async_copy(xp_ref.at[pl.ds(r * TOK_ROWS, TOK_ROWS), :],
                                      xs_ref.at[pl.ds(dst, TOK_ROWS), :], sem).start(priority=k)
        return carry
    lax.fori_loop(0, tm // ROW_UNROLL, body, 0)
    for _ in range(TOP_K):
        pltpu.make_async_copy(xp_ref, xs_ref.at[pl.ds(0, tm * TOK_ROWS), :], sem).wait()


def _dispatch(dest1, dest2, last_blk, nvalid, x1p, *, nb, tb=EXPERT_BLOCK, tm=2048):
    n = x1p.shape[0] // TOK_ROWS
    tm = min(tm, n)
    grid_spec = pltpu.PrefetchScalarGridSpec(
        num_scalar_prefetch=4,
        grid=(n // tm,),
        in_specs=[pl.BlockSpec((tm * TOK_ROWS, LANES), lambda i, a, b, lb, nv: (i, 0))],
        out_specs=pl.BlockSpec(memory_space=pl.ANY),
        scratch_shapes=[pltpu.VMEM((tb * TOK_ROWS, LANES), jnp.uint32),
                        pltpu.SemaphoreType.DMA(()),
                        pltpu.SemaphoreType.DMA(())],
    )
    return pl.pallas_call(
        functools.partial(_dispatch_kernel, tm=tm, tb=tb, nb=nb),
        grid_spec=grid_spec,
        out_shape=jax.ShapeDtypeStruct((nb * tb * TOK_ROWS, LANES), jnp.uint32),
        compiler_params=_cparams("arbitrary"),
        name="dispatch",
    )(dest1, dest2, last_blk, nvalid, x1p)


def _expert_kernel(blk_e_ref, first_ref, next_e_ref, nvalid_ref, xs_ref, wg_hbm, wu_hbm, wd_hbm, y_ref,
                   wgs, wus, wds, wgb, wub, wdb, sem, *, tb):
    rows = tb * TOK_ROWS

    def weight_copies(e, slot):
        return [pltpu.make_async_copy(src.at[e], dst.at[slot], sem.at[slot, j])
                for j, (src, dst) in enumerate(((wg_hbm, wgs), (wu_hbm, wus), (wd_hbm, wds)))]

    @pl.when((pl.program_id(0) == 0) & (nvalid_ref[0] > 0))
    def _():
        for cp in weight_copies(blk_e_ref[0], first_ref[0] - 1):
            cp.start()

    def run(i, sub, nblk):
        @pl.when(first_ref[i] > 0)
        def _():
            slot = first_ref[i] - 1

            @pl.when(next_e_ref[i] >= 0)
            def _():
                for cp in weight_copies(next_e_ref[i], 1 - slot):
                    cp.start()

            for cp in weight_copies(blk_e_ref[i], slot):
                cp.wait()
            wgb[...] = wgs[slot].astype(BF16)
            wub[...] = wus[slot].astype(BF16)
            wdb[...] = wds[slot].astype(BF16)
            mlp(sub, nblk)

        @pl.when(first_ref[i] == 0)
        def _():
            mlp(sub, nblk)

    def mlp(sub, nblk):
        m = nblk * tb
        span = pl.ds(sub * rows, nblk * rows)
        lo, hi = _unpack_bf16_pairs(_load_token_rows(xs_ref.at[span, :], m))
        xb = jnp.concatenate([lo.astype(BF16), hi.astype(BF16)], axis=1)
        hg = jnp.dot(xb, wgb[...], preferred_element_type=F32)
        hu = jnp.dot(xb, wub[...], preferred_element_type=F32)
        hid = (_silu(hg) * hu).astype(BF16)
        _store_token_rows(y_ref.at[span, :], _pack_bf16_pairs(jnp.dot(hid, wdb[...], preferred_element_type=F32)))

    nvalid = nvalid_ref[0]
    for sub in range(0, BLOCKS_PER_STEP, 2):
        i0 = pl.program_id(0) * BLOCKS_PER_STEP + sub
        i1 = i0 + 1
        paired = (i1 < nvalid) & (first_ref[i1] == 0)

        @pl.when(paired)
        def _():
            run(i0, sub, 2)

        @pl.when(jnp.logical_not(paired))
        def _():
            @pl.when(i0 < nvalid)
            def _():
                run(i0, sub, 1)

            @pl.when(i1 < nvalid)
            def _():
                run(i1, sub + 1, 1)

        for s_, i in ((sub, i0), (sub + 1, i1)):
            @pl.when(i >= nvalid)
            def _():
                y_ref[pl.ds(s_ * rows, rows), :] = jnp.zeros((rows, LANES), jnp.uint32)


def _experts(blk_e, first, next_e, nvalid, xs, w_gate, w_up, w_down, *, tb=EXPERT_BLOCK):
    nb = blk_e.shape[0]
    assert nb % BLOCKS_PER_STEP == 0
    step_rows = BLOCKS_PER_STEP * tb * TOK_ROWS
    last_step = lambda nv: jnp.maximum((nv[0] - 1) // BLOCKS_PER_STEP, 0)
    grid_spec = pltpu.PrefetchScalarGridSpec(
        num_scalar_prefetch=4,
        grid=(nb // BLOCKS_PER_STEP,),
        in_specs=[pl.BlockSpec((step_rows, LANES), lambda s, be, fr, ne, nv: (jnp.minimum(s, last_step(nv)), 0)),
                  pl.BlockSpec(memory_space=pl.ANY),
                  pl.BlockSpec(memory_space=pl.ANY),
                  pl.BlockSpec(memory_space=pl.ANY)],
        out_specs=pl.BlockSpec((step_rows, LANES), lambda s, be, fr, ne, nv: (s, 0)),
        scratch_shapes=[pltpu.VMEM((2, D_MODEL, D_EXPERT), F32),
                        pltpu.VMEM((2, D_MODEL, D_EXPERT), F32),
                        pltpu.VMEM((2, D_EXPERT, D_MODEL), F32),
                        pltpu.VMEM((D_MODEL, D_EXPERT), BF16),
                        pltpu.VMEM((D_MODEL, D_EXPERT), BF16),
                        pltpu.VMEM((D_EXPERT, D_MODEL), BF16),
                        pltpu.SemaphoreType.DMA((2, 3))],
    )
    return pl.pallas_call(
        functools.partial(_expert_kernel, tb=tb),
        grid_spec=grid_spec,
        out_shape=jax.ShapeDtypeStruct((nb * tb * TOK_ROWS, LANES), jnp.uint32),
        compiler_params=_cparams("arbitrary"),
        name="experts",
    )(blk_e, first, next_e, nvalid, xs, w_gate, w_up, w_down)


def _combine_kernel(d1_ref, d2_ref, y_hbm, x1_ref, route_ref, g_ref, b_ref, o_ref, ybuf, sem, *, tt):
    i = pl.program_id(0)
    nsteps = pl.num_programs(0)
    slot = i % 2

    def issue(tile, slot_):
        def body(j, carry):
            for u in range(ROW_UNROLL):
                r = j * ROW_UNROLL + u
                for k, d_ref in enumerate((d1_ref, d2_ref)):
                    src = pl.multiple_of(d_ref[tile * tt + r], TOK_ROWS)
                    pltpu.make_async_copy(y_hbm.at[pl.ds(src, TOK_ROWS), :],
                                          ybuf.at[slot_, k, pl.ds(r * TOK_ROWS, TOK_ROWS), :],
                                          sem.at[slot_]).start(priority=k)
            return carry
        lax.fori_loop(0, tt // ROW_UNROLL, body, 0)

    @pl.when(i == 0)
    def _():
        issue(0, 0)

    @pl.when(i + 1 < nsteps)
    def _():
        issue(i + 1, 1 - slot)

    for k in range(TOP_K):
        pltpu.make_async_copy(y_hbm.at[pl.ds(0, tt * TOK_ROWS), :], ybuf.at[slot, k], sem.at[slot]).wait()
    route = route_ref[...]
    g1 = route[:, R_G1:R_G1 + 1]
    g2 = route[:, R_G2:R_G2 + 1]
    lo1, hi1 = _unpack_bf16_pairs(_load_token_rows(ybuf.at[slot, 0], tt))
    lo2, hi2 = _unpack_bf16_pairs(_load_token_rows(ybuf.at[slot, 1], tt))
    x_lo, x_hi = _unpack_bf16_pairs(_load_token_rows(x1_ref, tt))
    h_lo = ALPHA_DN * x_lo + (g1 * lo1 + g2 * lo2)
    h_hi = ALPHA_DN * x_hi + (g1 * hi1 + g2 * hi2)
    mu = (jnp.sum(h_lo, axis=-1, keepdims=True) + jnp.sum(h_hi, axis=-1, keepdims=True)) * (1.0 / D_MODEL)
    c_lo = h_lo - mu
    c_hi = h_hi - mu
    var = (jnp.sum(c_lo * c_lo, axis=-1, keepdims=True) + jnp.sum(c_hi * c_hi, axis=-1, keepdims=True)) * (1.0 / D_MODEL)
    inv = lax.rsqrt(var + LN_EPS)
    o_ref[:, 0:HALF] = c_lo * inv * g_ref[:, 0:HALF] + b_ref[:, 0:HALF]
    o_ref[:, HALF:] = c_hi * inv * g_ref[:, HALF:] + b_ref[:, HALF:]


def _combine(dest1, dest2, ybuf, x1p, route, ln_g, ln_b, *, tt=1024):
    n = x1p.shape[0] // TOK_ROWS
    tt = min(tt, n)
    grid_spec = pltpu.PrefetchScalarGridSpec(
        num_scalar_prefetch=2,
        grid=(n // tt,),
        in_specs=[pl.BlockSpec(memory_space=pl.ANY),
                  pl.BlockSpec((tt * TOK_ROWS, LANES), lambda i, a, b: (i, 0)),
                  pl.BlockSpec((tt, ROUTE_W), lambda i, a, b: (i, 0)),
                  pl.BlockSpec((1, D_MODEL), lambda i, a, b: (0, 0)),
                  pl.BlockSpec((1, D_MODEL), lambda i, a, b: (0, 0))],
        out_specs=pl.BlockSpec((tt, D_MODEL), lambda i, a, b: (i, 0)),
        scratch_shapes=[pltpu.VMEM((2, TOP_K, tt * TOK_ROWS, LANES), jnp.uint32),
                        pltpu.SemaphoreType.DMA((2,))],
    )
    return pl.pallas_call(
        functools.partial(_combine_kernel, tt=tt),
        grid_spec=grid_spec,
        out_shape=jax.ShapeDtypeStruct((n, D_MODEL), F32),
        compiler_params=_cparams("arbitrary"),
        name="combine",
    )(dest1, dest2, ybuf, x1p, route, ln_g, ln_b)


def _block_plan(counts_row, n, tb):
    counts = counts_row[:N_EXPERTS, 0].astype(jnp.int32)
    pends = jnp.cumsum((counts + tb - 1) // tb * tb)
    nb = (n * TOP_K) // tb + N_EXPERTS
    nvalid = (pends[-1] // tb).astype(jnp.int32)
    blk = jnp.arange(nb, dtype=jnp.int32)
    blk_e = jnp.sum((pends[None, :] <= (blk * tb)[:, None]).astype(jnp.int32), axis=1)
    blk_e = jnp.minimum(blk_e, N_EXPERTS - 1)
    blk_e = jnp.where(blk < nvalid, blk_e, blk_e[jnp.maximum(nvalid - 1, 0)])
    first = jnp.concatenate([jnp.ones((1,), jnp.int32), (blk_e[1:] != blk_e[:-1]).astype(jnp.int32)])
    slot_e = (jnp.cumsum((counts > 0).astype(jnp.int32)) - 1) % 2
    first = first * (1 + slot_e[blk_e])
    ids = jnp.arange(N_EXPERTS, dtype=jnp.int32)
    later = lax.cummin(jnp.where(counts > 0, ids, N_EXPERTS), axis=0, reverse=True)
    nxt = jnp.concatenate([later[1:], jnp.full((1,), N_EXPERTS, jnp.int32)])
    nxt = jnp.where(nxt < N_EXPERTS, nxt, -1)
    last_blk = jnp.where(counts > 0, pends // tb - 1, -1).astype(jnp.int32)
    return blk_e, first, nxt[blk_e], nvalid.reshape(1), last_blk


def kernel(x, w_in, w_gk_up, b_gk, conv_w, a_log, dt_bias, gla_norm_g, gdn_norm_g, w_out, ln1_g, ln1_b, w_router_group, b_router_group, w_router_expert, b_router_expert, w_gate, w_up, w_down, ln2_g, ln2_b):
    bsz, seq, d = x.shape
    n = bsz * seq
    x2d = x.reshape(n, d)
    w_main, w_aux = _split_w_in(w_in[0])
    main, aux = _in_proj(x2d, w_main, w_aux, conv_w[0].astype(F32), seq=seq)
    wgk = jnp.zeros((AUX_W, 256), F32).at[:GLA_GATE_RANK].set(w_gk_up[0])
    o_a = _gla(main, aux, wgk, b_gk[0][None], gla_norm_g[0][None], bsz=bsz, seq=seq)
    o_b = _gdn(main, aux, _pad_aux_row(a_log[0], AUX_AB), _pad_aux_row(dt_bias[0], AUX_AB),
               gdn_norm_g[0][None], bsz=bsz, seq=seq)
    rpad = jnp.zeros((ROUTE_W - N_EXPERTS - N_GROUPS, d), F32)
    w_router = jnp.concatenate([w_router_expert[0].T, w_router_group[0].T, rpad], axis=0)
    w_router_hi = w_router.astype(BF16)
    w_router = jnp.concatenate([w_router_hi, (w_router - w_router_hi.astype(F32)).astype(BF16)], axis=0)
    b_router = jnp.concatenate([b_router_expert[0], b_router_group[0], rpad[:, 0]])[:, None]
    x1p, route, gates, counts = _post_mix(o_a, o_b, x2d, w_out[0].astype(BF16), ln1_g[0][None], ln1_b[0][None],
                                          w_router, b_router)
    dest = _plan(route, counts, tb=EXPERT_BLOCK)
    blk_e, first, next_e, nvalid, last_blk = _block_plan(counts, n, EXPERT_BLOCK)
    xs = _dispatch(dest[0], dest[1], last_blk, nvalid, x1p, nb=blk_e.shape[0])
    ybuf = _experts(blk_e, first, next_e, nvalid, xs, w_gate[0], w_up[0], w_down[0])
    out = _combine(dest[0], dest[1], ybuf, x1p, gates, ln2_g[0][None], ln2_b[0][None])
    return out.reshape(bsz, seq, d)
```

```python
import functools

import jax
import jax.numpy as jnp
import numpy as np
from jax import lax
from jax.experimental import pallas as pl
from jax.experimental.pallas import tpu as pltpu

F32 = jnp.float32
BF16 = jnp.bfloat16
HIGHEST = lax.Precision.HIGHEST

D_MODEL = 1024
DEPTH = 1
GLA_HEADS = 4
GLA_DK = 64
GLA_DV = 128
GLA_WIDTH = GLA_HEADS * GLA_DV
GLA_GATE_RANK = 16
GLA_GATE_NORM = 16.0
GDN_HEADS = 4
GDN_DK = 128
GDN_DV = 128
GDN_WIDTH = GDN_HEADS * GDN_DV
CONV_W = 4
CHUNK = 64
N_GROUPS = 8
EXPERTS_PER_GROUP = 8
N_EXPERTS = N_GROUPS * EXPERTS_PER_GROUP
TOP_K = 2
D_EXPERT = 512
LN_EPS = 1e-5
RMS_EPS = 1e-6
ALPHA_DN = (2.0 * DEPTH) ** 0.25

MAIN_W = 2 * GLA_HEADS * GLA_DK + 2 * GLA_WIDTH + 4 * GDN_WIDTH
AUX_W = 128
AUX_LRA = 0
AUX_BB = GLA_GATE_RANK
AUX_AB = GLA_GATE_RANK + GDN_HEADS
SUB = 16
VMEM_LIMIT = 56 * 1024 * 1024


def _cparams(*sem):
    return pltpu.CompilerParams(dimension_semantics=sem, vmem_limit_bytes=VMEM_LIMIT)


def _sigmoid(x):
    return 1.0 / (1.0 + jnp.exp(-x))


def _silu(x):
    return x * _sigmoid(x)


def _log_sigmoid(x):
    return jnp.minimum(x, 0.0) - jnp.log(1.0 + jnp.exp(-jnp.abs(x)))


def _softplus(x):
    return jnp.maximum(x, 0.0) + jnp.log(1.0 + jnp.exp(-jnp.abs(x)))


def _chunk_cumsum(x, c):
    pos = lax.broadcasted_iota(jnp.int32, x.shape, 0) % c
    shift = 1
    while shift < c:
        x = x + jnp.where(pos >= shift, pltpu.roll(x, shift, 0), 0.0)
        shift *= 2
    return x


CONV_PAD = 8
GDN_HW = GDN_HEADS * GDN_DK
GDN_QKV_BLOCK0 = (2 * GLA_HEADS * GLA_DK + 2 * GLA_WIDTH) // GDN_HW


def _in_proj_kernel(x_ref, wm_ref, wa_ref, cw_ref, main_ref, aux_ref, stage_ref, *, tm, tiles_per_seq):
    xb = x_ref[...].astype(BF16)

    @pl.when((pl.program_id(0) % tiles_per_seq) == 0)
    def _():
        stage_ref[:, 0:CONV_PAD, :] = jnp.zeros((3 * GDN_HEADS, CONV_PAD, GDN_DK), F32)

    for j in range(MAIN_W // GDN_HW):
        sl = slice(j * GDN_HW, (j + 1) * GDN_HW)
        r = jnp.dot(xb, wm_ref[:, sl], preferred_element_type=F32)
        b = j - GDN_QKV_BLOCK0
        if not 0 <= b < 3:
            main_ref[:, sl] = r.astype(BF16)
            continue
        for h in range(GDN_HEADS):
            st = stage_ref.at[b * GDN_HEADS + h]
            cols = slice(b * GDN_HW + h * GDN_DK, b * GDN_HW + (h + 1) * GDN_DK)
            st[CONV_PAD:, :] = r[:, h * GDN_DK:(h + 1) * GDN_DK]
            acc = jnp.zeros((tm, GDN_DK), F32)
            for i in range(CONV_W):
                lo = CONV_PAD - (CONV_W - 1) + i
                acc = acc + st[lo:lo + tm, :] * cw_ref[i:i + 1, cols]
            st[0:CONV_PAD, :] = st[tm:tm + CONV_PAD, :]
            yh = _silu(acc)
            if b < 2:
                yh = yh * lax.rsqrt(jnp.sum(yh * yh, axis=-1, keepdims=True) + RMS_EPS)
            if b == 0:
                yh = yh * (GDN_DK ** -0.5)
            main_ref[:, j * GDN_HW + h * GDN_DK:j * GDN_HW + (h + 1) * GDN_DK] = yh.astype(BF16)
    aux_ref[...] = jnp.dot(xb, wa_ref[...], preferred_element_type=F32)


def _in_proj(x2d, w_main, w_aux, conv_w, *, seq, tm=512):
    n = x2d.shape[0]
    return pl.pallas_call(
        functools.partial(_in_proj_kernel, tm=tm, tiles_per_seq=seq // tm),
        grid=(n // tm,),
        in_specs=[pl.BlockSpec((tm, D_MODEL), lambda i: (i, 0)),
                  pl.BlockSpec((D_MODEL, MAIN_W), lambda i: (0, 0)),
                  pl.BlockSpec((D_MODEL, AUX_W), lambda i: (0, 0)),
                  pl.BlockSpec((CONV_W, 3 * GDN_HW), lambda i: (0, 0))],
        out_specs=[pl.BlockSpec((tm, MAIN_W), lambda i: (i, 0)),
                   pl.BlockSpec((tm, AUX_W), lambda i: (i, 0))],
        out_shape=[jax.ShapeDtypeStruct((n, MAIN_W), BF16),
                   jax.ShapeDtypeStruct((n, AUX_W), F32)],
        scratch_shapes=[pltpu.VMEM((3 * GDN_HEADS, tm + CONV_PAD, GDN_DK), F32)],
        compiler_params=_cparams("arbitrary"),
        name="in_proj",
    )(x2d, w_main, w_aux, conv_w)


def _split_w_in(w_in):
    sizes = (GLA_HEADS * GLA_DK, GLA_HEADS * GLA_DK, GLA_WIDTH, GLA_WIDTH, GLA_GATE_RANK,
             GDN_HEADS * GDN_DK, GDN_HEADS * GDN_DK, GDN_WIDTH, GDN_WIDTH, GDN_HEADS, GDN_HEADS)
    offs = np.cumsum((0,) + sizes)
    seg = [w_in[:, offs[i]:offs[i + 1]] for i in range(len(sizes))]
    qa, ka, va, ra, lra, qb, kb, vb, zb, bb, ab = seg
    w_main = jnp.concatenate([qa, ka, va, ra, qb, kb, vb, zb], axis=1).astype(BF16)
    pad = jnp.zeros((w_in.shape[0], AUX_W - GLA_GATE_RANK - 2 * GDN_HEADS), w_in.dtype)
    w_aux = jnp.concatenate([lra, bb, ab, pad], axis=1).astype(BF16)
    return w_main, w_aux


_NT = (((1,), (1,)), ((), ()))
_TN = (((0,), (0,)), ((), ()))


def _gla_kernel(q_ref, k_ref, v_ref, r_ref, aux_ref, wgk_ref, bgk_ref, g_ref, o_ref, st_ref, *, ts, nseq):
    npairs = 2 * nseq
    c = CHUNK
    nsub = c // SUB

    @pl.when(pl.program_id(1) == 0)
    def _():
        st_ref[...] = jnp.zeros_like(st_ref)

    gk_all = [_log_sigmoid(jnp.dot(aux_ref[sq], wgk_ref[...], precision=HIGHEST, preferred_element_type=F32)
                           + bgk_ref[...]) * (1.0 / GLA_GATE_NORM) for sq in range(nseq)]

    row128 = lax.broadcasted_iota(jnp.int32, (c, 128), 0)
    lane128 = lax.broadcasted_iota(jnp.int32, (1, 128), 1)
    lane_masks = [(lane128 // GLA_DK) == hh for hh in range(2)]
    sub_valid = [row128 < SUB * (i + 1) for i in range(nsub)]
    ar = lax.broadcasted_iota(jnp.int32, (c, nsub * c), 0)
    ac = lax.broadcasted_iota(jnp.int32, (c, nsub * c), 1)
    amask = ((ac // c) == (ar // SUB)) & ((ac % c) <= ar)
    scale = GLA_DK ** -0.5
    gnorm = g_ref[...]

    nch = ts // c
    rows = [slice(ch * c, (ch + 1) * c) for ch in range(nch)]
    b_cum = [_chunk_cumsum(g, c) for g in gk_all]

    cp = [(ch, p) for ch in range(nch) for p in range(npairs)]
    qt, qd, kd, kstack, dlast = {}, {}, {}, {}, {}
    for ch, p in cp:
        sq, pp = divmod(p, 2)
        lanes = slice(128 * pp, 128 * (pp + 1))
        bp = b_cum[sq][rows[ch], lanes]
        qp = q_ref[sq, rows[ch], lanes].astype(F32) * scale
        kp = k_ref[sq, rows[ch], lanes].astype(F32)
        c_row = jnp.concatenate(
            [jnp.broadcast_to(bp[SUB * i:SUB * i + 1, :], (SUB, 128)) for i in range(nsub)], axis=0)
        qt[ch, p] = (qp * jnp.exp(bp - c_row)).astype(BF16)
        qd[ch, p] = qp * jnp.exp(bp)
        b_last = bp[c - 1:c, :]
        dlast[ch, p] = jnp.exp(b_last)
        kd[ch, p] = kp * jnp.exp(b_last - bp)
        kts = []
        for i in range(nsub):
            e = jnp.exp(jnp.where(sub_valid[i], bp[SUB * i:SUB * i + 1, :] - bp, 0.0))
            kts.append(jnp.where(sub_valid[i], kp * e, 0.0))
        kstack[ch, p] = jnp.concatenate(kts, axis=0)

    cph = [(ch, p, hh) for ch, p in cp for hh in range(2)]
    vh = {(ch, p, hh): v_ref[p // 2, rows[ch], 128 * (2 * (p % 2) + hh):128 * (2 * (p % 2) + hh + 1)]
          for ch, p, hh in cph}
    r_mat = {(ch, p, hh): lax.dot_general(qt[ch, p], jnp.where(lane_masks[hh], kstack[ch, p], 0.0).astype(BF16),
                                          _NT, preferred_element_type=F32) for ch, p, hh in cph}
    o_intra = {k: jnp.dot(jnp.where(amask, r_mat[k], 0.0).astype(BF16), jnp.concatenate([vh[k]] * nsub, axis=0),
                          preferred_element_type=F32) for k in cph}
    upd = {(ch, p, hh): lax.dot_general(vh[ch, p, hh], jnp.where(lane_masks[hh], kd[ch, p], 0.0).astype(BF16), _TN,
                                        preferred_element_type=F32) for ch, p, hh in cph}

    st = [st_ref[p] for p in range(npairs)]
    for ch in range(nch):
        st_b = [s.astype(BF16) for s in st]
        o_pair = [lax.dot_general(
            jnp.concatenate([jnp.where(lane_masks[hh], qd[ch, p], 0.0).astype(BF16) for hh in range(2)], axis=0),
            st_b[p], _NT, preferred_element_type=F32) for p in range(npairs)]
        o_inter = {(p, hh): o_pair[p][hh * c:(hh + 1) * c] for p in range(npairs) for hh in range(2)}
        st = [st[p] * dlast[ch, p] + upd[ch, p, 0] + upd[ch, p, 1] for p in range(npairs)]
        for p in range(npairs):
            for hh in range(2):
                sq, h = p // 2, 2 * (p % 2) + hh
                o = o_intra[ch, p, hh] + o_inter[p, hh]
                o = o * lax.rsqrt(jnp.mean(o * o, axis=-1, keepdims=True) + RMS_EPS) * gnorm
                gate = _silu(r_ref[sq, rows[ch], 128 * h:128 * (h + 1)].astype(F32))
                o_ref[sq, rows[ch], 128 * h:128 * (h + 1)] = (o * gate).astype(BF16)
    for p in range(npairs):
        st_ref[p] = st[p]


def _gla(main, aux, wgk_pad, b_gk, gla_norm_g, *, bsz, seq, ts=512, nseq=2):
    nt = seq // ts
    assert bsz % nseq == 0
    main4 = main.reshape(bsz // nseq, nseq, seq, MAIN_W)
    aux4 = aux.reshape(bsz // nseq, nseq, seq, AUX_W)
    tok = lambda width, col: pl.BlockSpec((None, nseq, ts, width), lambda b, t: (b, 0, t, col))
    out = pl.pallas_call(
        functools.partial(_gla_kernel, ts=ts, nseq=nseq),
        grid=(bsz // nseq, nt),
        in_specs=[tok(256, 0), tok(256, 1), tok(512, 1), tok(512, 2),
                  tok(AUX_W, 0),
                  pl.BlockSpec((AUX_W, 256), lambda b, t: (0, 0)),
                  pl.BlockSpec((1, 256), lambda b, t: (0, 0)),
                  pl.BlockSpec((1, 128), lambda b, t: (0, 0))],
        out_specs=tok(GLA_WIDTH, 0),
        out_shape=jax.ShapeDtypeStruct((bsz // nseq, nseq, seq, GLA_WIDTH), BF16),
        scratch_shapes=[pltpu.VMEM((2 * nseq, 128, 128), F32)],
        compiler_params=_cparams("parallel", "arbitrary"),
        name="gla",
    )(main4, main4, main4, main4, aux4, wgk_pad, b_gk, gla_norm_g)
    return out.reshape(bsz * seq, GLA_WIDTH)


def _gdn_kernel(q_ref, k_ref, v_ref, z_ref, aux_ref, alog_ref, dtb_ref, g_ref, o_ref, st_ref, *, ts, nseq):
    c = CHUNK
    nch = ts // c

    @pl.when(pl.program_id(1) == 0)
    def _():
        st_ref[...] = jnp.zeros_like(st_ref)

    beta_full, gam_full, gam_t = [], [], []
    for sq in range(nseq):
        aux = aux_ref[sq]
        beta_full.append(_sigmoid(aux))
        g_full = -jnp.exp(alog_ref[...]) * _softplus(aux + dtb_ref[...])
        gam_full.append(_chunk_cumsum(g_full, c))
        gam_t.append(gam_full[-1].T)
    gnorm = g_ref[...]

    row = lax.broadcasted_iota(jnp.int32, (c, 2 * c), 0)
    lane = lax.broadcasted_iota(jnp.int32, (c, 2 * c), 1)
    left = lane < c
    col = lane % c
    causal = col <= row
    strict = col < row
    same32 = (row // 32) == (col // 32)
    same16 = (row // 16) == (col // 16)
    lvl0 = same16 & strict
    lvl1 = same32 & jnp.logical_not(same16) & strict
    lvl2 = jnp.logical_not(same32) & strict

    def pack_rows(x2):
        return jnp.where(left, x2[0:c], x2[c:2 * c])

    def pack_cols(x1):
        return jnp.where(left, x1[0:c], x1[c:2 * c])

    def halves(x):
        return jnp.where(left, x, 0.0).astype(BF16), jnp.where(left, 0.0, x).astype(BF16)

    def mmp(x, y):
        return jnp.dot(x.astype(BF16), jnp.concatenate(halves(y), axis=0), preferred_element_type=F32)

    heads = range(nseq * GDN_HEADS)
    npair = nch // 2
    items = [(h, pr) for h in heads for pr in range(npair)]
    prow = [slice(2 * c * pr, 2 * c * (pr + 1)) for pr in range(npair)]

    qn, kn, beta, gam, gam_r, kb, rhs, rhs_b, qd, kd = ([None] * len(heads) for _ in range(10))
    for h in heads:
        sq, hh = divmod(h, GDN_HEADS)
        hl = slice(128 * hh, 128 * (hh + 1))
        kb[h] = k_ref[sq, :, hl]
        qn[h] = q_ref[sq, :, hl].astype(F32)
        kn[h] = kb[h].astype(F32)
        vh = v_ref[sq, :, hl].astype(F32)
        beta[h] = beta_full[sq][:, AUX_BB + hh:AUX_BB + hh + 1]
        gam[h] = gam_full[sq][:, AUX_AB + hh:AUX_AB + hh + 1]
        gam_r[h] = gam_t[sq][AUX_AB + hh:AUX_AB + hh + 1, :]
        egam = jnp.exp(gam[h])
        rhs[h] = jnp.concatenate([beta[h] * vh, beta[h] * egam * kn[h]], axis=1)
        rhs_b[h] = rhs[h].astype(BF16)
        qd[h] = (qn[h] * egam).astype(BF16)
        gl_rows = jnp.concatenate(
            [jnp.broadcast_to(gam[h][c * (j + 1) - 1:c * (j + 1), :], (c, 1)) for j in range(nch)], axis=0)
        kd[h] = (kn[h] * jnp.exp(gl_rows - gam[h])).astype(BF16)

    kk = [pack_rows(lax.dot_general(kb[h][prow[pr]], kb[h][prow[pr]], _NT, preferred_element_type=F32))
          for h, pr in items]
    qk = [pack_rows(lax.dot_general(qn[h][prow[pr]].astype(BF16), kb[h][prow[pr]], _NT,
                                    preferred_element_type=F32)) for h, pr in items]
    dec = [jnp.exp(jnp.where(causal, pack_cols(gam[h][prow[pr]]) - gam_r[h][:, prow[pr]], 0.0)) for h, pr in items]
    a = [pack_cols(beta[h][prow[pr]]) * kk[i] * dec[i] for i, (h, pr) in enumerate(items)]
    attn = [halves(jnp.where(causal, qk[i] * dec[i], 0.0)) for i in range(len(items))]

    x1 = [jnp.where(lvl0, -ai, 0.0) for ai in a]
    x2 = [mmp(x, x) for x in x1]
    y = [mmp(x, xx) for x, xx in zip(x1, x2)]
    p = [x + xx + yy for x, xx, yy in zip(x1, x2, y)]
    xk = x2
    for _ in range(2):
        xk = [mmp(x, x) for x in xk]
        y = [mmp(pp, x) for pp, x in zip(p, xk)]
        p = [pp + x + yy for pp, x, yy in zip(p, xk, y)]
    for lvl in (lvl1, lvl2):
        low = [jnp.where(lvl, ai, 0.0) for ai in a]
        y = [mmp(pp, lo) for pp, lo in zip(p, low)]
        zz = [lo + yy for lo, yy in zip(low, y)]
        y = [mmp(z, pp) for z, pp in zip(zz, p)]
        p = [pp - z - yy for pp, z, yy in zip(p, zz, y)]

    us = [[None] * nch for _ in heads]
    ws = [[None] * nch for _ in heads]
    for i, (h, pr) in enumerate(items):
        for half, p_half in enumerate(halves(p[i])):
            j = 2 * pr + half
            uw = rhs[h][c * j:c * (j + 1)] + jnp.dot(p_half, rhs_b[h][prow[pr]], preferred_element_type=F32)
            us[h][j] = uw[:, :GDN_DV]
            ws[h][j] = uw[:, GDN_DV:].astype(BF16)

    s = [st_ref[h] for h in heads]
    vprev = [None] * len(heads)
    for j in range(nch):
        rows = slice(c * j, c * (j + 1))
        sb = [s[h].astype(BF16) for h in heads]
        ws_qd = [jnp.dot(jnp.concatenate([ws[h][j], qd[h][rows]], axis=0), sb[h], preferred_element_type=F32)
                 for h in heads]
        vn16 = [(us[h][j] - ws_qd[h][0:c]).astype(BF16) for h in heads]
        o_inter = [ws_qd[h][c:2 * c] for h in heads]
        for h in heads:
            vpair = jnp.concatenate([vn16[h], jnp.zeros_like(vn16[h])] if j % 2 == 0 else [vprev[h], vn16[h]], axis=0)
            o = o_inter[h] + jnp.dot(attn[h * npair + j // 2][j % 2], vpair, preferred_element_type=F32)
            s[h] = jnp.exp(gam[h][c * (j + 1) - 1:c * (j + 1), :]) * s[h] + lax.dot_general(
                kd[h][rows], vn16[h], _TN, preferred_element_type=F32)
            o = o * lax.rsqrt(jnp.mean(o * o, axis=-1, keepdims=True) + RMS_EPS) * gnorm
            sq, hh = divmod(h, GDN_HEADS)
            gate = _silu(z_ref[sq, rows, 128 * hh:128 * (hh + 1)].astype(F32))
            o_ref[sq, rows, 128 * hh:128 * (hh + 1)] = (o * gate).astype(BF16)
        vprev = vn16
    for h in heads:
        st_ref[h] = s[h]


def _gdn(main, aux, alog_pad, dtb_pad, gdn_norm_g, *, bsz, seq, ts=512, nseq=2):
    nt = seq // ts
    hw = GDN_HEADS * GDN_DK
    assert bsz % nseq == 0
    main4 = main.reshape(bsz // nseq, nseq, seq, MAIN_W)
    aux4 = aux.reshape(bsz // nseq, nseq, seq, AUX_W)
    tok = lambda col: pl.BlockSpec((None, nseq, ts, hw), lambda b, t: (b, 0, t, col))
    out = pl.pallas_call(
        functools.partial(_gdn_kernel, ts=ts, nseq=nseq),
        grid=(bsz // nseq, nt),
        in_specs=[tok(3), tok(4), tok(5), tok(6),
                  pl.BlockSpec((None, nseq, ts, AUX_W), lambda b, t: (b, 0, t, 0)),
                  pl.BlockSpec((1, AUX_W), lambda b, t: (0, 0)),
                  pl.BlockSpec((1, AUX_W), lambda b, t: (0, 0)),
                  pl.BlockSpec((1, GDN_DV), lambda b, t: (0, 0))],
        out_specs=pl.BlockSpec((None, nseq, ts, GDN_WIDTH), lambda b, t: (b, 0, t, 0)),
        out_shape=jax.ShapeDtypeStruct((bsz // nseq, nseq, seq, GDN_WIDTH), BF16),
        scratch_shapes=[pltpu.VMEM((nseq * GDN_HEADS, GDN_DK, GDN_DV), F32)],
        compiler_params=_cparams("parallel", "arbitrary"),
        name="gdn",
    )(main4, main4, main4, main4, aux4, alog_pad, dtb_pad, gdn_norm_g)
    return out.reshape(bsz * seq, GDN_WIDTH)


def _pad_aux_row(v, offset):
    return jnp.zeros((1, AUX_W), F32).at[0, offset:offset + v.shape[0]].set(v.astype(F32))


ROUTE_W = 128
R_E1, R_E2, R_RANK1, R_RANK2, R_G1, R_G2 = range(6)
ROUTER_GROUP_COL = N_EXPERTS
POST_MIX_SPLIT = 4


def _layer_norm(h, g, b):
    mu = jnp.mean(h, axis=-1, keepdims=True)
    hc = h - mu
    var = jnp.mean(hc * hc, axis=-1, keepdims=True)
    return hc * lax.rsqrt(var + LN_EPS) * g + b


def _pack_bf16_pairs(x):
    w = x.shape[1] // 2
    lo = lax.bitcast_convert_type(x[:, :w].astype(BF16).astype(F32), jnp.uint32)
    hi = lax.bitcast_convert_type(x[:, w:].astype(BF16).astype(F32), jnp.uint32)
    return (lo >> 16) | (hi & jnp.uint32(0xFFFF0000))


def _unpack_bf16_pairs(p):
    lo = lax.bitcast_convert_type(p << 16, F32)
    hi = lax.bitcast_convert_type(p & jnp.uint32(0xFFFF0000), F32)
    return lo, hi


HALF = D_MODEL // 2
LANES = 128
TOK_ROWS = HALF // LANES


def _store_token_rows(ref, packed):
    m = packed.shape[0]
    for c in range(TOK_ROWS):
        ref[pl.ds(c, m, stride=TOK_ROWS), :] = packed[:, LANES * c:LANES * (c + 1)]


def _load_token_rows(ref, m):
    return jnp.concatenate([ref[pl.ds(c, m, stride=TOK_ROWS), :] for c in range(TOK_ROWS)], axis=1)


def _post_mix_kernel(oa_ref, ob_ref, x_ref, wo_ref, g_ref, b_ref, wr_ref, br_ref,
                     x1p_ref, route_ref, gates_ref, cnt_ref, carry_ref, *, tm):
    @pl.when(pl.program_id(0) == 0)
    def _():
        carry_ref[...] = jnp.zeros_like(carry_ref)

    m = tm // POST_MIX_SPLIT
    groups = range(POST_MIX_SPLIT)
    rs = [slice(h * m, (h + 1) * m) for h in groups]
    y = [jnp.dot(oa_ref[rs[h], :], wo_ref[0:GLA_WIDTH, :], preferred_element_type=F32)
         + jnp.dot(ob_ref[rs[h], :], wo_ref[GLA_WIDTH:, :], preferred_element_type=F32) for h in groups]
    x1 = [_layer_norm(ALPHA_DN * x_ref[rs[h], :] + y[h], g_ref[...], b_ref[...]) for h in groups]
    for h in groups:
        _store_token_rows(x1p_ref.at[pl.ds(h * m * TOK_ROWS, m * TOK_ROWS), :], _pack_bf16_pairs(x1[h]))

    xh = [x.astype(BF16) for x in x1]
    xl = [(x - xb.astype(F32)).astype(BF16) for x, xb in zip(x1, xh)]
    hw = [lax.dot_general(wr_ref[...], xb, _NT, preferred_element_type=F32) for xb in xh]
    lw = [lax.dot_general(wr_ref[0:ROUTE_W, :], xb, _NT, preferred_element_type=F32) for xb in xl]
    logits = [a[:ROUTE_W] + a[ROUTE_W:] + b + br_ref[...] for a, b in zip(hw, lw)]
    sub = lax.broadcasted_iota(jnp.int32, (ROUTE_W, m), 0)
    big = jnp.int32(1 << 20)
    neg = jnp.float32(-jnp.inf)

    def first_argmax(vals):
        mx = jnp.max(vals, axis=0, keepdims=True)
        idx = jnp.min(jnp.where(vals == mx, sub, big), axis=0, keepdims=True)
        return mx, idx

    is_group = (sub >= ROUTER_GROUP_COL) & (sub < ROUTER_GROUP_COL + N_GROUPS)
    gl = [jnp.where(is_group, lg, neg) for lg in logits]
    gtop = [first_argmax(v) for v in gl]
    p_group = [1.0 / jnp.sum(jnp.exp(v - t[0]), axis=0, keepdims=True) for v, t in zip(gl, gtop)]
    el = [jnp.where((sub // EXPERTS_PER_GROUP) == (t[1] - ROUTER_GROUP_COL), lg, neg) for lg, t in zip(logits, gtop)]
    top1 = [first_argmax(v) for v in el]
    top2 = [first_argmax(jnp.where(sub == t[1], neg, v)) for v, t in zip(el, top1)]
    ex = [jnp.exp(b[0] - a[0]) for a, b in zip(top1, top2)]
    w1 = [1.0 / (1.0 + e) for e in ex]
    gate1 = [p * w for p, w in zip(p_group, w1)]
    gate2 = [p * (e * w) for p, e, w in zip(p_group, ex, w1)]

    oh1 = [sub == t[1] for t in top1]
    oh2 = [sub == t[1] for t in top2]
    ri = lax.broadcasted_iota(jnp.int32, (m, m), 0)
    ci = lax.broadcasted_iota(jnp.int32, (m, m), 1)
    ustrict = (ri < ci).astype(BF16)
    oh1f = [o.astype(F32) for o in oh1]
    oh2f = [o.astype(F32) for o in oh2]
    c1 = [jnp.dot(o.astype(BF16), ustrict, preferred_element_type=F32) for o in oh1f]
    c2 = [jnp.dot(o.astype(BF16), ustrict, preferred_element_type=F32) for o in oh2f]
    tot1 = [jnp.sum(o, axis=1, keepdims=True) for o in oh1f]
    tot2 = [jnp.sum(o, axis=1, keepdims=True) for o in oh2f]
    carry = carry_ref[...]
    eye8 = jnp.eye(8, ROUTE_W, dtype=F32)
    zero = jnp.zeros((1, m), F32)
    for h in groups:
        rank1 = jnp.sum(jnp.where(oh1[h], c1[h] + carry, 0.0), axis=0, keepdims=True)
        rank2 = jnp.sum(jnp.where(oh2[h], c2[h] + (carry + tot1[h]), 0.0), axis=0, keepdims=True)
        carry = carry + tot1[h] + tot2[h]
        route_t = jnp.concatenate([top1[h][1].astype(F32), top2[h][1].astype(F32), rank1, rank2,
                                   gate1[h], gate2[h], zero, zero], axis=0)
        route_ref[:, rs[h]] = route_t
        gates_ref[rs[h], :] = lax.dot_general(route_t, eye8, _TN, precision=HIGHEST, preferred_element_type=F32)
    carry_ref[...] = carry
    cnt_ref[...] = carry


def _post_mix(o_a, o_b, x2d, w_out_b, ln_g, ln_b, w_router, b_router, *, tm=1024):
    n = x2d.shape[0]
    return pl.pallas_call(
        functools.partial(_post_mix_kernel, tm=tm),
        grid=(n // tm,),
        in_specs=[pl.BlockSpec((tm, GLA_WIDTH), lambda i: (i, 0)),
                  pl.BlockSpec((tm, GDN_WIDTH), lambda i: (i, 0)),
                  pl.BlockSpec((tm, D_MODEL), lambda i: (i, 0)),
                  pl.BlockSpec((GLA_WIDTH + GDN_WIDTH, D_MODEL), lambda i: (0, 0)),
                  pl.BlockSpec((1, D_MODEL), lambda i: (0, 0)),
                  pl.BlockSpec((1, D_MODEL), lambda i: (0, 0)),
                  pl.BlockSpec((2 * ROUTE_W, D_MODEL), lambda i: (0, 0)),
                  pl.BlockSpec((ROUTE_W, 1), lambda i: (0, 0))],
        out_specs=[pl.BlockSpec((tm * TOK_ROWS, LANES), lambda i: (i, 0)),
                   pl.BlockSpec((8, tm), lambda i: (0, i)),
                   pl.BlockSpec((tm, ROUTE_W), lambda i: (i, 0)),
                   pl.BlockSpec((ROUTE_W, 1), lambda i: (0, 0))],
        out_shape=[jax.ShapeDtypeStruct((n * TOK_ROWS, LANES), jnp.uint32),
                   jax.ShapeDtypeStruct((8, n), F32),
                   jax.ShapeDtypeStruct((n, ROUTE_W), F32),
                   jax.ShapeDtypeStruct((ROUTE_W, 1), F32)],
        scratch_shapes=[pltpu.VMEM((ROUTE_W, 1), F32)],
        compiler_params=_cparams("arbitrary"),
        name="post_mix",
    )(o_a, o_b, x2d, w_out_b, ln_g, ln_b, w_router, b_router)


EXPERT_BLOCK = 256
BLOCKS_PER_STEP = 4

ROW_UNROLL = 8


def _plan_kernel(route_ref, cnt_ref, dest_ref, *, tm, tb):
    counts = cnt_ref[...].astype(jnp.int32)
    padded = jnp.broadcast_to(((counts + (tb - 1)) & jnp.int32(-tb)).astype(F32), (ROUTE_W, ROUTE_W))
    pstart = (_chunk_cumsum(padded, ROUTE_W) - padded)[:, 0:1]
    route = route_ref[...]
    sub = lax.broadcasted_iota(jnp.int32, (ROUTE_W, tm), 0).astype(F32)
    rows = []
    for ce, cr in ((R_E1, R_RANK1), (R_E2, R_RANK2)):
        off = jnp.sum(jnp.where(sub == route[ce:ce + 1, :], pstart, 0.0), axis=0, keepdims=True)
        rows.append(off + route[cr:cr + 1, :])
    rows.extend([jnp.zeros((1, tm), F32)] * (8 - len(rows)))
    dest_ref[...] = jnp.concatenate(rows, axis=0).astype(jnp.int32) * TOK_ROWS


def _plan(route, counts, *, tb, tm=2048):
    n = route.shape[1]
    tm = min(tm, n)
    return pl.pallas_call(
        functools.partial(_plan_kernel, tm=tm, tb=tb),
        grid=(n // tm,),
        in_specs=[pl.BlockSpec((8, tm), lambda i: (0, i)),
                  pl.BlockSpec((ROUTE_W, 1), lambda i: (0, 0))],
        out_specs=pl.BlockSpec((8, tm), lambda i: (0, i)),
        out_shape=jax.ShapeDtypeStruct((8, n), jnp.int32),
        compiler_params=_cparams("parallel"),
        name="plan",
    )(route, counts)


def _dispatch_kernel(d1_ref, d2_ref, last_blk_ref, nvalid_ref, xp_ref, xs_ref, zbuf, sem, zsem, *, tm, tb, nb):
    base = pl.program_id(0) * tm
    rows = tb * TOK_ROWS

    @pl.when(pl.program_id(0) == 0)
    def _():
        zbuf[...] = jnp.zeros_like(zbuf)

        def zero_copy(blk):
            return pltpu.make_async_copy(zbuf, xs_ref.at[pl.ds(pl.multiple_of(blk * rows, rows), rows), :], zsem)

        def for_each_block(fn):
            def per_expert(e, carry):
                @pl.when(last_blk_ref[e] >= 0)
                def _():
                    fn(zero_copy(last_blk_ref[e]))
                return carry

            def per_tail(blk, carry):
                fn(zero_copy(blk))
                return carry
            lax.fori_loop(0, N_EXPERTS, per_expert, 0)
            lax.fori_loop(nvalid_ref[0], nb, per_tail, 0)

        for_each_block(lambda cp: cp.start())
        for_each_block(lambda cp: cp.wait())

    def body(j, carry):
        for u in range(ROW_UNROLL):
            r = j * ROW_UNROLL + u
            for k, d_ref in enumerate((d1_ref, d2_ref)):
                dst = pl.multiple_of(d_ref[base + r], TOK_ROWS)
                pltpu.make_async_copy(xp_ref.at[pl.ds(r * TOK_ROWS, TOK_ROWS), :],
                                      xs_ref.at[pl.ds(dst, TOK_ROWS), :], sem).start(priority=k)
        return carry
    lax.fori_loop(0, tm // ROW_UNROLL, body, 0)
    for _ in range(TOP_K):
        pltpu.make_async_copy(xp_ref, xs_ref.at[pl.ds(0, tm * TOK_ROWS), :], sem).wait()


def _dispatch(dest1, dest2, last_blk, nvalid, x1p, *, nb, tb=EXPERT_BLOCK, tm=2048):
    n = x1p.shape[0] // TOK_ROWS
    tm = min(tm, n)
    grid_spec = pltpu.PrefetchScalarGridSpec(
        num_scalar_prefetch=4,
        grid=(n // tm,),
        in_specs=[pl.BlockSpec((tm * TOK_ROWS, LANES), lambda i, a, b, lb, nv: (i, 0))],
        out_specs=pl.BlockSpec(memory_space=pl.ANY),
        scratch_shapes=[pltpu.VMEM((tb * TOK_ROWS, LANES), jnp.uint32),
                        pltpu.SemaphoreType.DMA(()),
                        pltpu.SemaphoreType.DMA(())],
    )
    return pl.pallas_call(
        functools.partial(_dispatch_kernel, tm=tm, tb=tb, nb=nb),
        grid_spec=grid_spec,
        out_shape=jax.ShapeDtypeStruct((nb * tb * TOK_ROWS, LANES), jnp.uint32),
        compiler_params=_cparams("arbitrary"),
        name="dispatch",
    )(dest1, dest2, last_blk, nvalid, x1p)


def _expert_kernel(blk_e_ref, first_ref, next_e_ref, nvalid_ref, xs_ref, wg_hbm, wu_hbm, wd_hbm, y_ref,
                   wgs, wus, wds, wgb, wub, wdb, sem, *, tb):
    rows = tb * TOK_ROWS

    def weight_copies(e, slot):
        return [pltpu.make_async_copy(src.at[e], dst.at[slot], sem.at[slot, j])
                for j, (src, dst) in enumerate(((wg_hbm, wgs), (wu_hbm, wus), (wd_hbm, wds)))]

    @pl.when((pl.program_id(0) == 0) & (nvalid_ref[0] > 0))
    def _():
        for cp in weight_copies(blk_e_ref[0], first_ref[0] - 1):
            cp.start()

    def run(i, sub, nblk):
        @pl.when(first_ref[i] > 0)
        def _():
            slot = first_ref[i] - 1

            @pl.when(next_e_ref[i] >= 0)
            def _():
                for cp in weight_copies(next_e_ref[i], 1 - slot):
                    cp.start()

            for cp in weight_copies(blk_e_ref[i], slot):
                cp.wait()
            wgb[...] = wgs[slot].astype(BF16)
            wub[...] = wus[slot].astype(BF16)
            wdb[...] = wds[slot].astype(BF16)
            mlp(sub, nblk)

        @pl.when(first_ref[i] == 0)
        def _():
            mlp(sub, nblk)

    def mlp(sub, nblk):
        m = nblk * tb
        span = pl.ds(sub * rows, nblk * rows)
        lo, hi = _unpack_bf16_pairs(_load_token_rows(xs_ref.at[span, :], m))
        xb = jnp.concatenate([lo.astype(BF16), hi.astype(BF16)], axis=1)
        hg = jnp.dot(xb, wgb[...], preferred_element_type=F32)
        hu = jnp.dot(xb, wub[...], preferred_element_type=F32)
        hid = (_silu(hg) * hu).astype(BF16)
        _store_token_rows(y_ref.at[span, :], _pack_bf16_pairs(jnp.dot(hid, wdb[...], preferred_element_type=F32)))

    nvalid = nvalid_ref[0]
    for sub in range(0, BLOCKS_PER_STEP, 2):
        i0 = pl.program_id(0) * BLOCKS_PER_STEP + sub
        i1 = i0 + 1
        paired = (i1 < nvalid) & (first_ref[i1] == 0)

        @pl.when(paired)
        def _():
            run(i0, sub, 2)

        @pl.when(jnp.logical_not(paired))
        def _():
            @pl.when(i0 < nvalid)
            def _():
                run(i0, sub, 1)

            @pl.when(i1 < nvalid)
            def _():
                run(i1, sub + 1, 1)

        for s_, i in ((sub, i0), (sub + 1, i1)):
            @pl.when(i >= nvalid)
            def _():
                y_ref[pl.ds(s_ * rows, rows), :] = jnp.zeros((rows, LANES), jnp.uint32)


def _experts(blk_e, first, next_e, nvalid, xs, w_gate, w_up, w_down, *, tb=EXPERT_BLOCK):
    nb = blk_e.shape[0]
    assert nb % BLOCKS_PER_STEP == 0
    step_rows = BLOCKS_PER_STEP * tb * TOK_ROWS
    last_step = lambda nv: jnp.maximum((nv[0] - 1) // BLOCKS_PER_STEP, 0)
    grid_spec = pltpu.PrefetchScalarGridSpec(
        num_scalar_prefetch=4,
        grid=(nb // BLOCKS_PER_STEP,),
        in_specs=[pl.BlockSpec((step_rows, LANES), lambda s, be, fr, ne, nv: (jnp.minimum(s, last_step(nv)), 0)),
                  pl.BlockSpec(memory_space=pl.ANY),
                  pl.BlockSpec(memory_space=pl.ANY),
                  pl.BlockSpec(memory_space=pl.ANY)],
        out_specs=pl.BlockSpec((step_rows, LANES), lambda s, be, fr, ne, nv: (s, 0)),
        scratch_shapes=[pltpu.VMEM((2, D_MODEL, D_EXPERT), F32),
                        pltpu.VMEM((2, D_MODEL, D_EXPERT), F32),
                        pltpu.VMEM((2, D_EXPERT, D_MODEL), F32),
                        pltpu.VMEM((D_MODEL, D_EXPERT), BF16),
                        pltpu.VMEM((D_MODEL, D_EXPERT), BF16),
                        pltpu.VMEM((D_EXPERT, D_MODEL), BF16),
                        pltpu.SemaphoreType.DMA((2, 3))],
    )
    return pl.pallas_call(
        functools.partial(_expert_kernel, tb=tb),
        grid_spec=grid_spec,
        out_shape=jax.ShapeDtypeStruct((nb * tb * TOK_ROWS, LANES), jnp.uint32),
        compiler_params=_cparams("arbitrary"),
        name="experts",
    )(blk_e, first, next_e, nvalid, xs, w_gate, w_up, w_down)


def _combine_kernel(d1_ref, d2_ref, y_hbm, x1_ref, route_ref, g_ref, b_ref, o_ref, ybuf, sem, *, tt):
    i = pl.program_id(0)
    nsteps = pl.num_programs(0)
    slot = i % 2

    def issue(tile, slot_):
        def body(j, carry):
            for u in range(ROW_UNROLL):
                r = j * ROW_UNROLL + u
                for k, d_ref in enumerate((d1_ref, d2_ref)):
                    src = pl.multiple_of(d_ref[tile * tt + r], TOK_ROWS)
                    pltpu.make_async_copy(y_hbm.at[pl.ds(src, TOK_ROWS), :],
                                          ybuf.at[slot_, k, pl.ds(r * TOK_ROWS, TOK_ROWS), :],
                                          sem.at[slot_]).start(priority=k)
            return carry
        lax.fori_loop(0, tt // ROW_UNROLL, body, 0)

    @pl.when(i == 0)
    def _():
        issue(0, 0)

    @pl.when(i + 1 < nsteps)
    def _():
        issue(i + 1, 1 - slot)

    for k in range(TOP_K):
        pltpu.make_async_copy(y_hbm.at[pl.ds(0, tt * TOK_ROWS), :], ybuf.at[slot, k], sem.at[slot]).wait()
    route = route_ref[...]
    g1 = route[:, R_G1:R_G1 + 1]
    g2 = route[:, R_G2:R_G2 + 1]
    lo1, hi1 = _unpack_bf16_pairs(_load_token_rows(ybuf.at[slot, 0], tt))
    lo2, hi2 = _unpack_bf16_pairs(_load_token_rows(ybuf.at[slot, 1], tt))
    x_lo, x_hi = _unpack_bf16_pairs(_load_token_rows(x1_ref, tt))
    h_lo = ALPHA_DN * x_lo + (g1 * lo1 + g2 * lo2)
    h_hi = ALPHA_DN * x_hi + (g1 * hi1 + g2 * hi2)
    mu = (jnp.sum(h_lo, axis=-1, keepdims=True) + jnp.sum(h_hi, axis=-1, keepdims=True)) * (1.0 / D_MODEL)
    c_lo = h_lo - mu
    c_hi = h_hi - mu
    var = (jnp.sum(c_lo * c_lo, axis=-1, keepdims=True) + jnp.sum(c_hi * c_hi, axis=-1, keepdims=True)) * (1.0 / D_MODEL)
    inv = lax.rsqrt(var + LN_EPS)
    o_ref[:, 0:HALF] = c_lo * inv * g_ref[:, 0:HALF] + b_ref[:, 0:HALF]
    o_ref[:, HALF:] = c_hi * inv * g_ref[:, HALF:] + b_ref[:, HALF:]


def _combine(dest1, dest2, ybuf, x1p, route, ln_g, ln_b, *, tt=512):
    n = x1p.shape[0] // TOK_ROWS
    tt = min(tt, n)
    grid_spec = pltpu.PrefetchScalarGridSpec(
        num_scalar_prefetch=2,
        grid=(n // tt,),
        in_specs=[pl.BlockSpec(memory_space=pl.ANY),
                  pl.BlockSpec((tt * TOK_ROWS, LANES), lambda i, a, b: (i, 0)),
                  pl.BlockSpec((tt, ROUTE_W), lambda i, a, b: (i, 0)),
                  pl.BlockSpec((1, D_MODEL), lambda i, a, b: (0, 0)),
                  pl.BlockSpec((1, D_MODEL), lambda i, a, b: (0, 0))],
        out_specs=pl.BlockSpec((tt, D_MODEL), lambda i, a, b: (i, 0)),
        scratch_shapes=[pltpu.VMEM((2, TOP_K, tt * TOK_ROWS, LANES), jnp.uint32),
                        pltpu.SemaphoreType.DMA((2,))],
    )
    return pl.pallas_call(
        functools.partial(_combine_kernel, tt=tt),
        grid_spec=grid_spec,
        out_shape=jax.ShapeDtypeStruct((n, D_MODEL), F32),
        compiler_params=_cparams("arbitrary"),
        name="combine",
    )(dest1, dest2, ybuf, x1p, route, ln_g, ln_b)


def _block_plan(counts_row, n, tb):
    counts = counts_row[:N_EXPERTS, 0].astype(jnp.int32)
    pends = jnp.cumsum((counts + tb - 1) // tb * tb)
    nb = (n * TOP_K) // tb + N_EXPERTS
    nvalid = (pends[-1] // tb).astype(jnp.int32)
    blk = jnp.arange(nb, dtype=jnp.int32)
    blk_e = jnp.sum((pends[None, :] <= (blk * tb)[:, None]).astype(jnp.int32), axis=1)
    blk_e = jnp.minimum(blk_e, N_EXPERTS - 1)
    blk_e = jnp.where(blk < nvalid, blk_e, blk_e[jnp.maximum(nvalid - 1, 0)])
    first = jnp.concatenate([jnp.ones((1,), jnp.int32), (blk_e[1:] != blk_e[:-1]).astype(jnp.int32)])
    slot_e = (jnp.cumsum((counts > 0).astype(jnp.int32)) - 1) % 2
    first = first * (1 + slot_e[blk_e])
    ids = jnp.arange(N_EXPERTS, dtype=jnp.int32)
    later = lax.cummin(jnp.where(counts > 0, ids, N_EXPERTS), axis=0, reverse=True)
    nxt = jnp.concatenate([later[1:], jnp.full((1,), N_EXPERTS, jnp.int32)])
    nxt = jnp.where(nxt < N_EXPERTS, nxt, -1)
    last_blk = jnp.where(counts > 0, pends // tb - 1, -1).astype(jnp.int32)
    return blk_e, first, nxt[blk_e], nvalid.reshape(1), last_blk


def kernel(x, w_in, w_gk_up, b_gk, conv_w, a_log, dt_bias, gla_norm_g, gdn_norm_g, w_out, ln1_g, ln1_b, w_router_group, b_router_group, w_router_expert, b_router_expert, w_gate, w_up, w_down, ln2_g, ln2_b):
    bsz, seq, d = x.shape
    n = bsz * seq
    x2d = x.reshape(n, d)
    w_main, w_aux = _split_w_in(w_in[0])
    main, aux = _in_proj(x2d, w_main, w_aux, conv_w[0].astype(F32), seq=seq)
    wgk = jnp.zeros((AUX_W, 256), F32).at[:GLA_GATE_RANK].set(w_gk_up[0])
    o_a = _gla(main, aux, wgk, b_gk[0][None], gla_norm_g[0][None], bsz=bsz, seq=seq)
    o_b = _gdn(main, aux, _pad_aux_row(a_log[0], AUX_AB), _pad_aux_row(dt_bias[0], AUX_AB),
               gdn_norm_g[0][None], bsz=bsz, seq=seq)
    rpad = jnp.zeros((ROUTE_W - N_EXPERTS - N_GROUPS, d), F32)
    w_router = jnp.concatenate([w_router_expert[0].T, w_router_group[0].T, rpad], axis=0)
    w_router_hi = w_router.astype(BF16)
    w_router = jnp.concatenate([w_router_hi, (w_router - w_router_hi.astype(F32)).astype(BF16)], axis=0)
    b_router = jnp.concatenate([b_router_expert[0], b_router_group[0], rpad[:, 0]])[:, None]
    x1p, route, gates, counts = _post_mix(o_a, o_b, x2d, w_out[0].astype(BF16), ln1_g[0][None], ln1_b[0][None],
                                          w_router, b_router)
    dest = _plan(route, counts, tb=EXPERT_BLOCK)
    blk_e, first, next_e, nvalid, last_blk = _block_plan(counts, n, EXPERT_BLOCK)
    xs = _dispatch(dest[0], dest[1], last_blk, nvalid, x1p, nb=blk_e.shape[0])
    ybuf = _experts(blk_e, first, next_e, nvalid, xs, w_gate[0], w_up[0], w_down[0])
    out = _combine(dest[0], dest[1], ybuf, x1p, gates, ln2_g[0][None], ln2_b[0][None])
    return out.reshape(bsz, seq, d)
```

```python
import functools

import jax
import jax.numpy as jnp
import numpy as np
from jax import lax
from jax.experimental import pallas as pl
from jax.experimental.pallas import tpu as pltpu

F32 = jnp.float32
BF16 = jnp.bfloat16
HIGHEST = lax.Precision.HIGHEST

D_MODEL = 1024
DEPTH = 1
GLA_HEADS = 4
GLA_DK = 64
GLA_DV = 128
GLA_WIDTH = GLA_HEADS * GLA_DV
GLA_GATE_RANK = 16
GLA_GATE_NORM = 16.0
GDN_HEADS = 4
GDN_DK = 128
GDN_DV = 128
GDN_WIDTH = GDN_HEADS * GDN_DV
CONV_W = 4
CHUNK = 64
N_GROUPS = 8
EXPERTS_PER_GROUP = 8
N_EXPERTS = N_GROUPS * EXPERTS_PER_GROUP
TOP_K = 2
D_EXPERT = 512
LN_EPS = 1e-5
RMS_EPS = 1e-6
ALPHA_DN = (2.0 * DEPTH) ** 0.25

MAIN_W = 2 * GLA_HEADS * GLA_DK + 2 * GLA_WIDTH + 4 * GDN_WIDTH
AUX_W = 128
AUX_LRA = 0
AUX_BB = GLA_GATE_RANK
AUX_AB = GLA_GATE_RANK + GDN_HEADS
SUB = 16
VMEM_LIMIT = 56 * 1024 * 1024


def _cparams(*sem):
    return pltpu.CompilerParams(dimension_semantics=sem, vmem_limit_bytes=VMEM_LIMIT)


def _sigmoid(x):
    return 1.0 / (1.0 + jnp.exp(-x))


def _silu(x):
    return x * _sigmoid(x)


def _log_sigmoid(x):
    return jnp.minimum(x, 0.0) - jnp.log(1.0 + jnp.exp(-jnp.abs(x)))


def _softplus(x):
    return jnp.maximum(x, 0.0) + jnp.log(1.0 + jnp.exp(-jnp.abs(x)))


def _chunk_cumsum(x, c):
    pos = lax.broadcasted_iota(jnp.int32, x.shape, 0) % c
    shift = 1
    while shift < c:
        x = x + jnp.where(pos >= shift, pltpu.roll(x, shift, 0), 0.0)
        shift *= 2
    return x


CONV_PAD = 8
GDN_HW = GDN_HEADS * GDN_DK
GDN_QKV_BLOCK0 = (2 * GLA_HEADS * GLA_DK + 2 * GLA_WIDTH) // GDN_HW


def _in_proj_kernel(x_ref, wm_ref, wa_ref, cw_ref, main_ref, aux_ref, stage_ref, *, tm, tiles_per_seq):
    xb = x_ref[...].astype(BF16)

    @pl.when((pl.program_id(0) % tiles_per_seq) == 0)
    def _():
        stage_ref[:, 0:CONV_PAD, :] = jnp.zeros((3 * GDN_HEADS, CONV_PAD, GDN_DK), F32)

    for j in range(MAIN_W // GDN_HW):
        sl = slice(j * GDN_HW, (j + 1) * GDN_HW)
        r = jnp.dot(xb, wm_ref[:, sl], preferred_element_type=F32)
        b = j - GDN_QKV_BLOCK0
        if not 0 <= b < 3:
            main_ref[:, sl] = r.astype(BF16)
            continue
        for h in range(GDN_HEADS):
            st = stage_ref.at[b * GDN_HEADS + h]
            cols = slice(b * GDN_HW + h * GDN_DK, b * GDN_HW + (h + 1) * GDN_DK)
            st[CONV_PAD:, :] = r[:, h * GDN_DK:(h + 1) * GDN_DK]
            acc = jnp.zeros((tm, GDN_DK), F32)
            for i in range(CONV_W):
                lo = CONV_PAD - (CONV_W - 1) + i
                acc = acc + st[lo:lo + tm, :] * cw_ref[i:i + 1, cols]
            st[0:CONV_PAD, :] = st[tm:tm + CONV_PAD, :]
            yh = _silu(acc)
            if b < 2:
                yh = yh * lax.rsqrt(jnp.sum(yh * yh, axis=-1, keepdims=True) + RMS_EPS)
            if b == 0:
                yh = yh * (GDN_DK ** -0.5)
            main_ref[:, j * GDN_HW + h * GDN_DK:j * GDN_HW + (h + 1) * GDN_DK] = yh.astype(BF16)
    aux_ref[...] = jnp.dot(xb, wa_ref[...], preferred_element_type=F32)


def _in_proj(x2d, w_main, w_aux, conv_w, *, seq, tm=512):
    n = x2d.shape[0]
    return pl.pallas_call(
        functools.partial(_in_proj_kernel, tm=tm, tiles_per_seq=seq // tm),
        grid=(n // tm,),
        in_specs=[pl.BlockSpec((tm, D_MODEL), lambda i: (i, 0)),
                  pl.BlockSpec((D_MODEL, MAIN_W), lambda i: (0, 0)),
                  pl.BlockSpec((D_MODEL, AUX_W), lambda i: (0, 0)),
                  pl.BlockSpec((CONV_W, 3 * GDN_HW), lambda i: (0, 0))],
        out_specs=[pl.BlockSpec((tm, MAIN_W), lambda i: (i, 0)),
                   pl.BlockSpec((tm, AUX_W), lambda i: (i, 0))],
        out_shape=[jax.ShapeDtypeStruct((n, MAIN_W), BF16),
                   jax.ShapeDtypeStruct((n, AUX_W), F32)],
        scratch_shapes=[pltpu.VMEM((3 * GDN_HEADS, tm + CONV_PAD, GDN_DK), F32)],
        compiler_params=_cparams("arbitrary"),
        name="in_proj",
    )(x2d, w_main, w_aux, conv_w)


def _split_w_in(w_in):
    sizes = (GLA_HEADS * GLA_DK, GLA_HEADS * GLA_DK, GLA_WIDTH, GLA_WIDTH, GLA_GATE_RANK,
             GDN_HEADS * GDN_DK, GDN_HEADS * GDN_DK, GDN_WIDTH, GDN_WIDTH, GDN_HEADS, GDN_HEADS)
    offs = np.cumsum((0,) + sizes)
    seg = [w_in[:, offs[i]:offs[i + 1]] for i in range(len(sizes))]
    qa, ka, va, ra, lra, qb, kb, vb, zb, bb, ab = seg
    w_main = jnp.concatenate([qa, ka, va, ra, qb, kb, vb, zb], axis=1).astype(BF16)
    pad = jnp.zeros((w_in.shape[0], AUX_W - GLA_GATE_RANK - 2 * GDN_HEADS), w_in.dtype)
    w_aux = jnp.concatenate([lra, bb, ab, pad], axis=1).astype(BF16)
    return w_main, w_aux


_NT = (((1,), (1,)), ((), ()))
_TN = (((0,), (0,)), ((), ()))


def _gla_kernel(q_ref, k_ref, v_ref, r_ref, aux_ref, wgk_ref, bgk_ref, g_ref, o_ref, st_ref, *, ts, nseq):
    npairs = 2 * nseq
    c = CHUNK
    nsub = c // SUB

    @pl.when(pl.program_id(1) == 0)
    def _():
        st_ref[...] = jnp.zeros_like(st_ref)

    gk_all = [_log_sigmoid(jnp.dot(aux_ref[sq], wgk_ref[...], precision=HIGHEST, preferred_element_type=F32)
                           + bgk_ref[...]) * (1.0 / GLA_GATE_NORM) for sq in range(nseq)]

    row128 = lax.broadcasted_iota(jnp.int32, (c, 128), 0)
    lane128 = lax.broadcasted_iota(jnp.int32, (1, 128), 1)
    lane_masks = [(lane128 // GLA_DK) == hh for hh in range(2)]
    sub_valid = [row128 < SUB * (i + 1) for i in range(nsub)]
    ar = lax.broadcasted_iota(jnp.int32, (c, nsub * c), 0)
    ac = lax.broadcasted_iota(jnp.int32, (c, nsub * c), 1)
    amask = ((ac // c) == (ar // SUB)) & ((ac % c) <= ar)
    scale = GLA_DK ** -0.5
    gnorm = g_ref[...]

    nch = ts // c
    rows = [slice(ch * c, (ch + 1) * c) for ch in range(nch)]
    b_cum = [_chunk_cumsum(g, c) for g in gk_all]

    cp = [(ch, p) for ch in range(nch) for p in range(npairs)]
    qt, qd, kd, kstack, dlast = {}, {}, {}, {}, {}
    for ch, p in cp:
        sq, pp = divmod(p, 2)
        lanes = slice(128 * pp, 128 * (pp + 1))
        bp = b_cum[sq][rows[ch], lanes]
        qp = q_ref[sq, rows[ch], lanes].astype(F32) * scale
        kp = k_ref[sq, rows[ch], lanes].astype(F32)
        c_row = jnp.concatenate(
            [jnp.broadcast_to(bp[SUB * i:SUB * i + 1, :], (SUB, 128)) for i in range(nsub)], axis=0)
        qt[ch, p] = (qp * jnp.exp(bp - c_row)).astype(BF16)
        qd[ch, p] = qp * jnp.exp(bp)
        b_last = bp[c - 1:c, :]
        dlast[ch, p] = jnp.exp(b_last)
        kd[ch, p] = kp * jnp.exp(b_last - bp)
        kts = []
        for i in range(nsub):
            e = jnp.exp(jnp.where(sub_valid[i], bp[SUB * i:SUB * i + 1, :] - bp, 0.0))
            kts.append(jnp.where(sub_valid[i], kp * e, 0.0))
        kstack[ch, p] = jnp.concatenate(kts, axis=0)

    cph = [(ch, p, hh) for ch, p in cp for hh in range(2)]
    vh = {(ch, p, hh): v_ref[p // 2, rows[ch], 128 * (2 * (p % 2) + hh):128 * (2 * (p % 2) + hh + 1)]
          for ch, p, hh in cph}
    r_mat = {(ch, p, hh): lax.dot_general(qt[ch, p], jnp.where(lane_masks[hh], kstack[ch, p], 0.0).astype(BF16),
                                          _NT, preferred_element_type=F32) for ch, p, hh in cph}
    o_intra = {k: jnp.dot(jnp.where(amask, r_mat[k], 0.0).astype(BF16), jnp.concatenate([vh[k]] * nsub, axis=0),
                          preferred_element_type=F32) for k in cph}
    upd = {(ch, p, hh): lax.dot_general(vh[ch, p, hh], jnp.where(lane_masks[hh], kd[ch, p], 0.0).astype(BF16), _TN,
                                        preferred_element_type=F32) for ch, p, hh in cph}

    st = [st_ref[p] for p in range(npairs)]
    for ch in range(nch):
        st_b = [s.astype(BF16) for s in st]
        o_pair = [lax.dot_general(
            jnp.concatenate([jnp.where(lane_masks[hh], qd[ch, p], 0.0).astype(BF16) for hh in range(2)], axis=0),
            st_b[p], _NT, preferred_element_type=F32) for p in range(npairs)]
        o_inter = {(p, hh): o_pair[p][hh * c:(hh + 1) * c] for p in range(npairs) for hh in range(2)}
        st = [st[p] * dlast[ch, p] + upd[ch, p, 0] + upd[ch, p, 1] for p in range(npairs)]
        for p in range(npairs):
            for hh in range(2):
                sq, h = p // 2, 2 * (p % 2) + hh
                o = o_intra[ch, p, hh] + o_inter[p, hh]
                o = o * lax.rsqrt(jnp.mean(o * o, axis=-1, keepdims=True) + RMS_EPS) * gnorm
                gate = _silu(r_ref[sq, rows[ch], 128 * h:128 * (h + 1)].astype(F32))
                o_ref[sq, rows[ch], 128 * h:128 * (h + 1)] = (o * gate).astype(BF16)
    for p in range(npairs):
        st_ref[p] = st[p]


def _gla(main, aux, wgk_pad, b_gk, gla_norm_g, *, bsz, seq, ts=512, nseq=2):
    nt = seq // ts
    assert bsz % nseq == 0
    main4 = main.reshape(bsz // nseq, nseq, seq, MAIN_W)
    aux4 = aux.reshape(bsz // nseq, nseq, seq, AUX_W)
    tok = lambda width, col: pl.BlockSpec((None, nseq, ts, width), lambda b, t: (b, 0, t, col))
    out = pl.pallas_call(
        functools.partial(_gla_kernel, ts=ts, nseq=nseq),
        grid=(bsz // nseq, nt),
        in_specs=[tok(256, 0), tok(256, 1), tok(512, 1), tok(512, 2),
                  tok(AUX_W, 0),
                  pl.BlockSpec((AUX_W, 256), lambda b, t: (0, 0)),
                  pl.BlockSpec((1, 256), lambda b, t: (0, 0)),
                  pl.BlockSpec((1, 128), lambda b, t: (0, 0))],
        out_specs=tok(GLA_WIDTH, 0),
        out_shape=jax.ShapeDtypeStruct((bsz // nseq, nseq, seq, GLA_WIDTH), BF16),
        scratch_shapes=[pltpu.VMEM((2 * nseq, 128, 128), F32)],
        compiler_params=_cparams("parallel", "arbitrary"),
        name="gla",
    )(main4, main4, main4, main4, aux4, wgk_pad, b_gk, gla_norm_g)
    return out.reshape(bsz * seq, GLA_WIDTH)


def _gdn_kernel(q_ref, k_ref, v_ref, z_ref, aux_ref, alog_ref, dtb_ref, g_ref, o_ref, st_ref, *, ts, nseq):
    c = CHUNK
    nch = ts // c

    @pl.when(pl.program_id(1) == 0)
    def _():
        st_ref[...] = jnp.zeros_like(st_ref)

    beta_full, gam_full, gam_t = [], [], []
    for sq in range(nseq):
        aux = aux_ref[sq]
        beta_full.append(_sigmoid(aux))
        g_full = -jnp.exp(alog_ref[...]) * _softplus(aux + dtb_ref[...])
        gam_full.append(_chunk_cumsum(g_full, c))
        gam_t.append(gam_full[-1].T)
    gnorm = g_ref[...]

    row = lax.broadcasted_iota(jnp.int32, (c, 2 * c), 0)
    lane = lax.broadcasted_iota(jnp.int32, (c, 2 * c), 1)
    left = lane < c
    col = lane % c
    causal = col <= row
    strict = col < row
    same32 = (row // 32) == (col // 32)
    same16 = (row // 16) == (col // 16)
    lvl0 = same16 & strict
    lvl1 = same32 & jnp.logical_not(same16) & strict
    lvl2 = jnp.logical_not(same32) & strict

    def pack_rows(x2):
        return jnp.where(left, x2[0:c], x2[c:2 * c])

    def pack_cols(x1):
        return jnp.where(left, x1[0:c], x1[c:2 * c])

    def halves(x):
        return jnp.where(left, x, 0.0).astype(BF16), jnp.where(left, 0.0, x).astype(BF16)

    def mmp(x, y):
        return jnp.dot(x.astype(BF16), jnp.concatenate(halves(y), axis=0), preferred_element_type=F32)

    heads = range(nseq * GDN_HEADS)
    npair = nch // 2
    items = [(h, pr) for h in heads for pr in range(npair)]
    prow = [slice(2 * c * pr, 2 * c * (pr + 1)) for pr in range(npair)]

    qn, kn, beta, gam, gam_r, kb, rhs, rhs_b, qd, kd = ([None] * len(heads) for _ in range(10))
    for h in heads:
        sq, hh = divmod(h, GDN_HEADS)
        hl = slice(128 * hh, 128 * (hh + 1))
        kb[h] = k_ref[sq, :, hl]
        qn[h] = q_ref[sq, :, hl].astype(F32)
        kn[h] = kb[h].astype(F32)
        vh = v_ref[sq, :, hl].astype(F32)
        beta[h] = beta_full[sq][:, AUX_BB + hh:AUX_BB + hh + 1]
        gam[h] = gam_full[sq][:, AUX_AB + hh:AUX_AB + hh + 1]
        gam_r[h] = gam_t[sq][AUX_AB + hh:AUX_AB + hh + 1, :]
        egam = jnp.exp(gam[h])
        rhs[h] = jnp.concatenate([beta[h] * vh, beta[h] * egam * kn[h]], axis=1)
        rhs_b[h] = rhs[h].astype(BF16)
        qd[h] = (qn[h] * egam).astype(BF16)
        gl_rows = jnp.concatenate(
            [jnp.broadcast_to(gam[h][c * (j + 1) - 1:c * (j + 1), :], (c, 1)) for j in range(nch)], axis=0)
        kd[h] = (kn[h] * jnp.exp(gl_rows - gam[h])).astype(BF16)

    kk = [pack_rows(lax.dot_general(kb[h][prow[pr]], kb[h][prow[pr]], _NT, preferred_element_type=F32))
          for h, pr in items]
    qk = [pack_rows(lax.dot_general(qn[h][prow[pr]].astype(BF16), kb[h][prow[pr]], _NT,
                                    preferred_element_type=F32)) for h, pr in items]
    dec = [jnp.exp(jnp.where(causal, pack_cols(gam[h][prow[pr]]) - gam_r[h][:, prow[pr]], 0.0)) for h, pr in items]
    a = [pack_cols(beta[h][prow[pr]]) * kk[i] * dec[i] for i, (h, pr) in enumerate(items)]
    attn = [halves(jnp.where(causal, qk[i] * dec[i], 0.0)) for i in range(len(items))]

    x1 = [jnp.where(lvl0, -ai, 0.0) for ai in a]
    x2 = [mmp(x, x) for x in x1]
    y = [mmp(x, xx) for x, xx in zip(x1, x2)]
    p = [x + xx + yy for x, xx, yy in zip(x1, x2, y)]
    xk = x2
    for _ in range(2):
        xk = [mmp(x, x) for x in xk]
        y = [mmp(pp, x) for pp, x in zip(p, xk)]
        p = [pp + x + yy for pp, x, yy in zip(p, xk, y)]
    for lvl in (lvl1, lvl2):
        low = [jnp.where(lvl, ai, 0.0) for ai in a]
        y = [mmp(pp, lo) for pp, lo in zip(p, low)]
        zz = [lo + yy for lo, yy in zip(low, y)]
        y = [mmp(z, pp) for z, pp in zip(zz, p)]
        p = [pp - z - yy for pp, z, yy in zip(p, zz, y)]

    us = [[None] * nch for _ in heads]
    ws = [[None] * nch for _ in heads]
    for i, (h, pr) in enumerate(items):
        for half, p_half in enumerate(halves(p[i])):
            j = 2 * pr + half
            uw = rhs[h][c * j:c * (j + 1)] + jnp.dot(p_half, rhs_b[h][prow[pr]], preferred_element_type=F32)
            us[h][j] = uw[:, :GDN_DV]
            ws[h][j] = uw[:, GDN_DV:].astype(BF16)

    s = [st_ref[h] for h in heads]
    vprev = [None] * len(heads)
    for j in range(nch):
        rows = slice(c * j, c * (j + 1))
        sb = [s[h].astype(BF16) for h in heads]
        ws_qd = [jnp.dot(jnp.concatenate([ws[h][j], qd[h][rows]], axis=0), sb[h], preferred_element_type=F32)
                 for h in heads]
        vn16 = [(us[h][j] - ws_qd[h][0:c]).astype(BF16) for h in heads]
        o_inter = [ws_qd[h][c:2 * c] for h in heads]
        for h in heads:
            vpair = jnp.concatenate([vn16[h], jnp.zeros_like(vn16[h])] if j % 2 == 0 else [vprev[h], vn16[h]], axis=0)
            o = o_inter[h] + jnp.dot(attn[h * npair + j // 2][j % 2], vpair, preferred_element_type=F32)
            s[h] = jnp.exp(gam[h][c * (j + 1) - 1:c * (j + 1), :]) * s[h] + lax.dot_general(
                kd[h][rows], vn16[h], _TN, preferred_element_type=F32)
            o = o * lax.rsqrt(jnp.mean(o * o, axis=-1, keepdims=True) + RMS_EPS) * gnorm
            sq, hh = divmod(h, GDN_HEADS)
            gate = _silu(z_ref[sq, rows, 128 * hh:128 * (hh + 1)].astype(F32))
            o_ref[sq, rows, 128 * hh:128 * (hh + 1)] = (o * gate).astype(BF16)
        vprev = vn16
    for h in heads:
        st_ref[h] = s[h]


def _gdn(main, aux, alog_pad, dtb_pad, gdn_norm_g, *, bsz, seq, ts=512, nseq=2):
    nt = seq // ts
    hw = GDN_HEADS * GDN_DK
    assert bsz % nseq == 0
    main4 = main.reshape(bsz // nseq, nseq, seq, MAIN_W)
    aux4 = aux.reshape(bsz // nseq, nseq, seq, AUX_W)
    tok = lambda col: pl.BlockSpec((None, nseq, ts, hw), lambda b, t: (b, 0, t, col))
    out = pl.pallas_call(
        functools.partial(_gdn_kernel, ts=ts, nseq=nseq),
        grid=(bsz // nseq, nt),
        in_specs=[tok(3), tok(4), tok(5), tok(6),
                  pl.BlockSpec((None, nseq, ts, AUX_W), lambda b, t: (b, 0, t, 0)),
                  pl.BlockSpec((1, AUX_W), lambda b, t: (0, 0)),
                  pl.BlockSpec((1, AUX_W), lambda b, t: (0, 0)),
                  pl.BlockSpec((1, GDN_DV), lambda b, t: (0, 0))],
        out_specs=pl.BlockSpec((None, nseq, ts, GDN_WIDTH), lambda b, t: (b, 0, t, 0)),
        out_shape=jax.ShapeDtypeStruct((bsz // nseq, nseq, seq, GDN_WIDTH), BF16),
        scratch_shapes=[pltpu.VMEM((nseq * GDN_HEADS, GDN_DK, GDN_DV), F32)],
        compiler_params=_cparams("parallel", "arbitrary"),
        name="gdn",
    )(main4, main4, main4, main4, aux4, alog_pad, dtb_pad, gdn_norm_g)
    return out.reshape(bsz * seq, GDN_WIDTH)


def _pad_aux_row(v, offset):
    return jnp.zeros((1, AUX_W), F32).at[0, offset:offset + v.shape[0]].set(v.astype(F32))


ROUTE_W = 128
R_E1, R_E2, R_RANK1, R_RANK2, R_G1, R_G2 = range(6)
ROUTER_GROUP_COL = N_EXPERTS
POST_MIX_SPLIT = 8


def _layer_norm(h, g, b):
    mu = jnp.mean(h, axis=-1, keepdims=True)
    hc = h - mu
    var = jnp.mean(hc * hc, axis=-1, keepdims=True)
    return hc * lax.rsqrt(var + LN_EPS) * g + b


def _pack_bf16_pairs(x):
    w = x.shape[1] // 2
    lo = lax.bitcast_convert_type(x[:, :w].astype(BF16).astype(F32), jnp.uint32)
    hi = lax.bitcast_convert_type(x[:, w:].astype(BF16).astype(F32), jnp.uint32)
    return (lo >> 16) | (hi & jnp.uint32(0xFFFF0000))


def _unpack_bf16_pairs(p):
    lo = lax.bitcast_convert_type(p << 16, F32)
    hi = lax.bitcast_convert_type(p & jnp.uint32(0xFFFF0000), F32)
    return lo, hi


HALF = D_MODEL // 2
LANES = 128
TOK_ROWS = HALF // LANES


def _store_token_rows(ref, packed):
    m = packed.shape[0]
    for c in range(TOK_ROWS):
        ref[pl.ds(c, m, stride=TOK_ROWS), :] = packed[:, LANES * c:LANES * (c + 1)]


def _load_token_rows(ref, m):
    return jnp.concatenate([ref[pl.ds(c, m, stride=TOK_ROWS), :] for c in range(TOK_ROWS)], axis=1)


def _post_mix_kernel(oa_ref, ob_ref, x_ref, wo_ref, g_ref, b_ref, wr_ref, br_ref,
                     x1p_ref, route_ref, gates_ref, cnt_ref, carry_ref, *, tm):
    @pl.when(pl.program_id(0) == 0)
    def _():
        carry_ref[...] = jnp.zeros_like(carry_ref)

    m = tm // POST_MIX_SPLIT
    groups = range(POST_MIX_SPLIT)
    rs = [slice(h * m, (h + 1) * m) for h in groups]
    y = [jnp.dot(oa_ref[rs[h], :], wo_ref[0:GLA_WIDTH, :], preferred_element_type=F32)
         + jnp.dot(ob_ref[rs[h], :], wo_ref[GLA_WIDTH:, :], preferred_element_type=F32) for h in groups]
    x1 = [_layer_norm(ALPHA_DN * x_ref[rs[h], :] + y[h], g_ref[...], b_ref[...]) for h in groups]
    for h in groups:
        _store_token_rows(x1p_ref.at[pl.ds(h * m * TOK_ROWS, m * TOK_ROWS), :], _pack_bf16_pairs(x1[h]))

    xh = [x.astype(BF16) for x in x1]
    xl = [(x - xb.astype(F32)).astype(BF16) for x, xb in zip(x1, xh)]
    hw = [lax.dot_general(wr_ref[...], xb, _NT, preferred_element_type=F32) for xb in xh]
    lw = [lax.dot_general(wr_ref[0:ROUTE_W, :], xb, _NT, preferred_element_type=F32) for xb in xl]
    logits = [a[:ROUTE_W] + a[ROUTE_W:] + b + br_ref[...] for a, b in zip(hw, lw)]
    sub = lax.broadcasted_iota(jnp.int32, (ROUTE_W, m), 0)
    big = jnp.int32(1 << 20)
    neg = jnp.float32(-jnp.inf)

    def first_argmax(vals):
        mx = jnp.max(vals, axis=0, keepdims=True)
        idx = jnp.min(jnp.where(vals == mx, sub, big), axis=0, keepdims=True)
        return mx, idx

    is_group = (sub >= ROUTER_GROUP_COL) & (sub < ROUTER_GROUP_COL + N_GROUPS)
    gl = [jnp.where(is_group, lg, neg) for lg in logits]
    gtop = [first_argmax(v) for v in gl]
    p_group = [1.0 / jnp.sum(jnp.exp(v - t[0]), axis=0, keepdims=True) for v, t in zip(gl, gtop)]
    el = [jnp.where((sub // EXPERTS_PER_GROUP) == (t[1] - ROUTER_GROUP_COL), lg, neg) for lg, t in zip(logits, gtop)]
    top1 = [first_argmax(v) for v in el]
    top2 = [first_argmax(jnp.where(sub == t[1], neg, v)) for v, t in zip(el, top1)]
    ex = [jnp.exp(b[0] - a[0]) for a, b in zip(top1, top2)]
    w1 = [1.0 / (1.0 + e) for e in ex]
    gate1 = [p * w for p, w in zip(p_group, w1)]
    gate2 = [p * (e * w) for p, e, w in zip(p_group, ex, w1)]

    oh1 = [sub == t[1] for t in top1]
    oh2 = [sub == t[1] for t in top2]
    ri = lax.broadcasted_iota(jnp.int32, (m, m), 0)
    ci = lax.broadcasted_iota(jnp.int32, (m, m), 1)
    ustrict = (ri < ci).astype(BF16)
    oh1f = [o.astype(F32) for o in oh1]
    oh2f = [o.astype(F32) for o in oh2]
    c1 = [jnp.dot(o.astype(BF16), ustrict, preferred_element_type=F32) for o in oh1f]
    c2 = [jnp.dot(o.astype(BF16), ustrict, preferred_element_type=F32) for o in oh2f]
    tot1 = [jnp.sum(o, axis=1, keepdims=True) for o in oh1f]
    tot2 = [jnp.sum(o, axis=1, keepdims=True) for o in oh2f]
    carry = carry_ref[...]
    eye8 = jnp.eye(8, ROUTE_W, dtype=F32)
    zero = jnp.zeros((1, m), F32)
    for h in groups:
        rank1 = jnp.sum(jnp.where(oh1[h], c1[h] + carry, 0.0), axis=0, keepdims=True)
        rank2 = jnp.sum(jnp.where(oh2[h], c2[h] + (carry + tot1[h]), 0.0), axis=0, keepdims=True)
        carry = carry + tot1[h] + tot2[h]
        route_t = jnp.concatenate([top1[h][1].astype(F32), top2[h][1].astype(F32), rank1, rank2,
                                   gate1[h], gate2[h], zero, zero], axis=0)
        route_ref[:, rs[h]] = route_t
        gates_ref[rs[h], :] = lax.dot_general(route_t, eye8, _TN, precision=HIGHEST, preferred_element_type=F32)
    carry_ref[...] = carry
    cnt_ref[...] = carry


def _post_mix(o_a, o_b, x2d, w_out_b, ln_g, ln_b, w_router, b_router, *, tm=2048):
    n = x2d.shape[0]
    tm = min(tm, n)
    return pl.pallas_call(
        functools.partial(_post_mix_kernel, tm=tm),
        grid=(n // tm,),
        in_specs=[pl.BlockSpec((tm, GLA_WIDTH), lambda i: (i, 0)),
                  pl.BlockSpec((tm, GDN_WIDTH), lambda i: (i, 0)),
                  pl.BlockSpec((tm, D_MODEL), lambda i: (i, 0)),
                  pl.BlockSpec((GLA_WIDTH + GDN_WIDTH, D_MODEL), lambda i: (0, 0)),
                  pl.BlockSpec((1, D_MODEL), lambda i: (0, 0)),
                  pl.BlockSpec((1, D_MODEL), lambda i: (0, 0)),
                  pl.BlockSpec((2 * ROUTE_W, D_MODEL), lambda i: (0, 0)),
                  pl.BlockSpec((ROUTE_W, 1), lambda i: (0, 0))],
        out_specs=[pl.BlockSpec((tm * TOK_ROWS, LANES), lambda i: (i, 0)),
                   pl.BlockSpec((8, tm), lambda i: (0, i)),
                   pl.BlockSpec((tm, ROUTE_W), lambda i: (i, 0)),
                   pl.BlockSpec((ROUTE_W, 1), lambda i: (0, 0))],
        out_shape=[jax.ShapeDtypeStruct((n * TOK_ROWS, LANES), jnp.uint32),
                   jax.ShapeDtypeStruct((8, n), F32),
                   jax.ShapeDtypeStruct((n, ROUTE_W), F32),
                   jax.ShapeDtypeStruct((ROUTE_W, 1), F32)],
        scratch_shapes=[pltpu.VMEM((ROUTE_W, 1), F32)],
        compiler_params=_cparams("arbitrary"),
        name="post_mix",
    )(o_a, o_b, x2d, w_out_b, ln_g, ln_b, w_router, b_router)


EXPERT_BLOCK = 256
BLOCKS_PER_STEP = 4

ROW_UNROLL = 8


def _plan_kernel(route_ref, cnt_ref, dest_ref, *, tm, tb):
    counts = cnt_ref[...].astype(jnp.int32)
    padded = jnp.broadcast_to(((counts + (tb - 1)) & jnp.int32(-tb)).astype(F32), (ROUTE_W, ROUTE_W))
    pstart = (_chunk_cumsum(padded, ROUTE_W) - padded)[:, 0:1]
    route = route_ref[...]
    sub = lax.broadcasted_iota(jnp.int32, (ROUTE_W, tm), 0).astype(F32)
    rows = []
    for ce, cr in ((R_E1, R_RANK1), (R_E2, R_RANK2)):
        off = jnp.sum(jnp.where(sub == route[ce:ce + 1, :], pstart, 0.0), axis=0, keepdims=True)
        rows.append(off + route[cr:cr + 1, :])
    rows.extend([jnp.zeros((1, tm), F32)] * (8 - len(rows)))
    dest_ref[...] = jnp.concatenate(rows, axis=0).astype(jnp.int32) * TOK_ROWS


def _plan(route, counts, *, tb, tm=2048):
    n = route.shape[1]
    tm = min(tm, n)
    return pl.pallas_call(
        functools.partial(_plan_kernel, tm=tm, tb=tb),
        grid=(n // tm,),
        in_specs=[pl.BlockSpec((8, tm), lambda i: (0, i)),
                  pl.BlockSpec((ROUTE_W, 1), lambda i: (0, 0))],
        out_specs=pl.BlockSpec((8, tm), lambda i: (0, i)),
        out_shape=jax.ShapeDtypeStruct((8, n), jnp.int32),
        compiler_params=_cparams("parallel"),
        name="plan",
    )(route, counts)


def _dispatch_kernel(d1_ref, d2_ref, last_blk_ref, nvalid_ref, xp_ref, xs_ref, zbuf, sem, zsem, *, tm, tb, nb):
    base = pl.program_id(0) * tm
    rows = tb * TOK_ROWS

    @pl.when(pl.program_id(0) == 0)
    def _():
        zbuf[...] = jnp.zeros_like(zbuf)

        def zero_copy(blk):
            return pltpu.make_async_copy(zbuf, xs_ref.at[pl.ds(pl.multiple_of(blk * rows, rows), rows), :], zsem)

        def for_each_block(fn):
            def per_expert(e, carry):
                @pl.when(last_blk_ref[e] >= 0)
                def _():
                    fn(zero_copy(last_blk_ref[e]))
                return carry

            def per_tail(blk, carry):
                fn(zero_copy(blk))
                return carry
            lax.fori_loop(0, N_EXPERTS, per_expert, 0)
            lax.fori_loop(nvalid_ref[0], nb, per_tail, 0)

        for_each_block(lambda cp: cp.start())
        for_each_block(lambda cp: cp.wait())

    def body(j, carry):
        for u in range(ROW_UNROLL):
            r = j * ROW_UNROLL + u
            for k, d_ref in enumerate((d1_ref, d2_ref)):
                dst = pl.multiple_of(d_ref[base + r], TOK_ROWS)
                pltpu.make_async_copy(xp_ref.at[pl.ds(r * TOK_ROWS, TOK_ROWS), :],
                                      xs_ref.at[pl.ds(dst, TOK_ROWS), :], sem).start(priority=k)
        return carry
    lax.fori_loop(0, tm // ROW_UNROLL, body, 0)
    for _ in range(TOP_K):
        pltpu.make_async_copy(xp_ref, xs_ref.at[pl.ds(0, tm * TOK_ROWS), :], sem).wait()


def _dispatch(dest1, dest2, last_blk, nvalid, x1p, *, nb, tb=EXPERT_BLOCK, tm=2048):
    n = x1p.shape[0] // TOK_ROWS
    tm = min(tm, n)
    grid_spec = pltpu.PrefetchScalarGridSpec(
        num_scalar_prefetch=4,
        grid=(n // tm,),
        in_specs=[pl.BlockSpec((tm * TOK_ROWS, LANES), lambda i, a, b, lb, nv: (i, 0))],
        out_specs=pl.BlockSpec(memory_space=pl.ANY),
        scratch_shapes=[pltpu.VMEM((tb * TOK_ROWS, LANES), jnp.uint32),
                        pltpu.SemaphoreType.DMA(()),
                        pltpu.SemaphoreType.DMA(())],
    )
    return pl.pallas_call(
        functools.partial(_dispatch_kernel, tm=tm, tb=tb, nb=nb),
        grid_spec=grid_spec,
        out_shape=jax.ShapeDtypeStruct((nb * tb * TOK_ROWS, LANES), jnp.uint32),
        compiler_params=_cparams("arbitrary"),
        name="dispatch",
    )(dest1, dest2, last_blk, nvalid, x1p)


def _expert_kernel(blk_e_ref, first_ref, next_e_ref, nvalid_ref, xs_ref, wg_hbm, wu_hbm, wd_hbm, y_ref,
                   wgs, wus, wds, wgb, wub, wdb, sem, *, tb):
    rows = tb * TOK_ROWS

    def weight_copies(e, slot):
        return [pltpu.make_async_copy(src.at[e], dst.at[slot], sem.at[slot, j])
                for j, (src, dst) in enumerate(((wg_hbm, wgs), (wu_hbm, wus), (wd_hbm, wds)))]

    @pl.when((pl.program_id(0) == 0) & (nvalid_ref[0] > 0))
    def _():
        for cp in weight_copies(blk_e_ref[0], first_ref[0] - 1):
            cp.start()

    def run(i, sub, nblk):
        @pl.when(first_ref[i] > 0)
        def _():
            slot = first_ref[i] - 1

            @pl.when(next_e_ref[i] >= 0)
            def _():
                for cp in weight_copies(next_e_ref[i], 1 - slot):
                    cp.start()

            for cp in weight_copies(blk_e_ref[i], slot):
                cp.wait()
            wgb[...] = wgs[slot].astype(BF16)
            wub[...] = wus[slot].astype(BF16)
            wdb[...] = wds[slot].astype(BF16)
            mlp(sub, nblk)

        @pl.when(first_ref[i] == 0)
        def _():
            mlp(sub, nblk)

    def mlp(sub, nblk):
        m = nblk * tb
        span = pl.ds(sub * rows, nblk * rows)
        lo, hi = _unpack_bf16_pairs(_load_token_rows(xs_ref.at[span, :], m))
        xb = jnp.concatenate([lo.astype(BF16), hi.astype(BF16)], axis=1)
        hg = jnp.dot(xb, wgb[...], preferred_element_type=F32)
        hu = jnp.dot(xb, wub[...], preferred_element_type=F32)
        hid = (_silu(hg) * hu).astype(BF16)
        _store_token_rows(y_ref.at[span, :], _pack_bf16_pairs(jnp.dot(hid, wdb[...], preferred_element_type=F32)))

    nvalid = nvalid_ref[0]
    for sub in range(0, BLOCKS_PER_STEP, 2):
        i0 = pl.program_id(0) * BLOCKS_PER_STEP + sub
        i1 = i0 + 1
        paired = (i1 < nvalid) & (first_ref[i1] == 0)

        @pl.when(paired)
        def _():
            run(i0, sub, 2)

        @pl.when(jnp.logical_not(paired))
        def _():
            @pl.when(i0 < nvalid)
            def _():
                run(i0, sub, 1)

            @pl.when(i1 < nvalid)
            def _():
                run(i1, sub + 1, 1)

        for s_, i in ((sub, i0), (sub + 1, i1)):
            @pl.when(i >= nvalid)
            def _():
                y_ref[pl.ds(s_ * rows, rows), :] = jnp.zeros((rows, LANES), jnp.uint32)


def _experts(blk_e, first, next_e, nvalid, xs, w_gate, w_up, w_down, *, tb=EXPERT_BLOCK):
    nb = blk_e.shape[0]
    assert nb % BLOCKS_PER_STEP == 0
    step_rows = BLOCKS_PER_STEP * tb * TOK_ROWS
    last_step = lambda nv: jnp.maximum((nv[0] - 1) // BLOCKS_PER_STEP, 0)
    grid_spec = pltpu.PrefetchScalarGridSpec(
        num_scalar_prefetch=4,
        grid=(nb // BLOCKS_PER_STEP,),
        in_specs=[pl.BlockSpec((step_rows, LANES), lambda s, be, fr, ne, nv: (jnp.minimum(s, last_step(nv)), 0)),
                  pl.BlockSpec(memory_space=pl.ANY),
                  pl.BlockSpec(memory_space=pl.ANY),
                  pl.BlockSpec(memory_space=pl.ANY)],
        out_specs=pl.BlockSpec((step_rows, LANES), lambda s, be, fr, ne, nv: (s, 0)),
        scratch_shapes=[pltpu.VMEM((2, D_MODEL, D_EXPERT), F32),
                        pltpu.VMEM((2, D_MODEL, D_EXPERT), F32),
                        pltpu.VMEM((2, D_EXPERT, D_MODEL), F32),
                        pltpu.VMEM((D_MODEL, D_EXPERT), BF16),
                        pltpu.VMEM((D_MODEL, D_EXPERT), BF16),
                        pltpu.VMEM((D_EXPERT, D_MODEL), BF16),
                        pltpu.SemaphoreType.DMA((2, 3))],
    )
    return pl.pallas_call(
        functools.partial(_expert_kernel, tb=tb),
        grid_spec=grid_spec,
        out_shape=jax.ShapeDtypeStruct((nb * tb * TOK_ROWS, LANES), jnp.uint32),
        compiler_params=_cparams("arbitrary"),
        name="experts",
    )(blk_e, first, next_e, nvalid, xs, w_gate, w_up, w_down)


def _combine_kernel(d1_ref, d2_ref, y_hbm, x1_ref, route_ref, g_ref, b_ref, o_ref, ybuf, sem, *, tt):
    i = pl.program_id(0)
    nsteps = pl.num_programs(0)
    slot = i % 2

    def issue(tile, slot_):
        def body(j, carry):
            for u in range(ROW_UNROLL):
                r = j * ROW_UNROLL + u
                for k, d_ref in enumerate((d1_ref, d2_ref)):
                    src = pl.multiple_of(d_ref[tile * tt + r], TOK_ROWS)
                    pltpu.make_async_copy(y_hbm.at[pl.ds(src, TOK_ROWS), :],
                                          ybuf.at[slot_, k, pl.ds(r * TOK_ROWS, TOK_ROWS), :],
                                          sem.at[slot_]).start(priority=k)
            return carry
        lax.fori_loop(0, tt // ROW_UNROLL, body, 0)

    @pl.when(i == 0)
    def _():
        issue(0, 0)

    @pl.when(i + 1 < nsteps)
    def _():
        issue(i + 1, 1 - slot)

    for k in range(TOP_K):
        pltpu.make_async_copy(y_hbm.at[pl.ds(0, tt * TOK_ROWS), :], ybuf.at[slot, k], sem.at[slot]).wait()
    route = route_ref[...]
    g1 = route[:, R_G1:R_G1 + 1]
    g2 = route[:, R_G2:R_G2 + 1]
    lo1, hi1 = _unpack_bf16_pairs(_load_token_rows(ybuf.at[slot, 0], tt))
    lo2, hi2 = _unpack_bf16_pairs(_load_token_rows(ybuf.at[slot, 1], tt))
    x_lo, x_hi = _unpack_bf16_pairs(_load_token_rows(x1_ref, tt))
    h_lo = ALPHA_DN * x_lo + (g1 * lo1 + g2 * lo2)
    h_hi = ALPHA_DN * x_hi + (g1 * hi1 + g2 * hi2)
    mu = (jnp.sum(h_lo, axis=-1, keepdims=True) + jnp.sum(h_hi, axis=-1, keepdims=True)) * (1.0 / D_MODEL)
    c_lo = h_lo - mu
    c_hi = h_hi - mu
    var = (jnp.sum(c_lo * c_lo, axis=-1, keepdims=True) + jnp.sum(c_hi * c_hi, axis=-1, keepdims=True)) * (1.0 / D_MODEL)
    inv = lax.rsqrt(var + LN_EPS)
    o_ref[:, 0:HALF] = c_lo * inv * g_ref[:, 0:HALF] + b_ref[:, 0:HALF]
    o_ref[:, HALF:] = c_hi * inv * g_ref[:, HALF:] + b_ref[:, HALF:]


def _combine(dest1, dest2, ybuf, x1p, route, ln_g, ln_b, *, tt=512):
    n = x1p.shape[0] // TOK_ROWS
    tt = min(tt, n)
    grid_spec = pltpu.PrefetchScalarGridSpec(
        num_scalar_prefetch=2,
        grid=(n // tt,),
        in_specs=[pl.BlockSpec(memory_space=pl.ANY),
                  pl.BlockSpec((tt * TOK_ROWS, LANES), lambda i, a, b: (i, 0)),
                  pl.BlockSpec((tt, ROUTE_W), lambda i, a, b: (i, 0)),
                  pl.BlockSpec((1, D_MODEL), lambda i, a, b: (0, 0)),
                  pl.BlockSpec((1, D_MODEL), lambda i, a, b: (0, 0))],
        out_specs=pl.BlockSpec((tt, D_MODEL), lambda i, a, b: (i, 0)),
        scratch_shapes=[pltpu.VMEM((2, TOP_K, tt * TOK_ROWS, LANES), jnp.uint32),
                        pltpu.SemaphoreType.DMA((2,))],
    )
    return pl.pallas_call(
        functools.partial(_combine_kernel, tt=tt),
        grid_spec=grid_spec,
        out_shape=jax.ShapeDtypeStruct((n, D_MODEL), F32),
        compiler_params=_cparams("arbitrary"),
        name="combine",
    )(dest1, dest2, ybuf, x1p, route, ln_g, ln_b)


def _block_plan(counts_row, n, tb):
    counts = counts_row[:N_EXPERTS, 0].astype(jnp.int32)
    pends = jnp.cumsum((counts + tb - 1) // tb * tb)
    nb = (n * TOP_K) // tb + N_EXPERTS
    nvalid = (pends[-1] // tb).astype(jnp.int32)
    blk = jnp.arange(nb, dtype=jnp.int32)
    blk_e = jnp.sum((pends[None, :] <= (blk * tb)[:, None]).astype(jnp.int32), axis=1)
    blk_e = jnp.minimum(blk_e, N_EXPERTS - 1)
    blk_e = jnp.where(blk < nvalid, blk_e, blk_e[jnp.maximum(nvalid - 1, 0)])
    first = jnp.concatenate([jnp.ones((1,), jnp.int32), (blk_e[1:] != blk_e[:-1]).astype(jnp.int32)])
    slot_e = (jnp.cumsum((counts > 0).astype(jnp.int32)) - 1) % 2
    first = first * (1 + slot_e[blk_e])
    ids = jnp.arange(N_EXPERTS, dtype=jnp.int32)
    later = lax.cummin(jnp.where(counts > 0, ids, N_EXPERTS), axis=0, reverse=True)
    nxt = jnp.concatenate([later[1:], jnp.full((1,), N_EXPERTS, jnp.int32)])
    nxt = jnp.where(nxt < N_EXPERTS, nxt, -1)
    last_blk = jnp.where(counts > 0, pends // tb - 1, -1).astype(jnp.int32)
    return blk_e, first, nxt[blk_e], nvalid.reshape(1), last_blk


def kernel(x, w_in, w_gk_up, b_gk, conv_w, a_log, dt_bias, gla_norm_g, gdn_norm_g, w_out, ln1_g, ln1_b, w_router_group, b_router_group, w_router_expert, b_router_expert, w_gate, w_up, w_down, ln2_g, ln2_b):
    bsz, seq, d = x.shape
    n = bsz * seq
    x2d = x.reshape(n, d)
    w_main, w_aux = _split_w_in(w_in[0])
    main, aux = _in_proj(x2d, w_main, w_aux, conv_w[0].astype(F32), seq=seq)
    wgk = jnp.zeros((AUX_W, 256), F32).at[:GLA_GATE_RANK].set(w_gk_up[0])
    o_a = _gla(main, aux, wgk, b_gk[0][None], gla_norm_g[0][None], bsz=bsz, seq=seq)
    o_b = _gdn(main, aux, _pad_aux_row(a_log[0], AUX_AB), _pad_aux_row(dt_bias[0], AUX_AB),
               gdn_norm_g[0][None], bsz=bsz, seq=seq)
    rpad = jnp.zeros((ROUTE_W - N_EXPERTS - N_GROUPS, d), F32)
    w_router = jnp.concatenate([w_router_expert[0].T, w_router_group[0].T, rpad], axis=0)
    w_router_hi = w_router.astype(BF16)
    w_router = jnp.concatenate([w_router_hi, (w_router - w_router_hi.astype(F32)).astype(BF16)], axis=0)
    b_router = jnp.concatenate([b_router_expert[0], b_router_group[0], rpad[:, 0]])[:, None]
    x1p, route, gates, counts = _post_mix(o_a, o_b, x2d, w_out[0].astype(BF16), ln1_g[0][None], ln1_b[0][None],
                                          w_router, b_router)
    dest = _plan(route, counts, tb=EXPERT_BLOCK)
    blk_e, first, next_e, nvalid, last_blk = _block_plan(counts, n, EXPERT_BLOCK)
    xs = _dispatch(dest[0], dest[1], last_blk, nvalid, x1p, nb=blk_e.shape[0])
    ybuf = _experts(blk_e, first, next_e, nvalid, xs, w_gate[0], w_up[0], w_down[0])
    out = _combine(dest[0], dest[1], ybuf, x1p, gates, ln2_g[0][None], ln2_b[0][None])
    return out.reshape(bsz, seq, d)
```

```python
import functools

import jax
import jax.numpy as jnp
import numpy as np
from jax import lax
from jax.experimental import pallas as pl
from jax.experimental.pallas import tpu as pltpu

F32 = jnp.float32
BF16 = jnp.bfloat16
HIGHEST = lax.Precision.HIGHEST

D_MODEL = 1024
DEPTH = 1
GLA_HEADS = 4
GLA_DK = 64
GLA_DV = 128
GLA_WIDTH = GLA_HEADS * GLA_DV
GLA_GATE_RANK = 16
GLA_GATE_NORM = 16.0
GDN_HEADS = 4
GDN_DK = 128
GDN_DV = 128
GDN_WIDTH = GDN_HEADS * GDN_DV
CONV_W = 4
CHUNK = 64
N_GROUPS = 8
EXPERTS_PER_GROUP = 8
N_EXPERTS = N_GROUPS * EXPERTS_PER_GROUP
TOP_K = 2
D_EXPERT = 512
LN_EPS = 1e-5
RMS_EPS = 1e-6
ALPHA_DN = (2.0 * DEPTH) ** 0.25

MAIN_W = 2 * GLA_HEADS * GLA_DK + 2 * GLA_WIDTH + 4 * GDN_WIDTH
AUX_W = 128
AUX_LRA = 0
AUX_BB = GLA_GATE_RANK
AUX_AB = GLA_GATE_RANK + GDN_HEADS
SUB = 16
VMEM_LIMIT = 56 * 1024 * 1024


def _cparams(*sem):
    return pltpu.CompilerParams(dimension_semantics=sem, vmem_limit_bytes=VMEM_LIMIT)


def _sigmoid(x):
    return 1.0 / (1.0 + jnp.exp(-x))


def _silu(x):
    return x * _sigmoid(x)


def _log_sigmoid(x):
    return jnp.minimum(x, 0.0) - jnp.log(1.0 + jnp.exp(-jnp.abs(x)))


def _softplus(x):
    return jnp.maximum(x, 0.0) + jnp.log(1.0 + jnp.exp(-jnp.abs(x)))


def _chunk_cumsum(x, c):
    pos = lax.broadcasted_iota(jnp.int32, x.shape, 0) % c
    shift = 1
    while shift < c:
        x = x + jnp.where(pos >= shift, pltpu.roll(x, shift, 0), 0.0)
        shift *= 2
    return x


CONV_PAD = 8
GDN_HW = GDN_HEADS * GDN_DK
GDN_QKV_BLOCK0 = (2 * GLA_HEADS * GLA_DK + 2 * GLA_WIDTH) // GDN_HW


def _in_proj_kernel(x_ref, wm_ref, wa_ref, cw_ref, main_ref, aux_ref, stage_ref, *, tm, tiles_per_seq):
    xb = x_ref[...].astype(BF16)

    @pl.when((pl.program_id(0) % tiles_per_seq) == 0)
    def _():
        stage_ref[:, 0:CONV_PAD, :] = jnp.zeros((3 * GDN_HEADS, CONV_PAD, GDN_DK), F32)

    for j in range(MAIN_W // GDN_HW):
        sl = slice(j * GDN_HW, (j + 1) * GDN_HW)
        r = jnp.dot(xb, wm_ref[:, sl], preferred_element_type=F32)
        b = j - GDN_QKV_BLOCK0
        if not 0 <= b < 3:
            main_ref[:, sl] = r.astype(BF16)
            continue
        for h in range(GDN_HEADS):
            st = stage_ref.at[b * GDN_HEADS + h]
            cols = slice(b * GDN_HW + h * GDN_DK, b * GDN_HW + (h + 1) * GDN_DK)
            st[CONV_PAD:, :] = r[:, h * GDN_DK:(h + 1) * GDN_DK]
            acc = jnp.zeros((tm, GDN_DK), F32)
            for i in range(CONV_W):
                lo = CONV_PAD - (CONV_W - 1) + i
                acc = acc + st[lo:lo + tm, :] * cw_ref[i:i + 1, cols]
            st[0:CONV_PAD, :] = st[tm:tm + CONV_PAD, :]
            yh = _silu(acc)
            if b < 2:
                yh = yh * lax.rsqrt(jnp.sum(yh * yh, axis=-1, keepdims=True) + RMS_EPS)
            if b == 0:
                yh = yh * (GDN_DK ** -0.5)
            main_ref[:, j * GDN_HW + h * GDN_DK:j * GDN_HW + (h + 1) * GDN_DK] = yh.astype(BF16)
    aux_ref[...] = jnp.dot(xb, wa_ref[...], preferred_element_type=F32)


def _in_proj(x2d, w_main, w_aux, conv_w, *, seq, tm=512):
    n = x2d.shape[0]
    return pl.pallas_call(
        functools.partial(_in_proj_kernel, tm=tm, tiles_per_seq=seq // tm),
        grid=(n // tm,),
        in_specs=[pl.BlockSpec((tm, D_MODEL), lambda i: (i, 0)),
                  pl.BlockSpec((D_MODEL, MAIN_W), lambda i: (0, 0)),
                  pl.BlockSpec((D_MODEL, AUX_W), lambda i: (0, 0)),
                  pl.BlockSpec((CONV_W, 3 * GDN_HW), lambda i: (0, 0))],
        out_specs=[pl.BlockSpec((tm, MAIN_W), lambda i: (i, 0)),
                   pl.BlockSpec((tm, AUX_W), lambda i: (i, 0))],
        out_shape=[jax.ShapeDtypeStruct((n, MAIN_W), BF16),
                   jax.ShapeDtypeStruct((n, AUX_W), F32)],
        scratch_shapes=[pltpu.VMEM((3 * GDN_HEADS, tm + CONV_PAD, GDN_DK), F32)],
        compiler_params=_cparams("arbitrary"),
        name="in_proj",
    )(x2d, w_main, w_aux, conv_w)


def _split_w_in(w_in):
    sizes = (GLA_HEADS * GLA_DK, GLA_HEADS * GLA_DK, GLA_WIDTH, GLA_WIDTH, GLA_GATE_RANK,
             GDN_HEADS * GDN_DK, GDN_HEADS * GDN_DK, GDN_WIDTH, GDN_WIDTH, GDN_HEADS, GDN_HEADS)
    offs = np.cumsum((0,) + sizes)
    seg = [w_in[:, offs[i]:offs[i + 1]] for i in range(len(sizes))]
    qa, ka, va, ra, lra, qb, kb, vb, zb, bb, ab = seg
    w_main = jnp.concatenate([qa, ka, va, ra, qb, kb, vb, zb], axis=1).astype(BF16)
    pad = jnp.zeros((w_in.shape[0], AUX_W - GLA_GATE_RANK - 2 * GDN_HEADS), w_in.dtype)
    w_aux = jnp.concatenate([lra, bb, ab, pad], axis=1).astype(BF16)
    return w_main, w_aux


_NT = (((1,), (1,)), ((), ()))
_TN = (((0,), (0,)), ((), ()))


def _gla_kernel(q_ref, k_ref, v_ref, r_ref, aux_ref, wgk_ref, bgk_ref, g_ref, o_ref, st_ref, *, ts, nseq):
    npairs = 2 * nseq
    c = CHUNK
    nsub = c // SUB

    @pl.when(pl.program_id(1) == 0)
    def _():
        st_ref[...] = jnp.zeros_like(st_ref)

    gk_all = [_log_sigmoid(jnp.dot(aux_ref[sq], wgk_ref[...], precision=HIGHEST, preferred_element_type=F32)
                           + bgk_ref[...]) * (1.0 / GLA_GATE_NORM) for sq in range(nseq)]

    row128 = lax.broadcasted_iota(jnp.int32, (c, 128), 0)
    lane128 = lax.broadcasted_iota(jnp.int32, (1, 128), 1)
    lane_masks = [(lane128 // GLA_DK) == hh for hh in range(2)]
    sub_valid = [row128 < SUB * (i + 1) for i in range(nsub)]
    ar = lax.broadcasted_iota(jnp.int32, (c, nsub * c), 0)
    ac = lax.broadcasted_iota(jnp.int32, (c, nsub * c), 1)
    amask = ((ac // c) == (ar // SUB)) & ((ac % c) <= ar)
    scale = GLA_DK ** -0.5
    gnorm = g_ref[...]

    nch = ts // c
    rows = [slice(ch * c, (ch + 1) * c) for ch in range(nch)]
    b_cum = [_chunk_cumsum(g, c) for g in gk_all]

    cp = [(ch, p) for ch in range(nch) for p in range(npairs)]
    qt, qd, kd, kstack, dlast = {}, {}, {}, {}, {}
    for ch, p in cp:
        sq, pp = divmod(p, 2)
        lanes = slice(128 * pp, 128 * (pp + 1))
        bp = b_cum[sq][rows[ch], lanes]
        qp = q_ref[sq, rows[ch], lanes].astype(F32) * scale
        kp = k_ref[sq, rows[ch], lanes].astype(F32)
        c_row = jnp.concatenate(
            [jnp.broadcast_to(bp[SUB * i:SUB * i + 1, :], (SUB, 128)) for i in range(nsub)], axis=0)
        qt[ch, p] = (qp * jnp.exp(bp - c_row)).astype(BF16)
        qd[ch, p] = qp * jnp.exp(bp)
        b_last = bp[c - 1:c, :]
        dlast[ch, p] = jnp.exp(b_last)
        kd[ch, p] = kp * jnp.exp(b_last - bp)
        kts = []
        for i in range(nsub):
            e = jnp.exp(jnp.where(sub_valid[i], bp[SUB * i:SUB * i + 1, :] - bp, 0.0))
            kts.append(jnp.where(sub_valid[i], kp * e, 0.0))
        kstack[ch, p] = jnp.concatenate(kts, axis=0)

    cph = [(ch, p, hh) for ch, p in cp for hh in range(2)]
    vh = {(ch, p, hh): v_ref[p // 2, rows[ch], 128 * (2 * (p % 2) + hh):128 * (2 * (p % 2) + hh + 1)]
          for ch, p, hh in cph}
    r_mat = {(ch, p, hh): lax.dot_general(qt[ch, p], jnp.where(lane_masks[hh], kstack[ch, p], 0.0).astype(BF16),
                                          _NT, preferred_element_type=F32) for ch, p, hh in cph}
    o_intra = {k: jnp.dot(jnp.where(amask, r_mat[k], 0.0).astype(BF16), jnp.concatenate([vh[k]] * nsub, axis=0),
                          preferred_element_type=F32) for k in cph}
    upd = {(ch, p, hh): lax.dot_general(vh[ch, p, hh], jnp.where(lane_masks[hh], kd[ch, p], 0.0).astype(BF16), _TN,
                                        preferred_element_type=F32) for ch, p, hh in cph}

    st = [st_ref[p] for p in range(npairs)]
    for ch in range(nch):
        st_b = [s.astype(BF16) for s in st]
        o_pair = [lax.dot_general(
            jnp.concatenate([jnp.where(lane_masks[hh], qd[ch, p], 0.0).astype(BF16) for hh in range(2)], axis=0),
            st_b[p], _NT, preferred_element_type=F32) for p in range(npairs)]
        o_inter = {(p, hh): o_pair[p][hh * c:(hh + 1) * c] for p in range(npairs) for hh in range(2)}
        st = [st[p] * dlast[ch, p] + upd[ch, p, 0] + upd[ch, p, 1] for p in range(npairs)]
        for p in range(npairs):
            for hh in range(2):
                sq, h = p // 2, 2 * (p % 2) + hh
                o = o_intra[ch, p, hh] + o_inter[p, hh]
                o = o * lax.rsqrt(jnp.mean(o * o, axis=-1, keepdims=True) + RMS_EPS) * gnorm
                gate = _silu(r_ref[sq, rows[ch], 128 * h:128 * (h + 1)].astype(F32))
                o_ref[sq, rows[ch], 128 * h:128 * (h + 1)] = (o * gate).astype(BF16)
    for p in range(npairs):
        st_ref[p] = st[p]


def _gla(main, aux, wgk_pad, b_gk, gla_norm_g, *, bsz, seq, ts=512, nseq=2):
    nt = seq // ts
    assert bsz % nseq == 0
    main4 = main.reshape(bsz // nseq, nseq, seq, MAIN_W)
    aux4 = aux.reshape(bsz // nseq, nseq, seq, AUX_W)
    tok = lambda width, col: pl.BlockSpec((None, nseq, ts, width), lambda b, t: (b, 0, t, col))
    out = pl.pallas_call(
        functools.partial(_gla_kernel, ts=ts, nseq=nseq),
        grid=(bsz // nseq, nt),
        in_specs=[tok(256, 0), tok(256, 1), tok(512, 1), tok(512, 2),
                  tok(AUX_W, 0),
                  pl.BlockSpec((AUX_W, 256), lambda b, t: (0, 0)),
                  pl.BlockSpec((1, 256), lambda b, t: (0, 0)),
                  pl.BlockSpec((1, 128), lambda b, t: (0, 0))],
        out_specs=tok(GLA_WIDTH, 0),
        out_shape=jax.ShapeDtypeStruct((bsz // nseq, nseq, seq, GLA_WIDTH), BF16),
        scratch_shapes=[pltpu.VMEM((2 * nseq, 128, 128), F32)],
        compiler_params=_cparams("parallel", "arbitrary"),
        name="gla",
    )(main4, main4, main4, main4, aux4, wgk_pad, b_gk, gla_norm_g)
    return out.reshape(bsz * seq, GLA_WIDTH)


def _gdn_kernel(q_ref, k_ref, v_ref, z_ref, aux_ref, alog_ref, dtb_ref, g_ref, o_ref, st_ref, *, ts, nseq):
    c = CHUNK
    nch = ts // c

    @pl.when(pl.program_id(1) == 0)
    def _():
        st_ref[...] = jnp.zeros_like(st_ref)

    beta_full, gam_full, gam_t = [], [], []
    for sq in range(nseq):
        aux = aux_ref[sq]
        beta_full.append(_sigmoid(aux))
        g_full = -jnp.exp(alog_ref[...]) * _softplus(aux + dtb_ref[...])
        gam_full.append(_chunk_cumsum(g_full, c))
        gam_t.append(gam_full[-1].T)
    gnorm = g_ref[...]

    row = lax.broadcasted_iota(jnp.int32, (c, 2 * c), 0)
    lane = lax.broadcasted_iota(jnp.int32, (c, 2 * c), 1)
    left = lane < c
    col = lane % c
    causal = col <= row
    strict = col < row
    same32 = (row // 32) == (col // 32)
    same16 = (row // 16) == (col // 16)
    lvl0 = same16 & strict
    lvl1 = same32 & jnp.logical_not(same16) & strict
    lvl2 = jnp.logical_not(same32) & strict

    def pack_rows(x2):
        return jnp.where(left, x2[0:c], x2[c:2 * c])

    def pack_cols(x1):
        return jnp.where(left, x1[0:c], x1[c:2 * c])

    def halves(x):
        return jnp.where(left, x, 0.0).astype(BF16), jnp.where(left, 0.0, x).astype(BF16)

    def mmp(x, y):
        return jnp.dot(x.astype(BF16), jnp.concatenate(halves(y), axis=0), preferred_element_type=F32)

    heads = range(nseq * GDN_HEADS)
    npair = nch // 2
    items = [(h, pr) for h in heads for pr in range(npair)]
    prow = [slice(2 * c * pr, 2 * c * (pr + 1)) for pr in range(npair)]

    qn, kn, beta, gam, gam_r, kb, rhs, rhs_b, qd, kd = ([None] * len(heads) for _ in range(10))
    for h in heads:
        sq, hh = divmod(h, GDN_HEADS)
        hl = slice(128 * hh, 128 * (hh + 1))
        kb[h] = k_ref[sq, :, hl]
        qn[h] = q_ref[sq, :, hl].astype(F32)
        kn[h] = kb[h].astype(F32)
        vh = v_ref[sq, :, hl].astype(F32)
        beta[h] = beta_full[sq][:, AUX_BB + hh:AUX_BB + hh + 1]
        gam[h] = gam_full[sq][:, AUX_AB + hh:AUX_AB + hh + 1]
        gam_r[h] = gam_t[sq][AUX_AB + hh:AUX_AB + hh + 1, :]
        egam = jnp.exp(gam[h])
        rhs[h] = jnp.concatenate([beta[h] * vh, beta[h] * egam * kn[h]], axis=1)
        rhs_b[h] = rhs[h].astype(BF16)
        qd[h] = (qn[h] * egam).astype(BF16)
        gl_rows = jnp.concatenate(
            [jnp.broadcast_to(gam[h][c * (j + 1) - 1:c * (j + 1), :], (c, 1)) for j in range(nch)], axis=0)
        kd[h] = (kn[h] * jnp.exp(gl_rows - gam[h])).astype(BF16)

    kk = [pack_rows(lax.dot_general(kb[h][prow[pr]], kb[h][prow[pr]], _NT, preferred_element_type=F32))
          for h, pr in items]
    qk = [pack_rows(lax.dot_general(qn[h][prow[pr]].astype(BF16), kb[h][prow[pr]], _NT,
                                    preferred_element_type=F32)) for h, pr in items]
    dec = [jnp.exp(jnp.where(causal, pack_cols(gam[h][prow[pr]]) - gam_r[h][:, prow[pr]], 0.0)) for h, pr in items]
    a = [pack_cols(beta[h][prow[pr]]) * kk[i] * dec[i] for i, (h, pr) in enumerate(items)]
    attn = [halves(jnp.where(causal, qk[i] * dec[i], 0.0)) for i in range(len(items))]

    x1 = [jnp.where(lvl0, -ai, 0.0) for ai in a]
    x2 = [mmp(x, x) for x in x1]
    y = [mmp(x, xx) for x, xx in zip(x1, x2)]
    p = [x + xx + yy for x, xx, yy in zip(x1, x2, y)]
    xk = x2
    for _ in range(2):
        xk = [mmp(x, x) for x in xk]
        y = [mmp(pp, x) for pp, x in zip(p, xk)]
        p = [pp + x + yy for pp, x, yy in zip(p, xk, y)]
    for lvl in (lvl1, lvl2):
        low = [jnp.where(lvl, ai, 0.0) for ai in a]
        y = [mmp(pp, lo) for pp, lo in zip(p, low)]
        zz = [lo + yy for lo, yy in zip(low, y)]
        y = [mmp(z, pp) for z, pp in zip(zz, p)]
        p = [pp - z - yy for pp, z, yy in zip(p, zz, y)]

    us = [[None] * nch for _ in heads]
    ws = [[None] * nch for _ in heads]
    for i, (h, pr) in enumerate(items):
        for half, p_half in enumerate(halves(p[i])):
            j = 2 * pr + half
            uw = rhs[h][c * j:c * (j + 1)] + jnp.dot(p_half, rhs_b[h][prow[pr]], preferred_element_type=F32)
            us[h][j] = uw[:, :GDN_DV]
            ws[h][j] = uw[:, GDN_DV:].astype(BF16)

    s = [st_ref[h] for h in heads]
    vprev = [None] * len(heads)
    for j in range(nch):
        rows = slice(c * j, c * (j + 1))
        sb = [s[h].astype(BF16) for h in heads]
        ws_qd = [jnp.dot(jnp.concatenate([ws[h][j], qd[h][rows]], axis=0), sb[h], preferred_element_type=F32)
                 for h in heads]
        vn16 = [(us[h][j] - ws_qd[h][0:c]).astype(BF16) for h in heads]
        o_inter = [ws_qd[h][c:2 * c] for h in heads]
        for h in heads:
            vpair = jnp.concatenate([vn16[h], jnp.zeros_like(vn16[h])] if j % 2 == 0 else [vprev[h], vn16[h]], axis=0)
            o = o_inter[h] + jnp.dot(attn[h * npair + j // 2][j % 2], vpair, preferred_element_type=F32)
            s[h] = jnp.exp(gam[h][c * (j + 1) - 1:c * (j + 1), :]) * s[h] + lax.dot_general(
                kd[h][rows], vn16[h], _TN, preferred_element_type=F32)
            o = o * lax.rsqrt(jnp.mean(o * o, axis=-1, keepdims=True) + RMS_EPS) * gnorm
            sq, hh = divmod(h, GDN_HEADS)
            gate = _silu(z_ref[sq, rows, 128 * hh:128 * (hh + 1)].astype(F32))
            o_ref[sq, rows, 128 * hh:128 * (hh + 1)] = (o * gate).astype(BF16)
        vprev = vn16
    for h in heads:
        st_ref[h] = s[h]


def _gdn(main, aux, alog_pad, dtb_pad, gdn_norm_g, *, bsz, seq, ts=512, nseq=2):
    nt = seq // ts
    hw = GDN_HEADS * GDN_DK
    assert bsz % nseq == 0
    main4 = main.reshape(bsz // nseq, nseq, seq, MAIN_W)
    aux4 = aux.reshape(bsz // nseq, nseq, seq, AUX_W)
    tok = lambda col: pl.BlockSpec((None, nseq, ts, hw), lambda b, t: (b, 0, t, col))
    out = pl.pallas_call(
        functools.partial(_gdn_kernel, ts=ts, nseq=nseq),
        grid=(bsz // nseq, nt),
        in_specs=[tok(3), tok(4), tok(5), tok(6),
                  pl.BlockSpec((None, nseq, ts, AUX_W), lambda b, t: (b, 0, t, 0)),
                  pl.BlockSpec((1, AUX_W), lambda b, t: (0, 0)),
                  pl.BlockSpec((1, AUX_W), lambda b, t: (0, 0)),
                  pl.BlockSpec((1, GDN_DV), lambda b, t: (0, 0))],
        out_specs=pl.BlockSpec((None, nseq, ts, GDN_WIDTH), lambda b, t: (b, 0, t, 0)),
        out_shape=jax.ShapeDtypeStruct((bsz // nseq, nseq, seq, GDN_WIDTH), BF16),
        scratch_shapes=[pltpu.VMEM((nseq * GDN_HEADS, GDN_DK, GDN_DV), F32)],
        compiler_params=_cparams("parallel", "arbitrary"),
        name="gdn",
    )(main4, main4, main4, main4, aux4, alog_pad, dtb_pad, gdn_norm_g)
    return out.reshape(bsz * seq, GDN_WIDTH)


def _pad_aux_row(v, offset):
    return jnp.zeros((1, AUX_W), F32).at[0, offset:offset + v.shape[0]].set(v.astype(F32))


ROUTE_W = 128
R_E1, R_E2, R_RANK1, R_RANK2, R_G1, R_G2 = range(6)
ROUTER_GROUP_COL = N_EXPERTS
POST_MIX_SPLIT = 8


def _layer_norm(h, g, b):
    mu = jnp.mean(h, axis=-1, keepdims=True)
    hc = h - mu
    var = jnp.mean(hc * hc, axis=-1, keepdims=True)
    return hc * lax.rsqrt(var + LN_EPS) * g + b


def _pack_bf16_pairs(x):
    w = x.shape[1] // 2
    lo = lax.bitcast_convert_type(x[:, :w].astype(BF16).astype(F32), jnp.uint32)
    hi = lax.bitcast_convert_type(x[:, w:].astype(BF16).astype(F32), jnp.uint32)
    return (lo >> 16) | (hi & jnp.uint32(0xFFFF0000))


def _unpack_bf16_pairs(p):
    lo = lax.bitcast_convert_type(p << 16, F32)
    hi = lax.bitcast_convert_type(p & jnp.uint32(0xFFFF0000), F32)
    return lo, hi


HALF = D_MODEL // 2
LANES = 128
TOK_ROWS = HALF // LANES


def _store_token_rows(ref, packed):
    m = packed.shape[0]
    for c in range(TOK_ROWS):
        ref[pl.ds(c, m, stride=TOK_ROWS), :] = packed[:, LANES * c:LANES * (c + 1)]


def _load_token_rows(ref, m):
    return jnp.concatenate([ref[pl.ds(c, m, stride=TOK_ROWS), :] for c in range(TOK_ROWS)], axis=1)


def _post_mix_kernel(oa_ref, ob_ref, x_ref, wo_ref, g_ref, b_ref, wr_ref, br_ref,
                     x1p_ref, route_ref, gates_ref, cnt_ref, carry_ref, *, tm):
    @pl.when(pl.program_id(0) == 0)
    def _():
        carry_ref[...] = jnp.zeros_like(carry_ref)

    m = tm // POST_MIX_SPLIT
    groups = range(POST_MIX_SPLIT)
    rs = [slice(h * m, (h + 1) * m) for h in groups]
    y = [jnp.dot(oa_ref[rs[h], :], wo_ref[0:GLA_WIDTH, :], preferred_element_type=F32)
         + jnp.dot(ob_ref[rs[h], :], wo_ref[GLA_WIDTH:, :], preferred_element_type=F32) for h in groups]
    x1 = [_layer_norm(ALPHA_DN * x_ref[rs[h], :] + y[h], g_ref[...], b_ref[...]) for h in groups]
    for h in groups:
        _store_token_rows(x1p_ref.at[pl.ds(h * m * TOK_ROWS, m * TOK_ROWS), :], _pack_bf16_pairs(x1[h]))

    xh = [x.astype(BF16) for x in x1]
    xl = [(x - xb.astype(F32)).astype(BF16) for x, xb in zip(x1, xh)]
    hw = [lax.dot_general(wr_ref[...], xb, _NT, preferred_element_type=F32) for xb in xh]
    lw = [lax.dot_general(wr_ref[0:ROUTE_W, :], xb, _NT, preferred_element_type=F32) for xb in xl]
    logits = [a[:ROUTE_W] + a[ROUTE_W:] + b + br_ref[...] for a, b in zip(hw, lw)]
    sub = lax.broadcasted_iota(jnp.int32, (ROUTE_W, m), 0)
    big = jnp.int32(1 << 20)
    neg = jnp.float32(-jnp.inf)

    def first_argmax(vals):
        mx = jnp.max(vals, axis=0, keepdims=True)
        idx = jnp.min(jnp.where(vals == mx, sub, big), axis=0, keepdims=True)
        return mx, idx

    is_group = (sub >= ROUTER_GROUP_COL) & (sub < ROUTER_GROUP_COL + N_GROUPS)
    gl = [jnp.where(is_group, lg, neg) for lg in logits]
    gtop = [first_argmax(v) for v in gl]
    p_group = [1.0 / jnp.sum(jnp.exp(v - t[0]), axis=0, keepdims=True) for v, t in zip(gl, gtop)]
    el = [jnp.where((sub // EXPERTS_PER_GROUP) == (t[1] - ROUTER_GROUP_COL), lg, neg) for lg, t in zip(logits, gtop)]
    top1 = [first_argmax(v) for v in el]
    top2 = [first_argmax(jnp.where(sub == t[1], neg, v)) for v, t in zip(el, top1)]
    ex = [jnp.exp(b[0] - a[0]) for a, b in zip(top1, top2)]
    w1 = [1.0 / (1.0 + e) for e in ex]
    gate1 = [p * w for p, w in zip(p_group, w1)]
    gate2 = [p * (e * w) for p, e, w in zip(p_group, ex, w1)]

    oh1 = [sub == t[1] for t in top1]
    oh2 = [sub == t[1] for t in top2]
    ri = lax.broadcasted_iota(jnp.int32, (m, m), 0)
    ci = lax.broadcasted_iota(jnp.int32, (m, m), 1)
    ustrict = (ri < ci).astype(BF16)
    oh1f = [o.astype(F32) for o in oh1]
    oh2f = [o.astype(F32) for o in oh2]
    c1 = [jnp.dot(o.astype(BF16), ustrict, preferred_element_type=F32) for o in oh1f]
    c2 = [jnp.dot(o.astype(BF16), ustrict, preferred_element_type=F32) for o in oh2f]
    tot1 = [jnp.sum(o, axis=1, keepdims=True) for o in oh1f]
    tot2 = [jnp.sum(o, axis=1, keepdims=True) for o in oh2f]
    carry = carry_ref[...]
    eye8 = jnp.eye(8, ROUTE_W, dtype=F32)
    zero = jnp.zeros((1, m), F32)
    for h in groups:
        rank1 = jnp.sum(jnp.where(oh1[h], c1[h] + carry, 0.0), axis=0, keepdims=True)
        rank2 = jnp.sum(jnp.where(oh2[h], c2[h] + (carry + tot1[h]), 0.0), axis=0, keepdims=True)
        carry = carry + tot1[h] + tot2[h]
        route_t = jnp.concatenate([top1[h][1].astype(F32), top2[h][1].astype(F32), rank1, rank2,
                                   gate1[h], gate2[h], zero, zero], axis=0)
        route_ref[:, rs[h]] = route_t
        gates_ref[rs[h], :] = lax.dot_general(route_t, eye8, _TN, precision=HIGHEST, preferred_element_type=F32)
    carry_ref[...] = carry
    cnt_ref[...] = carry


def _post_mix(o_a, o_b, x2d, w_out_b, ln_g, ln_b, w_router, b_router, *, tm=2048):
    n = x2d.shape[0]
    tm = min(tm, n)
    return pl.pallas_call(
        functools.partial(_post_mix_kernel, tm=tm),
        grid=(n // tm,),
        in_specs=[pl.BlockSpec((tm, GLA_WIDTH), lambda i: (i, 0)),
                  pl.BlockSpec((tm, GDN_WIDTH), lambda i: (i, 0)),
                  pl.BlockSpec((tm, D_MODEL), lambda i: (i, 0)),
                  pl.BlockSpec((GLA_WIDTH + GDN_WIDTH, D_MODEL), lambda i: (0, 0)),
                  pl.BlockSpec((1, D_MODEL), lambda i: (0, 0)),
                  pl.BlockSpec((1, D_MODEL), lambda i: (0, 0)),
                  pl.BlockSpec((2 * ROUTE_W, D_MODEL), lambda i: (0, 0)),
                  pl.BlockSpec((ROUTE_W, 1), lambda i: (0, 0))],
        out_specs=[pl.BlockSpec((tm * TOK_ROWS, LANES), lambda i: (i, 0)),
                   pl.BlockSpec((8, tm), lambda i: (0, i)),
                   pl.BlockSpec((tm, ROUTE_W), lambda i: (i, 0)),
                   pl.BlockSpec((ROUTE_W, 1), lambda i: (0, 0))],
        out_shape=[jax.ShapeDtypeStruct((n * TOK_ROWS, LANES), jnp.uint32),
                   jax.ShapeDtypeStruct((8, n), F32),
                   jax.ShapeDtypeStruct((n, ROUTE_W), F32),
                   jax.ShapeDtypeStruct((ROUTE_W, 1), F32)],
        scratch_shapes=[pltpu.VMEM((ROUTE_W, 1), F32)],
        compiler_params=_cparams("arbitrary"),
        name="post_mix",
    )(o_a, o_b, x2d, w_out_b, ln_g, ln_b, w_router, b_router)


EXPERT_BLOCK = 256
BLOCKS_PER_STEP = 4

ROW_UNROLL = 8


def _plan_kernel(route_ref, cnt_ref, *dest_refs, tm, tb):
    counts = cnt_ref[...].astype(jnp.int32)
    padded = jnp.broadcast_to(((counts + (tb - 1)) & jnp.int32(-tb)).astype(F32), (ROUTE_W, ROUTE_W))
    pstart = (_chunk_cumsum(padded, ROUTE_W) - padded)[:, 0:1]
    route = route_ref[...]
    sub = lax.broadcasted_iota(jnp.int32, (ROUTE_W, tm), 0).astype(F32)
    for dest_ref, (ce, cr) in zip(dest_refs, ((R_E1, R_RANK1), (R_E2, R_RANK2))):
        off = jnp.sum(jnp.where(sub == route[ce:ce + 1, :], pstart, 0.0), axis=0, keepdims=True)
        dest_ref[...] = (off + route[cr:cr + 1, :]).astype(jnp.int32) * TOK_ROWS


def _plan(route, counts, *, tb, tm=2048):
    n = route.shape[1]
    tm = min(tm, n)
    dests = pl.pallas_call(
        functools.partial(_plan_kernel, tm=tm, tb=tb),
        grid=(n // tm,),
        in_specs=[pl.BlockSpec((8, tm), lambda i: (0, i)),
                  pl.BlockSpec((ROUTE_W, 1), lambda i: (0, 0))],
        out_specs=[pl.BlockSpec((1, tm), lambda i: (0, i))] * TOP_K,
        out_shape=[jax.ShapeDtypeStruct((1, n), jnp.int32)] * TOP_K,
        compiler_params=_cparams("parallel"),
        name="plan",
    )(route, counts)
    return [d.reshape(n) for d in dests]


def _dispatch_kernel(d1_ref, d2_ref, last_blk_ref, nvalid_ref, xp_ref, xs_ref, zbuf, sem, zsem, *, tm, tb, nb):
    base = pl.program_id(0) * tm
    rows = tb * TOK_ROWS

    @pl.when(pl.program_id(0) == 0)
    def _():
        zbuf[...] = jnp.zeros_like(zbuf)

        def zero_copy(blk):
            return pltpu.make_async_copy(zbuf, xs_ref.at[pl.ds(pl.multiple_of(blk * rows, rows), rows), :], zsem)

        def for_each_block(fn):
            def per_expert(e, carry):
                @pl.when(last_blk_ref[e] >= 0)
                def _():
                    fn(zero_copy(last_blk_ref[e]))
                return carry

            def per_tail(blk, carry):
                fn(zero_copy(blk))
                return carry
            lax.fori_loop(0, N_EXPERTS, per_expert, 0)
            lax.fori_loop(nvalid_ref[0], nb, per_tail, 0)

        for_each_block(lambda cp: cp.start())
        for_each_block(lambda cp: cp.wait())

    def body(j, carry):
        for u in range(ROW_UNROLL):
            r = j * ROW_UNROLL + u
            for k, d_ref in enumerate((d1_ref, d2_ref)):
                dst = pl.multiple_of(d_ref[base + r], TOK_ROWS)
                pltpu.make_async_copy(xp_ref.at[pl.ds(r * TOK_ROWS, TOK_ROWS), :],
                                      xs_ref.at[pl.ds(dst, TOK_ROWS), :], sem).start(priority=k)
        return carry
    lax.fori_loop(0, tm // ROW_UNROLL, body, 0)
    for _ in range(TOP_K):
        pltpu.make_async_copy(xp_ref, xs_ref.at[pl.ds(0, tm * TOK_ROWS), :], sem).wait()


def _dispatch(dest1, dest2, last_blk, nvalid, x1p, *, nb, tb=EXPERT_BLOCK, tm=2048):
    n = x1p.shape[0] // TOK_ROWS
    tm = min(tm, n)
    grid_spec = pltpu.PrefetchScalarGridSpec(
        num_scalar_prefetch=4,
        grid=(n // tm,),
        in_specs=[pl.BlockSpec((tm * TOK_ROWS, LANES), lambda i, a, b, lb, nv: (i, 0))],
        out_specs=pl.BlockSpec(memory_space=pl.ANY),
        scratch_shapes=[pltpu.VMEM((tb * TOK_ROWS, LANES), jnp.uint32),
                        pltpu.SemaphoreType.DMA(()),
                        pltpu.SemaphoreType.DMA(())],
    )
    return pl.pallas_call(
        functools.partial(_dispatch_kernel, tm=tm, tb=tb, nb=nb),
        grid_spec=grid_spec,
        out_shape=jax.ShapeDtypeStruct((nb * tb * TOK_ROWS, LANES), jnp.uint32),
        compiler_params=_cparams("arbitrary"),
        name="dispatch",
    )(dest1, dest2, last_blk, nvalid, x1p)


def _expert_kernel(blk_e_ref, first_ref, next_e_ref, nvalid_ref, xs_ref, wg_hbm, wu_hbm, wd_hbm, y_ref,
                   wgs, wus, wds, wgb, wub, wdb, sem, *, tb):
    rows = tb * TOK_ROWS

    def weight_copies(e, slot):
        return [pltpu.make_async_copy(src.at[e], dst.at[slot], sem.at[slot, j])
                for j, (src, dst) in enumerate(((wg_hbm, wgs), (wu_hbm, wus), (wd_hbm, wds)))]

    @pl.when((pl.program_id(0) == 0) & (nvalid_ref[0] > 0))
    def _():
        for cp in weight_copies(blk_e_ref[0], first_ref[0] - 1):
            cp.start()

    def run(i, sub, nblk):
        @pl.when(first_ref[i] > 0)
        def _():
            slot = first_ref[i] - 1

            @pl.when(next_e_ref[i] >= 0)
            def _():
                for cp in weight_copies(next_e_ref[i], 1 - slot):
                    cp.start()

            for cp in weight_copies(blk_e_ref[i], slot):
                cp.wait()
            wgb[...] = wgs[slot].astype(BF16)
            wub[...] = wus[slot].astype(BF16)
            wdb[...] = wds[slot].astype(BF16)
            mlp(sub, nblk)

        @pl.when(first_ref[i] == 0)
        def _():
            mlp(sub, nblk)

    def mlp(sub, nblk):
        m = nblk * tb
        span = pl.ds(sub * rows, nblk * rows)
        lo, hi = _unpack_bf16_pairs(_load_token_rows(xs_ref.at[span, :], m))
        xb = jnp.concatenate([lo.astype(BF16), hi.astype(BF16)], axis=1)
        hg = jnp.dot(xb, wgb[...], preferred_element_type=F32)
        hu = jnp.dot(xb, wub[...], preferred_element_type=F32)
        hid = (_silu(hg) * hu).astype(BF16)
        _store_token_rows(y_ref.at[span, :], _pack_bf16_pairs(jnp.dot(hid, wdb[...], preferred_element_type=F32)))

    nvalid = nvalid_ref[0]
    for sub in range(0, BLOCKS_PER_STEP, 2):
        i0 = pl.program_id(0) * BLOCKS_PER_STEP + sub
        i1 = i0 + 1
        paired = (i1 < nvalid) & (first_ref[i1] == 0)

        @pl.when(paired)
        def _():
            run(i0, sub, 2)

        @pl.when(jnp.logical_not(paired))
        def _():
            @pl.when(i0 < nvalid)
            def _():
                run(i0, sub, 1)

            @pl.when(i1 < nvalid)
            def _():
                run(i1, sub + 1, 1)

        for s_, i in ((sub, i0), (sub + 1, i1)):
            @pl.when(i >= nvalid)
            def _():
                y_ref[pl.ds(s_ * rows, rows), :] = jnp.zeros((rows, LANES), jnp.uint32)


def _experts(blk_e, first, next_e, nvalid, xs, w_gate, w_up, w_down, *, tb=EXPERT_BLOCK):
    nb = blk_e.shape[0]
    assert nb % BLOCKS_PER_STEP == 0
    step_rows = BLOCKS_PER_STEP * tb * TOK_ROWS
    last_step = lambda nv: jnp.maximum((nv[0] - 1) // BLOCKS_PER_STEP, 0)
    grid_spec = pltpu.PrefetchScalarGridSpec(
        num_scalar_prefetch=4,
        grid=(nb // BLOCKS_PER_STEP,),
        in_specs=[pl.BlockSpec((step_rows, LANES), lambda s, be, fr, ne, nv: (jnp.minimum(s, last_step(nv)), 0)),
                  pl.BlockSpec(memory_space=pl.ANY),
                  pl.BlockSpec(memory_space=pl.ANY),
                  pl.BlockSpec(memory_space=pl.ANY)],
        out_specs=pl.BlockSpec((step_rows, LANES), lambda s, be, fr, ne, nv: (s, 0)),
        scratch_shapes=[pltpu.VMEM((2, D_MODEL, D_EXPERT), F32),
                        pltpu.VMEM((2, D_MODEL, D_EXPERT), F32),
                        pltpu.VMEM((2, D_EXPERT, D_MODEL), F32),
                        pltpu.VMEM((D_MODEL, D_EXPERT), BF16),
                        pltpu.VMEM((D_MODEL, D_EXPERT), BF16),
                        pltpu.VMEM((D_EXPERT, D_MODEL), BF16),
                        pltpu.SemaphoreType.DMA((2, 3))],
    )
    return pl.pallas_call(
        functools.partial(_expert_kernel, tb=tb),
        grid_spec=grid_spec,
        out_shape=jax.ShapeDtypeStruct((nb * tb * TOK_ROWS, LANES), jnp.uint32),
        compiler_params=_cparams("arbitrary"),
        name="experts",
    )(blk_e, first, next_e, nvalid, xs, w_gate, w_up, w_down)


def _combine_kernel(d1_ref, d2_ref, y_hbm, x1_ref, route_ref, g_ref, b_ref, o_ref, ybuf, sem, *, tt):
    i = pl.program_id(0)
    nsteps = pl.num_programs(0)
    slot = i % 2

    def issue(tile, slot_):
        def body(j, carry):
            for u in range(ROW_UNROLL):
                r = j * ROW_UNROLL + u
                for k, d_ref in enumerate((d1_ref, d2_ref)):
                    src = pl.multiple_of(d_ref[tile * tt + r], TOK_ROWS)
                    pltpu.make_async_copy(y_hbm.at[pl.ds(src, TOK_ROWS), :],
                                          ybuf.at[slot_, k, pl.ds(r * TOK_ROWS, TOK_ROWS), :],
                                          sem.at[slot_]).start(priority=k)
            return carry
        lax.fori_loop(0, tt // ROW_UNROLL, body, 0)

    @pl.when(i == 0)
    def _():
        issue(0, 0)

    @pl.when(i + 1 < nsteps)
    def _():
        issue(i + 1, 1 - slot)

    for k in range(TOP_K):
        pltpu.make_async_copy(y_hbm.at[pl.ds(0, tt * TOK_ROWS), :], ybuf.at[slot, k], sem.at[slot]).wait()
    route = route_ref[...]
    g1 = route[:, R_G1:R_G1 + 1]
    g2 = route[:, R_G2:R_G2 + 1]
    lo1, hi1 = _unpack_bf16_pairs(_load_token_rows(ybuf.at[slot, 0], tt))
    lo2, hi2 = _unpack_bf16_pairs(_load_token_rows(ybuf.at[slot, 1], tt))
    x_lo, x_hi = _unpack_bf16_pairs(_load_token_rows(x1_ref, tt))
    h_lo = ALPHA_DN * x_lo + (g1 * lo1 + g2 * lo2)
    h_hi = ALPHA_DN * x_hi + (g1 * hi1 + g2 * hi2)
    mu = (jnp.sum(h_lo, axis=-1, keepdims=True) + jnp.sum(h_hi, axis=-1, keepdims=True)) * (1.0 / D_MODEL)
    c_lo = h_lo - mu
    c_hi = h_hi - mu
    var = (jnp.sum(c_lo * c_lo, axis=-1, keepdims=True) + jnp.sum(c_hi * c_hi, axis=-1, keepdims=True)) * (1.0 / D_MODEL)
    inv = lax.rsqrt(var + LN_EPS)
    o_ref[:, 0:HALF] = c_lo * inv * g_ref[:, 0:HALF] + b_ref[:, 0:HALF]
    o_ref[:, HALF:] = c_hi * inv * g_ref[:, HALF:] + b_ref[:, HALF:]


def _combine(dest1, dest2, ybuf, x1p, route, ln_g, ln_b, *, tt=512):
    n = x1p.shape[0] // TOK_ROWS
    tt = min(tt, n)
    grid_spec = pltpu.PrefetchScalarGridSpec(
        num_scalar_prefetch=2,
        grid=(n // tt,),
        in_specs=[pl.BlockSpec(memory_space=pl.ANY),
                  pl.BlockSpec((tt * TOK_ROWS, LANES), lambda i, a, b: (i, 0)),
                  pl.BlockSpec((tt, ROUTE_W), lambda i, a, b: (i, 0)),
                  pl.BlockSpec((1, D_MODEL), lambda i, a, b: (0, 0)),
                  pl.BlockSpec((1, D_MODEL), lambda i, a, b: (0, 0))],
        out_specs=pl.BlockSpec((tt, D_MODEL), lambda i, a, b: (i, 0)),
        scratch_shapes=[pltpu.VMEM((2, TOP_K, tt * TOK_ROWS, LANES), jnp.uint32),
                        pltpu.SemaphoreType.DMA((2,))],
    )
    return pl.pallas_call(
        functools.partial(_combine_kernel, tt=tt),
        grid_spec=grid_spec,
        out_shape=jax.ShapeDtypeStruct((n, D_MODEL), F32),
        compiler_params=_cparams("arbitrary"),
        name="combine",
    )(dest1, dest2, ybuf, x1p, route, ln_g, ln_b)


def _block_plan(counts_row, n, tb):
    counts = counts_row[:N_EXPERTS, 0].astype(jnp.int32)
    pends = jnp.cumsum((counts + tb - 1) // tb * tb)
    nb = (n * TOP_K) // tb + N_EXPERTS
    nvalid = (pends[-1] // tb).astype(jnp.int32)
    blk = jnp.arange(nb, dtype=jnp.int32)
    blk_e = jnp.sum((pends[None, :] <= (blk * tb)[:, None]).astype(jnp.int32), axis=1)
    blk_e = jnp.minimum(blk_e, N_EXPERTS - 1)
    blk_e = jnp.where(blk < nvalid, blk_e, blk_e[jnp.maximum(nvalid - 1, 0)])
    first = jnp.concatenate([jnp.ones((1,), jnp.int32), (blk_e[1:] != blk_e[:-1]).astype(jnp.int32)])
    slot_e = (jnp.cumsum((counts > 0).astype(jnp.int32)) - 1) % 2
    first = first * (1 + slot_e[blk_e])
    ids = jnp.arange(N_EXPERTS, dtype=jnp.int32)
    later = lax.cummin(jnp.where(counts > 0, ids, N_EXPERTS), axis=0, reverse=True)
    nxt = jnp.concatenate([later[1:], jnp.full((1,), N_EXPERTS, jnp.int32)])
    nxt = jnp.where(nxt < N_EXPERTS, nxt, -1)
    last_blk = jnp.where(counts > 0, pends // tb - 1, -1).astype(jnp.int32)
    return blk_e, first, nxt[blk_e], nvalid.reshape(1), last_blk


def kernel(x, w_in, w_gk_up, b_gk, conv_w, a_log, dt_bias, gla_norm_g, gdn_norm_g, w_out, ln1_g, ln1_b, w_router_group, b_router_group, w_router_expert, b_router_expert, w_gate, w_up, w_down, ln2_g, ln2_b):
    bsz, seq, d = x.shape
    n = bsz * seq
    x2d = x.reshape(n, d)
    w_main, w_aux = _split_w_in(w_in[0])
    main, aux = _in_proj(x2d, w_main, w_aux, conv_w[0].astype(F32), seq=seq)
    wgk = jnp.zeros((AUX_W, 256), F32).at[:GLA_GATE_RANK].set(w_gk_up[0])
    o_a = _gla(main, aux, wgk, b_gk[0][None], gla_norm_g[0][None], bsz=bsz, seq=seq)
    o_b = _gdn(main, aux, _pad_aux_row(a_log[0], AUX_AB), _pad_aux_row(dt_bias[0], AUX_AB),
               gdn_norm_g[0][None], bsz=bsz, seq=seq)
    rpad = jnp.zeros((ROUTE_W - N_EXPERTS - N_GROUPS, d), F32)
    w_router = jnp.concatenate([w_router_expert[0].T, w_router_group[0].T, rpad], axis=0)
    w_router_hi = w_router.astype(BF16)
    w_router = jnp.concatenate([w_router_hi, (w_router - w_router_hi.astype(F32)).astype(BF16)], axis=0)
    b_router = jnp.concatenate([b_router_expert[0], b_router_group[0], rpad[:, 0]])[:, None]
    x1p, route, gates, counts = _post_mix(o_a, o_b, x2d, w_out[0].astype(BF16), ln1_g[0][None], ln1_b[0][None],
                                          w_router, b_router)
    dest = _plan(route, counts, tb=EXPERT_BLOCK)
    blk_e, first, next_e, nvalid, last_blk = _block_plan(counts, n, EXPERT_BLOCK)
    xs = _dispatch(dest[0], dest[1], last_blk, nvalid, x1p, nb=blk_e.shape[0])
    ybuf = _experts(blk_e, first, next_e, nvalid, xs, w_gate[0], w_up[0], w_down[0])
    out = _combine(dest[0], dest[1], ybuf, x1p, gates, ln2_g[0][None], ln2_b[0][None])
    return out.reshape(bsz, seq, d)
```

```python
import functools

import jax
import jax.numpy as jnp
import numpy as np
from jax import lax
from jax.experimental import pallas as pl
from jax.experimental.pallas import tpu as pltpu

F32 = jnp.float32
BF16 = jnp.bfloat16
HIGHEST = lax.Precision.HIGHEST

D_MODEL = 1024
DEPTH = 1
GLA_HEADS = 4
GLA_DK = 64
GLA_DV = 128
GLA_WIDTH = GLA_HEADS * GLA_DV
GLA_GATE_RANK = 16
GLA_GATE_NORM = 16.0
GDN_HEADS = 4
GDN_DK = 128
GDN_DV = 128
GDN_WIDTH = GDN_HEADS * GDN_DV
CONV_W = 4
CHUNK = 64
N_GROUPS = 8
EXPERTS_PER_GROUP = 8
N_EXPERTS = N_GROUPS * EXPERTS_PER_GROUP
TOP_K = 2
D_EXPERT = 512
LN_EPS = 1e-5
RMS_EPS = 1e-6
ALPHA_DN = (2.0 * DEPTH) ** 0.25

MAIN_W = 2 * GLA_HEADS * GLA_DK + 2 * GLA_WIDTH + 4 * GDN_WIDTH
AUX_W = 128
AUX_LRA = 0
AUX_BB = GLA_GATE_RANK
AUX_AB = GLA_GATE_RANK + GDN_HEADS
SUB = 16
VMEM_LIMIT = 56 * 1024 * 1024


def _cparams(*sem):
    return pltpu.CompilerParams(dimension_semantics=sem, vmem_limit_bytes=VMEM_LIMIT)


def _sigmoid(x):
    return 1.0 / (1.0 + jnp.exp(-x))


def _silu(x):
    return x * _sigmoid(x)


def _log_sigmoid(x):
    return jnp.minimum(x, 0.0) - jnp.log(1.0 + jnp.exp(-jnp.abs(x)))


def _softplus(x):
    return jnp.maximum(x, 0.0) + jnp.log(1.0 + jnp.exp(-jnp.abs(x)))


def _chunk_cumsum(x, c):
    pos = lax.broadcasted_iota(jnp.int32, x.shape, 0) % c
    shift = 1
    while shift < c:
        x = x + jnp.where(pos >= shift, pltpu.roll(x, shift, 0), 0.0)
        shift *= 2
    return x


CONV_PAD = 8
GDN_HW = GDN_HEADS * GDN_DK
GDN_QKV_BLOCK0 = (2 * GLA_HEADS * GLA_DK + 2 * GLA_WIDTH) // GDN_HW


def _in_proj_kernel(x_ref, wm_ref, wa_ref, cw_ref, main_ref, aux_ref, stage_ref, *, tm, tiles_per_seq):
    xb = x_ref[...].astype(BF16)

    @pl.when((pl.program_id(0) % tiles_per_seq) == 0)
    def _():
        stage_ref[:, 0:CONV_PAD, :] = jnp.zeros((3 * GDN_HEADS, CONV_PAD, GDN_DK), F32)

    for j in range(MAIN_W // GDN_HW):
        sl = slice(j * GDN_HW, (j + 1) * GDN_HW)
        r = jnp.dot(xb, wm_ref[:, sl], preferred_element_type=F32)
        b = j - GDN_QKV_BLOCK0
        if not 0 <= b < 3:
            main_ref[:, sl] = r.astype(BF16)
            continue
        for h in range(GDN_HEADS):
            st = stage_ref.at[b * GDN_HEADS + h]
            cols = slice(b * GDN_HW + h * GDN_DK, b * GDN_HW + (h + 1) * GDN_DK)
            st[CONV_PAD:, :] = r[:, h * GDN_DK:(h + 1) * GDN_DK]
            acc = jnp.zeros((tm, GDN_DK), F32)
            for i in range(CONV_W):
                lo = CONV_PAD - (CONV_W - 1) + i
                acc = acc + st[lo:lo + tm, :] * cw_ref[i:i + 1, cols]
            st[0:CONV_PAD, :] = st[tm:tm + CONV_PAD, :]
            yh = _silu(acc)
            if b < 2:
                yh = yh * lax.rsqrt(jnp.sum(yh * yh, axis=-1, keepdims=True) + RMS_EPS)
            if b == 0:
                yh = yh * (GDN_DK ** -0.5)
            main_ref[:, j * GDN_HW + h * GDN_DK:j * GDN_HW + (h + 1) * GDN_DK] = yh.astype(BF16)
    aux_ref[...] = jnp.dot(xb, wa_ref[...], preferred_element_type=F32)


def _in_proj(x2d, w_main, w_aux, conv_w, *, seq, tm=1024):
    n = x2d.shape[0]
    tm = min(tm, seq)
    return pl.pallas_call(
        functools.partial(_in_proj_kernel, tm=tm, tiles_per_seq=seq // tm),
        grid=(n // tm,),
        in_specs=[pl.BlockSpec((tm, D_MODEL), lambda i: (i, 0)),
                  pl.BlockSpec((D_MODEL, MAIN_W), lambda i: (0, 0), pipeline_mode=pl.Buffered(1)),
                  pl.BlockSpec((D_MODEL, AUX_W), lambda i: (0, 0), pipeline_mode=pl.Buffered(1)),
                  pl.BlockSpec((CONV_W, 3 * GDN_HW), lambda i: (0, 0))],
        out_specs=[pl.BlockSpec((tm, MAIN_W), lambda i: (i, 0)),
                   pl.BlockSpec((tm, AUX_W), lambda i: (i, 0))],
        out_shape=[jax.ShapeDtypeStruct((n, MAIN_W), BF16),
                   jax.ShapeDtypeStruct((n, AUX_W), F32)],
        scratch_shapes=[pltpu.VMEM((3 * GDN_HEADS, tm + CONV_PAD, GDN_DK), F32)],
        compiler_params=_cparams("arbitrary"),
        name="in_proj",
    )(x2d, w_main, w_aux, conv_w)


def _split_w_in(w_in):
    sizes = (GLA_HEADS * GLA_DK, GLA_HEADS * GLA_DK, GLA_WIDTH, GLA_WIDTH, GLA_GATE_RANK,
             GDN_HEADS * GDN_DK, GDN_HEADS * GDN_DK, GDN_WIDTH, GDN_WIDTH, GDN_HEADS, GDN_HEADS)
    offs = np.cumsum((0,) + sizes)
    seg = [w_in[:, offs[i]:offs[i + 1]] for i in range(len(sizes))]
    qa, ka, va, ra, lra, qb, kb, vb, zb, bb, ab = seg
    w_main = jnp.concatenate([qa, ka, va, ra, qb, kb, vb, zb], axis=1).astype(BF16)
    pad = jnp.zeros((w_in.shape[0], AUX_W - GLA_GATE_RANK - 2 * GDN_HEADS), w_in.dtype)
    w_aux = jnp.concatenate([lra, bb, ab, pad], axis=1).astype(BF16)
    return w_main, w_aux


_NT = (((1,), (1,)), ((), ()))
_TN = (((0,), (0,)), ((), ()))


def _gla_kernel(q_ref, k_ref, v_ref, r_ref, aux_ref, wgk_ref, bgk_ref, g_ref, o_ref, st_ref, *, ts, nseq):
    npairs = 2 * nseq
    c = CHUNK
    nsub = c // SUB

    @pl.when(pl.program_id(1) == 0)
    def _():
        st_ref[...] = jnp.zeros_like(st_ref)

    gk_all = [_log_sigmoid(jnp.dot(aux_ref[sq], wgk_ref[...], precision=HIGHEST, preferred_element_type=F32)
                           + bgk_ref[...]) * (1.0 / GLA_GATE_NORM) for sq in range(nseq)]

    row128 = lax.broadcasted_iota(jnp.int32, (c, 128), 0)
    lane128 = lax.broadcasted_iota(jnp.int32, (1, 128), 1)
    lane_masks = [(lane128 // GLA_DK) == hh for hh in range(2)]
    sub_valid = [row128 < SUB * (i + 1) for i in range(nsub)]
    ar = lax.broadcasted_iota(jnp.int32, (c, nsub * c), 0)
    ac = lax.broadcasted_iota(jnp.int32, (c, nsub * c), 1)
    amask = ((ac // c) == (ar // SUB)) & ((ac % c) <= ar)
    scale = GLA_DK ** -0.5
    gnorm = g_ref[...]

    nch = ts // c
    rows = [slice(ch * c, (ch + 1) * c) for ch in range(nch)]
    b_cum = [_chunk_cumsum(g, c) for g in gk_all]

    cp = [(ch, p) for ch in range(nch) for p in range(npairs)]
    qt, qd, kd, kstack, dlast = {}, {}, {}, {}, {}
    for ch, p in cp:
        sq, pp = divmod(p, 2)
        lanes = slice(128 * pp, 128 * (pp + 1))
        bp = b_cum[sq][rows[ch], lanes]
        qp = q_ref[sq, rows[ch], lanes].astype(F32) * scale
        kp = k_ref[sq, rows[ch], lanes].astype(F32)
        c_row = jnp.concatenate(
            [jnp.broadcast_to(bp[SUB * i:SUB * i + 1, :], (SUB, 128)) for i in range(nsub)], axis=0)
        qt[ch, p] = (qp * jnp.exp(bp - c_row)).astype(BF16)
        qd[ch, p] = qp * jnp.exp(bp)
        b_last = bp[c - 1:c, :]
        dlast[ch, p] = jnp.exp(b_last)
        kd[ch, p] = kp * jnp.exp(b_last - bp)
        kts = []
        for i in range(nsub):
            e = jnp.exp(jnp.where(sub_valid[i], bp[SUB * i:SUB * i + 1, :] - bp, 0.0))
            kts.append(jnp.where(sub_valid[i], kp * e, 0.0))
        kstack[ch, p] = jnp.concatenate(kts, axis=0)

    cph = [(ch, p, hh) for ch, p in cp for hh in range(2)]
    vh = {(ch, p, hh): v_ref[p // 2, rows[ch], 128 * (2 * (p % 2) + hh):128 * (2 * (p % 2) + hh + 1)]
          for ch, p, hh in cph}
    r_mat = {(ch, p, hh): lax.dot_general(qt[ch, p], jnp.where(lane_masks[hh], kstack[ch, p], 0.0).astype(BF16),
                                          _NT, preferred_element_type=F32) for ch, p, hh in cph}
    o_intra = {k: jnp.dot(jnp.where(amask, r_mat[k], 0.0).astype(BF16), jnp.concatenate([vh[k]] * nsub, axis=0),
                          preferred_element_type=F32) for k in cph}
    upd = {(ch, p, hh): lax.dot_general(vh[ch, p, hh], jnp.where(lane_masks[hh], kd[ch, p], 0.0).astype(BF16), _TN,
                                        preferred_element_type=F32) for ch, p, hh in cph}

    st = [st_ref[p] for p in range(npairs)]
    for ch in range(nch):
        st_b = [s.astype(BF16) for s in st]
        o_pair = [lax.dot_general(
            jnp.concatenate([jnp.where(lane_masks[hh], qd[ch, p], 0.0).astype(BF16) for hh in range(2)], axis=0),
            st_b[p], _NT, preferred_element_type=F32) for p in range(npairs)]
        o_inter = {(p, hh): o_pair[p][hh * c:(hh + 1) * c] for p in range(npairs) for hh in range(2)}
        st = [st[p] * dlast[ch, p] + upd[ch, p, 0] + upd[ch, p, 1] for p in range(npairs)]
        for p in range(npairs):
            for hh in range(2):
                sq, h = p // 2, 2 * (p % 2) + hh
                o = o_intra[ch, p, hh] + o_inter[p, hh]
                o = o * lax.rsqrt(jnp.mean(o * o, axis=-1, keepdims=True) + RMS_EPS) * gnorm
                gate = _silu(r_ref[sq, rows[ch], 128 * h:128 * (h + 1)].astype(F32))
                o_ref[sq, rows[ch], 128 * h:128 * (h + 1)] = (o * gate).astype(BF16)
    for p in range(npairs):
        st_ref[p] = st[p]


def _gla(main, aux, wgk_pad, b_gk, gla_norm_g, *, bsz, seq, ts=512, nseq=2):
    nt = seq // ts
    assert bsz % nseq == 0
    main4 = main.reshape(bsz // nseq, nseq, seq, MAIN_W)
    aux4 = aux.reshape(bsz // nseq, nseq, seq, AUX_W)
    tok = lambda width, col: pl.BlockSpec((None, nseq, ts, width), lambda b, t: (b, 0, t, col))
    out = pl.pallas_call(
        functools.partial(_gla_kernel, ts=ts, nseq=nseq),
        grid=(bsz // nseq, nt),
        in_specs=[tok(256, 0), tok(256, 1), tok(512, 1), tok(512, 2),
                  tok(AUX_W, 0),
                  pl.BlockSpec((AUX_W, 256), lambda b, t: (0, 0)),
                  pl.BlockSpec((1, 256), lambda b, t: (0, 0)),
                  pl.BlockSpec((1, 128), lambda b, t: (0, 0))],
        out_specs=tok(GLA_WIDTH, 0),
        out_shape=jax.ShapeDtypeStruct((bsz // nseq, nseq, seq, GLA_WIDTH), BF16),
        scratch_shapes=[pltpu.VMEM((2 * nseq, 128, 128), F32)],
        compiler_params=_cparams("parallel", "arbitrary"),
        name="gla",
    )(main4, main4, main4, main4, aux4, wgk_pad, b_gk, gla_norm_g)
    return out.reshape(bsz * seq, GLA_WIDTH)


def _gdn_kernel(q_ref, k_ref, v_ref, z_ref, aux_ref, alog_ref, dtb_ref, g_ref, o_ref, st_ref, *, ts, nseq):
    c = CHUNK
    nch = ts // c

    @pl.when(pl.program_id(1) == 0)
    def _():
        st_ref[...] = jnp.zeros_like(st_ref)

    beta_full, gam_full, gam_t = [], [], []
    for sq in range(nseq):
        aux = aux_ref[sq]
        beta_full.append(_sigmoid(aux))
        g_full = -jnp.exp(alog_ref[...]) * _softplus(aux + dtb_ref[...])
        gam_full.append(_chunk_cumsum(g_full, c))
        gam_t.append(gam_full[-1].T)
    gnorm = g_ref[...]

    row = lax.broadcasted_iota(jnp.int32, (c, 2 * c), 0)
    lane = lax.broadcasted_iota(jnp.int32, (c, 2 * c), 1)
    left = lane < c
    col = lane % c
    causal = col <= row
    strict = col < row
    same32 = (row // 32) == (col // 32)
    same16 = (row // 16) == (col // 16)
    lvl0 = same16 & strict
    lvl1 = same32 & jnp.logical_not(same16) & strict
    lvl2 = jnp.logical_not(same32) & strict

    def pack_rows(x2):
        return jnp.where(left, x2[0:c], x2[c:2 * c])

    def pack_cols(x1):
        return jnp.where(left, x1[0:c], x1[c:2 * c])

    def halves(x):
        return jnp.where(left, x, 0.0).astype(BF16), jnp.where(left, 0.0, x).astype(BF16)

    def mmp(x, y):
        return jnp.dot(x.astype(BF16), jnp.concatenate(halves(y), axis=0), preferred_element_type=F32)

    heads = range(nseq * GDN_HEADS)
    npair = nch // 2
    items = [(h, pr) for h in heads for pr in range(npair)]
    prow = [slice(2 * c * pr, 2 * c * (pr + 1)) for pr in range(npair)]

    qn, kn, beta, gam, gam_r, kb, rhs, rhs_b, qd, kd = ([None] * len(heads) for _ in range(10))
    for h in heads:
        sq, hh = divmod(h, GDN_HEADS)
        hl = slice(128 * hh, 128 * (hh + 1))
        kb[h] = k_ref[sq, :, hl]
        qn[h] = q_ref[sq, :, hl].astype(F32)
        kn[h] = kb[h].astype(F32)
        vh = v_ref[sq, :, hl].astype(F32)
        beta[h] = beta_full[sq][:, AUX_BB + hh:AUX_BB + hh + 1]
        gam[h] = gam_full[sq][:, AUX_AB + hh:AUX_AB + hh + 1]
        gam_r[h] = gam_t[sq][AUX_AB + hh:AUX_AB + hh + 1, :]
        egam = jnp.exp(gam[h])
        rhs[h] = jnp.concatenate([beta[h] * vh, beta[h] * egam * kn[h]], axis=1)
        rhs_b[h] = rhs[h].astype(BF16)
        qd[h] = (qn[h] * egam).astype(BF16)
        gl_rows = jnp.concatenate(
            [jnp.broadcast_to(gam[h][c * (j + 1) - 1:c * (j + 1), :], (c, 1)) for j in range(nch)], axis=0)
        kd[h] = (kn[h] * jnp.exp(gl_rows - gam[h])).astype(BF16)

    kk = [pack_rows(lax.dot_general(kb[h][prow[pr]], kb[h][prow[pr]], _NT, preferred_element_type=F32))
          for h, pr in items]
    qk = [pack_rows(lax.dot_general(qn[h][prow[pr]].astype(BF16), kb[h][prow[pr]], _NT,
                                    preferred_element_type=F32)) for h, pr in items]
    dec = [jnp.exp(jnp.where(causal, pack_cols(gam[h][prow[pr]]) - gam_r[h][:, prow[pr]], 0.0)) for h, pr in items]
    a = [pack_cols(beta[h][prow[pr]]) * kk[i] * dec[i] for i, (h, pr) in enumerate(items)]
    attn = [halves(jnp.where(causal, qk[i] * dec[i], 0.0)) for i in range(len(items))]

    x1 = [jnp.where(lvl0, -ai, 0.0) for ai in a]
    x2 = [mmp(x, x) for x in x1]
    y = [mmp(x, xx) for x, xx in zip(x1, x2)]
    p = [x + xx + yy for x, xx, yy in zip(x1, x2, y)]
    xk = x2
    for _ in range(2):
        xk = [mmp(x, x) for x in xk]
        y = [mmp(pp, x) for pp, x in zip(p, xk)]
        p = [pp + x + yy for pp, x, yy in zip(p, xk, y)]
    for lvl in (lvl1, lvl2):
        low = [jnp.where(lvl, ai, 0.0) for ai in a]
        y = [mmp(pp, lo) for pp, lo in zip(p, low)]
        zz = [lo + yy for lo, yy in zip(low, y)]
        y = [mmp(z, pp) for z, pp in zip(zz, p)]
        p = [pp - z - yy for pp, z, yy in zip(p, zz, y)]

    us = [[None] * nch for _ in heads]
    ws = [[None] * nch for _ in heads]
    for i, (h, pr) in enumerate(items):
        for half, p_half in enumerate(halves(p[i])):
            j = 2 * pr + half
            uw = rhs[h][c * j:c * (j + 1)] + jnp.dot(p_half, rhs_b[h][prow[pr]], preferred_element_type=F32)
            us[h][j] = uw[:, :GDN_DV]
            ws[h][j] = uw[:, GDN_DV:].astype(BF16)

    s = [st_ref[h] for h in heads]
    vprev = [None] * len(heads)
    for j in range(nch):
        rows = slice(c * j, c * (j + 1))
        sb = [s[h].astype(BF16) for h in heads]
        ws_qd = [jnp.dot(jnp.concatenate([ws[h][j], qd[h][rows]], axis=0), sb[h], preferred_element_type=F32)
                 for h in heads]
        vn16 = [(us[h][j] - ws_qd[h][0:c]).astype(BF16) for h in heads]
        o_inter = [ws_qd[h][c:2 * c] for h in heads]
        for h in heads:
            vpair = jnp.concatenate([vn16[h], jnp.zeros_like(vn16[h])] if j % 2 == 0 else [vprev[h], vn16[h]], axis=0)
            o = o_inter[h] + jnp.dot(attn[h * npair + j // 2][j % 2], vpair, preferred_element_type=F32)
            s[h] = jnp.exp(gam[h][c * (j + 1) - 1:c * (j + 1), :]) * s[h] + lax.dot_general(
                kd[h][rows], vn16[h], _TN, preferred_element_type=F32)
            o = o * lax.rsqrt(jnp.mean(o * o, axis=-1, keepdims=True) + RMS_EPS) * gnorm
            sq, hh = divmod(h, GDN_HEADS)
            gate = _silu(z_ref[sq, rows, 128 * hh:128 * (hh + 1)].astype(F32))
            o_ref[sq, rows, 128 * hh:128 * (hh + 1)] = (o * gate).astype(BF16)
        vprev = vn16
    for h in heads:
        st_ref[h] = s[h]


def _gdn(main, aux, alog_pad, dtb_pad, gdn_norm_g, *, bsz, seq, ts=512, nseq=2):
    nt = seq // ts
    hw = GDN_HEADS * GDN_DK
    assert bsz % nseq == 0
    main4 = main.reshape(bsz // nseq, nseq, seq, MAIN_W)
    aux4 = aux.reshape(bsz // nseq, nseq, seq, AUX_W)
    tok = lambda col: pl.BlockSpec((None, nseq, ts, hw), lambda b, t: (b, 0, t, col))
    out = pl.pallas_call(
        functools.partial(_gdn_kernel, ts=ts, nseq=nseq),
        grid=(bsz // nseq, nt),
        in_specs=[tok(3), tok(4), tok(5), tok(6),
                  pl.BlockSpec((None, nseq, ts, AUX_W), lambda b, t: (b, 0, t, 0)),
                  pl.BlockSpec((1, AUX_W), lambda b, t: (0, 0)),
                  pl.BlockSpec((1, AUX_W), lambda b, t: (0, 0)),
                  pl.BlockSpec((1, GDN_DV), lambda b, t: (0, 0))],
        out_specs=pl.BlockSpec((None, nseq, ts, GDN_WIDTH), lambda b, t: (b, 0, t, 0)),
        out_shape=jax.ShapeDtypeStruct((bsz // nseq, nseq, seq, GDN_WIDTH), BF16),
        scratch_shapes=[pltpu.VMEM((nseq * GDN_HEADS, GDN_DK, GDN_DV), F32)],
        compiler_params=_cparams("parallel", "arbitrary"),
        name="gdn",
    )(main4, main4, main4, main4, aux4, alog_pad, dtb_pad, gdn_norm_g)
    return out.reshape(bsz * seq, GDN_WIDTH)


def _pad_aux_row(v, offset):
    return jnp.zeros((1, AUX_W), F32).at[0, offset:offset + v.shape[0]].set(v.astype(F32))


ROUTE_W = 128
R_E1, R_E2, R_RANK1, R_RANK2, R_G1, R_G2 = range(6)
ROUTER_GROUP_COL = N_EXPERTS
POST_MIX_SPLIT = 8


def _layer_norm(h, g, b):
    mu = jnp.mean(h, axis=-1, keepdims=True)
    hc = h - mu
    var = jnp.mean(hc * hc, axis=-1, keepdims=True)
    return hc * lax.rsqrt(var + LN_EPS) * g + b


def _pack_bf16_pairs(x):
    w = x.shape[1] // 2
    lo = lax.bitcast_convert_type(x[:, :w].astype(BF16).astype(F32), jnp.uint32)
    hi = lax.bitcast_convert_type(x[:, w:].astype(BF16).astype(F32), jnp.uint32)
    return (lo >> 16) | (hi & jnp.uint32(0xFFFF0000))


def _unpack_bf16_pairs(p):
    lo = lax.bitcast_convert_type(p << 16, F32)
    hi = lax.bitcast_convert_type(p & jnp.uint32(0xFFFF0000), F32)
    return lo, hi


HALF = D_MODEL // 2
LANES = 128
TOK_ROWS = HALF // LANES


def _store_token_rows(ref, packed):
    m = packed.shape[0]
    for c in range(TOK_ROWS):
        ref[pl.ds(c, m, stride=TOK_ROWS), :] = packed[:, LANES * c:LANES * (c + 1)]


def _load_token_rows(ref, m):
    return jnp.concatenate([ref[pl.ds(c, m, stride=TOK_ROWS), :] for c in range(TOK_ROWS)], axis=1)


def _post_mix_kernel(oa_ref, ob_ref, x_ref, wo_ref, g_ref, b_ref, wr_ref, br_ref,
                     x1p_ref, route_ref, gates_ref, cnt_ref, carry_ref, *, tm):
    @pl.when(pl.program_id(0) == 0)
    def _():
        carry_ref[...] = jnp.zeros_like(carry_ref)

    m = tm // POST_MIX_SPLIT
    groups = range(POST_MIX_SPLIT)
    rs = [slice(h * m, (h + 1) * m) for h in groups]
    y = [jnp.dot(oa_ref[rs[h], :], wo_ref[0:GLA_WIDTH, :], preferred_element_type=F32)
         + jnp.dot(ob_ref[rs[h], :], wo_ref[GLA_WIDTH:, :], preferred_element_type=F32) for h in groups]
    x1 = [_layer_norm(ALPHA_DN * x_ref[rs[h], :] + y[h], g_ref[...], b_ref[...]) for h in groups]
    for h in groups:
        _store_token_rows(x1p_ref.at[pl.ds(h * m * TOK_ROWS, m * TOK_ROWS), :], _pack_bf16_pairs(x1[h]))

    xh = [x.astype(BF16) for x in x1]
    xl = [(x - xb.astype(F32)).astype(BF16) for x, xb in zip(x1, xh)]
    hw = [lax.dot_general(wr_ref[...], xb, _NT, preferred_element_type=F32) for xb in xh]
    lw = [lax.dot_general(wr_ref[0:ROUTE_W, :], xb, _NT, preferred_element_type=F32) for xb in xl]
    logits = [a[:ROUTE_W] + a[ROUTE_W:] + b + br_ref[...] for a, b in zip(hw, lw)]
    sub = lax.broadcasted_iota(jnp.int32, (ROUTE_W, m), 0)
    big = jnp.int32(1 << 20)
    neg = jnp.float32(-jnp.inf)

    def first_argmax(vals):
        mx = jnp.max(vals, axis=0, keepdims=True)
        idx = jnp.min(jnp.where(vals == mx, sub, big), axis=0, keepdims=True)
        return mx, idx

    is_group = (sub >= ROUTER_GROUP_COL) & (sub < ROUTER_GROUP_COL + N_GROUPS)
    gl = [jnp.where(is_group, lg, neg) for lg in logits]
    gtop = [first_argmax(v) for v in gl]
    p_group = [1.0 / jnp.sum(jnp.exp(v - t[0]), axis=0, keepdims=True) for v, t in zip(gl, gtop)]
    el = [jnp.where((sub // EXPERTS_PER_GROUP) == (t[1] - ROUTER_GROUP_COL), lg, neg) for lg, t in zip(logits, gtop)]
    top1 = [first_argmax(v) for v in el]
    top2 = [first_argmax(jnp.where(sub == t[1], neg, v)) for v, t in zip(el, top1)]
    ex = [jnp.exp(b[0] - a[0]) for a, b in zip(top1, top2)]
    w1 = [1.0 / (1.0 + e) for e in ex]
    gate1 = [p * w for p, w in zip(p_group, w1)]
    gate2 = [p * (e * w) for p, e, w in zip(p_group, ex, w1)]

    oh1 = [sub == t[1] for t in top1]
    oh2 = [sub == t[1] for t in top2]
    ri = lax.broadcasted_iota(jnp.int32, (m, m), 0)
    ci = lax.broadcasted_iota(jnp.int32, (m, m), 1)
    ustrict = (ri < ci).astype(BF16)
    oh1f = [o.astype(F32) for o in oh1]
    oh2f = [o.astype(F32) for o in oh2]
    c1 = [jnp.dot(o.astype(BF16), ustrict, preferred_element_type=F32) for o in oh1f]
    c2 = [jnp.dot(o.astype(BF16), ustrict, preferred_element_type=F32) for o in oh2f]
    tot1 = [jnp.sum(o, axis=1, keepdims=True) for o in oh1f]
    tot2 = [jnp.sum(o, axis=1, keepdims=True) for o in oh2f]
    carry = carry_ref[...]
    eye8 = jnp.eye(8, ROUTE_W, dtype=F32)
    zero = jnp.zeros((1, m), F32)
    for h in groups:
        rank1 = jnp.sum(jnp.where(oh1[h], c1[h] + carry, 0.0), axis=0, keepdims=True)
        rank2 = jnp.sum(jnp.where(oh2[h], c2[h] + (carry + tot1[h]), 0.0), axis=0, keepdims=True)
        carry = carry + tot1[h] + tot2[h]
        route_t = jnp.concatenate([top1[h][1].astype(F32), top2[h][1].astype(F32), rank1, rank2,
                                   gate1[h], gate2[h], zero, zero], axis=0)
        route_ref[:, rs[h]] = route_t
        gates_ref[rs[h], :] = lax.dot_general(route_t, eye8, _TN, precision=HIGHEST, preferred_element_type=F32)
    carry_ref[...] = carry
    cnt_ref[...] = carry


def _post_mix(o_a, o_b, x2d, w_out_b, ln_g, ln_b, w_router, b_router, *, tm=2048):
    n = x2d.shape[0]
    tm = min(tm, n)
    return pl.pallas_call(
        functools.partial(_post_mix_kernel, tm=tm),
        grid=(n // tm,),
        in_specs=[pl.BlockSpec((tm, GLA_WIDTH), lambda i: (i, 0)),
                  pl.BlockSpec((tm, GDN_WIDTH), lambda i: (i, 0)),
                  pl.BlockSpec((tm, D_MODEL), lambda i: (i, 0)),
                  pl.BlockSpec((GLA_WIDTH + GDN_WIDTH, D_MODEL), lambda i: (0, 0)),
                  pl.BlockSpec((1, D_MODEL), lambda i: (0, 0)),
                  pl.BlockSpec((1, D_MODEL), lambda i: (0, 0)),
                  pl.BlockSpec((2 * ROUTE_W, D_MODEL), lambda i: (0, 0)),
                  pl.BlockSpec((ROUTE_W, 1), lambda i: (0, 0))],
        out_specs=[pl.BlockSpec((tm * TOK_ROWS, LANES), lambda i: (i, 0)),
                   pl.BlockSpec((8, tm), lambda i: (0, i)),
                   pl.BlockSpec((tm, ROUTE_W), lambda i: (i, 0)),
                   pl.BlockSpec((ROUTE_W, 1), lambda i: (0, 0))],
        out_shape=[jax.ShapeDtypeStruct((n * TOK_ROWS, LANES), jnp.uint32),
                   jax.ShapeDtypeStruct((8, n), F32),
                   jax.ShapeDtypeStruct((n, ROUTE_W), F32),
                   jax.ShapeDtypeStruct((ROUTE_W, 1), F32)],
        scratch_shapes=[pltpu.VMEM((ROUTE_W, 1), F32)],
        compiler_params=_cparams("arbitrary"),
        name="post_mix",
    )(o_a, o_b, x2d, w_out_b, ln_g, ln_b, w_router, b_router)


EXPERT_BLOCK = 256
BLOCKS_PER_STEP = 4

ROW_UNROLL = 8


def _plan_kernel(route_ref, cnt_ref, *dest_refs, tm, tb):
    counts = cnt_ref[...].astype(jnp.int32)
    padded = jnp.broadcast_to(((counts + (tb - 1)) & jnp.int32(-tb)).astype(F32), (ROUTE_W, ROUTE_W))
    pstart = (_chunk_cumsum(padded, ROUTE_W) - padded)[:, 0:1]
    route = route_ref[...]
    sub = lax.broadcasted_iota(jnp.int32, (ROUTE_W, tm), 0).astype(F32)
    for dest_ref, (ce, cr) in zip(dest_refs, ((R_E1, R_RANK1), (R_E2, R_RANK2))):
        off = jnp.sum(jnp.where(sub == route[ce:ce + 1, :], pstart, 0.0), axis=0, keepdims=True)
        dest_ref[...] = (off + route[cr:cr + 1, :]).astype(jnp.int32) * TOK_ROWS


def _plan(route, counts, *, tb, tm=2048):
    n = route.shape[1]
    tm = min(tm, n)
    dests = pl.pallas_call(
        functools.partial(_plan_kernel, tm=tm, tb=tb),
        grid=(n // tm,),
        in_specs=[pl.BlockSpec((8, tm), lambda i: (0, i)),
                  pl.BlockSpec((ROUTE_W, 1), lambda i: (0, 0))],
        out_specs=[pl.BlockSpec((1, tm), lambda i: (0, i))] * TOP_K,
        out_shape=[jax.ShapeDtypeStruct((1, n), jnp.int32)] * TOP_K,
        compiler_params=_cparams("parallel"),
        name="plan",
    )(route, counts)
    return [d.reshape(n) for d in dests]


def _dispatch_kernel(d1_ref, d2_ref, last_blk_ref, nvalid_ref, xp_ref, xs_ref, zbuf, sem, zsem, *, tm, tb, nb):
    base = pl.program_id(0) * tm
    rows = tb * TOK_ROWS

    @pl.when(pl.program_id(0) == 0)
    def _():
        zbuf[...] = jnp.zeros_like(zbuf)

        def zero_copy(blk):
            return pltpu.make_async_copy(zbuf, xs_ref.at[pl.ds(pl.multiple_of(blk * rows, rows), rows), :], zsem)

        def for_each_block(fn):
            def per_expert(e, carry):
                @pl.when(last_blk_ref[e] >= 0)
                def _():
                    fn(zero_copy(last_blk_ref[e]))
                return carry

            def per_tail(blk, carry):
                fn(zero_copy(blk))
                return carry
            lax.fori_loop(0, N_EXPERTS, per_expert, 0)
            lax.fori_loop(nvalid_ref[0], nb, per_tail, 0)

        for_each_block(lambda cp: cp.start())
        for_each_block(lambda cp: cp.wait())

    def body(j, carry):
        for u in range(ROW_UNROLL):
            r = j * ROW_UNROLL + u
            for k, d_ref in enumerate((d1_ref, d2_ref)):
                dst = pl.multiple_of(d_ref[base + r], TOK_ROWS)
                pltpu.make_async_copy(xp_ref.at[pl.ds(r * TOK_ROWS, TOK_ROWS), :],
                                      xs_ref.at[pl.ds(dst, TOK_ROWS), :], sem).start(priority=k)
        return carry
    lax.fori_loop(0, tm // ROW_UNROLL, body, 0)
    for _ in range(TOP_K):
        pltpu.make_async_copy(xp_ref, xs_ref.at[pl.ds(0, tm * TOK_ROWS), :], sem).wait()


def _dispatch(dest1, dest2, last_blk, nvalid, x1p, *, nb, tb=EXPERT_BLOCK, tm=2048):
    n = x1p.shape[0] // TOK_ROWS
    tm = min(tm, n)
    grid_spec = pltpu.PrefetchScalarGridSpec(
        num_scalar_prefetch=4,
        grid=(n // tm,),
        in_specs=[pl.BlockSpec((tm * TOK_ROWS, LANES), lambda i, a, b, lb, nv: (i, 0))],
        out_specs=pl.BlockSpec(memory_space=pl.ANY),
        scratch_shapes=[pltpu.VMEM((tb * TOK_ROWS, LANES), jnp.uint32),
                        pltpu.SemaphoreType.DMA(()),
                        pltpu.SemaphoreType.DMA(())],
    )
    return pl.pallas_call(
        functools.partial(_dispatch_kernel, tm=tm, tb=tb, nb=nb),
        grid_spec=grid_spec,
        out_shape=jax.ShapeDtypeStruct((nb * tb * TOK_ROWS, LANES), jnp.uint32),
        compiler_params=_cparams("arbitrary"),
        name="dispatch",
    )(dest1, dest2, last_blk, nvalid, x1p)


def _expert_kernel(blk_e_ref, first_ref, next_e_ref, nvalid_ref, xs_ref, wg_hbm, wu_hbm, wd_hbm, y_ref,
                   wgs, wus, wds, wgb, wub, wdb, sem, *, tb):
    rows = tb * TOK_ROWS

    def weight_copies(e, slot):
        return [pltpu.make_async_copy(src.at[e], dst.at[slot], sem.at[slot, j])
                for j, (src, dst) in enumerate(((wg_hbm, wgs), (wu_hbm, wus), (wd_hbm, wds)))]

    @pl.when((pl.program_id(0) == 0) & (nvalid_ref[0] > 0))
    def _():
        for cp in weight_copies(blk_e_ref[0], first_ref[0] - 1):
            cp.start()

    def run(i, sub, nblk):
        @pl.when(first_ref[i] > 0)
        def _():
            slot = first_ref[i] - 1

            @pl.when(next_e_ref[i] >= 0)
            def _():
                for cp in weight_copies(next_e_ref[i], 1 - slot):
                    cp.start()

            for cp in weight_copies(blk_e_ref[i], slot):
                cp.wait()
            wgb[...] = wgs[slot].astype(BF16)
            wub[...] = wus[slot].astype(BF16)
            wdb[...] = wds[slot].astype(BF16)
            mlp(sub, nblk)

        @pl.when(first_ref[i] == 0)
        def _():
            mlp(sub, nblk)

    def mlp(sub, nblk):
        m = nblk * tb
        span = pl.ds(sub * rows, nblk * rows)
        lo, hi = _unpack_bf16_pairs(_load_token_rows(xs_ref.at[span, :], m))
        xb = jnp.concatenate([lo.astype(BF16), hi.astype(BF16)], axis=1)
        hg = jnp.dot(xb, wgb[...], preferred_element_type=F32)
        hu = jnp.dot(xb, wub[...], preferred_element_type=F32)
        hid = (_silu(hg) * hu).astype(BF16)
        _store_token_rows(y_ref.at[span, :], _pack_bf16_pairs(jnp.dot(hid, wdb[...], preferred_element_type=F32)))

    nvalid = nvalid_ref[0]
    for sub in range(0, BLOCKS_PER_STEP, 2):
        i0 = pl.program_id(0) * BLOCKS_PER_STEP + sub
        i1 = i0 + 1
        paired = (i1 < nvalid) & (first_ref[i1] == 0)

        @pl.when(paired)
        def _():
            run(i0, sub, 2)

        @pl.when(jnp.logical_not(paired))
        def _():
            @pl.when(i0 < nvalid)
            def _():
                run(i0, sub, 1)

            @pl.when(i1 < nvalid)
            def _():
                run(i1, sub + 1, 1)

        for s_, i in ((sub, i0), (sub + 1, i1)):
            @pl.when(i >= nvalid)
            def _():
                y_ref[pl.ds(s_ * rows, rows), :] = jnp.zeros((rows, LANES), jnp.uint32)


def _experts(blk_e, first, next_e, nvalid, xs, w_gate, w_up, w_down, *, tb=EXPERT_BLOCK):
    nb = blk_e.shape[0]
    assert nb % BLOCKS_PER_STEP == 0
    step_rows = BLOCKS_PER_STEP * tb * TOK_ROWS
    last_step = lambda nv: jnp.maximum((nv[0] - 1) // BLOCKS_PER_STEP, 0)
    grid_spec = pltpu.PrefetchScalarGridSpec(
        num_scalar_prefetch=4,
        grid=(nb // BLOCKS_PER_STEP,),
        in_specs=[pl.BlockSpec((step_rows, LANES), lambda s, be, fr, ne, nv: (jnp.minimum(s, last_step(nv)), 0)),
                  pl.BlockSpec(memory_space=pl.ANY),
                  pl.BlockSpec(memory_space=pl.ANY),
                  pl.BlockSpec(memory_space=pl.ANY)],
        out_specs=pl.BlockSpec((step_rows, LANES), lambda s, be, fr, ne, nv: (s, 0)),
        scratch_shapes=[pltpu.VMEM((2, D_MODEL, D_EXPERT), F32),
                        pltpu.VMEM((2, D_MODEL, D_EXPERT), F32),
                        pltpu.VMEM((2, D_EXPERT, D_MODEL), F32),
                        pltpu.VMEM((D_MODEL, D_EXPERT), BF16),
                        pltpu.VMEM((D_MODEL, D_EXPERT), BF16),
                        pltpu.VMEM((D_EXPERT, D_MODEL), BF16),
                        pltpu.SemaphoreType.DMA((2, 3))],
    )
    return pl.pallas_call(
        functools.partial(_expert_kernel, tb=tb),
        grid_spec=grid_spec,
        out_shape=jax.ShapeDtypeStruct((nb * tb * TOK_ROWS, LANES), jnp.uint32),
        compiler_params=_cparams("arbitrary"),
        name="experts",
    )(blk_e, first, next_e, nvalid, xs, w_gate, w_up, w_down)


def _combine_kernel(d1_ref, d2_ref, y_hbm, x1_ref, route_ref, g_ref, b_ref, o_ref, ybuf, sem, *, tt):
    i = pl.program_id(0)
    nsteps = pl.num_programs(0)
    slot = i % 2

    def issue(tile, slot_):
        def body(j, carry):
            for u in range(ROW_UNROLL):
                r = j * ROW_UNROLL + u
                for k, d_ref in enumerate((d1_ref, d2_ref)):
                    src = pl.multiple_of(d_ref[tile * tt + r], TOK_ROWS)
                    pltpu.make_async_copy(y_hbm.at[pl.ds(src, TOK_ROWS), :],
                                          ybuf.at[slot_, k, pl.ds(r * TOK_ROWS, TOK_ROWS), :],
                                          sem.at[slot_]).start(priority=k)
            return carry
        lax.fori_loop(0, tt // ROW_UNROLL, body, 0)

    @pl.when(i == 0)
    def _():
        issue(0, 0)

    @pl.when(i + 1 < nsteps)
    def _():
        issue(i + 1, 1 - slot)

    for k in range(TOP_K):
        pltpu.make_async_copy(y_hbm.at[pl.ds(0, tt * TOK_ROWS), :], ybuf.at[slot, k], sem.at[slot]).wait()
    route = route_ref[...]
    g1 = route[:, R_G1:R_G1 + 1]
    g2 = route[:, R_G2:R_G2 + 1]
    lo1, hi1 = _unpack_bf16_pairs(_load_token_rows(ybuf.at[slot, 0], tt))
    lo2, hi2 = _unpack_bf16_pairs(_load_token_rows(ybuf.at[slot, 1], tt))
    x_lo, x_hi = _unpack_bf16_pairs(_load_token_rows(x1_ref, tt))
    h_lo = ALPHA_DN * x_lo + (g1 * lo1 + g2 * lo2)
    h_hi = ALPHA_DN * x_hi + (g1 * hi1 + g2 * hi2)
    mu = (jnp.sum(h_lo, axis=-1, keepdims=True) + jnp.sum(h_hi, axis=-1, keepdims=True)) * (1.0 / D_MODEL)
    c_lo = h_lo - mu
    c_hi = h_hi - mu
    var = (jnp.sum(c_lo * c_lo, axis=-1, keepdims=True) + jnp.sum(c_hi * c_hi, axis=-1, keepdims=True)) * (1.0 / D_MODEL)
    inv = lax.rsqrt(var + LN_EPS)
    o_ref[:, 0:HALF] = c_lo * inv * g_ref[:, 0:HALF] + b_ref[:, 0:HALF]
    o_ref[:, HALF:] = c_hi * inv * g_ref[:, HALF:] + b_ref[:, HALF:]


def _combine(dest1, dest2, ybuf, x1p, route, ln_g, ln_b, *, tt=512):
    n = x1p.shape[0] // TOK_ROWS
    tt = min(tt, n)
    grid_spec = pltpu.PrefetchScalarGridSpec(
        num_scalar_prefetch=2,
        grid=(n // tt,),
        in_specs=[pl.BlockSpec(memory_space=pl.ANY),
                  pl.BlockSpec((tt * TOK_ROWS, LANES), lambda i, a, b: (i, 0)),
                  pl.BlockSpec((tt, ROUTE_W), lambda i, a, b: (i, 0)),
                  pl.BlockSpec((1, D_MODEL), lambda i, a, b: (0, 0)),
                  pl.BlockSpec((1, D_MODEL), lambda i, a, b: (0, 0))],
        out_specs=pl.BlockSpec((tt, D_MODEL), lambda i, a, b: (i, 0)),
        scratch_shapes=[pltpu.VMEM((2, TOP_K, tt * TOK_ROWS, LANES), jnp.uint32),
                        pltpu.SemaphoreType.DMA((2,))],
    )
    return pl.pallas_call(
        functools.partial(_combine_kernel, tt=tt),
        grid_spec=grid_spec,
        out_shape=jax.ShapeDtypeStruct((n, D_MODEL), F32),
        compiler_params=_cparams("arbitrary"),
        name="combine",
    )(dest1, dest2, ybuf, x1p, route, ln_g, ln_b)


def _block_plan(counts_row, n, tb):
    counts = counts_row[:N_EXPERTS, 0].astype(jnp.int32)
    pends = jnp.cumsum((counts + tb - 1) // tb * tb)
    nb = (n * TOP_K) // tb + N_EXPERTS
    nvalid = (pends[-1] // tb).astype(jnp.int32)
    blk = jnp.arange(nb, dtype=jnp.int32)
    blk_e = jnp.sum((pends[None, :] <= (blk * tb)[:, None]).astype(jnp.int32), axis=1)
    blk_e = jnp.minimum(blk_e, N_EXPERTS - 1)
    blk_e = jnp.where(blk < nvalid, blk_e, blk_e[jnp.maximum(nvalid - 1, 0)])
    first = jnp.concatenate([jnp.ones((1,), jnp.int32), (blk_e[1:] != blk_e[:-1]).astype(jnp.int32)])
    slot_e = (jnp.cumsum((counts > 0).astype(jnp.int32)) - 1) % 2
    first = first * (1 + slot_e[blk_e])
    ids = jnp.arange(N_EXPERTS, dtype=jnp.int32)
    later = lax.cummin(jnp.where(counts > 0, ids, N_EXPERTS), axis=0, reverse=True)
    nxt = jnp.concatenate([later[1:], jnp.full((1,), N_EXPERTS, jnp.int32)])
    nxt = jnp.where(nxt < N_EXPERTS, nxt, -1)
    last_blk = jnp.where(counts > 0, pends // tb - 1, -1).astype(jnp.int32)
    return blk_e, first, nxt[blk_e], nvalid.reshape(1), last_blk


def kernel(x, w_in, w_gk_up, b_gk, conv_w, a_log, dt_bias, gla_norm_g, gdn_norm_g, w_out, ln1_g, ln1_b, w_router_group, b_router_group, w_router_expert, b_router_expert, w_gate, w_up, w_down, ln2_g, ln2_b):
    bsz, seq, d = x.shape
    n = bsz * seq
    x2d = x.reshape(n, d)
    w_main, w_aux = _split_w_in(w_in[0])
    main, aux = _in_proj(x2d, w_main, w_aux, conv_w[0].astype(F32), seq=seq)
    wgk = jnp.zeros((AUX_W, 256), F32).at[:GLA_GATE_RANK].set(w_gk_up[0])
    o_a = _gla(main, aux, wgk, b_gk[0][None], gla_norm_g[0][None], bsz=bsz, seq=seq)
    o_b = _gdn(main, aux, _pad_aux_row(a_log[0], AUX_AB), _pad_aux_row(dt_bias[0], AUX_AB),
               gdn_norm_g[0][None], bsz=bsz, seq=seq)
    rpad = jnp.zeros((ROUTE_W - N_EXPERTS - N_GROUPS, d), F32)
    w_router = jnp.concatenate([w_router_expert[0].T, w_router_group[0].T, rpad], axis=0)
    w_router_hi = w_router.astype(BF16)
    w_router = jnp.concatenate([w_router_hi, (w_router - w_router_hi.astype(F32)).astype(BF16)], axis=0)
    b_router = jnp.concatenate([b_router_expert[0], b_router_group[0], rpad[:, 0]])[:, None]
    x1p, route, gates, counts = _post_mix(o_a, o_b, x2d, w_out[0].astype(BF16), ln1_g[0][None], ln1_b[0][None],
                                          w_router, b_router)
    dest = _plan(route, counts, tb=EXPERT_BLOCK)
    blk_e, first, next_e, nvalid, last_blk = _block_plan(counts, n, EXPERT_BLOCK)
    xs = _dispatch(dest[0], dest[1], last_blk, nvalid, x1p, nb=blk_e.shape[0])
    ybuf = _experts(blk_e, first, next_e, nvalid, xs, w_gate[0], w_up[0], w_down[0])
    out = _combine(dest[0], dest[1], ybuf, x1p, gates, ln2_g[0][None], ln2_b[0][None])
    return out.reshape(bsz, seq, d)
```

```python
import functools

import jax
import jax.numpy as jnp
import numpy as np
from jax import lax
from jax.experimental import pallas as pl
from jax.experimental.pallas import tpu as pltpu

F32 = jnp.float32
BF16 = jnp.bfloat16
HIGHEST = lax.Precision.HIGHEST

D_MODEL = 1024
DEPTH = 1
GLA_HEADS = 4
GLA_DK = 64
GLA_DV = 128
GLA_WIDTH = GLA_HEADS * GLA_DV
GLA_GATE_RANK = 16
GLA_GATE_NORM = 16.0
GDN_HEADS = 4
GDN_DK = 128
GDN_DV = 128
GDN_WIDTH = GDN_HEADS * GDN_DV
CONV_W = 4
CHUNK = 64
N_GROUPS = 8
EXPERTS_PER_GROUP = 8
N_EXPERTS = N_GROUPS * EXPERTS_PER_GROUP
TOP_K = 2
D_EXPERT = 512
LN_EPS = 1e-5
RMS_EPS = 1e-6
ALPHA_DN = (2.0 * DEPTH) ** 0.25

MAIN_W = 2 * GLA_HEADS * GLA_DK + 2 * GLA_WIDTH + 4 * GDN_WIDTH
AUX_W = 128
AUX_LRA = 0
AUX_BB = GLA_GATE_RANK
AUX_AB = GLA_GATE_RANK + GDN_HEADS
SUB = 16
VMEM_LIMIT = 56 * 1024 * 1024


def _cparams(*sem):
    return pltpu.CompilerParams(dimension_semantics=sem, vmem_limit_bytes=VMEM_LIMIT)


def _sigmoid(x):
    return 1.0 / (1.0 + jnp.exp(-x))


def _silu(x):
    return x * _sigmoid(x)


def _log_sigmoid(x):
    return jnp.minimum(x, 0.0) - jnp.log(1.0 + jnp.exp(-jnp.abs(x)))


def _softplus(x):
    return jnp.maximum(x, 0.0) + jnp.log(1.0 + jnp.exp(-jnp.abs(x)))


def _chunk_cumsum(x, c):
    pos = lax.broadcasted_iota(jnp.int32, x.shape, 0) % c
    shift = 1
    while shift < c:
        x = x + jnp.where(pos >= shift, pltpu.roll(x, shift, 0), 0.0)
        shift *= 2
    return x


CONV_PAD = 8
GDN_HW = GDN_HEADS * GDN_DK
GDN_QKV_BLOCK0 = (2 * GLA_HEADS * GLA_DK + 2 * GLA_WIDTH) // GDN_HW


def _in_proj_kernel(x_ref, wm_ref, wa_ref, cw_ref, main_ref, aux_ref, stage_ref, *, tm, tiles_per_seq):
    xb = x_ref[...].astype(BF16)

    @pl.when((pl.program_id(0) % tiles_per_seq) == 0)
    def _():
        stage_ref[:, 0:CONV_PAD, :] = jnp.zeros((3 * GDN_HEADS, CONV_PAD, GDN_DK), F32)

    for j in range(MAIN_W // GDN_HW):
        sl = slice(j * GDN_HW, (j + 1) * GDN_HW)
        r = jnp.dot(xb, wm_ref[:, sl], preferred_element_type=F32)
        b = j - GDN_QKV_BLOCK0
        if not 0 <= b < 3:
            main_ref[:, sl] = r.astype(BF16)
            continue
        for h in range(GDN_HEADS):
            st = stage_ref.at[b * GDN_HEADS + h]
            cols = slice(b * GDN_HW + h * GDN_DK, b * GDN_HW + (h + 1) * GDN_DK)
            st[CONV_PAD:, :] = r[:, h * GDN_DK:(h + 1) * GDN_DK]
            acc = jnp.zeros((tm, GDN_DK), F32)
            for i in range(CONV_W):
                lo = CONV_PAD - (CONV_W - 1) + i
                acc = acc + st[lo:lo + tm, :] * cw_ref[i:i + 1, cols]
            st[0:CONV_PAD, :] = st[tm:tm + CONV_PAD, :]
            yh = _silu(acc)
            if b < 2:
                yh = yh * lax.rsqrt(jnp.sum(yh * yh, axis=-1, keepdims=True) + RMS_EPS)
            if b == 0:
                yh = yh * (GDN_DK ** -0.5)
            main_ref[:, j * GDN_HW + h * GDN_DK:j * GDN_HW + (h + 1) * GDN_DK] = yh.astype(BF16)
    aux_ref[...] = jnp.dot(xb, wa_ref[...], preferred_element_type=F32)


def _in_proj(x2d, w_main, w_aux, conv_w, *, seq, tm=512):
    n = x2d.shape[0]
    return pl.pallas_call(
        functools.partial(_in_proj_kernel, tm=tm, tiles_per_seq=seq // tm),
        grid=(n // tm,),
        in_specs=[pl.BlockSpec((tm, D_MODEL), lambda i: (i, 0)),
                  pl.BlockSpec((D_MODEL, MAIN_W), lambda i: (0, 0)),
                  pl.BlockSpec((D_MODEL, AUX_W), lambda i: (0, 0)),
                  pl.BlockSpec((CONV_W, 3 * GDN_HW), lambda i: (0, 0))],
        out_specs=[pl.BlockSpec((tm, MAIN_W), lambda i: (i, 0)),
                   pl.BlockSpec((tm, AUX_W), lambda i: (i, 0))],
        out_shape=[jax.ShapeDtypeStruct((n, MAIN_W), BF16),
                   jax.ShapeDtypeStruct((n, AUX_W), F32)],
        scratch_shapes=[pltpu.VMEM((3 * GDN_HEADS, tm + CONV_PAD, GDN_DK), F32)],
        compiler_params=_cparams("arbitrary"),
        name="in_proj",
    )(x2d, w_main, w_aux, conv_w)


def _split_w_in(w_in):
    sizes = (GLA_HEADS * GLA_DK, GLA_HEADS * GLA_DK, GLA_WIDTH, GLA_WIDTH, GLA_GATE_RANK,
             GDN_HEADS * GDN_DK, GDN_HEADS * GDN_DK, GDN_WIDTH, GDN_WIDTH, GDN_HEADS, GDN_HEADS)
    offs = np.cumsum((0,) + sizes)
    seg = [w_in[:, offs[i]:offs[i + 1]] for i in range(len(sizes))]
    qa, ka, va, ra, lra, qb, kb, vb, zb, bb, ab = seg
    w_main = jnp.concatenate([qa, ka, va, ra, qb, kb, vb, zb], axis=1).astype(BF16)
    pad = jnp.zeros((w_in.shape[0], AUX_W - GLA_GATE_RANK - 2 * GDN_HEADS), w_in.dtype)
    w_aux = jnp.concatenate([lra, bb, ab, pad], axis=1).astype(BF16)
    return w_main, w_aux


_NT = (((1,), (1,)), ((), ()))
_TN = (((0,), (0,)), ((), ()))


def _gla_kernel(q_ref, k_ref, v_ref, r_ref, aux_ref, wgk_ref, bgk_ref, g_ref, o_ref, st_ref, *, ts, nseq):
    npairs = 2 * nseq
    c = CHUNK
    nsub = c // SUB

    @pl.when(pl.program_id(1) == 0)
    def _():
        st_ref[...] = jnp.zeros_like(st_ref)

    gk_all = [_log_sigmoid(jnp.dot(aux_ref[sq], wgk_ref[...], precision=HIGHEST, preferred_element_type=F32)
                           + bgk_ref[...]) * (1.0 / GLA_GATE_NORM) for sq in range(nseq)]

    row128 = lax.broadcasted_iota(jnp.int32, (c, 128), 0)
    lane128 = lax.broadcasted_iota(jnp.int32, (1, 128), 1)
    lane_masks = [(lane128 // GLA_DK) == hh for hh in range(2)]
    sub_valid = [row128 < SUB * (i + 1) for i in range(nsub)]
    ar = lax.broadcasted_iota(jnp.int32, (c, nsub * c), 0)
    ac = lax.broadcasted_iota(jnp.int32, (c, nsub * c), 1)
    amask = ((ac // c) == (ar // SUB)) & ((ac % c) <= ar)
    scale = GLA_DK ** -0.5
    gnorm = g_ref[...]

    nch = ts // c
    rows = [slice(ch * c, (ch + 1) * c) for ch in range(nch)]
    b_cum = [_chunk_cumsum(g, c) for g in gk_all]

    cp = [(ch, p) for ch in range(nch) for p in range(npairs)]
    qt, qd, kd, kstack, dlast = {}, {}, {}, {}, {}
    for ch, p in cp:
        sq, pp = divmod(p, 2)
        lanes = slice(128 * pp, 128 * (pp + 1))
        bp = b_cum[sq][rows[ch], lanes]
        qp = q_ref[sq, rows[ch], lanes].astype(F32) * scale
        kp = k_ref[sq, rows[ch], lanes].astype(F32)
        c_row = jnp.concatenate(
            [jnp.broadcast_to(bp[SUB * i:SUB * i + 1, :], (SUB, 128)) for i in range(nsub)], axis=0)
        qt[ch, p] = (qp * jnp.exp(bp - c_row)).astype(BF16)
        qd[ch, p] = qp * jnp.exp(bp)
        b_last = bp[c - 1:c, :]
        dlast[ch, p] = jnp.exp(b_last)
        kd[ch, p] = kp * jnp.exp(b_last - bp)
        kts = []
        for i in range(nsub):
            e = jnp.exp(jnp.where(sub_valid[i], bp[SUB * i:SUB * i + 1, :] - bp, 0.0))
            kts.append(jnp.where(sub_valid[i], kp * e, 0.0))
        kstack[ch, p] = jnp.concatenate(kts, axis=0)

    cph = [(ch, p, hh) for ch, p in cp for hh in range(2)]
    vh = {(ch, p, hh): v_ref[p // 2, rows[ch], 128 * (2 * (p % 2) + hh):128 * (2 * (p % 2) + hh + 1)]
          for ch, p, hh in cph}
    r_mat = {(ch, p, hh): lax.dot_general(qt[ch, p], jnp.where(lane_masks[hh], kstack[ch, p], 0.0).astype(BF16),
                                          _NT, preferred_element_type=F32) for ch, p, hh in cph}
    o_intra = {k: jnp.dot(jnp.where(amask, r_mat[k], 0.0).astype(BF16), jnp.concatenate([vh[k]] * nsub, axis=0),
                          preferred_element_type=F32) for k in cph}
    upd = {(ch, p): lax.dot_general(
        jnp.concatenate([vh[ch, p, hh] for hh in range(2)], axis=0),
        jnp.concatenate([jnp.where(lane_masks[hh], kd[ch, p], 0.0).astype(BF16) for hh in range(2)], axis=0),
        _TN, preferred_element_type=F32) for ch, p in cp}

    st = [st_ref[p] for p in range(npairs)]
    for ch in range(nch):
        st_b = [s.astype(BF16) for s in st]
        o_pair = [lax.dot_general(
            jnp.concatenate([jnp.where(lane_masks[hh], qd[ch, p], 0.0).astype(BF16) for hh in range(2)], axis=0),
            st_b[p], _NT, preferred_element_type=F32) for p in range(npairs)]
        o_inter = {(p, hh): o_pair[p][hh * c:(hh + 1) * c] for p in range(npairs) for hh in range(2)}
        st = [st[p] * dlast[ch, p] + upd[ch, p] for p in range(npairs)]
        for p in range(npairs):
            for hh in range(2):
                sq, h = p // 2, 2 * (p % 2) + hh
                o = o_intra[ch, p, hh] + o_inter[p, hh]
                o = o * lax.rsqrt(jnp.mean(o * o, axis=-1, keepdims=True) + RMS_EPS) * gnorm
                gate = _silu(r_ref[sq, rows[ch], 128 * h:128 * (h + 1)].astype(F32))
                o_ref[sq, rows[ch], 128 * h:128 * (h + 1)] = (o * gate).astype(BF16)
    for p in range(npairs):
        st_ref[p] = st[p]


def _gla(main, aux, wgk_pad, b_gk, gla_norm_g, *, bsz, seq, ts=512, nseq=2):
    nt = seq // ts
    assert bsz % nseq == 0
    main4 = main.reshape(bsz // nseq, nseq, seq, MAIN_W)
    aux4 = aux.reshape(bsz // nseq, nseq, seq, AUX_W)
    tok = lambda width, col: pl.BlockSpec((None, nseq, ts, width), lambda b, t: (b, 0, t, col))
    out = pl.pallas_call(
        functools.partial(_gla_kernel, ts=ts, nseq=nseq),
        grid=(bsz // nseq, nt),
        in_specs=[tok(256, 0), tok(256, 1), tok(512, 1), tok(512, 2),
                  tok(AUX_W, 0),
                  pl.BlockSpec((AUX_W, 256), lambda b, t: (0, 0)),
                  pl.BlockSpec((1, 256), lambda b, t: (0, 0)),
                  pl.BlockSpec((1, 128), lambda b, t: (0, 0))],
        out_specs=tok(GLA_WIDTH, 0),
        out_shape=jax.ShapeDtypeStruct((bsz // nseq, nseq, seq, GLA_WIDTH), BF16),
        scratch_shapes=[pltpu.VMEM((2 * nseq, 128, 128), F32)],
        compiler_params=_cparams("parallel", "arbitrary"),
        name="gla",
    )(main4, main4, main4, main4, aux4, wgk_pad, b_gk, gla_norm_g)
    return out.reshape(bsz * seq, GLA_WIDTH)


def _gdn_kernel(q_ref, k_ref, v_ref, z_ref, aux_ref, alog_ref, dtb_ref, g_ref, o_ref, st_ref, *, ts, nseq):
    c = CHUNK
    nch = ts // c

    @pl.when(pl.program_id(1) == 0)
    def _():
        st_ref[...] = jnp.zeros_like(st_ref)

    beta_full, gam_full, gam_t = [], [], []
    for sq in range(nseq):
        aux = aux_ref[sq]
        beta_full.append(_sigmoid(aux))
        g_full = -jnp.exp(alog_ref[...]) * _softplus(aux + dtb_ref[...])
        gam_full.append(_chunk_cumsum(g_full, c))
        gam_t.append(gam_full[-1].T)
    gnorm = g_ref[...]

    row = lax.broadcasted_iota(jnp.int32, (c, 2 * c), 0)
    lane = lax.broadcasted_iota(jnp.int32, (c, 2 * c), 1)
    left = lane < c
    col = lane % c
    causal = col <= row
    strict = col < row
    same32 = (row // 32) == (col // 32)
    same16 = (row // 16) == (col // 16)
    lvl0 = same16 & strict
    lvl1 = same32 & jnp.logical_not(same16) & strict
    lvl2 = jnp.logical_not(same32) & strict

    def pack_rows(x2):
        return jnp.where(left, x2[0:c], x2[c:2 * c])

    def pack_cols(x1):
        return jnp.where(left, x1[0:c], x1[c:2 * c])

    def halves(x):
        return jnp.where(left, x, 0.0).astype(BF16), jnp.where(left, 0.0, x).astype(BF16)

    def mmp(x, y):
        return jnp.dot(x.astype(BF16), jnp.concatenate(halves(y), axis=0), preferred_element_type=F32)

    heads = range(nseq * GDN_HEADS)
    npair = nch // 2
    items = [(h, pr) for h in heads for pr in range(npair)]
    prow = [slice(2 * c * pr, 2 * c * (pr + 1)) for pr in range(npair)]

    qn, kn, beta, gam, gam_r, kb, rhs, rhs_b, qd, kd = ([None] * len(heads) for _ in range(10))
    for h in heads:
        sq, hh = divmod(h, GDN_HEADS)
        hl = slice(128 * hh, 128 * (hh + 1))
        kb[h] = k_ref[sq, :, hl]
        qn[h] = q_ref[sq, :, hl].astype(F32)
        kn[h] = kb[h].astype(F32)
        vh = v_ref[sq, :, hl].astype(F32)
        beta[h] = beta_full[sq][:, AUX_BB + hh:AUX_BB + hh + 1]
        gam[h] = gam_full[sq][:, AUX_AB + hh:AUX_AB + hh + 1]
        gam_r[h] = gam_t[sq][AUX_AB + hh:AUX_AB + hh + 1, :]
        egam = jnp.exp(gam[h])
        rhs[h] = jnp.concatenate([beta[h] * vh, beta[h] * egam * kn[h]], axis=1)
        rhs_b[h] = rhs[h].astype(BF16)
        qd[h] = (qn[h] * egam).astype(BF16)
        gl_rows = jnp.concatenate(
            [jnp.broadcast_to(gam[h][c * (j + 1) - 1:c * (j + 1), :], (c, 1)) for j in range(nch)], axis=0)
        kd[h] = (kn[h] * jnp.exp(gl_rows - gam[h])).astype(BF16)

    gram = [lax.dot_general(jnp.concatenate([kb[h][prow[pr]], qn[h][prow[pr]].astype(BF16)], axis=0), kb[h][prow[pr]],
                            _NT, preferred_element_type=F32) for h, pr in items]
    kk = [pack_rows(g[0:2 * c]) for g in gram]
    qk = [pack_rows(g[2 * c:4 * c]) for g in gram]
    dec = [jnp.exp(jnp.where(causal, pack_cols(gam[h][prow[pr]]) - gam_r[h][:, prow[pr]], 0.0)) for h, pr in items]
    a = [pack_cols(beta[h][prow[pr]]) * kk[i] * dec[i] for i, (h, pr) in enumerate(items)]
    attn = [halves(jnp.where(causal, qk[i] * dec[i], 0.0)) for i in range(len(items))]

    x1 = [jnp.where(lvl0, -ai, 0.0) for ai in a]
    x2 = [mmp(x, x) for x in x1]
    y = [mmp(x, xx) for x, xx in zip(x1, x2)]
    p = [x + xx + yy for x, xx, yy in zip(x1, x2, y)]
    xk = x2
    for _ in range(2):
        xk = [mmp(x, x) for x in xk]
        y = [mmp(pp, x) for pp, x in zip(p, xk)]
        p = [pp + x + yy for pp, x, yy in zip(p, xk, y)]
    for lvl in (lvl1, lvl2):
        low = [jnp.where(lvl, ai, 0.0) for ai in a]
        y = [mmp(pp, lo) for pp, lo in zip(p, low)]
        zz = [lo + yy for lo, yy in zip(low, y)]
        y = [mmp(z, pp) for z, pp in zip(zz, p)]
        p = [pp - z - yy for pp, z, yy in zip(p, zz, y)]

    us = [[None] * nch for _ in heads]
    ws = [[None] * nch for _ in heads]
    for i, (h, pr) in enumerate(items):
        for half, p_half in enumerate(halves(p[i])):
            j = 2 * pr + half
            uw = rhs[h][c * j:c * (j + 1)] + jnp.dot(p_half, rhs_b[h][prow[pr]], preferred_element_type=F32)
            us[h][j] = uw[:, :GDN_DV]
            ws[h][j] = uw[:, GDN_DV:].astype(BF16)

    s = [st_ref[h] for h in heads]
    vprev = [None] * len(heads)
    for j in range(nch):
        rows = slice(c * j, c * (j + 1))
        sb = [s[h].astype(BF16) for h in heads]
        ws_qd = [jnp.dot(jnp.concatenate([ws[h][j], qd[h][rows]], axis=0), sb[h], preferred_element_type=F32)
                 for h in heads]
        vn16 = [(us[h][j] - ws_qd[h][0:c]).astype(BF16) for h in heads]
        o_inter = [ws_qd[h][c:2 * c] for h in heads]
        for h in heads:
            vpair = jnp.concatenate([vn16[h], jnp.zeros_like(vn16[h])] if j % 2 == 0 else [vprev[h], vn16[h]], axis=0)
            o = o_inter[h] + jnp.dot(attn[h * npair + j // 2][j % 2], vpair, preferred_element_type=F32)
            s[h] = jnp.exp(gam[h][c * (j + 1) - 1:c * (j + 1), :]) * s[h] + lax.dot_general(
                kd[h][rows], vn16[h], _TN, preferred_element_type=F32)
            o = o * lax.rsqrt(jnp.mean(o * o, axis=-1, keepdims=True) + RMS_EPS) * gnorm
            sq, hh = divmod(h, GDN_HEADS)
            gate = _silu(z_ref[sq, rows, 128 * hh:128 * (hh + 1)].astype(F32))
            o_ref[sq, rows, 128 * hh:128 * (hh + 1)] = (o * gate).astype(BF16)
        vprev = vn16
    for h in heads:
        st_ref[h] = s[h]


def _gdn(main, aux, alog_pad, dtb_pad, gdn_norm_g, *, bsz, seq, ts=512, nseq=2):
    nt = seq // ts
    hw = GDN_HEADS * GDN_DK
    assert bsz % nseq == 0
    main4 = main.reshape(bsz // nseq, nseq, seq, MAIN_W)
    aux4 = aux.reshape(bsz // nseq, nseq, seq, AUX_W)
    tok = lambda col: pl.BlockSpec((None, nseq, ts, hw), lambda b, t: (b, 0, t, col))
    out = pl.pallas_call(
        functools.partial(_gdn_kernel, ts=ts, nseq=nseq),
        grid=(bsz // nseq, nt),
        in_specs=[tok(3), tok(4), tok(5), tok(6),
                  pl.BlockSpec((None, nseq, ts, AUX_W), lambda b, t: (b, 0, t, 0)),
                  pl.BlockSpec((1, AUX_W), lambda b, t: (0, 0)),
                  pl.BlockSpec((1, AUX_W), lambda b, t: (0, 0)),
                  pl.BlockSpec((1, GDN_DV), lambda b, t: (0, 0))],
        out_specs=pl.BlockSpec((None, nseq, ts, GDN_WIDTH), lambda b, t: (b, 0, t, 0)),
        out_shape=jax.ShapeDtypeStruct((bsz // nseq, nseq, seq, GDN_WIDTH), BF16),
        scratch_shapes=[pltpu.VMEM((nseq * GDN_HEADS, GDN_DK, GDN_DV), F32)],
        compiler_params=_cparams("parallel", "arbitrary"),
        name="gdn",
    )(main4, main4, main4, main4, aux4, alog_pad, dtb_pad, gdn_norm_g)
    return out.reshape(bsz * seq, GDN_WIDTH)


def _pad_aux_row(v, offset):
    return jnp.zeros((1, AUX_W), F32).at[0, offset:offset + v.shape[0]].set(v.astype(F32))


ROUTE_W = 128
R_E1, R_E2, R_RANK1, R_RANK2, R_G1, R_G2 = range(6)
ROUTER_GROUP_COL = N_EXPERTS
POST_MIX_SPLIT = 8


def _layer_norm(h, g, b):
    mu = jnp.mean(h, axis=-1, keepdims=True)
    hc = h - mu
    var = jnp.mean(hc * hc, axis=-1, keepdims=True)
    return hc * lax.rsqrt(var + LN_EPS) * g + b


def _pack_bf16_pairs(x):
    w = x.shape[1] // 2
    lo = lax.bitcast_convert_type(x[:, :w].astype(BF16).astype(F32), jnp.uint32)
    hi = lax.bitcast_convert_type(x[:, w:].astype(BF16).astype(F32), jnp.uint32)
    return (lo >> 16) | (hi & jnp.uint32(0xFFFF0000))


def _unpack_bf16_pairs(p):
    lo = lax.bitcast_convert_type(p << 16, F32)
    hi = lax.bitcast_convert_type(p & jnp.uint32(0xFFFF0000), F32)
    return lo, hi


HALF = D_MODEL // 2
LANES = 128
TOK_ROWS = HALF // LANES


def _store_token_rows(ref, packed):
    m = packed.shape[0]
    for c in range(TOK_ROWS):
        ref[pl.ds(c, m, stride=TOK_ROWS), :] = packed[:, LANES * c:LANES * (c + 1)]


def _load_token_rows(ref, m):
    return jnp.concatenate([ref[pl.ds(c, m, stride=TOK_ROWS), :] for c in range(TOK_ROWS)], axis=1)


def _post_mix_kernel(oa_ref, ob_ref, x_ref, wo_ref, g_ref, b_ref, wr_ref, br_ref,
                     x1p_ref, route_ref, gates_ref, cnt_ref, carry_ref, *, tm):
    @pl.when(pl.program_id(0) == 0)
    def _():
        carry_ref[...] = jnp.zeros_like(carry_ref)

    m = tm // POST_MIX_SPLIT
    groups = range(POST_MIX_SPLIT)
    rs = [slice(h * m, (h + 1) * m) for h in groups]
    y = [jnp.dot(oa_ref[rs[h], :], wo_ref[0:GLA_WIDTH, :], preferred_element_type=F32)
         + jnp.dot(ob_ref[rs[h], :], wo_ref[GLA_WIDTH:, :], preferred_element_type=F32) for h in groups]
    x1 = [_layer_norm(ALPHA_DN * x_ref[rs[h], :] + y[h], g_ref[...], b_ref[...]) for h in groups]
    for h in groups:
        _store_token_rows(x1p_ref.at[pl.ds(h * m * TOK_ROWS, m * TOK_ROWS), :], _pack_bf16_pairs(x1[h]))

    xh = [x.astype(BF16) for x in x1]
    xl = [(x - xb.astype(F32)).astype(BF16) for x, xb in zip(x1, xh)]
    hw = [lax.dot_general(wr_ref[...], xb, _NT, preferred_element_type=F32) for xb in xh]
    lw = [lax.dot_general(wr_ref[0:ROUTE_W, :], xb, _NT, preferred_element_type=F32) for xb in xl]
    logits = [a[:ROUTE_W] + a[ROUTE_W:] + b + br_ref[...] for a, b in zip(hw, lw)]
    sub = lax.broadcasted_iota(jnp.int32, (ROUTE_W, m), 0)
    big = jnp.int32(1 << 20)
    neg = jnp.float32(-jnp.inf)

    def first_argmax(vals):
        mx = jnp.max(vals, axis=0, keepdims=True)
        idx = jnp.min(jnp.where(vals == mx, sub, big), axis=0, keepdims=True)
        return mx, idx

    is_group = (sub >= ROUTER_GROUP_COL) & (sub < ROUTER_GROUP_COL + N_GROUPS)
    gl = [jnp.where(is_group, lg, neg) for lg in logits]
    gtop = [first_argmax(v) for v in gl]
    p_group = [1.0 / jnp.sum(jnp.exp(v - t[0]), axis=0, keepdims=True) for v, t in zip(gl, gtop)]
    el = [jnp.where((sub // EXPERTS_PER_GROUP) == (t[1] - ROUTER_GROUP_COL), lg, neg) for lg, t in zip(logits, gtop)]
    top1 = [first_argmax(v) for v in el]
    top2 = [first_argmax(jnp.where(sub == t[1], neg, v)) for v, t in zip(el, top1)]
    ex = [jnp.exp(b[0] - a[0]) for a, b in zip(top1, top2)]
    w1 = [1.0 / (1.0 + e) for e in ex]
    gate1 = [p * w for p, w in zip(p_group, w1)]
    gate2 = [p * (e * w) for p, e, w in zip(p_group, ex, w1)]

    oh1 = [sub == t[1] for t in top1]
    oh2 = [sub == t[1] for t in top2]
    ri = lax.broadcasted_iota(jnp.int32, (m, m), 0)
    ci = lax.broadcasted_iota(jnp.int32, (m, m), 1)
    ustrict = (ri < ci).astype(BF16)
    oh1f = [o.astype(F32) for o in oh1]
    oh2f = [o.astype(F32) for o in oh2]
    c1 = [jnp.dot(o.astype(BF16), ustrict, preferred_element_type=F32) for o in oh1f]
    c2 = [jnp.dot(o.astype(BF16), ustrict, preferred_element_type=F32) for o in oh2f]
    tot1 = [jnp.sum(o, axis=1, keepdims=True) for o in oh1f]
    tot2 = [jnp.sum(o, axis=1, keepdims=True) for o in oh2f]
    carry = carry_ref[...]
    eye8 = jnp.eye(8, ROUTE_W, dtype=F32)
    zero = jnp.zeros((1, m), F32)
    for h in groups:
        rank1 = jnp.sum(jnp.where(oh1[h], c1[h] + carry, 0.0), axis=0, keepdims=True)
        rank2 = jnp.sum(jnp.where(oh2[h], c2[h] + (carry + tot1[h]), 0.0), axis=0, keepdims=True)
        carry = carry + tot1[h] + tot2[h]
        route_t = jnp.concatenate([top1[h][1].astype(F32), top2[h][1].astype(F32), rank1, rank2,
                                   gate1[h], gate2[h], zero, zero], axis=0)
        route_ref[:, rs[h]] = route_t
        gates_ref[rs[h], :] = lax.dot_general(route_t, eye8, _TN, precision=HIGHEST, preferred_element_type=F32)
    carry_ref[...] = carry
    cnt_ref[...] = carry


def _post_mix(o_a, o_b, x2d, w_out_b, ln_g, ln_b, w_router, b_router, *, tm=2048):
    n = x2d.shape[0]
    tm = min(tm, n)
    return pl.pallas_call(
        functools.partial(_post_mix_kernel, tm=tm),
        grid=(n // tm,),
        in_specs=[pl.BlockSpec((tm, GLA_WIDTH), lambda i: (i, 0)),
                  pl.BlockSpec((tm, GDN_WIDTH), lambda i: (i, 0)),
                  pl.BlockSpec((tm, D_MODEL), lambda i: (i, 0)),
                  pl.BlockSpec((GLA_WIDTH + GDN_WIDTH, D_MODEL), lambda i: (0, 0)),
                  pl.BlockSpec((1, D_MODEL), lambda i: (0, 0)),
                  pl.BlockSpec((1, D_MODEL), lambda i: (0, 0)),
                  pl.BlockSpec((2 * ROUTE_W, D_MODEL), lambda i: (0, 0)),
                  pl.BlockSpec((ROUTE_W, 1), lambda i: (0, 0))],
        out_specs=[pl.BlockSpec((tm * TOK_ROWS, LANES), lambda i: (i, 0)),
                   pl.BlockSpec((8, tm), lambda i: (0, i)),
                   pl.BlockSpec((tm, ROUTE_W), lambda i: (i, 0)),
                   pl.BlockSpec((ROUTE_W, 1), lambda i: (0, 0))],
        out_shape=[jax.ShapeDtypeStruct((n * TOK_ROWS, LANES), jnp.uint32),
                   jax.ShapeDtypeStruct((8, n), F32),
                   jax.ShapeDtypeStruct((n, ROUTE_W), F32),
                   jax.ShapeDtypeStruct((ROUTE_W, 1), F32)],
        scratch_shapes=[pltpu.VMEM((ROUTE_W, 1), F32)],
        compiler_params=_cparams("arbitrary"),
        name="post_mix",
    )(o_a, o_b, x2d, w_out_b, ln_g, ln_b, w_router, b_router)


EXPERT_BLOCK = 256
BLOCKS_PER_STEP = 4

ROW_UNROLL = 8


def _plan_kernel(route_ref, cnt_ref, *dest_refs, tm, tb):
    counts = cnt_ref[...].astype(jnp.int32)
    padded = jnp.broadcast_to(((counts + (tb - 1)) & jnp.int32(-tb)).astype(F32), (ROUTE_W, ROUTE_W))
    pstart = (_chunk_cumsum(padded, ROUTE_W) - padded)[:, 0:1]
    route = route_ref[...]
    sub = lax.broadcasted_iota(jnp.int32, (ROUTE_W, tm), 0).astype(F32)
    for dest_ref, (ce, cr) in zip(dest_refs, ((R_E1, R_RANK1), (R_E2, R_RANK2))):
        off = jnp.sum(jnp.where(sub == route[ce:ce + 1, :], pstart, 0.0), axis=0, keepdims=True)
        dest_ref[...] = (off + route[cr:cr + 1, :]).astype(jnp.int32) * TOK_ROWS


def _plan(route, counts, *, tb, tm=2048):
    n = route.shape[1]
    tm = min(tm, n)
    dests = pl.pallas_call(
        functools.partial(_plan_kernel, tm=tm, tb=tb),
        grid=(n // tm,),
        in_specs=[pl.BlockSpec((8, tm), lambda i: (0, i)),
                  pl.BlockSpec((ROUTE_W, 1), lambda i: (0, 0))],
        out_specs=[pl.BlockSpec((1, tm), lambda i: (0, i))] * TOP_K,
        out_shape=[jax.ShapeDtypeStruct((1, n), jnp.int32)] * TOP_K,
        compiler_params=_cparams("parallel"),
        name="plan",
    )(route, counts)
    return [d.reshape(n) for d in dests]


def _dispatch_kernel(d1_ref, d2_ref, last_blk_ref, nvalid_ref, xp_ref, xs_ref, zbuf, sem, zsem, *, tm, tb, nb):
    base = pl.program_id(0) * tm
    rows = tb * TOK_ROWS

    @pl.when(pl.program_id(0) == 0)
    def _():
        zbuf[...] = jnp.zeros_like(zbuf)

        def zero_copy(blk):
            return pltpu.make_async_copy(zbuf, xs_ref.at[pl.ds(pl.multiple_of(blk * rows, rows), rows), :], zsem)

        def for_each_block(fn):
            def per_expert(e, carry):
                @pl.when(last_blk_ref[e] >= 0)
                def _():
                    fn(zero_copy(last_blk_ref[e]))
                return carry

            def per_tail(blk, carry):
                fn(zero_copy(blk))
                return carry
            lax.fori_loop(0, N_EXPERTS, per_expert, 0)
            lax.fori_loop(nvalid_ref[0], nb, per_tail, 0)

        for_each_block(lambda cp: cp.start())
        for_each_block(lambda cp: cp.wait())

    def body(j, carry):
        for u in range(ROW_UNROLL):
            r = j * ROW_UNROLL + u
            for k, d_ref in enumerate((d1_ref, d2_ref)):
                dst = pl.multiple_of(d_ref[base + r], TOK_ROWS)
                pltpu.make_async_copy(xp_ref.at[pl.ds(r * TOK_ROWS, TOK_ROWS), :],
                                      xs_ref.at[pl.ds(dst, TOK_ROWS), :], sem).start(priority=k)
        return carry
    lax.fori_loop(0, tm // ROW_UNROLL, body, 0)
    for _ in range(TOP_K):
        pltpu.make_async_copy(xp_ref, xs_ref.at[pl.ds(0, tm * TOK_ROWS), :], sem).wait()


def _dispatch(dest1, dest2, last_blk, nvalid, x1p, *, nb, tb=EXPERT_BLOCK, tm=2048):
    n = x1p.shape[0] // TOK_ROWS
    tm = min(tm, n)
    grid_spec = pltpu.PrefetchScalarGridSpec(
        num_scalar_prefetch=4,
        grid=(n // tm,),
        in_specs=[pl.BlockSpec((tm * TOK_ROWS, LANES), lambda i, a, b, lb, nv: (i, 0))],
        out_specs=pl.BlockSpec(memory_space=pl.ANY),
        scratch_shapes=[pltpu.VMEM((tb * TOK_ROWS, LANES), jnp.uint32),
                        pltpu.SemaphoreType.DMA(()),
                        pltpu.SemaphoreType.DMA(())],
    )
    return pl.pallas_call(
        functools.partial(_dispatch_kernel, tm=tm, tb=tb, nb=nb),
        grid_spec=grid_spec,
        out_shape=jax.ShapeDtypeStruct((nb * tb * TOK_ROWS, LANES), jnp.uint32),
        compiler_params=_cparams("arbitrary"),
        name="dispatch",
    )(dest1, dest2, last_blk, nvalid, x1p)


def _expert_kernel(blk_e_ref, first_ref, next_e_ref, nvalid_ref, xs_ref, wg_hbm, wu_hbm, wd_hbm, y_ref,
                   wgs, wus, wds, wgb, wub, wdb, sem, *, tb):
    rows = tb * TOK_ROWS

    def weight_copies(e, slot):
        return [pltpu.make_async_copy(src.at[e], dst.at[slot], sem.at[slot, j])
                for j, (src, dst) in enumerate(((wg_hbm, wgs), (wu_hbm, wus), (wd_hbm, wds)))]

    @pl.when((pl.program_id(0) == 0) & (nvalid_ref[0] > 0))
    def _():
        for cp in weight_copies(blk_e_ref[0], first_ref[0] - 1):
            cp.start()

    def run(i, sub, nblk):
        @pl.when(first_ref[i] > 0)
        def _():
            slot = first_ref[i] - 1

            @pl.when(next_e_ref[i] >= 0)
            def _():
                for cp in weight_copies(next_e_ref[i], 1 - slot):
                    cp.start()

            for cp in weight_copies(blk_e_ref[i], slot):
                cp.wait()
            wgb[...] = wgs[slot].astype(BF16)
            wub[...] = wus[slot].astype(BF16)
            wdb[...] = wds[slot].astype(BF16)
            mlp(sub, nblk)

        @pl.when(first_ref[i] == 0)
        def _():
            mlp(sub, nblk)

    def mlp(sub, nblk):
        m = nblk * tb
        span = pl.ds(sub * rows, nblk * rows)
        lo, hi = _unpack_bf16_pairs(_load_token_rows(xs_ref.at[span, :], m))
        xb = jnp.concatenate([lo.astype(BF16), hi.astype(BF16)], axis=1)
        hg = jnp.dot(xb, wgb[...], preferred_element_type=F32)
        hu = jnp.dot(xb, wub[...], preferred_element_type=F32)
        hid = (_silu(hg) * hu).astype(BF16)
        _store_token_rows(y_ref.at[span, :], _pack_bf16_pairs(jnp.dot(hid, wdb[...], preferred_element_type=F32)))

    nvalid = nvalid_ref[0]
    for sub in range(0, BLOCKS_PER_STEP, 2):
        i0 = pl.program_id(0) * BLOCKS_PER_STEP + sub
        i1 = i0 + 1
        paired = (i1 < nvalid) & (first_ref[i1] == 0)

        @pl.when(paired)
        def _():
            run(i0, sub, 2)

        @pl.when(jnp.logical_not(paired))
        def _():
            @pl.when(i0 < nvalid)
            def _():
                run(i0, sub, 1)

            @pl.when(i1 < nvalid)
            def _():
                run(i1, sub + 1, 1)

        for s_, i in ((sub, i0), (sub + 1, i1)):
            @pl.when(i >= nvalid)
            def _():
                y_ref[pl.ds(s_ * rows, rows), :] = jnp.zeros((rows, LANES), jnp.uint32)


def _experts(blk_e, first, next_e, nvalid, xs, w_gate, w_up, w_down, *, tb=EXPERT_BLOCK):
    nb = blk_e.shape[0]
    assert nb % BLOCKS_PER_STEP == 0
    step_rows = BLOCKS_PER_STEP * tb * TOK_ROWS
    last_step = lambda nv: jnp.maximum((nv[0] - 1) // BLOCKS_PER_STEP, 0)
    grid_spec = pltpu.PrefetchScalarGridSpec(
        num_scalar_prefetch=4,
        grid=(nb // BLOCKS_PER_STEP,),
        in_specs=[pl.BlockSpec((step_rows, LANES), lambda s, be, fr, ne, nv: (jnp.minimum(s, last_step(nv)), 0)),
                  pl.BlockSpec(memory_space=pl.ANY),
                  pl.BlockSpec(memory_space=pl.ANY),
                  pl.BlockSpec(memory_space=pl.ANY)],
        out_specs=pl.BlockSpec((step_rows, LANES), lambda s, be, fr, ne, nv: (s, 0)),
        scratch_shapes=[pltpu.VMEM((2, D_MODEL, D_EXPERT), F32),
                        pltpu.VMEM((2, D_MODEL, D_EXPERT), F32),
                        pltpu.VMEM((2, D_EXPERT, D_MODEL), F32),
                        pltpu.VMEM((D_MODEL, D_EXPERT), BF16),
                        pltpu.VMEM((D_MODEL, D_EXPERT), BF16),
                        pltpu.VMEM((D_EXPERT, D_MODEL), BF16),
                        pltpu.SemaphoreType.DMA((2, 3))],
    )
    return pl.pallas_call(
        functools.partial(_expert_kernel, tb=tb),
        grid_spec=grid_spec,
        out_shape=jax.ShapeDtypeStruct((nb * tb * TOK_ROWS, LANES), jnp.uint32),
        compiler_params=_cparams("arbitrary"),
        name="experts",
    )(blk_e, first, next_e, nvalid, xs, w_gate, w_up, w_down)


def _combine_kernel(d1_ref, d2_ref, y_hbm, x1_ref, route_ref, g_ref, b_ref, o_ref, ybuf, sem, *, tt):
    i = pl.program_id(0)
    nsteps = pl.num_programs(0)
    slot = i % 2

    def issue(tile, slot_):
        def body(j, carry):
            for u in range(ROW_UNROLL):
                r = j * ROW_UNROLL + u
                for k, d_ref in enumerate((d1_ref, d2_ref)):
                    src = pl.multiple_of(d_ref[tile * tt + r], TOK_ROWS)
                    pltpu.make_async_copy(y_hbm.at[pl.ds(src, TOK_ROWS), :],
                                          ybuf.at[slot_, k, pl.ds(r * TOK_ROWS, TOK_ROWS), :],
                                          sem.at[slot_]).start(priority=k)
            return carry
        lax.fori_loop(0, tt // ROW_UNROLL, body, 0)

    @pl.when(i == 0)
    def _():
        issue(0, 0)

    @pl.when(i + 1 < nsteps)
    def _():
        issue(i + 1, 1 - slot)

    for k in range(TOP_K):
        pltpu.make_async_copy(y_hbm.at[pl.ds(0, tt * TOK_ROWS), :], ybuf.at[slot, k], sem.at[slot]).wait()
    route = route_ref[...]
    g1 = route[:, R_G1:R_G1 + 1]
    g2 = route[:, R_G2:R_G2 + 1]
    lo1, hi1 = _unpack_bf16_pairs(_load_token_rows(ybuf.at[slot, 0], tt))
    lo2, hi2 = _unpack_bf16_pairs(_load_token_rows(ybuf.at[slot, 1], tt))
    x_lo, x_hi = _unpack_bf16_pairs(_load_token_rows(x1_ref, tt))
    h_lo = ALPHA_DN * x_lo + (g1 * lo1 + g2 * lo2)
    h_hi = ALPHA_DN * x_hi + (g1 * hi1 + g2 * hi2)
    mu = (jnp.sum(h_lo, axis=-1, keepdims=True) + jnp.sum(h_hi, axis=-1, keepdims=True)) * (1.0 / D_MODEL)
    c_lo = h_lo - mu
    c_hi = h_hi - mu
    var = (jnp.sum(c_lo * c_lo, axis=-1, keepdims=True) + jnp.sum(c_hi * c_hi, axis=-1, keepdims=True)) * (1.0 / D_MODEL)
    inv = lax.rsqrt(var + LN_EPS)
    o_ref[:, 0:HALF] = c_lo * inv * g_ref[:, 0:HALF] + b_ref[:, 0:HALF]
    o_ref[:, HALF:] = c_hi * inv * g_ref[:, HALF:] + b_ref[:, HALF:]


def _combine(dest1, dest2, ybuf, x1p, route, ln_g, ln_b, *, tt=512):
    n = x1p.shape[0] // TOK_ROWS
    tt = min(tt, n)
    grid_spec = pltpu.PrefetchScalarGridSpec(
        num_scalar_prefetch=2,
        grid=(n // tt,),
        in_specs=[pl.BlockSpec(memory_space=pl.ANY),
                  pl.BlockSpec((tt * TOK_ROWS, LANES), lambda i, a, b: (i, 0)),
                  pl.BlockSpec((tt, ROUTE_W), lambda i, a, b: (i, 0)),
                  pl.BlockSpec((1, D_MODEL), lambda i, a, b: (0, 0)),
                  pl.BlockSpec((1, D_MODEL), lambda i, a, b: (0, 0))],
        out_specs=pl.BlockSpec((tt, D_MODEL), lambda i, a, b: (i, 0)),
        scratch_shapes=[pltpu.VMEM((2, TOP_K, tt * TOK_ROWS, LANES), jnp.uint32),
                        pltpu.SemaphoreType.DMA((2,))],
    )
    return pl.pallas_call(
        functools.partial(_combine_kernel, tt=tt),
        grid_spec=grid_spec,
        out_shape=jax.ShapeDtypeStruct((n, D_MODEL), F32),
        compiler_params=_cparams("arbitrary"),
        name="combine",
    )(dest1, dest2, ybuf, x1p, route, ln_g, ln_b)


def _block_plan(counts_row, n, tb):
    counts = counts_row[:N_EXPERTS, 0].astype(jnp.int32)
    pends = jnp.cumsum((counts + tb - 1) // tb * tb)
    nb = (n * TOP_K) // tb + N_EXPERTS
    nvalid = (pends[-1] // tb).astype(jnp.int32)
    blk = jnp.arange(nb, dtype=jnp.int32)
    blk_e = jnp.sum((pends[None, :] <= (blk * tb)[:, None]).astype(jnp.int32), axis=1)
    blk_e = jnp.minimum(blk_e, N_EXPERTS - 1)
    blk_e = jnp.where(blk < nvalid, blk_e, blk_e[jnp.maximum(nvalid - 1, 0)])
    first = jnp.concatenate([jnp.ones((1,), jnp.int32), (blk_e[1:] != blk_e[:-1]).astype(jnp.int32)])
    slot_e = (jnp.cumsum((counts > 0).astype(jnp.int32)) - 1) % 2
    first = first * (1 + slot_e[blk_e])
    ids = jnp.arange(N_EXPERTS, dtype=jnp.int32)
    later = lax.cummin(jnp.where(counts > 0, ids, N_EXPERTS), axis=0, reverse=True)
    nxt = jnp.concatenate([later[1:], jnp.full((1,), N_EXPERTS, jnp.int32)])
    nxt = jnp.where(nxt < N_EXPERTS, nxt, -1)
    last_blk = jnp.where(counts > 0, pends // tb - 1, -1).astype(jnp.int32)
    return blk_e, first, nxt[blk_e], nvalid.reshape(1), last_blk


def kernel(x, w_in, w_gk_up, b_gk, conv_w, a_log, dt_bias, gla_norm_g, gdn_norm_g, w_out, ln1_g, ln1_b, w_router_group, b_router_group, w_router_expert, b_router_expert, w_gate, w_up, w_down, ln2_g, ln2_b):
    bsz, seq, d = x.shape
    n = bsz * seq
    x2d = x.reshape(n, d)
    w_main, w_aux = _split_w_in(w_in[0])
    main, aux = _in_proj(x2d, w_main, w_aux, conv_w[0].astype(F32), seq=seq)
    wgk = jnp.zeros((AUX_W, 256), F32).at[:GLA_GATE_RANK].set(w_gk_up[0])
    o_a = _gla(main, aux, wgk, b_gk[0][None], gla_norm_g[0][None], bsz=bsz, seq=seq)
    o_b = _gdn(main, aux, _pad_aux_row(a_log[0], AUX_AB), _pad_aux_row(dt_bias[0], AUX_AB),
               gdn_norm_g[0][None], bsz=bsz, seq=seq)
    rpad = jnp.zeros((ROUTE_W - N_EXPERTS - N_GROUPS, d), F32)
    w_router = jnp.concatenate([w_router_expert[0].T, w_router_group[0].T, rpad], axis=0)
    w_router_hi = w_router.astype(BF16)
    w_router = jnp.concatenate([w_router_hi, (w_router - w_router_hi.astype(F32)).astype(BF16)], axis=0)
    b_router = jnp.concatenate([b_router_expert[0], b_router_group[0], rpad[:, 0]])[:, None]
    x1p, route, gates, counts = _post_mix(o_a, o_b, x2d, w_out[0].astype(BF16), ln1_g[0][None], ln1_b[0][None],
                                          w_router, b_router)
    dest = _plan(route, counts, tb=EXPERT_BLOCK)
    blk_e, first, next_e, nvalid, last_blk = _block_plan(counts, n, EXPERT_BLOCK)
    xs = _dispatch(dest[0], dest[1], last_blk, nvalid, x1p, nb=blk_e.shape[0])
    ybuf = _experts(blk_e, first, next_e, nvalid, xs, w_gate[0], w_up[0], w_down[0])
    out = _combine(dest[0], dest[1], ybuf, x1p, gates, ln2_g[0][None], ln2_b[0][None])
    return out.reshape(bsz, seq, d)
```
